```python
import math
import jax, jax.numpy as jnp
from jax import lax
import numpy as np

D_MODEL = 1024
BATCH = 8
SEQ = 8192
DEPTH = 1

HEAD_DIM = 64
DIL_GROUPS = ((128, 1), (512, 4), (2048, 16))
DIL_HEADS_PER_GROUP = 4
N_DIL_HEADS = DIL_HEADS_PER_GROUP * len(DIL_GROUPS)
N_SB_HEADS = 8
DIL_WIDTH = N_DIL_HEADS * HEAD_DIM
DIL_OUT_WIDTH = DIL_HEADS_PER_GROUP * HEAD_DIM
SB_WIDTH = N_SB_HEADS * HEAD_DIM
D_FF = 4 * D_MODEL
BLOCK = 128
RMS_EPS = 1e-6
NEG_INF = -1e30
IN_COLS = 3 * DIL_WIDTH + 3 * SB_WIDTH + 2 * D_MODEL
SPLITS = (DIL_WIDTH, 2 * DIL_WIDTH, 3 * DIL_WIDTH,
          3 * DIL_WIDTH + SB_WIDTH, 3 * DIL_WIDTH + 2 * SB_WIDTH, 3 * DIL_WIDTH + 3 * SB_WIDTH,
          3 * DIL_WIDTH + 3 * SB_WIDTH + D_MODEL)

kernel_name = "hybrid_dilated_stickbreaking_gated_block"


def rmsnorm(x, g):
    xf = x.astype(jnp.float32)
    y = xf * lax.rsqrt(jnp.mean(xf * xf, axis=-1, keepdims=True) + RMS_EPS)
    return (y * g.astype(jnp.float32)).astype(x.dtype)


def alibi_slopes(n):
    return jnp.exp2(-8.0 * jnp.arange(1, n + 1, dtype=jnp.float32) / n)


def dilated_window_group(q, k, v, slopes, window, dilation):
    b, s, h, dh = q.shape
    n_steps = window // dilation
    nb = -(-s // (dilation * BLOCK))
    sub_len = nb * BLOCK
    s_pad = sub_len * dilation

    def to_blocks(t):
        t = jnp.pad(t, ((0, 0), (0, s_pad - s), (0, 0), (0, 0)))
        t = t.reshape(b, sub_len, dilation, h, dh)
        t = t.transpose(0, 2, 3, 1, 4)
        return t.reshape(b, dilation, h, nb, BLOCK, dh)

    qb, kb, vb = to_blocks(q), to_blocks(k), to_blocks(v)

    def with_prev(t):
        prev = jnp.pad(t[:, :, :, :-1], ((0, 0), (0, 0), (0, 0), (1, 0), (0, 0), (0, 0)))
        return jnp.concatenate([prev, t], axis=4)

    kk, vv = with_prev(kb), with_prev(vb)
    scores = jnp.einsum('brhnqd,brhnkd->brhnqk', qb, kk).astype(jnp.float32) / math.sqrt(dh)
    qi = jnp.arange(BLOCK)[:, None]
    kj = jnp.arange(2 * BLOCK)[None, :]
    steps = qi + BLOCK - kj
    key_sub_idx = jnp.arange(nb)[:, None, None] * BLOCK + kj[None] - BLOCK
    valid = (steps >= 0) & (steps <= n_steps) & (key_sub_idx >= 0)
    bias = -slopes[:, None, None].astype(jnp.float32) * (steps * dilation).astype(jnp.float32)
    logits = scores + bias[None, None, :, None]
    logits = jnp.where(valid[None, None, None], logits, NEG_INF)
    lse = jax.nn.logsumexp(logits, axis=-1)
    p = jnp.exp(logits - lse[..., None])
    o = jnp.einsum('brhnqk,brhnkd->brhnqd', p.astype(v.dtype), vv)

    def from_blocks(t):
        extra = t.shape[5:]
        t = t.reshape((b, dilation, h, sub_len) + extra)
        t = jnp.moveaxis(t, 3, 1)
        t = t.reshape((b, s_pad, h) + extra)
        return t[:, :s]

    return from_blocks(o), from_blocks(lse)


def dilated_attention(q, k, v):
    b, s = q.shape[:2]
    slopes = alibi_slopes(N_DIL_HEADS)
    outs, lses = [], []
    for g, (window, dilation) in enumerate(DIL_GROUPS):
        sl = slice(g * DIL_HEADS_PER_GROUP, (g + 1) * DIL_HEADS_PER_GROUP)
        o, l = dilated_window_group(q[:, :, sl], k[:, :, sl], v[:, :, sl], slopes[sl], window, dilation)
        outs.append(o)
        lses.append(l)
    o_all = jnp.stack(outs, axis=0).astype(jnp.float32)
    w = jax.nn.softmax(jnp.stack(lses, axis=0), axis=0)
    out = jnp.sum(w[..., None] * o_all, axis=0).astype(q.dtype)
    return out.reshape(b, s, DIL_OUT_WIDTH)


def stick_breaking_attention(q, k, v):
    b, s, h, dh = q.shape
    nb = s // BLOCK
    scale = 1.0 / math.sqrt(dh)
    kt = k.transpose(0, 2, 1, 3)
    vt = v.transpose(0, 2, 1, 3)
    q_blocks = q.transpose(0, 2, 1, 3).reshape(b, h, nb, BLOCK, dh).transpose(2, 0, 1, 3, 4)
    key_pos = jnp.arange(s)

    def one_block(args):
        i, q_blk = args
        z = jnp.einsum('bhqd,bhkd->bhqk', q_blk, kt).astype(jnp.float32) * scale
        q_pos = i * BLOCK + jnp.arange(BLOCK)
        causal = key_pos[None, :] < q_pos[:, None]
        log_one_minus = jnp.where(causal, jax.nn.log_sigmoid(-z), 0.0)
        suffix = lax.cumsum(log_one_minus, axis=3, reverse=True) - log_one_minus
        log_a = jax.nn.log_sigmoid(z) + suffix
        a = jnp.where(causal, jnp.exp(log_a), 0.0)
        return jnp.einsum('bhqk,bhkd->bhqd', a.astype(vt.dtype), vt)

    o_blocks = lax.map(one_block, (jnp.arange(nb), q_blocks))
    return o_blocks.transpose(1, 0, 3, 2, 4).reshape(b, s, h * dh)


def _fwd_setup_inputs(seed: int = 0) -> dict:
    key = jax.random.key(seed)
    ks = jax.random.split(key, 12)
    f32 = jnp.float32
    x = jax.random.normal(ks[0], (BATCH, SEQ, D_MODEL), f32)
    norm_mix_g = 1.0 + 0.02 * jax.random.normal(ks[1], (DEPTH, D_MODEL), f32)
    w_in = jax.random.normal(ks[2], (DEPTH, D_MODEL, IN_COLS), f32) * D_MODEL ** -0.5
    b_gate = 0.02 * jax.random.normal(ks[3], (DEPTH, 2 * D_MODEL), f32)
    w_up_dil = jax.random.normal(ks[4], (DEPTH, DIL_OUT_WIDTH, D_MODEL), f32) * DIL_OUT_WIDTH ** -0.5
    w_up_sb = jax.random.normal(ks[5], (DEPTH, SB_WIDTH, D_MODEL), f32) * SB_WIDTH ** -0.5
    w_out = jax.random.normal(ks[6], (DEPTH, D_MODEL, D_MODEL), f32) * D_MODEL ** -0.5
    norm_mlp_g = 1.0 + 0.02 * jax.random.normal(ks[7], (DEPTH, D_MODEL), f32)
    w_mlp_in = jax.random.normal(ks[8], (DEPTH, D_MODEL, D_FF), f32) * D_MODEL ** -0.5
    w_mlp_out = jax.random.normal(ks[9], (DEPTH, D_FF, D_MODEL), f32) * D_FF ** -0.5
    norm_final_g = 1.0 + 0.02 * jax.random.normal(ks[10], (D_MODEL,), f32)
    return {"x": x, "norm_mix_g": norm_mix_g, "w_in": w_in, "b_gate": b_gate,
            "w_up_dil": w_up_dil, "w_up_sb": w_up_sb, "w_out": w_out,
            "norm_mlp_g": norm_mlp_g, "w_mlp_in": w_mlp_in, "w_mlp_out": w_mlp_out,
            "norm_final_g": norm_final_g}


def _fwd_reference(x, norm_mix_g, w_in, b_gate, w_up_dil, w_up_sb, w_out,
              norm_mlp_g, w_mlp_in, w_mlp_out, norm_final_g):
    b, s, _ = x.shape
    for layer in range(DEPTH):
        h = rmsnorm(x, norm_mix_g[layer])
        proj = h @ w_in[layer]
        q_a, k_a, v_a, q_b, k_b, v_b, gl_a, gl_b = jnp.split(proj, SPLITS, axis=-1)
        heads_a = lambda t: t.reshape(b, s, N_DIL_HEADS, HEAD_DIM)
        heads_b = lambda t: t.reshape(b, s, N_SB_HEADS, HEAD_DIM)
        o_a = dilated_attention(heads_a(q_a), heads_a(k_a), heads_a(v_a))
        o_b = stick_breaking_attention(heads_b(q_b), heads_b(k_b), heads_b(v_b))
        bg_a, bg_b = jnp.split(b_gate[layer], 2)
        g_a = jax.nn.sigmoid(gl_a + bg_a)
        g_b = jax.nn.sigmoid(gl_b + bg_b)
        merged = g_a * (o_a @ w_up_dil[layer]) + g_b * (o_b @ w_up_sb[layer])
        x = x + merged @ w_out[layer]
        h2 = rmsnorm(x, norm_mlp_g[layer])
        x = x + jnp.square(jax.nn.relu(h2 @ w_mlp_in[layer])) @ w_mlp_out[layer]
    return rmsnorm(x, norm_final_g)


import jax as _jax
import jax.numpy as _jnp

TWIN_FORMAT = 'train_step'
FWD_PARAMS = ['x', 'norm_mix_g', 'w_in', 'b_gate', 'w_up_dil', 'w_up_sb', 'w_out', 'norm_mlp_g', 'w_mlp_in', 'w_mlp_out', 'norm_final_g']
TWIN_WEIGHTS = ['norm_mix_g', 'w_in', 'b_gate', 'w_up_dil', 'w_up_sb', 'w_out', 'norm_mlp_g', 'w_mlp_in', 'w_mlp_out', 'norm_final_g']
TWIN_DIFF_INPUT = 'x'
TWIN_INPUTS = ['x', 'norm_mix_g', 'w_in', 'b_gate', 'w_up_dil', 'w_up_sb', 'w_out', 'norm_mlp_g', 'w_mlp_in', 'w_mlp_out', 'norm_final_g', 'loss_target', 'm_norm_mix_g', 'm_w_in', 'm_b_gate', 'm_w_up_dil', 'm_w_up_sb', 'm_w_out', 'm_norm_mlp_g', 'm_w_mlp_in', 'm_w_mlp_out', 'm_norm_final_g', 'v_norm_mix_g', 'v_w_in', 'v_b_gate', 'v_w_up_dil', 'v_w_up_sb', 'v_w_out', 'v_norm_mlp_g', 'v_w_mlp_in', 'v_w_mlp_out', 'v_norm_final_g']
TWIN_OUTPUTS = ['loss', 'grad_x', 'grad_norm_mix_g', 'grad_w_in', 'grad_b_gate', 'grad_w_up_dil', 'grad_w_up_sb', 'grad_w_out', 'grad_norm_mlp_g', 'grad_w_mlp_in', 'grad_w_mlp_out', 'grad_norm_final_g', 'delta_norm_mix_g', 'delta_w_in', 'delta_b_gate', 'delta_w_up_dil', 'delta_w_up_sb', 'delta_w_out', 'delta_norm_mlp_g', 'delta_w_mlp_in', 'delta_w_mlp_out', 'delta_norm_final_g', 'new_m_norm_mix_g', 'new_m_w_in', 'new_m_b_gate', 'new_m_w_up_dil', 'new_m_w_up_sb', 'new_m_w_out', 'new_m_norm_mlp_g', 'new_m_w_mlp_in', 'new_m_w_mlp_out', 'new_m_norm_final_g', 'new_v_norm_mix_g', 'new_v_w_in', 'new_v_b_gate', 'new_v_w_up_dil', 'new_v_w_up_sb', 'new_v_w_out', 'new_v_norm_mlp_g', 'new_v_w_mlp_in', 'new_v_w_mlp_out', 'new_v_norm_final_g']
TWIN_LEAF_KINDS = {'loss': 'loss', 'grad_x': 'grad_x', 'grad_norm_mix_g': 'grad_w', 'grad_w_in': 'grad_w', 'grad_b_gate': 'grad_w', 'grad_w_up_dil': 'grad_w', 'grad_w_up_sb': 'grad_w', 'grad_w_out': 'grad_w', 'grad_norm_mlp_g': 'grad_w', 'grad_w_mlp_in': 'grad_w', 'grad_w_mlp_out': 'grad_w', 'grad_norm_final_g': 'grad_w', 'delta_norm_mix_g': 'delta_w', 'delta_w_in': 'delta_w', 'delta_b_gate': 'delta_w', 'delta_w_up_dil': 'delta_w', 'delta_w_up_sb': 'delta_w', 'delta_w_out': 'delta_w', 'delta_norm_mlp_g': 'delta_w', 'delta_w_mlp_in': 'delta_w', 'delta_w_mlp_out': 'delta_w', 'delta_norm_final_g': 'delta_w', 'new_m_norm_mix_g': 'new_m', 'new_m_w_in': 'new_m', 'new_m_b_gate': 'new_m', 'new_m_w_up_dil': 'new_m', 'new_m_w_up_sb': 'new_m', 'new_m_w_out': 'new_m', 'new_m_norm_mlp_g': 'new_m', 'new_m_w_mlp_in': 'new_m', 'new_m_w_mlp_out': 'new_m', 'new_m_norm_final_g': 'new_m', 'new_v_norm_mix_g': 'new_v', 'new_v_w_in': 'new_v', 'new_v_b_gate': 'new_v', 'new_v_w_up_dil': 'new_v', 'new_v_w_up_sb': 'new_v', 'new_v_w_out': 'new_v', 'new_v_norm_mlp_g': 'new_v', 'new_v_w_mlp_in': 'new_v', 'new_v_w_mlp_out': 'new_v', 'new_v_norm_final_g': 'new_v'}


def _forward(args):
    return _fwd_reference(*[args[k] for k in FWD_PARAMS])


def _output_shape():
    def fwd():
        inp = _fwd_setup_inputs(0)
        return _fwd_reference(*[inp[k] for k in FWD_PARAMS])
    out = _jax.eval_shape(fwd)
    return out.shape, out.dtype

N_MICROBATCH = 1
ADAM_LR = 0.001
ADAM_B1 = 0.9
ADAM_B2 = 0.999
ADAM_EPS = 1e-08
ADAM_WD = 0.01
ADAM_STEP = 10
PER_EXAMPLE_BATCH_AXIS = {'x': 0, 'loss_target': 0}
SHARED_INPUTS = []
_WEIGHT_DTYPES = {'norm_mix_g': _jnp.float32, 'w_in': _jnp.float32, 'b_gate': _jnp.float32, 'w_up_dil': _jnp.float32, 'w_up_sb': _jnp.float32, 'w_out': _jnp.float32, 'norm_mlp_g': _jnp.float32, 'w_mlp_in': _jnp.float32, 'w_mlp_out': _jnp.float32, 'norm_final_g': _jnp.float32}
MOMENT_SCALE = {'norm_mix_g': 1.384152e-01, 'w_in': 5.516005e-02, 'b_gate': 2.805899e-02, 'w_up_dil': 4.547418e-02, 'w_up_sb': 9.015832e-02, 'w_out': 9.938753e-02, 'norm_mlp_g': 2.222453e-01, 'w_mlp_in': 1.096580e-01, 'w_mlp_out': 2.349207e-01, 'norm_final_g': 6.449809e+01}


def _to_microbatches(a, axis):
    t = _jnp.moveaxis(a, axis, 0)
    t = t.reshape((N_MICROBATCH, t.shape[0] // N_MICROBATCH) + t.shape[1:])
    return _jnp.moveaxis(t, 1, axis + 1)


def setup_inputs(seed: int = 0) -> dict:
    inp = _fwd_setup_inputs(seed)
    key = _jax.random.fold_in(_jax.random.key(seed), 7919)
    shape, _ = _output_shape()
    out = dict(inp)
    out["loss_target"] = _jax.random.normal(_jax.random.fold_in(key, 0), shape, _jnp.float32)
    for i, name in enumerate(TWIN_WEIGHTS):
        w = inp[name].astype(_jnp.float32)
        if MOMENT_SCALE is None:
            s = _jnp.sqrt(_jnp.mean(_jnp.square(w)) + 1e-30)
        else:
            s = MOMENT_SCALE[name]
        km, kv = _jax.random.split(_jax.random.fold_in(key, i + 1))
        out[name] = w
        out["m_" + name] = s * _jax.random.normal(km, w.shape, _jnp.float32)
        out["v_" + name] = (s * s) * _jax.random.uniform(kv, w.shape, _jnp.float32, 0.5, 1.5)
    if N_MICROBATCH > 1:
        for name, axis in PER_EXAMPLE_BATCH_AXIS.items():
            out[name] = _to_microbatches(out[name], axis)
    return {'x': out['x'], 'norm_mix_g': out['norm_mix_g'], 'w_in': out['w_in'], 'b_gate': out['b_gate'], 'w_up_dil': out['w_up_dil'], 'w_up_sb': out['w_up_sb'], 'w_out': out['w_out'], 'norm_mlp_g': out['norm_mlp_g'], 'w_mlp_in': out['w_mlp_in'], 'w_mlp_out': out['w_mlp_out'], 'norm_final_g': out['norm_final_g'], 'loss_target': out['loss_target'], 'm_norm_mix_g': out['m_norm_mix_g'], 'm_w_in': out['m_w_in'], 'm_b_gate': out['m_b_gate'], 'm_w_up_dil': out['m_w_up_dil'], 'm_w_up_sb': out['m_w_up_sb'], 'm_w_out': out['m_w_out'], 'm_norm_mlp_g': out['m_norm_mlp_g'], 'm_w_mlp_in': out['m_w_mlp_in'], 'm_w_mlp_out': out['m_w_mlp_out'], 'm_norm_final_g': out['m_norm_final_g'], 'v_norm_mix_g': out['v_norm_mix_g'], 'v_w_in': out['v_w_in'], 'v_b_gate': out['v_b_gate'], 'v_w_up_dil': out['v_w_up_dil'], 'v_w_up_sb': out['v_w_up_sb'], 'v_w_out': out['v_w_out'], 'v_norm_mlp_g': out['v_norm_mlp_g'], 'v_w_mlp_in': out['v_w_mlp_in'], 'v_w_mlp_out': out['v_w_mlp_out'], 'v_norm_final_g': out['v_norm_final_g']}


def _loss(weights, diff, rest, loss_target):
    with _jax.named_scope("forward"):
        args = {**rest, TWIN_DIFF_INPUT: diff, **{k: w.astype(_WEIGHT_DTYPES[k]) for k, w in weights.items()}}
        y = _forward(args)
    with _jax.named_scope("loss_head"):
        err = _jnp.square(y.astype(_jnp.float32) - loss_target)
        return 0.5 * _jnp.sum(_jnp.mean(err, axis=-1)) if err.ndim else 0.5 * err


def _adamw(w, g, m, v):
    m = ADAM_B1 * m + (1.0 - ADAM_B1) * g
    v = ADAM_B2 * v + (1.0 - ADAM_B2) * _jnp.square(g)
    m_hat = m / (1.0 - ADAM_B1 ** ADAM_STEP)
    v_hat = v / (1.0 - ADAM_B2 ** ADAM_STEP)
    delta = -ADAM_LR * (m_hat / (_jnp.sqrt(v_hat) + ADAM_EPS) + ADAM_WD * w)
    return delta, m, v


def reference(x, norm_mix_g, w_in, b_gate, w_up_dil, w_up_sb, w_out, norm_mlp_g, w_mlp_in, w_mlp_out, norm_final_g, loss_target, m_norm_mix_g, m_w_in, m_b_gate, m_w_up_dil, m_w_up_sb, m_w_out, m_norm_mlp_g, m_w_mlp_in, m_w_mlp_out, m_norm_final_g, v_norm_mix_g, v_w_in, v_b_gate, v_w_up_dil, v_w_up_sb, v_w_out, v_norm_mlp_g, v_w_mlp_in, v_w_mlp_out, v_norm_final_g):
    given = dict(x=x, norm_mix_g=norm_mix_g, w_in=w_in, b_gate=b_gate, w_up_dil=w_up_dil, w_up_sb=w_up_sb, w_out=w_out, norm_mlp_g=norm_mlp_g, w_mlp_in=w_mlp_in, w_mlp_out=w_mlp_out, norm_final_g=norm_final_g, loss_target=loss_target, m_norm_mix_g=m_norm_mix_g, m_w_in=m_w_in, m_b_gate=m_b_gate, m_w_up_dil=m_w_up_dil, m_w_up_sb=m_w_up_sb, m_w_out=m_w_out, m_norm_mlp_g=m_norm_mlp_g, m_w_mlp_in=m_w_mlp_in, m_w_mlp_out=m_w_mlp_out, m_norm_final_g=m_norm_final_g, v_norm_mix_g=v_norm_mix_g, v_w_in=v_w_in, v_b_gate=v_b_gate, v_w_up_dil=v_w_up_dil, v_w_up_sb=v_w_up_sb, v_w_out=v_w_out, v_norm_mlp_g=v_norm_mlp_g, v_w_mlp_in=v_w_mlp_in, v_w_mlp_out=v_w_mlp_out, v_norm_final_g=v_norm_final_g)
    weights = {n: given[n] for n in TWIN_WEIGHTS}
    shared = {n: given[n] for n in SHARED_INPUTS}
    per_example = {n: given[n] for n in ['x']}
    grad_fn = _jax.value_and_grad(_loss, argnums=(0, 1))

    def one_microbatch(ex, loss_target):
        ex = dict(ex)
        diff = ex.pop(TWIN_DIFF_INPUT)
        return grad_fn(weights, diff, {**shared, **ex}, loss_target)

    if N_MICROBATCH == 1:
        loss, (grad_w, grad_x) = one_microbatch(per_example, given["loss_target"])
    else:
        def body(carry, xs):
            loss_sum, grad_sum = carry
            l_k, (gw_k, gx_k) = one_microbatch(xs[0], xs[1])
            with _jax.named_scope("update"):
                return (loss_sum + l_k, _jax.tree.map(_jnp.add, grad_sum, gw_k)), gx_k

        init = (_jnp.zeros((), _jnp.float32), _jax.tree.map(_jnp.zeros_like, weights))
        (loss, grad_w), grad_x = _jax.lax.scan(body, init, (per_example, given["loss_target"]))
    with _jax.named_scope("update"):
        delta_w, new_m, new_v = {}, {}, {}
        for n in TWIN_WEIGHTS:
            delta_w[n], new_m[n], new_v[n] = _adamw(weights[n], grad_w[n], given["m_" + n], given["v_" + n])
    return (loss, grad_x, *[grad_w[n] for n in TWIN_WEIGHTS], *[delta_w[n] for n in TWIN_WEIGHTS],
            *[new_m[n] for n in TWIN_WEIGHTS], *[new_v[n] for n in TWIN_WEIGHTS])
```

```python
import functools
import math

import jax
import jax.numpy as jnp
import numpy as np
from jax import lax
from jax.experimental import pallas as pl
from jax.experimental.pallas import tpu as pltpu

F32 = jnp.float32
BF16 = jnp.bfloat16
MESH = pl.DeviceIdType.MESH

D_MODEL = 1024
HEAD_DIM = 64
DIL_GROUPS = ((128, 1), (512, 4), (2048, 16))
DIL_HEADS = 4
DIL_W = 256
N_DIL_HEADS = 12
SB_HEADS = 8
SB_W = SB_HEADS * HEAD_DIM
QKV_W = 3 * 3 * DIL_W + 3 * SB_W
GATE_W = 2 * D_MODEL
IN_COLS = QKV_W + GATE_W
D_FF = 4 * D_MODEL
BLOCK = 128
RMS_EPS = 1e-6
NEG_INF = -1e30
N_CHIPS = 4
N_DEV = 8

ADAM_LR = 0.001
ADAM_B1 = 0.9
ADAM_B2 = 0.999
ADAM_EPS = 1e-08
ADAM_WD = 0.01
ADAM_STEP = 10

VMEM_LIMIT = 56 * 1024 * 1024

SB_BQ = 256
SB_BK = 256


def _cparams(sem=None):
    if sem is None:
        return pltpu.CompilerParams(vmem_limit_bytes=VMEM_LIMIT)
    return pltpu.CompilerParams(dimension_semantics=sem, vmem_limit_bytes=VMEM_LIMIT)


def _dot(a, b, dims):
    return lax.dot_general(a, b, (dims, ((), ())), preferred_element_type=F32)


def _dot_nn(a, b):
    return _dot(a, b, ((1,), (0,)))


def _dot_nt(a, b):
    return _dot(a, b, ((1,), (1,)))


def _dot_tn(a, b):
    return _dot(a, b, ((0,), (0,)))


def _dot_f32_by_01(x, m01):
    hi = x.astype(BF16)
    r1 = x - hi.astype(F32)
    mid = r1.astype(BF16)
    lo = (r1 - mid.astype(F32)).astype(BF16)
    return _dot_nn(hi, m01) + _dot_nn(mid, m01) + _dot_nn(lo, m01)


def _matmul(a, b, *, mode, out_dtypes, name, tm=512, tn=512, tk=1024, extras=(), epilogue=None):
    if mode == "nn":
        (m, k), (k2, n) = a.shape, b.shape
    elif mode == "nt":
        (m, k), (n, k2) = a.shape, b.shape
    else:
        (k, m), (k2, n) = a.shape, b.shape
    assert k == k2, (a.shape, b.shape, mode)
    tm, tn, tk = min(tm, m), min(tn, n), min(tk, k)
    assert m % tm == 0 and n % tn == 0 and k % tk == 0, (m, n, k, tm, tn, tk)
    nk = k // tk
    n_out = len(out_dtypes)
    n_ex = len(extras)

    if mode == "nn":
        a_spec = pl.BlockSpec((tm, tk), lambda i, j, kk: (i, kk))
        b_spec = pl.BlockSpec((tk, tn), lambda i, j, kk: (kk, j))
        dot = _dot_nn
    elif mode == "nt":
        a_spec = pl.BlockSpec((tm, tk), lambda i, j, kk: (i, kk))
        b_spec = pl.BlockSpec((tn, tk), lambda i, j, kk: (j, kk))
        dot = _dot_nt
    else:
        a_spec = pl.BlockSpec((tk, tm), lambda i, j, kk: (kk, i))
        b_spec = pl.BlockSpec((tk, tn), lambda i, j, kk: (kk, j))
        dot = _dot_tn
    mn_spec = pl.BlockSpec((tm, tn), lambda i, j, kk: (i, j))

    def body(*refs):
        a_ref, b_ref = refs[0], refs[1]
        ex_refs = refs[2:2 + n_ex]
        out_refs = refs[2 + n_ex:2 + n_ex + n_out]
        acc_ref = refs[2 + n_ex + n_out] if nk > 1 else None
        part = dot(a_ref[...].astype(BF16), b_ref[...].astype(BF16))

        def finish(acc):
            if epilogue is None:
                outs = (acc,)
            else:
                outs = epilogue(acc, *[r[...] for r in ex_refs])
            for o_ref, o in zip(out_refs, outs):
                o_ref[...] = o.astype(o_ref.dtype)

        if nk == 1:
            finish(part)
        else:
            kk = pl.program_id(2)

            @pl.when(kk == 0)
            def _():
                acc_ref[...] = part

            @pl.when(kk > 0)
            def _():
                acc_ref[...] += part

            @pl.when(kk == nk - 1)
            def _():
                finish(acc_ref[...])

    outs = pl.pallas_call(
        body,
        name=name,
        grid=(m // tm, n // tn, nk),
        in_specs=[a_spec, b_spec] + [mn_spec] * n_ex,
        out_specs=[mn_spec] * n_out,
        out_shape=[jax.ShapeDtypeStruct((m, n), dt) for dt in out_dtypes],
        scratch_shapes=[pltpu.VMEM((tm, tn), F32)] if nk > 1 else [],
        compiler_params=_cparams(("parallel", "parallel", "arbitrary")),
    )(a, b, *extras)
    return outs


ROW_TILE = 512


def _rms_fwd(x, g, name):
    s, d = x.shape

    def body(x_ref, g_ref, h_ref):
        xv = x_ref[...]
        r = lax.rsqrt(jnp.mean(xv * xv, axis=-1, keepdims=True) + RMS_EPS)
        h_ref[...] = (xv * r * g_ref[...]).astype(BF16)

    return pl.pallas_call(
        body,
        name=name,
        grid=(s // ROW_TILE,),
        in_specs=[pl.BlockSpec((ROW_TILE, d), lambda i: (i, 0)), pl.BlockSpec((1, d), lambda i: (0, 0))],
        out_specs=pl.BlockSpec((ROW_TILE, d), lambda i: (i, 0)),
        out_shape=jax.ShapeDtypeStruct((s, d), BF16),
        compiler_params=_cparams(("parallel",)),
    )(x, g)


def _rms_bwd(dh, x, g, dres, name):
    s, d = x.shape

    def body(dh_ref, x_ref, g_ref, dres_ref, dx_ref, dg_ref):
        i = pl.program_id(0)
        xv = x_ref[...]
        r = lax.rsqrt(jnp.mean(xv * xv, axis=-1, keepdims=True) + RMS_EPS)
        xh = xv * r
        dhv = dh_ref[...]
        dxh = dhv * g_ref[...]
        dx = r * (dxh - xh * jnp.mean(dxh * xh, axis=-1, keepdims=True))
        dx_ref[...] = dres_ref[...] + dx
        part = jnp.sum((dhv * xh).reshape(ROW_TILE // 8, 8, d), axis=0)

        @pl.when(i == 0)
        def _():
            dg_ref[...] = part

        @pl.when(i > 0)
        def _():
            dg_ref[...] += part

    row = pl.BlockSpec((ROW_TILE, d), lambda i: (i, 0))
    return pl.pallas_call(
        body,
        name=name,
        grid=(s // ROW_TILE,),
        in_specs=[row, row, pl.BlockSpec((1, d), lambda i: (0, 0)), row],
        out_specs=[row, pl.BlockSpec((8, d), lambda i: (0, 0))],
        out_shape=[jax.ShapeDtypeStruct((s, d), F32), jax.ShapeDtypeStruct((8, d), F32)],
        compiler_params=_cparams(("arbitrary",)),
    )(dh, x, g, dres)


def _loss_head(x2, g, target):
    s, d = x2.shape

    def body(x_ref, g_ref, t_ref, dx_ref, dg_ref, loss_ref):
        i = pl.program_id(0)
        xv = x_ref[...]
        r = lax.rsqrt(jnp.mean(xv * xv, axis=-1, keepdims=True) + RMS_EPS)
        xh = xv * r
        gv = g_ref[...]
        err = xh * gv - t_ref[...]
        dy = err * (1.0 / d)
        dxh = dy * gv
        dx_ref[...] = r * (dxh - xh * jnp.mean(dxh * xh, axis=-1, keepdims=True))
        part_g = jnp.sum((dy * xh).reshape(ROW_TILE // 8, 8, d), axis=0)
        part_l = (0.5 / d) * jnp.sum((err * err).reshape(ROW_TILE // 8, 8, d), axis=0)

        @pl.when(i == 0)
        def _():
            dg_ref[...] = part_g
            loss_ref[...] = part_l

        @pl.when(i > 0)
        def _():
            dg_ref[...] += part_g
            loss_ref[...] += part_l

    row = pl.BlockSpec((ROW_TILE, d), lambda i: (i, 0))
    acc = pl.BlockSpec((8, d), lambda i: (0, 0))
    return pl.pallas_call(
        body,
        name="loss_head",
        grid=(s // ROW_TILE,),
        in_specs=[row, pl.BlockSpec((1, d), lambda i: (0, 0)), row],
        out_specs=[row, acc, acc],
        out_shape=[jax.ShapeDtypeStruct((s, d), F32), jax.ShapeDtypeStruct((8, d), F32),
                   jax.ShapeDtypeStruct((8, d), F32)],
        compiler_params=_cparams(("arbitrary",)),
    )(x2, g, target)


def _alibi_slopes():
    return np.exp2(np.float32(-8.0) * np.arange(1, N_DIL_HEADS + 1, dtype=np.float32) / np.float32(N_DIL_HEADS))


def _head_lane_mask(h, rows):
    lane = lax.broadcasted_iota(jnp.int32, (rows, DIL_W), 1)
    return (lane >= h * HEAD_DIM) & (lane < (h + 1) * HEAD_DIM)


def _band_terms(group, dil):
    qi = lax.broadcasted_iota(jnp.int32, (BLOCK, BLOCK), 0)
    kj = lax.broadcasted_iota(jnp.int32, (BLOCK, BLOCK), 1)
    steps_prev = (qi + BLOCK - kj).astype(F32) * float(dil)
    steps_cur = (qi - kj).astype(F32) * float(dil)
    return kj >= qi, kj <= qi, steps_prev, steps_cur


def _dil_fwd(qkv, group):
    _, dil = DIL_GROUPS[group]
    s = qkv.shape[0]
    sub = s // dil
    nb = sub // BLOCK
    view = qkv.reshape(sub, dil * QKV_W)
    wblk = QKV_W // DIL_W
    slopes = _alibi_slopes()[group * DIL_HEADS:(group + 1) * DIL_HEADS]

    def col(which):
        return lambda r, n: (n, r * wblk + which * 3 + group)

    def col_prev(which):
        return lambda r, n: (jnp.maximum(n - 1, 0), r * wblk + which * 3 + group)

    def body(q_ref, kc_ref, kp_ref, vc_ref, vp_ref, o_ref, lse_ref):
        n = pl.program_id(1)
        valid_p, valid_c, dist_p, dist_c = _band_terms(group, dil)
        valid_p = valid_p & (n > 0)
        q = q_ref[...]
        kc, kp, vc, vp = kc_ref[...], kp_ref[...], vc_ref[...], vp_ref[...]
        o_acc = jnp.zeros((BLOCK, DIL_W), F32)
        lse_acc = jnp.zeros((BLOCK, DIL_W), F32)
        for h in range(DIL_HEADS):
            hm = _head_lane_mask(h, BLOCK)
            qh = jnp.where(hm, q, jnp.zeros_like(q))
            lp = _dot_nt(qh, kp) * 0.125 - float(slopes[h]) * dist_p
            lc = _dot_nt(qh, kc) * 0.125 - float(slopes[h]) * dist_c
            lp = jnp.where(valid_p, lp, NEG_INF)
            lc = jnp.where(valid_c, lc, NEG_INF)
            mx = jnp.maximum(jnp.max(lp, axis=1, keepdims=True), jnp.max(lc, axis=1, keepdims=True))
            den = jnp.sum(jnp.exp(lp - mx), axis=1, keepdims=True) + jnp.sum(jnp.exp(lc - mx), axis=1, keepdims=True)
            lse = mx + jnp.log(den)
            pp = jnp.exp(lp - lse).astype(BF16)
            pc = jnp.exp(lc - lse).astype(BF16)
            oh = _dot_nn(pp, vp) + _dot_nn(pc, vc)
            o_acc = jnp.where(hm, oh, o_acc)
            lse_acc = jnp.where(hm, lse, lse_acc)
        o_ref[...] = o_acc
        lse_ref[...] = lse_acc

    blk = (BLOCK, DIL_W)
    o, lse = pl.pallas_call(
        body,
        name=f"dil_fwd_g{group}",
        grid=(dil, nb),
        in_specs=[pl.BlockSpec(blk, col(0)), pl.BlockSpec(blk, col(1)), pl.BlockSpec(blk, col_prev(1)),
                  pl.BlockSpec(blk, col(2)), pl.BlockSpec(blk, col_prev(2))],
        out_specs=[pl.BlockSpec(blk, lambda r, n: (n, r))] * 2,
        out_shape=[jax.ShapeDtypeStruct((sub, dil * DIL_W), F32)] * 2,
        compiler_params=_cparams(("parallel", "parallel")),
    )(view, view, view, view, view)
    return o.reshape(s, DIL_W), lse.reshape(s, DIL_W)


def _dil_bwd(qkv, do, lse, cterm, group):
    _, dil = DIL_GROUPS[group]
    s = qkv.shape[0]
    sub = s // dil
    nb = sub // BLOCK
    view = qkv.reshape(sub, dil * QKV_W)
    wblk = QKV_W // DIL_W
    slopes = _alibi_slopes()[group * DIL_HEADS:(group + 1) * DIL_HEADS]
    do_v, lse_v, c_v = (t.reshape(sub, dil * DIL_W) for t in (do, lse, cterm))

    def col(which, shift):
        if shift == 0:
            return lambda r, n: (n, r * wblk + which * 3 + group)
        if shift < 0:
            return lambda r, n: (jnp.maximum(n - 1, 0), r * wblk + which * 3 + group)
        return lambda r, n: (jnp.minimum(n + 1, nb - 1), r * wblk + which * 3 + group)

    def own(shift):
        if shift == 0:
            return lambda r, n: (n, r)
        return lambda r, n: (jnp.minimum(n + 1, nb - 1), r)

    def body(q_ref, qn_ref, kc_ref, kp_ref, vc_ref, vp_ref, do_ref, don_ref, lse_ref, lsen_ref, c_ref, cn_ref,
             dq_ref, dk_ref, dv_ref):
        n = pl.program_id(1)
        valid_p, valid_c, dist_p, dist_c = _band_terms(group, dil)
        has_prev = n > 0
        has_next = n < nb - 1
        q, qn = q_ref[...], qn_ref[...]
        kc, kp, vc, vp = kc_ref[...], kp_ref[...], vc_ref[...], vp_ref[...]
        dov, donv = do_ref[...], don_ref[...]
        lsev, lsenv, cv, cnv = lse_ref[...], lsen_ref[...], c_ref[...], cn_ref[...]
        dq_acc = jnp.zeros((BLOCK, DIL_W), F32)
        dk_acc = jnp.zeros((BLOCK, DIL_W), F32)
        dv_acc = jnp.zeros((BLOCK, DIL_W), F32)

        def head_col(t, hm):
            return jnp.max(jnp.where(hm, t, NEG_INF), axis=1, keepdims=True)

        for h in range(DIL_HEADS):
            hm = _head_lane_mask(h, BLOCK)
            slope = float(slopes[h])

            def pair(qh, k, v, doh, lse_h, c_h, valid, dist):
                logit = _dot_nt(qh, k) * 0.125 - slope * dist
                p = jnp.where(valid, jnp.exp(logit - lse_h), 0.0)
                dp = _dot_nt(doh, v)
                dlog = (p * (dp + c_h) * 0.125).astype(BF16)
                return p.astype(BF16), dlog

            qh = jnp.where(hm, q, jnp.zeros_like(q))
            doh = jnp.where(hm, dov, 0.0).astype(BF16)
            lse_h, c_h = head_col(lsev, hm), head_col(cv, hm)
            _, dlog_p = pair(qh, kp, vp, doh, lse_h, c_h, valid_p & has_prev, dist_p)
            p_c, dlog_c = pair(qh, kc, vc, doh, lse_h, c_h, valid_c, dist_c)
            dq_h = _dot_nn(dlog_p, kp) + _dot_nn(dlog_c, kc)
            dq_acc = jnp.where(hm, dq_h, dq_acc)
            qnh = jnp.where(hm, qn, jnp.zeros_like(qn))
            donh = jnp.where(hm, donv, 0.0).astype(BF16)
            p_n, dlog_n = pair(qnh, kc, vc, donh, head_col(lsenv, hm), head_col(cnv, hm), valid_p & has_next, dist_p)
            dk_acc += _dot_tn(dlog_c, qh) + _dot_tn(dlog_n, qnh)
            dv_acc += _dot_tn(p_c, doh) + _dot_tn(p_n, donh)
        dq_ref[...] = dq_acc
        dk_ref[...] = dk_acc
        dv_ref[...] = dv_acc

    blk = (BLOCK, DIL_W)
    outs = pl.pallas_call(
        body,
        name=f"dil_bwd_g{group}",
        grid=(dil, nb),
        in_specs=[pl.BlockSpec(blk, col(0, 0)), pl.BlockSpec(blk, col(0, 1)),
                  pl.BlockSpec(blk, col(1, 0)), pl.BlockSpec(blk, col(1, -1)),
                  pl.BlockSpec(blk, col(2, 0)), pl.BlockSpec(blk, col(2, -1)),
                  pl.BlockSpec(blk, own(0)), pl.BlockSpec(blk, own(1)),
                  pl.BlockSpec(blk, own(0)), pl.BlockSpec(blk, own(1)),
                  pl.BlockSpec(blk, own(0)), pl.BlockSpec(blk, own(1))],
        out_specs=[pl.BlockSpec(blk, lambda r, n: (n, r))] * 3,
        out_shape=[jax.ShapeDtypeStruct((sub, dil * DIL_W), F32)] * 3,
        compiler_params=_cparams(("parallel", "parallel")),
    )(view, view, view, view, view, view, do_v, do_v, lse_v, lse_v, c_v, c_v)
    return tuple(t.reshape(s, DIL_W) for t in outs)


def _softplus_parts(z):
    e = jnp.exp(-jnp.abs(z))
    return jnp.maximum(z, 0.0) + jnp.log(1.0 + e)


def _sb_fwd(q, k, v):
    nh, s, dh = q.shape
    nq, nkb = s // SB_BQ, s // SB_BK
    scale = 1.0 / math.sqrt(dh)

    def body(q_ref, k_ref, v_ref, o_ref, carry_ref):
        i = pl.program_id(1)
        qb = q_ref[0]
        row = lax.broadcasted_iota(jnp.int32, (SB_BQ, SB_BK), 0)
        colk = lax.broadcasted_iota(jnp.int32, (SB_BQ, SB_BK), 1)
        causal = colk < row
        rr = lax.broadcasted_iota(jnp.int32, (SB_BK, SB_BK), 0)
        cc = lax.broadcasted_iota(jnp.int32, (SB_BK, SB_BK), 1)
        later = (rr > cc).astype(BF16)
        blk_lane = lax.broadcasted_iota(jnp.int32, (SB_BQ, nkb), 1)

        def block(j, acc, cl, ctile, masked):
            start = pl.multiple_of(j * SB_BK, SB_BK)
            kb = k_ref[0, pl.ds(start, SB_BK), :]
            vb = v_ref[0, pl.ds(start, SB_BK), :]
            z = _dot_nt(qb, kb) * scale
            l = -_softplus_parts(z)
            if masked:
                l = jnp.where(causal, l, 0.0)
            suffix = _dot_f32_by_01(l, later) + cl
            a = jnp.exp(z + l + suffix)
            if masked:
                a = jnp.where(causal, a, 0.0)
            acc = acc + _dot_nn(a.astype(BF16), vb)
            ctile = jnp.where(blk_lane == j, cl, ctile)
            cl = cl + jnp.sum(l, axis=1, keepdims=True)
            return acc, cl, ctile

        acc0 = jnp.zeros((SB_BQ, dh), F32)
        cl0 = jnp.zeros((SB_BQ, 1), F32)
        ct0 = jnp.zeros((SB_BQ, nkb), F32)
        carry = block(i, acc0, cl0, ct0, True)
        acc, _, ctile = lax.fori_loop(0, i, lambda t, c: block(i - 1 - t, *c, False), carry)
        o_ref[0] = acc
        carry_ref[0] = ctile

    full = pl.BlockSpec((1, s, dh), lambda h, i: (h, 0, 0))
    return pl.pallas_call(
        body,
        name="sb_fwd",
        grid=(nh, nq),
        in_specs=[pl.BlockSpec((1, SB_BQ, dh), lambda h, i: (h, i, 0)), full, full],
        out_specs=[pl.BlockSpec((1, SB_BQ, dh), lambda h, i: (h, i, 0)),
                   pl.BlockSpec((1, SB_BQ, nkb), lambda h, i: (h, i, 0))],
        out_shape=[jax.ShapeDtypeStruct((nh, s, dh), F32), jax.ShapeDtypeStruct((nh, s, nkb), F32)],
        compiler_params=_cparams(("parallel", "parallel")),
    )(q, k, v)


def _sb_bwd(q, k, v, do, carries):
    nh, s, dh = q.shape
    nq, nkb = s // SB_BQ, s // SB_BK
    scale = 1.0 / math.sqrt(dh)

    def body(q_ref, k_ref, v_ref, do_ref, carry_ref, dq_ref, dk_ref, dv_ref):
        i = pl.program_id(1)

        @pl.when(i == 0)
        def _():
            dk_ref[...] = jnp.zeros_like(dk_ref)
            dv_ref[...] = jnp.zeros_like(dv_ref)

        qb = q_ref[0]
        dob = do_ref[0].astype(BF16)
        ctile = carry_ref[0]
        row = lax.broadcasted_iota(jnp.int32, (SB_BQ, SB_BK), 0)
        colk = lax.broadcasted_iota(jnp.int32, (SB_BQ, SB_BK), 1)
        causal = colk < row
        rr = lax.broadcasted_iota(jnp.int32, (SB_BK, SB_BK), 0)
        cc = lax.broadcasted_iota(jnp.int32, (SB_BK, SB_BK), 1)
        later = (rr > cc).astype(BF16)
        earlier = (rr < cc).astype(BF16)
        blk_lane = lax.broadcasted_iota(jnp.int32, (SB_BQ, nkb), 1)

        def block(j, dq, cg, masked):
            start = pl.multiple_of(j * SB_BK, SB_BK)
            kb = k_ref[0, pl.ds(start, SB_BK), :]
            vb = v_ref[0, pl.ds(start, SB_BK), :]
            cl = jnp.sum(jnp.where(blk_lane == j, ctile, 0.0), axis=1, keepdims=True)
            z = _dot_nt(qb, kb) * scale
            l = -_softplus_parts(z)
            sig = jnp.exp(z + l)
            if masked:
                l = jnp.where(causal, l, 0.0)
            suffix = _dot_f32_by_01(l, later) + cl
            a = jnp.exp(z + l + suffix)
            if masked:
                a = jnp.where(causal, a, 0.0)
            g = a * _dot_nt(dob, vb)
            gpre = _dot_f32_by_01(g, earlier) + cg
            dz = g - (g + gpre) * sig
            if masked:
                dz = jnp.where(causal, dz, 0.0)
            dz16 = (dz * scale).astype(BF16)
            dq = dq + _dot_nn(dz16, kb)
            dk_ref[0, pl.ds(start, SB_BK), :] += _dot_tn(dz16, qb)
            dv_ref[0, pl.ds(start, SB_BK), :] += _dot_tn(a.astype(BF16), dob)
            cg = cg + jnp.sum(g, axis=1, keepdims=True)
            return dq, cg

        init = (jnp.zeros((SB_BQ, dh), F32), jnp.zeros((SB_BQ, 1), F32))
        dq, cg = lax.fori_loop(0, i, lambda j, c: block(j, *c, False), init)
        dq, _ = block(i, dq, cg, True)
        dq_ref[0] = dq

    full = pl.BlockSpec((1, s, dh), lambda h, i: (h, 0, 0))
    qblk = pl.BlockSpec((1, SB_BQ, dh), lambda h, i: (h, i, 0))
    return pl.pallas_call(
        body,
        name="sb_bwd",
        grid=(nh, nq),
        in_specs=[qblk, full, full, qblk, pl.BlockSpec((1, SB_BQ, nkb), lambda h, i: (h, i, 0))],
        out_specs=[qblk, full, full],
        out_shape=[jax.ShapeDtypeStruct((nh, s, dh), F32)] * 3,
        compiler_params=_cparams(("parallel", "arbitrary")),
    )(q, k, v, do, carries)


MERGE_TILE = 256


def _group_mix(lses):
    mx = jnp.maximum(jnp.maximum(lses[0], lses[1]), lses[2])
    es = [jnp.exp(t - mx) for t in lses]
    den = es[0] + es[1] + es[2]
    return [e / den for e in es]


def _merge_fwd(o_groups, lse_groups, o_sb, gl, b_gate, w_up_dil, w_up_sb):
    s = gl.shape[0]
    t = MERGE_TILE

    def body(o0, o1, o2, l0, l1, l2, ob_ref, gl_ref, bg_ref, wd_ref, ws_ref, merged_ref, oa_ref):
        w = _group_mix([l0[...], l1[...], l2[...]])
        oa = (w[0] * o0[...] + w[1] * o1[...] + w[2] * o2[...]).astype(BF16)
        ua = _dot_nn(oa, wd_ref[...])
        ub = _dot_nn(ob_ref[...].astype(BF16), ws_ref[...])
        gate = jax.nn.sigmoid(gl_ref[...] + bg_ref[...])
        merged_ref[...] = (gate[:, :D_MODEL] * ua + gate[:, D_MODEL:] * ub).astype(BF16)
        oa_ref[...] = oa

    dil = pl.BlockSpec((t, DIL_W), lambda i: (i, 0))
    const = lambda shape: pl.BlockSpec(shape, lambda i: (0, 0))
    return pl.pallas_call(
        body,
        name="merge_fwd",
        grid=(s // t,),
        in_specs=[dil] * 6 + [pl.BlockSpec((t, SB_W), lambda i: (i, 0)), pl.BlockSpec((t, GATE_W), lambda i: (i, 0)),
                              const((1, GATE_W)), const((DIL_W, D_MODEL)), const((SB_W, D_MODEL))],
        out_specs=[pl.BlockSpec((t, D_MODEL), lambda i: (i, 0)), dil],
        out_shape=[jax.ShapeDtypeStruct((s, D_MODEL), BF16), jax.ShapeDtypeStruct((s, DIL_W), BF16)],
        compiler_params=_cparams(("parallel",)),
    )(*o_groups, *lse_groups, o_sb, gl, b_gate, w_up_dil, w_up_sb)


def _merge_bwd(dmerged, o_groups, lse_groups, o_sb, gl, b_gate, w_up_dil, w_up_sb):
    s = gl.shape[0]
    t = MERGE_TILE

    def body(dm_ref, o0, o1, o2, l0, l1, l2, ob_ref, gl_ref, bg_ref, wd_ref, ws_ref,
             dua_ref, dub_ref, dgl_ref, dbg_ref, dosb_ref, d0, d1, d2, c0, c1, c2):
        i = pl.program_id(0)
        og = [o0[...], o1[...], o2[...]]
        w = _group_mix([l0[...], l1[...], l2[...]])
        oa = (w[0] * og[0] + w[1] * og[1] + w[2] * og[2]).astype(BF16)
        ua = _dot_nn(oa, wd_ref[...])
        ub = _dot_nn(ob_ref[...].astype(BF16), ws_ref[...])
        gate = jax.nn.sigmoid(gl_ref[...] + bg_ref[...])
        ga, gb = gate[:, :D_MODEL], gate[:, D_MODEL:]
        dm = dm_ref[...]
        dua = (dm * ga).astype(BF16)
        dub = (dm * gb).astype(BF16)
        dua_ref[...] = dua
        dub_ref[...] = dub
        dgl_a = dm * ua * ga * (1.0 - ga)
        dgl_b = dm * ub * gb * (1.0 - gb)
        dgl_ref[:, :D_MODEL] = dgl_a.astype(BF16)
        dgl_ref[:, D_MODEL:] = dgl_b.astype(BF16)
        part = jnp.concatenate([jnp.sum(dgl_a.reshape(t // 8, 8, D_MODEL), axis=0),
                                jnp.sum(dgl_b.reshape(t // 8, 8, D_MODEL), axis=0)], axis=1)

        @pl.when(i == 0)
        def _():
            dbg_ref[...] = part

        @pl.when(i > 0)
        def _():
            dbg_ref[...] += part

        dosb_ref[...] = _dot_nt(dub, ws_ref[...])
        doa = _dot_nt(dua, wd_ref[...])
        rr = lax.broadcasted_iota(jnp.int32, (DIL_W, DIL_W), 0) // HEAD_DIM
        cc = lax.broadcasted_iota(jnp.int32, (DIL_W, DIL_W), 1) // HEAD_DIM
        same_head = (rr == cc).astype(BF16)
        dw = [_dot_f32_by_01(doa * og[g], same_head) for g in range(3)]
        mean_dw = w[0] * dw[0] + w[1] * dw[1] + w[2] * dw[2]
        for g, (d_ref, c_ref) in enumerate(((d0, c0), (d1, c1), (d2, c2))):
            d_ref[...] = w[g] * doa
            c_ref[...] = -w[g] * mean_dw

    dil = pl.BlockSpec((t, DIL_W), lambda i: (i, 0))
    wide = pl.BlockSpec((t, D_MODEL), lambda i: (i, 0))
    gate2 = pl.BlockSpec((t, GATE_W), lambda i: (i, 0))
    sbw = pl.BlockSpec((t, SB_W), lambda i: (i, 0))
    const = lambda shape: pl.BlockSpec(shape, lambda i: (0, 0))
    return pl.pallas_call(
        body,
        name="merge_bwd",
        grid=(s // t,),
        in_specs=[wide] + [dil] * 6 + [sbw, gate2, const((1, GATE_W)), const((DIL_W, D_MODEL)), const((SB_W, D_MODEL))],
        out_specs=[wide, wide, gate2, const((8, GATE_W)), sbw] + [dil] * 6,
        out_shape=[jax.ShapeDtypeStruct((s, D_MODEL), BF16), jax.ShapeDtypeStruct((s, D_MODEL), BF16),
                   jax.ShapeDtypeStruct((s, GATE_W), BF16), jax.ShapeDtypeStruct((8, GATE_W), F32),
                   jax.ShapeDtypeStruct((s, SB_W), F32)] + [jax.ShapeDtypeStruct((s, DIL_W), F32)] * 6,
        compiler_params=_cparams(("arbitrary",)),
    )(dmerged, *o_groups, *lse_groups, o_sb, gl, b_gate, w_up_dil, w_up_sb)


ANY = pl.BlockSpec(memory_space=pl.ANY)


def _place():
    x, y, c = lax.axis_index("x"), lax.axis_index("y"), lax.axis_index("c")
    other_chips = [(1 - x, y), (x, 1 - y), (1 - x, 1 - y)]
    return x, y, c, other_chips


def _all_gather_weights(pack):
    r, wd = pack.shape
    rh = r // 2

    def body(p_ref, out_ref, send_sems, recv_sems, local_sem):
        x, y, c, chips = _place()
        me, sibling = 2 * x + y, (x, y, 1 - c)

        def half(chip_idx, core):
            return out_ref.at[chip_idx, pl.ds(core * rh, rh), :]

        def copy(k, chip_idx, core, to, src=None):
            return pltpu.make_async_remote_copy(
                src_ref=half(chip_idx, core) if src is None else src, dst_ref=half(chip_idx, core),
                send_sem=send_sems.at[k], recv_sem=recv_sems.at[k], device_id=to, device_id_type=MESH)

        mine = pltpu.make_async_copy(p_ref, out_ref.at[me], local_sem)
        mine.start()
        first = [copy(j, me, c, (*chip, c), src=p_ref.at[pl.ds(c * rh, rh), :]) for j, chip in enumerate(chips)]
        for cp in first:
            cp.start()
        passed = [copy(3 + j, 2 * chip[0] + chip[1], c, sibling) for j, chip in enumerate(chips)]
        for j, chip in enumerate(chips):
            copy(j, 2 * chip[0] + chip[1], c, (x, y, c)).wait_recv()
            passed[j].start()
        for j, chip in enumerate(chips):
            copy(3 + j, 2 * chip[0] + chip[1], 1 - c, (x, y, c)).wait_recv()
        for cp in first + passed:
            cp.wait_send()
        mine.wait()

    return pl.pallas_call(
        body,
        name="all_gather_weights",
        in_specs=[ANY],
        out_specs=ANY,
        out_shape=jax.ShapeDtypeStruct((N_CHIPS, r, wd), pack.dtype),
        scratch_shapes=[pltpu.SemaphoreType.DMA((6,)), pltpu.SemaphoreType.DMA((6,)), pltpu.SemaphoreType.DMA],
    )(pack)


def _swap_halves(g):
    n, r, wd = g.shape
    rh = r // 2

    def body(g_ref, out_ref, send_sem, recv_sem):
        x, y, c, _ = _place()
        cp = pltpu.make_async_remote_copy(
            src_ref=g_ref.at[:, pl.ds((1 - c) * rh, rh), :], dst_ref=out_ref,
            send_sem=send_sem, recv_sem=recv_sem, device_id=(x, y, 1 - c), device_id_type=MESH)
        cp.start()
        cp.wait()

    return pl.pallas_call(
        body,
        name="grad_swap_halves",
        in_specs=[ANY],
        out_specs=ANY,
        out_shape=jax.ShapeDtypeStruct((n, rh, wd), g.dtype),
        scratch_shapes=[pltpu.SemaphoreType.DMA, pltpu.SemaphoreType.DMA],
    )(g)


def _add_halves(g, got, core):
    n, r, wd = g.shape
    rh = r // 2
    t = rh // 4
    nt = rh // t

    def body(c_ref, a_ref, b_ref, o_ref):
        o_ref[...] = a_ref[...] + b_ref[...]

    grid_spec = pltpu.PrefetchScalarGridSpec(
        num_scalar_prefetch=1,
        grid=(n, nt),
        in_specs=[pl.BlockSpec((1, t, wd), lambda s, i, c: (s, c[0] * nt + i, 0)),
                  pl.BlockSpec((1, t, wd), lambda s, i, c: (s, i, 0))],
        out_specs=pl.BlockSpec((1, t, wd), lambda s, i, c: (s, i, 0)),
    )
    return pl.pallas_call(
        body,
        name="grad_add_halves",
        grid_spec=grid_spec,
        out_shape=jax.ShapeDtypeStruct((n, rh, wd), F32),
        compiler_params=_cparams(("parallel", "parallel")),
    )(core, g, got)


def _exchange_chunks(h):
    n, rh, wd = h.shape

    def body(h_ref, out_ref, send_sems, recv_sems, local_sem):
        x, y, c, chips = _place()
        me = 2 * x + y
        mine = pltpu.make_async_copy(h_ref.at[me], out_ref.at[me], local_sem)
        mine.start()
        sends = []
        for j, chip in enumerate(chips):
            them = 2 * chip[0] + chip[1]
            sends.append(pltpu.make_async_remote_copy(
                src_ref=h_ref.at[them], dst_ref=out_ref.at[me],
                send_sem=send_sems.at[j], recv_sem=recv_sems.at[j], device_id=(*chip, c), device_id_type=MESH))
        for cp in sends:
            cp.start()
        for j, chip in enumerate(chips):
            them = 2 * chip[0] + chip[1]
            pltpu.make_async_remote_copy(
                src_ref=h_ref.at[them], dst_ref=out_ref.at[them],
                send_sem=send_sems.at[j], recv_sem=recv_sems.at[j], device_id=(*chip, c), device_id_type=MESH).wait_recv()
        for cp in sends:
            cp.wait_send()
        mine.wait()

    return pl.pallas_call(
        body,
        name="grad_exchange_chunks",
        in_specs=[ANY],
        out_specs=ANY,
        out_shape=jax.ShapeDtypeStruct((n, rh, wd), h.dtype),
        scratch_shapes=[pltpu.SemaphoreType.DMA((3,)), pltpu.SemaphoreType.DMA((3,)), pltpu.SemaphoreType.DMA],
    )(h)


def _sum_chips(b):
    n, rh, wd = b.shape
    t = rh // 4

    def body(b_ref, o_ref):
        o_ref[...] = ((b_ref[0] + b_ref[1]) + b_ref[2]) + b_ref[3]

    return pl.pallas_call(
        body,
        name="grad_sum_chips",
        grid=(rh // t,),
        in_specs=[pl.BlockSpec((n, t, wd), lambda i: (0, i, 0))],
        out_specs=pl.BlockSpec((t, wd), lambda i: (i, 0)),
        out_shape=jax.ShapeDtypeStruct((rh, wd), F32),
        compiler_params=_cparams(("parallel",)),
    )(b)


def _join_halves(tc):
    rh, wd = tc.shape

    def body(t_ref, out_ref, send_sem, recv_sem, local_sem):
        x, y, c, _ = _place()
        mine = pltpu.make_async_copy(t_ref, out_ref.at[pl.ds(c * rh, rh), :], local_sem)
        mine.start()
        cp = pltpu.make_async_remote_copy(
            src_ref=t_ref, dst_ref=out_ref.at[pl.ds(c * rh, rh), :],
            send_sem=send_sem, recv_sem=recv_sem, device_id=(x, y, 1 - c), device_id_type=MESH)
        cp.start()
        cp.wait()
        mine.wait()

    return pl.pallas_call(
        body,
        name="grad_join_halves",
        in_specs=[ANY],
        out_specs=ANY,
        out_shape=jax.ShapeDtypeStruct((2 * rh, wd), tc.dtype),
        scratch_shapes=[pltpu.SemaphoreType.DMA, pltpu.SemaphoreType.DMA, pltpu.SemaphoreType.DMA],
    )(tc)


def _all_reduce_small(pack):
    rows, lanes = pack.shape

    def body(p_ref, out_ref, buf, send_sems, recv_sems):
        x, y, c, _ = _place()
        me = 4 * x + 2 * y + c
        buf[me] = p_ref[...]
        sends = []
        for k in range(1, N_DEV):
            peer = (x ^ (k >> 2), y ^ ((k >> 1) & 1), c ^ (k & 1))
            sends.append(pltpu.make_async_remote_copy(
                src_ref=p_ref, dst_ref=buf.at[me], send_sem=send_sems.at[k - 1], recv_sem=recv_sems.at[k - 1],
                device_id=peer, device_id_type=MESH))
        for cp in sends:
            cp.start()
        for k in range(1, N_DEV):
            pltpu.make_async_remote_copy(
                src_ref=p_ref, dst_ref=buf.at[me ^ k], send_sem=send_sems.at[k - 1], recv_sem=recv_sems.at[k - 1],
                device_id=(x, y, c), device_id_type=MESH).wait_recv()
        for cp in sends:
            cp.wait_send()
        total = buf[0]
        for d in range(1, N_DEV):
            total = total + buf[d]
        out_ref[...] = total

    vm = pl.BlockSpec(memory_space=pltpu.VMEM)
    return pl.pallas_call(
        body,
        name="all_reduce_small",
        in_specs=[vm],
        out_specs=vm,
        out_shape=jax.ShapeDtypeStruct((rows, lanes), F32),
        scratch_shapes=[pltpu.VMEM((N_DEV, rows, lanes), F32), pltpu.SemaphoreType.DMA((N_DEV - 1,)),
                        pltpu.SemaphoreType.DMA((N_DEV - 1,))],
    )(pack)


def _adamw(g, w, m, v, name):
    rows, cols = g.shape
    t = rows
    for cand in (256, 128, 64, 32, 16, 8):
        if rows % cand == 0:
            t = cand
            break

    def body(g_ref, w_ref, m_ref, v_ref, d_ref, nm_ref, nv_ref):
        gv = g_ref[...]
        mv = ADAM_B1 * m_ref[...] + (1.0 - ADAM_B1) * gv
        vv = ADAM_B2 * v_ref[...] + (1.0 - ADAM_B2) * (gv * gv)
        m_hat = mv / (1.0 - ADAM_B1 ** ADAM_STEP)
        v_hat = vv / (1.0 - ADAM_B2 ** ADAM_STEP)
        d_ref[...] = -ADAM_LR * (m_hat / (jnp.sqrt(v_hat) + ADAM_EPS) + ADAM_WD * w_ref[...])
        nm_ref[...] = mv
        nv_ref[...] = vv

    blk = pl.BlockSpec((t, cols), lambda i: (i, 0))
    return pl.pallas_call(
        body,
        name=name,
        grid=(rows // t,),
        in_specs=[blk] * 4,
        out_specs=[blk] * 3,
        out_shape=[jax.ShapeDtypeStruct((rows, cols), F32)] * 3,
        compiler_params=_cparams(("parallel",)),
    )(g, w, m, v)


PACK_W = 1024
BIG = (("w_in", (D_MODEL, IN_COLS), 1), ("w_up_dil", (DIL_W, D_MODEL), 1), ("w_up_sb", (SB_W, D_MODEL), 1),
       ("w_out", (D_MODEL, D_MODEL), 0), ("w_mlp_in", (D_MODEL, D_FF), 1), ("w_mlp_out", (D_FF, D_MODEL), 0))


def _shard_shape(shape, axis):
    return tuple(d // N_CHIPS if a == axis else d for a, d in enumerate(shape))


def _pack_rows():
    rows, at = {}, 0
    for name, shape, axis in BIG:
        n = math.prod(_shard_shape(shape, axis)) // PACK_W
        rows[name] = (at, n)
        at += n
    return rows, at


def _pack_shards(shards):
    return jnp.concatenate([shards[name].reshape(-1, PACK_W) for name, _, _ in BIG], axis=0)


def _unpack_full(gathered):
    rows, _ = _pack_rows()
    full = {}
    for name, shape, axis in BIG:
        at, n = rows[name]
        parts = gathered[:, at:at + n, :].reshape((N_CHIPS,) + _shard_shape(shape, axis))
        if axis == 0:
            full[name] = parts.reshape(shape)
        else:
            full[name] = jnp.transpose(parts, (1, 0, 2)).reshape(shape)
    return full


def _pack_full_grads(grads):
    chunks = []
    for name, shape, axis in BIG:
        g = grads[name]
        if axis == 0:
            parts = g.reshape((N_CHIPS, shape[0] // N_CHIPS, shape[1]))
        else:
            parts = jnp.transpose(g.reshape((shape[0], N_CHIPS, shape[1] // N_CHIPS)), (1, 0, 2))
        chunks.append(parts.reshape(N_CHIPS, -1, PACK_W))
    return jnp.concatenate(chunks, axis=1)


def _unpack_shard(packed):
    rows, _ = _pack_rows()
    return {name: packed[rows[name][0]:rows[name][0] + rows[name][1]].reshape(_shard_shape(shape, axis))
            for name, shape, axis in BIG}


def _heads_major(t):
    s = t.shape[0]
    return jnp.transpose(t.reshape(s, SB_HEADS, HEAD_DIM), (1, 0, 2))


def _heads_minor(t):
    nh, s, dh = t.shape
    return jnp.transpose(t, (1, 0, 2)).reshape(s, nh * dh)


def _local_step(x, target, w, norm_mix_g, b_gate, norm_mlp_g, norm_final_g):
    w_qkv, w_gate = w["w_in"][:, :QKV_W], w["w_in"][:, QKV_W:]

    h = _rms_fwd(x, norm_mix_g, "norm_mix")
    (qkv,) = _matmul(h, w_qkv, mode="nn", out_dtypes=(BF16,), name="proj_qkv", tn=768)
    (gl,) = _matmul(h, w_gate, mode="nn", out_dtypes=(F32,), name="proj_gate")
    dil = [_dil_fwd(qkv, g) for g in range(3)]
    o_groups, lse_groups = [d[0] for d in dil], [d[1] for d in dil]
    sb0 = 9 * DIL_W
    q_sb, k_sb, v_sb = (_heads_major(qkv[:, sb0 + i * SB_W:sb0 + (i + 1) * SB_W]) for i in range(3))
    o_sb_h, carries = _sb_fwd(q_sb, k_sb, v_sb)
    o_sb = _heads_minor(o_sb_h)
    merged, o_a = _merge_fwd(o_groups, lse_groups, o_sb, gl, b_gate, w["w_up_dil"], w["w_up_sb"])
    (x1,) = _matmul(merged, w["w_out"], mode="nn", out_dtypes=(F32,), name="out_proj",
                    extras=(x,), epilogue=lambda acc, res: (res + acc,))
    h2 = _rms_fwd(x1, norm_mlp_g, "norm_mlp")
    u, act = _matmul(h2, w["w_mlp_in"], mode="nn", out_dtypes=(F32, BF16), name="mlp_in",
                     epilogue=lambda acc: (acc, jnp.square(jnp.maximum(acc, 0.0))))
    (x2,) = _matmul(act, w["w_mlp_out"], mode="nn", out_dtypes=(F32,), name="mlp_out", tk=2048,
                    extras=(x1,), epilogue=lambda acc, res: (res + acc,))
    dx2, dg_final, loss_part = _loss_head(x2, norm_final_g.reshape(1, D_MODEL), target)

    (du,) = _matmul(dx2, w["w_mlp_out"], mode="nt", out_dtypes=(BF16,), name="mlp_out_dx",
                    extras=(u,), epilogue=lambda acc, uu: (acc * (2.0 * jnp.maximum(uu, 0.0)),))
    (g_mlp_out,) = _matmul(act, dx2, mode="tn", out_dtypes=(F32,), name="mlp_out_dw")
    (g_mlp_in,) = _matmul(h2, du, mode="tn", out_dtypes=(F32,), name="mlp_in_dw")
    (dh2,) = _matmul(du, w["w_mlp_in"], mode="nt", out_dtypes=(F32,), name="mlp_in_dx", tk=2048)
    dx1, dg_mlp = _rms_bwd(dh2, x1, norm_mlp_g, dx2, "norm_mlp_bwd")

    (dmerged,) = _matmul(dx1, w["w_out"], mode="nt", out_dtypes=(F32,), name="out_proj_dx")
    (g_out,) = _matmul(merged, dx1, mode="tn", out_dtypes=(F32,), name="out_proj_dw")
    mb = _merge_bwd(dmerged, o_groups, lse_groups, o_sb, gl, b_gate, w["w_up_dil"], w["w_up_sb"])
    dua, dub, dgl, dbg, do_sb = mb[:5]
    do_groups, c_groups = mb[5:8], mb[8:11]
    (g_up_dil,) = _matmul(o_a, dua, mode="tn", out_dtypes=(F32,), name="up_dil_dw")
    (g_up_sb,) = _matmul(o_sb, dub, mode="tn", out_dtypes=(F32,), name="up_sb_dw")
    dq_sb, dk_sb, dv_sb = _sb_bwd(q_sb, k_sb, v_sb, _heads_major(do_sb), carries)
    dil_b = [_dil_bwd(qkv, do_groups[g], lse_groups[g], c_groups[g], g) for g in range(3)]
    dproj = jnp.concatenate(
        [dil_b[g][i].astype(BF16) for i in range(3) for g in range(3)]
        + [_heads_minor(t).astype(BF16) for t in (dq_sb, dk_sb, dv_sb)] + [dgl], axis=1)
    (g_in,) = _matmul(h, dproj, mode="tn", out_dtypes=(F32,), name="proj_dw", tn=IN_COLS // 2)
    (dh,) = _matmul(dproj, w["w_in"], mode="nt", out_dtypes=(F32,), name="proj_dx", tk=IN_COLS // 2)
    grad_x, dg_mix = _rms_bwd(dh, x, norm_mix_g, dx1, "norm_mix_bwd")

    big = {"w_in": g_in, "w_up_dil": g_up_dil, "w_up_sb": g_up_sb, "w_out": g_out,
           "w_mlp_in": g_mlp_in, "w_mlp_out": g_mlp_out}
    small = (dg_mix, dbg, dg_mlp, dg_final, loss_part)
    return grad_x, big, small


def kernel(x, norm_mix_g, w_in, b_gate, w_up_dil, w_up_sb, w_out, norm_mlp_g, w_mlp_in, w_mlp_out, norm_final_g, loss_target, m_norm_mix_g, m_w_in, m_b_gate, m_w_up_dil, m_w_up_sb, m_w_out, m_norm_mlp_g, m_w_mlp_in, m_w_mlp_out, m_norm_final_g, v_norm_mix_g, v_w_in, v_b_gate, v_w_up_dil, v_w_up_sb, v_w_out, v_norm_mlp_g, v_w_mlp_in, v_w_mlp_out, v_norm_final_g):
    shards = {"w_in": w_in[0], "w_up_dil": w_up_dil[0], "w_up_sb": w_up_sb[0], "w_out": w_out[0],
              "w_mlp_in": w_mlp_in[0], "w_mlp_out": w_mlp_out[0]}
    moments_m = {"w_in": m_w_in[0], "w_up_dil": m_w_up_dil[0], "w_up_sb": m_w_up_sb[0], "w_out": m_w_out[0],
                 "w_mlp_in": m_w_mlp_in[0], "w_mlp_out": m_w_mlp_out[0]}
    moments_v = {"w_in": v_w_in[0], "w_up_dil": v_w_up_dil[0], "w_up_sb": v_w_up_sb[0], "w_out": v_w_out[0],
                 "w_mlp_in": v_w_mlp_in[0], "w_mlp_out": v_w_mlp_out[0]}

    pack = _pack_shards({n: s.astype(BF16) for n, s in shards.items()})
    full = _unpack_full(_all_gather_weights(pack))

    grad_x, big, small = _local_step(x[0], loss_target[0], full, norm_mix_g, b_gate, norm_mlp_g, norm_final_g)

    core = lax.axis_index("c").astype(jnp.int32).reshape(1)
    gpack = _pack_full_grads(big)
    chip_sum = _add_halves(gpack, _swap_halves(gpack), core)
    reduced = _join_halves(_sum_chips(_exchange_chunks(chip_sum)))
    g_shard = _unpack_shard(reduced)

    dg_mix, dbg, dg_mlp, dg_final, loss_part = small
    loss_row = jnp.sum(loss_part, axis=0, keepdims=True)
    small_pack = jnp.concatenate(
        [jnp.sum(dg_mix, axis=0, keepdims=True), jnp.sum(dbg, axis=0, keepdims=True),
         jnp.sum(dg_mlp, axis=0, keepdims=True), jnp.sum(dg_final, axis=0, keepdims=True), loss_row], axis=1)
    n_small = small_pack.shape[1]
    small_sum = _all_reduce_small(small_pack.reshape(n_small // 128, 128)).reshape(1, n_small)
    g_norm_mix = small_sum[:, :D_MODEL]
    g_b_gate = small_sum[:, D_MODEL:3 * D_MODEL]
    g_norm_mlp = small_sum[:, 3 * D_MODEL:4 * D_MODEL]
    g_norm_final = small_sum[:, 4 * D_MODEL:5 * D_MODEL]
    loss = jnp.sum(small_sum[:, 5 * D_MODEL:])

    names = ["norm_mix_g", "w_in", "b_gate", "w_up_dil", "w_up_sb", "w_out", "norm_mlp_g", "w_mlp_in", "w_mlp_out",
             "norm_final_g"]
    grads = dict(g_shard)
    grads.update(norm_mix_g=g_norm_mix, b_gate=g_b_gate, norm_mlp_g=g_norm_mlp, norm_final_g=g_norm_final)
    weights = dict(shards)
    weights.update(norm_mix_g=norm_mix_g, b_gate=b_gate, norm_mlp_g=norm_mlp_g, norm_final_g=norm_final_g.reshape(1, D_MODEL))
    ms = dict(moments_m)
    ms.update(norm_mix_g=m_norm_mix_g, b_gate=m_b_gate, norm_mlp_g=m_norm_mlp_g, norm_final_g=m_norm_final_g.reshape(1, D_MODEL))
    vs = dict(moments_v)
    vs.update(norm_mix_g=v_norm_mix_g, b_gate=v_b_gate, norm_mlp_g=v_norm_mlp_g, norm_final_g=v_norm_final_g.reshape(1, D_MODEL))

    out_shapes = {"norm_mix_g": norm_mix_g.shape, "w_in": w_in.shape, "b_gate": b_gate.shape, "w_up_dil": w_up_dil.shape,
                  "w_up_sb": w_up_sb.shape, "w_out": w_out.shape, "norm_mlp_g": norm_mlp_g.shape,
                  "w_mlp_in": w_mlp_in.shape, "w_mlp_out": w_mlp_out.shape, "norm_final_g": norm_final_g.shape}
    g_out, d_out, m_out, v_out = [], [], [], []
    for n in names:
        d, nm, nv = _adamw(grads[n], weights[n], ms[n], vs[n], "adamw_" + n)
        shape = out_shapes[n]
        g_out.append(grads[n].reshape(shape))
        d_out.append(d.reshape(shape))
        m_out.append(nm.reshape(shape))
        v_out.append(nv.reshape(shape))
    return (loss, grad_x.reshape(x.shape), *g_out, *d_out, *m_out, *v_out)
```

```python
import functools
import math

import jax
import jax.numpy as jnp
import numpy as np
from jax import lax
from jax.experimental import pallas as pl
from jax.experimental.pallas import tpu as pltpu

F32 = jnp.float32
BF16 = jnp.bfloat16
MESH = pl.DeviceIdType.MESH

D_MODEL = 1024
HEAD_DIM = 64
DIL_GROUPS = ((128, 1), (512, 4), (2048, 16))
DIL_HEADS = 4
DIL_W = 256
N_DIL_HEADS = 12
SB_HEADS = 8
SB_W = SB_HEADS * HEAD_DIM
QKV_W = 3 * 3 * DIL_W + 3 * SB_W
GATE_W = 2 * D_MODEL
IN_COLS = QKV_W + GATE_W
D_FF = 4 * D_MODEL
BLOCK = 128
RMS_EPS = 1e-6
NEG_INF = -1e30
N_CHIPS = 4
N_DEV = 8

ADAM_LR = 0.001
ADAM_B1 = 0.9
ADAM_B2 = 0.999
ADAM_EPS = 1e-08
ADAM_WD = 0.01
ADAM_STEP = 10

VMEM_LIMIT = 56 * 1024 * 1024

SB_BQ = 256
SB_BK = 256


def _cparams(sem=None):
    if sem is None:
        return pltpu.CompilerParams(vmem_limit_bytes=VMEM_LIMIT)
    return pltpu.CompilerParams(dimension_semantics=sem, vmem_limit_bytes=VMEM_LIMIT)


def _dot(a, b, dims):
    return lax.dot_general(a, b, (dims, ((), ())), preferred_element_type=F32)


def _dot_nn(a, b):
    return _dot(a, b, ((1,), (0,)))


def _dot_nt(a, b):
    return _dot(a, b, ((1,), (1,)))


def _dot_tn(a, b):
    return _dot(a, b, ((0,), (0,)))


def _dot_f32_by_01(x, m01, pieces=3):
    hi = x.astype(BF16)
    r1 = x - hi.astype(F32)
    mid = r1.astype(BF16)
    if pieces == 2:
        return _dot_nn(hi, m01) + _dot_nn(mid, m01)
    lo = (r1 - mid.astype(F32)).astype(BF16)
    return _dot_nn(hi, m01) + _dot_nn(mid, m01) + _dot_nn(lo, m01)


def _matmul(a, b, *, mode, out_dtypes, name, tm=512, tn=512, tk=1024, extras=(), epilogue=None):
    if mode == "nn":
        (m, k), (k2, n) = a.shape, b.shape
    elif mode == "nt":
        (m, k), (n, k2) = a.shape, b.shape
    else:
        (k, m), (k2, n) = a.shape, b.shape
    assert k == k2, (a.shape, b.shape, mode)
    tm, tn, tk = min(tm, m), min(tn, n), min(tk, k)
    assert m % tm == 0 and n % tn == 0 and k % tk == 0, (m, n, k, tm, tn, tk)
    nk = k // tk
    n_out = len(out_dtypes)
    n_ex = len(extras)

    if mode == "nn":
        a_spec = pl.BlockSpec((tm, tk), lambda i, j, kk: (i, kk))
        b_spec = pl.BlockSpec((tk, tn), lambda i, j, kk: (kk, j))
        dot = _dot_nn
    elif mode == "nt":
        a_spec = pl.BlockSpec((tm, tk), lambda i, j, kk: (i, kk))
        b_spec = pl.BlockSpec((tn, tk), lambda i, j, kk: (j, kk))
        dot = _dot_nt
    else:
        a_spec = pl.BlockSpec((tk, tm), lambda i, j, kk: (kk, i))
        b_spec = pl.BlockSpec((tk, tn), lambda i, j, kk: (kk, j))
        dot = _dot_tn
    mn_spec = pl.BlockSpec((tm, tn), lambda i, j, kk: (i, j))

    def body(*refs):
        a_ref, b_ref = refs[0], refs[1]
        ex_refs = refs[2:2 + n_ex]
        out_refs = refs[2 + n_ex:2 + n_ex + n_out]
        acc_ref = refs[2 + n_ex + n_out] if nk > 1 else None
        part = dot(a_ref[...].astype(BF16), b_ref[...].astype(BF16))

        def finish(acc):
            if epilogue is None:
                outs = (acc,)
            else:
                outs = epilogue(acc, *[r[...] for r in ex_refs])
            for o_ref, o in zip(out_refs, outs):
                o_ref[...] = o.astype(o_ref.dtype)

        if nk == 1:
            finish(part)
        else:
            kk = pl.program_id(2)

            @pl.when(kk == 0)
            def _():
                acc_ref[...] = part

            @pl.when(kk > 0)
            def _():
                acc_ref[...] += part

            @pl.when(kk == nk - 1)
            def _():
                finish(acc_ref[...])

    outs = pl.pallas_call(
        body,
        name=name,
        grid=(m // tm, n // tn, nk),
        in_specs=[a_spec, b_spec] + [mn_spec] * n_ex,
        out_specs=[mn_spec] * n_out,
        out_shape=[jax.ShapeDtypeStruct((m, n), dt) for dt in out_dtypes],
        scratch_shapes=[pltpu.VMEM((tm, tn), F32)] if nk > 1 else [],
        compiler_params=_cparams(("parallel", "parallel", "arbitrary")),
    )(a, b, *extras)
    return outs


ROW_TILE = 512


def _rms_fwd(x, g, name):
    s, d = x.shape

    def body(x_ref, g_ref, h_ref):
        xv = x_ref[...]
        r = lax.rsqrt(jnp.mean(xv * xv, axis=-1, keepdims=True) + RMS_EPS)
        h_ref[...] = (xv * r * g_ref[...]).astype(BF16)

    return pl.pallas_call(
        body,
        name=name,
        grid=(s // ROW_TILE,),
        in_specs=[pl.BlockSpec((ROW_TILE, d), lambda i: (i, 0)), pl.BlockSpec((1, d), lambda i: (0, 0))],
        out_specs=pl.BlockSpec((ROW_TILE, d), lambda i: (i, 0)),
        out_shape=jax.ShapeDtypeStruct((s, d), BF16),
        compiler_params=_cparams(("parallel",)),
    )(x, g)


def _rms_bwd(dh, x, g, dres, name):
    s, d = x.shape

    def body(dh_ref, x_ref, g_ref, dres_ref, dx_ref, dg_ref):
        i = pl.program_id(0)
        xv = x_ref[...]
        r = lax.rsqrt(jnp.mean(xv * xv, axis=-1, keepdims=True) + RMS_EPS)
        xh = xv * r
        dhv = dh_ref[...]
        dxh = dhv * g_ref[...]
        dx = r * (dxh - xh * jnp.mean(dxh * xh, axis=-1, keepdims=True))
        dx_ref[...] = dres_ref[...] + dx
        part = jnp.sum((dhv * xh).reshape(ROW_TILE // 8, 8, d), axis=0)

        @pl.when(i == 0)
        def _():
            dg_ref[...] = part

        @pl.when(i > 0)
        def _():
            dg_ref[...] += part

    row = pl.BlockSpec((ROW_TILE, d), lambda i: (i, 0))
    return pl.pallas_call(
        body,
        name=name,
        grid=(s // ROW_TILE,),
        in_specs=[row, row, pl.BlockSpec((1, d), lambda i: (0, 0)), row],
        out_specs=[row, pl.BlockSpec((8, d), lambda i: (0, 0))],
        out_shape=[jax.ShapeDtypeStruct((s, d), F32), jax.ShapeDtypeStruct((8, d), F32)],
        compiler_params=_cparams(("arbitrary",)),
    )(dh, x, g, dres)


def _loss_head(x2, g, target):
    s, d = x2.shape

    def body(x_ref, g_ref, t_ref, dx_ref, dg_ref, loss_ref):
        i = pl.program_id(0)
        xv = x_ref[...]
        r = lax.rsqrt(jnp.mean(xv * xv, axis=-1, keepdims=True) + RMS_EPS)
        xh = xv * r
        gv = g_ref[...]
        err = xh * gv - t_ref[...]
        dy = err * (1.0 / d)
        dxh = dy * gv
        dx_ref[...] = r * (dxh - xh * jnp.mean(dxh * xh, axis=-1, keepdims=True))
        part_g = jnp.sum((dy * xh).reshape(ROW_TILE // 8, 8, d), axis=0)
        part_l = (0.5 / d) * jnp.sum((err * err).reshape(ROW_TILE // 8, 8, d), axis=0)

        @pl.when(i == 0)
        def _():
            dg_ref[...] = part_g
            loss_ref[...] = part_l

        @pl.when(i > 0)
        def _():
            dg_ref[...] += part_g
            loss_ref[...] += part_l

    row = pl.BlockSpec((ROW_TILE, d), lambda i: (i, 0))
    acc = pl.BlockSpec((8, d), lambda i: (0, 0))
    return pl.pallas_call(
        body,
        name="loss_head",
        grid=(s // ROW_TILE,),
        in_specs=[row, pl.BlockSpec((1, d), lambda i: (0, 0)), row],
        out_specs=[row, acc, acc],
        out_shape=[jax.ShapeDtypeStruct((s, d), F32), jax.ShapeDtypeStruct((8, d), F32),
                   jax.ShapeDtypeStruct((8, d), F32)],
        compiler_params=_cparams(("arbitrary",)),
    )(x2, g, target)


def _alibi_slopes():
    return np.exp2(np.float32(-8.0) * np.arange(1, N_DIL_HEADS + 1, dtype=np.float32) / np.float32(N_DIL_HEADS))


def _head_lane_mask(h, rows):
    lane = lax.broadcasted_iota(jnp.int32, (rows, DIL_W), 1)
    return (lane >= h * HEAD_DIM) & (lane < (h + 1) * HEAD_DIM)


def _band_terms(group, dil):
    qi = lax.broadcasted_iota(jnp.int32, (BLOCK, BLOCK), 0)
    kj = lax.broadcasted_iota(jnp.int32, (BLOCK, BLOCK), 1)
    steps_prev = (qi + BLOCK - kj).astype(F32) * float(dil)
    steps_cur = (qi - kj).astype(F32) * float(dil)
    return kj >= qi, kj <= qi, steps_prev, steps_cur


def _dil_fwd(qkv, group):
    _, dil = DIL_GROUPS[group]
    s = qkv.shape[0]
    sub = s // dil
    nb = sub // BLOCK
    view = qkv.reshape(sub, dil * QKV_W)
    wblk = QKV_W // DIL_W
    slopes = _alibi_slopes()[group * DIL_HEADS:(group + 1) * DIL_HEADS]

    def col(which):
        return lambda r, n: (n, r * wblk + which * 3 + group)

    def col_prev(which):
        return lambda r, n: (jnp.maximum(n - 1, 0), r * wblk + which * 3 + group)

    def body(q_ref, kc_ref, kp_ref, vc_ref, vp_ref, o_ref, lse_ref):
        n = pl.program_id(1)
        valid_p, valid_c, dist_p, dist_c = _band_terms(group, dil)
        valid_p = valid_p & (n > 0)
        q = q_ref[...]
        kc, kp, vc, vp = kc_ref[...], kp_ref[...], vc_ref[...], vp_ref[...]
        o_acc = jnp.zeros((BLOCK, DIL_W), F32)
        lse_acc = jnp.zeros((BLOCK, DIL_W), F32)
        for h in range(DIL_HEADS):
            hm = _head_lane_mask(h, BLOCK)
            qh = jnp.where(hm, q, jnp.zeros_like(q))
            lp = _dot_nt(qh, kp) * 0.125 - float(slopes[h]) * dist_p
            lc = _dot_nt(qh, kc) * 0.125 - float(slopes[h]) * dist_c
            lp = jnp.where(valid_p, lp, NEG_INF)
            lc = jnp.where(valid_c, lc, NEG_INF)
            mx = jnp.maximum(jnp.max(lp, axis=1, keepdims=True), jnp.max(lc, axis=1, keepdims=True))
            den = jnp.sum(jnp.exp(lp - mx), axis=1, keepdims=True) + jnp.sum(jnp.exp(lc - mx), axis=1, keepdims=True)
            lse = mx + jnp.log(den)
            pp = jnp.exp(lp - lse).astype(BF16)
            pc = jnp.exp(lc - lse).astype(BF16)
            oh = _dot_nn(pp, vp) + _dot_nn(pc, vc)
            o_acc = jnp.where(hm, oh, o_acc)
            lse_acc = jnp.where(hm, lse, lse_acc)
        o_ref[...] = o_acc
        lse_ref[...] = lse_acc

    blk = (BLOCK, DIL_W)
    o, lse = pl.pallas_call(
        body,
        name=f"dil_fwd_g{group}",
        grid=(dil, nb),
        in_specs=[pl.BlockSpec(blk, col(0)), pl.BlockSpec(blk, col(1)), pl.BlockSpec(blk, col_prev(1)),
                  pl.BlockSpec(blk, col(2)), pl.BlockSpec(blk, col_prev(2))],
        out_specs=[pl.BlockSpec(blk, lambda r, n: (n, r))] * 2,
        out_shape=[jax.ShapeDtypeStruct((sub, dil * DIL_W), F32)] * 2,
        compiler_params=_cparams(("parallel", "parallel")),
    )(view, view, view, view, view)
    return o.reshape(s, DIL_W), lse.reshape(s, DIL_W)


def _dil_bwd(qkv, do, lse, cterm, group):
    _, dil = DIL_GROUPS[group]
    s = qkv.shape[0]
    sub = s // dil
    nb = sub // BLOCK
    view = qkv.reshape(sub, dil * QKV_W)
    wblk = QKV_W // DIL_W
    slopes = _alibi_slopes()[group * DIL_HEADS:(group + 1) * DIL_HEADS]
    do_v, lse_v, c_v = (t.reshape(sub, dil * DIL_W) for t in (do, lse, cterm))

    def col(which, shift):
        if shift == 0:
            return lambda r, n: (n, r * wblk + which * 3 + group)
        if shift < 0:
            return lambda r, n: (jnp.maximum(n - 1, 0), r * wblk + which * 3 + group)
        return lambda r, n: (jnp.minimum(n + 1, nb - 1), r * wblk + which * 3 + group)

    def own(shift):
        if shift == 0:
            return lambda r, n: (n, r)
        return lambda r, n: (jnp.minimum(n + 1, nb - 1), r)

    def body(q_ref, qn_ref, kc_ref, kp_ref, vc_ref, vp_ref, do_ref, don_ref, lse_ref, lsen_ref, c_ref, cn_ref,
             dq_ref, dk_ref, dv_ref):
        n = pl.program_id(1)
        valid_p, valid_c, dist_p, dist_c = _band_terms(group, dil)
        has_prev = n > 0
        has_next = n < nb - 1
        q, qn = q_ref[...], qn_ref[...]
        kc, kp, vc, vp = kc_ref[...], kp_ref[...], vc_ref[...], vp_ref[...]
        dov, donv = do_ref[...], don_ref[...]
        lsev, lsenv, cv, cnv = lse_ref[...], lsen_ref[...], c_ref[...], cn_ref[...]
        dq_acc = jnp.zeros((BLOCK, DIL_W), F32)
        dk_acc = jnp.zeros((BLOCK, DIL_W), F32)
        dv_acc = jnp.zeros((BLOCK, DIL_W), F32)

        def head_col(t, hm):
            return jnp.max(jnp.where(hm, t, NEG_INF), axis=1, keepdims=True)

        for h in range(DIL_HEADS):
            hm = _head_lane_mask(h, BLOCK)
            slope = float(slopes[h])

            def pair(qh, k, v, doh, lse_h, c_h, valid, dist):
                logit = _dot_nt(qh, k) * 0.125 - slope * dist
                p = jnp.where(valid, jnp.exp(logit - lse_h), 0.0)
                dp = _dot_nt(doh, v)
                dlog = (p * (dp + c_h) * 0.125).astype(BF16)
                return p.astype(BF16), dlog

            qh = jnp.where(hm, q, jnp.zeros_like(q))
            doh = jnp.where(hm, dov, 0.0).astype(BF16)
            lse_h, c_h = head_col(lsev, hm), head_col(cv, hm)
            _, dlog_p = pair(qh, kp, vp, doh, lse_h, c_h, valid_p & has_prev, dist_p)
            p_c, dlog_c = pair(qh, kc, vc, doh, lse_h, c_h, valid_c, dist_c)
            dq_h = _dot_nn(dlog_p, kp) + _dot_nn(dlog_c, kc)
            dq_acc = jnp.where(hm, dq_h, dq_acc)
            qnh = jnp.where(hm, qn, jnp.zeros_like(qn))
            donh = jnp.where(hm, donv, 0.0).astype(BF16)
            p_n, dlog_n = pair(qnh, kc, vc, donh, head_col(lsenv, hm), head_col(cnv, hm), valid_p & has_next, dist_p)
            dk_acc += _dot_tn(dlog_c, qh) + _dot_tn(dlog_n, qnh)
            dv_acc += _dot_tn(p_c, doh) + _dot_tn(p_n, donh)
        dq_ref[...] = dq_acc
        dk_ref[...] = dk_acc
        dv_ref[...] = dv_acc

    blk = (BLOCK, DIL_W)
    outs = pl.pallas_call(
        body,
        name=f"dil_bwd_g{group}",
        grid=(dil, nb),
        in_specs=[pl.BlockSpec(blk, col(0, 0)), pl.BlockSpec(blk, col(0, 1)),
                  pl.BlockSpec(blk, col(1, 0)), pl.BlockSpec(blk, col(1, -1)),
                  pl.BlockSpec(blk, col(2, 0)), pl.BlockSpec(blk, col(2, -1)),
                  pl.BlockSpec(blk, own(0)), pl.BlockSpec(blk, own(1)),
                  pl.BlockSpec(blk, own(0)), pl.BlockSpec(blk, own(1)),
                  pl.BlockSpec(blk, own(0)), pl.BlockSpec(blk, own(1))],
        out_specs=[pl.BlockSpec(blk, lambda r, n: (n, r))] * 3,
        out_shape=[jax.ShapeDtypeStruct((sub, dil * DIL_W), F32)] * 3,
        compiler_params=_cparams(("parallel", "parallel")),
    )(view, view, view, view, view, view, do_v, do_v, lse_v, lse_v, c_v, c_v)
    return tuple(t.reshape(s, DIL_W) for t in outs)


SB_PAIRS = SB_HEADS // 2
SB_COL0 = (9 * DIL_W) // 128
LOG2E = 1.4426950408889634


def _sb_log_terms(zs):
    e = jnp.exp2(-jnp.abs(zs))
    return -(jnp.maximum(zs, 0.0) + jnp.log(1.0 + e) * LOG2E)


def _sb_consts(nkb):
    row = lax.broadcasted_iota(jnp.int32, (SB_BQ, SB_BK), 0)
    colk = lax.broadcasted_iota(jnp.int32, (SB_BQ, SB_BK), 1)
    rr = lax.broadcasted_iota(jnp.int32, (SB_BK, SB_BK), 0)
    cc = lax.broadcasted_iota(jnp.int32, (SB_BK, SB_BK), 1)
    lane = lax.broadcasted_iota(jnp.int32, (SB_BQ, 128), 1)
    blk_lane = lax.broadcasted_iota(jnp.int32, (SB_BQ, 2 * nkb), 1)
    return colk < row, rr, cc, lane < HEAD_DIM, blk_lane


def _sb_fwd(qkv):
    s = qkv.shape[0]
    nq, nkb = s // SB_BQ, s // SB_BK
    zscale = LOG2E / math.sqrt(HEAD_DIM)

    def body(q_ref, k_ref, v_ref, o_ref, carry_ref):
        i = pl.program_id(1)
        causal, rr, cc, first, blk_lane = _sb_consts(nkb)
        later = (rr > cc).astype(BF16)
        q2 = q_ref[...]
        qh = (jnp.where(first, q2, jnp.zeros_like(q2)), jnp.where(first, jnp.zeros_like(q2), q2))

        def block(j, accs, cls, ctile, masked):
            start = pl.multiple_of(j * SB_BK, SB_BK)
            kb = k_ref[pl.ds(start, SB_BK), :]
            vb = v_ref[pl.ds(start, SB_BK), :]
            zs = [_dot_nt(qh[hh], kb) * zscale for hh in range(2)]
            ls, sufs = [], []
            for hh in range(2):
                l = _sb_log_terms(zs[hh])
                if masked:
                    l = jnp.where(causal, l, 0.0)
                ls.append(l)
                sufs.append(_dot_f32_by_01(l, later, 2))
            new_accs, new_cls = [], []
            for hh in range(2):
                a = jnp.exp2(zs[hh] + ls[hh] + (sufs[hh] + cls[hh]))
                if masked:
                    a = jnp.where(causal, a, 0.0)
                new_accs.append(accs[hh] + _dot_nn(a.astype(BF16), vb))
            for hh in range(2):
                ctile = jnp.where(blk_lane == j + hh * nkb, cls[hh], ctile)
                new_cls.append(cls[hh] + jnp.sum(ls[hh], axis=1, keepdims=True))
            return tuple(new_accs), tuple(new_cls), ctile

        zero_acc = jnp.zeros((SB_BQ, 128), F32)
        zero_cl = jnp.zeros((SB_BQ, 1), F32)
        state = block(i, (zero_acc, zero_acc), (zero_cl, zero_cl), jnp.zeros((SB_BQ, 2 * nkb), F32), True)
        accs, _, ctile = lax.fori_loop(0, i, lambda t, c: block(i - 1 - t, *c, False), state)
        o_ref[...] = jnp.where(first, accs[0], accs[1])
        carry_ref[0] = ctile

    def full(which):
        return pl.BlockSpec((s, 128), lambda p, i: (0, SB_COL0 + 4 * which + p))

    return pl.pallas_call(
        body,
        name="sb_fwd",
        grid=(SB_PAIRS, nq),
        in_specs=[pl.BlockSpec((SB_BQ, 128), lambda p, i: (i, SB_COL0 + p)), full(1), full(2)],
        out_specs=[pl.BlockSpec((SB_BQ, 128), lambda p, i: (i, p)),
                   pl.BlockSpec((1, SB_BQ, 2 * nkb), lambda p, i: (p, i, 0))],
        out_shape=[jax.ShapeDtypeStruct((s, SB_W), F32), jax.ShapeDtypeStruct((SB_PAIRS, s, 2 * nkb), F32)],
        compiler_params=_cparams(("parallel", "parallel")),
    )(qkv, qkv, qkv)


def _sb_bwd(qkv, do, carries):
    s = qkv.shape[0]
    nq, nkb = s // SB_BQ, s // SB_BK
    scale = 1.0 / math.sqrt(HEAD_DIM)
    zscale = LOG2E * scale

    def body(q_ref, k_ref, v_ref, do_ref, carry_ref, dq_ref, dk_ref, dv_ref):
        i = pl.program_id(1)

        @pl.when(i == 0)
        def _():
            dk_ref[...] = jnp.zeros_like(dk_ref)
            dv_ref[...] = jnp.zeros_like(dv_ref)

        causal, rr, cc, first, blk_lane = _sb_consts(nkb)
        later = (rr > cc).astype(BF16)
        earlier = (rr < cc).astype(BF16)
        q2 = q_ref[...]
        qh = (jnp.where(first, q2, jnp.zeros_like(q2)), jnp.where(first, jnp.zeros_like(q2), q2))
        do2 = do_ref[...].astype(BF16)
        doh = (jnp.where(first, do2, jnp.zeros_like(do2)), jnp.where(first, jnp.zeros_like(do2), do2))
        ctile = carry_ref[0]

        def block(j, dqs, cgs, masked):
            start = pl.multiple_of(j * SB_BK, SB_BK)
            kb = k_ref[pl.ds(start, SB_BK), :]
            vb = v_ref[pl.ds(start, SB_BK), :]
            cl = [jnp.sum(jnp.where(blk_lane == j + hh * nkb, ctile, 0.0), axis=1, keepdims=True) for hh in range(2)]
            zs = [_dot_nt(qh[hh], kb) * zscale for hh in range(2)]
            da = [_dot_nt(doh[hh], vb) for hh in range(2)]
            ls, sigs, sufs = [], [], []
            for hh in range(2):
                l = _sb_log_terms(zs[hh])
                sigs.append(jnp.exp2(zs[hh] + l))
                if masked:
                    l = jnp.where(causal, l, 0.0)
                ls.append(l)
                sufs.append(_dot_f32_by_01(l, later, 2))
            a16, gs, gpres = [], [], []
            for hh in range(2):
                a = jnp.exp2(zs[hh] + ls[hh] + (sufs[hh] + cl[hh]))
                if masked:
                    a = jnp.where(causal, a, 0.0)
                g = a * da[hh]
                a16.append(a.astype(BF16))
                gs.append(g)
                gpres.append(_dot_f32_by_01(g, earlier, 2))
            new_dqs, new_cgs = [], []
            dk_part = jnp.zeros((SB_BK, 128), F32)
            dv_part = jnp.zeros((SB_BK, 128), F32)
            for hh in range(2):
                dz = gs[hh] - (gs[hh] + (gpres[hh] + cgs[hh])) * sigs[hh]
                if masked:
                    dz = jnp.where(causal, dz, 0.0)
                dz16 = (dz * scale).astype(BF16)
                new_dqs.append(dqs[hh] + _dot_nn(dz16, kb))
                dk_part = dk_part + _dot_tn(dz16, qh[hh])
                dv_part = dv_part + _dot_tn(a16[hh], doh[hh])
                new_cgs.append(cgs[hh] + jnp.sum(gs[hh], axis=1, keepdims=True))
            dk_ref[pl.ds(start, SB_BK), :] += dk_part
            dv_ref[pl.ds(start, SB_BK), :] += dv_part
            return tuple(new_dqs), tuple(new_cgs)

        zero_dq = jnp.zeros((SB_BQ, 128), F32)
        zero_cg = jnp.zeros((SB_BQ, 1), F32)
        state = lax.fori_loop(0, i, lambda j, c: block(j, *c, False), ((zero_dq, zero_dq), (zero_cg, zero_cg)))
        dqs, _ = block(i, *state, True)
        dq_ref[...] = jnp.where(first, dqs[0], dqs[1])

    def full(which):
        return pl.BlockSpec((s, 128), lambda p, i: (0, SB_COL0 + 4 * which + p))

    qblk = pl.BlockSpec((SB_BQ, 128), lambda p, i: (i, p))
    acc = pl.BlockSpec((s, 128), lambda p, i: (0, p))
    return pl.pallas_call(
        body,
        name="sb_bwd",
        grid=(SB_PAIRS, nq),
        in_specs=[pl.BlockSpec((SB_BQ, 128), lambda p, i: (i, SB_COL0 + p)), full(1), full(2), qblk,
                  pl.BlockSpec((1, SB_BQ, 2 * nkb), lambda p, i: (p, i, 0))],
        out_specs=[qblk, acc, acc],
        out_shape=[jax.ShapeDtypeStruct((s, SB_W), F32)] * 3,
        compiler_params=_cparams(("parallel", "arbitrary")),
    )(qkv, qkv, qkv, do, carries)


MERGE_TILE = 256


def _group_mix(lses):
    mx = jnp.maximum(jnp.maximum(lses[0], lses[1]), lses[2])
    es = [jnp.exp(t - mx) for t in lses]
    den = es[0] + es[1] + es[2]
    return [e / den for e in es]


def _merge_fwd(o_groups, lse_groups, o_sb, gl, b_gate, w_up_dil, w_up_sb):
    s = gl.shape[0]
    t = MERGE_TILE

    def body(o0, o1, o2, l0, l1, l2, ob_ref, gl_ref, bg_ref, wd_ref, ws_ref, merged_ref, oa_ref):
        w = _group_mix([l0[...], l1[...], l2[...]])
        oa = (w[0] * o0[...] + w[1] * o1[...] + w[2] * o2[...]).astype(BF16)
        ua = _dot_nn(oa, wd_ref[...])
        ub = _dot_nn(ob_ref[...].astype(BF16), ws_ref[...])
        gate = jax.nn.sigmoid(gl_ref[...] + bg_ref[...])
        merged_ref[...] = (gate[:, :D_MODEL] * ua + gate[:, D_MODEL:] * ub).astype(BF16)
        oa_ref[...] = oa

    dil = pl.BlockSpec((t, DIL_W), lambda i: (i, 0))
    const = lambda shape: pl.BlockSpec(shape, lambda i: (0, 0))
    return pl.pallas_call(
        body,
        name="merge_fwd",
        grid=(s // t,),
        in_specs=[dil] * 6 + [pl.BlockSpec((t, SB_W), lambda i: (i, 0)), pl.BlockSpec((t, GATE_W), lambda i: (i, 0)),
                              const((1, GATE_W)), const((DIL_W, D_MODEL)), const((SB_W, D_MODEL))],
        out_specs=[pl.BlockSpec((t, D_MODEL), lambda i: (i, 0)), dil],
        out_shape=[jax.ShapeDtypeStruct((s, D_MODEL), BF16), jax.ShapeDtypeStruct((s, DIL_W), BF16)],
        compiler_params=_cparams(("parallel",)),
    )(*o_groups, *lse_groups, o_sb, gl, b_gate, w_up_dil, w_up_sb)


def _merge_bwd(dmerged, o_groups, lse_groups, o_sb, gl, b_gate, w_up_dil, w_up_sb):
    s = gl.shape[0]
    t = MERGE_TILE

    def body(dm_ref, o0, o1, o2, l0, l1, l2, ob_ref, gl_ref, bg_ref, wd_ref, ws_ref,
             dua_ref, dub_ref, dgl_ref, dbg_ref, dosb_ref, d0, d1, d2, c0, c1, c2):
        i = pl.program_id(0)
        og = [o0[...], o1[...], o2[...]]
        w = _group_mix([l0[...], l1[...], l2[...]])
        oa = (w[0] * og[0] + w[1] * og[1] + w[2] * og[2]).astype(BF16)
        ua = _dot_nn(oa, wd_ref[...])
        ub = _dot_nn(ob_ref[...].astype(BF16), ws_ref[...])
        gate = jax.nn.sigmoid(gl_ref[...] + bg_ref[...])
        ga, gb = gate[:, :D_MODEL], gate[:, D_MODEL:]
        dm = dm_ref[...]
        dua = (dm * ga).astype(BF16)
        dub = (dm * gb).astype(BF16)
        dua_ref[...] = dua
        dub_ref[...] = dub
        dgl_a = dm * ua * ga * (1.0 - ga)
        dgl_b = dm * ub * gb * (1.0 - gb)
        dgl_ref[:, :D_MODEL] = dgl_a.astype(BF16)
        dgl_ref[:, D_MODEL:] = dgl_b.astype(BF16)
        part = jnp.concatenate([jnp.sum(dgl_a.reshape(t // 8, 8, D_MODEL), axis=0),
                                jnp.sum(dgl_b.reshape(t // 8, 8, D_MODEL), axis=0)], axis=1)

        @pl.when(i == 0)
        def _():
            dbg_ref[...] = part

        @pl.when(i > 0)
        def _():
            dbg_ref[...] += part

        dosb_ref[...] = _dot_nt(dub, ws_ref[...])
        doa = _dot_nt(dua, wd_ref[...])
        rr = lax.broadcasted_iota(jnp.int32, (DIL_W, DIL_W), 0) // HEAD_DIM
        cc = lax.broadcasted_iota(jnp.int32, (DIL_W, DIL_W), 1) // HEAD_DIM
        same_head = (rr == cc).astype(BF16)
        dw = [_dot_f32_by_01(doa * og[g], same_head) for g in range(3)]
        mean_dw = w[0] * dw[0] + w[1] * dw[1] + w[2] * dw[2]
        for g, (d_ref, c_ref) in enumerate(((d0, c0), (d1, c1), (d2, c2))):
            d_ref[...] = w[g] * doa
            c_ref[...] = -w[g] * mean_dw

    dil = pl.BlockSpec((t, DIL_W), lambda i: (i, 0))
    wide = pl.BlockSpec((t, D_MODEL), lambda i: (i, 0))
    gate2 = pl.BlockSpec((t, GATE_W), lambda i: (i, 0))
    sbw = pl.BlockSpec((t, SB_W), lambda i: (i, 0))
    const = lambda shape: pl.BlockSpec(shape, lambda i: (0, 0))
    return pl.pallas_call(
        body,
        name="merge_bwd",
        grid=(s // t,),
        in_specs=[wide] + [dil] * 6 + [sbw, gate2, const((1, GATE_W)), const((DIL_W, D_MODEL)), const((SB_W, D_MODEL))],
        out_specs=[wide, wide, gate2, const((8, GATE_W)), sbw] + [dil] * 6,
        out_shape=[jax.ShapeDtypeStruct((s, D_MODEL), BF16), jax.ShapeDtypeStruct((s, D_MODEL), BF16),
                   jax.ShapeDtypeStruct((s, GATE_W), BF16), jax.ShapeDtypeStruct((8, GATE_W), F32),
                   jax.ShapeDtypeStruct((s, SB_W), F32)] + [jax.ShapeDtypeStruct((s, DIL_W), F32)] * 6,
        compiler_params=_cparams(("arbitrary",)),
    )(dmerged, *o_groups, *lse_groups, o_sb, gl, b_gate, w_up_dil, w_up_sb)


ANY = pl.BlockSpec(memory_space=pl.ANY)


def _place():
    x, y, c = lax.axis_index("x"), lax.axis_index("y"), lax.axis_index("c")
    other_chips = [(1 - x, y), (x, 1 - y), (1 - x, 1 - y)]
    return x, y, c, other_chips


def _all_gather_weights(pack):
    r, wd = pack.shape
    rh = r // 2

    def body(p_ref, out_ref, send_sems, recv_sems, local_sem):
        x, y, c, chips = _place()
        me, sibling = 2 * x + y, (x, y, 1 - c)

        def half(chip_idx, core):
            return out_ref.at[chip_idx, pl.ds(core * rh, rh), :]

        def copy(k, chip_idx, core, to, src=None):
            return pltpu.make_async_remote_copy(
                src_ref=half(chip_idx, core) if src is None else src, dst_ref=half(chip_idx, core),
                send_sem=send_sems.at[k], recv_sem=recv_sems.at[k], device_id=to, device_id_type=MESH)

        mine = pltpu.make_async_copy(p_ref, out_ref.at[me], local_sem)
        mine.start()
        first = [copy(j, me, c, (*chip, c), src=p_ref.at[pl.ds(c * rh, rh), :]) for j, chip in enumerate(chips)]
        for cp in first:
            cp.start()
        passed = [copy(3 + j, 2 * chip[0] + chip[1], c, sibling) for j, chip in enumerate(chips)]
        for j, chip in enumerate(chips):
            copy(j, 2 * chip[0] + chip[1], c, (x, y, c)).wait_recv()
            passed[j].start()
        for j, chip in enumerate(chips):
            copy(3 + j, 2 * chip[0] + chip[1], 1 - c, (x, y, c)).wait_recv()
        for cp in first + passed:
            cp.wait_send()
        mine.wait()

    return pl.pallas_call(
        body,
        name="all_gather_weights",
        in_specs=[ANY],
        out_specs=ANY,
        out_shape=jax.ShapeDtypeStruct((N_CHIPS, r, wd), pack.dtype),
        scratch_shapes=[pltpu.SemaphoreType.DMA((6,)), pltpu.SemaphoreType.DMA((6,)), pltpu.SemaphoreType.DMA],
    )(pack)


def _swap_halves(g):
    n, r, wd = g.shape
    rh = r // 2

    def body(g_ref, out_ref, send_sem, recv_sem):
        x, y, c, _ = _place()
        cp = pltpu.make_async_remote_copy(
            src_ref=g_ref.at[:, pl.ds((1 - c) * rh, rh), :], dst_ref=out_ref,
            send_sem=send_sem, recv_sem=recv_sem, device_id=(x, y, 1 - c), device_id_type=MESH)
        cp.start()
        cp.wait()

    return pl.pallas_call(
        body,
        name="grad_swap_halves",
        in_specs=[ANY],
        out_specs=ANY,
        out_shape=jax.ShapeDtypeStruct((n, rh, wd), g.dtype),
        scratch_shapes=[pltpu.SemaphoreType.DMA, pltpu.SemaphoreType.DMA],
    )(g)


def _add_halves(g, got, core):
    n, r, wd = g.shape
    rh = r // 2
    t = rh // 4
    nt = rh // t

    def body(c_ref, a_ref, b_ref, o_ref):
        o_ref[...] = a_ref[...] + b_ref[...]

    grid_spec = pltpu.PrefetchScalarGridSpec(
        num_scalar_prefetch=1,
        grid=(n, nt),
        in_specs=[pl.BlockSpec((1, t, wd), lambda s, i, c: (s, c[0] * nt + i, 0)),
                  pl.BlockSpec((1, t, wd), lambda s, i, c: (s, i, 0))],
        out_specs=pl.BlockSpec((1, t, wd), lambda s, i, c: (s, i, 0)),
    )
    return pl.pallas_call(
        body,
        name="grad_add_halves",
        grid_spec=grid_spec,
        out_shape=jax.ShapeDtypeStruct((n, rh, wd), F32),
        compiler_params=_cparams(("parallel", "parallel")),
    )(core, g, got)


def _exchange_chunks(h):
    n, rh, wd = h.shape

    def body(h_ref, out_ref, send_sems, recv_sems, local_sem):
        x, y, c, chips = _place()
        me = 2 * x + y
        mine = pltpu.make_async_copy(h_ref.at[me], out_ref.at[me], local_sem)
        mine.start()
        sends = []
        for j, chip in enumerate(chips):
            them = 2 * chip[0] + chip[1]
            sends.append(pltpu.make_async_remote_copy(
                src_ref=h_ref.at[them], dst_ref=out_ref.at[me],
                send_sem=send_sems.at[j], recv_sem=recv_sems.at[j], device_id=(*chip, c), device_id_type=MESH))
        for cp in sends:
            cp.start()
        for j, chip in enumerate(chips):
            them = 2 * chip[0] + chip[1]
            pltpu.make_async_remote_copy(
                src_ref=h_ref.at[them], dst_ref=out_ref.at[them],
                send_sem=send_sems.at[j], recv_sem=recv_sems.at[j], device_id=(*chip, c), device_id_type=MESH).wait_recv()
        for cp in sends:
            cp.wait_send()
        mine.wait()

    return pl.pallas_call(
        body,
        name="grad_exchange_chunks",
        in_specs=[ANY],
        out_specs=ANY,
        out_shape=jax.ShapeDtypeStruct((n, rh, wd), h.dtype),
        scratch_shapes=[pltpu.SemaphoreType.DMA((3,)), pltpu.SemaphoreType.DMA((3,)), pltpu.SemaphoreType.DMA],
    )(h)


def _sum_chips(b):
    n, rh, wd = b.shape
    t = rh // 4

    def body(b_ref, o_ref):
        o_ref[...] = ((b_ref[0] + b_ref[1]) + b_ref[2]) + b_ref[3]

    return pl.pallas_call(
        body,
        name="grad_sum_chips",
        grid=(rh // t,),
        in_specs=[pl.BlockSpec((n, t, wd), lambda i: (0, i, 0))],
        out_specs=pl.BlockSpec((t, wd), lambda i: (i, 0)),
        out_shape=jax.ShapeDtypeStruct((rh, wd), F32),
        compiler_params=_cparams(("parallel",)),
    )(b)


def _join_halves(tc):
    rh, wd = tc.shape

    def body(t_ref, out_ref, send_sem, recv_sem, local_sem):
        x, y, c, _ = _place()
        mine = pltpu.make_async_copy(t_ref, out_ref.at[pl.ds(c * rh, rh), :], local_sem)
        mine.start()
        cp = pltpu.make_async_remote_copy(
            src_ref=t_ref, dst_ref=out_ref.at[pl.ds(c * rh, rh), :],
            send_sem=send_sem, recv_sem=recv_sem, device_id=(x, y, 1 - c), device_id_type=MESH)
        cp.start()
        cp.wait()
        mine.wait()

    return pl.pallas_call(
        body,
        name="grad_join_halves",
        in_specs=[ANY],
        out_specs=ANY,
        out_shape=jax.ShapeDtypeStruct((2 * rh, wd), tc.dtype),
        scratch_shapes=[pltpu.SemaphoreType.DMA, pltpu.SemaphoreType.DMA, pltpu.SemaphoreType.DMA],
    )(tc)


def _all_reduce_small(pack):
    rows, lanes = pack.shape

    def body(p_ref, out_ref, buf, send_sems, recv_sems):
        x, y, c, _ = _place()
        me = 4 * x + 2 * y + c
        buf[me] = p_ref[...]
        sends = []
        for k in range(1, N_DEV):
            peer = (x ^ (k >> 2), y ^ ((k >> 1) & 1), c ^ (k & 1))
            sends.append(pltpu.make_async_remote_copy(
                src_ref=p_ref, dst_ref=buf.at[me], send_sem=send_sems.at[k - 1], recv_sem=recv_sems.at[k - 1],
                device_id=peer, device_id_type=MESH))
        for cp in sends:
            cp.start()
        for k in range(1, N_DEV):
            pltpu.make_async_remote_copy(
                src_ref=p_ref, dst_ref=buf.at[me ^ k], send_sem=send_sems.at[k - 1], recv_sem=recv_sems.at[k - 1],
                device_id=(x, y, c), device_id_type=MESH).wait_recv()
        for cp in sends:
            cp.wait_send()
        total = buf[0]
        for d in range(1, N_DEV):
            total = total + buf[d]
        out_ref[...] = total

    vm = pl.BlockSpec(memory_space=pltpu.VMEM)
    return pl.pallas_call(
        body,
        name="all_reduce_small",
        in_specs=[vm],
        out_specs=vm,
        out_shape=jax.ShapeDtypeStruct((rows, lanes), F32),
        scratch_shapes=[pltpu.VMEM((N_DEV, rows, lanes), F32), pltpu.SemaphoreType.DMA((N_DEV - 1,)),
                        pltpu.SemaphoreType.DMA((N_DEV - 1,))],
    )(pack)


def _adamw(g, w, m, v, name):
    rows, cols = g.shape
    t = rows
    for cand in (256, 128, 64, 32, 16, 8):
        if rows % cand == 0:
            t = cand
            break

    def body(g_ref, w_ref, m_ref, v_ref, d_ref, nm_ref, nv_ref):
        gv = g_ref[...]
        mv = ADAM_B1 * m_ref[...] + (1.0 - ADAM_B1) * gv
        vv = ADAM_B2 * v_ref[...] + (1.0 - ADAM_B2) * (gv * gv)
        m_hat = mv / (1.0 - ADAM_B1 ** ADAM_STEP)
        v_hat = vv / (1.0 - ADAM_B2 ** ADAM_STEP)
        d_ref[...] = -ADAM_LR * (m_hat / (jnp.sqrt(v_hat) + ADAM_EPS) + ADAM_WD * w_ref[...])
        nm_ref[...] = mv
        nv_ref[...] = vv

    blk = pl.BlockSpec((t, cols), lambda i: (i, 0))
    return pl.pallas_call(
        body,
        name=name,
        grid=(rows // t,),
        in_specs=[blk] * 4,
        out_specs=[blk] * 3,
        out_shape=[jax.ShapeDtypeStruct((rows, cols), F32)] * 3,
        compiler_params=_cparams(("parallel",)),
    )(g, w, m, v)


PACK_W = 1024
BIG = (("w_in", (D_MODEL, IN_COLS), 1), ("w_up_dil", (DIL_W, D_MODEL), 1), ("w_up_sb", (SB_W, D_MODEL), 1),
       ("w_out", (D_MODEL, D_MODEL), 0), ("w_mlp_in", (D_MODEL, D_FF), 1), ("w_mlp_out", (D_FF, D_MODEL), 0))


def _shard_shape(shape, axis):
    return tuple(d // N_CHIPS if a == axis else d for a, d in enumerate(shape))


def _pack_rows():
    rows, at = {}, 0
    for name, shape, axis in BIG:
        n = math.prod(_shard_shape(shape, axis)) // PACK_W
        rows[name] = (at, n)
        at += n
    return rows, at


def _pack_shards(shards):
    return jnp.concatenate([shards[name].reshape(-1, PACK_W) for name, _, _ in BIG], axis=0)


def _unpack_full(gathered):
    rows, _ = _pack_rows()
    full = {}
    for name, shape, axis in BIG:
        at, n = rows[name]
        parts = gathered[:, at:at + n, :].reshape((N_CHIPS,) + _shard_shape(shape, axis))
        if axis == 0:
            full[name] = parts.reshape(shape)
        else:
            full[name] = jnp.transpose(parts, (1, 0, 2)).reshape(shape)
    return full


def _pack_full_grads(grads):
    chunks = []
    for name, shape, axis in BIG:
        g = grads[name]
        if axis == 0:
            parts = g.reshape((N_CHIPS, shape[0] // N_CHIPS, shape[1]))
        else:
            parts = jnp.transpose(g.reshape((shape[0], N_CHIPS, shape[1] // N_CHIPS)), (1, 0, 2))
        chunks.append(parts.reshape(N_CHIPS, -1, PACK_W))
    return jnp.concatenate(chunks, axis=1)


def _unpack_shard(packed):
    rows, _ = _pack_rows()
    return {name: packed[rows[name][0]:rows[name][0] + rows[name][1]].reshape(_shard_shape(shape, axis))
            for name, shape, axis in BIG}


def _local_step(x, target, w, norm_mix_g, b_gate, norm_mlp_g, norm_final_g):
    w_qkv, w_gate = w["w_in"][:, :QKV_W], w["w_in"][:, QKV_W:]

    h = _rms_fwd(x, norm_mix_g, "norm_mix")
    (qkv,) = _matmul(h, w_qkv, mode="nn", out_dtypes=(BF16,), name="proj_qkv", tn=768)
    (gl,) = _matmul(h, w_gate, mode="nn", out_dtypes=(F32,), name="proj_gate")
    dil = [_dil_fwd(qkv, g) for g in range(3)]
    o_groups, lse_groups = [d[0] for d in dil], [d[1] for d in dil]
    o_sb, carries = _sb_fwd(qkv)
    merged, o_a = _merge_fwd(o_groups, lse_groups, o_sb, gl, b_gate, w["w_up_dil"], w["w_up_sb"])
    (x1,) = _matmul(merged, w["w_out"], mode="nn", out_dtypes=(F32,), name="out_proj",
                    extras=(x,), epilogue=lambda acc, res: (res + acc,))
    h2 = _rms_fwd(x1, norm_mlp_g, "norm_mlp")
    u, act = _matmul(h2, w["w_mlp_in"], mode="nn", out_dtypes=(F32, BF16), name="mlp_in",
                     epilogue=lambda acc: (acc, jnp.square(jnp.maximum(acc, 0.0))))
    (x2,) = _matmul(act, w["w_mlp_out"], mode="nn", out_dtypes=(F32,), name="mlp_out", tk=2048,
                    extras=(x1,), epilogue=lambda acc, res: (res + acc,))
    dx2, dg_final, loss_part = _loss_head(x2, norm_final_g.reshape(1, D_MODEL), target)

    (du,) = _matmul(dx2, w["w_mlp_out"], mode="nt", out_dtypes=(BF16,), name="mlp_out_dx",
                    extras=(u,), epilogue=lambda acc, uu: (acc * (2.0 * jnp.maximum(uu, 0.0)),))
    (g_mlp_out,) = _matmul(act, dx2, mode="tn", out_dtypes=(F32,), name="mlp_out_dw")
    (g_mlp_in,) = _matmul(h2, du, mode="tn", out_dtypes=(F32,), name="mlp_in_dw")
    (dh2,) = _matmul(du, w["w_mlp_in"], mode="nt", out_dtypes=(F32,), name="mlp_in_dx", tk=2048)
    dx1, dg_mlp = _rms_bwd(dh2, x1, norm_mlp_g, dx2, "norm_mlp_bwd")

    (dmerged,) = _matmul(dx1, w["w_out"], mode="nt", out_dtypes=(F32,), name="out_proj_dx")
    (g_out,) = _matmul(merged, dx1, mode="tn", out_dtypes=(F32,), name="out_proj_dw")
    mb = _merge_bwd(dmerged, o_groups, lse_groups, o_sb, gl, b_gate, w["w_up_dil"], w["w_up_sb"])
    dua, dub, dgl, dbg, do_sb = mb[:5]
    do_groups, c_groups = mb[5:8], mb[8:11]
    (g_up_dil,) = _matmul(o_a, dua, mode="tn", out_dtypes=(F32,), name="up_dil_dw")
    (g_up_sb,) = _matmul(o_sb, dub, mode="tn", out_dtypes=(F32,), name="up_sb_dw")
    dq_sb, dk_sb, dv_sb = _sb_bwd(qkv, do_sb, carries)
    dil_b = [_dil_bwd(qkv, do_groups[g], lse_groups[g], c_groups[g], g) for g in range(3)]
    dproj = jnp.concatenate(
        [dil_b[g][i].astype(BF16) for i in range(3) for g in range(3)]
        + [t.astype(BF16) for t in (dq_sb, dk_sb, dv_sb)] + [dgl], axis=1)
    (g_in,) = _matmul(h, dproj, mode="tn", out_dtypes=(F32,), name="proj_dw", tn=IN_COLS // 2)
    (dh,) = _matmul(dproj, w["w_in"], mode="nt", out_dtypes=(F32,), name="proj_dx", tk=IN_COLS // 2)
    grad_x, dg_mix = _rms_bwd(dh, x, norm_mix_g, dx1, "norm_mix_bwd")

    big = {"w_in": g_in, "w_up_dil": g_up_dil, "w_up_sb": g_up_sb, "w_out": g_out,
           "w_mlp_in": g_mlp_in, "w_mlp_out": g_mlp_out}
    small = (dg_mix, dbg, dg_mlp, dg_final, loss_part)
    return grad_x, big, small


def kernel(x, norm_mix_g, w_in, b_gate, w_up_dil, w_up_sb, w_out, norm_mlp_g, w_mlp_in, w_mlp_out, norm_final_g, loss_target, m_norm_mix_g, m_w_in, m_b_gate, m_w_up_dil, m_w_up_sb, m_w_out, m_norm_mlp_g, m_w_mlp_in, m_w_mlp_out, m_norm_final_g, v_norm_mix_g, v_w_in, v_b_gate, v_w_up_dil, v_w_up_sb, v_w_out, v_norm_mlp_g, v_w_mlp_in, v_w_mlp_out, v_norm_final_g):
    shards = {"w_in": w_in[0], "w_up_dil": w_up_dil[0], "w_up_sb": w_up_sb[0], "w_out": w_out[0],
              "w_mlp_in": w_mlp_in[0], "w_mlp_out": w_mlp_out[0]}
    moments_m = {"w_in": m_w_in[0], "w_up_dil": m_w_up_dil[0], "w_up_sb": m_w_up_sb[0], "w_out": m_w_out[0],
                 "w_mlp_in": m_w_mlp_in[0], "w_mlp_out": m_w_mlp_out[0]}
    moments_v = {"w_in": v_w_in[0], "w_up_dil": v_w_up_dil[0], "w_up_sb": v_w_up_sb[0], "w_out": v_w_out[0],
                 "w_mlp_in": v_w_mlp_in[0], "w_mlp_out": v_w_mlp_out[0]}

    pack = _pack_shards({n: s.astype(BF16) for n, s in shards.items()})
    full = _unpack_full(_all_gather_weights(pack))

    grad_x, big, small = _local_step(x[0], loss_target[0], full, norm_mix_g, b_gate, norm_mlp_g, norm_final_g)

    core = lax.axis_index("c").astype(jnp.int32).reshape(1)
    gpack = _pack_full_grads(big)
    chip_sum = _add_halves(gpack, _swap_halves(gpack), core)
    reduced = _join_halves(_sum_chips(_exchange_chunks(chip_sum)))
    g_shard = _unpack_shard(reduced)

    dg_mix, dbg, dg_mlp, dg_final, loss_part = small
    loss_row = jnp.sum(loss_part, axis=0, keepdims=True)
    small_pack = jnp.concatenate(
        [jnp.sum(dg_mix, axis=0, keepdims=True), jnp.sum(dbg, axis=0, keepdims=True),
         jnp.sum(dg_mlp, axis=0, keepdims=True), jnp.sum(dg_final, axis=0, keepdims=True), loss_row], axis=1)
    n_small = small_pack.shape[1]
    small_sum = _all_reduce_small(small_pack.reshape(n_small // 128, 128)).reshape(1, n_small)
    g_norm_mix = small_sum[:, :D_MODEL]
    g_b_gate = small_sum[:, D_MODEL:3 * D_MODEL]
    g_norm_mlp = small_sum[:, 3 * D_MODEL:4 * D_MODEL]
    g_norm_final = small_sum[:, 4 * D_MODEL:5 * D_MODEL]
    loss = jnp.sum(small_sum[:, 5 * D_MODEL:])

    names = ["norm_mix_g", "w_in", "b_gate", "w_up_dil", "w_up_sb", "w_out", "norm_mlp_g", "w_mlp_in", "w_mlp_out",
             "norm_final_g"]
    grads = dict(g_shard)
    grads.update(norm_mix_g=g_norm_mix, b_gate=g_b_gate, norm_mlp_g=g_norm_mlp, norm_final_g=g_norm_final)
    weights = dict(shards)
    weights.update(norm_mix_g=norm_mix_g, b_gate=b_gate, norm_mlp_g=norm_mlp_g, norm_final_g=norm_final_g.reshape(1, D_MODEL))
    ms = dict(moments_m)
    ms.update(norm_mix_g=m_norm_mix_g, b_gate=m_b_gate, norm_mlp_g=m_norm_mlp_g, norm_final_g=m_norm_final_g.reshape(1, D_MODEL))
    vs = dict(moments_v)
    vs.update(norm_mix_g=v_norm_mix_g, b_gate=v_b_gate, norm_mlp_g=v_norm_mlp_g, norm_final_g=v_norm_final_g.reshape(1, D_MODEL))

    out_shapes = {"norm_mix_g": norm_mix_g.shape, "w_in": w_in.shape, "b_gate": b_gate.shape, "w_up_dil": w_up_dil.shape,
                  "w_up_sb": w_up_sb.shape, "w_out": w_out.shape, "norm_mlp_g": norm_mlp_g.shape,
                  "w_mlp_in": w_mlp_in.shape, "w_mlp_out": w_mlp_out.shape, "norm_final_g": norm_final_g.shape}
    g_out, d_out, m_out, v_out = [], [], [], []
    for n in names:
        d, nm, nv = _adamw(grads[n], weights[n], ms[n], vs[n], "adamw_" + n)
        shape = out_shapes[n]
        g_out.append(grads[n].reshape(shape))
        d_out.append(d.reshape(shape))
        m_out.append(nm.reshape(shape))
        v_out.append(nv.reshape(shape))
    return (loss, grad_x.reshape(x.shape), *g_out, *d_out, *m_out, *v_out)
```

```python
import functools
import math

import jax
import jax.numpy as jnp
import numpy as np
from jax import lax
from jax.experimental import pallas as pl
from jax.experimental.pallas import tpu as pltpu

F32 = jnp.float32
BF16 = jnp.bfloat16
MESH = pl.DeviceIdType.MESH

D_MODEL = 1024
HEAD_DIM = 64
DIL_GROUPS = ((128, 1), (512, 4), (2048, 16))
DIL_HEADS = 4
DIL_W = 256
N_DIL_HEADS = 12
SB_HEADS = 8
SB_W = SB_HEADS * HEAD_DIM
QKV_W = 3 * 3 * DIL_W + 3 * SB_W
GATE_W = 2 * D_MODEL
IN_COLS = QKV_W + GATE_W
D_FF = 4 * D_MODEL
BLOCK = 128
RMS_EPS = 1e-6
NEG_INF = -1e30
N_CHIPS = 4
N_DEV = 8

ADAM_LR = 0.001
ADAM_B1 = 0.9
ADAM_B2 = 0.999
ADAM_EPS = 1e-08
ADAM_WD = 0.01
ADAM_STEP = 10

VMEM_LIMIT = 56 * 1024 * 1024

SB_BQ = 256
SB_BK = 256


def _cparams(sem=None):
    if sem is None:
        return pltpu.CompilerParams(vmem_limit_bytes=VMEM_LIMIT)
    return pltpu.CompilerParams(dimension_semantics=sem, vmem_limit_bytes=VMEM_LIMIT)


def _dot(a, b, dims):
    return lax.dot_general(a, b, (dims, ((), ())), preferred_element_type=F32)


def _dot_nn(a, b):
    return _dot(a, b, ((1,), (0,)))


def _dot_nt(a, b):
    return _dot(a, b, ((1,), (1,)))


def _dot_tn(a, b):
    return _dot(a, b, ((0,), (0,)))


def _dot_f32_by_01(x, m01, pieces=3):
    hi = x.astype(BF16)
    r1 = x - hi.astype(F32)
    mid = r1.astype(BF16)
    if pieces == 2:
        return _dot_nn(hi, m01) + _dot_nn(mid, m01)
    lo = (r1 - mid.astype(F32)).astype(BF16)
    return _dot_nn(hi, m01) + _dot_nn(mid, m01) + _dot_nn(lo, m01)


def _matmul(a, b, *, mode, out_dtypes, name, tm=512, tn=512, tk=1024, extras=(), epilogue=None):
    if mode == "nn":
        (m, k), (k2, n) = a.shape, b.shape
    elif mode == "nt":
        (m, k), (n, k2) = a.shape, b.shape
    else:
        (k, m), (k2, n) = a.shape, b.shape
    assert k == k2, (a.shape, b.shape, mode)
    tm, tn, tk = min(tm, m), min(tn, n), min(tk, k)
    assert m % tm == 0 and n % tn == 0 and k % tk == 0, (m, n, k, tm, tn, tk)
    nk = k // tk
    n_out = len(out_dtypes)
    n_ex = len(extras)

    if mode == "nn":
        a_spec = pl.BlockSpec((tm, tk), lambda i, j, kk: (i, kk))
        b_spec = pl.BlockSpec((tk, tn), lambda i, j, kk: (kk, j))
        dot = _dot_nn
    elif mode == "nt":
        a_spec = pl.BlockSpec((tm, tk), lambda i, j, kk: (i, kk))
        b_spec = pl.BlockSpec((tn, tk), lambda i, j, kk: (j, kk))
        dot = _dot_nt
    else:
        a_spec = pl.BlockSpec((tk, tm), lambda i, j, kk: (kk, i))
        b_spec = pl.BlockSpec((tk, tn), lambda i, j, kk: (kk, j))
        dot = _dot_tn
    mn_spec = pl.BlockSpec((tm, tn), lambda i, j, kk: (i, j))

    def body(*refs):
        a_ref, b_ref = refs[0], refs[1]
        ex_refs = refs[2:2 + n_ex]
        out_refs = refs[2 + n_ex:2 + n_ex + n_out]
        acc_ref = refs[2 + n_ex + n_out] if nk > 1 else None
        part = dot(a_ref[...].astype(BF16), b_ref[...].astype(BF16))

        def finish(acc):
            if epilogue is None:
                outs = (acc,)
            else:
                outs = epilogue(acc, *[r[...] for r in ex_refs])
            for o_ref, o in zip(out_refs, outs):
                o_ref[...] = o.astype(o_ref.dtype)

        if nk == 1:
            finish(part)
        else:
            kk = pl.program_id(2)

            @pl.when(kk == 0)
            def _():
                acc_ref[...] = part

            @pl.when(kk > 0)
            def _():
                acc_ref[...] += part

            @pl.when(kk == nk - 1)
            def _():
                finish(acc_ref[...])

    outs = pl.pallas_call(
        body,
        name=name,
        grid=(m // tm, n // tn, nk),
        in_specs=[a_spec, b_spec] + [mn_spec] * n_ex,
        out_specs=[mn_spec] * n_out,
        out_shape=[jax.ShapeDtypeStruct((m, n), dt) for dt in out_dtypes],
        scratch_shapes=[pltpu.VMEM((tm, tn), F32)] if nk > 1 else [],
        compiler_params=_cparams(("parallel", "parallel", "arbitrary")),
    )(a, b, *extras)
    return outs


ROW_TILE = 512


def _rms_fwd(x, g, name):
    s, d = x.shape

    def body(x_ref, g_ref, h_ref):
        xv = x_ref[...]
        r = lax.rsqrt(jnp.mean(xv * xv, axis=-1, keepdims=True) + RMS_EPS)
        h_ref[...] = (xv * r * g_ref[...]).astype(BF16)

    return pl.pallas_call(
        body,
        name=name,
        grid=(s // ROW_TILE,),
        in_specs=[pl.BlockSpec((ROW_TILE, d), lambda i: (i, 0)), pl.BlockSpec((1, d), lambda i: (0, 0))],
        out_specs=pl.BlockSpec((ROW_TILE, d), lambda i: (i, 0)),
        out_shape=jax.ShapeDtypeStruct((s, d), BF16),
        compiler_params=_cparams(("parallel",)),
    )(x, g)


def _rms_bwd(dh, x, g, dres, name):
    s, d = x.shape

    def body(dh_ref, x_ref, g_ref, dres_ref, dx_ref, dg_ref):
        i = pl.program_id(0)
        xv = x_ref[...]
        r = lax.rsqrt(jnp.mean(xv * xv, axis=-1, keepdims=True) + RMS_EPS)
        xh = xv * r
        dhv = dh_ref[...]
        dxh = dhv * g_ref[...]
        dx = r * (dxh - xh * jnp.mean(dxh * xh, axis=-1, keepdims=True))
        dx_ref[...] = dres_ref[...] + dx
        part = jnp.sum((dhv * xh).reshape(ROW_TILE // 8, 8, d), axis=0)

        @pl.when(i == 0)
        def _():
            dg_ref[...] = part

        @pl.when(i > 0)
        def _():
            dg_ref[...] += part

    row = pl.BlockSpec((ROW_TILE, d), lambda i: (i, 0))
    return pl.pallas_call(
        body,
        name=name,
        grid=(s // ROW_TILE,),
        in_specs=[row, row, pl.BlockSpec((1, d), lambda i: (0, 0)), row],
        out_specs=[row, pl.BlockSpec((8, d), lambda i: (0, 0))],
        out_shape=[jax.ShapeDtypeStruct((s, d), F32), jax.ShapeDtypeStruct((8, d), F32)],
        compiler_params=_cparams(("arbitrary",)),
    )(dh, x, g, dres)


def _loss_head(x2, g, target):
    s, d = x2.shape

    def body(x_ref, g_ref, t_ref, dx_ref, dg_ref, loss_ref):
        i = pl.program_id(0)
        xv = x_ref[...]
        r = lax.rsqrt(jnp.mean(xv * xv, axis=-1, keepdims=True) + RMS_EPS)
        xh = xv * r
        gv = g_ref[...]
        err = xh * gv - t_ref[...]
        dy = err * (1.0 / d)
        dxh = dy * gv
        dx_ref[...] = r * (dxh - xh * jnp.mean(dxh * xh, axis=-1, keepdims=True))
        part_g = jnp.sum((dy * xh).reshape(ROW_TILE // 8, 8, d), axis=0)
        part_l = (0.5 / d) * jnp.sum((err * err).reshape(ROW_TILE // 8, 8, d), axis=0)

        @pl.when(i == 0)
        def _():
            dg_ref[...] = part_g
            loss_ref[...] = part_l

        @pl.when(i > 0)
        def _():
            dg_ref[...] += part_g
            loss_ref[...] += part_l

    row = pl.BlockSpec((ROW_TILE, d), lambda i: (i, 0))
    acc = pl.BlockSpec((8, d), lambda i: (0, 0))
    return pl.pallas_call(
        body,
        name="loss_head",
        grid=(s // ROW_TILE,),
        in_specs=[row, pl.BlockSpec((1, d), lambda i: (0, 0)), row],
        out_specs=[row, acc, acc],
        out_shape=[jax.ShapeDtypeStruct((s, d), F32), jax.ShapeDtypeStruct((8, d), F32),
                   jax.ShapeDtypeStruct((8, d), F32)],
        compiler_params=_cparams(("arbitrary",)),
    )(x2, g, target)


def _alibi_slopes():
    return np.exp2(np.float32(-8.0) * np.arange(1, N_DIL_HEADS + 1, dtype=np.float32) / np.float32(N_DIL_HEADS))


def _head_lane_mask(h, rows):
    lane = lax.broadcasted_iota(jnp.int32, (rows, DIL_W), 1)
    return (lane >= h * HEAD_DIM) & (lane < (h + 1) * HEAD_DIM)


def _band_terms(group, dil):
    qi = lax.broadcasted_iota(jnp.int32, (BLOCK, BLOCK), 0)
    kj = lax.broadcasted_iota(jnp.int32, (BLOCK, BLOCK), 1)
    steps_prev = (qi + BLOCK - kj).astype(F32) * float(dil)
    steps_cur = (qi - kj).astype(F32) * float(dil)
    return kj >= qi, kj <= qi, steps_prev, steps_cur


def _dil_fwd(qkv, group):
    _, dil = DIL_GROUPS[group]
    s = qkv.shape[0]
    sub = s // dil
    nb = sub // BLOCK
    view = qkv.reshape(sub, dil * QKV_W)
    wblk = QKV_W // DIL_W
    slopes = _alibi_slopes()[group * DIL_HEADS:(group + 1) * DIL_HEADS]

    def col(which):
        return lambda r, n: (n, r * wblk + which * 3 + group)

    def col_prev(which):
        return lambda r, n: (jnp.maximum(n - 1, 0), r * wblk + which * 3 + group)

    def body(q_ref, kc_ref, kp_ref, vc_ref, vp_ref, o_ref, lse_ref):
        n = pl.program_id(1)
        valid_p, valid_c, dist_p, dist_c = _band_terms(group, dil)
        valid_p = valid_p & (n > 0)
        q = q_ref[...]
        kc, kp, vc, vp = kc_ref[...], kp_ref[...], vc_ref[...], vp_ref[...]
        o_acc = jnp.zeros((BLOCK, DIL_W), F32)
        lse_acc = jnp.zeros((BLOCK, DIL_W), F32)
        for h in range(DIL_HEADS):
            hm = _head_lane_mask(h, BLOCK)
            qh = jnp.where(hm, q, jnp.zeros_like(q))
            lp = _dot_nt(qh, kp) * 0.125 - float(slopes[h]) * dist_p
            lc = _dot_nt(qh, kc) * 0.125 - float(slopes[h]) * dist_c
            lp = jnp.where(valid_p, lp, NEG_INF)
            lc = jnp.where(valid_c, lc, NEG_INF)
            mx = jnp.maximum(jnp.max(lp, axis=1, keepdims=True), jnp.max(lc, axis=1, keepdims=True))
            den = jnp.sum(jnp.exp(lp - mx), axis=1, keepdims=True) + jnp.sum(jnp.exp(lc - mx), axis=1, keepdims=True)
            lse = mx + jnp.log(den)
            pp = jnp.exp(lp - lse).astype(BF16)
            pc = jnp.exp(lc - lse).astype(BF16)
            oh = _dot_nn(pp, vp) + _dot_nn(pc, vc)
            o_acc = jnp.where(hm, oh, o_acc)
            lse_acc = jnp.where(hm, lse, lse_acc)
        o_ref[...] = o_acc
        lse_ref[...] = lse_acc

    blk = (BLOCK, DIL_W)
    o, lse = pl.pallas_call(
        body,
        name=f"dil_fwd_g{group}",
        grid=(dil, nb),
        in_specs=[pl.BlockSpec(blk, col(0)), pl.BlockSpec(blk, col(1)), pl.BlockSpec(blk, col_prev(1)),
                  pl.BlockSpec(blk, col(2)), pl.BlockSpec(blk, col_prev(2))],
        out_specs=[pl.BlockSpec(blk, lambda r, n: (n, r))] * 2,
        out_shape=[jax.ShapeDtypeStruct((sub, dil * DIL_W), F32)] * 2,
        compiler_params=_cparams(("parallel", "parallel")),
    )(view, view, view, view, view)
    return o.reshape(s, DIL_W), lse.reshape(s, DIL_W)


def _dil_bwd(qkv, do, lse, cterm, group):
    _, dil = DIL_GROUPS[group]
    s = qkv.shape[0]
    sub = s // dil
    nb = sub // BLOCK
    view = qkv.reshape(sub, dil * QKV_W)
    wblk = QKV_W // DIL_W
    slopes = _alibi_slopes()[group * DIL_HEADS:(group + 1) * DIL_HEADS]
    do_v, lse_v, c_v = (t.reshape(sub, dil * DIL_W) for t in (do, lse, cterm))

    def col(which, shift):
        if shift == 0:
            return lambda r, n: (n, r * wblk + which * 3 + group)
        if shift < 0:
            return lambda r, n: (jnp.maximum(n - 1, 0), r * wblk + which * 3 + group)
        return lambda r, n: (jnp.minimum(n + 1, nb - 1), r * wblk + which * 3 + group)

    def own(shift):
        if shift == 0:
            return lambda r, n: (n, r)
        return lambda r, n: (jnp.minimum(n + 1, nb - 1), r)

    def body(q_ref, qn_ref, kc_ref, kp_ref, vc_ref, vp_ref, do_ref, don_ref, lse_ref, lsen_ref, c_ref, cn_ref,
             dq_ref, dk_ref, dv_ref):
        n = pl.program_id(1)
        valid_p, valid_c, dist_p, dist_c = _band_terms(group, dil)
        has_prev = n > 0
        has_next = n < nb - 1
        q, qn = q_ref[...], qn_ref[...]
        kc, kp, vc, vp = kc_ref[...], kp_ref[...], vc_ref[...], vp_ref[...]
        dov, donv = do_ref[...], don_ref[...]
        lsev, lsenv, cv, cnv = lse_ref[...], lsen_ref[...], c_ref[...], cn_ref[...]
        dq_acc = jnp.zeros((BLOCK, DIL_W), F32)
        dk_acc = jnp.zeros((BLOCK, DIL_W), F32)
        dv_acc = jnp.zeros((BLOCK, DIL_W), F32)

        def head_col(t, hm):
            return jnp.max(jnp.where(hm, t, NEG_INF), axis=1, keepdims=True)

        for h in range(DIL_HEADS):
            hm = _head_lane_mask(h, BLOCK)
            slope = float(slopes[h])

            def pair(qh, k, v, doh, lse_h, c_h, valid, dist):
                logit = _dot_nt(qh, k) * 0.125 - slope * dist
                p = jnp.where(valid, jnp.exp(logit - lse_h), 0.0)
                dp = _dot_nt(doh, v)
                dlog = (p * (dp + c_h) * 0.125).astype(BF16)
                return p.astype(BF16), dlog

            qh = jnp.where(hm, q, jnp.zeros_like(q))
            doh = jnp.where(hm, dov, 0.0).astype(BF16)
            lse_h, c_h = head_col(lsev, hm), head_col(cv, hm)
            _, dlog_p = pair(qh, kp, vp, doh, lse_h, c_h, valid_p & has_prev, dist_p)
            p_c, dlog_c = pair(qh, kc, vc, doh, lse_h, c_h, valid_c, dist_c)
            dq_h = _dot_nn(dlog_p, kp) + _dot_nn(dlog_c, kc)
            dq_acc = jnp.where(hm, dq_h, dq_acc)
            qnh = jnp.where(hm, qn, jnp.zeros_like(qn))
            donh = jnp.where(hm, donv, 0.0).astype(BF16)
            p_n, dlog_n = pair(qnh, kc, vc, donh, head_col(lsenv, hm), head_col(cnv, hm), valid_p & has_next, dist_p)
            dk_acc += _dot_tn(dlog_c, qh) + _dot_tn(dlog_n, qnh)
            dv_acc += _dot_tn(p_c, doh) + _dot_tn(p_n, donh)
        dq_ref[...] = dq_acc
        dk_ref[...] = dk_acc
        dv_ref[...] = dv_acc

    blk = (BLOCK, DIL_W)
    outs = pl.pallas_call(
        body,
        name=f"dil_bwd_g{group}",
        grid=(dil, nb),
        in_specs=[pl.BlockSpec(blk, col(0, 0)), pl.BlockSpec(blk, col(0, 1)),
                  pl.BlockSpec(blk, col(1, 0)), pl.BlockSpec(blk, col(1, -1)),
                  pl.BlockSpec(blk, col(2, 0)), pl.BlockSpec(blk, col(2, -1)),
                  pl.BlockSpec(blk, own(0)), pl.BlockSpec(blk, own(1)),
                  pl.BlockSpec(blk, own(0)), pl.BlockSpec(blk, own(1)),
                  pl.BlockSpec(blk, own(0)), pl.BlockSpec(blk, own(1))],
        out_specs=[pl.BlockSpec(blk, lambda r, n: (n, r))] * 3,
        out_shape=[jax.ShapeDtypeStruct((sub, dil * DIL_W), F32)] * 3,
        compiler_params=_cparams(("parallel", "parallel")),
    )(view, view, view, view, view, view, do_v, do_v, lse_v, lse_v, c_v, c_v)
    return tuple(t.reshape(s, DIL_W) for t in outs)


SB_PAIRS = SB_HEADS // 2
SB_COL0 = (9 * DIL_W) // 128
LOG2E = 1.4426950408889634


def _sb_log_terms(zs):
    e = jnp.exp2(-jnp.abs(zs))
    return -(jnp.maximum(zs, 0.0) + jnp.log(1.0 + e) * LOG2E)


def _sb_consts(nkb):
    row = lax.broadcasted_iota(jnp.int32, (SB_BQ, SB_BK), 0)
    colk = lax.broadcasted_iota(jnp.int32, (SB_BQ, SB_BK), 1)
    rr = lax.broadcasted_iota(jnp.int32, (SB_BK, SB_BK), 0)
    cc = lax.broadcasted_iota(jnp.int32, (SB_BK, SB_BK), 1)
    lane = lax.broadcasted_iota(jnp.int32, (SB_BQ, 128), 1)
    assert 2 * nkb <= 128
    return colk < row, rr, cc, lane < HEAD_DIM, lane


def _split_heads(t):
    first = lax.broadcasted_iota(jnp.int32, t.shape, 1) < HEAD_DIM
    zero = jnp.zeros_like(t)
    return jnp.where(first, t, zero), jnp.where(first, zero, t)


def _sb_fwd(qkv):
    s = qkv.shape[0]
    nq, nkb = s // SB_BQ, s // SB_BK
    zscale = LOG2E / math.sqrt(HEAD_DIM)

    def body(q_ref, k_ref, v_ref, o_ref, carry_ref, zs_scr, a_scr, acc_scr, cl_scr):
        i = pl.program_id(1)
        causal, rr, cc, _, lane = _sb_consts(nkb)
        later = (rr > cc).astype(BF16)
        qh = _split_heads(q_ref[...])

        def rows(j):
            return pl.ds(pl.multiple_of(j * SB_BK, SB_BK), SB_BK)

        def scores_to(slot, j):
            kb = k_ref[rows(j), :]
            for hh in range(2):
                zs_scr[slot, hh] = _dot_nt(qh[hh], kb) * zscale

        def weights(slot, j, masked):
            xs, sums, sufs = [], [], []
            for hh in range(2):
                zs = zs_scr[slot, hh]
                l = _sb_log_terms(zs)
                if masked:
                    l = jnp.where(causal, l, 0.0)
                xs.append(zs + l)
                sums.append(jnp.sum(l, axis=1, keepdims=True))
                sufs.append(_dot_f32_by_01(l, later, 2))
            for hh in range(2):
                cl = cl_scr[hh]
                a = jnp.exp2(xs[hh] + (sufs[hh] + jnp.concatenate([cl, cl], axis=1)))
                if masked:
                    a = jnp.where(causal, a, 0.0)
                a_scr[slot, :, hh * SB_BK:(hh + 1) * SB_BK] = a.astype(BF16)
            for hh in range(2):
                cl = cl_scr[hh]
                carry_ref[0] = jnp.where(lane == j + hh * nkb, cl, carry_ref[0])
                cl_scr[hh] = cl + sums[hh]

        def add_av(slot, j):
            v0, v1 = _split_heads(v_ref[rows(j), :])
            acc_scr[...] += _dot_nn(a_scr[slot], jnp.concatenate([v0, v1], axis=0))

        acc_scr[...] = jnp.zeros_like(acc_scr)
        cl_scr[...] = jnp.zeros_like(cl_scr)
        carry_ref[...] = jnp.zeros_like(carry_ref)
        scores_to(0, i)
        scores_to(1, jnp.maximum(i - 1, 0))
        weights(0, i, True)

        def step(j, prev, cur):
            scores_to(prev, jnp.maximum(j - 1, 0))
            add_av(prev, j + 1)
            weights(cur, j, False)

        def two_steps(u, _):
            j = i - 1 - 2 * u
            step(j, 0, 1)
            step(j - 1, 1, 0)
            return 0

        lax.fori_loop(0, i // 2, two_steps, 0)

        @pl.when(i % 2 == 1)
        def _():
            step(0, 0, 1)
            add_av(1, 0)

        @pl.when(i % 2 == 0)
        def _():
            add_av(0, 0)

        o_ref[...] = acc_scr[...]

    def full(which):
        return pl.BlockSpec((s, 128), lambda p, i: (0, SB_COL0 + 4 * which + p))

    return pl.pallas_call(
        body,
        name="sb_fwd",
        grid=(SB_PAIRS, nq),
        in_specs=[pl.BlockSpec((SB_BQ, 128), lambda p, i: (i, SB_COL0 + p)), full(1), full(2)],
        out_specs=[pl.BlockSpec((SB_BQ, 128), lambda p, i: (i, p)),
                   pl.BlockSpec((1, SB_BQ, 128), lambda p, i: (p, i, 0))],
        out_shape=[jax.ShapeDtypeStruct((s, SB_W), F32), jax.ShapeDtypeStruct((SB_PAIRS, s, 128), F32)],
        scratch_shapes=[pltpu.VMEM((2, 2, SB_BQ, SB_BK), F32), pltpu.VMEM((2, SB_BQ, 2 * SB_BK), BF16),
                        pltpu.VMEM((SB_BQ, 128), F32), pltpu.VMEM((2, SB_BQ, 128), F32)],
        compiler_params=_cparams(("parallel", "parallel")),
    )(qkv, qkv, qkv)


def _sb_bwd(qkv, do, carries):
    s = qkv.shape[0]
    nq, nkb = s // SB_BQ, s // SB_BK
    scale = 1.0 / math.sqrt(HEAD_DIM)
    zscale = LOG2E * scale

    def body(q_ref, k_ref, v_ref, do_ref, carry_ref, dq_ref, dk_ref, dv_ref, zs_scr, da_scr, dz_scr, a_scr, cg_scr):
        i = pl.program_id(1)

        @pl.when(i == 0)
        def _():
            dk_ref[...] = jnp.zeros_like(dk_ref)
            dv_ref[...] = jnp.zeros_like(dv_ref)

        causal, rr, cc, first, lane = _sb_consts(nkb)
        later = (rr > cc).astype(BF16)
        earlier = (rr < cc).astype(BF16)
        q2 = q_ref[...]
        qh = _split_heads(q2)
        do2 = do_ref[...].astype(BF16)
        doh = _split_heads(do2)
        ctile = carry_ref[0]

        def rows(j):
            return pl.ds(pl.multiple_of(j * SB_BK, SB_BK), SB_BK)

        def products_to(slot, j):
            kb, vb = k_ref[rows(j), :], v_ref[rows(j), :]
            for hh in range(2):
                zs_scr[slot, hh] = _dot_nt(qh[hh], kb) * zscale
                da_scr[slot, hh] = _dot_nt(doh[hh], vb)

        def by_head(t):
            return jnp.where(first, t[:SB_BK], t[SB_BK:])

        def apply(slot, j):
            k0, k1 = _split_heads(k_ref[rows(j), :])
            dq_ref[...] += _dot_nn(dz_scr[slot], jnp.concatenate([k0, k1], axis=0))
            dk_ref[rows(j), :] += by_head(_dot_tn(dz_scr[slot], q2))
            dv_ref[rows(j), :] += by_head(_dot_tn(a_scr[slot], do2))

        def grads(slot, j, masked):
            xs, sigs, sufs = [], [], []
            for hh in range(2):
                zs = zs_scr[slot, hh]
                l = _sb_log_terms(zs)
                sigs.append(jnp.exp2(zs + l))
                if masked:
                    l = jnp.where(causal, l, 0.0)
                xs.append(zs + l)
                sufs.append(_dot_f32_by_01(l, later, 2))
            gs, gpres = [], []
            for hh in range(2):
                cl = jnp.sum(jnp.where(lane == j + hh * nkb, ctile, 0.0), axis=1, keepdims=True)
                a = jnp.exp2(xs[hh] + (sufs[hh] + cl))
                if masked:
                    a = jnp.where(causal, a, 0.0)
                g = a * da_scr[slot, hh]
                a_scr[slot, :, hh * SB_BK:(hh + 1) * SB_BK] = a.astype(BF16)
                gs.append(g)
                gpres.append(_dot_f32_by_01(g, earlier, 2))
            for hh in range(2):
                cg = cg_scr[hh]
                dz = gs[hh] - (gs[hh] + (gpres[hh] + jnp.concatenate([cg, cg], axis=1))) * sigs[hh]
                if masked:
                    dz = jnp.where(causal, dz, 0.0)
                dz_scr[slot, :, hh * SB_BK:(hh + 1) * SB_BK] = (dz * scale).astype(BF16)
                cg_scr[hh] = cg + jnp.sum(gs[hh], axis=1, keepdims=True)

        dq_ref[...] = jnp.zeros_like(dq_ref)
        cg_scr[...] = jnp.zeros_like(cg_scr)
        dz_scr[1] = jnp.zeros((SB_BQ, 2 * SB_BK), BF16)
        a_scr[1] = jnp.zeros((SB_BQ, 2 * SB_BK), BF16)
        products_to(0, 0)

        def step(j, cur, nxt):
            products_to(nxt, j + 1)
            apply(nxt, jnp.maximum(j - 1, 0))
            grads(cur, j, False)

        def two_steps(u, _):
            step(2 * u, 0, 1)
            step(2 * u + 1, 1, 0)
            return 0

        lax.fori_loop(0, i // 2, two_steps, 0)

        def last(cur, nxt):
            apply(nxt, jnp.maximum(i - 1, 0))
            grads(cur, i, True)
            apply(cur, i)

        @pl.when(i % 2 == 1)
        def _():
            step(i - 1, 0, 1)
            last(1, 0)

        @pl.when(i % 2 == 0)
        def _():
            last(0, 1)

    def full(which):
        return pl.BlockSpec((s, 128), lambda p, i: (0, SB_COL0 + 4 * which + p))

    qblk = pl.BlockSpec((SB_BQ, 128), lambda p, i: (i, p))
    acc = pl.BlockSpec((s, 128), lambda p, i: (0, p))
    return pl.pallas_call(
        body,
        name="sb_bwd",
        grid=(SB_PAIRS, nq),
        in_specs=[pl.BlockSpec((SB_BQ, 128), lambda p, i: (i, SB_COL0 + p)), full(1), full(2), qblk,
                  pl.BlockSpec((1, SB_BQ, 128), lambda p, i: (p, i, 0))],
        out_specs=[qblk, acc, acc],
        out_shape=[jax.ShapeDtypeStruct((s, SB_W), F32)] * 3,
        scratch_shapes=[pltpu.VMEM((2, 2, SB_BQ, SB_BK), F32), pltpu.VMEM((2, 2, SB_BQ, SB_BK), F32),
                        pltpu.VMEM((2, SB_BQ, 2 * SB_BK), BF16), pltpu.VMEM((2, SB_BQ, 2 * SB_BK), BF16),
                        pltpu.VMEM((2, SB_BQ, 128), F32)],
        compiler_params=_cparams(("parallel", "arbitrary")),
    )(qkv, qkv, qkv, do, carries)


MERGE_TILE = 256


def _group_mix(lses):
    mx = jnp.maximum(jnp.maximum(lses[0], lses[1]), lses[2])
    es = [jnp.exp(t - mx) for t in lses]
    den = es[0] + es[1] + es[2]
    return [e / den for e in es]


def _merge_fwd(o_groups, lse_groups, o_sb, gl, b_gate, w_up_dil, w_up_sb):
    s = gl.shape[0]
    t = MERGE_TILE

    def body(o0, o1, o2, l0, l1, l2, ob_ref, gl_ref, bg_ref, wd_ref, ws_ref, merged_ref, oa_ref):
        w = _group_mix([l0[...], l1[...], l2[...]])
        oa = (w[0] * o0[...] + w[1] * o1[...] + w[2] * o2[...]).astype(BF16)
        ua = _dot_nn(oa, wd_ref[...])
        ub = _dot_nn(ob_ref[...].astype(BF16), ws_ref[...])
        gate = jax.nn.sigmoid(gl_ref[...] + bg_ref[...])
        merged_ref[...] = (gate[:, :D_MODEL] * ua + gate[:, D_MODEL:] * ub).astype(BF16)
        oa_ref[...] = oa

    dil = pl.BlockSpec((t, DIL_W), lambda i: (i, 0))
    const = lambda shape: pl.BlockSpec(shape, lambda i: (0, 0))
    return pl.pallas_call(
        body,
        name="merge_fwd",
        grid=(s // t,),
        in_specs=[dil] * 6 + [pl.BlockSpec((t, SB_W), lambda i: (i, 0)), pl.BlockSpec((t, GATE_W), lambda i: (i, 0)),
                              const((1, GATE_W)), const((DIL_W, D_MODEL)), const((SB_W, D_MODEL))],
        out_specs=[pl.BlockSpec((t, D_MODEL), lambda i: (i, 0)), dil],
        out_shape=[jax.ShapeDtypeStruct((s, D_MODEL), BF16), jax.ShapeDtypeStruct((s, DIL_W), BF16)],
        compiler_params=_cparams(("parallel",)),
    )(*o_groups, *lse_groups, o_sb, gl, b_gate, w_up_dil, w_up_sb)


def _merge_bwd(dmerged, o_groups, lse_groups, o_sb, gl, b_gate, w_up_dil, w_up_sb):
    s = gl.shape[0]
    t = MERGE_TILE

    def body(dm_ref, o0, o1, o2, l0, l1, l2, ob_ref, gl_ref, bg_ref, wd_ref, ws_ref,
             dua_ref, dub_ref, dgl_ref, dbg_ref, dosb_ref, d0, d1, d2, c0, c1, c2):
        i = pl.program_id(0)
        og = [o0[...], o1[...], o2[...]]
        w = _group_mix([l0[...], l1[...], l2[...]])
        oa = (w[0] * og[0] + w[1] * og[1] + w[2] * og[2]).astype(BF16)
        ua = _dot_nn(oa, wd_ref[...])
        ub = _dot_nn(ob_ref[...].astype(BF16), ws_ref[...])
        gate = jax.nn.sigmoid(gl_ref[...] + bg_ref[...])
        ga, gb = gate[:, :D_MODEL], gate[:, D_MODEL:]
        dm = dm_ref[...]
        dua = (dm * ga).astype(BF16)
        dub = (dm * gb).astype(BF16)
        dua_ref[...] = dua
        dub_ref[...] = dub
        dgl_a = dm * ua * ga * (1.0 - ga)
        dgl_b = dm * ub * gb * (1.0 - gb)
        dgl_ref[:, :D_MODEL] = dgl_a.astype(BF16)
        dgl_ref[:, D_MODEL:] = dgl_b.astype(BF16)
        part = jnp.concatenate([jnp.sum(dgl_a.reshape(t // 8, 8, D_MODEL), axis=0),
                                jnp.sum(dgl_b.reshape(t // 8, 8, D_MODEL), axis=0)], axis=1)

        @pl.when(i == 0)
        def _():
            dbg_ref[...] = part

        @pl.when(i > 0)
        def _():
            dbg_ref[...] += part

        dosb_ref[...] = _dot_nt(dub, ws_ref[...])
        doa = _dot_nt(dua, wd_ref[...])
        rr = lax.broadcasted_iota(jnp.int32, (DIL_W, DIL_W), 0) // HEAD_DIM
        cc = lax.broadcasted_iota(jnp.int32, (DIL_W, DIL_W), 1) // HEAD_DIM
        same_head = (rr == cc).astype(BF16)
        dw = [_dot_f32_by_01(doa * og[g], same_head) for g in range(3)]
        mean_dw = w[0] * dw[0] + w[1] * dw[1] + w[2] * dw[2]
        for g, (d_ref, c_ref) in enumerate(((d0, c0), (d1, c1), (d2, c2))):
            d_ref[...] = w[g] * doa
            c_ref[...] = -w[g] * mean_dw

    dil = pl.BlockSpec((t, DIL_W), lambda i: (i, 0))
    wide = pl.BlockSpec((t, D_MODEL), lambda i: (i, 0))
    gate2 = pl.BlockSpec((t, GATE_W), lambda i: (i, 0))
    sbw = pl.BlockSpec((t, SB_W), lambda i: (i, 0))
    const = lambda shape: pl.BlockSpec(shape, lambda i: (0, 0))
    return pl.pallas_call(
        body,
        name="merge_bwd",
        grid=(s // t,),
        in_specs=[wide] + [dil] * 6 + [sbw, gate2, const((1, GATE_W)), const((DIL_W, D_MODEL)), const((SB_W, D_MODEL))],
        out_specs=[wide, wide, gate2, const((8, GATE_W)), sbw] + [dil] * 6,
        out_shape=[jax.ShapeDtypeStruct((s, D_MODEL), BF16), jax.ShapeDtypeStruct((s, D_MODEL), BF16),
                   jax.ShapeDtypeStruct((s, GATE_W), BF16), jax.ShapeDtypeStruct((8, GATE_W), F32),
                   jax.ShapeDtypeStruct((s, SB_W), F32)] + [jax.ShapeDtypeStruct((s, DIL_W), F32)] * 6,
        compiler_params=_cparams(("arbitrary",)),
    )(dmerged, *o_groups, *lse_groups, o_sb, gl, b_gate, w_up_dil, w_up_sb)


ANY = pl.BlockSpec(memory_space=pl.ANY)


def _place():
    x, y, c = lax.axis_index("x"), lax.axis_index("y"), lax.axis_index("c")
    other_chips = [(1 - x, y), (x, 1 - y), (1 - x, 1 - y)]
    return x, y, c, other_chips


def _all_gather_weights(pack):
    r, wd = pack.shape
    rh = r // 2

    def body(p_ref, out_ref, send_sems, recv_sems, local_sem):
        x, y, c, chips = _place()
        me, sibling = 2 * x + y, (x, y, 1 - c)

        def half(chip_idx, core):
            return out_ref.at[chip_idx, pl.ds(core * rh, rh), :]

        def copy(k, chip_idx, core, to, src=None):
            return pltpu.make_async_remote_copy(
                src_ref=half(chip_idx, core) if src is None else src, dst_ref=half(chip_idx, core),
                send_sem=send_sems.at[k], recv_sem=recv_sems.at[k], device_id=to, device_id_type=MESH)

        mine = pltpu.make_async_copy(p_ref, out_ref.at[me], local_sem)
        mine.start()
        first = [copy(j, me, c, (*chip, c), src=p_ref.at[pl.ds(c * rh, rh), :]) for j, chip in enumerate(chips)]
        for cp in first:
            cp.start()
        passed = [copy(3 + j, 2 * chip[0] + chip[1], c, sibling) for j, chip in enumerate(chips)]
        for j, chip in enumerate(chips):
            copy(j, 2 * chip[0] + chip[1], c, (x, y, c)).wait_recv()
            passed[j].start()
        for j, chip in enumerate(chips):
            copy(3 + j, 2 * chip[0] + chip[1], 1 - c, (x, y, c)).wait_recv()
        for cp in first + passed:
            cp.wait_send()
        mine.wait()

    return pl.pallas_call(
        body,
        name="all_gather_weights",
        in_specs=[ANY],
        out_specs=ANY,
        out_shape=jax.ShapeDtypeStruct((N_CHIPS, r, wd), pack.dtype),
        scratch_shapes=[pltpu.SemaphoreType.DMA((6,)), pltpu.SemaphoreType.DMA((6,)), pltpu.SemaphoreType.DMA],
    )(pack)


def _swap_halves(g):
    n, r, wd = g.shape
    rh = r // 2

    def body(g_ref, out_ref, send_sem, recv_sem):
        x, y, c, _ = _place()
        cp = pltpu.make_async_remote_copy(
            src_ref=g_ref.at[:, pl.ds((1 - c) * rh, rh), :], dst_ref=out_ref,
            send_sem=send_sem, recv_sem=recv_sem, device_id=(x, y, 1 - c), device_id_type=MESH)
        cp.start()
        cp.wait()

    return pl.pallas_call(
        body,
        name="grad_swap_halves",
        in_specs=[ANY],
        out_specs=ANY,
        out_shape=jax.ShapeDtypeStruct((n, rh, wd), g.dtype),
        scratch_shapes=[pltpu.SemaphoreType.DMA, pltpu.SemaphoreType.DMA],
    )(g)


def _add_halves(g, got, core):
    n, r, wd = g.shape
    rh = r // 2
    t = rh // 4
    nt = rh // t

    def body(c_ref, a_ref, b_ref, o_ref):
        o_ref[...] = a_ref[...] + b_ref[...]

    grid_spec = pltpu.PrefetchScalarGridSpec(
        num_scalar_prefetch=1,
        grid=(n, nt),
        in_specs=[pl.BlockSpec((1, t, wd), lambda s, i, c: (s, c[0] * nt + i, 0)),
                  pl.BlockSpec((1, t, wd), lambda s, i, c: (s, i, 0))],
        out_specs=pl.BlockSpec((1, t, wd), lambda s, i, c: (s, i, 0)),
    )
    return pl.pallas_call(
        body,
        name="grad_add_halves",
        grid_spec=grid_spec,
        out_shape=jax.ShapeDtypeStruct((n, rh, wd), F32),
        compiler_params=_cparams(("parallel", "parallel")),
    )(core, g, got)


def _exchange_chunks(h):
    n, rh, wd = h.shape

    def body(h_ref, out_ref, send_sems, recv_sems, local_sem):
        x, y, c, chips = _place()
        me = 2 * x + y
        mine = pltpu.make_async_copy(h_ref.at[me], out_ref.at[me], local_sem)
        mine.start()
        sends = []
        for j, chip in enumerate(chips):
            them = 2 * chip[0] + chip[1]
            sends.append(pltpu.make_async_remote_copy(
                src_ref=h_ref.at[them], dst_ref=out_ref.at[me],
                send_sem=send_sems.at[j], recv_sem=recv_sems.at[j], device_id=(*chip, c), device_id_type=MESH))
        for cp in sends:
            cp.start()
        for j, chip in enumerate(chips):
            them = 2 * chip[0] + chip[1]
            pltpu.make_async_remote_copy(
                src_ref=h_ref.at[them], dst_ref=out_ref.at[them],
                send_sem=send_sems.at[j], recv_sem=recv_sems.at[j], device_id=(*chip, c), device_id_type=MESH).wait_recv()
        for cp in sends:
            cp.wait_send()
        mine.wait()

    return pl.pallas_call(
        body,
        name="grad_exchange_chunks",
        in_specs=[ANY],
        out_specs=ANY,
        out_shape=jax.ShapeDtypeStruct((n, rh, wd), h.dtype),
        scratch_shapes=[pltpu.SemaphoreType.DMA((3,)), pltpu.SemaphoreType.DMA((3,)), pltpu.SemaphoreType.DMA],
    )(h)


def _sum_chips(b):
    n, rh, wd = b.shape
    t = rh // 4

    def body(b_ref, o_ref):
        o_ref[...] = ((b_ref[0] + b_ref[1]) + b_ref[2]) + b_ref[3]

    return pl.pallas_call(
        body,
        name="grad_sum_chips",
        grid=(rh // t,),
        in_specs=[pl.BlockSpec((n, t, wd), lambda i: (0, i, 0))],
        out_specs=pl.BlockSpec((t, wd), lambda i: (i, 0)),
        out_shape=jax.ShapeDtypeStruct((rh, wd), F32),
        compiler_params=_cparams(("parallel",)),
    )(b)


def _join_halves(tc):
    rh, wd = tc.shape

    def body(t_ref, out_ref, send_sem, recv_sem, local_sem):
        x, y, c, _ = _place()
        mine = pltpu.make_async_copy(t_ref, out_ref.at[pl.ds(c * rh, rh), :], local_sem)
        mine.start()
        cp = pltpu.make_async_remote_copy(
            src_ref=t_ref, dst_ref=out_ref.at[pl.ds(c * rh, rh), :],
            send_sem=send_sem, recv_sem=recv_sem, device_id=(x, y, 1 - c), device_id_type=MESH)
        cp.start()
        cp.wait()
        mine.wait()

    return pl.pallas_call(
        body,
        name="grad_join_halves",
        in_specs=[ANY],
        out_specs=ANY,
        out_shape=jax.ShapeDtypeStruct((2 * rh, wd), tc.dtype),
        scratch_shapes=[pltpu.SemaphoreType.DMA, pltpu.SemaphoreType.DMA, pltpu.SemaphoreType.DMA],
    )(tc)


def _all_reduce_small(pack):
    rows, lanes = pack.shape

    def body(p_ref, out_ref, buf, send_sems, recv_sems):
        x, y, c, _ = _place()
        me = 4 * x + 2 * y + c
        buf[me] = p_ref[...]
        sends = []
        for k in range(1, N_DEV):
            peer = (x ^ (k >> 2), y ^ ((k >> 1) & 1), c ^ (k & 1))
            sends.append(pltpu.make_async_remote_copy(
                src_ref=p_ref, dst_ref=buf.at[me], send_sem=send_sems.at[k - 1], recv_sem=recv_sems.at[k - 1],
                device_id=peer, device_id_type=MESH))
        for cp in sends:
            cp.start()
        for k in range(1, N_DEV):
            pltpu.make_async_remote_copy(
                src_ref=p_ref, dst_ref=buf.at[me ^ k], send_sem=send_sems.at[k - 1], recv_sem=recv_sems.at[k - 1],
                device_id=(x, y, c), device_id_type=MESH).wait_recv()
        for cp in sends:
            cp.wait_send()
        total = buf[0]
        for d in range(1, N_DEV):
            total = total + buf[d]
        out_ref[...] = total

    vm = pl.BlockSpec(memory_space=pltpu.VMEM)
    return pl.pallas_call(
        body,
        name="all_reduce_small",
        in_specs=[vm],
        out_specs=vm,
        out_shape=jax.ShapeDtypeStruct((rows, lanes), F32),
        scratch_shapes=[pltpu.VMEM((N_DEV, rows, lanes), F32), pltpu.SemaphoreType.DMA((N_DEV - 1,)),
                        pltpu.SemaphoreType.DMA((N_DEV - 1,))],
    )(pack)


def _adamw(g, w, m, v, name):
    rows, cols = g.shape
    t = rows
    for cand in (256, 128, 64, 32, 16, 8):
        if rows % cand == 0:
            t = cand
            break

    def body(g_ref, w_ref, m_ref, v_ref, d_ref, nm_ref, nv_ref):
        gv = g_ref[...]
        mv = ADAM_B1 * m_ref[...] + (1.0 - ADAM_B1) * gv
        vv = ADAM_B2 * v_ref[...] + (1.0 - ADAM_B2) * (gv * gv)
        m_hat = mv / (1.0 - ADAM_B1 ** ADAM_STEP)
        v_hat = vv / (1.0 - ADAM_B2 ** ADAM_STEP)
        d_ref[...] = -ADAM_LR * (m_hat / (jnp.sqrt(v_hat) + ADAM_EPS) + ADAM_WD * w_ref[...])
        nm_ref[...] = mv
        nv_ref[...] = vv

    blk = pl.BlockSpec((t, cols), lambda i: (i, 0))
    return pl.pallas_call(
        body,
        name=name,
        grid=(rows // t,),
        in_specs=[blk] * 4,
        out_specs=[blk] * 3,
        out_shape=[jax.ShapeDtypeStruct((rows, cols), F32)] * 3,
        compiler_params=_cparams(("parallel",)),
    )(g, w, m, v)


PACK_W = 1024
BIG = (("w_in", (D_MODEL, IN_COLS), 1), ("w_up_dil", (DIL_W, D_MODEL), 1), ("w_up_sb", (SB_W, D_MODEL), 1),
       ("w_out", (D_MODEL, D_MODEL), 0), ("w_mlp_in", (D_MODEL, D_FF), 1), ("w_mlp_out", (D_FF, D_MODEL), 0))


def _shard_shape(shape, axis):
    return tuple(d // N_CHIPS if a == axis else d for a, d in enumerate(shape))


def _pack_rows():
    rows, at = {}, 0
    for name, shape, axis in BIG:
        n = math.prod(_shard_shape(shape, axis)) // PACK_W
        rows[name] = (at, n)
        at += n
    return rows, at


def _pack_shards(shards):
    return jnp.concatenate([shards[name].reshape(-1, PACK_W) for name, _, _ in BIG], axis=0)


def _unpack_full(gathered):
    rows, _ = _pack_rows()
    full = {}
    for name, shape, axis in BIG:
        at, n = rows[name]
        parts = gathered[:, at:at + n, :].reshape((N_CHIPS,) + _shard_shape(shape, axis))
        if axis == 0:
            full[name] = parts.reshape(shape)
        else:
            full[name] = jnp.transpose(parts, (1, 0, 2)).reshape(shape)
    return full


def _pack_full_grads(grads):
    chunks = []
    for name, shape, axis in BIG:
        g = grads[name]
        if axis == 0:
            parts = g.reshape((N_CHIPS, shape[0] // N_CHIPS, shape[1]))
        else:
            parts = jnp.transpose(g.reshape((shape[0], N_CHIPS, shape[1] // N_CHIPS)), (1, 0, 2))
        chunks.append(parts.reshape(N_CHIPS, -1, PACK_W))
    return jnp.concatenate(chunks, axis=1)


def _unpack_shard(packed):
    rows, _ = _pack_rows()
    return {name: packed[rows[name][0]:rows[name][0] + rows[name][1]].reshape(_shard_shape(shape, axis))
            for name, shape, axis in BIG}


def _local_step(x, target, w, norm_mix_g, b_gate, norm_mlp_g, norm_final_g):
    w_qkv, w_gate = w["w_in"][:, :QKV_W], w["w_in"][:, QKV_W:]

    h = _rms_fwd(x, norm_mix_g, "norm_mix")
    (qkv,) = _matmul(h, w_qkv, mode="nn", out_dtypes=(BF16,), name="proj_qkv", tn=768)
    (gl,) = _matmul(h, w_gate, mode="nn", out_dtypes=(F32,), name="proj_gate")
    dil = [_dil_fwd(qkv, g) for g in range(3)]
    o_groups, lse_groups = [d[0] for d in dil], [d[1] for d in dil]
    o_sb, carries = _sb_fwd(qkv)
    merged, o_a = _merge_fwd(o_groups, lse_groups, o_sb, gl, b_gate, w["w_up_dil"], w["w_up_sb"])
    (x1,) = _matmul(merged, w["w_out"], mode="nn", out_dtypes=(F32,), name="out_proj",
                    extras=(x,), epilogue=lambda acc, res: (res + acc,))
    h2 = _rms_fwd(x1, norm_mlp_g, "norm_mlp")
    u, act = _matmul(h2, w["w_mlp_in"], mode="nn", out_dtypes=(F32, BF16), name="mlp_in",
                     epilogue=lambda acc: (acc, jnp.square(jnp.maximum(acc, 0.0))))
    (x2,) = _matmul(act, w["w_mlp_out"], mode="nn", out_dtypes=(F32,), name="mlp_out", tk=2048,
                    extras=(x1,), epilogue=lambda acc, res: (res + acc,))
    dx2, dg_final, loss_part = _loss_head(x2, norm_final_g.reshape(1, D_MODEL), target)

    (du,) = _matmul(dx2, w["w_mlp_out"], mode="nt", out_dtypes=(BF16,), name="mlp_out_dx",
                    extras=(u,), epilogue=lambda acc, uu: (acc * (2.0 * jnp.maximum(uu, 0.0)),))
    (g_mlp_out,) = _matmul(act, dx2, mode="tn", out_dtypes=(F32,), name="mlp_out_dw")
    (g_mlp_in,) = _matmul(h2, du, mode="tn", out_dtypes=(F32,), name="mlp_in_dw")
    (dh2,) = _matmul(du, w["w_mlp_in"], mode="nt", out_dtypes=(F32,), name="mlp_in_dx", tk=2048)
    dx1, dg_mlp = _rms_bwd(dh2, x1, norm_mlp_g, dx2, "norm_mlp_bwd")

    (dmerged,) = _matmul(dx1, w["w_out"], mode="nt", out_dtypes=(F32,), name="out_proj_dx")
    (g_out,) = _matmul(merged, dx1, mode="tn", out_dtypes=(F32,), name="out_proj_dw")
    mb = _merge_bwd(dmerged, o_groups, lse_groups, o_sb, gl, b_gate, w["w_up_dil"], w["w_up_sb"])
    dua, dub, dgl, dbg, do_sb = mb[:5]
    do_groups, c_groups = mb[5:8], mb[8:11]
    (g_up_dil,) = _matmul(o_a, dua, mode="tn", out_dtypes=(F32,), name="up_dil_dw")
    (g_up_sb,) = _matmul(o_sb, dub, mode="tn", out_dtypes=(F32,), name="up_sb_dw")
    dq_sb, dk_sb, dv_sb = _sb_bwd(qkv, do_sb, carries)
    dil_b = [_dil_bwd(qkv, do_groups[g], lse_groups[g], c_groups[g], g) for g in range(3)]
    dproj = jnp.concatenate(
        [dil_b[g][i].astype(BF16) for i in range(3) for g in range(3)]
        + [t.astype(BF16) for t in (dq_sb, dk_sb, dv_sb)] + [dgl], axis=1)
    (g_in,) = _matmul(h, dproj, mode="tn", out_dtypes=(F32,), name="proj_dw", tn=IN_COLS // 2)
    (dh,) = _matmul(dproj, w["w_in"], mode="nt", out_dtypes=(F32,), name="proj_dx", tk=IN_COLS // 2)
    grad_x, dg_mix = _rms_bwd(dh, x, norm_mix_g, dx1, "norm_mix_bwd")

    big = {"w_in": g_in, "w_up_dil": g_up_dil, "w_up_sb": g_up_sb, "w_out": g_out,
           "w_mlp_in": g_mlp_in, "w_mlp_out": g_mlp_out}
    small = (dg_mix, dbg, dg_mlp, dg_final, loss_part)
    return grad_x, big, small


def kernel(x, norm_mix_g, w_in, b_gate, w_up_dil, w_up_sb, w_out, norm_mlp_g, w_mlp_in, w_mlp_out, norm_final_g, loss_target, m_norm_mix_g, m_w_in, m_b_gate, m_w_up_dil, m_w_up_sb, m_w_out, m_norm_mlp_g, m_w_mlp_in, m_w_mlp_out, m_norm_final_g, v_norm_mix_g, v_w_in, v_b_gate, v_w_up_dil, v_w_up_sb, v_w_out, v_norm_mlp_g, v_w_mlp_in, v_w_mlp_out, v_norm_final_g):
    shards = {"w_in": w_in[0], "w_up_dil": w_up_dil[0], "w_up_sb": w_up_sb[0], "w_out": w_out[0],
              "w_mlp_in": w_mlp_in[0], "w_mlp_out": w_mlp_out[0]}
    moments_m = {"w_in": m_w_in[0], "w_up_dil": m_w_up_dil[0], "w_up_sb": m_w_up_sb[0], "w_out": m_w_out[0],
                 "w_mlp_in": m_w_mlp_in[0], "w_mlp_out": m_w_mlp_out[0]}
    moments_v = {"w_in": v_w_in[0], "w_up_dil": v_w_up_dil[0], "w_up_sb": v_w_up_sb[0], "w_out": v_w_out[0],
                 "w_mlp_in": v_w_mlp_in[0], "w_mlp_out": v_w_mlp_out[0]}

    pack = _pack_shards({n: s.astype(BF16) for n, s in shards.items()})
    full = _unpack_full(_all_gather_weights(pack))

    grad_x, big, small = _local_step(x[0], loss_target[0], full, norm_mix_g, b_gate, norm_mlp_g, norm_final_g)

    core = lax.axis_index("c").astype(jnp.int32).reshape(1)
    gpack = _pack_full_grads(big)
    chip_sum = _add_halves(gpack, _swap_halves(gpack), core)
    reduced = _join_halves(_sum_chips(_exchange_chunks(chip_sum)))
    g_shard = _unpack_shard(reduced)

    dg_mix, dbg, dg_mlp, dg_final, loss_part = small
    loss_row = jnp.sum(loss_part, axis=0, keepdims=True)
    small_pack = jnp.concatenate(
        [jnp.sum(dg_mix, axis=0, keepdims=True), jnp.sum(dbg, axis=0, keepdims=True),
         jnp.sum(dg_mlp, axis=0, keepdims=True), jnp.sum(dg_final, axis=0, keepdims=True), loss_row], axis=1)
    n_small = small_pack.shape[1]
    small_sum = _all_reduce_small(small_pack.reshape(n_small // 128, 128)).reshape(1, n_small)
    g_norm_mix = small_sum[:, :D_MODEL]
    g_b_gate = small_sum[:, D_MODEL:3 * D_MODEL]
    g_norm_mlp = small_sum[:, 3 * D_MODEL:4 * D_MODEL]
    g_norm_final = small_sum[:, 4 * D_MODEL:5 * D_MODEL]
    loss = jnp.sum(small_sum[:, 5 * D_MODEL:])

    names = ["norm_mix_g", "w_in", "b_gate", "w_up_dil", "w_up_sb", "w_out", "norm_mlp_g", "w_mlp_in", "w_mlp_out",
             "norm_final_g"]
    grads = dict(g_shard)
    grads.update(norm_mix_g=g_norm_mix, b_gate=g_b_gate, norm_mlp_g=g_norm_mlp, norm_final_g=g_norm_final)
    weights = dict(shards)
    weights.update(norm_mix_g=norm_mix_g, b_gate=b_gate, norm_mlp_g=norm_mlp_g, norm_final_g=norm_final_g.reshape(1, D_MODEL))
    ms = dict(moments_m)
    ms.update(norm_mix_g=m_norm_mix_g, b_gate=m_b_gate, norm_mlp_g=m_norm_mlp_g, norm_final_g=m_norm_final_g.reshape(1, D_MODEL))
    vs = dict(moments_v)
    vs.update(norm_mix_g=v_norm_mix_g, b_gate=v_b_gate, norm_mlp_g=v_norm_mlp_g, norm_final_g=v_norm_final_g.reshape(1, D_MODEL))

    out_shapes = {"norm_mix_g": norm_mix_g.shape, "w_in": w_in.shape, "b_gate": b_gate.shape, "w_up_dil": w_up_dil.shape,
                  "w_up_sb": w_up_sb.shape, "w_out": w_out.shape, "norm_mlp_g": norm_mlp_g.shape,
                  "w_mlp_in": w_mlp_in.shape, "w_mlp_out": w_mlp_out.shape, "norm_final_g": norm_final_g.shape}
    g_out, d_out, m_out, v_out = [], [], [], []
    for n in names:
        d, nm, nv = _adamw(grads[n], weights[n], ms[n], vs[n], "adamw_" + n)
        shape = out_shapes[n]
        g_out.append(grads[n].reshape(shape))
        d_out.append(d.reshape(shape))
        m_out.append(nm.reshape(shape))
        v_out.append(nv.reshape(shape))
    return (loss, grad_x.reshape(x.shape), *g_out, *d_out, *m_out, *v_out)
```

```python
import functools
import math

import jax
import jax.numpy as jnp
import numpy as np
from jax import lax
from jax.experimental import pallas as pl
from jax.experimental.pallas import tpu as pltpu

F32 = jnp.float32
BF16 = jnp.bfloat16
MESH = pl.DeviceIdType.MESH

D_MODEL = 1024
HEAD_DIM = 64
DIL_GROUPS = ((128, 1), (512, 4), (2048, 16))
DIL_HEADS = 4
DIL_W = 256
N_DIL_HEADS = 12
SB_HEADS = 8
SB_W = SB_HEADS * HEAD_DIM
QKV_W = 3 * 3 * DIL_W + 3 * SB_W
GATE_W = 2 * D_MODEL
IN_COLS = QKV_W + GATE_W
D_FF = 4 * D_MODEL
BLOCK = 128
RMS_EPS = 1e-6
NEG_INF = -1e30
N_CHIPS = 4
N_DEV = 8

ADAM_LR = 0.001
ADAM_B1 = 0.9
ADAM_B2 = 0.999
ADAM_EPS = 1e-08
ADAM_WD = 0.01
ADAM_STEP = 10

VMEM_LIMIT = 56 * 1024 * 1024

SB_BQ = 256
SB_BK = 256


def _cparams(sem=None):
    if sem is None:
        return pltpu.CompilerParams(vmem_limit_bytes=VMEM_LIMIT)
    return pltpu.CompilerParams(dimension_semantics=sem, vmem_limit_bytes=VMEM_LIMIT)


def _dot(a, b, dims):
    return lax.dot_general(a, b, (dims, ((), ())), preferred_element_type=F32)


def _dot_nn(a, b):
    return _dot(a, b, ((1,), (0,)))


def _dot_nt(a, b):
    return _dot(a, b, ((1,), (1,)))


def _dot_tn(a, b):
    return _dot(a, b, ((0,), (0,)))


def _dot_f32_by_01(x, m01, pieces=3):
    hi = x.astype(BF16)
    r1 = x - hi.astype(F32)
    mid = r1.astype(BF16)
    if pieces == 2:
        return _dot_nn(hi, m01) + _dot_nn(mid, m01)
    lo = (r1 - mid.astype(F32)).astype(BF16)
    return _dot_nn(hi, m01) + _dot_nn(mid, m01) + _dot_nn(lo, m01)


def _matmul(a, b, *, mode, out_dtypes, name, tm=1024, tn=1024, tk=1024, extras=(), epilogue=None):
    if mode == "nn":
        (m, k), (k2, n) = a.shape, b.shape
    elif mode == "nt":
        (m, k), (n, k2) = a.shape, b.shape
    else:
        (k, m), (k2, n) = a.shape, b.shape
    assert k == k2, (a.shape, b.shape, mode)
    tm, tn, tk = min(tm, m), min(tn, n), min(tk, k)
    assert m % tm == 0 and n % tn == 0 and k % tk == 0, (m, n, k, tm, tn, tk)
    nk = k // tk
    n_out = len(out_dtypes)
    n_ex = len(extras)

    if mode == "nn":
        a_spec = pl.BlockSpec((tm, tk), lambda i, j, kk: (i, kk))
        b_spec = pl.BlockSpec((tk, tn), lambda i, j, kk: (kk, j))
        dot = _dot_nn
    elif mode == "nt":
        a_spec = pl.BlockSpec((tm, tk), lambda i, j, kk: (i, kk))
        b_spec = pl.BlockSpec((tn, tk), lambda i, j, kk: (j, kk))
        dot = _dot_nt
    else:
        a_spec = pl.BlockSpec((tk, tm), lambda i, j, kk: (kk, i))
        b_spec = pl.BlockSpec((tk, tn), lambda i, j, kk: (kk, j))
        dot = _dot_tn
    mn_spec = pl.BlockSpec((tm, tn), lambda i, j, kk: (i, j))

    def body(*refs):
        a_ref, b_ref = refs[0], refs[1]
        ex_refs = refs[2:2 + n_ex]
        out_refs = refs[2 + n_ex:2 + n_ex + n_out]
        acc_ref = refs[2 + n_ex + n_out] if nk > 1 else None
        part = dot(a_ref[...].astype(BF16), b_ref[...].astype(BF16))

        def finish(acc):
            if epilogue is None:
                outs = (acc,)
            else:
                outs = epilogue(acc, *[r[...] for r in ex_refs])
            for o_ref, o in zip(out_refs, outs):
                o_ref[...] = o.astype(o_ref.dtype)

        if nk == 1:
            finish(part)
        else:
            kk = pl.program_id(2)

            @pl.when(kk == 0)
            def _():
                acc_ref[...] = part

            @pl.when(kk > 0)
            def _():
                acc_ref[...] += part

            @pl.when(kk == nk - 1)
            def _():
                finish(acc_ref[...])

    outs = pl.pallas_call(
        body,
        name=name,
        grid=(m // tm, n // tn, nk),
        in_specs=[a_spec, b_spec] + [mn_spec] * n_ex,
        out_specs=[mn_spec] * n_out,
        out_shape=[jax.ShapeDtypeStruct((m, n), dt) for dt in out_dtypes],
        scratch_shapes=[pltpu.VMEM((tm, tn), F32)] if nk > 1 else [],
        compiler_params=_cparams(("parallel", "parallel", "arbitrary")),
    )(a, b, *extras)
    return outs


ROW_TILE = 512


def _rms_fwd(x, g, name):
    s, d = x.shape

    def body(x_ref, g_ref, h_ref):
        xv = x_ref[...]
        r = lax.rsqrt(jnp.mean(xv * xv, axis=-1, keepdims=True) + RMS_EPS)
        h_ref[...] = (xv * r * g_ref[...]).astype(BF16)

    return pl.pallas_call(
        body,
        name=name,
        grid=(s // ROW_TILE,),
        in_specs=[pl.BlockSpec((ROW_TILE, d), lambda i: (i, 0)), pl.BlockSpec((1, d), lambda i: (0, 0))],
        out_specs=pl.BlockSpec((ROW_TILE, d), lambda i: (i, 0)),
        out_shape=jax.ShapeDtypeStruct((s, d), BF16),
        compiler_params=_cparams(("parallel",)),
    )(x, g)


def _rms_bwd(dh, x, g, dres, name):
    s, d = x.shape

    def body(dh_ref, x_ref, g_ref, dres_ref, dx_ref, dg_ref):
        i = pl.program_id(0)
        xv = x_ref[...]
        r = lax.rsqrt(jnp.mean(xv * xv, axis=-1, keepdims=True) + RMS_EPS)
        xh = xv * r
        dhv = dh_ref[...]
        dxh = dhv * g_ref[...]
        dx = r * (dxh - xh * jnp.mean(dxh * xh, axis=-1, keepdims=True))
        dx_ref[...] = dres_ref[...] + dx
        part = jnp.sum((dhv * xh).reshape(ROW_TILE // 8, 8, d), axis=0)

        @pl.when(i == 0)
        def _():
            dg_ref[...] = part

        @pl.when(i > 0)
        def _():
            dg_ref[...] += part

    row = pl.BlockSpec((ROW_TILE, d), lambda i: (i, 0))
    return pl.pallas_call(
        body,
        name=name,
        grid=(s // ROW_TILE,),
        in_specs=[row, row, pl.BlockSpec((1, d), lambda i: (0, 0)), row],
        out_specs=[row, pl.BlockSpec((8, d), lambda i: (0, 0))],
        out_shape=[jax.ShapeDtypeStruct((s, d), F32), jax.ShapeDtypeStruct((8, d), F32)],
        compiler_params=_cparams(("arbitrary",)),
    )(dh, x, g, dres)


def _loss_head(x2, g, target):
    s, d = x2.shape

    def body(x_ref, g_ref, t_ref, dx_ref, dg_ref, loss_ref):
        i = pl.program_id(0)
        xv = x_ref[...]
        r = lax.rsqrt(jnp.mean(xv * xv, axis=-1, keepdims=True) + RMS_EPS)
        xh = xv * r
        gv = g_ref[...]
        err = xh * gv - t_ref[...]
        dy = err * (1.0 / d)
        dxh = dy * gv
        dx_ref[...] = r * (dxh - xh * jnp.mean(dxh * xh, axis=-1, keepdims=True))
        part_g = jnp.sum((dy * xh).reshape(ROW_TILE // 8, 8, d), axis=0)
        part_l = (0.5 / d) * jnp.sum((err * err).reshape(ROW_TILE // 8, 8, d), axis=0)

        @pl.when(i == 0)
        def _():
            dg_ref[...] = part_g
            loss_ref[...] = part_l

        @pl.when(i > 0)
        def _():
            dg_ref[...] += part_g
            loss_ref[...] += part_l

    row = pl.BlockSpec((ROW_TILE, d), lambda i: (i, 0))
    acc = pl.BlockSpec((8, d), lambda i: (0, 0))
    return pl.pallas_call(
        body,
        name="loss_head",
        grid=(s // ROW_TILE,),
        in_specs=[row, pl.BlockSpec((1, d), lambda i: (0, 0)), row],
        out_specs=[row, acc, acc],
        out_shape=[jax.ShapeDtypeStruct((s, d), F32), jax.ShapeDtypeStruct((8, d), F32),
                   jax.ShapeDtypeStruct((8, d), F32)],
        compiler_params=_cparams(("arbitrary",)),
    )(x2, g, target)


def _alibi_slopes():
    return np.exp2(np.float32(-8.0) * np.arange(1, N_DIL_HEADS + 1, dtype=np.float32) / np.float32(N_DIL_HEADS))


def _head_lane_mask(h, rows):
    lane = lax.broadcasted_iota(jnp.int32, (rows, DIL_W), 1)
    return (lane >= h * HEAD_DIM) & (lane < (h + 1) * HEAD_DIM)


def _band_terms(group, dil):
    qi = lax.broadcasted_iota(jnp.int32, (BLOCK, BLOCK), 0)
    kj = lax.broadcasted_iota(jnp.int32, (BLOCK, BLOCK), 1)
    steps_prev = (qi + BLOCK - kj).astype(F32) * float(dil)
    steps_cur = (qi - kj).astype(F32) * float(dil)
    return kj >= qi, kj <= qi, steps_prev, steps_cur


def _dil_fwd(qkv, group):
    _, dil = DIL_GROUPS[group]
    s = qkv.shape[0]
    sub = s // dil
    nb = sub // BLOCK
    view = qkv.reshape(sub, dil * QKV_W)
    wblk = QKV_W // DIL_W
    slopes = _alibi_slopes()[group * DIL_HEADS:(group + 1) * DIL_HEADS]

    def col(which):
        return lambda r, n: (n, r * wblk + which * 3 + group)

    def col_prev(which):
        return lambda r, n: (jnp.maximum(n - 1, 0), r * wblk + which * 3 + group)

    def body(q_ref, kc_ref, kp_ref, vc_ref, vp_ref, o_ref, lse_ref):
        n = pl.program_id(1)
        valid_p, valid_c, dist_p, dist_c = _band_terms(group, dil)
        valid_p = valid_p & (n > 0)
        q = q_ref[...]
        kc, kp, vc, vp = kc_ref[...], kp_ref[...], vc_ref[...], vp_ref[...]
        o_acc = jnp.zeros((BLOCK, DIL_W), F32)
        lse_acc = jnp.zeros((BLOCK, DIL_W), F32)
        for h in range(DIL_HEADS):
            hm = _head_lane_mask(h, BLOCK)
            qh = jnp.where(hm, q, jnp.zeros_like(q))
            lp = _dot_nt(qh, kp) * 0.125 - float(slopes[h]) * dist_p
            lc = _dot_nt(qh, kc) * 0.125 - float(slopes[h]) * dist_c
            lp = jnp.where(valid_p, lp, NEG_INF)
            lc = jnp.where(valid_c, lc, NEG_INF)
            mx = jnp.maximum(jnp.max(lp, axis=1, keepdims=True), jnp.max(lc, axis=1, keepdims=True))
            den = jnp.sum(jnp.exp(lp - mx), axis=1, keepdims=True) + jnp.sum(jnp.exp(lc - mx), axis=1, keepdims=True)
            lse = mx + jnp.log(den)
            pp = jnp.exp(lp - lse).astype(BF16)
            pc = jnp.exp(lc - lse).astype(BF16)
            oh = _dot_nn(pp, vp) + _dot_nn(pc, vc)
            o_acc = jnp.where(hm, oh, o_acc)
            lse_acc = jnp.where(hm, lse, lse_acc)
        o_ref[...] = o_acc
        lse_ref[...] = lse_acc

    blk = (BLOCK, DIL_W)
    o, lse = pl.pallas_call(
        body,
        name=f"dil_fwd_g{group}",
        grid=(dil, nb),
        in_specs=[pl.BlockSpec(blk, col(0)), pl.BlockSpec(blk, col(1)), pl.BlockSpec(blk, col_prev(1)),
                  pl.BlockSpec(blk, col(2)), pl.BlockSpec(blk, col_prev(2))],
        out_specs=[pl.BlockSpec(blk, lambda r, n: (n, r))] * 2,
        out_shape=[jax.ShapeDtypeStruct((sub, dil * DIL_W), F32)] * 2,
        compiler_params=_cparams(("parallel", "parallel")),
    )(view, view, view, view, view)
    return o.reshape(s, DIL_W), lse.reshape(s, DIL_W)


def _dil_bwd(qkv, do, lse, cterm, group):
    _, dil = DIL_GROUPS[group]
    s = qkv.shape[0]
    sub = s // dil
    nb = sub // BLOCK
    view = qkv.reshape(sub, dil * QKV_W)
    wblk = QKV_W // DIL_W
    slopes = _alibi_slopes()[group * DIL_HEADS:(group + 1) * DIL_HEADS]
    do_v, lse_v, c_v = (t.reshape(sub, dil * DIL_W) for t in (do, lse, cterm))

    def col(which, shift):
        if shift == 0:
            return lambda r, n: (n, r * wblk + which * 3 + group)
        if shift < 0:
            return lambda r, n: (jnp.maximum(n - 1, 0), r * wblk + which * 3 + group)
        return lambda r, n: (jnp.minimum(n + 1, nb - 1), r * wblk + which * 3 + group)

    def own(shift):
        if shift == 0:
            return lambda r, n: (n, r)
        return lambda r, n: (jnp.minimum(n + 1, nb - 1), r)

    def body(q_ref, qn_ref, kc_ref, kp_ref, vc_ref, vp_ref, do_ref, don_ref, lse_ref, lsen_ref, c_ref, cn_ref,
             dq_ref, dk_ref, dv_ref):
        n = pl.program_id(1)
        valid_p, valid_c, dist_p, dist_c = _band_terms(group, dil)
        has_prev = n > 0
        has_next = n < nb - 1
        q, qn = q_ref[...], qn_ref[...]
        kc, kp, vc, vp = kc_ref[...], kp_ref[...], vc_ref[...], vp_ref[...]
        dov, donv = do_ref[...], don_ref[...]
        lsev, lsenv, cv, cnv = lse_ref[...], lsen_ref[...], c_ref[...], cn_ref[...]
        dq_acc = jnp.zeros((BLOCK, DIL_W), F32)
        dk_acc = jnp.zeros((BLOCK, DIL_W), F32)
        dv_acc = jnp.zeros((BLOCK, DIL_W), F32)

        def head_col(t, hm):
            return jnp.max(jnp.where(hm, t, NEG_INF), axis=1, keepdims=True)

        for h in range(DIL_HEADS):
            hm = _head_lane_mask(h, BLOCK)
            slope = float(slopes[h])

            def pair(qh, k, v, doh, lse_h, c_h, valid, dist):
                logit = _dot_nt(qh, k) * 0.125 - slope * dist
                p = jnp.where(valid, jnp.exp(logit - lse_h), 0.0)
                dp = _dot_nt(doh, v)
                dlog = (p * (dp + c_h) * 0.125).astype(BF16)
                return p.astype(BF16), dlog

            qh = jnp.where(hm, q, jnp.zeros_like(q))
            doh = jnp.where(hm, dov, 0.0).astype(BF16)
            lse_h, c_h = head_col(lsev, hm), head_col(cv, hm)
            _, dlog_p = pair(qh, kp, vp, doh, lse_h, c_h, valid_p & has_prev, dist_p)
            p_c, dlog_c = pair(qh, kc, vc, doh, lse_h, c_h, valid_c, dist_c)
            dq_h = _dot_nn(dlog_p, kp) + _dot_nn(dlog_c, kc)
            dq_acc = jnp.where(hm, dq_h, dq_acc)
            qnh = jnp.where(hm, qn, jnp.zeros_like(qn))
            donh = jnp.where(hm, donv, 0.0).astype(BF16)
            p_n, dlog_n = pair(qnh, kc, vc, donh, head_col(lsenv, hm), head_col(cnv, hm), valid_p & has_next, dist_p)
            dk_acc += _dot_tn(dlog_c, qh) + _dot_tn(dlog_n, qnh)
            dv_acc += _dot_tn(p_c, doh) + _dot_tn(p_n, donh)
        dq_ref[...] = dq_acc
        dk_ref[...] = dk_acc
        dv_ref[...] = dv_acc

    blk = (BLOCK, DIL_W)
    outs = pl.pallas_call(
        body,
        name=f"dil_bwd_g{group}",
        grid=(dil, nb),
        in_specs=[pl.BlockSpec(blk, col(0, 0)), pl.BlockSpec(blk, col(0, 1)),
                  pl.BlockSpec(blk, col(1, 0)), pl.BlockSpec(blk, col(1, -1)),
                  pl.BlockSpec(blk, col(2, 0)), pl.BlockSpec(blk, col(2, -1)),
                  pl.BlockSpec(blk, own(0)), pl.BlockSpec(blk, own(1)),
                  pl.BlockSpec(blk, own(0)), pl.BlockSpec(blk, own(1)),
                  pl.BlockSpec(blk, own(0)), pl.BlockSpec(blk, own(1))],
        out_specs=[pl.BlockSpec(blk, lambda r, n: (n, r))] * 3,
        out_shape=[jax.ShapeDtypeStruct((sub, dil * DIL_W), F32)] * 3,
        compiler_params=_cparams(("parallel", "parallel")),
    )(view, view, view, view, view, view, do_v, do_v, lse_v, lse_v, c_v, c_v)
    return tuple(t.reshape(s, DIL_W) for t in outs)


SB_PAIRS = SB_HEADS // 2
SB_COL0 = (9 * DIL_W) // 128
LOG2E = 1.4426950408889634


def _sb_log_terms(zs):
    e = jnp.exp2(-jnp.abs(zs))
    return -(jnp.maximum(zs, 0.0) + jnp.log(1.0 + e) * LOG2E)


def _sb_consts(nkb):
    row = lax.broadcasted_iota(jnp.int32, (SB_BQ, SB_BK), 0)
    colk = lax.broadcasted_iota(jnp.int32, (SB_BQ, SB_BK), 1)
    rr = lax.broadcasted_iota(jnp.int32, (SB_BK, SB_BK), 0)
    cc = lax.broadcasted_iota(jnp.int32, (SB_BK, SB_BK), 1)
    lane = lax.broadcasted_iota(jnp.int32, (SB_BQ, 128), 1)
    assert 2 * nkb <= 128
    return colk < row, rr, cc, lane < HEAD_DIM, lane


def _split_heads(t):
    first = lax.broadcasted_iota(jnp.int32, t.shape, 1) < HEAD_DIM
    zero = jnp.zeros_like(t)
    return jnp.where(first, t, zero), jnp.where(first, zero, t)


def _sb_fwd(qkv):
    s = qkv.shape[0]
    nq, nkb = s // SB_BQ, s // SB_BK
    zscale = LOG2E / math.sqrt(HEAD_DIM)

    def body(q_ref, k_ref, v_ref, o_ref, carry_ref, zs_scr, a_scr, acc_scr, cl_scr):
        i = pl.program_id(1)
        causal, rr, cc, _, lane = _sb_consts(nkb)
        later = (rr > cc).astype(BF16)
        qh = _split_heads(q_ref[...])

        def rows(j):
            return pl.ds(pl.multiple_of(j * SB_BK, SB_BK), SB_BK)

        def scores_to(slot, j):
            kb = k_ref[rows(j), :]
            for hh in range(2):
                zs_scr[slot, hh] = _dot_nt(qh[hh], kb) * zscale

        def weights(slot, j, masked):
            xs, sums, sufs = [], [], []
            for hh in range(2):
                zs = zs_scr[slot, hh]
                l = _sb_log_terms(zs)
                if masked:
                    l = jnp.where(causal, l, 0.0)
                xs.append(zs + l)
                sums.append(jnp.sum(l, axis=1, keepdims=True))
                sufs.append(_dot_f32_by_01(l, later, 2))
            for hh in range(2):
                cl = cl_scr[hh]
                a = jnp.exp2(xs[hh] + (sufs[hh] + jnp.concatenate([cl, cl], axis=1)))
                if masked:
                    a = jnp.where(causal, a, 0.0)
                a_scr[slot, :, hh * SB_BK:(hh + 1) * SB_BK] = a.astype(BF16)
            for hh in range(2):
                cl = cl_scr[hh]
                carry_ref[0] = jnp.where(lane == j + hh * nkb, cl, carry_ref[0])
                cl_scr[hh] = cl + sums[hh]

        def add_av(slot, j):
            v0, v1 = _split_heads(v_ref[rows(j), :])
            acc_scr[...] += _dot_nn(a_scr[slot], jnp.concatenate([v0, v1], axis=0))

        acc_scr[...] = jnp.zeros_like(acc_scr)
        cl_scr[...] = jnp.zeros_like(cl_scr)
        carry_ref[...] = jnp.zeros_like(carry_ref)
        scores_to(0, i)
        scores_to(1, jnp.maximum(i - 1, 0))
        weights(0, i, True)

        def step(j, prev, cur):
            scores_to(prev, jnp.maximum(j - 1, 0))
            add_av(prev, j + 1)
            weights(cur, j, False)

        def two_steps(u, _):
            j = i - 1 - 2 * u
            step(j, 0, 1)
            step(j - 1, 1, 0)
            return 0

        lax.fori_loop(0, i // 2, two_steps, 0)

        @pl.when(i % 2 == 1)
        def _():
            step(0, 0, 1)
            add_av(1, 0)

        @pl.when(i % 2 == 0)
        def _():
            add_av(0, 0)

        o_ref[...] = acc_scr[...]

    def full(which):
        return pl.BlockSpec((s, 128), lambda p, i: (0, SB_COL0 + 4 * which + p))

    return pl.pallas_call(
        body,
        name="sb_fwd",
        grid=(SB_PAIRS, nq),
        in_specs=[pl.BlockSpec((SB_BQ, 128), lambda p, i: (i, SB_COL0 + p)), full(1), full(2)],
        out_specs=[pl.BlockSpec((SB_BQ, 128), lambda p, i: (i, p)),
                   pl.BlockSpec((1, SB_BQ, 128), lambda p, i: (p, i, 0))],
        out_shape=[jax.ShapeDtypeStruct((s, SB_W), F32), jax.ShapeDtypeStruct((SB_PAIRS, s, 128), F32)],
        scratch_shapes=[pltpu.VMEM((2, 2, SB_BQ, SB_BK), F32), pltpu.VMEM((2, SB_BQ, 2 * SB_BK), BF16),
                        pltpu.VMEM((SB_BQ, 128), F32), pltpu.VMEM((2, SB_BQ, 128), F32)],
        compiler_params=_cparams(("parallel", "parallel")),
    )(qkv, qkv, qkv)


def _sb_bwd(qkv, do, carries):
    s = qkv.shape[0]
    nq, nkb = s // SB_BQ, s // SB_BK
    scale = 1.0 / math.sqrt(HEAD_DIM)
    zscale = LOG2E * scale

    def body(q_ref, k_ref, v_ref, do_ref, carry_ref, dq_ref, dk_ref, dv_ref, zs_scr, da_scr, dz_scr, a_scr, cg_scr):
        i = pl.program_id(1)

        @pl.when(i == 0)
        def _():
            dk_ref[...] = jnp.zeros_like(dk_ref)
            dv_ref[...] = jnp.zeros_like(dv_ref)

        causal, rr, cc, first, lane = _sb_consts(nkb)
        later = (rr > cc).astype(BF16)
        earlier = (rr < cc).astype(BF16)
        q2 = q_ref[...]
        qh = _split_heads(q2)
        do2 = do_ref[...].astype(BF16)
        doh = _split_heads(do2)
        ctile = carry_ref[0]

        def rows(j):
            return pl.ds(pl.multiple_of(j * SB_BK, SB_BK), SB_BK)

        def products_to(slot, j):
            kb, vb = k_ref[rows(j), :], v_ref[rows(j), :]
            for hh in range(2):
                zs_scr[slot, hh] = _dot_nt(qh[hh], kb) * zscale
                da_scr[slot, hh] = _dot_nt(doh[hh], vb)

        def by_head(t):
            return jnp.where(first, t[:SB_BK], t[SB_BK:])

        def apply(slot, j):
            k0, k1 = _split_heads(k_ref[rows(j), :])
            dq_ref[...] += _dot_nn(dz_scr[slot], jnp.concatenate([k0, k1], axis=0))
            dk_ref[rows(j), :] += by_head(_dot_tn(dz_scr[slot], q2))
            dv_ref[rows(j), :] += by_head(_dot_tn(a_scr[slot], do2))

        def grads(slot, j, masked):
            xs, sigs, sufs = [], [], []
            for hh in range(2):
                zs = zs_scr[slot, hh]
                l = _sb_log_terms(zs)
                sigs.append(jnp.exp2(zs + l))
                if masked:
                    l = jnp.where(causal, l, 0.0)
                xs.append(zs + l)
                sufs.append(_dot_f32_by_01(l, later, 2))
            gs, gpres = [], []
            for hh in range(2):
                cl = jnp.sum(jnp.where(lane == j + hh * nkb, ctile, 0.0), axis=1, keepdims=True)
                a = jnp.exp2(xs[hh] + (sufs[hh] + cl))
                if masked:
                    a = jnp.where(causal, a, 0.0)
                g = a * da_scr[slot, hh]
                a_scr[slot, :, hh * SB_BK:(hh + 1) * SB_BK] = a.astype(BF16)
                gs.append(g)
                gpres.append(_dot_f32_by_01(g, earlier, 2))
            for hh in range(2):
                cg = cg_scr[hh]
                dz = gs[hh] - (gs[hh] + (gpres[hh] + jnp.concatenate([cg, cg], axis=1))) * sigs[hh]
                if masked:
                    dz = jnp.where(causal, dz, 0.0)
                dz_scr[slot, :, hh * SB_BK:(hh + 1) * SB_BK] = (dz * scale).astype(BF16)
                cg_scr[hh] = cg + jnp.sum(gs[hh], axis=1, keepdims=True)

        dq_ref[...] = jnp.zeros_like(dq_ref)
        cg_scr[...] = jnp.zeros_like(cg_scr)
        dz_scr[1] = jnp.zeros((SB_BQ, 2 * SB_BK), BF16)
        a_scr[1] = jnp.zeros((SB_BQ, 2 * SB_BK), BF16)
        products_to(0, 0)

        def step(j, cur, nxt):
            products_to(nxt, j + 1)
            apply(nxt, jnp.maximum(j - 1, 0))
            grads(cur, j, False)

        def two_steps(u, _):
            step(2 * u, 0, 1)
            step(2 * u + 1, 1, 0)
            return 0

        lax.fori_loop(0, i // 2, two_steps, 0)

        def last(cur, nxt):
            apply(nxt, jnp.maximum(i - 1, 0))
            grads(cur, i, True)
            apply(cur, i)

        @pl.when(i % 2 == 1)
        def _():
            step(i - 1, 0, 1)
            last(1, 0)

        @pl.when(i % 2 == 0)
        def _():
            last(0, 1)

    def full(which):
        return pl.BlockSpec((s, 128), lambda p, i: (0, SB_COL0 + 4 * which + p))

    qblk = pl.BlockSpec((SB_BQ, 128), lambda p, i: (i, p))
    acc = pl.BlockSpec((s, 128), lambda p, i: (0, p))
    return pl.pallas_call(
        body,
        name="sb_bwd",
        grid=(SB_PAIRS, nq),
        in_specs=[pl.BlockSpec((SB_BQ, 128), lambda p, i: (i, SB_COL0 + p)), full(1), full(2), qblk,
                  pl.BlockSpec((1, SB_BQ, 128), lambda p, i: (p, i, 0))],
        out_specs=[qblk, acc, acc],
        out_shape=[jax.ShapeDtypeStruct((s, SB_W), F32)] * 3,
        scratch_shapes=[pltpu.VMEM((2, 2, SB_BQ, SB_BK), F32), pltpu.VMEM((2, 2, SB_BQ, SB_BK), F32),
                        pltpu.VMEM((2, SB_BQ, 2 * SB_BK), BF16), pltpu.VMEM((2, SB_BQ, 2 * SB_BK), BF16),
                        pltpu.VMEM((2, SB_BQ, 128), F32)],
        compiler_params=_cparams(("parallel", "arbitrary")),
    )(qkv, qkv, qkv, do, carries)


MERGE_TILE = 256


def _group_mix(lses):
    mx = jnp.maximum(jnp.maximum(lses[0], lses[1]), lses[2])
    es = [jnp.exp(t - mx) for t in lses]
    den = es[0] + es[1] + es[2]
    return [e / den for e in es]


def _merge_fwd(o_groups, lse_groups, o_sb, gl, b_gate, w_up_dil, w_up_sb):
    s = gl.shape[0]
    t = MERGE_TILE

    def body(o0, o1, o2, l0, l1, l2, ob_ref, gl_ref, bg_ref, wd_ref, ws_ref, merged_ref, oa_ref):
        w = _group_mix([l0[...], l1[...], l2[...]])
        oa = (w[0] * o0[...] + w[1] * o1[...] + w[2] * o2[...]).astype(BF16)
        ua = _dot_nn(oa, wd_ref[...])
        ub = _dot_nn(ob_ref[...].astype(BF16), ws_ref[...])
        gate = jax.nn.sigmoid(gl_ref[...] + bg_ref[...])
        merged_ref[...] = (gate[:, :D_MODEL] * ua + gate[:, D_MODEL:] * ub).astype(BF16)
        oa_ref[...] = oa

    dil = pl.BlockSpec((t, DIL_W), lambda i: (i, 0))
    const = lambda shape: pl.BlockSpec(shape, lambda i: (0, 0))
    return pl.pallas_call(
        body,
        name="merge_fwd",
        grid=(s // t,),
        in_specs=[dil] * 6 + [pl.BlockSpec((t, SB_W), lambda i: (i, 0)), pl.BlockSpec((t, GATE_W), lambda i: (i, 0)),
                              const((1, GATE_W)), const((DIL_W, D_MODEL)), const((SB_W, D_MODEL))],
        out_specs=[pl.BlockSpec((t, D_MODEL), lambda i: (i, 0)), dil],
        out_shape=[jax.ShapeDtypeStruct((s, D_MODEL), BF16), jax.ShapeDtypeStruct((s, DIL_W), BF16)],
        compiler_params=_cparams(("parallel",)),
    )(*o_groups, *lse_groups, o_sb, gl, b_gate, w_up_dil, w_up_sb)


def _merge_bwd(dmerged, o_groups, lse_groups, o_sb, gl, b_gate, w_up_dil, w_up_sb):
    s = gl.shape[0]
    t = MERGE_TILE

    def body(dm_ref, o0, o1, o2, l0, l1, l2, ob_ref, gl_ref, bg_ref, wd_ref, ws_ref,
             dua_ref, dub_ref, dgl_ref, dbg_ref, dosb_ref, d0, d1, d2, c0, c1, c2):
        i = pl.program_id(0)
        og = [o0[...], o1[...], o2[...]]
        w = _group_mix([l0[...], l1[...], l2[...]])
        oa = (w[0] * og[0] + w[1] * og[1] + w[2] * og[2]).astype(BF16)
        ua = _dot_nn(oa, wd_ref[...])
        ub = _dot_nn(ob_ref[...].astype(BF16), ws_ref[...])
        gate = jax.nn.sigmoid(gl_ref[...] + bg_ref[...])
        ga, gb = gate[:, :D_MODEL], gate[:, D_MODEL:]
        dm = dm_ref[...]
        dua = (dm * ga).astype(BF16)
        dub = (dm * gb).astype(BF16)
        dua_ref[...] = dua
        dub_ref[...] = dub
        dgl_a = dm * ua * ga * (1.0 - ga)
        dgl_b = dm * ub * gb * (1.0 - gb)
        dgl_ref[:, :D_MODEL] = dgl_a.astype(BF16)
        dgl_ref[:, D_MODEL:] = dgl_b.astype(BF16)
        part = jnp.concatenate([jnp.sum(dgl_a.reshape(t // 8, 8, D_MODEL), axis=0),
                                jnp.sum(dgl_b.reshape(t // 8, 8, D_MODEL), axis=0)], axis=1)

        @pl.when(i == 0)
        def _():
            dbg_ref[...] = part

        @pl.when(i > 0)
        def _():
            dbg_ref[...] += part

        dosb_ref[...] = _dot_nt(dub, ws_ref[...])
        doa = _dot_nt(dua, wd_ref[...])
        rr = lax.broadcasted_iota(jnp.int32, (DIL_W, DIL_W), 0) // HEAD_DIM
        cc = lax.broadcasted_iota(jnp.int32, (DIL_W, DIL_W), 1) // HEAD_DIM
        same_head = (rr == cc).astype(BF16)
        dw = [_dot_f32_by_01(doa * og[g], same_head) for g in range(3)]
        mean_dw = w[0] * dw[0] + w[1] * dw[1] + w[2] * dw[2]
        for g, (d_ref, c_ref) in enumerate(((d0, c0), (d1, c1), (d2, c2))):
            d_ref[...] = w[g] * doa
            c_ref[...] = -w[g] * mean_dw

    dil = pl.BlockSpec((t, DIL_W), lambda i: (i, 0))
    wide = pl.BlockSpec((t, D_MODEL), lambda i: (i, 0))
    gate2 = pl.BlockSpec((t, GATE_W), lambda i: (i, 0))
    sbw = pl.BlockSpec((t, SB_W), lambda i: (i, 0))
    const = lambda shape: pl.BlockSpec(shape, lambda i: (0, 0))
    return pl.pallas_call(
        body,
        name="merge_bwd",
        grid=(s // t,),
        in_specs=[wide] + [dil] * 6 + [sbw, gate2, const((1, GATE_W)), const((DIL_W, D_MODEL)), const((SB_W, D_MODEL))],
        out_specs=[wide, wide, gate2, const((8, GATE_W)), sbw] + [dil] * 6,
        out_shape=[jax.ShapeDtypeStruct((s, D_MODEL), BF16), jax.ShapeDtypeStruct((s, D_MODEL), BF16),
                   jax.ShapeDtypeStruct((s, GATE_W), BF16), jax.ShapeDtypeStruct((8, GATE_W), F32),
                   jax.ShapeDtypeStruct((s, SB_W), F32)] + [jax.ShapeDtypeStruct((s, DIL_W), F32)] * 6,
        compiler_params=_cparams(("arbitrary",)),
    )(dmerged, *o_groups, *lse_groups, o_sb, gl, b_gate, w_up_dil, w_up_sb)


ANY = pl.BlockSpec(memory_space=pl.ANY)


def _place():
    x, y, c = lax.axis_index("x"), lax.axis_index("y"), lax.axis_index("c")
    other_chips = [(1 - x, y), (x, 1 - y), (1 - x, 1 - y)]
    return x, y, c, other_chips


def _all_gather_weights(pack):
    r, wd = pack.shape
    rh = r // 2
    pack = pack.reshape(2, rh, wd)

    def body(p_ref, out_ref, send_sems, recv_sems, local_sem):
        x, y, c, chips = _place()
        me, sibling = 2 * x + y, (x, y, 1 - c)

        def half(chip_idx, core):
            return out_ref.at[chip_idx, core]

        def copy(k, chip_idx, core, to, src=None):
            return pltpu.make_async_remote_copy(
                src_ref=half(chip_idx, core) if src is None else src, dst_ref=half(chip_idx, core),
                send_sem=send_sems.at[k], recv_sem=recv_sems.at[k], device_id=to, device_id_type=MESH)

        mine = pltpu.make_async_copy(p_ref, out_ref.at[me], local_sem)
        mine.start()
        first = [copy(j, me, c, (*chip, c), src=p_ref.at[c]) for j, chip in enumerate(chips)]
        for cp in first:
            cp.start()
        passed = [copy(3 + j, 2 * chip[0] + chip[1], c, sibling) for j, chip in enumerate(chips)]
        for j, chip in enumerate(chips):
            copy(j, 2 * chip[0] + chip[1], c, (x, y, c)).wait_recv()
            passed[j].start()
        for j, chip in enumerate(chips):
            copy(3 + j, 2 * chip[0] + chip[1], 1 - c, (x, y, c)).wait_recv()
        for cp in first + passed:
            cp.wait_send()
        mine.wait()

    return pl.pallas_call(
        body,
        name="all_gather_weights",
        in_specs=[ANY],
        out_specs=ANY,
        out_shape=jax.ShapeDtypeStruct((N_CHIPS, 2, rh, wd), pack.dtype),
        scratch_shapes=[pltpu.SemaphoreType.DMA((6,)), pltpu.SemaphoreType.DMA((6,)), pltpu.SemaphoreType.DMA],
    )(pack).reshape(N_CHIPS, r, wd)


def _swap_halves(g):
    n, r, wd = g.shape
    rh = r // 2
    g = g.reshape(n, 2, rh, wd)

    def body(g_ref, out_ref, send_sem, recv_sem):
        x, y, c, _ = _place()
        cp = pltpu.make_async_remote_copy(
            src_ref=g_ref.at[:, 1 - c], dst_ref=out_ref,
            send_sem=send_sem, recv_sem=recv_sem, device_id=(x, y, 1 - c), device_id_type=MESH)
        cp.start()
        cp.wait()

    return pl.pallas_call(
        body,
        name="grad_swap_halves",
        in_specs=[ANY],
        out_specs=ANY,
        out_shape=jax.ShapeDtypeStruct((n, rh, wd), g.dtype),
        scratch_shapes=[pltpu.SemaphoreType.DMA, pltpu.SemaphoreType.DMA],
    )(g)


def _add_halves(g, got, core):
    n, r, wd = g.shape
    rh = r // 2
    t = rh // 4
    nt = rh // t

    def body(c_ref, a_ref, b_ref, o_ref):
        o_ref[...] = (a_ref[0] + b_ref[...]).astype(BF16)

    grid_spec = pltpu.PrefetchScalarGridSpec(
        num_scalar_prefetch=1,
        grid=(n, nt),
        in_specs=[pl.BlockSpec((1, 1, t, wd), lambda s, i, c: (s, c[0], i, 0)),
                  pl.BlockSpec((1, t, wd), lambda s, i, c: (s, i, 0))],
        out_specs=pl.BlockSpec((1, t, wd), lambda s, i, c: (s, i, 0)),
    )
    return pl.pallas_call(
        body,
        name="grad_add_halves",
        grid_spec=grid_spec,
        out_shape=jax.ShapeDtypeStruct((n, rh, wd), BF16),
        compiler_params=_cparams(("parallel", "parallel")),
    )(core, g.reshape(n, 2, rh, wd), got)


def _exchange_chunks(h):
    n, rh, wd = h.shape

    def body(h_ref, out_ref, send_sems, recv_sems, local_sem):
        x, y, c, chips = _place()
        me = 2 * x + y
        mine = pltpu.make_async_copy(h_ref.at[me], out_ref.at[me], local_sem)
        mine.start()
        sends = []
        for j, chip in enumerate(chips):
            them = 2 * chip[0] + chip[1]
            sends.append(pltpu.make_async_remote_copy(
                src_ref=h_ref.at[them], dst_ref=out_ref.at[me],
                send_sem=send_sems.at[j], recv_sem=recv_sems.at[j], device_id=(*chip, c), device_id_type=MESH))
        for cp in sends:
            cp.start()
        for j, chip in enumerate(chips):
            them = 2 * chip[0] + chip[1]
            pltpu.make_async_remote_copy(
                src_ref=h_ref.at[them], dst_ref=out_ref.at[them],
                send_sem=send_sems.at[j], recv_sem=recv_sems.at[j], device_id=(*chip, c), device_id_type=MESH).wait_recv()
        for cp in sends:
            cp.wait_send()
        mine.wait()

    return pl.pallas_call(
        body,
        name="grad_exchange_chunks",
        in_specs=[ANY],
        out_specs=ANY,
        out_shape=jax.ShapeDtypeStruct((n, rh, wd), h.dtype),
        scratch_shapes=[pltpu.SemaphoreType.DMA((3,)), pltpu.SemaphoreType.DMA((3,)), pltpu.SemaphoreType.DMA],
    )(h)


def _sum_chips(b):
    n, rh, wd = b.shape
    t = rh // 4

    def body(b_ref, o_ref):
        b0, b1, b2, b3 = (b_ref[k].astype(F32) for k in range(n))
        o_ref[...] = ((b0 + b1) + b2) + b3

    return pl.pallas_call(
        body,
        name="grad_sum_chips",
        grid=(rh // t,),
        in_specs=[pl.BlockSpec((n, t, wd), lambda i: (0, i, 0))],
        out_specs=pl.BlockSpec((t, wd), lambda i: (i, 0)),
        out_shape=jax.ShapeDtypeStruct((rh, wd), F32),
        compiler_params=_cparams(("parallel",)),
    )(b)


def _join_halves(tc):
    rh, wd = tc.shape

    def body(t_ref, out_ref, send_sem, recv_sem, local_sem):
        x, y, c, _ = _place()
        mine = pltpu.make_async_copy(t_ref, out_ref.at[c], local_sem)
        mine.start()
        cp = pltpu.make_async_remote_copy(
            src_ref=t_ref, dst_ref=out_ref.at[c],
            send_sem=send_sem, recv_sem=recv_sem, device_id=(x, y, 1 - c), device_id_type=MESH)
        cp.start()
        cp.wait()
        mine.wait()

    return pl.pallas_call(
        body,
        name="grad_join_halves",
        in_specs=[ANY],
        out_specs=ANY,
        out_shape=jax.ShapeDtypeStruct((2, rh, wd), tc.dtype),
        scratch_shapes=[pltpu.SemaphoreType.DMA, pltpu.SemaphoreType.DMA, pltpu.SemaphoreType.DMA],
    )(tc).reshape(2 * rh, wd)


def _all_reduce_small(pack):
    rows, lanes = pack.shape

    def body(p_ref, out_ref, buf, send_sems, recv_sems):
        x, y, c, _ = _place()
        me = 4 * x + 2 * y + c
        buf[me] = p_ref[...]
        sends = []
        for k in range(1, N_DEV):
            peer = (x ^ (k >> 2), y ^ ((k >> 1) & 1), c ^ (k & 1))
            sends.append(pltpu.make_async_remote_copy(
                src_ref=p_ref, dst_ref=buf.at[me], send_sem=send_sems.at[k - 1], recv_sem=recv_sems.at[k - 1],
                device_id=peer, device_id_type=MESH))
        for cp in sends:
            cp.start()
        for k in range(1, N_DEV):
            pltpu.make_async_remote_copy(
                src_ref=p_ref, dst_ref=buf.at[me ^ k], send_sem=send_sems.at[k - 1], recv_sem=recv_sems.at[k - 1],
                device_id=(x, y, c), device_id_type=MESH).wait_recv()
        for cp in sends:
            cp.wait_send()
        total = buf[0]
        for d in range(1, N_DEV):
            total = total + buf[d]
        out_ref[...] = total

    vm = pl.BlockSpec(memory_space=pltpu.VMEM)
    return pl.pallas_call(
        body,
        name="all_reduce_small",
        in_specs=[vm],
        out_specs=vm,
        out_shape=jax.ShapeDtypeStruct((rows, lanes), F32),
        scratch_shapes=[pltpu.VMEM((N_DEV, rows, lanes), F32), pltpu.SemaphoreType.DMA((N_DEV - 1,)),
                        pltpu.SemaphoreType.DMA((N_DEV - 1,))],
    )(pack)


def _adamw(g, w, m, v, name):
    rows, cols = g.shape
    t = rows
    for cand in (256, 128, 64, 32, 16, 8):
        if rows % cand == 0:
            t = cand
            break

    def body(g_ref, w_ref, m_ref, v_ref, d_ref, nm_ref, nv_ref):
        gv = g_ref[...]
        mv = ADAM_B1 * m_ref[...] + (1.0 - ADAM_B1) * gv
        vv = ADAM_B2 * v_ref[...] + (1.0 - ADAM_B2) * (gv * gv)
        m_hat = mv / (1.0 - ADAM_B1 ** ADAM_STEP)
        v_hat = vv / (1.0 - ADAM_B2 ** ADAM_STEP)
        d_ref[...] = -ADAM_LR * (m_hat / (jnp.sqrt(v_hat) + ADAM_EPS) + ADAM_WD * w_ref[...])
        nm_ref[...] = mv
        nv_ref[...] = vv

    blk = pl.BlockSpec((t, cols), lambda i: (i, 0))
    return pl.pallas_call(
        body,
        name=name,
        grid=(rows // t,),
        in_specs=[blk] * 4,
        out_specs=[blk] * 3,
        out_shape=[jax.ShapeDtypeStruct((rows, cols), F32)] * 3,
        compiler_params=_cparams(("parallel",)),
    )(g, w, m, v)


PACK_W = 1024
BIG = (("w_in", (D_MODEL, IN_COLS), 1), ("w_up_dil", (DIL_W, D_MODEL), 1), ("w_up_sb", (SB_W, D_MODEL), 1),
       ("w_out", (D_MODEL, D_MODEL), 0), ("w_mlp_in", (D_MODEL, D_FF), 1), ("w_mlp_out", (D_FF, D_MODEL), 0))


def _shard_shape(shape, axis):
    return tuple(d // N_CHIPS if a == axis else d for a, d in enumerate(shape))


def _pack_rows():
    rows, at = {}, 0
    for name, shape, axis in BIG:
        n = math.prod(_shard_shape(shape, axis)) // PACK_W
        rows[name] = (at, n)
        at += n
    return rows, at


def _pack_shards(shards):
    return jnp.concatenate([shards[name].reshape(-1, PACK_W) for name, _, _ in BIG], axis=0)


def _unpack_full(gathered):
    rows, _ = _pack_rows()
    full = {}
    for name, shape, axis in BIG:
        at, n = rows[name]
        parts = gathered[:, at:at + n, :].reshape((N_CHIPS,) + _shard_shape(shape, axis))
        if axis == 0:
            full[name] = parts.reshape(shape)
        else:
            full[name] = jnp.transpose(parts, (1, 0, 2)).reshape(shape)
    return full


def _pack_full_grads(grads):
    chunks = []
    for name, shape, axis in BIG:
        g = grads[name]
        if axis == 0:
            parts = g.reshape((N_CHIPS, shape[0] // N_CHIPS, shape[1]))
        else:
            parts = jnp.transpose(g.reshape((shape[0], N_CHIPS, shape[1] // N_CHIPS)), (1, 0, 2))
        chunks.append(parts.reshape(N_CHIPS, -1, PACK_W))
    return jnp.concatenate(chunks, axis=1)


def _unpack_shard(packed):
    rows, _ = _pack_rows()
    return {name: packed[rows[name][0]:rows[name][0] + rows[name][1]].reshape(_shard_shape(shape, axis))
            for name, shape, axis in BIG}


def _local_step(x, target, w, norm_mix_g, b_gate, norm_mlp_g, norm_final_g):
    w_qkv, w_gate = w["w_in"][:, :QKV_W], w["w_in"][:, QKV_W:]

    h = _rms_fwd(x, norm_mix_g, "norm_mix")
    (qkv,) = _matmul(h, w_qkv, mode="nn", out_dtypes=(BF16,), name="proj_qkv", tn=768)
    (gl,) = _matmul(h, w_gate, mode="nn", out_dtypes=(F32,), name="proj_gate")
    dil = [_dil_fwd(qkv, g) for g in range(3)]
    o_groups, lse_groups = [d[0] for d in dil], [d[1] for d in dil]
    o_sb, carries = _sb_fwd(qkv)
    merged, o_a = _merge_fwd(o_groups, lse_groups, o_sb, gl, b_gate, w["w_up_dil"], w["w_up_sb"])
    (x1,) = _matmul(merged, w["w_out"], mode="nn", out_dtypes=(F32,), name="out_proj",
                    extras=(x,), epilogue=lambda acc, res: (res + acc,))
    h2 = _rms_fwd(x1, norm_mlp_g, "norm_mlp")
    u, act = _matmul(h2, w["w_mlp_in"], mode="nn", out_dtypes=(F32, BF16), name="mlp_in",
                     epilogue=lambda acc: (acc, jnp.square(jnp.maximum(acc, 0.0))))
    (x2,) = _matmul(act, w["w_mlp_out"], mode="nn", out_dtypes=(F32,), name="mlp_out", tk=2048,
                    extras=(x1,), epilogue=lambda acc, res: (res + acc,))
    dx2, dg_final, loss_part = _loss_head(x2, norm_final_g.reshape(1, D_MODEL), target)

    (du,) = _matmul(dx2, w["w_mlp_out"], mode="nt", out_dtypes=(BF16,), name="mlp_out_dx",
                    extras=(u,), epilogue=lambda acc, uu: (acc * (2.0 * jnp.maximum(uu, 0.0)),))
    (g_mlp_out,) = _matmul(act, dx2, mode="tn", out_dtypes=(F32,), name="mlp_out_dw")
    (g_mlp_in,) = _matmul(h2, du, mode="tn", out_dtypes=(F32,), name="mlp_in_dw")
    (dh2,) = _matmul(du, w["w_mlp_in"], mode="nt", out_dtypes=(F32,), name="mlp_in_dx", tk=2048)
    dx1, dg_mlp = _rms_bwd(dh2, x1, norm_mlp_g, dx2, "norm_mlp_bwd")

    (dmerged,) = _matmul(dx1, w["w_out"], mode="nt", out_dtypes=(F32,), name="out_proj_dx")
    (g_out,) = _matmul(merged, dx1, mode="tn", out_dtypes=(F32,), name="out_proj_dw")
    mb = _merge_bwd(dmerged, o_groups, lse_groups, o_sb, gl, b_gate, w["w_up_dil"], w["w_up_sb"])
    dua, dub, dgl, dbg, do_sb = mb[:5]
    do_groups, c_groups = mb[5:8], mb[8:11]
    (g_up_dil,) = _matmul(o_a, dua, mode="tn", out_dtypes=(F32,), name="up_dil_dw")
    (g_up_sb,) = _matmul(o_sb, dub, mode="tn", out_dtypes=(F32,), name="up_sb_dw")
    dq_sb, dk_sb, dv_sb = _sb_bwd(qkv, do_sb, carries)
    dil_b = [_dil_bwd(qkv, do_groups[g], lse_groups[g], c_groups[g], g) for g in range(3)]
    dproj = jnp.concatenate(
        [dil_b[g][i].astype(BF16) for i in range(3) for g in range(3)]
        + [t.astype(BF16) for t in (dq_sb, dk_sb, dv_sb)] + [dgl], axis=1)
    (g_in,) = _matmul(h, dproj, mode="tn", out_dtypes=(F32,), name="proj_dw", tm=512, tn=IN_COLS // 2)
    (dh,) = _matmul(dproj, w["w_in"], mode="nt", out_dtypes=(F32,), name="proj_dx", tk=IN_COLS // 2)
    grad_x, dg_mix = _rms_bwd(dh, x, norm_mix_g, dx1, "norm_mix_bwd")

    big = {"w_in": g_in, "w_up_dil": g_up_dil, "w_up_sb": g_up_sb, "w_out": g_out,
           "w_mlp_in": g_mlp_in, "w_mlp_out": g_mlp_out}
    small = (dg_mix, dbg, dg_mlp, dg_final, loss_part)
    return grad_x, big, small


def kernel(x, norm_mix_g, w_in, b_gate, w_up_dil, w_up_sb, w_out, norm_mlp_g, w_mlp_in, w_mlp_out, norm_final_g, loss_target, m_norm_mix_g, m_w_in, m_b_gate, m_w_up_dil, m_w_up_sb, m_w_out, m_norm_mlp_g, m_w_mlp_in, m_w_mlp_out, m_norm_final_g, v_norm_mix_g, v_w_in, v_b_gate, v_w_up_dil, v_w_up_sb, v_w_out, v_norm_mlp_g, v_w_mlp_in, v_w_mlp_out, v_norm_final_g):
    shards = {"w_in": w_in[0], "w_up_dil": w_up_dil[0], "w_up_sb": w_up_sb[0], "w_out": w_out[0],
              "w_mlp_in": w_mlp_in[0], "w_mlp_out": w_mlp_out[0]}
    moments_m = {"w_in": m_w_in[0], "w_up_dil": m_w_up_dil[0], "w_up_sb": m_w_up_sb[0], "w_out": m_w_out[0],
                 "w_mlp_in": m_w_mlp_in[0], "w_mlp_out": m_w_mlp_out[0]}
    moments_v = {"w_in": v_w_in[0], "w_up_dil": v_w_up_dil[0], "w_up_sb": v_w_up_sb[0], "w_out": v_w_out[0],
                 "w_mlp_in": v_w_mlp_in[0], "w_mlp_out": v_w_mlp_out[0]}

    pack = _pack_shards({n: s.astype(BF16) for n, s in shards.items()})
    full = _unpack_full(_all_gather_weights(pack))

    grad_x, big, small = _local_step(x[0], loss_target[0], full, norm_mix_g, b_gate, norm_mlp_g, norm_final_g)

    core = lax.axis_index("c").astype(jnp.int32).reshape(1)
    gpack = _pack_full_grads(big)
    chip_sum = _add_halves(gpack, _swap_halves(gpack), core)
    reduced = _join_halves(_sum_chips(_exchange_chunks(chip_sum)))
    g_shard = _unpack_shard(reduced)

    dg_mix, dbg, dg_mlp, dg_final, loss_part = small
    loss_row = jnp.sum(loss_part, axis=0, keepdims=True)
    small_pack = jnp.concatenate(
        [jnp.sum(dg_mix, axis=0, keepdims=True), jnp.sum(dbg, axis=0, keepdims=True),
         jnp.sum(dg_mlp, axis=0, keepdims=True), jnp.sum(dg_final, axis=0, keepdims=True), loss_row], axis=1)
    n_small = small_pack.shape[1]
    small_sum = _all_reduce_small(small_pack.reshape(n_small // 128, 128)).reshape(1, n_small)
    g_norm_mix = small_sum[:, :D_MODEL]
    g_b_gate = small_sum[:, D_MODEL:3 * D_MODEL]
    g_norm_mlp = small_sum[:, 3 * D_MODEL:4 * D_MODEL]
    g_norm_final = small_sum[:, 4 * D_MODEL:5 * D_MODEL]
    loss = jnp.sum(small_sum[:, 5 * D_MODEL:])

    names = ["norm_mix_g", "w_in", "b_gate", "w_up_dil", "w_up_sb", "w_out", "norm_mlp_g", "w_mlp_in", "w_mlp_out",
             "norm_final_g"]
    grads = dict(g_shard)
    grads.update(norm_mix_g=g_norm_mix, b_gate=g_b_gate, norm_mlp_g=g_norm_mlp, norm_final_g=g_norm_final)
    weights = dict(shards)
    weights.update(norm_mix_g=norm_mix_g, b_gate=b_gate, norm_mlp_g=norm_mlp_g, norm_final_g=norm_final_g.reshape(1, D_MODEL))
    ms = dict(moments_m)
    ms.update(norm_mix_g=m_norm_mix_g, b_gate=m_b_gate, norm_mlp_g=m_norm_mlp_g, norm_final_g=m_norm_final_g.reshape(1, D_MODEL))
    vs = dict(moments_v)
    vs.update(norm_mix_g=v_norm_mix_g, b_gate=v_b_gate, norm_mlp_g=v_norm_mlp_g, norm_final_g=v_norm_final_g.reshape(1, D_MODEL))

    out_shapes = {"norm_mix_g": norm_mix_g.shape, "w_in": w_in.shape, "b_gate": b_gate.shape, "w_up_dil": w_up_dil.shape,
                  "w_up_sb": w_up_sb.shape, "w_out": w_out.shape, "norm_mlp_g": norm_mlp_g.shape,
                  "w_mlp_in": w_mlp_in.shape, "w_mlp_out": w_mlp_out.shape, "norm_final_g": norm_final_g.shape}
    g_out, d_out, m_out, v_out = [], [], [], []
    for n in names:
        d, nm, nv = _adamw(grads[n], weights[n], ms[n], vs[n], "adamw_" + n)
        shape = out_shapes[n]
        g_out.append(grads[n].reshape(shape))
        d_out.append(d.reshape(shape))
        m_out.append(nm.reshape(shape))
        v_out.append(nv.reshape(shape))
    return (loss, grad_x.reshape(x.shape), *g_out, *d_out, *m_out, *v_out)
```

```python
import functools
import math

import jax
import jax.numpy as jnp
import numpy as np
from jax import lax
from jax.experimental import pallas as pl
from jax.experimental.pallas import tpu as pltpu

F32 = jnp.float32
BF16 = jnp.bfloat16
MESH = pl.DeviceIdType.MESH

D_MODEL = 1024
HEAD_DIM = 64
DIL_GROUPS = ((128, 1), (512, 4), (2048, 16))
DIL_HEADS = 4
DIL_W = 256
N_DIL_HEADS = 12
SB_HEADS = 8
SB_W = SB_HEADS * HEAD_DIM
QKV_W = 3 * 3 * DIL_W + 3 * SB_W
GATE_W = 2 * D_MODEL
IN_COLS = QKV_W + GATE_W
D_FF = 4 * D_MODEL
BLOCK = 128
RMS_EPS = 1e-6
NEG_INF = -1e30
N_CHIPS = 4
N_DEV = 8

ADAM_LR = 0.001
ADAM_B1 = 0.9
ADAM_B2 = 0.999
ADAM_EPS = 1e-08
ADAM_WD = 0.01
ADAM_STEP = 10

VMEM_LIMIT = 56 * 1024 * 1024

SB_BQ = 256
SB_BK = 256


def _cparams(sem=None):
    if sem is None:
        return pltpu.CompilerParams(vmem_limit_bytes=VMEM_LIMIT)
    return pltpu.CompilerParams(dimension_semantics=sem, vmem_limit_bytes=VMEM_LIMIT)


def _dot(a, b, dims):
    return lax.dot_general(a, b, (dims, ((), ())), preferred_element_type=F32)


def _dot_nn(a, b):
    return _dot(a, b, ((1,), (0,)))


def _dot_nt(a, b):
    return _dot(a, b, ((1,), (1,)))


def _dot_tn(a, b):
    return _dot(a, b, ((0,), (0,)))


def _dot_f32_by_01(x, m01, pieces=3):
    hi = x.astype(BF16)
    r1 = x - hi.astype(F32)
    mid = r1.astype(BF16)
    if pieces == 2:
        return _dot_nn(hi, m01) + _dot_nn(mid, m01)
    lo = (r1 - mid.astype(F32)).astype(BF16)
    return _dot_nn(hi, m01) + _dot_nn(mid, m01) + _dot_nn(lo, m01)


def _matmul(a, b, *, mode, out_dtypes, name, tm=1024, tn=1024, tk=1024, extras=(), epilogue=None):
    if mode == "nn":
        (m, k), (k2, n) = a.shape, b.shape
    elif mode == "nt":
        (m, k), (n, k2) = a.shape, b.shape
    else:
        (k, m), (k2, n) = a.shape, b.shape
    assert k == k2, (a.shape, b.shape, mode)
    tm, tn, tk = min(tm, m), min(tn, n), min(tk, k)
    assert m % tm == 0 and n % tn == 0 and k % tk == 0, (m, n, k, tm, tn, tk)
    nk = k // tk
    n_out = len(out_dtypes)
    n_ex = len(extras)

    if mode == "nn":
        a_spec = pl.BlockSpec((tm, tk), lambda i, j, kk: (i, kk))
        b_spec = pl.BlockSpec((tk, tn), lambda i, j, kk: (kk, j))
        dot = _dot_nn
    elif mode == "nt":
        a_spec = pl.BlockSpec((tm, tk), lambda i, j, kk: (i, kk))
        b_spec = pl.BlockSpec((tn, tk), lambda i, j, kk: (j, kk))
        dot = _dot_nt
    else:
        a_spec = pl.BlockSpec((tk, tm), lambda i, j, kk: (kk, i))
        b_spec = pl.BlockSpec((tk, tn), lambda i, j, kk: (kk, j))
        dot = _dot_tn
    mn_spec = pl.BlockSpec((tm, tn), lambda i, j, kk: (i, j))

    def body(*refs):
        a_ref, b_ref = refs[0], refs[1]
        ex_refs = refs[2:2 + n_ex]
        out_refs = refs[2 + n_ex:2 + n_ex + n_out]
        acc_ref = refs[2 + n_ex + n_out] if nk > 1 else None
        part = dot(a_ref[...].astype(BF16), b_ref[...].astype(BF16))

        def finish(acc):
            if epilogue is None:
                outs = (acc,)
            else:
                outs = epilogue(acc, *[r[...] for r in ex_refs])
            for o_ref, o in zip(out_refs, outs):
                o_ref[...] = o.astype(o_ref.dtype)

        if nk == 1:
            finish(part)
        else:
            kk = pl.program_id(2)

            @pl.when(kk == 0)
            def _():
                acc_ref[...] = part

            @pl.when(kk > 0)
            def _():
                acc_ref[...] += part

            @pl.when(kk == nk - 1)
            def _():
                finish(acc_ref[...])

    outs = pl.pallas_call(
        body,
        name=name,
        grid=(m // tm, n // tn, nk),
        in_specs=[a_spec, b_spec] + [mn_spec] * n_ex,
        out_specs=[mn_spec] * n_out,
        out_shape=[jax.ShapeDtypeStruct((m, n), dt) for dt in out_dtypes],
        scratch_shapes=[pltpu.VMEM((tm, tn), F32)] if nk > 1 else [],
        compiler_params=_cparams(("parallel", "parallel", "arbitrary")),
    )(a, b, *extras)
    return outs


ROW_TILE = 512


def _rms_fwd(x, g, name):
    s, d = x.shape

    def body(x_ref, g_ref, h_ref):
        xv = x_ref[...]
        r = lax.rsqrt(jnp.mean(xv * xv, axis=-1, keepdims=True) + RMS_EPS)
        h_ref[...] = (xv * r * g_ref[...]).astype(BF16)

    return pl.pallas_call(
        body,
        name=name,
        grid=(s // ROW_TILE,),
        in_specs=[pl.BlockSpec((ROW_TILE, d), lambda i: (i, 0)), pl.BlockSpec((1, d), lambda i: (0, 0))],
        out_specs=pl.BlockSpec((ROW_TILE, d), lambda i: (i, 0)),
        out_shape=jax.ShapeDtypeStruct((s, d), BF16),
        compiler_params=_cparams(("parallel",)),
    )(x, g)


def _rms_bwd(dh, x, g, dres, name):
    s, d = x.shape

    def body(dh_ref, x_ref, g_ref, dres_ref, dx_ref, dg_ref):
        i = pl.program_id(0)
        xv = x_ref[...]
        r = lax.rsqrt(jnp.mean(xv * xv, axis=-1, keepdims=True) + RMS_EPS)
        xh = xv * r
        dhv = dh_ref[...]
        dxh = dhv * g_ref[...]
        dx = r * (dxh - xh * jnp.mean(dxh * xh, axis=-1, keepdims=True))
        dx_ref[...] = dres_ref[...] + dx
        part = jnp.sum((dhv * xh).reshape(ROW_TILE // 8, 8, d), axis=0)

        @pl.when(i == 0)
        def _():
            dg_ref[...] = part

        @pl.when(i > 0)
        def _():
            dg_ref[...] += part

    row = pl.BlockSpec((ROW_TILE, d), lambda i: (i, 0))
    return pl.pallas_call(
        body,
        name=name,
        grid=(s // ROW_TILE,),
        in_specs=[row, row, pl.BlockSpec((1, d), lambda i: (0, 0)), row],
        out_specs=[row, pl.BlockSpec((8, d), lambda i: (0, 0))],
        out_shape=[jax.ShapeDtypeStruct((s, d), F32), jax.ShapeDtypeStruct((8, d), F32)],
        compiler_params=_cparams(("arbitrary",)),
    )(dh, x, g, dres)


def _loss_head(x2, g, target):
    s, d = x2.shape

    def body(x_ref, g_ref, t_ref, dx_ref, dg_ref, loss_ref):
        i = pl.program_id(0)
        xv = x_ref[...]
        r = lax.rsqrt(jnp.mean(xv * xv, axis=-1, keepdims=True) + RMS_EPS)
        xh = xv * r
        gv = g_ref[...]
        err = xh * gv - t_ref[...]
        dy = err * (1.0 / d)
        dxh = dy * gv
        dx_ref[...] = r * (dxh - xh * jnp.mean(dxh * xh, axis=-1, keepdims=True))
        part_g = jnp.sum((dy * xh).reshape(ROW_TILE // 8, 8, d), axis=0)
        part_l = (0.5 / d) * jnp.sum((err * err).reshape(ROW_TILE // 8, 8, d), axis=0)

        @pl.when(i == 0)
        def _():
            dg_ref[...] = part_g
            loss_ref[...] = part_l

        @pl.when(i > 0)
        def _():
            dg_ref[...] += part_g
            loss_ref[...] += part_l

    row = pl.BlockSpec((ROW_TILE, d), lambda i: (i, 0))
    acc = pl.BlockSpec((8, d), lambda i: (0, 0))
    return pl.pallas_call(
        body,
        name="loss_head",
        grid=(s // ROW_TILE,),
        in_specs=[row, pl.BlockSpec((1, d), lambda i: (0, 0)), row],
        out_specs=[row, acc, acc],
        out_shape=[jax.ShapeDtypeStruct((s, d), F32), jax.ShapeDtypeStruct((8, d), F32),
                   jax.ShapeDtypeStruct((8, d), F32)],
        compiler_params=_cparams(("arbitrary",)),
    )(x2, g, target)


def _alibi_slopes():
    return np.exp2(np.float32(-8.0) * np.arange(1, N_DIL_HEADS + 1, dtype=np.float32) / np.float32(N_DIL_HEADS))


def _head_lane_mask(h, rows):
    lane = lax.broadcasted_iota(jnp.int32, (rows, DIL_W), 1)
    return (lane >= h * HEAD_DIM) & (lane < (h + 1) * HEAD_DIM)


def _band_terms(group, dil):
    qi = lax.broadcasted_iota(jnp.int32, (BLOCK, BLOCK), 0)
    kj = lax.broadcasted_iota(jnp.int32, (BLOCK, BLOCK), 1)
    steps_prev = (qi + BLOCK - kj).astype(F32) * float(dil)
    steps_cur = (qi - kj).astype(F32) * float(dil)
    return kj >= qi, kj <= qi, steps_prev, steps_cur


def _dil_fwd(qkv, group):
    _, dil = DIL_GROUPS[group]
    s = qkv.shape[0]
    sub = s // dil
    nb = sub // BLOCK
    view = qkv.reshape(sub, dil * QKV_W)
    wblk = QKV_W // DIL_W
    slopes = _alibi_slopes()[group * DIL_HEADS:(group + 1) * DIL_HEADS]

    def col(which):
        return lambda r, n: (n, r * wblk + which * 3 + group)

    def col_prev(which):
        return lambda r, n: (jnp.maximum(n - 1, 0), r * wblk + which * 3 + group)

    def body(q_ref, kc_ref, kp_ref, vc_ref, vp_ref, o_ref, lse_ref):
        n = pl.program_id(1)
        valid_p, valid_c, dist_p, dist_c = _band_terms(group, dil)
        valid_p = valid_p & (n > 0)
        q = q_ref[...]
        kc, kp, vc, vp = kc_ref[...], kp_ref[...], vc_ref[...], vp_ref[...]
        o_acc = jnp.zeros((BLOCK, DIL_W), F32)
        lse_acc = jnp.zeros((BLOCK, DIL_W), F32)
        for h in range(DIL_HEADS):
            hm = _head_lane_mask(h, BLOCK)
            qh = jnp.where(hm, q, jnp.zeros_like(q))
            lp = _dot_nt(qh, kp) * 0.125 - float(slopes[h]) * dist_p
            lc = _dot_nt(qh, kc) * 0.125 - float(slopes[h]) * dist_c
            lp = jnp.where(valid_p, lp, NEG_INF)
            lc = jnp.where(valid_c, lc, NEG_INF)
            mx = jnp.maximum(jnp.max(lp, axis=1, keepdims=True), jnp.max(lc, axis=1, keepdims=True))
            den = jnp.sum(jnp.exp(lp - mx), axis=1, keepdims=True) + jnp.sum(jnp.exp(lc - mx), axis=1, keepdims=True)
            lse = mx + jnp.log(den)
            pp = jnp.exp(lp - lse).astype(BF16)
            pc = jnp.exp(lc - lse).astype(BF16)
            oh = _dot_nn(pp, vp) + _dot_nn(pc, vc)
            o_acc = jnp.where(hm, oh, o_acc)
            lse_acc = jnp.where(hm, lse, lse_acc)
        o_ref[...] = o_acc
        lse_ref[...] = lse_acc

    blk = (BLOCK, DIL_W)
    o, lse = pl.pallas_call(
        body,
        name=f"dil_fwd_g{group}",
        grid=(dil, nb),
        in_specs=[pl.BlockSpec(blk, col(0)), pl.BlockSpec(blk, col(1)), pl.BlockSpec(blk, col_prev(1)),
                  pl.BlockSpec(blk, col(2)), pl.BlockSpec(blk, col_prev(2))],
        out_specs=[pl.BlockSpec(blk, lambda r, n: (n, r))] * 2,
        out_shape=[jax.ShapeDtypeStruct((sub, dil * DIL_W), F32)] * 2,
        compiler_params=_cparams(("parallel", "parallel")),
    )(view, view, view, view, view)
    return o.reshape(s, DIL_W), lse.reshape(s, DIL_W)


def _dil_bwd(qkv, do, lse, cterm, group):
    _, dil = DIL_GROUPS[group]
    s = qkv.shape[0]
    sub = s // dil
    nb = sub // BLOCK
    view = qkv.reshape(sub, dil * QKV_W)
    wblk = QKV_W // DIL_W
    slopes = _alibi_slopes()[group * DIL_HEADS:(group + 1) * DIL_HEADS]
    do_v, lse_v, c_v = (t.reshape(sub, dil * DIL_W) for t in (do, lse, cterm))

    def col(which, shift):
        if shift == 0:
            return lambda r, n: (n, r * wblk + which * 3 + group)
        if shift < 0:
            return lambda r, n: (jnp.maximum(n - 1, 0), r * wblk + which * 3 + group)
        return lambda r, n: (jnp.minimum(n + 1, nb - 1), r * wblk + which * 3 + group)

    def own(shift):
        if shift == 0:
            return lambda r, n: (n, r)
        return lambda r, n: (jnp.minimum(n + 1, nb - 1), r)

    def body(q_ref, qn_ref, kc_ref, kp_ref, vc_ref, vp_ref, do_ref, don_ref, lse_ref, lsen_ref, c_ref, cn_ref,
             dq_ref, dk_ref, dv_ref):
        n = pl.program_id(1)
        valid_p, valid_c, dist_p, dist_c = _band_terms(group, dil)
        has_prev = n > 0
        has_next = n < nb - 1
        q, qn = q_ref[...], qn_ref[...]
        kc, kp, vc, vp = kc_ref[...], kp_ref[...], vc_ref[...], vp_ref[...]
        dov, donv = do_ref[...], don_ref[...]
        lsev, lsenv, cv, cnv = lse_ref[...], lsen_ref[...], c_ref[...], cn_ref[...]
        dq_acc = jnp.zeros((BLOCK, DIL_W), F32)
        dk_acc = jnp.zeros((BLOCK, DIL_W), F32)
        dv_acc = jnp.zeros((BLOCK, DIL_W), F32)

        def head_col(t, hm):
            return jnp.max(jnp.where(hm, t, NEG_INF), axis=1, keepdims=True)

        for h in range(DIL_HEADS):
            hm = _head_lane_mask(h, BLOCK)
            slope = float(slopes[h])

            def pair(qh, k, v, doh, lse_h, c_h, valid, dist):
                logit = _dot_nt(qh, k) * 0.125 - slope * dist
                p = jnp.where(valid, jnp.exp(logit - lse_h), 0.0)
                dp = _dot_nt(doh, v)
                dlog = (p * (dp + c_h) * 0.125).astype(BF16)
                return p.astype(BF16), dlog

            qh = jnp.where(hm, q, jnp.zeros_like(q))
            doh = jnp.where(hm, dov, 0.0).astype(BF16)
            lse_h, c_h = head_col(lsev, hm), head_col(cv, hm)
            _, dlog_p = pair(qh, kp, vp, doh, lse_h, c_h, valid_p & has_prev, dist_p)
            p_c, dlog_c = pair(qh, kc, vc, doh, lse_h, c_h, valid_c, dist_c)
            dq_h = _dot_nn(dlog_p, kp) + _dot_nn(dlog_c, kc)
            dq_acc = jnp.where(hm, dq_h, dq_acc)
            qnh = jnp.where(hm, qn, jnp.zeros_like(qn))
            donh = jnp.where(hm, donv, 0.0).astype(BF16)
            p_n, dlog_n = pair(qnh, kc, vc, donh, head_col(lsenv, hm), head_col(cnv, hm), valid_p & has_next, dist_p)
            dk_acc += _dot_tn(dlog_c, qh) + _dot_tn(dlog_n, qnh)
            dv_acc += _dot_tn(p_c, doh) + _dot_tn(p_n, donh)
        dq_ref[...] = dq_acc.astype(BF16)
        dk_ref[...] = dk_acc.astype(BF16)
        dv_ref[...] = dv_acc.astype(BF16)

    blk = (BLOCK, DIL_W)
    outs = pl.pallas_call(
        body,
        name=f"dil_bwd_g{group}",
        grid=(dil, nb),
        in_specs=[pl.BlockSpec(blk, col(0, 0)), pl.BlockSpec(blk, col(0, 1)),
                  pl.BlockSpec(blk, col(1, 0)), pl.BlockSpec(blk, col(1, -1)),
                  pl.BlockSpec(blk, col(2, 0)), pl.BlockSpec(blk, col(2, -1)),
                  pl.BlockSpec(blk, own(0)), pl.BlockSpec(blk, own(1)),
                  pl.BlockSpec(blk, own(0)), pl.BlockSpec(blk, own(1)),
                  pl.BlockSpec(blk, own(0)), pl.BlockSpec(blk, own(1))],
        out_specs=[pl.BlockSpec(blk, lambda r, n: (n, r))] * 3,
        out_shape=[jax.ShapeDtypeStruct((sub, dil * DIL_W), BF16)] * 3,
        compiler_params=_cparams(("parallel", "parallel")),
    )(view, view, view, view, view, view, do_v, do_v, lse_v, lse_v, c_v, c_v)
    return tuple(t.reshape(s, DIL_W) for t in outs)


SB_PAIRS = SB_HEADS // 2
SB_COL0 = (9 * DIL_W) // 128
LOG2E = 1.4426950408889634


def _sb_log_terms(zs):
    e = jnp.exp2(-jnp.abs(zs))
    return -(jnp.maximum(zs, 0.0) + jnp.log(1.0 + e) * LOG2E)


def _sb_consts(nkb):
    row = lax.broadcasted_iota(jnp.int32, (SB_BQ, SB_BK), 0)
    colk = lax.broadcasted_iota(jnp.int32, (SB_BQ, SB_BK), 1)
    rr = lax.broadcasted_iota(jnp.int32, (SB_BK, SB_BK), 0)
    cc = lax.broadcasted_iota(jnp.int32, (SB_BK, SB_BK), 1)
    lane = lax.broadcasted_iota(jnp.int32, (SB_BQ, 128), 1)
    assert 2 * nkb <= 128
    return colk < row, rr, cc, lane < HEAD_DIM, lane


def _split_heads(t):
    first = lax.broadcasted_iota(jnp.int32, t.shape, 1) < HEAD_DIM
    zero = jnp.zeros_like(t)
    return jnp.where(first, t, zero), jnp.where(first, zero, t)


def _sb_fwd(qkv):
    s = qkv.shape[0]
    nq, nkb = s // SB_BQ, s // SB_BK
    zscale = LOG2E / math.sqrt(HEAD_DIM)

    def body(q_ref, k_ref, v_ref, o_ref, carry_ref, zs_scr, a_scr, acc_scr, cl_scr):
        i = pl.program_id(1)
        causal, rr, cc, _, lane = _sb_consts(nkb)
        later = (rr > cc).astype(BF16)
        qh = _split_heads(q_ref[...])

        def rows(j):
            return pl.ds(pl.multiple_of(j * SB_BK, SB_BK), SB_BK)

        def scores_to(slot, j):
            kb = k_ref[rows(j), :]
            for hh in range(2):
                zs_scr[slot, hh] = _dot_nt(qh[hh], kb) * zscale

        def weights(slot, j, masked):
            xs, sums, sufs = [], [], []
            for hh in range(2):
                zs = zs_scr[slot, hh]
                l = _sb_log_terms(zs)
                if masked:
                    l = jnp.where(causal, l, 0.0)
                xs.append(zs + l)
                sums.append(jnp.sum(l, axis=1, keepdims=True))
                sufs.append(_dot_f32_by_01(l, later, 2))
            for hh in range(2):
                cl = cl_scr[hh]
                a = jnp.exp2(xs[hh] + (sufs[hh] + jnp.concatenate([cl, cl], axis=1)))
                if masked:
                    a = jnp.where(causal, a, 0.0)
                a_scr[slot, :, hh * SB_BK:(hh + 1) * SB_BK] = a.astype(BF16)
            for hh in range(2):
                cl = cl_scr[hh]
                carry_ref[0] = jnp.where(lane == j + hh * nkb, cl, carry_ref[0])
                cl_scr[hh] = cl + sums[hh]

        def add_av(slot, j):
            v0, v1 = _split_heads(v_ref[rows(j), :])
            acc_scr[...] += _dot_nn(a_scr[slot], jnp.concatenate([v0, v1], axis=0))

        acc_scr[...] = jnp.zeros_like(acc_scr)
        cl_scr[...] = jnp.zeros_like(cl_scr)
        carry_ref[...] = jnp.zeros_like(carry_ref)
        scores_to(0, i)
        scores_to(1, jnp.maximum(i - 1, 0))
        weights(0, i, True)

        def step(j, prev, cur):
            scores_to(prev, jnp.maximum(j - 1, 0))
            add_av(prev, j + 1)
            weights(cur, j, False)

        def two_steps(u, _):
            j = i - 1 - 2 * u
            step(j, 0, 1)
            step(j - 1, 1, 0)
            return 0

        lax.fori_loop(0, i // 2, two_steps, 0)

        @pl.when(i % 2 == 1)
        def _():
            step(0, 0, 1)
            add_av(1, 0)

        @pl.when(i % 2 == 0)
        def _():
            add_av(0, 0)

        o_ref[...] = acc_scr[...]

    def full(which):
        return pl.BlockSpec((s, 128), lambda p, i: (0, SB_COL0 + 4 * which + p))

    return pl.pallas_call(
        body,
        name="sb_fwd",
        grid=(SB_PAIRS, nq),
        in_specs=[pl.BlockSpec((SB_BQ, 128), lambda p, i: (i, SB_COL0 + p)), full(1), full(2)],
        out_specs=[pl.BlockSpec((SB_BQ, 128), lambda p, i: (i, p)),
                   pl.BlockSpec((1, SB_BQ, 128), lambda p, i: (p, i, 0))],
        out_shape=[jax.ShapeDtypeStruct((s, SB_W), F32), jax.ShapeDtypeStruct((SB_PAIRS, s, 128), F32)],
        scratch_shapes=[pltpu.VMEM((2, 2, SB_BQ, SB_BK), F32), pltpu.VMEM((2, SB_BQ, 2 * SB_BK), BF16),
                        pltpu.VMEM((SB_BQ, 128), F32), pltpu.VMEM((2, SB_BQ, 128), F32)],
        compiler_params=_cparams(("parallel", "parallel")),
    )(qkv, qkv, qkv)


def _sb_bwd(qkv, do, carries):
    s = qkv.shape[0]
    nq, nkb = s // SB_BQ, s // SB_BK
    scale = 1.0 / math.sqrt(HEAD_DIM)
    zscale = LOG2E * scale

    def body(q_ref, k_ref, v_ref, do_ref, carry_ref, dq_ref, dk_ref, dv_ref, zs_scr, da_scr, dz_scr, a_scr, cg_scr):
        i = pl.program_id(1)

        @pl.when(i == 0)
        def _():
            dk_ref[...] = jnp.zeros_like(dk_ref)
            dv_ref[...] = jnp.zeros_like(dv_ref)

        causal, rr, cc, first, lane = _sb_consts(nkb)
        later = (rr > cc).astype(BF16)
        earlier = (rr < cc).astype(BF16)
        q2 = q_ref[...]
        qh = _split_heads(q2)
        do2 = do_ref[...].astype(BF16)
        doh = _split_heads(do2)
        ctile = carry_ref[0]

        def rows(j):
            return pl.ds(pl.multiple_of(j * SB_BK, SB_BK), SB_BK)

        def products_to(slot, j):
            kb, vb = k_ref[rows(j), :], v_ref[rows(j), :]
            for hh in range(2):
                zs_scr[slot, hh] = _dot_nt(qh[hh], kb) * zscale
                da_scr[slot, hh] = _dot_nt(doh[hh], vb)

        def by_head(t):
            return jnp.where(first, t[:SB_BK], t[SB_BK:])

        def apply(slot, j):
            k0, k1 = _split_heads(k_ref[rows(j), :])
            dq_ref[...] += _dot_nn(dz_scr[slot], jnp.concatenate([k0, k1], axis=0))
            dk_ref[rows(j), :] += by_head(_dot_tn(dz_scr[slot], q2))
            dv_ref[rows(j), :] += by_head(_dot_tn(a_scr[slot], do2))

        def grads(slot, j, masked):
            xs, sigs, sufs = [], [], []
            for hh in range(2):
                zs = zs_scr[slot, hh]
                l = _sb_log_terms(zs)
                sigs.append(jnp.exp2(zs + l))
                if masked:
                    l = jnp.where(causal, l, 0.0)
                xs.append(zs + l)
                sufs.append(_dot_f32_by_01(l, later, 2))
            gs, gpres = [], []
            for hh in range(2):
                cl = jnp.sum(jnp.where(lane == j + hh * nkb, ctile, 0.0), axis=1, keepdims=True)
                a = jnp.exp2(xs[hh] + (sufs[hh] + cl))
                if masked:
                    a = jnp.where(causal, a, 0.0)
                g = a * da_scr[slot, hh]
                a_scr[slot, :, hh * SB_BK:(hh + 1) * SB_BK] = a.astype(BF16)
                gs.append(g)
                gpres.append(_dot_f32_by_01(g, earlier, 2))
            for hh in range(2):
                cg = cg_scr[hh]
                dz = gs[hh] - (gs[hh] + (gpres[hh] + jnp.concatenate([cg, cg], axis=1))) * sigs[hh]
                if masked:
                    dz = jnp.where(causal, dz, 0.0)
                dz_scr[slot, :, hh * SB_BK:(hh + 1) * SB_BK] = (dz * scale).astype(BF16)
                cg_scr[hh] = cg + jnp.sum(gs[hh], axis=1, keepdims=True)

        dq_ref[...] = jnp.zeros_like(dq_ref)
        cg_scr[...] = jnp.zeros_like(cg_scr)
        dz_scr[1] = jnp.zeros((SB_BQ, 2 * SB_BK), BF16)
        a_scr[1] = jnp.zeros((SB_BQ, 2 * SB_BK), BF16)
        products_to(0, 0)

        def step(j, cur, nxt):
            products_to(nxt, j + 1)
            apply(nxt, jnp.maximum(j - 1, 0))
            grads(cur, j, False)

        def two_steps(u, _):
            step(2 * u, 0, 1)
            step(2 * u + 1, 1, 0)
            return 0

        lax.fori_loop(0, i // 2, two_steps, 0)

        def last(cur, nxt):
            apply(nxt, jnp.maximum(i - 1, 0))
            grads(cur, i, True)
            apply(cur, i)

        @pl.when(i % 2 == 1)
        def _():
            step(i - 1, 0, 1)
            last(1, 0)

        @pl.when(i % 2 == 0)
        def _():
            last(0, 1)

    def full(which):
        return pl.BlockSpec((s, 128), lambda p, i: (0, SB_COL0 + 4 * which + p))

    qblk = pl.BlockSpec((SB_BQ, 128), lambda p, i: (i, p))
    acc = pl.BlockSpec((s, 128), lambda p, i: (0, p))
    return pl.pallas_call(
        body,
        name="sb_bwd",
        grid=(SB_PAIRS, nq),
        in_specs=[pl.BlockSpec((SB_BQ, 128), lambda p, i: (i, SB_COL0 + p)), full(1), full(2), qblk,
                  pl.BlockSpec((1, SB_BQ, 128), lambda p, i: (p, i, 0))],
        out_specs=[qblk, acc, acc],
        out_shape=[jax.ShapeDtypeStruct((s, SB_W), F32)] * 3,
        scratch_shapes=[pltpu.VMEM((2, 2, SB_BQ, SB_BK), F32), pltpu.VMEM((2, 2, SB_BQ, SB_BK), F32),
                        pltpu.VMEM((2, SB_BQ, 2 * SB_BK), BF16), pltpu.VMEM((2, SB_BQ, 2 * SB_BK), BF16),
                        pltpu.VMEM((2, SB_BQ, 128), F32)],
        compiler_params=_cparams(("parallel", "arbitrary")),
    )(qkv, qkv, qkv, do, carries)


MERGE_TILE = 256


def _group_mix(lses):
    mx = jnp.maximum(jnp.maximum(lses[0], lses[1]), lses[2])
    es = [jnp.exp(t - mx) for t in lses]
    den = es[0] + es[1] + es[2]
    return [e / den for e in es]


def _merge_fwd(o_groups, lse_groups, o_sb, gl, b_gate, w_up_dil, w_up_sb):
    s = gl.shape[0]
    t = MERGE_TILE

    def body(o0, o1, o2, l0, l1, l2, ob_ref, gl_ref, bg_ref, wd_ref, ws_ref, merged_ref, oa_ref):
        w = _group_mix([l0[...], l1[...], l2[...]])
        oa = (w[0] * o0[...] + w[1] * o1[...] + w[2] * o2[...]).astype(BF16)
        ua = _dot_nn(oa, wd_ref[...])
        ub = _dot_nn(ob_ref[...].astype(BF16), ws_ref[...])
        gate = jax.nn.sigmoid(gl_ref[...] + bg_ref[...])
        merged_ref[...] = (gate[:, :D_MODEL] * ua + gate[:, D_MODEL:] * ub).astype(BF16)
        oa_ref[...] = oa

    dil = pl.BlockSpec((t, DIL_W), lambda i: (i, 0))
    const = lambda shape: pl.BlockSpec(shape, lambda i: (0, 0))
    return pl.pallas_call(
        body,
        name="merge_fwd",
        grid=(s // t,),
        in_specs=[dil] * 6 + [pl.BlockSpec((t, SB_W), lambda i: (i, 0)), pl.BlockSpec((t, GATE_W), lambda i: (i, 0)),
                              const((1, GATE_W)), const((DIL_W, D_MODEL)), const((SB_W, D_MODEL))],
        out_specs=[pl.BlockSpec((t, D_MODEL), lambda i: (i, 0)), dil],
        out_shape=[jax.ShapeDtypeStruct((s, D_MODEL), BF16), jax.ShapeDtypeStruct((s, DIL_W), BF16)],
        compiler_params=_cparams(("parallel",)),
    )(*o_groups, *lse_groups, o_sb, gl, b_gate, w_up_dil, w_up_sb)


def _merge_bwd(dmerged, o_groups, lse_groups, o_sb, gl, b_gate, w_up_dil, w_up_sb):
    s = gl.shape[0]
    t = MERGE_TILE

    def body(dm_ref, o0, o1, o2, l0, l1, l2, ob_ref, gl_ref, bg_ref, wd_ref, ws_ref,
             dua_ref, dub_ref, dgl_ref, dbg_ref, dosb_ref, d0, d1, d2, c0, c1, c2):
        i = pl.program_id(0)
        og = [o0[...], o1[...], o2[...]]
        w = _group_mix([l0[...], l1[...], l2[...]])
        oa = (w[0] * og[0] + w[1] * og[1] + w[2] * og[2]).astype(BF16)
        ua = _dot_nn(oa, wd_ref[...])
        ub = _dot_nn(ob_ref[...].astype(BF16), ws_ref[...])
        gate = jax.nn.sigmoid(gl_ref[...] + bg_ref[...])
        ga, gb = gate[:, :D_MODEL], gate[:, D_MODEL:]
        dm = dm_ref[...]
        dua = (dm * ga).astype(BF16)
        dub = (dm * gb).astype(BF16)
        dua_ref[...] = dua
        dub_ref[...] = dub
        dgl_a = dm * ua * ga * (1.0 - ga)
        dgl_b = dm * ub * gb * (1.0 - gb)
        dgl_ref[:, :D_MODEL] = dgl_a.astype(BF16)
        dgl_ref[:, D_MODEL:] = dgl_b.astype(BF16)
        part = jnp.concatenate([jnp.sum(dgl_a.reshape(t // 8, 8, D_MODEL), axis=0),
                                jnp.sum(dgl_b.reshape(t // 8, 8, D_MODEL), axis=0)], axis=1)

        @pl.when(i == 0)
        def _():
            dbg_ref[...] = part

        @pl.when(i > 0)
        def _():
            dbg_ref[...] += part

        dosb_ref[...] = _dot_nt(dub, ws_ref[...])
        doa = _dot_nt(dua, wd_ref[...])
        rr = lax.broadcasted_iota(jnp.int32, (DIL_W, DIL_W), 0) // HEAD_DIM
        cc = lax.broadcasted_iota(jnp.int32, (DIL_W, DIL_W), 1) // HEAD_DIM
        same_head = (rr == cc).astype(BF16)
        dw = [_dot_f32_by_01(doa * og[g], same_head) for g in range(3)]
        mean_dw = w[0] * dw[0] + w[1] * dw[1] + w[2] * dw[2]
        for g, (d_ref, c_ref) in enumerate(((d0, c0), (d1, c1), (d2, c2))):
            d_ref[...] = w[g] * doa
            c_ref[...] = -w[g] * mean_dw

    dil = pl.BlockSpec((t, DIL_W), lambda i: (i, 0))
    wide = pl.BlockSpec((t, D_MODEL), lambda i: (i, 0))
    gate2 = pl.BlockSpec((t, GATE_W), lambda i: (i, 0))
    sbw = pl.BlockSpec((t, SB_W), lambda i: (i, 0))
    const = lambda shape: pl.BlockSpec(shape, lambda i: (0, 0))
    return pl.pallas_call(
        body,
        name="merge_bwd",
        grid=(s // t,),
        in_specs=[wide] + [dil] * 6 + [sbw, gate2, const((1, GATE_W)), const((DIL_W, D_MODEL)), const((SB_W, D_MODEL))],
        out_specs=[wide, wide, gate2, const((8, GATE_W)), sbw] + [dil] * 6,
        out_shape=[jax.ShapeDtypeStruct((s, D_MODEL), BF16), jax.ShapeDtypeStruct((s, D_MODEL), BF16),
                   jax.ShapeDtypeStruct((s, GATE_W), BF16), jax.ShapeDtypeStruct((8, GATE_W), F32),
                   jax.ShapeDtypeStruct((s, SB_W), F32)] + [jax.ShapeDtypeStruct((s, DIL_W), F32)] * 6,
        compiler_params=_cparams(("arbitrary",)),
    )(dmerged, *o_groups, *lse_groups, o_sb, gl, b_gate, w_up_dil, w_up_sb)


ANY = pl.BlockSpec(memory_space=pl.ANY)


def _place():
    x, y, c = lax.axis_index("x"), lax.axis_index("y"), lax.axis_index("c")
    other_chips = [(1 - x, y), (x, 1 - y), (1 - x, 1 - y)]
    return x, y, c, other_chips


def _all_gather_weights(pack):
    r, wd = pack.shape
    rh = r // 2
    pack = pack.reshape(2, rh, wd)

    def body(p_ref, out_ref, send_sems, recv_sems):
        x, y, c, chips = _place()
        me, sibling = 2 * x + y, (x, y, 1 - c)

        def half(chip_idx, core):
            return out_ref.at[chip_idx, core]

        def copy(k, chip_idx, core, to, src=None):
            return pltpu.make_async_remote_copy(
                src_ref=half(chip_idx, core) if src is None else src, dst_ref=half(chip_idx, core),
                send_sem=send_sems.at[k], recv_sem=recv_sems.at[k], device_id=to, device_id_type=MESH)

        first = [copy(j, me, c, (*chip, c), src=p_ref.at[c]) for j, chip in enumerate(chips)]
        for cp in first:
            cp.start()
        passed = [copy(3 + j, 2 * chip[0] + chip[1], c, sibling) for j, chip in enumerate(chips)]
        for j, chip in enumerate(chips):
            copy(j, 2 * chip[0] + chip[1], c, (x, y, c)).wait_recv()
            passed[j].start()
        for j, chip in enumerate(chips):
            copy(3 + j, 2 * chip[0] + chip[1], 1 - c, (x, y, c)).wait_recv()
        for cp in first + passed:
            cp.wait_send()

    others = pl.pallas_call(
        body,
        name="all_gather_weights",
        in_specs=[ANY],
        out_specs=ANY,
        out_shape=jax.ShapeDtypeStruct((N_CHIPS, 2, rh, wd), pack.dtype),
        scratch_shapes=[pltpu.SemaphoreType.DMA((6,)), pltpu.SemaphoreType.DMA((6,))],
    )(pack)
    me = 2 * lax.axis_index("x") + lax.axis_index("y")
    return lax.dynamic_update_slice(others, pack[None], (me, 0, 0, 0)).reshape(N_CHIPS, r, wd)


def _swap_halves(g):
    n, r, wd = g.shape
    rh = r // 2
    g = g.reshape(n, 2, rh, wd)

    def body(g_ref, out_ref, send_sem, recv_sem):
        x, y, c, _ = _place()
        cp = pltpu.make_async_remote_copy(
            src_ref=g_ref.at[:, 1 - c], dst_ref=out_ref,
            send_sem=send_sem, recv_sem=recv_sem, device_id=(x, y, 1 - c), device_id_type=MESH)
        cp.start()
        cp.wait()

    return pl.pallas_call(
        body,
        name="grad_swap_halves",
        in_specs=[ANY],
        out_specs=ANY,
        out_shape=jax.ShapeDtypeStruct((n, rh, wd), g.dtype),
        scratch_shapes=[pltpu.SemaphoreType.DMA, pltpu.SemaphoreType.DMA],
    )(g)


def _add_halves(g, got, core):
    n, r, wd = g.shape
    rh = r // 2
    t = rh // 4
    nt = rh // t

    def body(c_ref, a_ref, b_ref, o_ref):
        o_ref[...] = (a_ref[0] + b_ref[...]).astype(BF16)

    grid_spec = pltpu.PrefetchScalarGridSpec(
        num_scalar_prefetch=1,
        grid=(n, nt),
        in_specs=[pl.BlockSpec((1, 1, t, wd), lambda s, i, c: (s, c[0], i, 0)),
                  pl.BlockSpec((1, t, wd), lambda s, i, c: (s, i, 0))],
        out_specs=pl.BlockSpec((1, t, wd), lambda s, i, c: (s, i, 0)),
    )
    return pl.pallas_call(
        body,
        name="grad_add_halves",
        grid_spec=grid_spec,
        out_shape=jax.ShapeDtypeStruct((n, rh, wd), BF16),
        compiler_params=_cparams(("parallel", "parallel")),
    )(core, g.reshape(n, 2, rh, wd), got)


def _exchange_chunks(h):
    n, rh, wd = h.shape

    def body(h_ref, out_ref, send_sems, recv_sems):
        x, y, c, chips = _place()
        me = 2 * x + y
        sends = []
        for j, chip in enumerate(chips):
            them = 2 * chip[0] + chip[1]
            sends.append(pltpu.make_async_remote_copy(
                src_ref=h_ref.at[them], dst_ref=out_ref.at[me],
                send_sem=send_sems.at[j], recv_sem=recv_sems.at[j], device_id=(*chip, c), device_id_type=MESH))
        for cp in sends:
            cp.start()
        for j, chip in enumerate(chips):
            them = 2 * chip[0] + chip[1]
            pltpu.make_async_remote_copy(
                src_ref=h_ref.at[them], dst_ref=out_ref.at[them],
                send_sem=send_sems.at[j], recv_sem=recv_sems.at[j], device_id=(*chip, c), device_id_type=MESH).wait_recv()
        for cp in sends:
            cp.wait_send()

    return pl.pallas_call(
        body,
        name="grad_exchange_chunks",
        in_specs=[ANY],
        out_specs=ANY,
        out_shape=jax.ShapeDtypeStruct((n, rh, wd), h.dtype),
        scratch_shapes=[pltpu.SemaphoreType.DMA((3,)), pltpu.SemaphoreType.DMA((3,))],
    )(h)


def _sum_chips(b, h, chip):
    n, rh, wd = b.shape
    t = rh // 4

    def body(chip_ref, b_ref, own_ref, o_ref):
        own = own_ref[0]
        s0, s1, s2, s3 = (jnp.where(chip_ref[0] == k, own, b_ref[k]).astype(F32) for k in range(n))
        o_ref[...] = ((s0 + s1) + s2) + s3

    grid_spec = pltpu.PrefetchScalarGridSpec(
        num_scalar_prefetch=1,
        grid=(rh // t,),
        in_specs=[pl.BlockSpec((n, t, wd), lambda i, chip: (0, i, 0)),
                  pl.BlockSpec((1, t, wd), lambda i, chip: (chip[0], i, 0))],
        out_specs=pl.BlockSpec((t, wd), lambda i, chip: (i, 0)),
    )
    return pl.pallas_call(
        body,
        name="grad_sum_chips",
        grid_spec=grid_spec,
        out_shape=jax.ShapeDtypeStruct((rh, wd), F32),
        compiler_params=_cparams(("parallel",)),
    )(chip, b, h)


def _join_halves(tc):
    rh, wd = tc.shape

    def body(t_ref, out_ref, send_sem, recv_sem):
        x, y, c, _ = _place()
        cp = pltpu.make_async_remote_copy(
            src_ref=t_ref, dst_ref=out_ref.at[c],
            send_sem=send_sem, recv_sem=recv_sem, device_id=(x, y, 1 - c), device_id_type=MESH)
        cp.start()
        cp.wait()

    halves = pl.pallas_call(
        body,
        name="grad_join_halves",
        in_specs=[ANY],
        out_specs=ANY,
        out_shape=jax.ShapeDtypeStruct((2, rh, wd), tc.dtype),
        scratch_shapes=[pltpu.SemaphoreType.DMA, pltpu.SemaphoreType.DMA],
    )(tc)
    return lax.dynamic_update_slice(halves, tc[None], (lax.axis_index("c"), 0, 0)).reshape(2 * rh, wd)


def _all_reduce_small(pack):
    rows, lanes = pack.shape

    def body(p_ref, out_ref, buf, send_sems, recv_sems):
        x, y, c, _ = _place()
        me = 4 * x + 2 * y + c
        buf[me] = p_ref[...]
        sends = []
        for k in range(1, N_DEV):
            peer = (x ^ (k >> 2), y ^ ((k >> 1) & 1), c ^ (k & 1))
            sends.append(pltpu.make_async_remote_copy(
                src_ref=p_ref, dst_ref=buf.at[me], send_sem=send_sems.at[k - 1], recv_sem=recv_sems.at[k - 1],
                device_id=peer, device_id_type=MESH))
        for cp in sends:
            cp.start()
        for k in range(1, N_DEV):
            pltpu.make_async_remote_copy(
                src_ref=p_ref, dst_ref=buf.at[me ^ k], send_sem=send_sems.at[k - 1], recv_sem=recv_sems.at[k - 1],
                device_id=(x, y, c), device_id_type=MESH).wait_recv()
        for cp in sends:
            cp.wait_send()
        total = buf[0]
        for d in range(1, N_DEV):
            total = total + buf[d]
        out_ref[...] = total

    vm = pl.BlockSpec(memory_space=pltpu.VMEM)
    return pl.pallas_call(
        body,
        name="all_reduce_small",
        in_specs=[vm],
        out_specs=vm,
        out_shape=jax.ShapeDtypeStruct((rows, lanes), F32),
        scratch_shapes=[pltpu.VMEM((N_DEV, rows, lanes), F32), pltpu.SemaphoreType.DMA((N_DEV - 1,)),
                        pltpu.SemaphoreType.DMA((N_DEV - 1,))],
    )(pack)


def _adamw(g, w, m, v, name):
    rows, cols = g.shape
    t = rows
    for cand in (256, 128, 64, 32, 16, 8):
        if rows % cand == 0:
            t = cand
            break

    def body(g_ref, w_ref, m_ref, v_ref, d_ref, nm_ref, nv_ref):
        gv = g_ref[...]
        mv = ADAM_B1 * m_ref[...] + (1.0 - ADAM_B1) * gv
        vv = ADAM_B2 * v_ref[...] + (1.0 - ADAM_B2) * (gv * gv)
        m_hat = mv / (1.0 - ADAM_B1 ** ADAM_STEP)
        v_hat = vv / (1.0 - ADAM_B2 ** ADAM_STEP)
        d_ref[...] = -ADAM_LR * (m_hat / (jnp.sqrt(v_hat) + ADAM_EPS) + ADAM_WD * w_ref[...])
        nm_ref[...] = mv
        nv_ref[...] = vv

    blk = pl.BlockSpec((t, cols), lambda i: (i, 0))
    return pl.pallas_call(
        body,
        name=name,
        grid=(rows // t,),
        in_specs=[blk] * 4,
        out_specs=[blk] * 3,
        out_shape=[jax.ShapeDtypeStruct((rows, cols), F32)] * 3,
        compiler_params=_cparams(("parallel",)),
    )(g, w, m, v)


PACK_W = 1024
BIG = (("w_in", (D_MODEL, IN_COLS), 1), ("w_up_dil", (DIL_W, D_MODEL), 1), ("w_up_sb", (SB_W, D_MODEL), 1),
       ("w_out", (D_MODEL, D_MODEL), 0), ("w_mlp_in", (D_MODEL, D_FF), 1), ("w_mlp_out", (D_FF, D_MODEL), 0))


def _shard_shape(shape, axis):
    return tuple(d // N_CHIPS if a == axis else d for a, d in enumerate(shape))


def _pack_rows():
    rows, at = {}, 0
    for name, shape, axis in BIG:
        n = math.prod(_shard_shape(shape, axis)) // PACK_W
        rows[name] = (at, n)
        at += n
    return rows, at


def _pack_shards(shards):
    return jnp.concatenate([shards[name].reshape(-1, PACK_W) for name, _, _ in BIG], axis=0)


def _unpack_full(gathered):
    rows, _ = _pack_rows()
    full = {}
    for name, shape, axis in BIG:
        at, n = rows[name]
        parts = gathered[:, at:at + n, :].reshape((N_CHIPS,) + _shard_shape(shape, axis))
        if axis == 0:
            full[name] = parts.reshape(shape)
        else:
            full[name] = jnp.transpose(parts, (1, 0, 2)).reshape(shape)
    return full


def _pack_full_grads(grads):
    chunks = []
    for name, shape, axis in BIG:
        g = grads[name]
        if axis == 0:
            parts = g.reshape((N_CHIPS, shape[0] // N_CHIPS, shape[1]))
        else:
            parts = jnp.transpose(g.reshape((shape[0], N_CHIPS, shape[1] // N_CHIPS)), (1, 0, 2))
        chunks.append(parts.reshape(N_CHIPS, -1, PACK_W))
    return jnp.concatenate(chunks, axis=1)


def _unpack_shard(packed):
    rows, _ = _pack_rows()
    return {name: packed[rows[name][0]:rows[name][0] + rows[name][1]].reshape(_shard_shape(shape, axis))
            for name, shape, axis in BIG}


def _local_step(x, target, w, norm_mix_g, b_gate, norm_mlp_g, norm_final_g):
    w_qkv, w_gate = w["w_in"][:, :QKV_W], w["w_in"][:, QKV_W:]

    h = _rms_fwd(x, norm_mix_g, "norm_mix")
    (qkv,) = _matmul(h, w_qkv, mode="nn", out_dtypes=(BF16,), name="proj_qkv", tn=768)
    (gl,) = _matmul(h, w_gate, mode="nn", out_dtypes=(F32,), name="proj_gate")
    dil = [_dil_fwd(qkv, g) for g in range(3)]
    o_groups, lse_groups = [d[0] for d in dil], [d[1] for d in dil]
    o_sb, carries = _sb_fwd(qkv)
    merged, o_a = _merge_fwd(o_groups, lse_groups, o_sb, gl, b_gate, w["w_up_dil"], w["w_up_sb"])
    (x1,) = _matmul(merged, w["w_out"], mode="nn", out_dtypes=(F32,), name="out_proj",
                    extras=(x,), epilogue=lambda acc, res: (res + acc,))
    h2 = _rms_fwd(x1, norm_mlp_g, "norm_mlp")
    u, act = _matmul(h2, w["w_mlp_in"], mode="nn", out_dtypes=(F32, BF16), name="mlp_in",
                     epilogue=lambda acc: (acc, jnp.square(jnp.maximum(acc, 0.0))))
    (x2,) = _matmul(act, w["w_mlp_out"], mode="nn", out_dtypes=(F32,), name="mlp_out", tk=2048,
                    extras=(x1,), epilogue=lambda acc, res: (res + acc,))
    dx2, dg_final, loss_part = _loss_head(x2, norm_final_g.reshape(1, D_MODEL), target)

    (du,) = _matmul(dx2, w["w_mlp_out"], mode="nt", out_dtypes=(BF16,), name="mlp_out_dx",
                    extras=(u,), epilogue=lambda acc, uu: (acc * (2.0 * jnp.maximum(uu, 0.0)),))
    (g_mlp_out,) = _matmul(act, dx2, mode="tn", out_dtypes=(F32,), name="mlp_out_dw")
    (g_mlp_in,) = _matmul(h2, du, mode="tn", out_dtypes=(F32,), name="mlp_in_dw")
    (dh2,) = _matmul(du, w["w_mlp_in"], mode="nt", out_dtypes=(F32,), name="mlp_in_dx", tk=2048)
    dx1, dg_mlp = _rms_bwd(dh2, x1, norm_mlp_g, dx2, "norm_mlp_bwd")

    (dmerged,) = _matmul(dx1, w["w_out"], mode="nt", out_dtypes=(F32,), name="out_proj_dx")
    (g_out,) = _matmul(merged, dx1, mode="tn", out_dtypes=(F32,), name="out_proj_dw")
    mb = _merge_bwd(dmerged, o_groups, lse_groups, o_sb, gl, b_gate, w["w_up_dil"], w["w_up_sb"])
    dua, dub, dgl, dbg, do_sb = mb[:5]
    do_groups, c_groups = mb[5:8], mb[8:11]
    (g_up_dil,) = _matmul(o_a, dua, mode="tn", out_dtypes=(F32,), name="up_dil_dw")
    (g_up_sb,) = _matmul(o_sb, dub, mode="tn", out_dtypes=(F32,), name="up_sb_dw")
    dq_sb, dk_sb, dv_sb = _sb_bwd(qkv, do_sb, carries)
    dil_b = [_dil_bwd(qkv, do_groups[g], lse_groups[g], c_groups[g], g) for g in range(3)]
    dproj = jnp.concatenate(
        [dil_b[g][i].astype(BF16) for i in range(3) for g in range(3)]
        + [t.astype(BF16) for t in (dq_sb, dk_sb, dv_sb)] + [dgl], axis=1)
    (g_in,) = _matmul(h, dproj, mode="tn", out_dtypes=(F32,), name="proj_dw", tm=512, tn=IN_COLS // 2)
    (dh,) = _matmul(dproj, w["w_in"], mode="nt", out_dtypes=(F32,), name="proj_dx", tk=IN_COLS // 2)
    grad_x, dg_mix = _rms_bwd(dh, x, norm_mix_g, dx1, "norm_mix_bwd")

    big = {"w_in": g_in, "w_up_dil": g_up_dil, "w_up_sb": g_up_sb, "w_out": g_out,
           "w_mlp_in": g_mlp_in, "w_mlp_out": g_mlp_out}
    small = (dg_mix, dbg, dg_mlp, dg_final, loss_part)
    return grad_x, big, small


def kernel(x, norm_mix_g, w_in, b_gate, w_up_dil, w_up_sb, w_out, norm_mlp_g, w_mlp_in, w_mlp_out, norm_final_g, loss_target, m_norm_mix_g, m_w_in, m_b_gate, m_w_up_dil, m_w_up_sb, m_w_out, m_norm_mlp_g, m_w_mlp_in, m_w_mlp_out, m_norm_final_g, v_norm_mix_g, v_w_in, v_b_gate, v_w_up_dil, v_w_up_sb, v_w_out, v_norm_mlp_g, v_w_mlp_in, v_w_mlp_out, v_norm_final_g):
    shards = {"w_in": w_in[0], "w_up_dil": w_up_dil[0], "w_up_sb": w_up_sb[0], "w_out": w_out[0],
              "w_mlp_in": w_mlp_in[0], "w_mlp_out": w_mlp_out[0]}
    moments_m = {"w_in": m_w_in[0], "w_up_dil": m_w_up_dil[0], "w_up_sb": m_w_up_sb[0], "w_out": m_w_out[0],
                 "w_mlp_in": m_w_mlp_in[0], "w_mlp_out": m_w_mlp_out[0]}
    moments_v = {"w_in": v_w_in[0], "w_up_dil": v_w_up_dil[0], "w_up_sb": v_w_up_sb[0], "w_out": v_w_out[0],
                 "w_mlp_in": v_w_mlp_in[0], "w_mlp_out": v_w_mlp_out[0]}

    pack = _pack_shards({n: s.astype(BF16) for n, s in shards.items()})
    full = _unpack_full(_all_gather_weights(pack))

    grad_x, big, small = _local_step(x[0], loss_target[0], full, norm_mix_g, b_gate, norm_mlp_g, norm_final_g)

    core = lax.axis_index("c").astype(jnp.int32).reshape(1)
    gpack = _pack_full_grads(big)
    chip_sum = _add_halves(gpack, _swap_halves(gpack), core)
    chip = (2 * lax.axis_index("x") + lax.axis_index("y")).astype(jnp.int32).reshape(1)
    reduced = _join_halves(_sum_chips(_exchange_chunks(chip_sum), chip_sum, chip))
    g_shard = _unpack_shard(reduced)

    dg_mix, dbg, dg_mlp, dg_final, loss_part = small
    loss_row = jnp.sum(loss_part, axis=0, keepdims=True)
    small_pack = jnp.concatenate(
        [jnp.sum(dg_mix, axis=0, keepdims=True), jnp.sum(dbg, axis=0, keepdims=True),
         jnp.sum(dg_mlp, axis=0, keepdims=True), jnp.sum(dg_final, axis=0, keepdims=True), loss_row], axis=1)
    n_small = small_pack.shape[1]
    small_sum = _all_reduce_small(small_pack.reshape(n_small // 128, 128)).reshape(1, n_small)
    g_norm_mix = small_sum[:, :D_MODEL]
    g_b_gate = small_sum[:, D_MODEL:3 * D_MODEL]
    g_norm_mlp = small_sum[:, 3 * D_MODEL:4 * D_MODEL]
    g_norm_final = small_sum[:, 4 * D_MODEL:5 * D_MODEL]
    loss = jnp.sum(small_sum[:, 5 * D_MODEL:])

    names = ["norm_mix_g", "w_in", "b_gate", "w_up_dil", "w_up_sb", "w_out", "norm_mlp_g", "w_mlp_in", "w_mlp_out",
             "norm_final_g"]
    grads = dict(g_shard)
    grads.update(norm_mix_g=g_norm_mix, b_gate=g_b_gate, norm_mlp_g=g_norm_mlp, norm_final_g=g_norm_final)
    weights = dict(shards)
    weights.update(norm_mix_g=norm_mix_g, b_gate=b_gate, norm_mlp_g=norm_mlp_g, norm_final_g=norm_final_g.reshape(1, D_MODEL))
    ms = dict(moments_m)
    ms.update(norm_mix_g=m_norm_mix_g, b_gate=m_b_gate, norm_mlp_g=m_norm_mlp_g, norm_final_g=m_norm_final_g.reshape(1, D_MODEL))
    vs = dict(moments_v)
    vs.update(norm_mix_g=v_norm_mix_g, b_gate=v_b_gate, norm_mlp_g=v_norm_mlp_g, norm_final_g=v_norm_final_g.reshape(1, D_MODEL))

    out_shapes = {"norm_mix_g": norm_mix_g.shape, "w_in": w_in.shape, "b_gate": b_gate.shape, "w_up_dil": w_up_dil.shape,
                  "w_up_sb": w_up_sb.shape, "w_out": w_out.shape, "norm_mlp_g": norm_mlp_g.shape,
                  "w_mlp_in": w_mlp_in.shape, "w_mlp_out": w_mlp_out.shape, "norm_final_g": norm_final_g.shape}
    g_out, d_out, m_out, v_out = [], [], [], []
    for n in names:
        d, nm, nv = _adamw(grads[n], weights[n], ms[n], vs[n], "adamw_" + n)
        shape = out_shapes[n]
        g_out.append(grads[n].reshape(shape))
        d_out.append(d.reshape(shape))
        m_out.append(nm.reshape(shape))
        v_out.append(nv.reshape(shape))
    return (loss, grad_x.reshape(x.shape), *g_out, *d_out, *m_out, *v_out)
```

```python
import functools
import math

import jax
import jax.numpy as jnp
import numpy as np
from jax import lax
from jax.experimental import pallas as pl
from jax.experimental.pallas import tpu as pltpu

F32 = jnp.float32
BF16 = jnp.bfloat16
MESH = pl.DeviceIdType.MESH

D_MODEL = 1024
HEAD_DIM = 64
DIL_GROUPS = ((128, 1), (512, 4), (2048, 16))
DIL_HEADS = 4
DIL_W = 256
N_DIL_HEADS = 12
SB_HEADS = 8
SB_W = SB_HEADS * HEAD_DIM
QKV_W = 3 * 3 * DIL_W + 3 * SB_W
GATE_W = 2 * D_MODEL
IN_COLS = QKV_W + GATE_W
D_FF = 4 * D_MODEL
BLOCK = 128
RMS_EPS = 1e-6
NEG_INF = -1e30
N_CHIPS = 4
N_DEV = 8

ADAM_LR = 0.001
ADAM_B1 = 0.9
ADAM_B2 = 0.999
ADAM_EPS = 1e-08
ADAM_WD = 0.01
ADAM_STEP = 10

VMEM_LIMIT = 56 * 1024 * 1024

SB_BQ = 256
SB_BK = 256


def _cparams(sem=None):
    if sem is None:
        return pltpu.CompilerParams(vmem_limit_bytes=VMEM_LIMIT)
    return pltpu.CompilerParams(dimension_semantics=sem, vmem_limit_bytes=VMEM_LIMIT)


def _dot(a, b, dims):
    return lax.dot_general(a, b, (dims, ((), ())), preferred_element_type=F32)


def _dot_nn(a, b):
    return _dot(a, b, ((1,), (0,)))


def _dot_nt(a, b):
    return _dot(a, b, ((1,), (1,)))


def _dot_tn(a, b):
    return _dot(a, b, ((0,), (0,)))


def _dot_f32_by_01(x, m01, pieces=3):
    hi = x.astype(BF16)
    r1 = x - hi.astype(F32)
    mid = r1.astype(BF16)
    if pieces == 2:
        return _dot_nn(hi, m01) + _dot_nn(mid, m01)
    lo = (r1 - mid.astype(F32)).astype(BF16)
    return _dot_nn(hi, m01) + _dot_nn(mid, m01) + _dot_nn(lo, m01)


def _matmul(a, b, *, mode, out_dtypes, name, tm=1024, tn=1024, tk=1024, extras=(), epilogue=None):
    if mode == "nn":
        (m, k), (k2, n) = a.shape, b.shape
    elif mode == "nt":
        (m, k), (n, k2) = a.shape, b.shape
    else:
        (k, m), (k2, n) = a.shape, b.shape
    assert k == k2, (a.shape, b.shape, mode)
    tm, tn, tk = min(tm, m), min(tn, n), min(tk, k)
    assert m % tm == 0 and n % tn == 0 and k % tk == 0, (m, n, k, tm, tn, tk)
    nk = k // tk
    n_out = len(out_dtypes)
    n_ex = len(extras)

    if mode == "nn":
        a_spec = pl.BlockSpec((tm, tk), lambda i, j, kk: (i, kk))
        b_spec = pl.BlockSpec((tk, tn), lambda i, j, kk: (kk, j))
        dot = _dot_nn
    elif mode == "nt":
        a_spec = pl.BlockSpec((tm, tk), lambda i, j, kk: (i, kk))
        b_spec = pl.BlockSpec((tn, tk), lambda i, j, kk: (j, kk))
        dot = _dot_nt
    else:
        a_spec = pl.BlockSpec((tk, tm), lambda i, j, kk: (kk, i))
        b_spec = pl.BlockSpec((tk, tn), lambda i, j, kk: (kk, j))
        dot = _dot_tn
    mn_spec = pl.BlockSpec((tm, tn), lambda i, j, kk: (i, j))

    def body(*refs):
        a_ref, b_ref = refs[0], refs[1]
        ex_refs = refs[2:2 + n_ex]
        out_refs = refs[2 + n_ex:2 + n_ex + n_out]
        acc_ref = refs[2 + n_ex + n_out] if nk > 1 else None
        part = dot(a_ref[...].astype(BF16), b_ref[...].astype(BF16))

        def finish(acc):
            if epilogue is None:
                outs = (acc,)
            else:
                outs = epilogue(acc, *[r[...] for r in ex_refs])
            for o_ref, o in zip(out_refs, outs):
                o_ref[...] = o.astype(o_ref.dtype)

        if nk == 1:
            finish(part)
        else:
            kk = pl.program_id(2)

            @pl.when(kk == 0)
            def _():
                acc_ref[...] = part

            @pl.when(kk > 0)
            def _():
                acc_ref[...] += part

            @pl.when(kk == nk - 1)
            def _():
                finish(acc_ref[...])

    outs = pl.pallas_call(
        body,
        name=name,
        grid=(m // tm, n // tn, nk),
        in_specs=[a_spec, b_spec] + [mn_spec] * n_ex,
        out_specs=[mn_spec] * n_out,
        out_shape=[jax.ShapeDtypeStruct((m, n), dt) for dt in out_dtypes],
        scratch_shapes=[pltpu.VMEM((tm, tn), F32)] if nk > 1 else [],
        compiler_params=_cparams(("parallel", "parallel", "arbitrary")),
    )(a, b, *extras)
    return outs


ROW_TILE = 512


def _rms_fwd(x, g, name):
    s, d = x.shape

    def body(x_ref, g_ref, h_ref):
        xv = x_ref[...]
        r = lax.rsqrt(jnp.mean(xv * xv, axis=-1, keepdims=True) + RMS_EPS)
        h_ref[...] = (xv * r * g_ref[...]).astype(BF16)

    return pl.pallas_call(
        body,
        name=name,
        grid=(s // ROW_TILE,),
        in_specs=[pl.BlockSpec((ROW_TILE, d), lambda i: (i, 0)), pl.BlockSpec((1, d), lambda i: (0, 0))],
        out_specs=pl.BlockSpec((ROW_TILE, d), lambda i: (i, 0)),
        out_shape=jax.ShapeDtypeStruct((s, d), BF16),
        compiler_params=_cparams(("parallel",)),
    )(x, g)


def _rms_bwd(dh, x, g, dres, name):
    s, d = x.shape

    def body(dh_ref, x_ref, g_ref, dres_ref, dx_ref, dg_ref):
        i = pl.program_id(0)
        xv = x_ref[...]
        r = lax.rsqrt(jnp.mean(xv * xv, axis=-1, keepdims=True) + RMS_EPS)
        xh = xv * r
        dhv = dh_ref[...]
        dxh = dhv * g_ref[...]
        dx = r * (dxh - xh * jnp.mean(dxh * xh, axis=-1, keepdims=True))
        dx_ref[...] = dres_ref[...] + dx
        part = jnp.sum((dhv * xh).reshape(ROW_TILE // 8, 8, d), axis=0)

        @pl.when(i == 0)
        def _():
            dg_ref[...] = part

        @pl.when(i > 0)
        def _():
            dg_ref[...] += part

    row = pl.BlockSpec((ROW_TILE, d), lambda i: (i, 0))
    return pl.pallas_call(
        body,
        name=name,
        grid=(s // ROW_TILE,),
        in_specs=[row, row, pl.BlockSpec((1, d), lambda i: (0, 0)), row],
        out_specs=[row, pl.BlockSpec((8, d), lambda i: (0, 0))],
        out_shape=[jax.ShapeDtypeStruct((s, d), F32), jax.ShapeDtypeStruct((8, d), F32)],
        compiler_params=_cparams(("arbitrary",)),
    )(dh, x, g, dres)


def _loss_head(x2, g, target):
    s, d = x2.shape

    def body(x_ref, g_ref, t_ref, dx_ref, dg_ref, loss_ref):
        i = pl.program_id(0)
        xv = x_ref[...]
        r = lax.rsqrt(jnp.mean(xv * xv, axis=-1, keepdims=True) + RMS_EPS)
        xh = xv * r
        gv = g_ref[...]
        err = xh * gv - t_ref[...]
        dy = err * (1.0 / d)
        dxh = dy * gv
        dx_ref[...] = r * (dxh - xh * jnp.mean(dxh * xh, axis=-1, keepdims=True))
        part_g = jnp.sum((dy * xh).reshape(ROW_TILE // 8, 8, d), axis=0)
        part_l = (0.5 / d) * jnp.sum((err * err).reshape(ROW_TILE // 8, 8, d), axis=0)

        @pl.when(i == 0)
        def _():
            dg_ref[...] = part_g
            loss_ref[...] = part_l

        @pl.when(i > 0)
        def _():
            dg_ref[...] += part_g
            loss_ref[...] += part_l

    row = pl.BlockSpec((ROW_TILE, d), lambda i: (i, 0))
    acc = pl.BlockSpec((8, d), lambda i: (0, 0))
    return pl.pallas_call(
        body,
        name="loss_head",
        grid=(s // ROW_TILE,),
        in_specs=[row, pl.BlockSpec((1, d), lambda i: (0, 0)), row],
        out_specs=[row, acc, acc],
        out_shape=[jax.ShapeDtypeStruct((s, d), F32), jax.ShapeDtypeStruct((8, d), F32),
                   jax.ShapeDtypeStruct((8, d), F32)],
        compiler_params=_cparams(("arbitrary",)),
    )(x2, g, target)


def _alibi_slopes():
    return np.exp2(np.float32(-8.0) * np.arange(1, N_DIL_HEADS + 1, dtype=np.float32) / np.float32(N_DIL_HEADS))


def _head_lane_mask(h, rows):
    lane = lax.broadcasted_iota(jnp.int32, (rows, DIL_W), 1)
    return (lane >= h * HEAD_DIM) & (lane < (h + 1) * HEAD_DIM)


def _band_terms(dil, has_prev):
    qi = lax.broadcasted_iota(jnp.int32, (BLOCK, 2 * BLOCK), 0)
    kj = lax.broadcasted_iota(jnp.int32, (BLOCK, 2 * BLOCK), 1)
    steps = qi + BLOCK - kj
    valid = (steps >= 0) & (steps <= BLOCK) & ((kj >= BLOCK) | has_prev)
    return valid, steps.astype(F32) * float(dil)


def _dil_fwd(qkv_g, group):
    _, dil = DIL_GROUPS[group]
    s = qkv_g.shape[0]
    sub = s // dil
    nb = sub // BLOCK
    view = qkv_g.reshape(sub, dil * 3 * DIL_W)
    slopes = _alibi_slopes()[group * DIL_HEADS:(group + 1) * DIL_HEADS]

    def col(which):
        return lambda r, n: (n, r * 3 + which)

    def col_prev(which):
        return lambda r, n: (jnp.maximum(n - 1, 0), r * 3 + which)

    def body(q_ref, kc_ref, kp_ref, vc_ref, vp_ref, o_ref, lse_ref):
        n = pl.program_id(1)
        valid, dist = _band_terms(dil, n > 0)
        q = q_ref[...]
        k2 = jnp.concatenate([kp_ref[...], kc_ref[...]], axis=0)
        v2 = jnp.concatenate([vp_ref[...], vc_ref[...]], axis=0)
        masks = [_head_lane_mask(h, BLOCK) for h in range(DIL_HEADS)]
        logits = [_dot_nt(jnp.where(masks[h], q, jnp.zeros_like(q)), k2) for h in range(DIL_HEADS)]
        ps, lses = [], []
        for h in range(DIL_HEADS):
            lg = jnp.where(valid, logits[h] * 0.125 - float(slopes[h]) * dist, NEG_INF)
            mx = jnp.max(lg, axis=1, keepdims=True)
            lse = mx + jnp.log(jnp.sum(jnp.exp(lg - mx), axis=1, keepdims=True))
            ps.append(jnp.exp(lg - lse).astype(BF16))
            lses.append(lse)
        o_acc = jnp.zeros((BLOCK, DIL_W), F32)
        lse_acc = jnp.zeros((BLOCK, DIL_W), F32)
        for h in range(DIL_HEADS):
            o_acc = jnp.where(masks[h], _dot_nn(ps[h], v2), o_acc)
            lse_acc = jnp.where(masks[h], lses[h], lse_acc)
        o_ref[...] = o_acc
        lse_ref[...] = lse_acc

    blk = (BLOCK, DIL_W)
    o, lse = pl.pallas_call(
        body,
        name=f"dil_fwd_g{group}",
        grid=(dil, nb),
        in_specs=[pl.BlockSpec(blk, col(0)), pl.BlockSpec(blk, col(1)), pl.BlockSpec(blk, col_prev(1)),
                  pl.BlockSpec(blk, col(2)), pl.BlockSpec(blk, col_prev(2))],
        out_specs=[pl.BlockSpec(blk, lambda r, n: (n, r))] * 2,
        out_shape=[jax.ShapeDtypeStruct((sub, dil * DIL_W), F32)] * 2,
        compiler_params=_cparams(("parallel", "parallel")),
    )(view, view, view, view, view)
    return o.reshape(s, DIL_W), lse.reshape(s, DIL_W)


def _dil_bwd(qkv, do, lse, cterm, group):
    _, dil = DIL_GROUPS[group]
    s = qkv.shape[0]
    sub = s // dil
    nb = sub // BLOCK
    view = qkv.reshape(sub, dil * 3 * DIL_W)
    slopes = _alibi_slopes()[group * DIL_HEADS:(group + 1) * DIL_HEADS]
    do_v, lse_v, c_v = (t.reshape(sub, dil * DIL_W) for t in (do, lse, cterm))

    def col(which, shift):
        if shift == 0:
            return lambda r, n: (n, r * 3 + which)
        if shift < 0:
            return lambda r, n: (jnp.maximum(n - 1, 0), r * 3 + which)
        return lambda r, n: (jnp.minimum(n + 1, nb - 1), r * 3 + which)

    def own(shift):
        if shift == 0:
            return lambda r, n: (n, r)
        return lambda r, n: (jnp.minimum(n + 1, nb - 1), r)

    def body(q_ref, qn_ref, kc_ref, kp_ref, vc_ref, vp_ref, do_ref, don_ref, lse_ref, lsen_ref, c_ref, cn_ref,
             dq_ref, dk_ref, dv_ref):
        n = pl.program_id(1)
        valid, dist = _band_terms(dil, n > 0)
        valid_n = _band_terms(dil, True)[0][:, :BLOCK] & (n < nb - 1)
        dist_n = dist[:, :BLOCK]
        q, qn = q_ref[...], qn_ref[...]
        kc, vc = kc_ref[...], vc_ref[...]
        k2 = jnp.concatenate([kp_ref[...], kc], axis=0)
        v2 = jnp.concatenate([vp_ref[...], vc], axis=0)
        dov, donv = do_ref[...], don_ref[...]
        lsev, lsenv, cv, cnv = lse_ref[...], lsen_ref[...], c_ref[...], cn_ref[...]
        masks = [_head_lane_mask(h, BLOCK) for h in range(DIL_HEADS)]

        def head_col(t, hm):
            return jnp.max(jnp.where(hm, t, NEG_INF), axis=1, keepdims=True)

        qhs = [jnp.where(hm, q, jnp.zeros_like(q)) for hm in masks]
        qnhs = [jnp.where(hm, qn, jnp.zeros_like(qn)) for hm in masks]
        dohs = [jnp.where(hm, dov, 0.0).astype(BF16) for hm in masks]
        donhs = [jnp.where(hm, donv, 0.0).astype(BF16) for hm in masks]
        logit = [_dot_nt(qhs[h], k2) for h in range(DIL_HEADS)]
        dp = [_dot_nt(dohs[h], v2) for h in range(DIL_HEADS)]
        logit_n = [_dot_nt(qnhs[h], kc) for h in range(DIL_HEADS)]
        dp_n = [_dot_nt(donhs[h], vc) for h in range(DIL_HEADS)]
        p16, dlog, pn16, dlog_n = [], [], [], []
        for h in range(DIL_HEADS):
            hm, slope = masks[h], float(slopes[h])
            p = jnp.where(valid, jnp.exp(logit[h] * 0.125 - slope * dist - head_col(lsev, hm)), 0.0)
            dlog.append((p * (dp[h] + head_col(cv, hm)) * 0.125).astype(BF16))
            p16.append(p.astype(BF16))
            pn = jnp.where(valid_n, jnp.exp(logit_n[h] * 0.125 - slope * dist_n - head_col(lsenv, hm)), 0.0)
            dlog_n.append((pn * (dp_n[h] + head_col(cnv, hm)) * 0.125).astype(BF16))
            pn16.append(pn.astype(BF16))
        dq_acc = jnp.zeros((BLOCK, DIL_W), F32)
        dk_acc = jnp.zeros((BLOCK, DIL_W), F32)
        dv_acc = jnp.zeros((BLOCK, DIL_W), F32)
        for h in range(DIL_HEADS):
            dq_acc = jnp.where(masks[h], _dot_nn(dlog[h], k2), dq_acc)
            dk_acc += _dot_tn(dlog[h][:, BLOCK:], qhs[h]) + _dot_tn(dlog_n[h], qnhs[h])
            dv_acc += _dot_tn(p16[h][:, BLOCK:], dohs[h]) + _dot_tn(pn16[h], donhs[h])
        dq_ref[...] = dq_acc.astype(BF16)
        dk_ref[...] = dk_acc.astype(BF16)
        dv_ref[...] = dv_acc.astype(BF16)

    blk = (BLOCK, DIL_W)
    outs = pl.pallas_call(
        body,
        name=f"dil_bwd_g{group}",
        grid=(dil, nb),
        in_specs=[pl.BlockSpec(blk, col(0, 0)), pl.BlockSpec(blk, col(0, 1)),
                  pl.BlockSpec(blk, col(1, 0)), pl.BlockSpec(blk, col(1, -1)),
                  pl.BlockSpec(blk, col(2, 0)), pl.BlockSpec(blk, col(2, -1)),
                  pl.BlockSpec(blk, own(0)), pl.BlockSpec(blk, own(1)),
                  pl.BlockSpec(blk, own(0)), pl.BlockSpec(blk, own(1)),
                  pl.BlockSpec(blk, own(0)), pl.BlockSpec(blk, own(1))],
        out_specs=[pl.BlockSpec(blk, lambda r, n: (n, r))] * 3,
        out_shape=[jax.ShapeDtypeStruct((sub, dil * DIL_W), BF16)] * 3,
        compiler_params=_cparams(("parallel", "parallel")),
    )(view, view, view, view, view, view, do_v, do_v, lse_v, lse_v, c_v, c_v)
    return tuple(t.reshape(s, DIL_W) for t in outs)


SB_PAIRS = SB_HEADS // 2
SB_COL0 = 0
LOG2E = 1.4426950408889634


def _sb_log_terms(zs):
    e = jnp.exp2(-jnp.abs(zs))
    return -(jnp.maximum(zs, 0.0) + jnp.log(1.0 + e) * LOG2E)


def _sb_consts(nkb):
    row = lax.broadcasted_iota(jnp.int32, (SB_BQ, SB_BK), 0)
    colk = lax.broadcasted_iota(jnp.int32, (SB_BQ, SB_BK), 1)
    rr = lax.broadcasted_iota(jnp.int32, (SB_BK, SB_BK), 0)
    cc = lax.broadcasted_iota(jnp.int32, (SB_BK, SB_BK), 1)
    lane = lax.broadcasted_iota(jnp.int32, (SB_BQ, 128), 1)
    assert 2 * nkb <= 128
    return colk < row, rr, cc, lane < HEAD_DIM, lane


def _split_heads(t):
    first = lax.broadcasted_iota(jnp.int32, t.shape, 1) < HEAD_DIM
    zero = jnp.zeros_like(t)
    return jnp.where(first, t, zero), jnp.where(first, zero, t)


def _sb_fwd(qkv):
    s = qkv.shape[0]
    nq, nkb = s // SB_BQ, s // SB_BK
    zscale = LOG2E / math.sqrt(HEAD_DIM)

    def body(q_ref, k_ref, v_ref, o_ref, carry_ref, zs_scr, a_scr, acc_scr, cl_scr):
        i = pl.program_id(1)
        causal, rr, cc, _, lane = _sb_consts(nkb)
        later = (rr > cc).astype(BF16)
        qh = _split_heads(q_ref[...])

        def rows(j):
            return pl.ds(pl.multiple_of(j * SB_BK, SB_BK), SB_BK)

        def scores_to(slot, j):
            kb = k_ref[rows(j), :]
            for hh in range(2):
                zs_scr[slot, hh] = _dot_nt(qh[hh], kb) * zscale

        def weights(slot, j, masked):
            xs, sums, sufs = [], [], []
            for hh in range(2):
                zs = zs_scr[slot, hh]
                l = _sb_log_terms(zs)
                if masked:
                    l = jnp.where(causal, l, 0.0)
                xs.append(zs + l)
                sums.append(jnp.sum(l, axis=1, keepdims=True))
                sufs.append(_dot_f32_by_01(l, later, 2))
            for hh in range(2):
                cl = cl_scr[hh]
                a = jnp.exp2(xs[hh] + (sufs[hh] + jnp.concatenate([cl, cl], axis=1)))
                if masked:
                    a = jnp.where(causal, a, 0.0)
                a_scr[slot, :, hh * SB_BK:(hh + 1) * SB_BK] = a.astype(BF16)
            for hh in range(2):
                cl = cl_scr[hh]
                carry_ref[0] = jnp.where(lane == j + hh * nkb, cl, carry_ref[0])
                cl_scr[hh] = cl + sums[hh]

        def add_av(slot, j):
            v0, v1 = _split_heads(v_ref[rows(j), :])
            acc_scr[...] += _dot_nn(a_scr[slot], jnp.concatenate([v0, v1], axis=0))

        acc_scr[...] = jnp.zeros_like(acc_scr)
        cl_scr[...] = jnp.zeros_like(cl_scr)
        carry_ref[...] = jnp.zeros_like(carry_ref)
        scores_to(0, i)
        scores_to(1, jnp.maximum(i - 1, 0))
        weights(0, i, True)

        def step(j, prev, cur):
            scores_to(prev, jnp.maximum(j - 1, 0))
            add_av(prev, j + 1)
            weights(cur, j, False)

        def two_steps(u, _):
            j = i - 1 - 2 * u
            step(j, 0, 1)
            step(j - 1, 1, 0)
            return 0

        lax.fori_loop(0, i // 2, two_steps, 0)

        @pl.when(i % 2 == 1)
        def _():
            step(0, 0, 1)
            add_av(1, 0)

        @pl.when(i % 2 == 0)
        def _():
            add_av(0, 0)

        o_ref[...] = acc_scr[...]

    def full(which):
        return pl.BlockSpec((s, 128), lambda p, i: (0, SB_COL0 + 4 * which + p))

    return pl.pallas_call(
        body,
        name="sb_fwd",
        grid=(SB_PAIRS, nq),
        in_specs=[pl.BlockSpec((SB_BQ, 128), lambda p, i: (i, SB_COL0 + p)), full(1), full(2)],
        out_specs=[pl.BlockSpec((SB_BQ, 128), lambda p, i: (i, p)),
                   pl.BlockSpec((1, SB_BQ, 128), lambda p, i: (p, i, 0))],
        out_shape=[jax.ShapeDtypeStruct((s, SB_W), F32), jax.ShapeDtypeStruct((SB_PAIRS, s, 128), F32)],
        scratch_shapes=[pltpu.VMEM((2, 2, SB_BQ, SB_BK), F32), pltpu.VMEM((2, SB_BQ, 2 * SB_BK), BF16),
                        pltpu.VMEM((SB_BQ, 128), F32), pltpu.VMEM((2, SB_BQ, 128), F32)],
        compiler_params=_cparams(("parallel", "parallel")),
    )(qkv, qkv, qkv)


def _sb_bwd(qkv, do, carries):
    s = qkv.shape[0]
    nq, nkb = s // SB_BQ, s // SB_BK
    scale = 1.0 / math.sqrt(HEAD_DIM)
    zscale = LOG2E * scale

    def body(q_ref, k_ref, v_ref, do_ref, carry_ref, dq_ref, dk_ref, dv_ref, zs_scr, da_scr, dz_scr, a_scr, cg_scr):
        i = pl.program_id(1)

        @pl.when(i == 0)
        def _():
            dk_ref[...] = jnp.zeros_like(dk_ref)
            dv_ref[...] = jnp.zeros_like(dv_ref)

        causal, rr, cc, first, lane = _sb_consts(nkb)
        later = (rr > cc).astype(BF16)
        earlier = (rr < cc).astype(BF16)
        q2 = q_ref[...]
        qh = _split_heads(q2)
        do2 = do_ref[...].astype(BF16)
        doh = _split_heads(do2)
        ctile = carry_ref[0]

        def rows(j):
            return pl.ds(pl.multiple_of(j * SB_BK, SB_BK), SB_BK)

        def products_to(slot, j):
            kb, vb = k_ref[rows(j), :], v_ref[rows(j), :]
            for hh in range(2):
                zs_scr[slot, hh] = _dot_nt(qh[hh], kb) * zscale
                da_scr[slot, hh] = _dot_nt(doh[hh], vb)

        def by_head(t):
            return jnp.where(first, t[:SB_BK], t[SB_BK:])

        def apply(slot, j):
            k0, k1 = _split_heads(k_ref[rows(j), :])
            dq_ref[...] += _dot_nn(dz_scr[slot], jnp.concatenate([k0, k1], axis=0))
            dk_ref[rows(j), :] += by_head(_dot_tn(dz_scr[slot], q2))
            dv_ref[rows(j), :] += by_head(_dot_tn(a_scr[slot], do2))

        def grads(slot, j, masked):
            xs, sigs, sufs = [], [], []
            for hh in range(2):
                zs = zs_scr[slot, hh]
                l = _sb_log_terms(zs)
                sigs.append(jnp.exp2(zs + l))
                if masked:
                    l = jnp.where(causal, l, 0.0)
                xs.append(zs + l)
                sufs.append(_dot_f32_by_01(l, later, 2))
            gs, gpres = [], []
            for hh in range(2):
                cl = jnp.sum(jnp.where(lane == j + hh * nkb, ctile, 0.0), axis=1, keepdims=True)
                a = jnp.exp2(xs[hh] + (sufs[hh] + cl))
                if masked:
                    a = jnp.where(causal, a, 0.0)
                g = a * da_scr[slot, hh]
                a_scr[slot, :, hh * SB_BK:(hh + 1) * SB_BK] = a.astype(BF16)
                gs.append(g)
                gpres.append(_dot_f32_by_01(g, earlier, 2))
            for hh in range(2):
                cg = cg_scr[hh]
                dz = gs[hh] - (gs[hh] + (gpres[hh] + jnp.concatenate([cg, cg], axis=1))) * sigs[hh]
                if masked:
                    dz = jnp.where(causal, dz, 0.0)
                dz_scr[slot, :, hh * SB_BK:(hh + 1) * SB_BK] = (dz * scale).astype(BF16)
                cg_scr[hh] = cg + jnp.sum(gs[hh], axis=1, keepdims=True)

        dq_ref[...] = jnp.zeros_like(dq_ref)
        cg_scr[...] = jnp.zeros_like(cg_scr)
        dz_scr[1] = jnp.zeros((SB_BQ, 2 * SB_BK), BF16)
        a_scr[1] = jnp.zeros((SB_BQ, 2 * SB_BK), BF16)
        products_to(0, 0)

        def step(j, cur, nxt):
            products_to(nxt, j + 1)
            apply(nxt, jnp.maximum(j - 1, 0))
            grads(cur, j, False)

        def two_steps(u, _):
            step(2 * u, 0, 1)
            step(2 * u + 1, 1, 0)
            return 0

        lax.fori_loop(0, i // 2, two_steps, 0)

        def last(cur, nxt):
            apply(nxt, jnp.maximum(i - 1, 0))
            grads(cur, i, True)
            apply(cur, i)

        @pl.when(i % 2 == 1)
        def _():
            step(i - 1, 0, 1)
            last(1, 0)

        @pl.when(i % 2 == 0)
        def _():
            last(0, 1)

    def full(which):
        return pl.BlockSpec((s, 128), lambda p, i: (0, SB_COL0 + 4 * which + p))

    qblk = pl.BlockSpec((SB_BQ, 128), lambda p, i: (i, p))
    acc = pl.BlockSpec((s, 128), lambda p, i: (0, p))
    return pl.pallas_call(
        body,
        name="sb_bwd",
        grid=(SB_PAIRS, nq),
        in_specs=[pl.BlockSpec((SB_BQ, 128), lambda p, i: (i, SB_COL0 + p)), full(1), full(2), qblk,
                  pl.BlockSpec((1, SB_BQ, 128), lambda p, i: (p, i, 0))],
        out_specs=[qblk, acc, acc],
        out_shape=[jax.ShapeDtypeStruct((s, SB_W), F32)] * 3,
        scratch_shapes=[pltpu.VMEM((2, 2, SB_BQ, SB_BK), F32), pltpu.VMEM((2, 2, SB_BQ, SB_BK), F32),
                        pltpu.VMEM((2, SB_BQ, 2 * SB_BK), BF16), pltpu.VMEM((2, SB_BQ, 2 * SB_BK), BF16),
                        pltpu.VMEM((2, SB_BQ, 128), F32)],
        compiler_params=_cparams(("parallel", "arbitrary")),
    )(qkv, qkv, qkv, do, carries)


MERGE_TILE = 256


def _group_mix(lses):
    mx = jnp.maximum(jnp.maximum(lses[0], lses[1]), lses[2])
    es = [jnp.exp(t - mx) for t in lses]
    den = es[0] + es[1] + es[2]
    return [e / den for e in es]


def _merge_fwd(o_groups, lse_groups, o_sb, gl, b_gate, w_up_dil, w_up_sb):
    s = gl.shape[0]
    t = MERGE_TILE

    def body(o0, o1, o2, l0, l1, l2, ob_ref, gl_ref, bg_ref, wd_ref, ws_ref, merged_ref, oa_ref):
        w = _group_mix([l0[...], l1[...], l2[...]])
        oa = (w[0] * o0[...] + w[1] * o1[...] + w[2] * o2[...]).astype(BF16)
        ua = _dot_nn(oa, wd_ref[...])
        ub = _dot_nn(ob_ref[...].astype(BF16), ws_ref[...])
        gate = jax.nn.sigmoid(gl_ref[...] + bg_ref[...])
        merged_ref[...] = (gate[:, :D_MODEL] * ua + gate[:, D_MODEL:] * ub).astype(BF16)
        oa_ref[...] = oa

    dil = pl.BlockSpec((t, DIL_W), lambda i: (i, 0))
    const = lambda shape: pl.BlockSpec(shape, lambda i: (0, 0))
    return pl.pallas_call(
        body,
        name="merge_fwd",
        grid=(s // t,),
        in_specs=[dil] * 6 + [pl.BlockSpec((t, SB_W), lambda i: (i, 0)), pl.BlockSpec((t, GATE_W), lambda i: (i, 0)),
                              const((1, GATE_W)), const((DIL_W, D_MODEL)), const((SB_W, D_MODEL))],
        out_specs=[pl.BlockSpec((t, D_MODEL), lambda i: (i, 0)), dil],
        out_shape=[jax.ShapeDtypeStruct((s, D_MODEL), BF16), jax.ShapeDtypeStruct((s, DIL_W), BF16)],
        compiler_params=_cparams(("parallel",)),
    )(*o_groups, *lse_groups, o_sb, gl, b_gate, w_up_dil, w_up_sb)


def _merge_bwd(dmerged, o_groups, lse_groups, o_sb, gl, b_gate, w_up_dil, w_up_sb):
    s = gl.shape[0]
    t = MERGE_TILE

    def body(dm_ref, o0, o1, o2, l0, l1, l2, ob_ref, gl_ref, bg_ref, wd_ref, ws_ref,
             dua_ref, dub_ref, dgl_ref, dbg_ref, dosb_ref, d0, d1, d2, c0, c1, c2):
        i = pl.program_id(0)
        og = [o0[...], o1[...], o2[...]]
        w = _group_mix([l0[...], l1[...], l2[...]])
        oa = (w[0] * og[0] + w[1] * og[1] + w[2] * og[2]).astype(BF16)
        ua = _dot_nn(oa, wd_ref[...])
        ub = _dot_nn(ob_ref[...].astype(BF16), ws_ref[...])
        gate = jax.nn.sigmoid(gl_ref[...] + bg_ref[...])
        ga, gb = gate[:, :D_MODEL], gate[:, D_MODEL:]
        dm = dm_ref[...]
        dua = (dm * ga).astype(BF16)
        dub = (dm * gb).astype(BF16)
        dua_ref[...] = dua
        dub_ref[...] = dub
        dgl_a = dm * ua * ga * (1.0 - ga)
        dgl_b = dm * ub * gb * (1.0 - gb)
        dgl_ref[:, :D_MODEL] = dgl_a.astype(BF16)
        dgl_ref[:, D_MODEL:] = dgl_b.astype(BF16)
        part = jnp.concatenate([jnp.sum(dgl_a.reshape(t // 8, 8, D_MODEL), axis=0),
                                jnp.sum(dgl_b.reshape(t // 8, 8, D_MODEL), axis=0)], axis=1)

        @pl.when(i == 0)
        def _():
            dbg_ref[...] = part

        @pl.when(i > 0)
        def _():
            dbg_ref[...] += part

        dosb_ref[...] = _dot_nt(dub, ws_ref[...])
        doa = _dot_nt(dua, wd_ref[...])
        rr = lax.broadcasted_iota(jnp.int32, (DIL_W, DIL_W), 0) // HEAD_DIM
        cc = lax.broadcasted_iota(jnp.int32, (DIL_W, DIL_W), 1) // HEAD_DIM
        same_head = (rr == cc).astype(BF16)
        dw = [_dot_f32_by_01(doa * og[g], same_head) for g in range(3)]
        mean_dw = w[0] * dw[0] + w[1] * dw[1] + w[2] * dw[2]
        for g, (d_ref, c_ref) in enumerate(((d0, c0), (d1, c1), (d2, c2))):
            d_ref[...] = w[g] * doa
            c_ref[...] = -w[g] * mean_dw

    dil = pl.BlockSpec((t, DIL_W), lambda i: (i, 0))
    wide = pl.BlockSpec((t, D_MODEL), lambda i: (i, 0))
    gate2 = pl.BlockSpec((t, GATE_W), lambda i: (i, 0))
    sbw = pl.BlockSpec((t, SB_W), lambda i: (i, 0))
    const = lambda shape: pl.BlockSpec(shape, lambda i: (0, 0))
    return pl.pallas_call(
        body,
        name="merge_bwd",
        grid=(s // t,),
        in_specs=[wide] + [dil] * 6 + [sbw, gate2, const((1, GATE_W)), const((DIL_W, D_MODEL)), const((SB_W, D_MODEL))],
        out_specs=[wide, wide, gate2, const((8, GATE_W)), sbw] + [dil] * 6,
        out_shape=[jax.ShapeDtypeStruct((s, D_MODEL), BF16), jax.ShapeDtypeStruct((s, D_MODEL), BF16),
                   jax.ShapeDtypeStruct((s, GATE_W), BF16), jax.ShapeDtypeStruct((8, GATE_W), F32),
                   jax.ShapeDtypeStruct((s, SB_W), F32)] + [jax.ShapeDtypeStruct((s, DIL_W), F32)] * 6,
        compiler_params=_cparams(("arbitrary",)),
    )(dmerged, *o_groups, *lse_groups, o_sb, gl, b_gate, w_up_dil, w_up_sb)


ANY = pl.BlockSpec(memory_space=pl.ANY)


def _place():
    x, y, c = lax.axis_index("x"), lax.axis_index("y"), lax.axis_index("c")
    other_chips = [(1 - x, y), (x, 1 - y), (1 - x, 1 - y)]
    return x, y, c, other_chips


def _all_gather_weights(pack):
    r, wd = pack.shape
    rh = r // 2
    pack = pack.reshape(2, rh, wd)

    def body(p_ref, out_ref, send_sems, recv_sems):
        x, y, c, chips = _place()
        me, sibling = 2 * x + y, (x, y, 1 - c)

        def half(chip_idx, core):
            return out_ref.at[chip_idx, core]

        def copy(k, chip_idx, core, to, src=None):
            return pltpu.make_async_remote_copy(
                src_ref=half(chip_idx, core) if src is None else src, dst_ref=half(chip_idx, core),
                send_sem=send_sems.at[k], recv_sem=recv_sems.at[k], device_id=to, device_id_type=MESH)

        first = [copy(j, me, c, (*chip, c), src=p_ref.at[c]) for j, chip in enumerate(chips)]
        for cp in first:
            cp.start()
        passed = [copy(3 + j, 2 * chip[0] + chip[1], c, sibling) for j, chip in enumerate(chips)]
        for j, chip in enumerate(chips):
            copy(j, 2 * chip[0] + chip[1], c, (x, y, c)).wait_recv()
            passed[j].start()
        for j, chip in enumerate(chips):
            copy(3 + j, 2 * chip[0] + chip[1], 1 - c, (x, y, c)).wait_recv()
        for cp in first + passed:
            cp.wait_send()

    others = pl.pallas_call(
        body,
        name="all_gather_weights",
        in_specs=[ANY],
        out_specs=ANY,
        out_shape=jax.ShapeDtypeStruct((N_CHIPS, 2, rh, wd), pack.dtype),
        scratch_shapes=[pltpu.SemaphoreType.DMA((6,)), pltpu.SemaphoreType.DMA((6,))],
    )(pack)
    me = 2 * lax.axis_index("x") + lax.axis_index("y")
    return lax.dynamic_update_slice(others, pack[None], (me, 0, 0, 0)).reshape(N_CHIPS, r, wd)


def _swap_halves(g):
    n, r, wd = g.shape
    rh = r // 2
    g = g.reshape(n, 2, rh, wd)

    def body(g_ref, out_ref, send_sem, recv_sem):
        x, y, c, _ = _place()
        cp = pltpu.make_async_remote_copy(
            src_ref=g_ref.at[:, 1 - c], dst_ref=out_ref,
            send_sem=send_sem, recv_sem=recv_sem, device_id=(x, y, 1 - c), device_id_type=MESH)
        cp.start()
        cp.wait()

    return pl.pallas_call(
        body,
        name="grad_swap_halves",
        in_specs=[ANY],
        out_specs=ANY,
        out_shape=jax.ShapeDtypeStruct((n, rh, wd), g.dtype),
        scratch_shapes=[pltpu.SemaphoreType.DMA, pltpu.SemaphoreType.DMA],
    )(g)


def _add_halves(g, got, core):
    n, r, wd = g.shape
    rh = r // 2
    t = rh // 4
    nt = rh // t

    def body(c_ref, a_ref, b_ref, o_ref):
        o_ref[...] = (a_ref[0] + b_ref[...]).astype(BF16)

    grid_spec = pltpu.PrefetchScalarGridSpec(
        num_scalar_prefetch=1,
        grid=(n, nt),
        in_specs=[pl.BlockSpec((1, 1, t, wd), lambda s, i, c: (s, c[0], i, 0)),
                  pl.BlockSpec((1, t, wd), lambda s, i, c: (s, i, 0))],
        out_specs=pl.BlockSpec((1, t, wd), lambda s, i, c: (s, i, 0)),
    )
    return pl.pallas_call(
        body,
        name="grad_add_halves",
        grid_spec=grid_spec,
        out_shape=jax.ShapeDtypeStruct((n, rh, wd), BF16),
        compiler_params=_cparams(("parallel", "parallel")),
    )(core, g.reshape(n, 2, rh, wd), got)


def _exchange_chunks(h):
    n, rh, wd = h.shape

    def body(h_ref, out_ref, send_sems, recv_sems):
        x, y, c, chips = _place()
        me = 2 * x + y
        sends = []
        for j, chip in enumerate(chips):
            them = 2 * chip[0] + chip[1]
            sends.append(pltpu.make_async_remote_copy(
                src_ref=h_ref.at[them], dst_ref=out_ref.at[me],
                send_sem=send_sems.at[j], recv_sem=recv_sems.at[j], device_id=(*chip, c), device_id_type=MESH))
        for cp in sends:
            cp.start()
        for j, chip in enumerate(chips):
            them = 2 * chip[0] + chip[1]
            pltpu.make_async_remote_copy(
                src_ref=h_ref.at[them], dst_ref=out_ref.at[them],
                send_sem=send_sems.at[j], recv_sem=recv_sems.at[j], device_id=(*chip, c), device_id_type=MESH).wait_recv()
        for cp in sends:
            cp.wait_send()

    return pl.pallas_call(
        body,
        name="grad_exchange_chunks",
        in_specs=[ANY],
        out_specs=ANY,
        out_shape=jax.ShapeDtypeStruct((n, rh, wd), h.dtype),
        scratch_shapes=[pltpu.SemaphoreType.DMA((3,)), pltpu.SemaphoreType.DMA((3,))],
    )(h)


def _sum_chips(b, h, chip):
    n, rh, wd = b.shape
    t = rh // 4

    def body(chip_ref, b_ref, own_ref, o_ref):
        own = own_ref[0]
        s0, s1, s2, s3 = (jnp.where(chip_ref[0] == k, own, b_ref[k]).astype(F32) for k in range(n))
        o_ref[...] = ((s0 + s1) + s2) + s3

    grid_spec = pltpu.PrefetchScalarGridSpec(
        num_scalar_prefetch=1,
        grid=(rh // t,),
        in_specs=[pl.BlockSpec((n, t, wd), lambda i, chip: (0, i, 0)),
                  pl.BlockSpec((1, t, wd), lambda i, chip: (chip[0], i, 0))],
        out_specs=pl.BlockSpec((t, wd), lambda i, chip: (i, 0)),
    )
    return pl.pallas_call(
        body,
        name="grad_sum_chips",
        grid_spec=grid_spec,
        out_shape=jax.ShapeDtypeStruct((rh, wd), F32),
        compiler_params=_cparams(("parallel",)),
    )(chip, b, h)


def _join_halves(tc):
    rh, wd = tc.shape

    def body(t_ref, out_ref, send_sem, recv_sem):
        x, y, c, _ = _place()
        cp = pltpu.make_async_remote_copy(
            src_ref=t_ref, dst_ref=out_ref.at[c],
            send_sem=send_sem, recv_sem=recv_sem, device_id=(x, y, 1 - c), device_id_type=MESH)
        cp.start()
        cp.wait()

    halves = pl.pallas_call(
        body,
        name="grad_join_halves",
        in_specs=[ANY],
        out_specs=ANY,
        out_shape=jax.ShapeDtypeStruct((2, rh, wd), tc.dtype),
        scratch_shapes=[pltpu.SemaphoreType.DMA, pltpu.SemaphoreType.DMA],
    )(tc)
    return lax.dynamic_update_slice(halves, tc[None], (lax.axis_index("c"), 0, 0)).reshape(2 * rh, wd)


def _all_reduce_small(pack):
    rows, lanes = pack.shape

    def body(p_ref, out_ref, buf, send_sems, recv_sems):
        x, y, c, _ = _place()
        me = 4 * x + 2 * y + c
        buf[me] = p_ref[...]
        sends = []
        for k in range(1, N_DEV):
            peer = (x ^ (k >> 2), y ^ ((k >> 1) & 1), c ^ (k & 1))
            sends.append(pltpu.make_async_remote_copy(
                src_ref=p_ref, dst_ref=buf.at[me], send_sem=send_sems.at[k - 1], recv_sem=recv_sems.at[k - 1],
                device_id=peer, device_id_type=MESH))
        for cp in sends:
            cp.start()
        for k in range(1, N_DEV):
            pltpu.make_async_remote_copy(
                src_ref=p_ref, dst_ref=buf.at[me ^ k], send_sem=send_sems.at[k - 1], recv_sem=recv_sems.at[k - 1],
                device_id=(x, y, c), device_id_type=MESH).wait_recv()
        for cp in sends:
            cp.wait_send()
        total = buf[0]
        for d in range(1, N_DEV):
            total = total + buf[d]
        out_ref[...] = total

    vm = pl.BlockSpec(memory_space=pltpu.VMEM)
    return pl.pallas_call(
        body,
        name="all_reduce_small",
        in_specs=[vm],
        out_specs=vm,
        out_shape=jax.ShapeDtypeStruct((rows, lanes), F32),
        scratch_shapes=[pltpu.VMEM((N_DEV, rows, lanes), F32), pltpu.SemaphoreType.DMA((N_DEV - 1,)),
                        pltpu.SemaphoreType.DMA((N_DEV - 1,))],
    )(pack)


def _adamw(g, w, m, v, name):
    rows, cols = g.shape
    t = rows
    for cand in (256, 128, 64, 32, 16, 8):
        if rows % cand == 0:
            t = cand
            break

    def body(g_ref, w_ref, m_ref, v_ref, d_ref, nm_ref, nv_ref):
        gv = g_ref[...]
        mv = ADAM_B1 * m_ref[...] + (1.0 - ADAM_B1) * gv
        vv = ADAM_B2 * v_ref[...] + (1.0 - ADAM_B2) * (gv * gv)
        m_hat = mv / (1.0 - ADAM_B1 ** ADAM_STEP)
        v_hat = vv / (1.0 - ADAM_B2 ** ADAM_STEP)
        d_ref[...] = -ADAM_LR * (m_hat / (jnp.sqrt(v_hat) + ADAM_EPS) + ADAM_WD * w_ref[...])
        nm_ref[...] = mv
        nv_ref[...] = vv

    blk = pl.BlockSpec((t, cols), lambda i: (i, 0))
    return pl.pallas_call(
        body,
        name=name,
        grid=(rows // t,),
        in_specs=[blk] * 4,
        out_specs=[blk] * 3,
        out_shape=[jax.ShapeDtypeStruct((rows, cols), F32)] * 3,
        compiler_params=_cparams(("parallel",)),
    )(g, w, m, v)


PACK_W = 1024
BIG = (("w_in", (D_MODEL, IN_COLS), 1), ("w_up_dil", (DIL_W, D_MODEL), 1), ("w_up_sb", (SB_W, D_MODEL), 1),
       ("w_out", (D_MODEL, D_MODEL), 0), ("w_mlp_in", (D_MODEL, D_FF), 1), ("w_mlp_out", (D_FF, D_MODEL), 0))


def _shard_shape(shape, axis):
    return tuple(d // N_CHIPS if a == axis else d for a, d in enumerate(shape))


def _pack_rows():
    rows, at = {}, 0
    for name, shape, axis in BIG:
        n = math.prod(_shard_shape(shape, axis)) // PACK_W
        rows[name] = (at, n)
        at += n
    return rows, at


def _pack_shards(shards):
    return jnp.concatenate([shards[name].reshape(-1, PACK_W) for name, _, _ in BIG], axis=0)


def _unpack_full(gathered):
    rows, _ = _pack_rows()
    full = {}
    for name, shape, axis in BIG:
        at, n = rows[name]
        parts = gathered[:, at:at + n, :].reshape((N_CHIPS,) + _shard_shape(shape, axis))
        if axis == 0:
            full[name] = parts.reshape(shape)
        else:
            full[name] = jnp.transpose(parts, (1, 0, 2)).reshape(shape)
    return full


def _pack_full_grads(grads):
    chunks = []
    for name, shape, axis in BIG:
        g = grads[name]
        if axis == 0:
            parts = g.reshape((N_CHIPS, shape[0] // N_CHIPS, shape[1]))
        else:
            parts = jnp.transpose(g.reshape((shape[0], N_CHIPS, shape[1] // N_CHIPS)), (1, 0, 2))
        chunks.append(parts.reshape(N_CHIPS, -1, PACK_W))
    return jnp.concatenate(chunks, axis=1)


def _unpack_shard(packed):
    rows, _ = _pack_rows()
    return {name: packed[rows[name][0]:rows[name][0] + rows[name][1]].reshape(_shard_shape(shape, axis))
            for name, shape, axis in BIG}


def _local_step(x, target, w, norm_mix_g, b_gate, norm_mlp_g, norm_final_g):
    w_in = w["w_in"]
    sb0 = 9 * DIL_W
    w_sb, w_gate = w_in[:, sb0:QKV_W], w_in[:, QKV_W:]
    w_dil = [jnp.concatenate([w_in[:, (3 * i + g) * DIL_W:(3 * i + g + 1) * DIL_W] for i in range(3)], axis=1)
             for g in range(3)]

    h = _rms_fwd(x, norm_mix_g, "norm_mix")
    qkv_dil = [_matmul(h, w_dil[g], mode="nn", out_dtypes=(BF16,), name=f"proj_dil_g{g}", tn=768)[0] for g in range(3)]
    (qkv_sb,) = _matmul(h, w_sb, mode="nn", out_dtypes=(BF16,), name="proj_sb", tn=768)
    (gl,) = _matmul(h, w_gate, mode="nn", out_dtypes=(F32,), name="proj_gate")
    dil = [_dil_fwd(qkv_dil[g], g) for g in range(3)]
    o_groups, lse_groups = [d[0] for d in dil], [d[1] for d in dil]
    o_sb, carries = _sb_fwd(qkv_sb)
    merged, o_a = _merge_fwd(o_groups, lse_groups, o_sb, gl, b_gate, w["w_up_dil"], w["w_up_sb"])
    (x1,) = _matmul(merged, w["w_out"], mode="nn", out_dtypes=(F32,), name="out_proj",
                    extras=(x,), epilogue=lambda acc, res: (res + acc,))
    h2 = _rms_fwd(x1, norm_mlp_g, "norm_mlp")
    u, act = _matmul(h2, w["w_mlp_in"], mode="nn", out_dtypes=(F32, BF16), name="mlp_in",
                     epilogue=lambda acc: (acc, jnp.square(jnp.maximum(acc, 0.0))))
    (x2,) = _matmul(act, w["w_mlp_out"], mode="nn", out_dtypes=(F32,), name="mlp_out", tk=2048,
                    extras=(x1,), epilogue=lambda acc, res: (res + acc,))
    dx2, dg_final, loss_part = _loss_head(x2, norm_final_g.reshape(1, D_MODEL), target)

    (du,) = _matmul(dx2, w["w_mlp_out"], mode="nt", out_dtypes=(BF16,), name="mlp_out_dx",
                    extras=(u,), epilogue=lambda acc, uu: (acc * (2.0 * jnp.maximum(uu, 0.0)),))
    (g_mlp_out,) = _matmul(act, dx2, mode="tn", out_dtypes=(F32,), name="mlp_out_dw")
    (g_mlp_in,) = _matmul(h2, du, mode="tn", out_dtypes=(F32,), name="mlp_in_dw")
    (dh2,) = _matmul(du, w["w_mlp_in"], mode="nt", out_dtypes=(F32,), name="mlp_in_dx", tk=2048)
    dx1, dg_mlp = _rms_bwd(dh2, x1, norm_mlp_g, dx2, "norm_mlp_bwd")

    (dmerged,) = _matmul(dx1, w["w_out"], mode="nt", out_dtypes=(F32,), name="out_proj_dx")
    (g_out,) = _matmul(merged, dx1, mode="tn", out_dtypes=(F32,), name="out_proj_dw")
    mb = _merge_bwd(dmerged, o_groups, lse_groups, o_sb, gl, b_gate, w["w_up_dil"], w["w_up_sb"])
    dua, dub, dgl, dbg, do_sb = mb[:5]
    do_groups, c_groups = mb[5:8], mb[8:11]
    (g_up_dil,) = _matmul(o_a, dua, mode="tn", out_dtypes=(F32,), name="up_dil_dw")
    (g_up_sb,) = _matmul(o_sb, dub, mode="tn", out_dtypes=(F32,), name="up_sb_dw")
    dq_sb, dk_sb, dv_sb = _sb_bwd(qkv_sb, do_sb, carries)
    dil_b = [_dil_bwd(qkv_dil[g], do_groups[g], lse_groups[g], c_groups[g], g) for g in range(3)]
    dproj = jnp.concatenate(
        [dil_b[g][i].astype(BF16) for i in range(3) for g in range(3)]
        + [t.astype(BF16) for t in (dq_sb, dk_sb, dv_sb)] + [dgl], axis=1)
    (g_in,) = _matmul(h, dproj, mode="tn", out_dtypes=(F32,), name="proj_dw", tm=512, tn=IN_COLS // 2)
    (dh,) = _matmul(dproj, w["w_in"], mode="nt", out_dtypes=(F32,), name="proj_dx", tk=IN_COLS // 2)
    grad_x, dg_mix = _rms_bwd(dh, x, norm_mix_g, dx1, "norm_mix_bwd")

    big = {"w_in": g_in, "w_up_dil": g_up_dil, "w_up_sb": g_up_sb, "w_out": g_out,
           "w_mlp_in": g_mlp_in, "w_mlp_out": g_mlp_out}
    small = (dg_mix, dbg, dg_mlp, dg_final, loss_part)
    return grad_x, big, small


def kernel(x, norm_mix_g, w_in, b_gate, w_up_dil, w_up_sb, w_out, norm_mlp_g, w_mlp_in, w_mlp_out, norm_final_g, loss_target, m_norm_mix_g, m_w_in, m_b_gate, m_w_up_dil, m_w_up_sb, m_w_out, m_norm_mlp_g, m_w_mlp_in, m_w_mlp_out, m_norm_final_g, v_norm_mix_g, v_w_in, v_b_gate, v_w_up_dil, v_w_up_sb, v_w_out, v_norm_mlp_g, v_w_mlp_in, v_w_mlp_out, v_norm_final_g):
    shards = {"w_in": w_in[0], "w_up_dil": w_up_dil[0], "w_up_sb": w_up_sb[0], "w_out": w_out[0],
              "w_mlp_in": w_mlp_in[0], "w_mlp_out": w_mlp_out[0]}
    moments_m = {"w_in": m_w_in[0], "w_up_dil": m_w_up_dil[0], "w_up_sb": m_w_up_sb[0], "w_out": m_w_out[0],
                 "w_mlp_in": m_w_mlp_in[0], "w_mlp_out": m_w_mlp_out[0]}
    moments_v = {"w_in": v_w_in[0], "w_up_dil": v_w_up_dil[0], "w_up_sb": v_w_up_sb[0], "w_out": v_w_out[0],
                 "w_mlp_in": v_w_mlp_in[0], "w_mlp_out": v_w_mlp_out[0]}

    pack = _pack_shards({n: s.astype(BF16) for n, s in shards.items()})
    full = _unpack_full(_all_gather_weights(pack))

    grad_x, big, small = _local_step(x[0], loss_target[0], full, norm_mix_g, b_gate, norm_mlp_g, norm_final_g)

    core = lax.axis_index("c").astype(jnp.int32).reshape(1)
    gpack = _pack_full_grads(big)
    chip_sum = _add_halves(gpack, _swap_halves(gpack), core)
    chip = (2 * lax.axis_index("x") + lax.axis_index("y")).astype(jnp.int32).reshape(1)
    reduced = _join_halves(_sum_chips(_exchange_chunks(chip_sum), chip_sum, chip))
    g_shard = _unpack_shard(reduced)

    dg_mix, dbg, dg_mlp, dg_final, loss_part = small
    loss_row = jnp.sum(loss_part, axis=0, keepdims=True)
    small_pack = jnp.concatenate(
        [jnp.sum(dg_mix, axis=0, keepdims=True), jnp.sum(dbg, axis=0, keepdims=True),
         jnp.sum(dg_mlp, axis=0, keepdims=True), jnp.sum(dg_final, axis=0, keepdims=True), loss_row], axis=1)
    n_small = small_pack.shape[1]
    small_sum = _all_reduce_small(small_pack.reshape(n_small // 128, 128)).reshape(1, n_small)
    g_norm_mix = small_sum[:, :D_MODEL]
    g_b_gate = small_sum[:, D_MODEL:3 * D_MODEL]
    g_norm_mlp = small_sum[:, 3 * D_MODEL:4 * D_MODEL]
    g_norm_final = small_sum[:, 4 * D_MODEL:5 * D_MODEL]
    loss = jnp.sum(small_sum[:, 5 * D_MODEL:])

    names = ["norm_mix_g", "w_in", "b_gate", "w_up_dil", "w_up_sb", "w_out", "norm_mlp_g", "w_mlp_in", "w_mlp_out",
             "norm_final_g"]
    grads = dict(g_shard)
    grads.update(norm_mix_g=g_norm_mix, b_gate=g_b_gate, norm_mlp_g=g_norm_mlp, norm_final_g=g_norm_final)
    weights = dict(shards)
    weights.update(norm_mix_g=norm_mix_g, b_gate=b_gate, norm_mlp_g=norm_mlp_g, norm_final_g=norm_final_g.reshape(1, D_MODEL))
    ms = dict(moments_m)
    ms.update(norm_mix_g=m_norm_mix_g, b_gate=m_b_gate, norm_mlp_g=m_norm_mlp_g, norm_final_g=m_norm_final_g.reshape(1, D_MODEL))
    vs = dict(moments_v)
    vs.update(norm_mix_g=v_norm_mix_g, b_gate=v_b_gate, norm_mlp_g=v_norm_mlp_g, norm_final_g=v_norm_final_g.reshape(1, D_MODEL))

    out_shapes = {"norm_mix_g": norm_mix_g.shape, "w_in": w_in.shape, "b_gate": b_gate.shape, "w_up_dil": w_up_dil.shape,
                  "w_up_sb": w_up_sb.shape, "w_out": w_out.shape, "norm_mlp_g": norm_mlp_g.shape,
                  "w_mlp_in": w_mlp_in.shape, "w_mlp_out": w_mlp_out.shape, "norm_final_g": norm_final_g.shape}
    g_out, d_out, m_out, v_out = [], [], [], []
    for n in names:
        d, nm, nv = _adamw(grads[n], weights[n], ms[n], vs[n], "adamw_" + n)
        shape = out_shapes[n]
        g_out.append(grads[n].reshape(shape))
        d_out.append(d.reshape(shape))
        m_out.append(nm.reshape(shape))
        v_out.append(nv.reshape(shape))
    return (loss, grad_x.reshape(x.shape), *g_out, *d_out, *m_out, *v_out)
```

```python
import functools
import math

import jax
import jax.numpy as jnp
import numpy as np
from jax import lax
from jax.experimental import pallas as pl
from jax.experimental.pallas import tpu as pltpu

F32 = jnp.float32
BF16 = jnp.bfloat16
MESH = pl.DeviceIdType.MESH

D_MODEL = 1024
HEAD_DIM = 64
DIL_GROUPS = ((128, 1), (512, 4), (2048, 16))
DIL_HEADS = 4
DIL_W = 256
N_DIL_HEADS = 12
SB_HEADS = 8
SB_W = SB_HEADS * HEAD_DIM
QKV_W = 3 * 3 * DIL_W + 3 * SB_W
GATE_W = 2 * D_MODEL
IN_COLS = QKV_W + GATE_W
D_FF = 4 * D_MODEL
BLOCK = 128
RMS_EPS = 1e-6
NEG_INF = -1e30
N_CHIPS = 4
N_DEV = 8

ADAM_LR = 0.001
ADAM_B1 = 0.9
ADAM_B2 = 0.999
ADAM_EPS = 1e-08
ADAM_WD = 0.01
ADAM_STEP = 10

VMEM_LIMIT = 56 * 1024 * 1024

SB_BQ = 256
SB_BK = 256


def _cparams(sem=None):
    if sem is None:
        return pltpu.CompilerParams(vmem_limit_bytes=VMEM_LIMIT)
    return pltpu.CompilerParams(dimension_semantics=sem, vmem_limit_bytes=VMEM_LIMIT)


def _dot(a, b, dims):
    return lax.dot_general(a, b, (dims, ((), ())), preferred_element_type=F32)


def _dot_nn(a, b):
    return _dot(a, b, ((1,), (0,)))


def _dot_nt(a, b):
    return _dot(a, b, ((1,), (1,)))


def _dot_tn(a, b):
    return _dot(a, b, ((0,), (0,)))


def _dot_f32_by_01(x, m01, pieces=3):
    hi = x.astype(BF16)
    r1 = x - hi.astype(F32)
    mid = r1.astype(BF16)
    if pieces == 2:
        return _dot_nn(hi, m01) + _dot_nn(mid, m01)
    lo = (r1 - mid.astype(F32)).astype(BF16)
    return _dot_nn(hi, m01) + _dot_nn(mid, m01) + _dot_nn(lo, m01)


def _matmul(a, b, *, mode, out_dtypes, name, tm=1024, tn=1024, tk=1024, extras=(), epilogue=None):
    if mode == "nn":
        (m, k), (k2, n) = a.shape, b.shape
    elif mode == "nt":
        (m, k), (n, k2) = a.shape, b.shape
    else:
        (k, m), (k2, n) = a.shape, b.shape
    assert k == k2, (a.shape, b.shape, mode)
    tm, tn, tk = min(tm, m), min(tn, n), min(tk, k)
    assert m % tm == 0 and n % tn == 0 and k % tk == 0, (m, n, k, tm, tn, tk)
    nk = k // tk
    n_out = len(out_dtypes)
    n_ex = len(extras)

    if mode == "nn":
        a_spec = pl.BlockSpec((tm, tk), lambda i, j, kk: (i, kk))
        b_spec = pl.BlockSpec((tk, tn), lambda i, j, kk: (kk, j))
        dot = _dot_nn
    elif mode == "nt":
        a_spec = pl.BlockSpec((tm, tk), lambda i, j, kk: (i, kk))
        b_spec = pl.BlockSpec((tn, tk), lambda i, j, kk: (j, kk))
        dot = _dot_nt
    else:
        a_spec = pl.BlockSpec((tk, tm), lambda i, j, kk: (kk, i))
        b_spec = pl.BlockSpec((tk, tn), lambda i, j, kk: (kk, j))
        dot = _dot_tn
    mn_spec = pl.BlockSpec((tm, tn), lambda i, j, kk: (i, j))

    def body(*refs):
        a_ref, b_ref = refs[0], refs[1]
        ex_refs = refs[2:2 + n_ex]
        out_refs = refs[2 + n_ex:2 + n_ex + n_out]
        acc_ref = refs[2 + n_ex + n_out] if nk > 1 else None
        part = dot(a_ref[...].astype(BF16), b_ref[...].astype(BF16))

        def finish(acc):
            if epilogue is None:
                outs = (acc,)
            else:
                outs = epilogue(acc, *[r[...] for r in ex_refs])
            for o_ref, o in zip(out_refs, outs):
                o_ref[...] = o.astype(o_ref.dtype)

        if nk == 1:
            finish(part)
        else:
            kk = pl.program_id(2)

            @pl.when(kk == 0)
            def _():
                acc_ref[...] = part

            @pl.when(kk > 0)
            def _():
                acc_ref[...] += part

            @pl.when(kk == nk - 1)
            def _():
                finish(acc_ref[...])

    outs = pl.pallas_call(
        body,
        name=name,
        grid=(m // tm, n // tn, nk),
        in_specs=[a_spec, b_spec] + [mn_spec] * n_ex,
        out_specs=[mn_spec] * n_out,
        out_shape=[jax.ShapeDtypeStruct((m, n), dt) for dt in out_dtypes],
        scratch_shapes=[pltpu.VMEM((tm, tn), F32)] if nk > 1 else [],
        compiler_params=_cparams(("parallel", "parallel", "arbitrary")),
    )(a, b, *extras)
    return outs


ROW_TILE = 512


def _rms_fwd(x, g, name):
    s, d = x.shape

    def body(x_ref, g_ref, h_ref):
        xv = x_ref[...]
        r = lax.rsqrt(jnp.mean(xv * xv, axis=-1, keepdims=True) + RMS_EPS)
        h_ref[...] = (xv * r * g_ref[...]).astype(BF16)

    return pl.pallas_call(
        body,
        name=name,
        grid=(s // ROW_TILE,),
        in_specs=[pl.BlockSpec((ROW_TILE, d), lambda i: (i, 0)), pl.BlockSpec((1, d), lambda i: (0, 0))],
        out_specs=pl.BlockSpec((ROW_TILE, d), lambda i: (i, 0)),
        out_shape=jax.ShapeDtypeStruct((s, d), BF16),
        compiler_params=_cparams(("parallel",)),
    )(x, g)


def _rms_bwd(dh, x, g, dres, name):
    s, d = x.shape

    def body(dh_ref, x_ref, g_ref, dres_ref, dx_ref, dg_ref):
        i = pl.program_id(0)
        xv = x_ref[...]
        r = lax.rsqrt(jnp.mean(xv * xv, axis=-1, keepdims=True) + RMS_EPS)
        xh = xv * r
        dhv = dh_ref[...]
        dxh = dhv * g_ref[...]
        dx = r * (dxh - xh * jnp.mean(dxh * xh, axis=-1, keepdims=True))
        dx_ref[...] = dres_ref[...] + dx
        part = jnp.sum((dhv * xh).reshape(ROW_TILE // 8, 8, d), axis=0)

        @pl.when(i == 0)
        def _():
            dg_ref[...] = part

        @pl.when(i > 0)
        def _():
            dg_ref[...] += part

    row = pl.BlockSpec((ROW_TILE, d), lambda i: (i, 0))
    return pl.pallas_call(
        body,
        name=name,
        grid=(s // ROW_TILE,),
        in_specs=[row, row, pl.BlockSpec((1, d), lambda i: (0, 0)), row],
        out_specs=[row, pl.BlockSpec((8, d), lambda i: (0, 0))],
        out_shape=[jax.ShapeDtypeStruct((s, d), F32), jax.ShapeDtypeStruct((8, d), F32)],
        compiler_params=_cparams(("arbitrary",)),
    )(dh, x, g, dres)


def _loss_head(x2, g, target):
    s, d = x2.shape

    def body(x_ref, g_ref, t_ref, dx_ref, dg_ref, loss_ref):
        i = pl.program_id(0)
        xv = x_ref[...]
        r = lax.rsqrt(jnp.mean(xv * xv, axis=-1, keepdims=True) + RMS_EPS)
        xh = xv * r
        gv = g_ref[...]
        err = xh * gv - t_ref[...]
        dy = err * (1.0 / d)
        dxh = dy * gv
        dx_ref[...] = r * (dxh - xh * jnp.mean(dxh * xh, axis=-1, keepdims=True))
        part_g = jnp.sum((dy * xh).reshape(ROW_TILE // 8, 8, d), axis=0)
        part_l = (0.5 / d) * jnp.sum((err * err).reshape(ROW_TILE // 8, 8, d), axis=0)

        @pl.when(i == 0)
        def _():
            dg_ref[...] = part_g
            loss_ref[...] = part_l

        @pl.when(i > 0)
        def _():
            dg_ref[...] += part_g
            loss_ref[...] += part_l

    row = pl.BlockSpec((ROW_TILE, d), lambda i: (i, 0))
    acc = pl.BlockSpec((8, d), lambda i: (0, 0))
    return pl.pallas_call(
        body,
        name="loss_head",
        grid=(s // ROW_TILE,),
        in_specs=[row, pl.BlockSpec((1, d), lambda i: (0, 0)), row],
        out_specs=[row, acc, acc],
        out_shape=[jax.ShapeDtypeStruct((s, d), F32), jax.ShapeDtypeStruct((8, d), F32),
                   jax.ShapeDtypeStruct((8, d), F32)],
        compiler_params=_cparams(("arbitrary",)),
    )(x2, g, target)


def _alibi_slopes():
    return np.exp2(np.float32(-8.0) * np.arange(1, N_DIL_HEADS + 1, dtype=np.float32) / np.float32(N_DIL_HEADS))


def _head_lane_mask(h, rows):
    lane = lax.broadcasted_iota(jnp.int32, (rows, DIL_W), 1)
    return (lane >= h * HEAD_DIM) & (lane < (h + 1) * HEAD_DIM)


def _band_terms(dil, has_prev):
    qi = lax.broadcasted_iota(jnp.int32, (BLOCK, 2 * BLOCK), 0)
    kj = lax.broadcasted_iota(jnp.int32, (BLOCK, 2 * BLOCK), 1)
    steps = qi + BLOCK - kj
    valid = (steps >= 0) & (steps <= BLOCK) & ((kj >= BLOCK) | has_prev)
    return valid, steps.astype(F32) * float(dil)


def _dil_fwd(qkv_g, group):
    _, dil = DIL_GROUPS[group]
    s = qkv_g.shape[0]
    sub = s // dil
    nb = sub // BLOCK
    view = qkv_g.reshape(sub, dil * 3 * DIL_W)
    slopes = _alibi_slopes()[group * DIL_HEADS:(group + 1) * DIL_HEADS]

    def col(which):
        return lambda r, n: (n, r * 3 + which)

    def col_prev(which):
        return lambda r, n: (jnp.maximum(n - 1, 0), r * 3 + which)

    def body(q_ref, kc_ref, kp_ref, vc_ref, vp_ref, o_ref, lse_ref):
        n = pl.program_id(1)
        valid, dist = _band_terms(dil, n > 0)
        q = q_ref[...]
        k2 = jnp.concatenate([kp_ref[...], kc_ref[...]], axis=0)
        v2 = jnp.concatenate([vp_ref[...], vc_ref[...]], axis=0)
        masks = [_head_lane_mask(h, BLOCK) for h in range(DIL_HEADS)]
        logits = [_dot_nt(jnp.where(masks[h], q, jnp.zeros_like(q)), k2) for h in range(DIL_HEADS)]
        ps, lses = [], []
        for h in range(DIL_HEADS):
            lg = jnp.where(valid, logits[h] * 0.125 - float(slopes[h]) * dist, NEG_INF)
            mx = jnp.max(lg, axis=1, keepdims=True)
            lse = mx + jnp.log(jnp.sum(jnp.exp(lg - mx), axis=1, keepdims=True))
            ps.append(jnp.exp(lg - lse).astype(BF16))
            lses.append(lse)
        o_acc = jnp.zeros((BLOCK, DIL_W), F32)
        lse_acc = jnp.zeros((BLOCK, DIL_W), F32)
        for h in range(DIL_HEADS):
            o_acc = jnp.where(masks[h], _dot_nn(ps[h], v2), o_acc)
            lse_acc = jnp.where(masks[h], lses[h], lse_acc)
        o_ref[...] = o_acc
        lse_ref[...] = lse_acc

    blk = (BLOCK, DIL_W)
    o, lse = pl.pallas_call(
        body,
        name=f"dil_fwd_g{group}",
        grid=(dil, nb),
        in_specs=[pl.BlockSpec(blk, col(0)), pl.BlockSpec(blk, col(1)), pl.BlockSpec(blk, col_prev(1)),
                  pl.BlockSpec(blk, col(2)), pl.BlockSpec(blk, col_prev(2))],
        out_specs=[pl.BlockSpec(blk, lambda r, n: (n, r))] * 2,
        out_shape=[jax.ShapeDtypeStruct((sub, dil * DIL_W), F32)] * 2,
        compiler_params=_cparams(("parallel", "parallel")),
    )(view, view, view, view, view)
    return o.reshape(s, DIL_W), lse.reshape(s, DIL_W)


def _dil_bwd(qkv, do, lse, cterm, group):
    _, dil = DIL_GROUPS[group]
    s = qkv.shape[0]
    sub = s // dil
    nb = sub // BLOCK
    view = qkv.reshape(sub, dil * 3 * DIL_W)
    slopes = _alibi_slopes()[group * DIL_HEADS:(group + 1) * DIL_HEADS]
    do_v, lse_v, c_v = (t.reshape(sub, dil * DIL_W) for t in (do, lse, cterm))

    def col(which, shift):
        if shift == 0:
            return lambda r, n: (n, r * 3 + which)
        if shift < 0:
            return lambda r, n: (jnp.maximum(n - 1, 0), r * 3 + which)
        return lambda r, n: (jnp.minimum(n + 1, nb - 1), r * 3 + which)

    def own(shift):
        if shift == 0:
            return lambda r, n: (n, r)
        return lambda r, n: (jnp.minimum(n + 1, nb - 1), r)

    def body(q_ref, qn_ref, kc_ref, kp_ref, vc_ref, vp_ref, do_ref, don_ref, lse_ref, lsen_ref, c_ref, cn_ref,
             dq_ref, dk_ref, dv_ref):
        n = pl.program_id(1)
        valid, dist = _band_terms(dil, n > 0)
        valid_n = _band_terms(dil, True)[0][:, :BLOCK] & (n < nb - 1)
        dist_n = dist[:, :BLOCK]
        q, qn = q_ref[...], qn_ref[...]
        kc, vc = kc_ref[...], vc_ref[...]
        k2 = jnp.concatenate([kp_ref[...], kc], axis=0)
        v2 = jnp.concatenate([vp_ref[...], vc], axis=0)
        dov, donv = do_ref[...], don_ref[...]
        lsev, lsenv, cv, cnv = lse_ref[...], lsen_ref[...], c_ref[...], cn_ref[...]
        masks = [_head_lane_mask(h, BLOCK) for h in range(DIL_HEADS)]

        def head_col(t, hm):
            return jnp.max(jnp.where(hm, t, NEG_INF), axis=1, keepdims=True)

        qhs = [jnp.where(hm, q, jnp.zeros_like(q)) for hm in masks]
        qnhs = [jnp.where(hm, qn, jnp.zeros_like(qn)) for hm in masks]
        dohs = [jnp.where(hm, dov, 0.0).astype(BF16) for hm in masks]
        donhs = [jnp.where(hm, donv, 0.0).astype(BF16) for hm in masks]
        logit = [_dot_nt(qhs[h], k2) for h in range(DIL_HEADS)]
        dp = [_dot_nt(dohs[h], v2) for h in range(DIL_HEADS)]
        logit_n = [_dot_nt(qnhs[h], kc) for h in range(DIL_HEADS)]
        dp_n = [_dot_nt(donhs[h], vc) for h in range(DIL_HEADS)]
        p16, dlog, pn16, dlog_n = [], [], [], []
        for h in range(DIL_HEADS):
            hm, slope = masks[h], float(slopes[h])
            p = jnp.where(valid, jnp.exp(logit[h] * 0.125 - slope * dist - head_col(lsev, hm)), 0.0)
            dlog.append((p * (dp[h] + head_col(cv, hm)) * 0.125).astype(BF16))
            p16.append(p.astype(BF16))
            pn = jnp.where(valid_n, jnp.exp(logit_n[h] * 0.125 - slope * dist_n - head_col(lsenv, hm)), 0.0)
            dlog_n.append((pn * (dp_n[h] + head_col(cnv, hm)) * 0.125).astype(BF16))
            pn16.append(pn.astype(BF16))
        dq_acc = jnp.zeros((BLOCK, DIL_W), F32)
        dk_acc = jnp.zeros((BLOCK, DIL_W), F32)
        dv_acc = jnp.zeros((BLOCK, DIL_W), F32)
        for h in range(DIL_HEADS):
            dq_acc = jnp.where(masks[h], _dot_nn(dlog[h], k2), dq_acc)
            dk_acc += _dot_tn(dlog[h][:, BLOCK:], qhs[h]) + _dot_tn(dlog_n[h], qnhs[h])
            dv_acc += _dot_tn(p16[h][:, BLOCK:], dohs[h]) + _dot_tn(pn16[h], donhs[h])
        dq_ref[...] = dq_acc.astype(BF16)
        dk_ref[...] = dk_acc.astype(BF16)
        dv_ref[...] = dv_acc.astype(BF16)

    blk = (BLOCK, DIL_W)
    outs = pl.pallas_call(
        body,
        name=f"dil_bwd_g{group}",
        grid=(dil, nb),
        in_specs=[pl.BlockSpec(blk, col(0, 0)), pl.BlockSpec(blk, col(0, 1)),
                  pl.BlockSpec(blk, col(1, 0)), pl.BlockSpec(blk, col(1, -1)),
                  pl.BlockSpec(blk, col(2, 0)), pl.BlockSpec(blk, col(2, -1)),
                  pl.BlockSpec(blk, own(0)), pl.BlockSpec(blk, own(1)),
                  pl.BlockSpec(blk, own(0)), pl.BlockSpec(blk, own(1)),
                  pl.BlockSpec(blk, own(0)), pl.BlockSpec(blk, own(1))],
        out_specs=[pl.BlockSpec(blk, lambda r, n: (n, r))] * 3,
        out_shape=[jax.ShapeDtypeStruct((sub, dil * DIL_W), BF16)] * 3,
        compiler_params=_cparams(("parallel", "parallel")),
    )(view, view, view, view, view, view, do_v, do_v, lse_v, lse_v, c_v, c_v)
    return tuple(t.reshape(s, DIL_W) for t in outs)


SB_PAIRS = SB_HEADS // 2
SB_COL0 = 0
LOG2E = 1.4426950408889634


SB_EXP_CLAMP = 64.0


def _sb_softplus2(zs):
    t = 1.0 + jnp.exp2(jnp.minimum(zs, SB_EXP_CLAMP))
    return jnp.maximum(jnp.log(t) * LOG2E, zs)


def _sb_consts(nkb):
    row = lax.broadcasted_iota(jnp.int32, (SB_BQ, SB_BK), 0)
    colk = lax.broadcasted_iota(jnp.int32, (SB_BQ, SB_BK), 1)
    rr = lax.broadcasted_iota(jnp.int32, (SB_BK, SB_BK), 0)
    cc = lax.broadcasted_iota(jnp.int32, (SB_BK, SB_BK), 1)
    lane = lax.broadcasted_iota(jnp.int32, (SB_BQ, 128), 1)
    assert 2 * nkb <= 128
    return colk < row, rr, cc, lane < HEAD_DIM, lane


def _split_heads(t):
    first = lax.broadcasted_iota(jnp.int32, t.shape, 1) < HEAD_DIM
    zero = jnp.zeros_like(t)
    return jnp.where(first, t, zero), jnp.where(first, zero, t)


def _sb_fwd(qkv):
    s = qkv.shape[0]
    nq, nkb = s // SB_BQ, s // SB_BK
    zscale = LOG2E / math.sqrt(HEAD_DIM)

    def body(q_ref, k_ref, v_ref, o_ref, carry_ref, zs_scr, a_scr, acc_scr, cl_scr):
        i = pl.program_id(1)
        causal, rr, cc, _, lane = _sb_consts(nkb)
        later = (rr > cc).astype(BF16)
        qh = _split_heads(q_ref[...])

        def rows(j):
            return pl.ds(pl.multiple_of(j * SB_BK, SB_BK), SB_BK)

        def scores_to(slot, j):
            kb = k_ref[rows(j), :]
            for hh in range(2):
                zs_scr[slot, hh] = _dot_nt(qh[hh], kb) * zscale

        def weights(slot, j, masked):
            xs, sums, sufs = [], [], []
            for hh in range(2):
                zs = zs_scr[slot, hh]
                sp = _sb_softplus2(zs)
                if masked:
                    sp = jnp.where(causal, sp, 0.0)
                xs.append(zs - sp)
                sums.append(jnp.sum(sp, axis=1, keepdims=True))
                sufs.append(_dot_f32_by_01(sp, later, 2))
            for hh in range(2):
                cl = cl_scr[hh]
                a = jnp.exp2(xs[hh] - (sufs[hh] + jnp.concatenate([cl, cl], axis=1)))
                if masked:
                    a = jnp.where(causal, a, 0.0)
                a_scr[slot, :, hh * SB_BK:(hh + 1) * SB_BK] = a.astype(BF16)
            for hh in range(2):
                cl = cl_scr[hh]
                carry_ref[0] = jnp.where(lane == j + hh * nkb, cl, carry_ref[0])
                cl_scr[hh] = cl + sums[hh]

        def add_av(slot, j):
            v0, v1 = _split_heads(v_ref[rows(j), :])
            acc_scr[...] += _dot_nn(a_scr[slot], jnp.concatenate([v0, v1], axis=0))

        acc_scr[...] = jnp.zeros_like(acc_scr)
        cl_scr[...] = jnp.zeros_like(cl_scr)
        carry_ref[...] = jnp.zeros_like(carry_ref)
        scores_to(0, i)
        scores_to(1, jnp.maximum(i - 1, 0))
        weights(0, i, True)

        def step(j, prev, cur):
            scores_to(prev, jnp.maximum(j - 1, 0))
            add_av(prev, j + 1)
            weights(cur, j, False)

        def two_steps(u, _):
            j = i - 1 - 2 * u
            step(j, 0, 1)
            step(j - 1, 1, 0)
            return 0

        lax.fori_loop(0, i // 2, two_steps, 0)

        @pl.when(i % 2 == 1)
        def _():
            step(0, 0, 1)
            add_av(1, 0)

        @pl.when(i % 2 == 0)
        def _():
            add_av(0, 0)

        o_ref[...] = acc_scr[...]

    def full(which):
        return pl.BlockSpec((s, 128), lambda p, i: (0, SB_COL0 + 4 * which + p))

    return pl.pallas_call(
        body,
        name="sb_fwd",
        grid=(SB_PAIRS, nq),
        in_specs=[pl.BlockSpec((SB_BQ, 128), lambda p, i: (i, SB_COL0 + p)), full(1), full(2)],
        out_specs=[pl.BlockSpec((SB_BQ, 128), lambda p, i: (i, p)),
                   pl.BlockSpec((1, SB_BQ, 128), lambda p, i: (p, i, 0))],
        out_shape=[jax.ShapeDtypeStruct((s, SB_W), F32), jax.ShapeDtypeStruct((SB_PAIRS, s, 128), F32)],
        scratch_shapes=[pltpu.VMEM((2, 2, SB_BQ, SB_BK), F32), pltpu.VMEM((2, SB_BQ, 2 * SB_BK), BF16),
                        pltpu.VMEM((SB_BQ, 128), F32), pltpu.VMEM((2, SB_BQ, 128), F32)],
        compiler_params=_cparams(("parallel", "parallel")),
    )(qkv, qkv, qkv)


def _sb_bwd(qkv, do, carries):
    s = qkv.shape[0]
    nq, nkb = s // SB_BQ, s // SB_BK
    scale = 1.0 / math.sqrt(HEAD_DIM)
    zscale = LOG2E * scale

    def body(q_ref, k_ref, v_ref, do_ref, carry_ref, dq_ref, dk_ref, dv_ref, zs_scr, da_scr, dz_scr, a_scr, cg_scr):
        i = pl.program_id(1)

        @pl.when(i == 0)
        def _():
            dk_ref[...] = jnp.zeros_like(dk_ref)
            dv_ref[...] = jnp.zeros_like(dv_ref)

        causal, rr, cc, first, lane = _sb_consts(nkb)
        later = (rr > cc).astype(BF16)
        earlier = (rr < cc).astype(BF16)
        q2 = q_ref[...]
        qh = _split_heads(q2)
        do2 = do_ref[...].astype(BF16)
        doh = _split_heads(do2)
        ctile = carry_ref[0]

        def rows(j):
            return pl.ds(pl.multiple_of(j * SB_BK, SB_BK), SB_BK)

        def products_to(slot, j):
            kb, vb = k_ref[rows(j), :], v_ref[rows(j), :]
            for hh in range(2):
                zs_scr[slot, hh] = _dot_nt(qh[hh], kb) * zscale
                da_scr[slot, hh] = _dot_nt(doh[hh], vb)

        def by_head(t):
            return jnp.where(first, t[:SB_BK], t[SB_BK:])

        def apply(slot, j):
            k0, k1 = _split_heads(k_ref[rows(j), :])
            dq_ref[...] += _dot_nn(dz_scr[slot], jnp.concatenate([k0, k1], axis=0)) * scale
            dk_ref[rows(j), :] += by_head(_dot_tn(dz_scr[slot], q2)) * scale
            dv_ref[rows(j), :] += by_head(_dot_tn(a_scr[slot], do2))

        def grads(slot, j, masked):
            xs, sigs, sufs = [], [], []
            for hh in range(2):
                zs = zs_scr[slot, hh]
                sp = _sb_softplus2(zs)
                x = zs - sp
                xs.append(x)
                sigs.append(jnp.exp2(x))
                if masked:
                    sp = jnp.where(causal, sp, 0.0)
                sufs.append(_dot_f32_by_01(sp, later, 2))
            gs, gpres = [], []
            for hh in range(2):
                cl = jnp.sum(jnp.where(lane == j + hh * nkb, ctile, 0.0), axis=1, keepdims=True)
                a = jnp.exp2(xs[hh] - (sufs[hh] + cl))
                if masked:
                    a = jnp.where(causal, a, 0.0)
                g = a * da_scr[slot, hh]
                a_scr[slot, :, hh * SB_BK:(hh + 1) * SB_BK] = a.astype(BF16)
                gs.append(g)
                gpres.append(_dot_f32_by_01(g, earlier, 2))
            for hh in range(2):
                cg = cg_scr[hh]
                dz = gs[hh] - (gs[hh] + (gpres[hh] + jnp.concatenate([cg, cg], axis=1))) * sigs[hh]
                if masked:
                    dz = jnp.where(causal, dz, 0.0)
                dz_scr[slot, :, hh * SB_BK:(hh + 1) * SB_BK] = dz.astype(BF16)
                cg_scr[hh] = cg + jnp.sum(gs[hh], axis=1, keepdims=True)

        dq_ref[...] = jnp.zeros_like(dq_ref)
        cg_scr[...] = jnp.zeros_like(cg_scr)
        dz_scr[1] = jnp.zeros((SB_BQ, 2 * SB_BK), BF16)
        a_scr[1] = jnp.zeros((SB_BQ, 2 * SB_BK), BF16)
        products_to(0, 0)

        def step(j, cur, nxt):
            products_to(nxt, j + 1)
            apply(nxt, jnp.maximum(j - 1, 0))
            grads(cur, j, False)

        def two_steps(u, _):
            step(2 * u, 0, 1)
            step(2 * u + 1, 1, 0)
            return 0

        lax.fori_loop(0, i // 2, two_steps, 0)

        def last(cur, nxt):
            apply(nxt, jnp.maximum(i - 1, 0))
            grads(cur, i, True)
            apply(cur, i)

        @pl.when(i % 2 == 1)
        def _():
            step(i - 1, 0, 1)
            last(1, 0)

        @pl.when(i % 2 == 0)
        def _():
            last(0, 1)

    def full(which):
        return pl.BlockSpec((s, 128), lambda p, i: (0, SB_COL0 + 4 * which + p))

    qblk = pl.BlockSpec((SB_BQ, 128), lambda p, i: (i, p))
    acc = pl.BlockSpec((s, 128), lambda p, i: (0, p))
    return pl.pallas_call(
        body,
        name="sb_bwd",
        grid=(SB_PAIRS, nq),
        in_specs=[pl.BlockSpec((SB_BQ, 128), lambda p, i: (i, SB_COL0 + p)), full(1), full(2), qblk,
                  pl.BlockSpec((1, SB_BQ, 128), lambda p, i: (p, i, 0))],
        out_specs=[qblk, acc, acc],
        out_shape=[jax.ShapeDtypeStruct((s, SB_W), F32)] * 3,
        scratch_shapes=[pltpu.VMEM((2, 2, SB_BQ, SB_BK), F32), pltpu.VMEM((2, 2, SB_BQ, SB_BK), F32),
                        pltpu.VMEM((2, SB_BQ, 2 * SB_BK), BF16), pltpu.VMEM((2, SB_BQ, 2 * SB_BK), BF16),
                        pltpu.VMEM((2, SB_BQ, 128), F32)],
        compiler_params=_cparams(("parallel", "arbitrary")),
    )(qkv, qkv, qkv, do, carries)


MERGE_TILE = 256


def _group_mix(lses):
    mx = jnp.maximum(jnp.maximum(lses[0], lses[1]), lses[2])
    es = [jnp.exp(t - mx) for t in lses]
    den = es[0] + es[1] + es[2]
    return [e / den for e in es]


def _merge_fwd(o_groups, lse_groups, o_sb, gl, b_gate, w_up_dil, w_up_sb):
    s = gl.shape[0]
    t = MERGE_TILE

    def body(o0, o1, o2, l0, l1, l2, ob_ref, gl_ref, bg_ref, wd_ref, ws_ref, merged_ref, oa_ref):
        w = _group_mix([l0[...], l1[...], l2[...]])
        oa = (w[0] * o0[...] + w[1] * o1[...] + w[2] * o2[...]).astype(BF16)
        ua = _dot_nn(oa, wd_ref[...])
        ub = _dot_nn(ob_ref[...].astype(BF16), ws_ref[...])
        gate = jax.nn.sigmoid(gl_ref[...] + bg_ref[...])
        merged_ref[...] = (gate[:, :D_MODEL] * ua + gate[:, D_MODEL:] * ub).astype(BF16)
        oa_ref[...] = oa

    dil = pl.BlockSpec((t, DIL_W), lambda i: (i, 0))
    const = lambda shape: pl.BlockSpec(shape, lambda i: (0, 0))
    return pl.pallas_call(
        body,
        name="merge_fwd",
        grid=(s // t,),
        in_specs=[dil] * 6 + [pl.BlockSpec((t, SB_W), lambda i: (i, 0)), pl.BlockSpec((t, GATE_W), lambda i: (i, 0)),
                              const((1, GATE_W)), const((DIL_W, D_MODEL)), const((SB_W, D_MODEL))],
        out_specs=[pl.BlockSpec((t, D_MODEL), lambda i: (i, 0)), dil],
        out_shape=[jax.ShapeDtypeStruct((s, D_MODEL), BF16), jax.ShapeDtypeStruct((s, DIL_W), BF16)],
        compiler_params=_cparams(("parallel",)),
    )(*o_groups, *lse_groups, o_sb, gl, b_gate, w_up_dil, w_up_sb)


def _merge_bwd(dmerged, o_groups, lse_groups, o_sb, gl, b_gate, w_up_dil, w_up_sb):
    s = gl.shape[0]
    t = MERGE_TILE

    def body(dm_ref, o0, o1, o2, l0, l1, l2, ob_ref, gl_ref, bg_ref, wd_ref, ws_ref,
             dua_ref, dub_ref, dgl_ref, dbg_ref, dosb_ref, d0, d1, d2, c0, c1, c2):
        i = pl.program_id(0)
        og = [o0[...], o1[...], o2[...]]
        w = _group_mix([l0[...], l1[...], l2[...]])
        oa = (w[0] * og[0] + w[1] * og[1] + w[2] * og[2]).astype(BF16)
        ua = _dot_nn(oa, wd_ref[...])
        ub = _dot_nn(ob_ref[...].astype(BF16), ws_ref[...])
        gate = jax.nn.sigmoid(gl_ref[...] + bg_ref[...])
        ga, gb = gate[:, :D_MODEL], gate[:, D_MODEL:]
        dm = dm_ref[...]
        dua = (dm * ga).astype(BF16)
        dub = (dm * gb).astype(BF16)
        dua_ref[...] = dua
        dub_ref[...] = dub
        dgl_a = dm * ua * ga * (1.0 - ga)
        dgl_b = dm * ub * gb * (1.0 - gb)
        dgl_ref[:, :D_MODEL] = dgl_a.astype(BF16)
        dgl_ref[:, D_MODEL:] = dgl_b.astype(BF16)
        part = jnp.concatenate([jnp.sum(dgl_a.reshape(t // 8, 8, D_MODEL), axis=0),
                                jnp.sum(dgl_b.reshape(t // 8, 8, D_MODEL), axis=0)], axis=1)

        @pl.when(i == 0)
        def _():
            dbg_ref[...] = part

        @pl.when(i > 0)
        def _():
            dbg_ref[...] += part

        dosb_ref[...] = _dot_nt(dub, ws_ref[...])
        doa = _dot_nt(dua, wd_ref[...])
        rr = lax.broadcasted_iota(jnp.int32, (DIL_W, DIL_W), 0) // HEAD_DIM
        cc = lax.broadcasted_iota(jnp.int32, (DIL_W, DIL_W), 1) // HEAD_DIM
        same_head = (rr == cc).astype(BF16)
        dw = [_dot_f32_by_01(doa * og[g], same_head) for g in range(3)]
        mean_dw = w[0] * dw[0] + w[1] * dw[1] + w[2] * dw[2]
        for g, (d_ref, c_ref) in enumerate(((d0, c0), (d1, c1), (d2, c2))):
            d_ref[...] = w[g] * doa
            c_ref[...] = -w[g] * mean_dw

    dil = pl.BlockSpec((t, DIL_W), lambda i: (i, 0))
    wide = pl.BlockSpec((t, D_MODEL), lambda i: (i, 0))
    gate2 = pl.BlockSpec((t, GATE_W), lambda i: (i, 0))
    sbw = pl.BlockSpec((t, SB_W), lambda i: (i, 0))
    const = lambda shape: pl.BlockSpec(shape, lambda i: (0, 0))
    return pl.pallas_call(
        body,
        name="merge_bwd",
        grid=(s // t,),
        in_specs=[wide] + [dil] * 6 + [sbw, gate2, const((1, GATE_W)), const((DIL_W, D_MODEL)), const((SB_W, D_MODEL))],
        out_specs=[wide, wide, gate2, const((8, GATE_W)), sbw] + [dil] * 6,
        out_shape=[jax.ShapeDtypeStruct((s, D_MODEL), BF16), jax.ShapeDtypeStruct((s, D_MODEL), BF16),
                   jax.ShapeDtypeStruct((s, GATE_W), BF16), jax.ShapeDtypeStruct((8, GATE_W), F32),
                   jax.ShapeDtypeStruct((s, SB_W), F32)] + [jax.ShapeDtypeStruct((s, DIL_W), F32)] * 6,
        compiler_params=_cparams(("arbitrary",)),
    )(dmerged, *o_groups, *lse_groups, o_sb, gl, b_gate, w_up_dil, w_up_sb)


ANY = pl.BlockSpec(memory_space=pl.ANY)


def _place():
    x, y, c = lax.axis_index("x"), lax.axis_index("y"), lax.axis_index("c")
    other_chips = [(1 - x, y), (x, 1 - y), (1 - x, 1 - y)]
    return x, y, c, other_chips


def _all_gather_weights(pack):
    r, wd = pack.shape
    rh = r // 2
    pack = pack.reshape(2, rh, wd)

    def body(p_ref, out_ref, send_sems, recv_sems):
        x, y, c, chips = _place()
        me, sibling = 2 * x + y, (x, y, 1 - c)

        def half(chip_idx, core):
            return out_ref.at[chip_idx, core]

        def copy(k, chip_idx, core, to, src=None):
            return pltpu.make_async_remote_copy(
                src_ref=half(chip_idx, core) if src is None else src, dst_ref=half(chip_idx, core),
                send_sem=send_sems.at[k], recv_sem=recv_sems.at[k], device_id=to, device_id_type=MESH)

        first = [copy(j, me, c, (*chip, c), src=p_ref.at[c]) for j, chip in enumerate(chips)]
        for cp in first:
            cp.start()
        passed = [copy(3 + j, 2 * chip[0] + chip[1], c, sibling) for j, chip in enumerate(chips)]
        for j, chip in enumerate(chips):
            copy(j, 2 * chip[0] + chip[1], c, (x, y, c)).wait_recv()
            passed[j].start()
        for j, chip in enumerate(chips):
            copy(3 + j, 2 * chip[0] + chip[1], 1 - c, (x, y, c)).wait_recv()
        for cp in first + passed:
            cp.wait_send()

    others = pl.pallas_call(
        body,
        name="all_gather_weights",
        in_specs=[ANY],
        out_specs=ANY,
        out_shape=jax.ShapeDtypeStruct((N_CHIPS, 2, rh, wd), pack.dtype),
        scratch_shapes=[pltpu.SemaphoreType.DMA((6,)), pltpu.SemaphoreType.DMA((6,))],
    )(pack)
    me = 2 * lax.axis_index("x") + lax.axis_index("y")
    return lax.dynamic_update_slice(others, pack[None], (me, 0, 0, 0)).reshape(N_CHIPS, r, wd)


def _swap_halves(g):
    n, r, wd = g.shape
    rh = r // 2
    g = g.reshape(n, 2, rh, wd)

    def body(g_ref, out_ref, send_sem, recv_sem):
        x, y, c, _ = _place()
        cp = pltpu.make_async_remote_copy(
            src_ref=g_ref.at[:, 1 - c], dst_ref=out_ref,
            send_sem=send_sem, recv_sem=recv_sem, device_id=(x, y, 1 - c), device_id_type=MESH)
        cp.start()
        cp.wait()

    return pl.pallas_call(
        body,
        name="grad_swap_halves",
        in_specs=[ANY],
        out_specs=ANY,
        out_shape=jax.ShapeDtypeStruct((n, rh, wd), g.dtype),
        scratch_shapes=[pltpu.SemaphoreType.DMA, pltpu.SemaphoreType.DMA],
    )(g)


def _add_halves(g, got, core):
    n, r, wd = g.shape
    rh = r // 2
    t = rh // 4
    nt = rh // t

    def body(c_ref, a_ref, b_ref, o_ref):
        o_ref[...] = (a_ref[0] + b_ref[...]).astype(BF16)

    grid_spec = pltpu.PrefetchScalarGridSpec(
        num_scalar_prefetch=1,
        grid=(n, nt),
        in_specs=[pl.BlockSpec((1, 1, t, wd), lambda s, i, c: (s, c[0], i, 0)),
                  pl.BlockSpec((1, t, wd), lambda s, i, c: (s, i, 0))],
        out_specs=pl.BlockSpec((1, t, wd), lambda s, i, c: (s, i, 0)),
    )
    return pl.pallas_call(
        body,
        name="grad_add_halves",
        grid_spec=grid_spec,
        out_shape=jax.ShapeDtypeStruct((n, rh, wd), BF16),
        compiler_params=_cparams(("parallel", "parallel")),
    )(core, g.reshape(n, 2, rh, wd), got)


def _exchange_chunks(h):
    n, rh, wd = h.shape

    def body(h_ref, out_ref, send_sems, recv_sems):
        x, y, c, chips = _place()
        me = 2 * x + y
        sends = []
        for j, chip in enumerate(chips):
            them = 2 * chip[0] + chip[1]
            sends.append(pltpu.make_async_remote_copy(
                src_ref=h_ref.at[them], dst_ref=out_ref.at[me],
                send_sem=send_sems.at[j], recv_sem=recv_sems.at[j], device_id=(*chip, c), device_id_type=MESH))
        for cp in sends:
            cp.start()
        for j, chip in enumerate(chips):
            them = 2 * chip[0] + chip[1]
            pltpu.make_async_remote_copy(
                src_ref=h_ref.at[them], dst_ref=out_ref.at[them],
                send_sem=send_sems.at[j], recv_sem=recv_sems.at[j], device_id=(*chip, c), device_id_type=MESH).wait_recv()
        for cp in sends:
            cp.wait_send()

    return pl.pallas_call(
        body,
        name="grad_exchange_chunks",
        in_specs=[ANY],
        out_specs=ANY,
        out_shape=jax.ShapeDtypeStruct((n, rh, wd), h.dtype),
        scratch_shapes=[pltpu.SemaphoreType.DMA((3,)), pltpu.SemaphoreType.DMA((3,))],
    )(h)


def _sum_chips(b, h, chip):
    n, rh, wd = b.shape
    t = rh // 4

    def body(chip_ref, b_ref, own_ref, o_ref):
        own = own_ref[0]
        s0, s1, s2, s3 = (jnp.where(chip_ref[0] == k, own, b_ref[k]).astype(F32) for k in range(n))
        o_ref[...] = ((s0 + s1) + s2) + s3

    grid_spec = pltpu.PrefetchScalarGridSpec(
        num_scalar_prefetch=1,
        grid=(rh // t,),
        in_specs=[pl.BlockSpec((n, t, wd), lambda i, chip: (0, i, 0)),
                  pl.BlockSpec((1, t, wd), lambda i, chip: (chip[0], i, 0))],
        out_specs=pl.BlockSpec((t, wd), lambda i, chip: (i, 0)),
    )
    return pl.pallas_call(
        body,
        name="grad_sum_chips",
        grid_spec=grid_spec,
        out_shape=jax.ShapeDtypeStruct((rh, wd), F32),
        compiler_params=_cparams(("parallel",)),
    )(chip, b, h)


def _join_halves(tc):
    rh, wd = tc.shape

    def body(t_ref, out_ref, send_sem, recv_sem):
        x, y, c, _ = _place()
        cp = pltpu.make_async_remote_copy(
            src_ref=t_ref, dst_ref=out_ref.at[c],
            send_sem=send_sem, recv_sem=recv_sem, device_id=(x, y, 1 - c), device_id_type=MESH)
        cp.start()
        cp.wait()

    halves = pl.pallas_call(
        body,
        name="grad_join_halves",
        in_specs=[ANY],
        out_specs=ANY,
        out_shape=jax.ShapeDtypeStruct((2, rh, wd), tc.dtype),
        scratch_shapes=[pltpu.SemaphoreType.DMA, pltpu.SemaphoreType.DMA],
    )(tc)
    return lax.dynamic_update_slice(halves, tc[None], (lax.axis_index("c"), 0, 0)).reshape(2 * rh, wd)


def _all_reduce_small(pack):
    rows, lanes = pack.shape

    def body(p_ref, out_ref, buf, send_sems, recv_sems):
        x, y, c, _ = _place()
        me = 4 * x + 2 * y + c
        buf[me] = p_ref[...]
        sends = []
        for k in range(1, N_DEV):
            peer = (x ^ (k >> 2), y ^ ((k >> 1) & 1), c ^ (k & 1))
            sends.append(pltpu.make_async_remote_copy(
                src_ref=p_ref, dst_ref=buf.at[me], send_sem=send_sems.at[k - 1], recv_sem=recv_sems.at[k - 1],
                device_id=peer, device_id_type=MESH))
        for cp in sends:
            cp.start()
        for k in range(1, N_DEV):
            pltpu.make_async_remote_copy(
                src_ref=p_ref, dst_ref=buf.at[me ^ k], send_sem=send_sems.at[k - 1], recv_sem=recv_sems.at[k - 1],
                device_id=(x, y, c), device_id_type=MESH).wait_recv()
        for cp in sends:
            cp.wait_send()
        total = buf[0]
        for d in range(1, N_DEV):
            total = total + buf[d]
        out_ref[...] = total

    vm = pl.BlockSpec(memory_space=pltpu.VMEM)
    return pl.pallas_call(
        body,
        name="all_reduce_small",
        in_specs=[vm],
        out_specs=vm,
        out_shape=jax.ShapeDtypeStruct((rows, lanes), F32),
        scratch_shapes=[pltpu.VMEM((N_DEV, rows, lanes), F32), pltpu.SemaphoreType.DMA((N_DEV - 1,)),
                        pltpu.SemaphoreType.DMA((N_DEV - 1,))],
    )(pack)


def _adamw(g, w, m, v, name):
    rows, cols = g.shape
    t = rows
    for cand in (256, 128, 64, 32, 16, 8):
        if rows % cand == 0:
            t = cand
            break

    def body(g_ref, w_ref, m_ref, v_ref, d_ref, nm_ref, nv_ref):
        gv = g_ref[...]
        mv = ADAM_B1 * m_ref[...] + (1.0 - ADAM_B1) * gv
        vv = ADAM_B2 * v_ref[...] + (1.0 - ADAM_B2) * (gv * gv)
        m_hat = mv / (1.0 - ADAM_B1 ** ADAM_STEP)
        v_hat = vv / (1.0 - ADAM_B2 ** ADAM_STEP)
        d_ref[...] = -ADAM_LR * (m_hat / (jnp.sqrt(v_hat) + ADAM_EPS) + ADAM_WD * w_ref[...])
        nm_ref[...] = mv
        nv_ref[...] = vv

    blk = pl.BlockSpec((t, cols), lambda i: (i, 0))
    return pl.pallas_call(
        body,
        name=name,
        grid=(rows // t,),
        in_specs=[blk] * 4,
        out_specs=[blk] * 3,
        out_shape=[jax.ShapeDtypeStruct((rows, cols), F32)] * 3,
        compiler_params=_cparams(("parallel",)),
    )(g, w, m, v)


PACK_W = 1024
BIG = (("w_in", (D_MODEL, IN_COLS), 1), ("w_up_dil", (DIL_W, D_MODEL), 1), ("w_up_sb", (SB_W, D_MODEL), 1),
       ("w_out", (D_MODEL, D_MODEL), 0), ("w_mlp_in", (D_MODEL, D_FF), 1), ("w_mlp_out", (D_FF, D_MODEL), 0))


def _shard_shape(shape, axis):
    return tuple(d // N_CHIPS if a == axis else d for a, d in enumerate(shape))


def _pack_rows():
    rows, at = {}, 0
    for name, shape, axis in BIG:
        n = math.prod(_shard_shape(shape, axis)) // PACK_W
        rows[name] = (at, n)
        at += n
    return rows, at


def _pack_shards(shards):
    return jnp.concatenate([shards[name].reshape(-1, PACK_W) for name, _, _ in BIG], axis=0)


def _unpack_full(gathered):
    rows, _ = _pack_rows()
    full = {}
    for name, shape, axis in BIG:
        at, n = rows[name]
        parts = gathered[:, at:at + n, :].reshape((N_CHIPS,) + _shard_shape(shape, axis))
        if axis == 0:
            full[name] = parts.reshape(shape)
        else:
            full[name] = jnp.transpose(parts, (1, 0, 2)).reshape(shape)
    return full


def _pack_full_grads(grads):
    chunks = []
    for name, shape, axis in BIG:
        g = grads[name]
        if axis == 0:
            parts = g.reshape((N_CHIPS, shape[0] // N_CHIPS, shape[1]))
        else:
            parts = jnp.transpose(g.reshape((shape[0], N_CHIPS, shape[1] // N_CHIPS)), (1, 0, 2))
        chunks.append(parts.reshape(N_CHIPS, -1, PACK_W))
    return jnp.concatenate(chunks, axis=1)


def _unpack_shard(packed):
    rows, _ = _pack_rows()
    return {name: packed[rows[name][0]:rows[name][0] + rows[name][1]].reshape(_shard_shape(shape, axis))
            for name, shape, axis in BIG}


def _local_step(x, target, w, norm_mix_g, b_gate, norm_mlp_g, norm_final_g):
    w_in = w["w_in"]
    sb0 = 9 * DIL_W
    w_sb, w_gate = w_in[:, sb0:QKV_W], w_in[:, QKV_W:]
    w_dil = [jnp.concatenate([w_in[:, (3 * i + g) * DIL_W:(3 * i + g + 1) * DIL_W] for i in range(3)], axis=1)
             for g in range(3)]

    h = _rms_fwd(x, norm_mix_g, "norm_mix")
    qkv_dil = [_matmul(h, w_dil[g], mode="nn", out_dtypes=(BF16,), name=f"proj_dil_g{g}", tn=768)[0] for g in range(3)]
    (qkv_sb,) = _matmul(h, w_sb, mode="nn", out_dtypes=(BF16,), name="proj_sb", tn=768)
    (gl,) = _matmul(h, w_gate, mode="nn", out_dtypes=(F32,), name="proj_gate")
    dil = [_dil_fwd(qkv_dil[g], g) for g in range(3)]
    o_groups, lse_groups = [d[0] for d in dil], [d[1] for d in dil]
    o_sb, carries = _sb_fwd(qkv_sb)
    merged, o_a = _merge_fwd(o_groups, lse_groups, o_sb, gl, b_gate, w["w_up_dil"], w["w_up_sb"])
    (x1,) = _matmul(merged, w["w_out"], mode="nn", out_dtypes=(F32,), name="out_proj",
                    extras=(x,), epilogue=lambda acc, res: (res + acc,))
    h2 = _rms_fwd(x1, norm_mlp_g, "norm_mlp")
    u, act = _matmul(h2, w["w_mlp_in"], mode="nn", out_dtypes=(F32, BF16), name="mlp_in",
                     epilogue=lambda acc: (acc, jnp.square(jnp.maximum(acc, 0.0))))
    (x2,) = _matmul(act, w["w_mlp_out"], mode="nn", out_dtypes=(F32,), name="mlp_out", tk=2048,
                    extras=(x1,), epilogue=lambda acc, res: (res + acc,))
    dx2, dg_final, loss_part = _loss_head(x2, norm_final_g.reshape(1, D_MODEL), target)

    (du,) = _matmul(dx2, w["w_mlp_out"], mode="nt", out_dtypes=(BF16,), name="mlp_out_dx",
                    extras=(u,), epilogue=lambda acc, uu: (acc * (2.0 * jnp.maximum(uu, 0.0)),))
    (g_mlp_out,) = _matmul(act, dx2, mode="tn", out_dtypes=(F32,), name="mlp_out_dw")
    (g_mlp_in,) = _matmul(h2, du, mode="tn", out_dtypes=(F32,), name="mlp_in_dw")
    (dh2,) = _matmul(du, w["w_mlp_in"], mode="nt", out_dtypes=(F32,), name="mlp_in_dx", tk=2048)
    dx1, dg_mlp = _rms_bwd(dh2, x1, norm_mlp_g, dx2, "norm_mlp_bwd")

    (dmerged,) = _matmul(dx1, w["w_out"], mode="nt", out_dtypes=(F32,), name="out_proj_dx")
    (g_out,) = _matmul(merged, dx1, mode="tn", out_dtypes=(F32,), name="out_proj_dw")
    mb = _merge_bwd(dmerged, o_groups, lse_groups, o_sb, gl, b_gate, w["w_up_dil"], w["w_up_sb"])
    dua, dub, dgl, dbg, do_sb = mb[:5]
    do_groups, c_groups = mb[5:8], mb[8:11]
    (g_up_dil,) = _matmul(o_a, dua, mode="tn", out_dtypes=(F32,), name="up_dil_dw")
    (g_up_sb,) = _matmul(o_sb, dub, mode="tn", out_dtypes=(F32,), name="up_sb_dw")
    dq_sb, dk_sb, dv_sb = _sb_bwd(qkv_sb, do_sb, carries)
    dil_b = [_dil_bwd(qkv_dil[g], do_groups[g], lse_groups[g], c_groups[g], g) for g in range(3)]
    dproj = jnp.concatenate(
        [dil_b[g][i].astype(BF16) for i in range(3) for g in range(3)]
        + [t.astype(BF16) for t in (dq_sb, dk_sb, dv_sb)] + [dgl], axis=1)
    (g_in,) = _matmul(h, dproj, mode="tn", out_dtypes=(F32,), name="proj_dw", tm=512, tn=IN_COLS // 2)
    (dh,) = _matmul(dproj, w["w_in"], mode="nt", out_dtypes=(F32,), name="proj_dx", tk=IN_COLS // 2)
    grad_x, dg_mix = _rms_bwd(dh, x, norm_mix_g, dx1, "norm_mix_bwd")

    big = {"w_in": g_in, "w_up_dil": g_up_dil, "w_up_sb": g_up_sb, "w_out": g_out,
           "w_mlp_in": g_mlp_in, "w_mlp_out": g_mlp_out}
    small = (dg_mix, dbg, dg_mlp, dg_final, loss_part)
    return grad_x, big, small


def kernel(x, norm_mix_g, w_in, b_gate, w_up_dil, w_up_sb, w_out, norm_mlp_g, w_mlp_in, w_mlp_out, norm_final_g, loss_target, m_norm_mix_g, m_w_in, m_b_gate, m_w_up_dil, m_w_up_sb, m_w_out, m_norm_mlp_g, m_w_mlp_in, m_w_mlp_out, m_norm_final_g, v_norm_mix_g, v_w_in, v_b_gate, v_w_up_dil, v_w_up_sb, v_w_out, v_norm_mlp_g, v_w_mlp_in, v_w_mlp_out, v_norm_final_g):
    shards = {"w_in": w_in[0], "w_up_dil": w_up_dil[0], "w_up_sb": w_up_sb[0], "w_out": w_out[0],
              "w_mlp_in": w_mlp_in[0], "w_mlp_out": w_mlp_out[0]}
    moments_m = {"w_in": m_w_in[0], "w_up_dil": m_w_up_dil[0], "w_up_sb": m_w_up_sb[0], "w_out": m_w_out[0],
                 "w_mlp_in": m_w_mlp_in[0], "w_mlp_out": m_w_mlp_out[0]}
    moments_v = {"w_in": v_w_in[0], "w_up_dil": v_w_up_dil[0], "w_up_sb": v_w_up_sb[0], "w_out": v_w_out[0],
                 "w_mlp_in": v_w_mlp_in[0], "w_mlp_out": v_w_mlp_out[0]}

    pack = _pack_shards({n: s.astype(BF16) for n, s in shards.items()})
    full = _unpack_full(_all_gather_weights(pack))

    grad_x, big, small = _local_step(x[0], loss_target[0], full, norm_mix_g, b_gate, norm_mlp_g, norm_final_g)

    core = lax.axis_index("c").astype(jnp.int32).reshape(1)
    gpack = _pack_full_grads(big)
    chip_sum = _add_halves(gpack, _swap_halves(gpack), core)
    chip = (2 * lax.axis_index("x") + lax.axis_index("y")).astype(jnp.int32).reshape(1)
    reduced = _join_halves(_sum_chips(_exchange_chunks(chip_sum), chip_sum, chip))
    g_shard = _unpack_shard(reduced)

    dg_mix, dbg, dg_mlp, dg_final, loss_part = small
    loss_row = jnp.sum(loss_part, axis=0, keepdims=True)
    small_pack = jnp.concatenate(
        [jnp.sum(dg_mix, axis=0, keepdims=True), jnp.sum(dbg, axis=0, keepdims=True),
         jnp.sum(dg_mlp, axis=0, keepdims=True), jnp.sum(dg_final, axis=0, keepdims=True), loss_row], axis=1)
    n_small = small_pack.shape[1]
    small_sum = _all_reduce_small(small_pack.reshape(n_small // 128, 128)).reshape(1, n_small)
    g_norm_mix = small_sum[:, :D_MODEL]
    g_b_gate = small_sum[:, D_MODEL:3 * D_MODEL]
    g_norm_mlp = small_sum[:, 3 * D_MODEL:4 * D_MODEL]
    g_norm_final = small_sum[:, 4 * D_MODEL:5 * D_MODEL]
    loss = jnp.sum(small_sum[:, 5 * D_MODEL:])

    names = ["norm_mix_g", "w_in", "b_gate", "w_up_dil", "w_up_sb", "w_out", "norm_mlp_g", "w_mlp_in", "w_mlp_out",
             "norm_final_g"]
    grads = dict(g_shard)
    grads.update(norm_mix_g=g_norm_mix, b_gate=g_b_gate, norm_mlp_g=g_norm_mlp, norm_final_g=g_norm_final)
    weights = dict(shards)
    weights.update(norm_mix_g=norm_mix_g, b_gate=b_gate, norm_mlp_g=norm_mlp_g, norm_final_g=norm_final_g.reshape(1, D_MODEL))
    ms = dict(moments_m)
    ms.update(norm_mix_g=m_norm_mix_g, b_gate=m_b_gate, norm_mlp_g=m_norm_mlp_g, norm_final_g=m_norm_final_g.reshape(1, D_MODEL))
    vs = dict(moments_v)
    vs.update(norm_mix_g=v_norm_mix_g, b_gate=v_b_gate, norm_mlp_g=v_norm_mlp_g, norm_final_g=v_norm_final_g.reshape(1, D_MODEL))

    out_shapes = {"norm_mix_g": norm_mix_g.shape, "w_in": w_in.shape, "b_gate": b_gate.shape, "w_up_dil": w_up_dil.shape,
                  "w_up_sb": w_up_sb.shape, "w_out": w_out.shape, "norm_mlp_g": norm_mlp_g.shape,
                  "w_mlp_in": w_mlp_in.shape, "w_mlp_out": w_mlp_out.shape, "norm_final_g": norm_final_g.shape}
    g_out, d_out, m_out, v_out = [], [], [], []
    for n in names:
        d, nm, nv = _adamw(grads[n], weights[n], ms[n], vs[n], "adamw_" + n)
        shape = out_shapes[n]
        g_out.append(grads[n].reshape(shape))
        d_out.append(d.reshape(shape))
        m_out.append(nm.reshape(shape))
        v_out.append(nv.reshape(shape))
    return (loss, grad_x.reshape(x.shape), *g_out, *d_out, *m_out, *v_out)
```

```python
import functools
import math

import jax
import jax.numpy as jnp
import numpy as np
from jax import lax
from jax.experimental import pallas as pl
from jax.experimental.pallas import tpu as pltpu

F32 = jnp.float32
BF16 = jnp.bfloat16
MESH = pl.DeviceIdType.MESH

D_MODEL = 1024
HEAD_DIM = 64
DIL_GROUPS = ((128, 1), (512, 4), (2048, 16))
DIL_HEADS = 4
DIL_W = 256
N_DIL_HEADS = 12
SB_HEADS = 8
SB_W = SB_HEADS * HEAD_DIM
QKV_W = 3 * 3 * DIL_W + 3 * SB_W
GATE_W = 2 * D_MODEL
IN_COLS = QKV_W + GATE_W
D_FF = 4 * D_MODEL
BLOCK = 128
RMS_EPS = 1e-6
NEG_INF = -1e30
N_CHIPS = 4
N_DEV = 8

ADAM_LR = 0.001
ADAM_B1 = 0.9
ADAM_B2 = 0.999
ADAM_EPS = 1e-08
ADAM_WD = 0.01
ADAM_STEP = 10

VMEM_LIMIT = 56 * 1024 * 1024

SB_BQ = 256
SB_BK = 256


def _cparams(sem=None):
    if sem is None:
        return pltpu.CompilerParams(vmem_limit_bytes=VMEM_LIMIT)
    return pltpu.CompilerParams(dimension_semantics=sem, vmem_limit_bytes=VMEM_LIMIT)


def _dot(a, b, dims):
    return lax.dot_general(a, b, (dims, ((), ())), preferred_element_type=F32)


def _dot_nn(a, b):
    return _dot(a, b, ((1,), (0,)))


def _dot_nt(a, b):
    return _dot(a, b, ((1,), (1,)))


def _dot_tn(a, b):
    return _dot(a, b, ((0,), (0,)))


def _dot_f32_by_01(x, m01, pieces=3):
    hi = x.astype(BF16)
    r1 = x - hi.astype(F32)
    mid = r1.astype(BF16)
    if pieces == 2:
        return _dot_nn(hi, m01) + _dot_nn(mid, m01)
    lo = (r1 - mid.astype(F32)).astype(BF16)
    return _dot_nn(hi, m01) + _dot_nn(mid, m01) + _dot_nn(lo, m01)


def _matmul(a, b, *, mode, out_dtypes, name, tm=1024, tn=1024, tk=1024, extras=(), epilogue=None):
    if mode == "nn":
        (m, k), (k2, n) = a.shape, b.shape
    elif mode == "nt":
        (m, k), (n, k2) = a.shape, b.shape
    else:
        (k, m), (k2, n) = a.shape, b.shape
    assert k == k2, (a.shape, b.shape, mode)
    tm, tn, tk = min(tm, m), min(tn, n), min(tk, k)
    assert m % tm == 0 and n % tn == 0 and k % tk == 0, (m, n, k, tm, tn, tk)
    nk = k // tk
    n_out = len(out_dtypes)
    n_ex = len(extras)

    if mode == "nn":
        a_spec = pl.BlockSpec((tm, tk), lambda i, j, kk: (i, kk))
        b_spec = pl.BlockSpec((tk, tn), lambda i, j, kk: (kk, j))
        dot = _dot_nn
    elif mode == "nt":
        a_spec = pl.BlockSpec((tm, tk), lambda i, j, kk: (i, kk))
        b_spec = pl.BlockSpec((tn, tk), lambda i, j, kk: (j, kk))
        dot = _dot_nt
    else:
        a_spec = pl.BlockSpec((tk, tm), lambda i, j, kk: (kk, i))
        b_spec = pl.BlockSpec((tk, tn), lambda i, j, kk: (kk, j))
        dot = _dot_tn
    mn_spec = pl.BlockSpec((tm, tn), lambda i, j, kk: (i, j))

    def body(*refs):
        a_ref, b_ref = refs[0], refs[1]
        ex_refs = refs[2:2 + n_ex]
        out_refs = refs[2 + n_ex:2 + n_ex + n_out]
        acc_ref = refs[2 + n_ex + n_out] if nk > 1 else None
        part = dot(a_ref[...].astype(BF16), b_ref[...].astype(BF16))

        def finish(acc):
            if epilogue is None:
                outs = (acc,)
            else:
                outs = epilogue(acc, *[r[...] for r in ex_refs])
            for o_ref, o in zip(out_refs, outs):
                o_ref[...] = o.astype(o_ref.dtype)

        if nk == 1:
            finish(part)
        else:
            kk = pl.program_id(2)

            @pl.when(kk == 0)
            def _():
                acc_ref[...] = part

            @pl.when(kk > 0)
            def _():
                acc_ref[...] += part

            @pl.when(kk == nk - 1)
            def _():
                finish(acc_ref[...])

    outs = pl.pallas_call(
        body,
        name=name,
        grid=(m // tm, n // tn, nk),
        in_specs=[a_spec, b_spec] + [mn_spec] * n_ex,
        out_specs=[mn_spec] * n_out,
        out_shape=[jax.ShapeDtypeStruct((m, n), dt) for dt in out_dtypes],
        scratch_shapes=[pltpu.VMEM((tm, tn), F32)] if nk > 1 else [],
        compiler_params=_cparams(("parallel", "parallel", "arbitrary")),
    )(a, b, *extras)
    return outs


ROW_TILE = 512


def _rms_fwd(x, g, name):
    s, d = x.shape

    def body(x_ref, g_ref, h_ref):
        xv = x_ref[...]
        r = lax.rsqrt(jnp.mean(xv * xv, axis=-1, keepdims=True) + RMS_EPS)
        h_ref[...] = (xv * r * g_ref[...]).astype(BF16)

    return pl.pallas_call(
        body,
        name=name,
        grid=(s // ROW_TILE,),
        in_specs=[pl.BlockSpec((ROW_TILE, d), lambda i: (i, 0)), pl.BlockSpec((1, d), lambda i: (0, 0))],
        out_specs=pl.BlockSpec((ROW_TILE, d), lambda i: (i, 0)),
        out_shape=jax.ShapeDtypeStruct((s, d), BF16),
        compiler_params=_cparams(("parallel",)),
    )(x, g)


def _rms_bwd(dh, x, g, dres, name):
    s, d = x.shape

    def body(dh_ref, x_ref, g_ref, dres_ref, dx_ref, dg_ref):
        i = pl.program_id(0)
        xv = x_ref[...]
        r = lax.rsqrt(jnp.mean(xv * xv, axis=-1, keepdims=True) + RMS_EPS)
        xh = xv * r
        dhv = dh_ref[...]
        dxh = dhv * g_ref[...]
        dx = r * (dxh - xh * jnp.mean(dxh * xh, axis=-1, keepdims=True))
        dx_ref[...] = dres_ref[...] + dx
        part = jnp.sum((dhv * xh).reshape(ROW_TILE // 8, 8, d), axis=0)

        @pl.when(i == 0)
        def _():
            dg_ref[...] = part

        @pl.when(i > 0)
        def _():
            dg_ref[...] += part

    row = pl.BlockSpec((ROW_TILE, d), lambda i: (i, 0))
    return pl.pallas_call(
        body,
        name=name,
        grid=(s // ROW_TILE,),
        in_specs=[row, row, pl.BlockSpec((1, d), lambda i: (0, 0)), row],
        out_specs=[row, pl.BlockSpec((8, d), lambda i: (0, 0))],
        out_shape=[jax.ShapeDtypeStruct((s, d), F32), jax.ShapeDtypeStruct((8, d), F32)],
        compiler_params=_cparams(("arbitrary",)),
    )(dh, x, g, dres)


def _loss_head(x2, g, target):
    s, d = x2.shape

    def body(x_ref, g_ref, t_ref, dx_ref, dg_ref, loss_ref):
        i = pl.program_id(0)
        xv = x_ref[...]
        r = lax.rsqrt(jnp.mean(xv * xv, axis=-1, keepdims=True) + RMS_EPS)
        xh = xv * r
        gv = g_ref[...]
        err = xh * gv - t_ref[...]
        dy = err * (1.0 / d)
        dxh = dy * gv
        dx_ref[...] = r * (dxh - xh * jnp.mean(dxh * xh, axis=-1, keepdims=True))
        part_g = jnp.sum((dy * xh).reshape(ROW_TILE // 8, 8, d), axis=0)
        part_l = (0.5 / d) * jnp.sum((err * err).reshape(ROW_TILE // 8, 8, d), axis=0)

        @pl.when(i == 0)
        def _():
            dg_ref[...] = part_g
            loss_ref[...] = part_l

        @pl.when(i > 0)
        def _():
            dg_ref[...] += part_g
            loss_ref[...] += part_l

    row = pl.BlockSpec((ROW_TILE, d), lambda i: (i, 0))
    acc = pl.BlockSpec((8, d), lambda i: (0, 0))
    return pl.pallas_call(
        body,
        name="loss_head",
        grid=(s // ROW_TILE,),
        in_specs=[row, pl.BlockSpec((1, d), lambda i: (0, 0)), row],
        out_specs=[row, acc, acc],
        out_shape=[jax.ShapeDtypeStruct((s, d), F32), jax.ShapeDtypeStruct((8, d), F32),
                   jax.ShapeDtypeStruct((8, d), F32)],
        compiler_params=_cparams(("arbitrary",)),
    )(x2, g, target)


def _alibi_slopes():
    return np.exp2(np.float32(-8.0) * np.arange(1, N_DIL_HEADS + 1, dtype=np.float32) / np.float32(N_DIL_HEADS))


def _head_lane_mask(h, rows):
    lane = lax.broadcasted_iota(jnp.int32, (rows, DIL_W), 1)
    return (lane >= h * HEAD_DIM) & (lane < (h + 1) * HEAD_DIM)


def _band_terms(dil, has_prev):
    qi = lax.broadcasted_iota(jnp.int32, (BLOCK, 2 * BLOCK), 0)
    kj = lax.broadcasted_iota(jnp.int32, (BLOCK, 2 * BLOCK), 1)
    steps = qi + BLOCK - kj
    valid = (steps >= 0) & (steps <= BLOCK) & ((kj >= BLOCK) | has_prev)
    return valid, steps.astype(F32) * float(dil)


def _dil_fwd(qkv_g, group):
    _, dil = DIL_GROUPS[group]
    s = qkv_g.shape[0]
    sub = s // dil
    nb = sub // BLOCK
    view = qkv_g.reshape(sub, dil * 3 * DIL_W)
    slopes = _alibi_slopes()[group * DIL_HEADS:(group + 1) * DIL_HEADS]

    def col(which):
        return lambda r, n: (n, r * 3 + which)

    def col_prev(which):
        return lambda r, n: (jnp.maximum(n - 1, 0), r * 3 + which)

    def body(q_ref, kc_ref, kp_ref, vc_ref, vp_ref, o_ref, lse_ref):
        n = pl.program_id(1)
        valid, dist = _band_terms(dil, n > 0)
        q = q_ref[...]
        k2 = jnp.concatenate([kp_ref[...], kc_ref[...]], axis=0)
        v2 = jnp.concatenate([vp_ref[...], vc_ref[...]], axis=0)
        masks = [_head_lane_mask(h, BLOCK) for h in range(DIL_HEADS)]
        logits = [_dot_nt(jnp.where(masks[h], q, jnp.zeros_like(q)), k2) for h in range(DIL_HEADS)]
        ps, lses = [], []
        for h in range(DIL_HEADS):
            lg = jnp.where(valid, logits[h] * 0.125 - float(slopes[h]) * dist, NEG_INF)
            mx = jnp.max(lg, axis=1, keepdims=True)
            lse = mx + jnp.log(jnp.sum(jnp.exp(lg - mx), axis=1, keepdims=True))
            ps.append(jnp.exp(lg - lse).astype(BF16))
            lses.append(lse)
        o_acc = jnp.zeros((BLOCK, DIL_W), F32)
        lse_acc = jnp.zeros((BLOCK, DIL_W), F32)
        for h in range(DIL_HEADS):
            o_acc = jnp.where(masks[h], _dot_nn(ps[h], v2), o_acc)
            lse_acc = jnp.where(masks[h], lses[h], lse_acc)
        o_ref[...] = o_acc
        lse_ref[...] = lse_acc

    blk = (BLOCK, DIL_W)
    o, lse = pl.pallas_call(
        body,
        name=f"dil_fwd_g{group}",
        grid=(dil, nb),
        in_specs=[pl.BlockSpec(blk, col(0)), pl.BlockSpec(blk, col(1)), pl.BlockSpec(blk, col_prev(1)),
                  pl.BlockSpec(blk, col(2)), pl.BlockSpec(blk, col_prev(2))],
        out_specs=[pl.BlockSpec(blk, lambda r, n: (n, r))] * 2,
        out_shape=[jax.ShapeDtypeStruct((sub, dil * DIL_W), F32)] * 2,
        compiler_params=_cparams(("parallel", "parallel")),
    )(view, view, view, view, view)
    return o.reshape(s, DIL_W), lse.reshape(s, DIL_W)


def _dil_bwd(qkv, do, lse, cterm, group):
    _, dil = DIL_GROUPS[group]
    s = qkv.shape[0]
    sub = s // dil
    nb = sub // BLOCK
    view = qkv.reshape(sub, dil * 3 * DIL_W)
    slopes = _alibi_slopes()[group * DIL_HEADS:(group + 1) * DIL_HEADS]
    do_v, lse_v, c_v = (t.reshape(sub, dil * DIL_W) for t in (do, lse, cterm))

    def col(which, shift):
        if shift == 0:
            return lambda r, n: (n, r * 3 + which)
        if shift < 0:
            return lambda r, n: (jnp.maximum(n - 1, 0), r * 3 + which)
        return lambda r, n: (jnp.minimum(n + 1, nb - 1), r * 3 + which)

    def own(shift):
        if shift == 0:
            return lambda r, n: (n, r)
        return lambda r, n: (jnp.minimum(n + 1, nb - 1), r)

    def body(q_ref, qn_ref, kc_ref, kp_ref, vc_ref, vp_ref, do_ref, don_ref, lse_ref, lsen_ref, c_ref, cn_ref,
             dq_ref, dk_ref, dv_ref):
        n = pl.program_id(1)
        valid, dist = _band_terms(dil, n > 0)
        valid_n = _band_terms(dil, True)[0][:, :BLOCK] & (n < nb - 1)
        dist_n = dist[:, :BLOCK]
        q, qn = q_ref[...], qn_ref[...]
        kc, vc = kc_ref[...], vc_ref[...]
        k2 = jnp.concatenate([kp_ref[...], kc], axis=0)
        v2 = jnp.concatenate([vp_ref[...], vc], axis=0)
        dov, donv = do_ref[...], don_ref[...]
        lsev, lsenv, cv, cnv = lse_ref[...], lsen_ref[...], c_ref[...], cn_ref[...]
        masks = [_head_lane_mask(h, BLOCK) for h in range(DIL_HEADS)]

        def head_col(t, hm):
            return jnp.max(jnp.where(hm, t, NEG_INF), axis=1, keepdims=True)

        qhs = [jnp.where(hm, q, jnp.zeros_like(q)) for hm in masks]
        qnhs = [jnp.where(hm, qn, jnp.zeros_like(qn)) for hm in masks]
        dohs = [jnp.where(hm, dov, 0.0).astype(BF16) for hm in masks]
        donhs = [jnp.where(hm, donv, 0.0).astype(BF16) for hm in masks]
        logit = [_dot_nt(qhs[h], k2) for h in range(DIL_HEADS)]
        dp = [_dot_nt(dohs[h], v2) for h in range(DIL_HEADS)]
        logit_n = [_dot_nt(qnhs[h], kc) for h in range(DIL_HEADS)]
        dp_n = [_dot_nt(donhs[h], vc) for h in range(DIL_HEADS)]
        p16, dlog, pn16, dlog_n = [], [], [], []
        for h in range(DIL_HEADS):
            hm, slope = masks[h], float(slopes[h])
            p = jnp.where(valid, jnp.exp(logit[h] * 0.125 - slope * dist - head_col(lsev, hm)), 0.0)
            dlog.append((p * (dp[h] + head_col(cv, hm)) * 0.125).astype(BF16))
            p16.append(p.astype(BF16))
            pn = jnp.where(valid_n, jnp.exp(logit_n[h] * 0.125 - slope * dist_n - head_col(lsenv, hm)), 0.0)
            dlog_n.append((pn * (dp_n[h] + head_col(cnv, hm)) * 0.125).astype(BF16))
            pn16.append(pn.astype(BF16))
        dq_acc = jnp.zeros((BLOCK, DIL_W), F32)
        dk_acc = jnp.zeros((BLOCK, DIL_W), F32)
        dv_acc = jnp.zeros((BLOCK, DIL_W), F32)
        for h in range(DIL_HEADS):
            dq_acc = jnp.where(masks[h], _dot_nn(dlog[h], k2), dq_acc)
            dk_acc += _dot_tn(dlog[h][:, BLOCK:], qhs[h]) + _dot_tn(dlog_n[h], qnhs[h])
            dv_acc += _dot_tn(p16[h][:, BLOCK:], dohs[h]) + _dot_tn(pn16[h], donhs[h])
        dq_ref[...] = dq_acc.astype(BF16)
        dk_ref[...] = dk_acc.astype(BF16)
        dv_ref[...] = dv_acc.astype(BF16)

    blk = (BLOCK, DIL_W)
    outs = pl.pallas_call(
        body,
        name=f"dil_bwd_g{group}",
        grid=(dil, nb),
        in_specs=[pl.BlockSpec(blk, col(0, 0)), pl.BlockSpec(blk, col(0, 1)),
                  pl.BlockSpec(blk, col(1, 0)), pl.BlockSpec(blk, col(1, -1)),
                  pl.BlockSpec(blk, col(2, 0)), pl.BlockSpec(blk, col(2, -1)),
                  pl.BlockSpec(blk, own(0)), pl.BlockSpec(blk, own(1)),
                  pl.BlockSpec(blk, own(0)), pl.BlockSpec(blk, own(1)),
                  pl.BlockSpec(blk, own(0)), pl.BlockSpec(blk, own(1))],
        out_specs=[pl.BlockSpec(blk, lambda r, n: (n, r))] * 3,
        out_shape=[jax.ShapeDtypeStruct((sub, dil * DIL_W), BF16)] * 3,
        compiler_params=_cparams(("parallel", "parallel")),
    )(view, view, view, view, view, view, do_v, do_v, lse_v, lse_v, c_v, c_v)
    return tuple(t.reshape(s, DIL_W) for t in outs)


SB_PAIRS = SB_HEADS // 2
SB_COL0 = 0
LOG2E = 1.4426950408889634


SB_EXP_CLAMP = 64.0


def _sb_softplus2(zs):
    t = 1.0 + jnp.exp2(jnp.minimum(zs, SB_EXP_CLAMP))
    return jnp.maximum(jnp.log(t) * LOG2E, zs)


def _sb_consts(nkb):
    row = lax.broadcasted_iota(jnp.int32, (SB_BQ, SB_BK), 0)
    colk = lax.broadcasted_iota(jnp.int32, (SB_BQ, SB_BK), 1)
    rr = lax.broadcasted_iota(jnp.int32, (SB_BK, SB_BK), 0)
    cc = lax.broadcasted_iota(jnp.int32, (SB_BK, SB_BK), 1)
    lane = lax.broadcasted_iota(jnp.int32, (SB_BQ, 128), 1)
    assert 2 * nkb <= 128
    return colk < row, rr, cc, lane < HEAD_DIM, lane


def _split_heads(t):
    first = lax.broadcasted_iota(jnp.int32, t.shape, 1) < HEAD_DIM
    zero = jnp.zeros_like(t)
    return jnp.where(first, t, zero), jnp.where(first, zero, t)


def _sb_fwd(qkv):
    s = qkv.shape[0]
    nq, nkb = s // SB_BQ, s // SB_BK
    zscale = LOG2E / math.sqrt(HEAD_DIM)

    def body(q_ref, k_ref, v_ref, o_ref, carry_ref, zs_scr, a_scr, acc_scr, cl_scr):
        i = pl.program_id(1)
        causal, rr, cc, _, lane = _sb_consts(nkb)
        later = (rr > cc).astype(BF16)
        qh = _split_heads(q_ref[...])

        def rows(j):
            return pl.ds(pl.multiple_of(j * SB_BK, SB_BK), SB_BK)

        def scores_to(slot, j):
            kb = k_ref[rows(j), :]
            for hh in range(2):
                zs_scr[slot, hh] = _dot_nt(qh[hh], kb) * zscale

        def weights(slot, j, masked):
            xs, sums, sufs = [], [], []
            for hh in range(2):
                zs = zs_scr[slot, hh]
                sp = _sb_softplus2(zs)
                if masked:
                    sp = jnp.where(causal, sp, 0.0)
                xs.append(zs - sp)
                sums.append(jnp.sum(sp, axis=1, keepdims=True))
                sufs.append(_dot_f32_by_01(sp, later, 2))
            for hh in range(2):
                cl = cl_scr[hh]
                a = jnp.exp2(xs[hh] - (sufs[hh] + jnp.concatenate([cl, cl], axis=1)))
                if masked:
                    a = jnp.where(causal, a, 0.0)
                a_scr[slot, :, hh * SB_BK:(hh + 1) * SB_BK] = a.astype(BF16)
            for hh in range(2):
                cl = cl_scr[hh]
                carry_ref[0] = jnp.where(lane == j + hh * nkb, cl, carry_ref[0])
                cl_scr[hh] = cl + sums[hh]

        def add_av(slot, j):
            v0, v1 = _split_heads(v_ref[rows(j), :])
            acc_scr[...] += _dot_nn(a_scr[slot], jnp.concatenate([v0, v1], axis=0))

        acc_scr[...] = jnp.zeros_like(acc_scr)
        cl_scr[...] = jnp.zeros_like(cl_scr)
        carry_ref[...] = jnp.zeros_like(carry_ref)
        scores_to(0, i)
        scores_to(1, jnp.maximum(i - 1, 0))
        weights(0, i, True)

        def step(j, prev, cur):
            scores_to(prev, jnp.maximum(j - 1, 0))
            add_av(prev, j + 1)
            weights(cur, j, False)

        def two_steps(u, _):
            j = i - 1 - 2 * u
            step(j, 0, 1)
            step(j - 1, 1, 0)
            return 0

        lax.fori_loop(0, i // 2, two_steps, 0)

        @pl.when(i % 2 == 1)
        def _():
            step(0, 0, 1)
            add_av(1, 0)

        @pl.when(i % 2 == 0)
        def _():
            add_av(0, 0)

        o_ref[...] = acc_scr[...]

    def full(which):
        return pl.BlockSpec((s, 128), lambda p, i: (0, SB_COL0 + 4 * which + p))

    return pl.pallas_call(
        body,
        name="sb_fwd",
        grid=(SB_PAIRS, nq),
        in_specs=[pl.BlockSpec((SB_BQ, 128), lambda p, i: (i, SB_COL0 + p)), full(1), full(2)],
        out_specs=[pl.BlockSpec((SB_BQ, 128), lambda p, i: (i, p)),
                   pl.BlockSpec((1, SB_BQ, 128), lambda p, i: (p, i, 0))],
        out_shape=[jax.ShapeDtypeStruct((s, SB_W), F32), jax.ShapeDtypeStruct((SB_PAIRS, s, 128), F32)],
        scratch_shapes=[pltpu.VMEM((2, 2, SB_BQ, SB_BK), F32), pltpu.VMEM((2, SB_BQ, 2 * SB_BK), BF16),
                        pltpu.VMEM((SB_BQ, 128), F32), pltpu.VMEM((2, SB_BQ, 128), F32)],
        compiler_params=_cparams(("parallel", "parallel")),
    )(qkv, qkv, qkv)


def _sb_bwd(qkv, do, carries, chip_sums):
    s = qkv.shape[0]
    nq, nkb = s // SB_BQ, s // SB_BK
    scale = 1.0 / math.sqrt(HEAD_DIM)
    zscale = LOG2E * scale

    def body(q_ref, k_ref, v_ref, do_ref, carry_ref, sums_ref, dq_ref, dk_ref, dv_ref, got_ref,
             zs_scr, da_scr, dz_scr, a_scr, cg_scr, send_sems, recv_sems):
        i = pl.program_id(1)
        first_step = (pl.program_id(0) == 0) & (i == 0)
        last_step = (pl.program_id(0) == SB_PAIRS - 1) & (i == nq - 1)

        @pl.when(first_step)
        def _():
            _exchange_start(sums_ref, got_ref, send_sems, recv_sems)

        @pl.when(i == 0)
        def _():
            dk_ref[...] = jnp.zeros_like(dk_ref)
            dv_ref[...] = jnp.zeros_like(dv_ref)

        causal, rr, cc, first, lane = _sb_consts(nkb)
        later = (rr > cc).astype(BF16)
        earlier = (rr < cc).astype(BF16)
        q2 = q_ref[...]
        qh = _split_heads(q2)
        do2 = do_ref[...].astype(BF16)
        doh = _split_heads(do2)
        ctile = carry_ref[0]

        def rows(j):
            return pl.ds(pl.multiple_of(j * SB_BK, SB_BK), SB_BK)

        def products_to(slot, j):
            kb, vb = k_ref[rows(j), :], v_ref[rows(j), :]
            for hh in range(2):
                zs_scr[slot, hh] = _dot_nt(qh[hh], kb) * zscale
                da_scr[slot, hh] = _dot_nt(doh[hh], vb)

        def by_head(t):
            return jnp.where(first, t[:SB_BK], t[SB_BK:])

        def apply(slot, j):
            k0, k1 = _split_heads(k_ref[rows(j), :])
            dq_ref[...] += _dot_nn(dz_scr[slot], jnp.concatenate([k0, k1], axis=0)) * scale
            dk_ref[rows(j), :] += by_head(_dot_tn(dz_scr[slot], q2)) * scale
            dv_ref[rows(j), :] += by_head(_dot_tn(a_scr[slot], do2))

        def grads(slot, j, masked):
            xs, sigs, sufs = [], [], []
            for hh in range(2):
                zs = zs_scr[slot, hh]
                sp = _sb_softplus2(zs)
                x = zs - sp
                xs.append(x)
                sigs.append(jnp.exp2(x))
                if masked:
                    sp = jnp.where(causal, sp, 0.0)
                sufs.append(_dot_f32_by_01(sp, later, 2))
            gs, gpres = [], []
            for hh in range(2):
                cl = jnp.sum(jnp.where(lane == j + hh * nkb, ctile, 0.0), axis=1, keepdims=True)
                a = jnp.exp2(xs[hh] - (sufs[hh] + cl))
                if masked:
                    a = jnp.where(causal, a, 0.0)
                g = a * da_scr[slot, hh]
                a_scr[slot, :, hh * SB_BK:(hh + 1) * SB_BK] = a.astype(BF16)
                gs.append(g)
                gpres.append(_dot_f32_by_01(g, earlier, 2))
            for hh in range(2):
                cg = cg_scr[hh]
                dz = gs[hh] - (gs[hh] + (gpres[hh] + jnp.concatenate([cg, cg], axis=1))) * sigs[hh]
                if masked:
                    dz = jnp.where(causal, dz, 0.0)
                dz_scr[slot, :, hh * SB_BK:(hh + 1) * SB_BK] = dz.astype(BF16)
                cg_scr[hh] = cg + jnp.sum(gs[hh], axis=1, keepdims=True)

        dq_ref[...] = jnp.zeros_like(dq_ref)
        cg_scr[...] = jnp.zeros_like(cg_scr)
        dz_scr[1] = jnp.zeros((SB_BQ, 2 * SB_BK), BF16)
        a_scr[1] = jnp.zeros((SB_BQ, 2 * SB_BK), BF16)
        products_to(0, 0)

        def step(j, cur, nxt):
            products_to(nxt, j + 1)
            apply(nxt, jnp.maximum(j - 1, 0))
            grads(cur, j, False)

        def two_steps(u, _):
            step(2 * u, 0, 1)
            step(2 * u + 1, 1, 0)
            return 0

        lax.fori_loop(0, i // 2, two_steps, 0)

        def last(cur, nxt):
            apply(nxt, jnp.maximum(i - 1, 0))
            grads(cur, i, True)
            apply(cur, i)

        @pl.when(i % 2 == 1)
        def _():
            step(i - 1, 0, 1)
            last(1, 0)

        @pl.when(i % 2 == 0)
        def _():
            last(0, 1)

        @pl.when(last_step)
        def _():
            _exchange_wait(sums_ref, got_ref, send_sems, recv_sems)

    def full(which):
        return pl.BlockSpec((s, 128), lambda p, i: (0, SB_COL0 + 4 * which + p))

    qblk = pl.BlockSpec((SB_BQ, 128), lambda p, i: (i, p))
    acc = pl.BlockSpec((s, 128), lambda p, i: (0, p))
    return pl.pallas_call(
        body,
        name="sb_bwd",
        grid=(SB_PAIRS, nq),
        in_specs=[pl.BlockSpec((SB_BQ, 128), lambda p, i: (i, SB_COL0 + p)), full(1), full(2), qblk,
                  pl.BlockSpec((1, SB_BQ, 128), lambda p, i: (p, i, 0)), ANY],
        out_specs=[qblk, acc, acc, ANY],
        out_shape=[jax.ShapeDtypeStruct((s, SB_W), F32)] * 3 + [jax.ShapeDtypeStruct(chip_sums.shape, chip_sums.dtype)],
        scratch_shapes=[pltpu.VMEM((2, 2, SB_BQ, SB_BK), F32), pltpu.VMEM((2, 2, SB_BQ, SB_BK), F32),
                        pltpu.VMEM((2, SB_BQ, 2 * SB_BK), BF16), pltpu.VMEM((2, SB_BQ, 2 * SB_BK), BF16),
                        pltpu.VMEM((2, SB_BQ, 128), F32),
                        pltpu.SemaphoreType.DMA((3,)), pltpu.SemaphoreType.DMA((3,))],
        compiler_params=_cparams(("arbitrary", "arbitrary")),
    )(qkv, qkv, qkv, do, carries, chip_sums)


MERGE_TILE = 256


def _group_mix(lses):
    mx = jnp.maximum(jnp.maximum(lses[0], lses[1]), lses[2])
    es = [jnp.exp(t - mx) for t in lses]
    den = es[0] + es[1] + es[2]
    return [e / den for e in es]


def _merge_fwd(o_groups, lse_groups, o_sb, gl, b_gate, w_up_dil, w_up_sb):
    s = gl.shape[0]
    t = MERGE_TILE

    def body(o0, o1, o2, l0, l1, l2, ob_ref, gl_ref, bg_ref, wd_ref, ws_ref, merged_ref, oa_ref):
        w = _group_mix([l0[...], l1[...], l2[...]])
        oa = (w[0] * o0[...] + w[1] * o1[...] + w[2] * o2[...]).astype(BF16)
        ua = _dot_nn(oa, wd_ref[...])
        ub = _dot_nn(ob_ref[...].astype(BF16), ws_ref[...])
        gate = jax.nn.sigmoid(gl_ref[...] + bg_ref[...])
        merged_ref[...] = (gate[:, :D_MODEL] * ua + gate[:, D_MODEL:] * ub).astype(BF16)
        oa_ref[...] = oa

    dil = pl.BlockSpec((t, DIL_W), lambda i: (i, 0))
    const = lambda shape: pl.BlockSpec(shape, lambda i: (0, 0))
    return pl.pallas_call(
        body,
        name="merge_fwd",
        grid=(s // t,),
        in_specs=[dil] * 6 + [pl.BlockSpec((t, SB_W), lambda i: (i, 0)), pl.BlockSpec((t, GATE_W), lambda i: (i, 0)),
                              const((1, GATE_W)), const((DIL_W, D_MODEL)), const((SB_W, D_MODEL))],
        out_specs=[pl.BlockSpec((t, D_MODEL), lambda i: (i, 0)), dil],
        out_shape=[jax.ShapeDtypeStruct((s, D_MODEL), BF16), jax.ShapeDtypeStruct((s, DIL_W), BF16)],
        compiler_params=_cparams(("parallel",)),
    )(*o_groups, *lse_groups, o_sb, gl, b_gate, w_up_dil, w_up_sb)


def _merge_bwd(dmerged, o_groups, lse_groups, o_sb, gl, b_gate, w_up_dil, w_up_sb, swap):
    s = gl.shape[0]
    t = MERGE_TILE
    n_chunks, r_swap, w_swap = swap.shape
    swap = swap.reshape(n_chunks, 2, r_swap // 2, w_swap)

    def body(dm_ref, o0, o1, o2, l0, l1, l2, ob_ref, gl_ref, bg_ref, wd_ref, ws_ref, swap_ref,
             dua_ref, dub_ref, dgl_ref, dbg_ref, dosb_ref, d0, d1, d2, c0, c1, c2, got_ref, send_sem, recv_sem):
        i = pl.program_id(0)

        @pl.when(i == 0)
        def _():
            _swap_copy(swap_ref, got_ref, send_sem, recv_sem).start()

        @pl.when(i == pl.num_programs(0) - 1)
        def _():
            _swap_copy(swap_ref, got_ref, send_sem, recv_sem).wait()

        og = [o0[...], o1[...], o2[...]]
        w = _group_mix([l0[...], l1[...], l2[...]])
        oa = (w[0] * og[0] + w[1] * og[1] + w[2] * og[2]).astype(BF16)
        ua = _dot_nn(oa, wd_ref[...])
        ub = _dot_nn(ob_ref[...].astype(BF16), ws_ref[...])
        gate = jax.nn.sigmoid(gl_ref[...] + bg_ref[...])
        ga, gb = gate[:, :D_MODEL], gate[:, D_MODEL:]
        dm = dm_ref[...]
        dua = (dm * ga).astype(BF16)
        dub = (dm * gb).astype(BF16)
        dua_ref[...] = dua
        dub_ref[...] = dub
        dgl_a = dm * ua * ga * (1.0 - ga)
        dgl_b = dm * ub * gb * (1.0 - gb)
        dgl_ref[:, :D_MODEL] = dgl_a.astype(BF16)
        dgl_ref[:, D_MODEL:] = dgl_b.astype(BF16)
        part = jnp.concatenate([jnp.sum(dgl_a.reshape(t // 8, 8, D_MODEL), axis=0),
                                jnp.sum(dgl_b.reshape(t // 8, 8, D_MODEL), axis=0)], axis=1)

        @pl.when(i == 0)
        def _():
            dbg_ref[...] = part

        @pl.when(i > 0)
        def _():
            dbg_ref[...] += part

        dosb_ref[...] = _dot_nt(dub, ws_ref[...])
        doa = _dot_nt(dua, wd_ref[...])
        rr = lax.broadcasted_iota(jnp.int32, (DIL_W, DIL_W), 0) // HEAD_DIM
        cc = lax.broadcasted_iota(jnp.int32, (DIL_W, DIL_W), 1) // HEAD_DIM
        same_head = (rr == cc).astype(BF16)
        dw = [_dot_f32_by_01(doa * og[g], same_head) for g in range(3)]
        mean_dw = w[0] * dw[0] + w[1] * dw[1] + w[2] * dw[2]
        for g, (d_ref, c_ref) in enumerate(((d0, c0), (d1, c1), (d2, c2))):
            d_ref[...] = w[g] * doa
            c_ref[...] = -w[g] * mean_dw

    dil = pl.BlockSpec((t, DIL_W), lambda i: (i, 0))
    wide = pl.BlockSpec((t, D_MODEL), lambda i: (i, 0))
    gate2 = pl.BlockSpec((t, GATE_W), lambda i: (i, 0))
    sbw = pl.BlockSpec((t, SB_W), lambda i: (i, 0))
    const = lambda shape: pl.BlockSpec(shape, lambda i: (0, 0))
    return pl.pallas_call(
        body,
        name="merge_bwd",
        grid=(s // t,),
        in_specs=[wide] + [dil] * 6 + [sbw, gate2, const((1, GATE_W)), const((DIL_W, D_MODEL)), const((SB_W, D_MODEL)),
                                       ANY],
        out_specs=[wide, wide, gate2, const((8, GATE_W)), sbw] + [dil] * 6 + [ANY],
        out_shape=[jax.ShapeDtypeStruct((s, D_MODEL), BF16), jax.ShapeDtypeStruct((s, D_MODEL), BF16),
                   jax.ShapeDtypeStruct((s, GATE_W), BF16), jax.ShapeDtypeStruct((8, GATE_W), F32),
                   jax.ShapeDtypeStruct((s, SB_W), F32)] + [jax.ShapeDtypeStruct((s, DIL_W), F32)] * 6
        + [jax.ShapeDtypeStruct((n_chunks, r_swap // 2, w_swap), swap.dtype)],
        scratch_shapes=[pltpu.SemaphoreType.DMA, pltpu.SemaphoreType.DMA],
        compiler_params=_cparams(("arbitrary",)),
    )(dmerged, *o_groups, *lse_groups, o_sb, gl, b_gate, w_up_dil, w_up_sb, swap)


ANY = pl.BlockSpec(memory_space=pl.ANY)


def _place():
    x, y, c = lax.axis_index("x"), lax.axis_index("y"), lax.axis_index("c")
    other_chips = [(1 - x, y), (x, 1 - y), (1 - x, 1 - y)]
    return x, y, c, other_chips


def _all_gather_weights(pack):
    r, wd = pack.shape
    rh = r // 2
    pack = pack.reshape(2, rh, wd)

    def body(p_ref, out_ref, send_sems, recv_sems):
        x, y, c, chips = _place()
        me, sibling = 2 * x + y, (x, y, 1 - c)

        def half(chip_idx, core):
            return out_ref.at[chip_idx, core]

        def copy(k, chip_idx, core, to, src=None):
            return pltpu.make_async_remote_copy(
                src_ref=half(chip_idx, core) if src is None else src, dst_ref=half(chip_idx, core),
                send_sem=send_sems.at[k], recv_sem=recv_sems.at[k], device_id=to, device_id_type=MESH)

        first = [copy(j, me, c, (*chip, c), src=p_ref.at[c]) for j, chip in enumerate(chips)]
        for cp in first:
            cp.start()
        passed = [copy(3 + j, 2 * chip[0] + chip[1], c, sibling) for j, chip in enumerate(chips)]
        for j, chip in enumerate(chips):
            copy(j, 2 * chip[0] + chip[1], c, (x, y, c)).wait_recv()
            passed[j].start()
        for j, chip in enumerate(chips):
            copy(3 + j, 2 * chip[0] + chip[1], 1 - c, (x, y, c)).wait_recv()
        for cp in first + passed:
            cp.wait_send()

    others = pl.pallas_call(
        body,
        name="all_gather_weights",
        in_specs=[ANY],
        out_specs=ANY,
        out_shape=jax.ShapeDtypeStruct((N_CHIPS, 2, rh, wd), pack.dtype),
        scratch_shapes=[pltpu.SemaphoreType.DMA((6,)), pltpu.SemaphoreType.DMA((6,))],
    )(pack)
    me = 2 * lax.axis_index("x") + lax.axis_index("y")
    return lax.dynamic_update_slice(others, pack[None], (me, 0, 0, 0)).reshape(N_CHIPS, r, wd)


def _swap_copy(g_ref, out_ref, send_sem, recv_sem):
    x, y, c, _ = _place()
    return pltpu.make_async_remote_copy(
        src_ref=g_ref.at[:, 1 - c], dst_ref=out_ref,
        send_sem=send_sem, recv_sem=recv_sem, device_id=(x, y, 1 - c), device_id_type=MESH)


def _swap_halves(g):
    n, r, wd = g.shape
    rh = r // 2
    g = g.reshape(n, 2, rh, wd)

    def body(g_ref, out_ref, send_sem, recv_sem):
        cp = _swap_copy(g_ref, out_ref, send_sem, recv_sem)
        cp.start()
        cp.wait()

    return pl.pallas_call(
        body,
        name="grad_swap_halves",
        in_specs=[ANY],
        out_specs=ANY,
        out_shape=jax.ShapeDtypeStruct((n, rh, wd), g.dtype),
        scratch_shapes=[pltpu.SemaphoreType.DMA, pltpu.SemaphoreType.DMA],
    )(g)


def _add_halves(g, got, core):
    n, r, wd = g.shape
    rh = r // 2
    t = rh // 4
    nt = rh // t

    def body(c_ref, a_ref, b_ref, o_ref):
        o_ref[...] = (a_ref[0] + b_ref[...]).astype(BF16)

    grid_spec = pltpu.PrefetchScalarGridSpec(
        num_scalar_prefetch=1,
        grid=(n, nt),
        in_specs=[pl.BlockSpec((1, 1, t, wd), lambda s, i, c: (s, c[0], i, 0)),
                  pl.BlockSpec((1, t, wd), lambda s, i, c: (s, i, 0))],
        out_specs=pl.BlockSpec((1, t, wd), lambda s, i, c: (s, i, 0)),
    )
    return pl.pallas_call(
        body,
        name="grad_add_halves",
        grid_spec=grid_spec,
        out_shape=jax.ShapeDtypeStruct((n, rh, wd), BF16),
        compiler_params=_cparams(("parallel", "parallel")),
    )(core, g.reshape(n, 2, rh, wd), got)


def _exchange_copies(h_ref, out_ref, send_sems, recv_sems):
    x, y, c, chips = _place()
    me = 2 * x + y
    sends, arrivals = [], []
    for j, chip in enumerate(chips):
        them = 2 * chip[0] + chip[1]
        sends.append(pltpu.make_async_remote_copy(
            src_ref=h_ref.at[them], dst_ref=out_ref.at[me],
            send_sem=send_sems.at[j], recv_sem=recv_sems.at[j], device_id=(*chip, c), device_id_type=MESH))
        arrivals.append(pltpu.make_async_remote_copy(
            src_ref=h_ref.at[them], dst_ref=out_ref.at[them],
            send_sem=send_sems.at[j], recv_sem=recv_sems.at[j], device_id=(*chip, c), device_id_type=MESH))
    return sends, arrivals


def _exchange_start(h_ref, out_ref, send_sems, recv_sems):
    for cp in _exchange_copies(h_ref, out_ref, send_sems, recv_sems)[0]:
        cp.start()


def _exchange_wait(h_ref, out_ref, send_sems, recv_sems):
    sends, arrivals = _exchange_copies(h_ref, out_ref, send_sems, recv_sems)
    for cp in arrivals:
        cp.wait_recv()
    for cp in sends:
        cp.wait_send()


def _exchange_chunks(h):
    n, rh, wd = h.shape

    def body(h_ref, out_ref, send_sems, recv_sems):
        _exchange_start(h_ref, out_ref, send_sems, recv_sems)
        _exchange_wait(h_ref, out_ref, send_sems, recv_sems)

    return pl.pallas_call(
        body,
        name="grad_exchange_chunks",
        in_specs=[ANY],
        out_specs=ANY,
        out_shape=jax.ShapeDtypeStruct((n, rh, wd), h.dtype),
        scratch_shapes=[pltpu.SemaphoreType.DMA((3,)), pltpu.SemaphoreType.DMA((3,))],
    )(h)


def _sum_chips(b, h, chip):
    n, rh, wd = b.shape
    t = rh // 4

    def body(chip_ref, b_ref, own_ref, o_ref):
        own = own_ref[0]
        s0, s1, s2, s3 = (jnp.where(chip_ref[0] == k, own, b_ref[k]).astype(F32) for k in range(n))
        o_ref[...] = ((s0 + s1) + s2) + s3

    grid_spec = pltpu.PrefetchScalarGridSpec(
        num_scalar_prefetch=1,
        grid=(rh // t,),
        in_specs=[pl.BlockSpec((n, t, wd), lambda i, chip: (0, i, 0)),
                  pl.BlockSpec((1, t, wd), lambda i, chip: (chip[0], i, 0))],
        out_specs=pl.BlockSpec((t, wd), lambda i, chip: (i, 0)),
    )
    return pl.pallas_call(
        body,
        name="grad_sum_chips",
        grid_spec=grid_spec,
        out_shape=jax.ShapeDtypeStruct((rh, wd), F32),
        compiler_params=_cparams(("parallel",)),
    )(chip, b, h)


def _join_halves(tc):
    rh, wd = tc.shape

    def body(t_ref, out_ref, send_sem, recv_sem):
        x, y, c, _ = _place()
        cp = pltpu.make_async_remote_copy(
            src_ref=t_ref, dst_ref=out_ref.at[c],
            send_sem=send_sem, recv_sem=recv_sem, device_id=(x, y, 1 - c), device_id_type=MESH)
        cp.start()
        cp.wait()

    halves = pl.pallas_call(
        body,
        name="grad_join_halves",
        in_specs=[ANY],
        out_specs=ANY,
        out_shape=jax.ShapeDtypeStruct((2, rh, wd), tc.dtype),
        scratch_shapes=[pltpu.SemaphoreType.DMA, pltpu.SemaphoreType.DMA],
    )(tc)
    return lax.dynamic_update_slice(halves, tc[None], (lax.axis_index("c"), 0, 0)).reshape(2 * rh, wd)


def _all_reduce_small(pack):
    rows, lanes = pack.shape

    def body(p_ref, out_ref, buf, send_sems, recv_sems):
        x, y, c, _ = _place()
        me = 4 * x + 2 * y + c
        buf[me] = p_ref[...]
        sends = []
        for k in range(1, N_DEV):
            peer = (x ^ (k >> 2), y ^ ((k >> 1) & 1), c ^ (k & 1))
            sends.append(pltpu.make_async_remote_copy(
                src_ref=p_ref, dst_ref=buf.at[me], send_sem=send_sems.at[k - 1], recv_sem=recv_sems.at[k - 1],
                device_id=peer, device_id_type=MESH))
        for cp in sends:
            cp.start()
        for k in range(1, N_DEV):
            pltpu.make_async_remote_copy(
                src_ref=p_ref, dst_ref=buf.at[me ^ k], send_sem=send_sems.at[k - 1], recv_sem=recv_sems.at[k - 1],
                device_id=(x, y, c), device_id_type=MESH).wait_recv()
        for cp in sends:
            cp.wait_send()
        total = buf[0]
        for d in range(1, N_DEV):
            total = total + buf[d]
        out_ref[...] = total

    vm = pl.BlockSpec(memory_space=pltpu.VMEM)
    return pl.pallas_call(
        body,
        name="all_reduce_small",
        in_specs=[vm],
        out_specs=vm,
        out_shape=jax.ShapeDtypeStruct((rows, lanes), F32),
        scratch_shapes=[pltpu.VMEM((N_DEV, rows, lanes), F32), pltpu.SemaphoreType.DMA((N_DEV - 1,)),
                        pltpu.SemaphoreType.DMA((N_DEV - 1,))],
    )(pack)


def _adamw(g, w, m, v, name):
    rows, cols = g.shape
    t = rows
    for cand in (256, 128, 64, 32, 16, 8):
        if rows % cand == 0:
            t = cand
            break

    def body(g_ref, w_ref, m_ref, v_ref, d_ref, nm_ref, nv_ref):
        gv = g_ref[...]
        mv = ADAM_B1 * m_ref[...] + (1.0 - ADAM_B1) * gv
        vv = ADAM_B2 * v_ref[...] + (1.0 - ADAM_B2) * (gv * gv)
        m_hat = mv / (1.0 - ADAM_B1 ** ADAM_STEP)
        v_hat = vv / (1.0 - ADAM_B2 ** ADAM_STEP)
        d_ref[...] = -ADAM_LR * (m_hat / (jnp.sqrt(v_hat) + ADAM_EPS) + ADAM_WD * w_ref[...])
        nm_ref[...] = mv
        nv_ref[...] = vv

    blk = pl.BlockSpec((t, cols), lambda i: (i, 0))
    return pl.pallas_call(
        body,
        name=name,
        grid=(rows // t,),
        in_specs=[blk] * 4,
        out_specs=[blk] * 3,
        out_shape=[jax.ShapeDtypeStruct((rows, cols), F32)] * 3,
        compiler_params=_cparams(("parallel",)),
    )(g, w, m, v)


PACK_W = 1024
BIG = (("w_in", (D_MODEL, IN_COLS), 1), ("w_up_dil", (DIL_W, D_MODEL), 1), ("w_up_sb", (SB_W, D_MODEL), 1),
       ("w_out", (D_MODEL, D_MODEL), 0), ("w_mlp_in", (D_MODEL, D_FF), 1), ("w_mlp_out", (D_FF, D_MODEL), 0))


def _shard_shape(shape, axis):
    return tuple(d // N_CHIPS if a == axis else d for a, d in enumerate(shape))


MIXER_GROUP, MLP_GROUP = BIG[:4], BIG[4:]


def _pack_rows(group=BIG):
    rows, at = {}, 0
    for name, shape, axis in group:
        n = math.prod(_shard_shape(shape, axis)) // PACK_W
        rows[name] = (at, n)
        at += n
    return rows, at


def _pack_shards(shards):
    return jnp.concatenate([shards[name].reshape(-1, PACK_W) for name, _, _ in BIG], axis=0)


def _unpack_full(gathered):
    rows, _ = _pack_rows()
    full = {}
    for name, shape, axis in BIG:
        at, n = rows[name]
        parts = gathered[:, at:at + n, :].reshape((N_CHIPS,) + _shard_shape(shape, axis))
        if axis == 0:
            full[name] = parts.reshape(shape)
        else:
            full[name] = jnp.transpose(parts, (1, 0, 2)).reshape(shape)
    return full


def _pack_full_grads(grads, group):
    chunks = []
    for name, shape, axis in group:
        g = grads[name]
        if axis == 0:
            parts = g.reshape((N_CHIPS, shape[0] // N_CHIPS, shape[1]))
        else:
            parts = jnp.transpose(g.reshape((shape[0], N_CHIPS, shape[1] // N_CHIPS)), (1, 0, 2))
        chunks.append(parts.reshape(N_CHIPS, -1, PACK_W))
    return jnp.concatenate(chunks, axis=1)


def _unpack_shard(packed, group):
    rows, _ = _pack_rows(group)
    return {name: packed[rows[name][0]:rows[name][0] + rows[name][1]].reshape(_shard_shape(shape, axis))
            for name, shape, axis in group}


def _local_step(x, target, w, norm_mix_g, b_gate, norm_mlp_g, norm_final_g, core):
    w_in = w["w_in"]
    sb0 = 9 * DIL_W
    w_sb, w_gate = w_in[:, sb0:QKV_W], w_in[:, QKV_W:]
    w_dil = [jnp.concatenate([w_in[:, (3 * i + g) * DIL_W:(3 * i + g + 1) * DIL_W] for i in range(3)], axis=1)
             for g in range(3)]

    h = _rms_fwd(x, norm_mix_g, "norm_mix")
    qkv_dil = [_matmul(h, w_dil[g], mode="nn", out_dtypes=(BF16,), name=f"proj_dil_g{g}", tn=768)[0] for g in range(3)]
    (qkv_sb,) = _matmul(h, w_sb, mode="nn", out_dtypes=(BF16,), name="proj_sb", tn=768)
    (gl,) = _matmul(h, w_gate, mode="nn", out_dtypes=(F32,), name="proj_gate")
    dil = [_dil_fwd(qkv_dil[g], g) for g in range(3)]
    o_groups, lse_groups = [d[0] for d in dil], [d[1] for d in dil]
    o_sb, carries = _sb_fwd(qkv_sb)
    merged, o_a = _merge_fwd(o_groups, lse_groups, o_sb, gl, b_gate, w["w_up_dil"], w["w_up_sb"])
    (x1,) = _matmul(merged, w["w_out"], mode="nn", out_dtypes=(F32,), name="out_proj",
                    extras=(x,), epilogue=lambda acc, res: (res + acc,))
    h2 = _rms_fwd(x1, norm_mlp_g, "norm_mlp")
    u, act = _matmul(h2, w["w_mlp_in"], mode="nn", out_dtypes=(F32, BF16), name="mlp_in",
                     epilogue=lambda acc: (acc, jnp.square(jnp.maximum(acc, 0.0))))
    (x2,) = _matmul(act, w["w_mlp_out"], mode="nn", out_dtypes=(F32,), name="mlp_out", tk=2048,
                    extras=(x1,), epilogue=lambda acc, res: (res + acc,))
    dx2, dg_final, loss_part = _loss_head(x2, norm_final_g.reshape(1, D_MODEL), target)

    (du,) = _matmul(dx2, w["w_mlp_out"], mode="nt", out_dtypes=(BF16,), name="mlp_out_dx",
                    extras=(u,), epilogue=lambda acc, uu: (acc * (2.0 * jnp.maximum(uu, 0.0)),))
    (g_mlp_out,) = _matmul(act, dx2, mode="tn", out_dtypes=(F32,), name="mlp_out_dw")
    (g_mlp_in,) = _matmul(h2, du, mode="tn", out_dtypes=(F32,), name="mlp_in_dw")
    (dh2,) = _matmul(du, w["w_mlp_in"], mode="nt", out_dtypes=(F32,), name="mlp_in_dx", tk=2048)
    dx1, dg_mlp = _rms_bwd(dh2, x1, norm_mlp_g, dx2, "norm_mlp_bwd")

    mlp_pack = _pack_full_grads({"w_mlp_in": g_mlp_in, "w_mlp_out": g_mlp_out}, MLP_GROUP)
    (dmerged,) = _matmul(dx1, w["w_out"], mode="nt", out_dtypes=(F32,), name="out_proj_dx")
    (g_out,) = _matmul(merged, dx1, mode="tn", out_dtypes=(F32,), name="out_proj_dw")
    mb = _merge_bwd(dmerged, o_groups, lse_groups, o_sb, gl, b_gate, w["w_up_dil"], w["w_up_sb"], mlp_pack)
    dua, dub, dgl, dbg, do_sb = mb[:5]
    do_groups, c_groups = mb[5:8], mb[8:11]
    mlp_sums = _add_halves(mlp_pack, mb[11], core)
    (g_up_dil,) = _matmul(o_a, dua, mode="tn", out_dtypes=(F32,), name="up_dil_dw")
    (g_up_sb,) = _matmul(o_sb, dub, mode="tn", out_dtypes=(F32,), name="up_sb_dw")
    dq_sb, dk_sb, dv_sb, mlp_got = _sb_bwd(qkv_sb, do_sb, carries, mlp_sums)
    dil_b = [_dil_bwd(qkv_dil[g], do_groups[g], lse_groups[g], c_groups[g], g) for g in range(3)]
    dproj = jnp.concatenate(
        [dil_b[g][i].astype(BF16) for i in range(3) for g in range(3)]
        + [t.astype(BF16) for t in (dq_sb, dk_sb, dv_sb)] + [dgl], axis=1)
    (g_in,) = _matmul(h, dproj, mode="tn", out_dtypes=(F32,), name="proj_dw", tm=512, tn=IN_COLS // 2)
    (dh,) = _matmul(dproj, w["w_in"], mode="nt", out_dtypes=(F32,), name="proj_dx", tk=IN_COLS // 2)
    grad_x, dg_mix = _rms_bwd(dh, x, norm_mix_g, dx1, "norm_mix_bwd")

    mixer = {"w_in": g_in, "w_up_dil": g_up_dil, "w_up_sb": g_up_sb, "w_out": g_out}
    small = (dg_mix, dbg, dg_mlp, dg_final, loss_part)
    return grad_x, mixer, (mlp_got, mlp_sums), small


def kernel(x, norm_mix_g, w_in, b_gate, w_up_dil, w_up_sb, w_out, norm_mlp_g, w_mlp_in, w_mlp_out, norm_final_g, loss_target, m_norm_mix_g, m_w_in, m_b_gate, m_w_up_dil, m_w_up_sb, m_w_out, m_norm_mlp_g, m_w_mlp_in, m_w_mlp_out, m_norm_final_g, v_norm_mix_g, v_w_in, v_b_gate, v_w_up_dil, v_w_up_sb, v_w_out, v_norm_mlp_g, v_w_mlp_in, v_w_mlp_out, v_norm_final_g):
    shards = {"w_in": w_in[0], "w_up_dil": w_up_dil[0], "w_up_sb": w_up_sb[0], "w_out": w_out[0],
              "w_mlp_in": w_mlp_in[0], "w_mlp_out": w_mlp_out[0]}
    moments_m = {"w_in": m_w_in[0], "w_up_dil": m_w_up_dil[0], "w_up_sb": m_w_up_sb[0], "w_out": m_w_out[0],
                 "w_mlp_in": m_w_mlp_in[0], "w_mlp_out": m_w_mlp_out[0]}
    moments_v = {"w_in": v_w_in[0], "w_up_dil": v_w_up_dil[0], "w_up_sb": v_w_up_sb[0], "w_out": v_w_out[0],
                 "w_mlp_in": v_w_mlp_in[0], "w_mlp_out": v_w_mlp_out[0]}

    pack = _pack_shards({n: s.astype(BF16) for n, s in shards.items()})
    full = _unpack_full(_all_gather_weights(pack))

    core = lax.axis_index("c").astype(jnp.int32).reshape(1)
    chip = (2 * lax.axis_index("x") + lax.axis_index("y")).astype(jnp.int32).reshape(1)
    grad_x, mixer, (mlp_got, mlp_sums), small = _local_step(
        x[0], loss_target[0], full, norm_mix_g, b_gate, norm_mlp_g, norm_final_g, core)

    gpack = _pack_full_grads(mixer, MIXER_GROUP)
    chip_sum = _add_halves(gpack, _swap_halves(gpack), core)
    reduced = _join_halves(_sum_chips(_exchange_chunks(chip_sum), chip_sum, chip))
    reduced_mlp = _join_halves(_sum_chips(mlp_got, mlp_sums, chip))
    g_shard = {**_unpack_shard(reduced, MIXER_GROUP), **_unpack_shard(reduced_mlp, MLP_GROUP)}

    dg_mix, dbg, dg_mlp, dg_final, loss_part = small
    loss_row = jnp.sum(loss_part, axis=0, keepdims=True)
    small_pack = jnp.concatenate(
        [jnp.sum(dg_mix, axis=0, keepdims=True), jnp.sum(dbg, axis=0, keepdims=True),
         jnp.sum(dg_mlp, axis=0, keepdims=True), jnp.sum(dg_final, axis=0, keepdims=True), loss_row], axis=1)
    n_small = small_pack.shape[1]
    small_sum = _all_reduce_small(small_pack.reshape(n_small // 128, 128)).reshape(1, n_small)
    g_norm_mix = small_sum[:, :D_MODEL]
    g_b_gate = small_sum[:, D_MODEL:3 * D_MODEL]
    g_norm_mlp = small_sum[:, 3 * D_MODEL:4 * D_MODEL]
    g_norm_final = small_sum[:, 4 * D_MODEL:5 * D_MODEL]
    loss = jnp.sum(small_sum[:, 5 * D_MODEL:])

    names = ["norm_mix_g", "w_in", "b_gate", "w_up_dil", "w_up_sb", "w_out", "norm_mlp_g", "w_mlp_in", "w_mlp_out",
             "norm_final_g"]
    grads = dict(g_shard)
    grads.update(norm_mix_g=g_norm_mix, b_gate=g_b_gate, norm_mlp_g=g_norm_mlp, norm_final_g=g_norm_final)
    weights = dict(shards)
    weights.update(norm_mix_g=norm_mix_g, b_gate=b_gate, norm_mlp_g=norm_mlp_g, norm_final_g=norm_final_g.reshape(1, D_MODEL))
    ms = dict(moments_m)
    ms.update(norm_mix_g=m_norm_mix_g, b_gate=m_b_gate, norm_mlp_g=m_norm_mlp_g, norm_final_g=m_norm_final_g.reshape(1, D_MODEL))
    vs = dict(moments_v)
    vs.update(norm_mix_g=v_norm_mix_g, b_gate=v_b_gate, norm_mlp_g=v_norm_mlp_g, norm_final_g=v_norm_final_g.reshape(1, D_MODEL))

    out_shapes = {"norm_mix_g": norm_mix_g.shape, "w_in": w_in.shape, "b_gate": b_gate.shape, "w_up_dil": w_up_dil.shape,
                  "w_up_sb": w_up_sb.shape, "w_out": w_out.shape, "norm_mlp_g": norm_mlp_g.shape,
                  "w_mlp_in": w_mlp_in.shape, "w_mlp_out": w_mlp_out.shape, "norm_final_g": norm_final_g.shape}
    g_out, d_out, m_out, v_out = [], [], [], []
    for n in names:
        d, nm, nv = _adamw(grads[n], weights[n], ms[n], vs[n], "adamw_" + n)
        shape = out_shapes[n]
        g_out.append(grads[n].reshape(shape))
        d_out.append(d.reshape(shape))
        m_out.append(nm.reshape(shape))
        v_out.append(nv.reshape(shape))
    return (loss, grad_x.reshape(x.shape), *g_out, *d_out, *m_out, *v_out)
```

```python
import functools
import math

import jax
import jax.numpy as jnp
import numpy as np
from jax import lax
from jax.experimental import pallas as pl
from jax.experimental.pallas import tpu as pltpu

F32 = jnp.float32
BF16 = jnp.bfloat16
MESH = pl.DeviceIdType.MESH

D_MODEL = 1024
HEAD_DIM = 64
DIL_GROUPS = ((128, 1), (512, 4), (2048, 16))
DIL_HEADS = 4
DIL_W = 256
N_DIL_HEADS = 12
SB_HEADS = 8
SB_W = SB_HEADS * HEAD_DIM
QKV_W = 3 * 3 * DIL_W + 3 * SB_W
GATE_W = 2 * D_MODEL
IN_COLS = QKV_W + GATE_W
D_FF = 4 * D_MODEL
BLOCK = 128
RMS_EPS = 1e-6
NEG_INF = -1e30
N_CHIPS = 4
N_DEV = 8

ADAM_LR = 0.001
ADAM_B1 = 0.9
ADAM_B2 = 0.999
ADAM_EPS = 1e-08
ADAM_WD = 0.01
ADAM_STEP = 10

VMEM_LIMIT = 56 * 1024 * 1024

SB_BQ = 256
SB_BK = 256


def _cparams(sem=None):
    if sem is None:
        return pltpu.CompilerParams(vmem_limit_bytes=VMEM_LIMIT)
    return pltpu.CompilerParams(dimension_semantics=sem, vmem_limit_bytes=VMEM_LIMIT)


def _dot(a, b, dims):
    return lax.dot_general(a, b, (dims, ((), ())), preferred_element_type=F32)


def _dot_nn(a, b):
    return _dot(a, b, ((1,), (0,)))


def _dot_nt(a, b):
    return _dot(a, b, ((1,), (1,)))


def _dot_tn(a, b):
    return _dot(a, b, ((0,), (0,)))


def _dot_f32_by_01(x, m01, pieces=3):
    hi = x.astype(BF16)
    r1 = x - hi.astype(F32)
    mid = r1.astype(BF16)
    if pieces == 2:
        return _dot_nn(hi, m01) + _dot_nn(mid, m01)
    lo = (r1 - mid.astype(F32)).astype(BF16)
    return _dot_nn(hi, m01) + _dot_nn(mid, m01) + _dot_nn(lo, m01)


def _matmul(a, b, *, mode, out_dtypes, name, tm=1024, tn=1024, tk=1024, extras=(), epilogue=None):
    if mode == "nn":
        (m, k), (k2, n) = a.shape, b.shape
    elif mode == "nt":
        (m, k), (n, k2) = a.shape, b.shape
    else:
        (k, m), (k2, n) = a.shape, b.shape
    assert k == k2, (a.shape, b.shape, mode)
    tm, tn, tk = min(tm, m), min(tn, n), min(tk, k)
    assert m % tm == 0 and n % tn == 0 and k % tk == 0, (m, n, k, tm, tn, tk)
    nk = k // tk
    n_out = len(out_dtypes)
    n_ex = len(extras)

    if mode == "nn":
        a_spec = pl.BlockSpec((tm, tk), lambda i, j, kk: (i, kk))
        b_spec = pl.BlockSpec((tk, tn), lambda i, j, kk: (kk, j))
        dot = _dot_nn
    elif mode == "nt":
        a_spec = pl.BlockSpec((tm, tk), lambda i, j, kk: (i, kk))
        b_spec = pl.BlockSpec((tn, tk), lambda i, j, kk: (j, kk))
        dot = _dot_nt
    else:
        a_spec = pl.BlockSpec((tk, tm), lambda i, j, kk: (kk, i))
        b_spec = pl.BlockSpec((tk, tn), lambda i, j, kk: (kk, j))
        dot = _dot_tn
    mn_spec = pl.BlockSpec((tm, tn), lambda i, j, kk: (i, j))

    def body(*refs):
        a_ref, b_ref = refs[0], refs[1]
        ex_refs = refs[2:2 + n_ex]
        out_refs = refs[2 + n_ex:2 + n_ex + n_out]
        acc_ref = refs[2 + n_ex + n_out] if nk > 1 else None
        part = dot(a_ref[...].astype(BF16), b_ref[...].astype(BF16))

        def finish(acc):
            if epilogue is None:
                outs = (acc,)
            else:
                outs = epilogue(acc, *[r[...] for r in ex_refs])
            for o_ref, o in zip(out_refs, outs):
                o_ref[...] = o.astype(o_ref.dtype)

        if nk == 1:
            finish(part)
        else:
            kk = pl.program_id(2)

            @pl.when(kk == 0)
            def _():
                acc_ref[...] = part

            @pl.when(kk > 0)
            def _():
                acc_ref[...] += part

            @pl.when(kk == nk - 1)
            def _():
                finish(acc_ref[...])

    outs = pl.pallas_call(
        body,
        name=name,
        grid=(m // tm, n // tn, nk),
        in_specs=[a_spec, b_spec] + [mn_spec] * n_ex,
        out_specs=[mn_spec] * n_out,
        out_shape=[jax.ShapeDtypeStruct((m, n), dt) for dt in out_dtypes],
        scratch_shapes=[pltpu.VMEM((tm, tn), F32)] if nk > 1 else [],
        compiler_params=_cparams(("parallel", "parallel", "arbitrary")),
    )(a, b, *extras)
    return outs


ROW_TILE = 512


def _rms_fwd(x, g, name):
    s, d = x.shape

    def body(x_ref, g_ref, h_ref):
        xv = x_ref[...]
        r = lax.rsqrt(jnp.mean(xv * xv, axis=-1, keepdims=True) + RMS_EPS)
        h_ref[...] = (xv * r * g_ref[...]).astype(BF16)

    return pl.pallas_call(
        body,
        name=name,
        grid=(s // ROW_TILE,),
        in_specs=[pl.BlockSpec((ROW_TILE, d), lambda i: (i, 0)), pl.BlockSpec((1, d), lambda i: (0, 0))],
        out_specs=pl.BlockSpec((ROW_TILE, d), lambda i: (i, 0)),
        out_shape=jax.ShapeDtypeStruct((s, d), BF16),
        compiler_params=_cparams(("parallel",)),
    )(x, g)


def _rms_bwd(dh, x, g, dres, name):
    s, d = x.shape

    def body(dh_ref, x_ref, g_ref, dres_ref, dx_ref, dg_ref):
        i = pl.program_id(0)
        xv = x_ref[...]
        r = lax.rsqrt(jnp.mean(xv * xv, axis=-1, keepdims=True) + RMS_EPS)
        xh = xv * r
        dhv = dh_ref[...]
        dxh = dhv * g_ref[...]
        dx = r * (dxh - xh * jnp.mean(dxh * xh, axis=-1, keepdims=True))
        dx_ref[...] = dres_ref[...] + dx
        part = jnp.sum((dhv * xh).reshape(ROW_TILE // 8, 8, d), axis=0)

        @pl.when(i == 0)
        def _():
            dg_ref[...] = part

        @pl.when(i > 0)
        def _():
            dg_ref[...] += part

    row = pl.BlockSpec((ROW_TILE, d), lambda i: (i, 0))
    return pl.pallas_call(
        body,
        name=name,
        grid=(s // ROW_TILE,),
        in_specs=[row, row, pl.BlockSpec((1, d), lambda i: (0, 0)), row],
        out_specs=[row, pl.BlockSpec((8, d), lambda i: (0, 0))],
        out_shape=[jax.ShapeDtypeStruct((s, d), F32), jax.ShapeDtypeStruct((8, d), F32)],
        compiler_params=_cparams(("arbitrary",)),
    )(dh, x, g, dres)


def _loss_head(x2, g, target):
    s, d = x2.shape

    def body(x_ref, g_ref, t_ref, dx_ref, dg_ref, loss_ref):
        i = pl.program_id(0)
        xv = x_ref[...]
        r = lax.rsqrt(jnp.mean(xv * xv, axis=-1, keepdims=True) + RMS_EPS)
        xh = xv * r
        gv = g_ref[...]
        err = xh * gv - t_ref[...]
        dy = err * (1.0 / d)
        dxh = dy * gv
        dx_ref[...] = r * (dxh - xh * jnp.mean(dxh * xh, axis=-1, keepdims=True))
        part_g = jnp.sum((dy * xh).reshape(ROW_TILE // 8, 8, d), axis=0)
        part_l = (0.5 / d) * jnp.sum((err * err).reshape(ROW_TILE // 8, 8, d), axis=0)

        @pl.when(i == 0)
        def _():
            dg_ref[...] = part_g
            loss_ref[...] = part_l

        @pl.when(i > 0)
        def _():
            dg_ref[...] += part_g
            loss_ref[...] += part_l

    row = pl.BlockSpec((ROW_TILE, d), lambda i: (i, 0))
    acc = pl.BlockSpec((8, d), lambda i: (0, 0))
    return pl.pallas_call(
        body,
        name="loss_head",
        grid=(s // ROW_TILE,),
        in_specs=[row, pl.BlockSpec((1, d), lambda i: (0, 0)), row],
        out_specs=[row, acc, acc],
        out_shape=[jax.ShapeDtypeStruct((s, d), F32), jax.ShapeDtypeStruct((8, d), F32),
                   jax.ShapeDtypeStruct((8, d), F32)],
        compiler_params=_cparams(("arbitrary",)),
    )(x2, g, target)


def _alibi_slopes():
    return np.exp2(np.float32(-8.0) * np.arange(1, N_DIL_HEADS + 1, dtype=np.float32) / np.float32(N_DIL_HEADS))


def _head_lane_mask(h, rows):
    lane = lax.broadcasted_iota(jnp.int32, (rows, DIL_W), 1)
    return (lane >= h * HEAD_DIM) & (lane < (h + 1) * HEAD_DIM)


def _band_terms(dil, has_prev):
    qi = lax.broadcasted_iota(jnp.int32, (BLOCK, 2 * BLOCK), 0)
    kj = lax.broadcasted_iota(jnp.int32, (BLOCK, 2 * BLOCK), 1)
    steps = qi + BLOCK - kj
    valid = (steps >= 0) & (steps <= BLOCK) & ((kj >= BLOCK) | has_prev)
    return valid, steps.astype(F32) * float(dil)


def _dil_fwd(qkv_g, group):
    _, dil = DIL_GROUPS[group]
    s = qkv_g.shape[0]
    sub = s // dil
    nb = sub // BLOCK
    view = qkv_g.reshape(sub, dil * 3 * DIL_W)
    slopes = _alibi_slopes()[group * DIL_HEADS:(group + 1) * DIL_HEADS]

    def col(which):
        return lambda r, n: (n, r * 3 + which)

    def col_prev(which):
        return lambda r, n: (jnp.maximum(n - 1, 0), r * 3 + which)

    def body(q_ref, kc_ref, kp_ref, vc_ref, vp_ref, o_ref, lse_ref):
        n = pl.program_id(1)
        valid, dist = _band_terms(dil, n > 0)
        q = q_ref[...]
        k2 = jnp.concatenate([kp_ref[...], kc_ref[...]], axis=0)
        v2 = jnp.concatenate([vp_ref[...], vc_ref[...]], axis=0)
        masks = [_head_lane_mask(h, BLOCK) for h in range(DIL_HEADS)]
        logits = [_dot_nt(jnp.where(masks[h], q, jnp.zeros_like(q)), k2) for h in range(DIL_HEADS)]
        ps, lses = [], []
        for h in range(DIL_HEADS):
            lg = jnp.where(valid, logits[h] * 0.125 - float(slopes[h]) * dist, NEG_INF)
            mx = jnp.max(lg, axis=1, keepdims=True)
            lse = mx + jnp.log(jnp.sum(jnp.exp(lg - mx), axis=1, keepdims=True))
            ps.append(jnp.exp(lg - lse).astype(BF16))
            lses.append(lse)
        o_acc = jnp.zeros((BLOCK, DIL_W), F32)
        lse_acc = jnp.zeros((BLOCK, DIL_W), F32)
        for h in range(DIL_HEADS):
            o_acc = jnp.where(masks[h], _dot_nn(ps[h], v2), o_acc)
            lse_acc = jnp.where(masks[h], lses[h], lse_acc)
        o_ref[...] = o_acc
        lse_ref[...] = lse_acc

    blk = (BLOCK, DIL_W)
    o, lse = pl.pallas_call(
        body,
        name=f"dil_fwd_g{group}",
        grid=(dil, nb),
        in_specs=[pl.BlockSpec(blk, col(0)), pl.BlockSpec(blk, col(1)), pl.BlockSpec(blk, col_prev(1)),
                  pl.BlockSpec(blk, col(2)), pl.BlockSpec(blk, col_prev(2))],
        out_specs=[pl.BlockSpec(blk, lambda r, n: (n, r))] * 2,
        out_shape=[jax.ShapeDtypeStruct((sub, dil * DIL_W), F32)] * 2,
        compiler_params=_cparams(("parallel", "parallel")),
    )(view, view, view, view, view)
    return o.reshape(s, DIL_W), lse.reshape(s, DIL_W)


def _dil_bwd(qkv, do, lse, cterm, group):
    _, dil = DIL_GROUPS[group]
    s = qkv.shape[0]
    sub = s // dil
    nb = sub // BLOCK
    view = qkv.reshape(sub, dil * 3 * DIL_W)
    slopes = _alibi_slopes()[group * DIL_HEADS:(group + 1) * DIL_HEADS]
    do_v, lse_v, c_v = (t.reshape(sub, dil * DIL_W) for t in (do, lse, cterm))

    def col(which, shift):
        if shift == 0:
            return lambda r, n: (n, r * 3 + which)
        if shift < 0:
            return lambda r, n: (jnp.maximum(n - 1, 0), r * 3 + which)
        return lambda r, n: (jnp.minimum(n + 1, nb - 1), r * 3 + which)

    def own(shift):
        if shift == 0:
            return lambda r, n: (n, r)
        return lambda r, n: (jnp.minimum(n + 1, nb - 1), r)

    def body(q_ref, qn_ref, kc_ref, kp_ref, vc_ref, vp_ref, do_ref, don_ref, lse_ref, lsen_ref, c_ref, cn_ref,
             dq_ref, dk_ref, dv_ref):
        n = pl.program_id(1)
        valid, dist = _band_terms(dil, n > 0)
        valid_n = _band_terms(dil, True)[0][:, :BLOCK] & (n < nb - 1)
        dist_n = dist[:, :BLOCK]
        q, qn = q_ref[...], qn_ref[...]
        kc, vc = kc_ref[...], vc_ref[...]
        k2 = jnp.concatenate([kp_ref[...], kc], axis=0)
        v2 = jnp.concatenate([vp_ref[...], vc], axis=0)
        dov, donv = do_ref[...], don_ref[...]
        lsev, lsenv, cv, cnv = lse_ref[...], lsen_ref[...], c_ref[...], cn_ref[...]
        masks = [_head_lane_mask(h, BLOCK) for h in range(DIL_HEADS)]

        def head_col(t, hm):
            return jnp.max(jnp.where(hm, t, NEG_INF), axis=1, keepdims=True)

        qhs = [jnp.where(hm, q, jnp.zeros_like(q)) for hm in masks]
        qnhs = [jnp.where(hm, qn, jnp.zeros_like(qn)) for hm in masks]
        dohs = [jnp.where(hm, dov, 0.0).astype(BF16) for hm in masks]
        donhs = [jnp.where(hm, donv, 0.0).astype(BF16) for hm in masks]
        logit = [_dot_nt(qhs[h], k2) for h in range(DIL_HEADS)]
        dp = [_dot_nt(dohs[h], v2) for h in range(DIL_HEADS)]
        logit_n = [_dot_nt(qnhs[h], kc) for h in range(DIL_HEADS)]
        dp_n = [_dot_nt(donhs[h], vc) for h in range(DIL_HEADS)]
        p16, dlog, pn16, dlog_n = [], [], [], []
        for h in range(DIL_HEADS):
            hm, slope = masks[h], float(slopes[h])
            p = jnp.where(valid, jnp.exp(logit[h] * 0.125 - slope * dist - head_col(lsev, hm)), 0.0)
            dlog.append((p * (dp[h] + head_col(cv, hm)) * 0.125).astype(BF16))
            p16.append(p.astype(BF16))
            pn = jnp.where(valid_n, jnp.exp(logit_n[h] * 0.125 - slope * dist_n - head_col(lsenv, hm)), 0.0)
            dlog_n.append((pn * (dp_n[h] + head_col(cnv, hm)) * 0.125).astype(BF16))
            pn16.append(pn.astype(BF16))
        dq_acc = jnp.zeros((BLOCK, DIL_W), F32)
        dk_acc = jnp.zeros((BLOCK, DIL_W), F32)
        dv_acc = jnp.zeros((BLOCK, DIL_W), F32)
        for h in range(DIL_HEADS):
            dq_acc = jnp.where(masks[h], _dot_nn(dlog[h], k2), dq_acc)
            dk_acc += _dot_tn(dlog[h][:, BLOCK:], qhs[h]) + _dot_tn(dlog_n[h], qnhs[h])
            dv_acc += _dot_tn(p16[h][:, BLOCK:], dohs[h]) + _dot_tn(pn16[h], donhs[h])
        dq_ref[...] = dq_acc.astype(BF16)
        dk_ref[...] = dk_acc.astype(BF16)
        dv_ref[...] = dv_acc.astype(BF16)

    blk = (BLOCK, DIL_W)
    outs = pl.pallas_call(
        body,
        name=f"dil_bwd_g{group}",
        grid=(dil, nb),
        in_specs=[pl.BlockSpec(blk, col(0, 0)), pl.BlockSpec(blk, col(0, 1)),
                  pl.BlockSpec(blk, col(1, 0)), pl.BlockSpec(blk, col(1, -1)),
                  pl.BlockSpec(blk, col(2, 0)), pl.BlockSpec(blk, col(2, -1)),
                  pl.BlockSpec(blk, own(0)), pl.BlockSpec(blk, own(1)),
                  pl.BlockSpec(blk, own(0)), pl.BlockSpec(blk, own(1)),
                  pl.BlockSpec(blk, own(0)), pl.BlockSpec(blk, own(1))],
        out_specs=[pl.BlockSpec(blk, lambda r, n: (n, r))] * 3,
        out_shape=[jax.ShapeDtypeStruct((sub, dil * DIL_W), BF16)] * 3,
        compiler_params=_cparams(("parallel", "parallel")),
    )(view, view, view, view, view, view, do_v, do_v, lse_v, lse_v, c_v, c_v)
    return tuple(t.reshape(s, DIL_W) for t in outs)


SB_PAIRS = SB_HEADS // 2
SB_COL0 = 0
LOG2E = 1.4426950408889634


SB_EXP_CLAMP = 64.0


def _sb_softplus2(zs):
    t = 1.0 + jnp.exp2(jnp.minimum(zs, SB_EXP_CLAMP))
    return jnp.maximum(jnp.log(t) * LOG2E, zs)


def _sb_consts(nkb):
    row = lax.broadcasted_iota(jnp.int32, (SB_BQ, SB_BK), 0)
    colk = lax.broadcasted_iota(jnp.int32, (SB_BQ, SB_BK), 1)
    rr = lax.broadcasted_iota(jnp.int32, (SB_BK, SB_BK), 0)
    cc = lax.broadcasted_iota(jnp.int32, (SB_BK, SB_BK), 1)
    lane = lax.broadcasted_iota(jnp.int32, (SB_BQ, 128), 1)
    assert 2 * nkb <= 128
    return colk < row, rr, cc, lane < HEAD_DIM, lane


def _split_heads(t):
    first = lax.broadcasted_iota(jnp.int32, t.shape, 1) < HEAD_DIM
    zero = jnp.zeros_like(t)
    return jnp.where(first, t, zero), jnp.where(first, zero, t)


def _sb_fwd(qkv, shard_pack):
    s = qkv.shape[0]
    nq, nkb = s // SB_BQ, s // SB_BK
    zscale = LOG2E / math.sqrt(HEAD_DIM)
    r_pack, w_pack = shard_pack.shape

    def body(q_ref, k_ref, v_ref, pack_ref, o_ref, carry_ref, others_ref, zs_scr, a_scr, acc_scr, cl_scr,
             send_sems, recv_sems):
        i = pl.program_id(1)
        pair = pl.program_id(0)
        gather = (pack_ref, others_ref, send_sems, recv_sems)

        @pl.when((pair == 0) & (i == 0))
        def _():
            _gather_start(*gather)

        @pl.when((pair == 1) & (i == 0))
        def _():
            _gather_pass_on(*gather)

        @pl.when((pair == SB_PAIRS - 1) & (i == nq - 1))
        def _():
            _gather_finish(*gather)

        causal, rr, cc, _, lane = _sb_consts(nkb)
        later = (rr > cc).astype(BF16)
        qh = _split_heads(q_ref[...])

        def rows(j):
            return pl.ds(pl.multiple_of(j * SB_BK, SB_BK), SB_BK)

        def scores_to(slot, j):
            kb = k_ref[rows(j), :]
            for hh in range(2):
                zs_scr[slot, hh] = _dot_nt(qh[hh], kb) * zscale

        def weights(slot, j, masked):
            xs, sums, sufs = [], [], []
            for hh in range(2):
                zs = zs_scr[slot, hh]
                sp = _sb_softplus2(zs)
                if masked:
                    sp = jnp.where(causal, sp, 0.0)
                xs.append(zs - sp)
                sums.append(jnp.sum(sp, axis=1, keepdims=True))
                sufs.append(_dot_f32_by_01(sp, later, 2))
            for hh in range(2):
                cl = cl_scr[hh]
                a = jnp.exp2(xs[hh] - (sufs[hh] + jnp.concatenate([cl, cl], axis=1)))
                if masked:
                    a = jnp.where(causal, a, 0.0)
                a_scr[slot, :, hh * SB_BK:(hh + 1) * SB_BK] = a.astype(BF16)
            for hh in range(2):
                cl = cl_scr[hh]
                carry_ref[0] = jnp.where(lane == j + hh * nkb, cl, carry_ref[0])
                cl_scr[hh] = cl + sums[hh]

        def add_av(slot, j):
            v0, v1 = _split_heads(v_ref[rows(j), :])
            acc_scr[...] += _dot_nn(a_scr[slot], jnp.concatenate([v0, v1], axis=0))

        acc_scr[...] = jnp.zeros_like(acc_scr)
        cl_scr[...] = jnp.zeros_like(cl_scr)
        carry_ref[...] = jnp.zeros_like(carry_ref)
        scores_to(0, i)
        scores_to(1, jnp.maximum(i - 1, 0))
        weights(0, i, True)

        def step(j, prev, cur):
            scores_to(prev, jnp.maximum(j - 1, 0))
            add_av(prev, j + 1)
            weights(cur, j, False)

        def two_steps(u, _):
            j = i - 1 - 2 * u
            step(j, 0, 1)
            step(j - 1, 1, 0)
            return 0

        lax.fori_loop(0, i // 2, two_steps, 0)

        @pl.when(i % 2 == 1)
        def _():
            step(0, 0, 1)
            add_av(1, 0)

        @pl.when(i % 2 == 0)
        def _():
            add_av(0, 0)

        o_ref[...] = acc_scr[...]

    def full(which):
        return pl.BlockSpec((s, 128), lambda p, i: (0, SB_COL0 + 4 * which + p))

    return pl.pallas_call(
        body,
        name="sb_fwd",
        grid=(SB_PAIRS, nq),
        in_specs=[pl.BlockSpec((SB_BQ, 128), lambda p, i: (i, SB_COL0 + p)), full(1), full(2), ANY],
        out_specs=[pl.BlockSpec((SB_BQ, 128), lambda p, i: (i, p)),
                   pl.BlockSpec((1, SB_BQ, 128), lambda p, i: (p, i, 0)), ANY],
        out_shape=[jax.ShapeDtypeStruct((s, SB_W), F32), jax.ShapeDtypeStruct((SB_PAIRS, s, 128), F32),
                   jax.ShapeDtypeStruct((N_CHIPS, 2, r_pack // 2, w_pack), shard_pack.dtype)],
        scratch_shapes=[pltpu.VMEM((2, 2, SB_BQ, SB_BK), F32), pltpu.VMEM((2, SB_BQ, 2 * SB_BK), BF16),
                        pltpu.VMEM((SB_BQ, 128), F32), pltpu.VMEM((2, SB_BQ, 128), F32),
                        pltpu.SemaphoreType.DMA((6,)), pltpu.SemaphoreType.DMA((6,))],
        compiler_params=_cparams(("arbitrary", "arbitrary")),
    )(qkv, qkv, qkv, shard_pack.reshape(2, r_pack // 2, w_pack))


def _sb_bwd(qkv, do, carries, chip_sums):
    s = qkv.shape[0]
    nq, nkb = s // SB_BQ, s // SB_BK
    scale = 1.0 / math.sqrt(HEAD_DIM)
    zscale = LOG2E * scale

    def body(q_ref, k_ref, v_ref, do_ref, carry_ref, sums_ref, dq_ref, dk_ref, dv_ref, got_ref,
             zs_scr, da_scr, dz_scr, a_scr, cg_scr, send_sems, recv_sems):
        i = pl.program_id(1)
        first_step = (pl.program_id(0) == 0) & (i == 0)
        last_step = (pl.program_id(0) == SB_PAIRS - 1) & (i == nq - 1)

        @pl.when(first_step)
        def _():
            _exchange_start(sums_ref, got_ref, send_sems, recv_sems)

        @pl.when(i == 0)
        def _():
            dk_ref[...] = jnp.zeros_like(dk_ref)
            dv_ref[...] = jnp.zeros_like(dv_ref)

        causal, rr, cc, first, lane = _sb_consts(nkb)
        later = (rr > cc).astype(BF16)
        earlier = (rr < cc).astype(BF16)
        q2 = q_ref[...]
        qh = _split_heads(q2)
        do2 = do_ref[...].astype(BF16)
        doh = _split_heads(do2)
        ctile = carry_ref[0]

        def rows(j):
            return pl.ds(pl.multiple_of(j * SB_BK, SB_BK), SB_BK)

        def products_to(slot, j):
            kb, vb = k_ref[rows(j), :], v_ref[rows(j), :]
            for hh in range(2):
                zs_scr[slot, hh] = _dot_nt(qh[hh], kb) * zscale
                da_scr[slot, hh] = _dot_nt(doh[hh], vb)

        def by_head(t):
            return jnp.where(first, t[:SB_BK], t[SB_BK:])

        def apply(slot, j):
            k0, k1 = _split_heads(k_ref[rows(j), :])
            dq_ref[...] += _dot_nn(dz_scr[slot], jnp.concatenate([k0, k1], axis=0)) * scale
            dk_ref[rows(j), :] += by_head(_dot_tn(dz_scr[slot], q2)) * scale
            dv_ref[rows(j), :] += by_head(_dot_tn(a_scr[slot], do2))

        def grads(slot, j, masked):
            xs, sigs, sufs = [], [], []
            for hh in range(2):
                zs = zs_scr[slot, hh]
                sp = _sb_softplus2(zs)
                x = zs - sp
                xs.append(x)
                sigs.append(jnp.exp2(x))
                if masked:
                    sp = jnp.where(causal, sp, 0.0)
                sufs.append(_dot_f32_by_01(sp, later, 2))
            gs, gpres = [], []
            for hh in range(2):
                cl = jnp.sum(jnp.where(lane == j + hh * nkb, ctile, 0.0), axis=1, keepdims=True)
                a = jnp.exp2(xs[hh] - (sufs[hh] + cl))
                if masked:
                    a = jnp.where(causal, a, 0.0)
                g = a * da_scr[slot, hh]
                a_scr[slot, :, hh * SB_BK:(hh + 1) * SB_BK] = a.astype(BF16)
                gs.append(g)
                gpres.append(_dot_f32_by_01(g, earlier, 2))
            for hh in range(2):
                cg = cg_scr[hh]
                dz = gs[hh] - (gs[hh] + (gpres[hh] + jnp.concatenate([cg, cg], axis=1))) * sigs[hh]
                if masked:
                    dz = jnp.where(causal, dz, 0.0)
                dz_scr[slot, :, hh * SB_BK:(hh + 1) * SB_BK] = dz.astype(BF16)
                cg_scr[hh] = cg + jnp.sum(gs[hh], axis=1, keepdims=True)

        dq_ref[...] = jnp.zeros_like(dq_ref)
        cg_scr[...] = jnp.zeros_like(cg_scr)
        dz_scr[1] = jnp.zeros((SB_BQ, 2 * SB_BK), BF16)
        a_scr[1] = jnp.zeros((SB_BQ, 2 * SB_BK), BF16)
        products_to(0, 0)

        def step(j, cur, nxt):
            products_to(nxt, j + 1)
            apply(nxt, jnp.maximum(j - 1, 0))
            grads(cur, j, False)

        def two_steps(u, _):
            step(2 * u, 0, 1)
            step(2 * u + 1, 1, 0)
            return 0

        lax.fori_loop(0, i // 2, two_steps, 0)

        def last(cur, nxt):
            apply(nxt, jnp.maximum(i - 1, 0))
            grads(cur, i, True)
            apply(cur, i)

        @pl.when(i % 2 == 1)
        def _():
            step(i - 1, 0, 1)
            last(1, 0)

        @pl.when(i % 2 == 0)
        def _():
            last(0, 1)

        @pl.when(last_step)
        def _():
            _exchange_wait(sums_ref, got_ref, send_sems, recv_sems)

    def full(which):
        return pl.BlockSpec((s, 128), lambda p, i: (0, SB_COL0 + 4 * which + p))

    qblk = pl.BlockSpec((SB_BQ, 128), lambda p, i: (i, p))
    acc = pl.BlockSpec((s, 128), lambda p, i: (0, p))
    return pl.pallas_call(
        body,
        name="sb_bwd",
        grid=(SB_PAIRS, nq),
        in_specs=[pl.BlockSpec((SB_BQ, 128), lambda p, i: (i, SB_COL0 + p)), full(1), full(2), qblk,
                  pl.BlockSpec((1, SB_BQ, 128), lambda p, i: (p, i, 0)), ANY],
        out_specs=[qblk, acc, acc, ANY],
        out_shape=[jax.ShapeDtypeStruct((s, SB_W), F32)] * 3 + [jax.ShapeDtypeStruct(chip_sums.shape, chip_sums.dtype)],
        scratch_shapes=[pltpu.VMEM((2, 2, SB_BQ, SB_BK), F32), pltpu.VMEM((2, 2, SB_BQ, SB_BK), F32),
                        pltpu.VMEM((2, SB_BQ, 2 * SB_BK), BF16), pltpu.VMEM((2, SB_BQ, 2 * SB_BK), BF16),
                        pltpu.VMEM((2, SB_BQ, 128), F32),
                        pltpu.SemaphoreType.DMA((3,)), pltpu.SemaphoreType.DMA((3,))],
        compiler_params=_cparams(("arbitrary", "arbitrary")),
    )(qkv, qkv, qkv, do, carries, chip_sums)


MERGE_TILE = 256


def _group_mix(lses):
    mx = jnp.maximum(jnp.maximum(lses[0], lses[1]), lses[2])
    es = [jnp.exp(t - mx) for t in lses]
    den = es[0] + es[1] + es[2]
    return [e / den for e in es]


def _merge_fwd(o_groups, lse_groups, o_sb, gl, b_gate, w_up_dil, w_up_sb):
    s = gl.shape[0]
    t = MERGE_TILE

    def body(o0, o1, o2, l0, l1, l2, ob_ref, gl_ref, bg_ref, wd_ref, ws_ref, merged_ref, oa_ref):
        w = _group_mix([l0[...], l1[...], l2[...]])
        oa = (w[0] * o0[...] + w[1] * o1[...] + w[2] * o2[...]).astype(BF16)
        ua = _dot_nn(oa, wd_ref[...])
        ub = _dot_nn(ob_ref[...].astype(BF16), ws_ref[...])
        gate = jax.nn.sigmoid(gl_ref[...] + bg_ref[...])
        merged_ref[...] = (gate[:, :D_MODEL] * ua + gate[:, D_MODEL:] * ub).astype(BF16)
        oa_ref[...] = oa

    dil = pl.BlockSpec((t, DIL_W), lambda i: (i, 0))
    const = lambda shape: pl.BlockSpec(shape, lambda i: (0, 0))
    return pl.pallas_call(
        body,
        name="merge_fwd",
        grid=(s // t,),
        in_specs=[dil] * 6 + [pl.BlockSpec((t, SB_W), lambda i: (i, 0)), pl.BlockSpec((t, GATE_W), lambda i: (i, 0)),
                              const((1, GATE_W)), const((DIL_W, D_MODEL)), const((SB_W, D_MODEL))],
        out_specs=[pl.BlockSpec((t, D_MODEL), lambda i: (i, 0)), dil],
        out_shape=[jax.ShapeDtypeStruct((s, D_MODEL), BF16), jax.ShapeDtypeStruct((s, DIL_W), BF16)],
        compiler_params=_cparams(("parallel",)),
    )(*o_groups, *lse_groups, o_sb, gl, b_gate, w_up_dil, w_up_sb)


def _merge_bwd(dmerged, o_groups, lse_groups, o_sb, gl, b_gate, w_up_dil, w_up_sb, swap):
    s = gl.shape[0]
    t = MERGE_TILE
    n_chunks, r_swap, w_swap = swap.shape
    swap = swap.reshape(n_chunks, 2, r_swap // 2, w_swap)

    def body(dm_ref, o0, o1, o2, l0, l1, l2, ob_ref, gl_ref, bg_ref, wd_ref, ws_ref, swap_ref,
             dua_ref, dub_ref, dgl_ref, dbg_ref, dosb_ref, d0, d1, d2, c0, c1, c2, got_ref, send_sem, recv_sem):
        i = pl.program_id(0)

        @pl.when(i == 0)
        def _():
            _swap_copy(swap_ref, got_ref, send_sem, recv_sem).start()

        @pl.when(i == pl.num_programs(0) - 1)
        def _():
            _swap_copy(swap_ref, got_ref, send_sem, recv_sem).wait()

        og = [o0[...], o1[...], o2[...]]
        w = _group_mix([l0[...], l1[...], l2[...]])
        oa = (w[0] * og[0] + w[1] * og[1] + w[2] * og[2]).astype(BF16)
        ua = _dot_nn(oa, wd_ref[...])
        ub = _dot_nn(ob_ref[...].astype(BF16), ws_ref[...])
        gate = jax.nn.sigmoid(gl_ref[...] + bg_ref[...])
        ga, gb = gate[:, :D_MODEL], gate[:, D_MODEL:]
        dm = dm_ref[...]
        dua = (dm * ga).astype(BF16)
        dub = (dm * gb).astype(BF16)
        dua_ref[...] = dua
        dub_ref[...] = dub
        dgl_a = dm * ua * ga * (1.0 - ga)
        dgl_b = dm * ub * gb * (1.0 - gb)
        dgl_ref[:, :D_MODEL] = dgl_a.astype(BF16)
        dgl_ref[:, D_MODEL:] = dgl_b.astype(BF16)
        part = jnp.concatenate([jnp.sum(dgl_a.reshape(t // 8, 8, D_MODEL), axis=0),
                                jnp.sum(dgl_b.reshape(t // 8, 8, D_MODEL), axis=0)], axis=1)

        @pl.when(i == 0)
        def _():
            dbg_ref[...] = part

        @pl.when(i > 0)
        def _():
            dbg_ref[...] += part

        dosb_ref[...] = _dot_nt(dub, ws_ref[...])
        doa = _dot_nt(dua, wd_ref[...])
        rr = lax.broadcasted_iota(jnp.int32, (DIL_W, DIL_W), 0) // HEAD_DIM
        cc = lax.broadcasted_iota(jnp.int32, (DIL_W, DIL_W), 1) // HEAD_DIM
        same_head = (rr == cc).astype(BF16)
        dw = [_dot_f32_by_01(doa * og[g], same_head) for g in range(3)]
        mean_dw = w[0] * dw[0] + w[1] * dw[1] + w[2] * dw[2]
        for g, (d_ref, c_ref) in enumerate(((d0, c0), (d1, c1), (d2, c2))):
            d_ref[...] = w[g] * doa
            c_ref[...] = -w[g] * mean_dw

    dil = pl.BlockSpec((t, DIL_W), lambda i: (i, 0))
    wide = pl.BlockSpec((t, D_MODEL), lambda i: (i, 0))
    gate2 = pl.BlockSpec((t, GATE_W), lambda i: (i, 0))
    sbw = pl.BlockSpec((t, SB_W), lambda i: (i, 0))
    const = lambda shape: pl.BlockSpec(shape, lambda i: (0, 0))
    return pl.pallas_call(
        body,
        name="merge_bwd",
        grid=(s // t,),
        in_specs=[wide] + [dil] * 6 + [sbw, gate2, const((1, GATE_W)), const((DIL_W, D_MODEL)), const((SB_W, D_MODEL)),
                                       ANY],
        out_specs=[wide, wide, gate2, const((8, GATE_W)), sbw] + [dil] * 6 + [ANY],
        out_shape=[jax.ShapeDtypeStruct((s, D_MODEL), BF16), jax.ShapeDtypeStruct((s, D_MODEL), BF16),
                   jax.ShapeDtypeStruct((s, GATE_W), BF16), jax.ShapeDtypeStruct((8, GATE_W), F32),
                   jax.ShapeDtypeStruct((s, SB_W), F32)] + [jax.ShapeDtypeStruct((s, DIL_W), F32)] * 6
        + [jax.ShapeDtypeStruct((n_chunks, r_swap // 2, w_swap), swap.dtype)],
        scratch_shapes=[pltpu.SemaphoreType.DMA, pltpu.SemaphoreType.DMA],
        compiler_params=_cparams(("arbitrary",)),
    )(dmerged, *o_groups, *lse_groups, o_sb, gl, b_gate, w_up_dil, w_up_sb, swap)


ANY = pl.BlockSpec(memory_space=pl.ANY)


def _place():
    x, y, c = lax.axis_index("x"), lax.axis_index("y"), lax.axis_index("c")
    other_chips = [(1 - x, y), (x, 1 - y), (1 - x, 1 - y)]
    return x, y, c, other_chips


def _gather_copies(p_ref, out_ref, send_sems, recv_sems):
    x, y, c, chips = _place()
    me, sibling = 2 * x + y, (x, y, 1 - c)
    idx = [2 * chip[0] + chip[1] for chip in chips]

    def copy(k, chip_idx, core, to, src=None):
        return pltpu.make_async_remote_copy(
            src_ref=out_ref.at[chip_idx, core] if src is None else src, dst_ref=out_ref.at[chip_idx, core],
            send_sem=send_sems.at[k], recv_sem=recv_sems.at[k], device_id=to, device_id_type=MESH)

    first = [copy(j, me, c, (*chip, c), src=p_ref.at[c]) for j, chip in enumerate(chips)]
    landed = [copy(j, idx[j], c, (x, y, c)) for j in range(3)]
    passed = [copy(3 + j, idx[j], c, sibling) for j in range(3)]
    handed = [copy(3 + j, idx[j], 1 - c, (x, y, c)) for j in range(3)]
    return first, landed, passed, handed


def _gather_start(*refs):
    for cp in _gather_copies(*refs)[0]:
        cp.start()


def _gather_pass_on(*refs):
    _, landed, passed, _ = _gather_copies(*refs)
    for arrived, onward in zip(landed, passed):
        arrived.wait_recv()
        onward.start()


def _gather_finish(*refs):
    first, _, passed, handed = _gather_copies(*refs)
    for cp in handed:
        cp.wait_recv()
    for cp in first + passed:
        cp.wait_send()


def _fill_own_slot(others, pack):
    n, _, rh, wd = others.shape
    me = 2 * lax.axis_index("x") + lax.axis_index("y")
    return lax.dynamic_update_slice(others, pack.reshape(1, 2, rh, wd), (me, 0, 0, 0)).reshape(n, 2 * rh, wd)


def _all_gather_weights(pack):
    r, wd = pack.shape
    rh = r // 2

    def body(p_ref, out_ref, send_sems, recv_sems):
        _gather_start(p_ref, out_ref, send_sems, recv_sems)
        _gather_pass_on(p_ref, out_ref, send_sems, recv_sems)
        _gather_finish(p_ref, out_ref, send_sems, recv_sems)

    others = pl.pallas_call(
        body,
        name="all_gather_weights",
        in_specs=[ANY],
        out_specs=ANY,
        out_shape=jax.ShapeDtypeStruct((N_CHIPS, 2, rh, wd), pack.dtype),
        scratch_shapes=[pltpu.SemaphoreType.DMA((6,)), pltpu.SemaphoreType.DMA((6,))],
    )(pack.reshape(2, rh, wd))
    return _fill_own_slot(others, pack)


def _swap_copy(g_ref, out_ref, send_sem, recv_sem):
    x, y, c, _ = _place()
    return pltpu.make_async_remote_copy(
        src_ref=g_ref.at[:, 1 - c], dst_ref=out_ref,
        send_sem=send_sem, recv_sem=recv_sem, device_id=(x, y, 1 - c), device_id_type=MESH)


def _swap_halves(g):
    n, r, wd = g.shape
    rh = r // 2
    g = g.reshape(n, 2, rh, wd)

    def body(g_ref, out_ref, send_sem, recv_sem):
        cp = _swap_copy(g_ref, out_ref, send_sem, recv_sem)
        cp.start()
        cp.wait()

    return pl.pallas_call(
        body,
        name="grad_swap_halves",
        in_specs=[ANY],
        out_specs=ANY,
        out_shape=jax.ShapeDtypeStruct((n, rh, wd), g.dtype),
        scratch_shapes=[pltpu.SemaphoreType.DMA, pltpu.SemaphoreType.DMA],
    )(g)


def _add_halves(g, got, core):
    n, r, wd = g.shape
    rh = r // 2
    t = rh // 4
    nt = rh // t

    def body(c_ref, a_ref, b_ref, o_ref):
        o_ref[...] = (a_ref[0] + b_ref[...]).astype(BF16)

    grid_spec = pltpu.PrefetchScalarGridSpec(
        num_scalar_prefetch=1,
        grid=(n, nt),
        in_specs=[pl.BlockSpec((1, 1, t, wd), lambda s, i, c: (s, c[0], i, 0)),
                  pl.BlockSpec((1, t, wd), lambda s, i, c: (s, i, 0))],
        out_specs=pl.BlockSpec((1, t, wd), lambda s, i, c: (s, i, 0)),
    )
    return pl.pallas_call(
        body,
        name="grad_add_halves",
        grid_spec=grid_spec,
        out_shape=jax.ShapeDtypeStruct((n, rh, wd), BF16),
        compiler_params=_cparams(("parallel", "parallel")),
    )(core, g.reshape(n, 2, rh, wd), got)


def _exchange_copies(h_ref, out_ref, send_sems, recv_sems):
    x, y, c, chips = _place()
    me = 2 * x + y
    sends, arrivals = [], []
    for j, chip in enumerate(chips):
        them = 2 * chip[0] + chip[1]
        sends.append(pltpu.make_async_remote_copy(
            src_ref=h_ref.at[them], dst_ref=out_ref.at[me],
            send_sem=send_sems.at[j], recv_sem=recv_sems.at[j], device_id=(*chip, c), device_id_type=MESH))
        arrivals.append(pltpu.make_async_remote_copy(
            src_ref=h_ref.at[them], dst_ref=out_ref.at[them],
            send_sem=send_sems.at[j], recv_sem=recv_sems.at[j], device_id=(*chip, c), device_id_type=MESH))
    return sends, arrivals


def _exchange_start(h_ref, out_ref, send_sems, recv_sems):
    for cp in _exchange_copies(h_ref, out_ref, send_sems, recv_sems)[0]:
        cp.start()


def _exchange_wait(h_ref, out_ref, send_sems, recv_sems):
    sends, arrivals = _exchange_copies(h_ref, out_ref, send_sems, recv_sems)
    for cp in arrivals:
        cp.wait_recv()
    for cp in sends:
        cp.wait_send()


def _exchange_chunks(h):
    n, rh, wd = h.shape

    def body(h_ref, out_ref, send_sems, recv_sems):
        _exchange_start(h_ref, out_ref, send_sems, recv_sems)
        _exchange_wait(h_ref, out_ref, send_sems, recv_sems)

    return pl.pallas_call(
        body,
        name="grad_exchange_chunks",
        in_specs=[ANY],
        out_specs=ANY,
        out_shape=jax.ShapeDtypeStruct((n, rh, wd), h.dtype),
        scratch_shapes=[pltpu.SemaphoreType.DMA((3,)), pltpu.SemaphoreType.DMA((3,))],
    )(h)


def _sum_chips(b, h, chip):
    n, rh, wd = b.shape
    t = rh // 4

    def body(chip_ref, b_ref, own_ref, o_ref):
        own = own_ref[0]
        s0, s1, s2, s3 = (jnp.where(chip_ref[0] == k, own, b_ref[k]).astype(F32) for k in range(n))
        o_ref[...] = ((s0 + s1) + s2) + s3

    grid_spec = pltpu.PrefetchScalarGridSpec(
        num_scalar_prefetch=1,
        grid=(rh // t,),
        in_specs=[pl.BlockSpec((n, t, wd), lambda i, chip: (0, i, 0)),
                  pl.BlockSpec((1, t, wd), lambda i, chip: (chip[0], i, 0))],
        out_specs=pl.BlockSpec((t, wd), lambda i, chip: (i, 0)),
    )
    return pl.pallas_call(
        body,
        name="grad_sum_chips",
        grid_spec=grid_spec,
        out_shape=jax.ShapeDtypeStruct((rh, wd), F32),
        compiler_params=_cparams(("parallel",)),
    )(chip, b, h)


def _join_halves(tc):
    rh, wd = tc.shape

    def body(t_ref, out_ref, send_sem, recv_sem):
        x, y, c, _ = _place()
        cp = pltpu.make_async_remote_copy(
            src_ref=t_ref, dst_ref=out_ref.at[c],
            send_sem=send_sem, recv_sem=recv_sem, device_id=(x, y, 1 - c), device_id_type=MESH)
        cp.start()
        cp.wait()

    halves = pl.pallas_call(
        body,
        name="grad_join_halves",
        in_specs=[ANY],
        out_specs=ANY,
        out_shape=jax.ShapeDtypeStruct((2, rh, wd), tc.dtype),
        scratch_shapes=[pltpu.SemaphoreType.DMA, pltpu.SemaphoreType.DMA],
    )(tc)
    return lax.dynamic_update_slice(halves, tc[None], (lax.axis_index("c"), 0, 0)).reshape(2 * rh, wd)


def _all_reduce_small(pack):
    rows, lanes = pack.shape

    def body(p_ref, out_ref, buf, send_sems, recv_sems):
        x, y, c, _ = _place()
        me = 4 * x + 2 * y + c
        buf[me] = p_ref[...]
        sends = []
        for k in range(1, N_DEV):
            peer = (x ^ (k >> 2), y ^ ((k >> 1) & 1), c ^ (k & 1))
            sends.append(pltpu.make_async_remote_copy(
                src_ref=p_ref, dst_ref=buf.at[me], send_sem=send_sems.at[k - 1], recv_sem=recv_sems.at[k - 1],
                device_id=peer, device_id_type=MESH))
        for cp in sends:
            cp.start()
        for k in range(1, N_DEV):
            pltpu.make_async_remote_copy(
                src_ref=p_ref, dst_ref=buf.at[me ^ k], send_sem=send_sems.at[k - 1], recv_sem=recv_sems.at[k - 1],
                device_id=(x, y, c), device_id_type=MESH).wait_recv()
        for cp in sends:
            cp.wait_send()
        total = buf[0]
        for d in range(1, N_DEV):
            total = total + buf[d]
        out_ref[...] = total

    vm = pl.BlockSpec(memory_space=pltpu.VMEM)
    return pl.pallas_call(
        body,
        name="all_reduce_small",
        in_specs=[vm],
        out_specs=vm,
        out_shape=jax.ShapeDtypeStruct((rows, lanes), F32),
        scratch_shapes=[pltpu.VMEM((N_DEV, rows, lanes), F32), pltpu.SemaphoreType.DMA((N_DEV - 1,)),
                        pltpu.SemaphoreType.DMA((N_DEV - 1,))],
    )(pack)


def _adamw(g, w, m, v, name):
    rows, cols = g.shape
    t = rows
    for cand in (256, 128, 64, 32, 16, 8):
        if rows % cand == 0:
            t = cand
            break

    def body(g_ref, w_ref, m_ref, v_ref, d_ref, nm_ref, nv_ref):
        gv = g_ref[...]
        mv = ADAM_B1 * m_ref[...] + (1.0 - ADAM_B1) * gv
        vv = ADAM_B2 * v_ref[...] + (1.0 - ADAM_B2) * (gv * gv)
        m_hat = mv / (1.0 - ADAM_B1 ** ADAM_STEP)
        v_hat = vv / (1.0 - ADAM_B2 ** ADAM_STEP)
        d_ref[...] = -ADAM_LR * (m_hat / (jnp.sqrt(v_hat) + ADAM_EPS) + ADAM_WD * w_ref[...])
        nm_ref[...] = mv
        nv_ref[...] = vv

    blk = pl.BlockSpec((t, cols), lambda i: (i, 0))
    return pl.pallas_call(
        body,
        name=name,
        grid=(rows // t,),
        in_specs=[blk] * 4,
        out_specs=[blk] * 3,
        out_shape=[jax.ShapeDtypeStruct((rows, cols), F32)] * 3,
        compiler_params=_cparams(("parallel",)),
    )(g, w, m, v)


PACK_W = 1024
BIG = (("w_in", (D_MODEL, IN_COLS), 1), ("w_up_dil", (DIL_W, D_MODEL), 1), ("w_up_sb", (SB_W, D_MODEL), 1),
       ("w_out", (D_MODEL, D_MODEL), 0), ("w_mlp_in", (D_MODEL, D_FF), 1), ("w_mlp_out", (D_FF, D_MODEL), 0))


def _shard_shape(shape, axis):
    return tuple(d // N_CHIPS if a == axis else d for a, d in enumerate(shape))


MIXER_GROUP, MLP_GROUP = BIG[:4], BIG[4:]


def _pack_rows(group=BIG):
    rows, at = {}, 0
    for name, shape, axis in group:
        n = math.prod(_shard_shape(shape, axis)) // PACK_W
        rows[name] = (at, n)
        at += n
    return rows, at


def _pack_shards(shards, group):
    return jnp.concatenate([shards[name].reshape(-1, PACK_W) for name, _, _ in group], axis=0)


def _unpack_full(gathered, group):
    rows, _ = _pack_rows(group)
    full = {}
    for name, shape, axis in group:
        at, n = rows[name]
        parts = gathered[:, at:at + n, :].reshape((N_CHIPS,) + _shard_shape(shape, axis))
        if axis == 0:
            full[name] = parts.reshape(shape)
        else:
            full[name] = jnp.transpose(parts, (1, 0, 2)).reshape(shape)
    return full


def _pack_full_grads(grads, group):
    chunks = []
    for name, shape, axis in group:
        g = grads[name]
        if axis == 0:
            parts = g.reshape((N_CHIPS, shape[0] // N_CHIPS, shape[1]))
        else:
            parts = jnp.transpose(g.reshape((shape[0], N_CHIPS, shape[1] // N_CHIPS)), (1, 0, 2))
        chunks.append(parts.reshape(N_CHIPS, -1, PACK_W))
    return jnp.concatenate(chunks, axis=1)


def _unpack_shard(packed, group):
    rows, _ = _pack_rows(group)
    return {name: packed[rows[name][0]:rows[name][0] + rows[name][1]].reshape(_shard_shape(shape, axis))
            for name, shape, axis in group}


def _local_step(x, target, w, mlp_shards, norm_mix_g, b_gate, norm_mlp_g, norm_final_g, core):
    w_in = w["w_in"]
    sb0 = 9 * DIL_W
    w_sb, w_gate = w_in[:, sb0:QKV_W], w_in[:, QKV_W:]
    w_dil = [jnp.concatenate([w_in[:, (3 * i + g) * DIL_W:(3 * i + g + 1) * DIL_W] for i in range(3)], axis=1)
             for g in range(3)]

    h = _rms_fwd(x, norm_mix_g, "norm_mix")
    qkv_dil = [_matmul(h, w_dil[g], mode="nn", out_dtypes=(BF16,), name=f"proj_dil_g{g}", tn=768)[0] for g in range(3)]
    (qkv_sb,) = _matmul(h, w_sb, mode="nn", out_dtypes=(BF16,), name="proj_sb", tn=768)
    (gl,) = _matmul(h, w_gate, mode="nn", out_dtypes=(F32,), name="proj_gate")
    dil = [_dil_fwd(qkv_dil[g], g) for g in range(3)]
    o_groups, lse_groups = [d[0] for d in dil], [d[1] for d in dil]
    o_sb, carries, mlp_others = _sb_fwd(qkv_sb, mlp_shards)
    w = {**w, **_unpack_full(_fill_own_slot(mlp_others, mlp_shards), MLP_GROUP)}
    merged, o_a = _merge_fwd(o_groups, lse_groups, o_sb, gl, b_gate, w["w_up_dil"], w["w_up_sb"])
    (x1,) = _matmul(merged, w["w_out"], mode="nn", out_dtypes=(F32,), name="out_proj",
                    extras=(x,), epilogue=lambda acc, res: (res + acc,))
    h2 = _rms_fwd(x1, norm_mlp_g, "norm_mlp")
    u, act = _matmul(h2, w["w_mlp_in"], mode="nn", out_dtypes=(F32, BF16), name="mlp_in",
                     epilogue=lambda acc: (acc, jnp.square(jnp.maximum(acc, 0.0))))
    (x2,) = _matmul(act, w["w_mlp_out"], mode="nn", out_dtypes=(F32,), name="mlp_out", tk=2048,
                    extras=(x1,), epilogue=lambda acc, res: (res + acc,))
    dx2, dg_final, loss_part = _loss_head(x2, norm_final_g.reshape(1, D_MODEL), target)

    (du,) = _matmul(dx2, w["w_mlp_out"], mode="nt", out_dtypes=(BF16,), name="mlp_out_dx",
                    extras=(u,), epilogue=lambda acc, uu: (acc * (2.0 * jnp.maximum(uu, 0.0)),))
    (g_mlp_out,) = _matmul(act, dx2, mode="tn", out_dtypes=(F32,), name="mlp_out_dw")
    (g_mlp_in,) = _matmul(h2, du, mode="tn", out_dtypes=(F32,), name="mlp_in_dw")
    (dh2,) = _matmul(du, w["w_mlp_in"], mode="nt", out_dtypes=(F32,), name="mlp_in_dx", tk=2048)
    dx1, dg_mlp = _rms_bwd(dh2, x1, norm_mlp_g, dx2, "norm_mlp_bwd")

    mlp_pack = _pack_full_grads({"w_mlp_in": g_mlp_in, "w_mlp_out": g_mlp_out}, MLP_GROUP)
    (dmerged,) = _matmul(dx1, w["w_out"], mode="nt", out_dtypes=(F32,), name="out_proj_dx")
    (g_out,) = _matmul(merged, dx1, mode="tn", out_dtypes=(F32,), name="out_proj_dw")
    mb = _merge_bwd(dmerged, o_groups, lse_groups, o_sb, gl, b_gate, w["w_up_dil"], w["w_up_sb"], mlp_pack)
    dua, dub, dgl, dbg, do_sb = mb[:5]
    do_groups, c_groups = mb[5:8], mb[8:11]
    mlp_sums = _add_halves(mlp_pack, mb[11], core)
    (g_up_dil,) = _matmul(o_a, dua, mode="tn", out_dtypes=(F32,), name="up_dil_dw")
    (g_up_sb,) = _matmul(o_sb, dub, mode="tn", out_dtypes=(F32,), name="up_sb_dw")
    dq_sb, dk_sb, dv_sb, mlp_got = _sb_bwd(qkv_sb, do_sb, carries, mlp_sums)
    dil_b = [_dil_bwd(qkv_dil[g], do_groups[g], lse_groups[g], c_groups[g], g) for g in range(3)]
    dproj = jnp.concatenate(
        [dil_b[g][i].astype(BF16) for i in range(3) for g in range(3)]
        + [t.astype(BF16) for t in (dq_sb, dk_sb, dv_sb)] + [dgl], axis=1)
    (g_in,) = _matmul(h, dproj, mode="tn", out_dtypes=(F32,), name="proj_dw", tm=512, tn=IN_COLS // 2)
    (dh,) = _matmul(dproj, w["w_in"], mode="nt", out_dtypes=(F32,), name="proj_dx", tk=IN_COLS // 2)
    grad_x, dg_mix = _rms_bwd(dh, x, norm_mix_g, dx1, "norm_mix_bwd")

    mixer = {"w_in": g_in, "w_up_dil": g_up_dil, "w_up_sb": g_up_sb, "w_out": g_out}
    small = (dg_mix, dbg, dg_mlp, dg_final, loss_part)
    return grad_x, mixer, (mlp_got, mlp_sums), small


def kernel(x, norm_mix_g, w_in, b_gate, w_up_dil, w_up_sb, w_out, norm_mlp_g, w_mlp_in, w_mlp_out, norm_final_g, loss_target, m_norm_mix_g, m_w_in, m_b_gate, m_w_up_dil, m_w_up_sb, m_w_out, m_norm_mlp_g, m_w_mlp_in, m_w_mlp_out, m_norm_final_g, v_norm_mix_g, v_w_in, v_b_gate, v_w_up_dil, v_w_up_sb, v_w_out, v_norm_mlp_g, v_w_mlp_in, v_w_mlp_out, v_norm_final_g):
    shards = {"w_in": w_in[0], "w_up_dil": w_up_dil[0], "w_up_sb": w_up_sb[0], "w_out": w_out[0],
              "w_mlp_in": w_mlp_in[0], "w_mlp_out": w_mlp_out[0]}
    moments_m = {"w_in": m_w_in[0], "w_up_dil": m_w_up_dil[0], "w_up_sb": m_w_up_sb[0], "w_out": m_w_out[0],
                 "w_mlp_in": m_w_mlp_in[0], "w_mlp_out": m_w_mlp_out[0]}
    moments_v = {"w_in": v_w_in[0], "w_up_dil": v_w_up_dil[0], "w_up_sb": v_w_up_sb[0], "w_out": v_w_out[0],
                 "w_mlp_in": v_w_mlp_in[0], "w_mlp_out": v_w_mlp_out[0]}

    shards16 = {n: s.astype(BF16) for n, s in shards.items()}
    full = _unpack_full(_all_gather_weights(_pack_shards(shards16, MIXER_GROUP)), MIXER_GROUP)
    mlp_shards = _pack_shards(shards16, MLP_GROUP)

    core = lax.axis_index("c").astype(jnp.int32).reshape(1)
    chip = (2 * lax.axis_index("x") + lax.axis_index("y")).astype(jnp.int32).reshape(1)
    grad_x, mixer, (mlp_got, mlp_sums), small = _local_step(
        x[0], loss_target[0], full, mlp_shards, norm_mix_g, b_gate, norm_mlp_g, norm_final_g, core)

    gpack = _pack_full_grads(mixer, MIXER_GROUP)
    chip_sum = _add_halves(gpack, _swap_halves(gpack), core)
    reduced = _join_halves(_sum_chips(_exchange_chunks(chip_sum), chip_sum, chip))
    reduced_mlp = _join_halves(_sum_chips(mlp_got, mlp_sums, chip))
    g_shard = {**_unpack_shard(reduced, MIXER_GROUP), **_unpack_shard(reduced_mlp, MLP_GROUP)}

    dg_mix, dbg, dg_mlp, dg_final, loss_part = small
    loss_row = jnp.sum(loss_part, axis=0, keepdims=True)
    small_pack = jnp.concatenate(
        [jnp.sum(dg_mix, axis=0, keepdims=True), jnp.sum(dbg, axis=0, keepdims=True),
         jnp.sum(dg_mlp, axis=0, keepdims=True), jnp.sum(dg_final, axis=0, keepdims=True), loss_row], axis=1)
    n_small = small_pack.shape[1]
    small_sum = _all_reduce_small(small_pack.reshape(n_small // 128, 128)).reshape(1, n_small)
    g_norm_mix = small_sum[:, :D_MODEL]
    g_b_gate = small_sum[:, D_MODEL:3 * D_MODEL]
    g_norm_mlp = small_sum[:, 3 * D_MODEL:4 * D_MODEL]
    g_norm_final = small_sum[:, 4 * D_MODEL:5 * D_MODEL]
    loss = jnp.sum(small_sum[:, 5 * D_MODEL:])

    names = ["norm_mix_g", "w_in", "b_gate", "w_up_dil", "w_up_sb", "w_out", "norm_mlp_g", "w_mlp_in", "w_mlp_out",
             "norm_final_g"]
    grads = dict(g_shard)
    grads.update(norm_mix_g=g_norm_mix, b_gate=g_b_gate, norm_mlp_g=g_norm_mlp, norm_final_g=g_norm_final)
    weights = dict(shards)
    weights.update(norm_mix_g=norm_mix_g, b_gate=b_gate, norm_mlp_g=norm_mlp_g, norm_final_g=norm_final_g.reshape(1, D_MODEL))
    ms = dict(moments_m)
    ms.update(norm_mix_g=m_norm_mix_g, b_gate=m_b_gate, norm_mlp_g=m_norm_mlp_g, norm_final_g=m_norm_final_g.reshape(1, D_MODEL))
    vs = dict(moments_v)
    vs.update(norm_mix_g=v_norm_mix_g, b_gate=v_b_gate, norm_mlp_g=v_norm_mlp_g, norm_final_g=v_norm_final_g.reshape(1, D_MODEL))

    out_shapes = {"norm_mix_g": norm_mix_g.shape, "w_in": w_in.shape, "b_gate": b_gate.shape, "w_up_dil": w_up_dil.shape,
                  "w_up_sb": w_up_sb.shape, "w_out": w_out.shape, "norm_mlp_g": norm_mlp_g.shape,
                  "w_mlp_in": w_mlp_in.shape, "w_mlp_out": w_mlp_out.shape, "norm_final_g": norm_final_g.shape}
    g_out, d_out, m_out, v_out = [], [], [], []
    for n in names:
        d, nm, nv = _adamw(grads[n], weights[n], ms[n], vs[n], "adamw_" + n)
        shape = out_shapes[n]
        g_out.append(grads[n].reshape(shape))
        d_out.append(d.reshape(shape))
        m_out.append(nm.reshape(shape))
        v_out.append(nv.reshape(shape))
    return (loss, grad_x.reshape(x.shape), *g_out, *d_out, *m_out, *v_out)
```

```python
import functools
import math

import jax
import jax.numpy as jnp
import numpy as np
from jax import lax
from jax.experimental import pallas as pl
from jax.experimental.pallas import tpu as pltpu

F32 = jnp.float32
BF16 = jnp.bfloat16
MESH = pl.DeviceIdType.MESH

D_MODEL = 1024
HEAD_DIM = 64
DIL_GROUPS = ((128, 1), (512, 4), (2048, 16))
DIL_HEADS = 4
DIL_W = 256
N_DIL_HEADS = 12
SB_HEADS = 8
SB_W = SB_HEADS * HEAD_DIM
QKV_W = 3 * 3 * DIL_W + 3 * SB_W
GATE_W = 2 * D_MODEL
IN_COLS = QKV_W + GATE_W
D_FF = 4 * D_MODEL
BLOCK = 128
RMS_EPS = 1e-6
NEG_INF = -1e30
N_CHIPS = 4
N_DEV = 8

ADAM_LR = 0.001
ADAM_B1 = 0.9
ADAM_B2 = 0.999
ADAM_EPS = 1e-08
ADAM_WD = 0.01
ADAM_STEP = 10

VMEM_LIMIT = 56 * 1024 * 1024

SB_BQ = 256
SB_BK = 256


def _cparams(sem=None):
    if sem is None:
        return pltpu.CompilerParams(vmem_limit_bytes=VMEM_LIMIT)
    return pltpu.CompilerParams(dimension_semantics=sem, vmem_limit_bytes=VMEM_LIMIT)


def _dot(a, b, dims):
    return lax.dot_general(a, b, (dims, ((), ())), preferred_element_type=F32)


def _dot_nn(a, b):
    return _dot(a, b, ((1,), (0,)))


def _dot_nt(a, b):
    return _dot(a, b, ((1,), (1,)))


def _dot_tn(a, b):
    return _dot(a, b, ((0,), (0,)))


def _dot_f32_by_01(x, m01, pieces=3):
    hi = x.astype(BF16)
    r1 = x - hi.astype(F32)
    mid = r1.astype(BF16)
    if pieces == 2:
        return _dot_nn(hi, m01) + _dot_nn(mid, m01)
    lo = (r1 - mid.astype(F32)).astype(BF16)
    return _dot_nn(hi, m01) + _dot_nn(mid, m01) + _dot_nn(lo, m01)


def _matmul(a, b, *, mode, out_dtypes, name, tm=1024, tn=1024, tk=1024, extras=(), epilogue=None):
    if mode == "nn":
        (m, k), (k2, n) = a.shape, b.shape
    elif mode == "nt":
        (m, k), (n, k2) = a.shape, b.shape
    else:
        (k, m), (k2, n) = a.shape, b.shape
    assert k == k2, (a.shape, b.shape, mode)
    tm, tn, tk = min(tm, m), min(tn, n), min(tk, k)
    assert m % tm == 0 and n % tn == 0 and k % tk == 0, (m, n, k, tm, tn, tk)
    nk = k // tk
    n_out = len(out_dtypes)
    n_ex = len(extras)

    if mode == "nn":
        a_spec = pl.BlockSpec((tm, tk), lambda i, j, kk: (i, kk))
        b_spec = pl.BlockSpec((tk, tn), lambda i, j, kk: (kk, j))
        dot = _dot_nn
    elif mode == "nt":
        a_spec = pl.BlockSpec((tm, tk), lambda i, j, kk: (i, kk))
        b_spec = pl.BlockSpec((tn, tk), lambda i, j, kk: (j, kk))
        dot = _dot_nt
    else:
        a_spec = pl.BlockSpec((tk, tm), lambda i, j, kk: (kk, i))
        b_spec = pl.BlockSpec((tk, tn), lambda i, j, kk: (kk, j))
        dot = _dot_tn
    mn_spec = pl.BlockSpec((tm, tn), lambda i, j, kk: (i, j))

    def body(*refs):
        a_ref, b_ref = refs[0], refs[1]
        ex_refs = refs[2:2 + n_ex]
        out_refs = refs[2 + n_ex:2 + n_ex + n_out]
        acc_ref = refs[2 + n_ex + n_out] if nk > 1 else None
        part = dot(a_ref[...].astype(BF16), b_ref[...].astype(BF16))

        def finish(acc):
            if epilogue is None:
                outs = (acc,)
            else:
                outs = epilogue(acc, *[r[...] for r in ex_refs])
            for o_ref, o in zip(out_refs, outs):
                o_ref[...] = o.astype(o_ref.dtype)

        if nk == 1:
            finish(part)
        else:
            kk = pl.program_id(2)

            @pl.when(kk == 0)
            def _():
                acc_ref[...] = part

            @pl.when(kk > 0)
            def _():
                acc_ref[...] += part

            @pl.when(kk == nk - 1)
            def _():
                finish(acc_ref[...])

    outs = pl.pallas_call(
        body,
        name=name,
        grid=(m // tm, n // tn, nk),
        in_specs=[a_spec, b_spec] + [mn_spec] * n_ex,
        out_specs=[mn_spec] * n_out,
        out_shape=[jax.ShapeDtypeStruct((m, n), dt) for dt in out_dtypes],
        scratch_shapes=[pltpu.VMEM((tm, tn), F32)] if nk > 1 else [],
        compiler_params=_cparams(("parallel", "parallel", "arbitrary")),
    )(a, b, *extras)
    return outs


ROW_TILE = 512


def _rms_fwd(x, g, name):
    s, d = x.shape

    def body(x_ref, g_ref, h_ref):
        xv = x_ref[...]
        r = lax.rsqrt(jnp.mean(xv * xv, axis=-1, keepdims=True) + RMS_EPS)
        h_ref[...] = (xv * r * g_ref[...]).astype(BF16)

    return pl.pallas_call(
        body,
        name=name,
        grid=(s // ROW_TILE,),
        in_specs=[pl.BlockSpec((ROW_TILE, d), lambda i: (i, 0)), pl.BlockSpec((1, d), lambda i: (0, 0))],
        out_specs=pl.BlockSpec((ROW_TILE, d), lambda i: (i, 0)),
        out_shape=jax.ShapeDtypeStruct((s, d), BF16),
        compiler_params=_cparams(("parallel",)),
    )(x, g)


def _rms_bwd(dh, x, g, dres, name):
    s, d = x.shape

    def body(dh_ref, x_ref, g_ref, dres_ref, dx_ref, dg_ref):
        i = pl.program_id(0)
        xv = x_ref[...]
        r = lax.rsqrt(jnp.mean(xv * xv, axis=-1, keepdims=True) + RMS_EPS)
        xh = xv * r
        dhv = dh_ref[...]
        dxh = dhv * g_ref[...]
        dx = r * (dxh - xh * jnp.mean(dxh * xh, axis=-1, keepdims=True))
        dx_ref[...] = dres_ref[...] + dx
        part = jnp.sum((dhv * xh).reshape(ROW_TILE // 8, 8, d), axis=0)

        @pl.when(i == 0)
        def _():
            dg_ref[...] = part

        @pl.when(i > 0)
        def _():
            dg_ref[...] += part

    row = pl.BlockSpec((ROW_TILE, d), lambda i: (i, 0))
    return pl.pallas_call(
        body,
        name=name,
        grid=(s // ROW_TILE,),
        in_specs=[row, row, pl.BlockSpec((1, d), lambda i: (0, 0)), row],
        out_specs=[row, pl.BlockSpec((8, d), lambda i: (0, 0))],
        out_shape=[jax.ShapeDtypeStruct((s, d), F32), jax.ShapeDtypeStruct((8, d), F32)],
        compiler_params=_cparams(("arbitrary",)),
    )(dh, x, g, dres)


def _loss_head(x2, g, target):
    s, d = x2.shape

    def body(x_ref, g_ref, t_ref, dx_ref, dg_ref, loss_ref):
        i = pl.program_id(0)
        xv = x_ref[...]
        r = lax.rsqrt(jnp.mean(xv * xv, axis=-1, keepdims=True) + RMS_EPS)
        xh = xv * r
        gv = g_ref[...]
        err = xh * gv - t_ref[...]
        dy = err * (1.0 / d)
        dxh = dy * gv
        dx_ref[...] = r * (dxh - xh * jnp.mean(dxh * xh, axis=-1, keepdims=True))
        part_g = jnp.sum((dy * xh).reshape(ROW_TILE // 8, 8, d), axis=0)
        part_l = (0.5 / d) * jnp.sum((err * err).reshape(ROW_TILE // 8, 8, d), axis=0)

        @pl.when(i == 0)
        def _():
            dg_ref[...] = part_g
            loss_ref[...] = part_l

        @pl.when(i > 0)
        def _():
            dg_ref[...] += part_g
            loss_ref[...] += part_l

    row = pl.BlockSpec((ROW_TILE, d), lambda i: (i, 0))
    acc = pl.BlockSpec((8, d), lambda i: (0, 0))
    return pl.pallas_call(
        body,
        name="loss_head",
        grid=(s // ROW_TILE,),
        in_specs=[row, pl.BlockSpec((1, d), lambda i: (0, 0)), row],
        out_specs=[row, acc, acc],
        out_shape=[jax.ShapeDtypeStruct((s, d), F32), jax.ShapeDtypeStruct((8, d), F32),
                   jax.ShapeDtypeStruct((8, d), F32)],
        compiler_params=_cparams(("arbitrary",)),
    )(x2, g, target)


def _alibi_slopes():
    return np.exp2(np.float32(-8.0) * np.arange(1, N_DIL_HEADS + 1, dtype=np.float32) / np.float32(N_DIL_HEADS))


def _head_lane_mask(h, rows):
    lane = lax.broadcasted_iota(jnp.int32, (rows, DIL_W), 1)
    return (lane >= h * HEAD_DIM) & (lane < (h + 1) * HEAD_DIM)


def _band_terms(dil, has_prev):
    qi = lax.broadcasted_iota(jnp.int32, (BLOCK, 2 * BLOCK), 0)
    kj = lax.broadcasted_iota(jnp.int32, (BLOCK, 2 * BLOCK), 1)
    steps = qi + BLOCK - kj
    valid = (steps >= 0) & (steps <= BLOCK) & ((kj >= BLOCK) | has_prev)
    return valid, steps.astype(F32) * float(dil)


def _dil_fwd(qkv_g, group):
    _, dil = DIL_GROUPS[group]
    s = qkv_g.shape[0]
    sub = s // dil
    nb = sub // BLOCK
    view = qkv_g.reshape(sub, dil * 3 * DIL_W)
    slopes = _alibi_slopes()[group * DIL_HEADS:(group + 1) * DIL_HEADS]

    def col(which):
        return lambda r, n: (n, r * 3 + which)

    def col_prev(which):
        return lambda r, n: (jnp.maximum(n - 1, 0), r * 3 + which)

    def body(q_ref, kc_ref, kp_ref, vc_ref, vp_ref, o_ref, lse_ref):
        n = pl.program_id(1)
        valid, dist = _band_terms(dil, n > 0)
        q = q_ref[...]
        k2 = jnp.concatenate([kp_ref[...], kc_ref[...]], axis=0)
        v2 = jnp.concatenate([vp_ref[...], vc_ref[...]], axis=0)
        masks = [_head_lane_mask(h, BLOCK) for h in range(DIL_HEADS)]
        logits = [_dot_nt(jnp.where(masks[h], q, jnp.zeros_like(q)), k2) for h in range(DIL_HEADS)]
        ps, lses = [], []
        for h in range(DIL_HEADS):
            lg = jnp.where(valid, logits[h] * 0.125 - float(slopes[h]) * dist, NEG_INF)
            mx = jnp.max(lg, axis=1, keepdims=True)
            lse = mx + jnp.log(jnp.sum(jnp.exp(lg - mx), axis=1, keepdims=True))
            ps.append(jnp.exp(lg - lse).astype(BF16))
            lses.append(lse)
        o_acc = jnp.zeros((BLOCK, DIL_W), F32)
        lse_acc = jnp.zeros((BLOCK, DIL_W), F32)
        for h in range(DIL_HEADS):
            o_acc = jnp.where(masks[h], _dot_nn(ps[h], v2), o_acc)
            lse_acc = jnp.where(masks[h], lses[h], lse_acc)
        o_ref[...] = o_acc
        lse_ref[...] = lse_acc

    blk = (BLOCK, DIL_W)
    o, lse = pl.pallas_call(
        body,
        name=f"dil_fwd_g{group}",
        grid=(dil, nb),
        in_specs=[pl.BlockSpec(blk, col(0)), pl.BlockSpec(blk, col(1)), pl.BlockSpec(blk, col_prev(1)),
                  pl.BlockSpec(blk, col(2)), pl.BlockSpec(blk, col_prev(2))],
        out_specs=[pl.BlockSpec(blk, lambda r, n: (n, r))] * 2,
        out_shape=[jax.ShapeDtypeStruct((sub, dil * DIL_W), F32)] * 2,
        compiler_params=_cparams(("parallel", "parallel")),
    )(view, view, view, view, view)
    return o.reshape(s, DIL_W), lse.reshape(s, DIL_W)


def _dil_bwd(qkv, do, lse, cterm, group):
    _, dil = DIL_GROUPS[group]
    s = qkv.shape[0]
    sub = s // dil
    nb = sub // BLOCK
    view = qkv.reshape(sub, dil * 3 * DIL_W)
    slopes = _alibi_slopes()[group * DIL_HEADS:(group + 1) * DIL_HEADS]
    do_v, lse_v, c_v = (t.reshape(sub, dil * DIL_W) for t in (do, lse, cterm))

    def col(which, shift):
        if shift == 0:
            return lambda r, n: (n, r * 3 + which)
        if shift < 0:
            return lambda r, n: (jnp.maximum(n - 1, 0), r * 3 + which)
        return lambda r, n: (jnp.minimum(n + 1, nb - 1), r * 3 + which)

    def own(shift):
        if shift == 0:
            return lambda r, n: (n, r)
        return lambda r, n: (jnp.minimum(n + 1, nb - 1), r)

    def body(q_ref, qn_ref, kc_ref, kp_ref, vc_ref, vp_ref, do_ref, don_ref, lse_ref, lsen_ref, c_ref, cn_ref,
             dq_ref, dk_ref, dv_ref):
        n = pl.program_id(1)
        valid, dist = _band_terms(dil, n > 0)
        valid_n = _band_terms(dil, True)[0][:, :BLOCK] & (n < nb - 1)
        dist_n = dist[:, :BLOCK]
        q, qn = q_ref[...], qn_ref[...]
        kc, vc = kc_ref[...], vc_ref[...]
        k2 = jnp.concatenate([kp_ref[...], kc], axis=0)
        v2 = jnp.concatenate([vp_ref[...], vc], axis=0)
        dov, donv = do_ref[...], don_ref[...]
        lsev, lsenv, cv, cnv = lse_ref[...], lsen_ref[...], c_ref[...], cn_ref[...]
        masks = [_head_lane_mask(h, BLOCK) for h in range(DIL_HEADS)]

        def head_col(t, hm):
            return jnp.max(jnp.where(hm, t, NEG_INF), axis=1, keepdims=True)

        qhs = [jnp.where(hm, q, jnp.zeros_like(q)) for hm in masks]
        qnhs = [jnp.where(hm, qn, jnp.zeros_like(qn)) for hm in masks]
        dohs = [jnp.where(hm, dov, 0.0).astype(BF16) for hm in masks]
        donhs = [jnp.where(hm, donv, 0.0).astype(BF16) for hm in masks]
        logit = [_dot_nt(qhs[h], k2) for h in range(DIL_HEADS)]
        dp = [_dot_nt(dohs[h], v2) for h in range(DIL_HEADS)]
        logit_n = [_dot_nt(qnhs[h], kc) for h in range(DIL_HEADS)]
        dp_n = [_dot_nt(donhs[h], vc) for h in range(DIL_HEADS)]
        p16, dlog, pn16, dlog_n = [], [], [], []
        for h in range(DIL_HEADS):
            hm, slope = masks[h], float(slopes[h])
            p = jnp.where(valid, jnp.exp(logit[h] * 0.125 - slope * dist - head_col(lsev, hm)), 0.0)
            dlog.append((p * (dp[h] + head_col(cv, hm)) * 0.125).astype(BF16))
            p16.append(p.astype(BF16))
            pn = jnp.where(valid_n, jnp.exp(logit_n[h] * 0.125 - slope * dist_n - head_col(lsenv, hm)), 0.0)
            dlog_n.append((pn * (dp_n[h] + head_col(cnv, hm)) * 0.125).astype(BF16))
            pn16.append(pn.astype(BF16))
        dq_acc = jnp.zeros((BLOCK, DIL_W), F32)
        dk_acc = jnp.zeros((BLOCK, DIL_W), F32)
        dv_acc = jnp.zeros((BLOCK, DIL_W), F32)
        for h in range(DIL_HEADS):
            dq_acc = jnp.where(masks[h], _dot_nn(dlog[h], k2), dq_acc)
            dk_acc += _dot_tn(dlog[h][:, BLOCK:], qhs[h]) + _dot_tn(dlog_n[h], qnhs[h])
            dv_acc += _dot_tn(p16[h][:, BLOCK:], dohs[h]) + _dot_tn(pn16[h], donhs[h])
        dq_ref[...] = dq_acc.astype(BF16)
        dk_ref[...] = dk_acc.astype(BF16)
        dv_ref[...] = dv_acc.astype(BF16)

    blk = (BLOCK, DIL_W)
    outs = pl.pallas_call(
        body,
        name=f"dil_bwd_g{group}",
        grid=(dil, nb),
        in_specs=[pl.BlockSpec(blk, col(0, 0)), pl.BlockSpec(blk, col(0, 1)),
                  pl.BlockSpec(blk, col(1, 0)), pl.BlockSpec(blk, col(1, -1)),
                  pl.BlockSpec(blk, col(2, 0)), pl.BlockSpec(blk, col(2, -1)),
                  pl.BlockSpec(blk, own(0)), pl.BlockSpec(blk, own(1)),
                  pl.BlockSpec(blk, own(0)), pl.BlockSpec(blk, own(1)),
                  pl.BlockSpec(blk, own(0)), pl.BlockSpec(blk, own(1))],
        out_specs=[pl.BlockSpec(blk, lambda r, n: (n, r))] * 3,
        out_shape=[jax.ShapeDtypeStruct((sub, dil * DIL_W), BF16)] * 3,
        compiler_params=_cparams(("parallel", "parallel")),
    )(view, view, view, view, view, view, do_v, do_v, lse_v, lse_v, c_v, c_v)
    return tuple(t.reshape(s, DIL_W) for t in outs)


SB_PAIRS = SB_HEADS // 2
SB_COL0 = 0
LOG2E = 1.4426950408889634


SB_EXP_CLAMP = 64.0


def _sb_softplus2(zs):
    t = 1.0 + jnp.exp2(jnp.minimum(zs, SB_EXP_CLAMP))
    return jnp.maximum(jnp.log(t) * LOG2E, zs)


def _sb_consts(nkb):
    row = lax.broadcasted_iota(jnp.int32, (SB_BQ, SB_BK), 0)
    colk = lax.broadcasted_iota(jnp.int32, (SB_BQ, SB_BK), 1)
    rr = lax.broadcasted_iota(jnp.int32, (SB_BK, SB_BK), 0)
    cc = lax.broadcasted_iota(jnp.int32, (SB_BK, SB_BK), 1)
    lane = lax.broadcasted_iota(jnp.int32, (SB_BQ, 128), 1)
    assert 2 * nkb <= 128
    return colk < row, rr, cc, lane < HEAD_DIM, lane


def _split_heads(t):
    first = lax.broadcasted_iota(jnp.int32, t.shape, 1) < HEAD_DIM
    zero = jnp.zeros_like(t)
    return jnp.where(first, t, zero), jnp.where(first, zero, t)


def _sb_fwd(qkv, shard_pack):
    s = qkv.shape[0]
    nq, nkb = s // SB_BQ, s // SB_BK
    zscale = LOG2E / math.sqrt(HEAD_DIM)
    r_pack, w_pack = shard_pack.shape

    def body(q_ref, k_ref, v_ref, pack_ref, o_ref, a_row, others_ref, zs_scr, a_scr, acc_scr, cl_scr,
             send_sems, recv_sems):
        i = pl.program_id(1)
        pair = pl.program_id(0)
        gather = (pack_ref, others_ref, send_sems, recv_sems)

        @pl.when((pair == 0) & (i == 0))
        def _():
            _gather_start(*gather)

        @pl.when((pair == 1) & (i == 0))
        def _():
            _gather_pass_on(*gather)

        @pl.when((pair == SB_PAIRS - 1) & (i == nq - 1))
        def _():
            _gather_finish(*gather)

        causal, rr, cc, _, _ = _sb_consts(nkb)
        later = (rr > cc).astype(BF16)
        qh = _split_heads(q_ref[...])

        def rows(j):
            return pl.ds(pl.multiple_of(j * SB_BK, SB_BK), SB_BK)

        def scores_to(slot, j):
            kb = k_ref[rows(j), :]
            for hh in range(2):
                zs_scr[slot, hh] = _dot_nt(qh[hh], kb) * zscale

        def weights(slot, j, masked):
            xs, sums, sufs = [], [], []
            for hh in range(2):
                zs = zs_scr[slot, hh]
                sp = _sb_softplus2(zs)
                if masked:
                    sp = jnp.where(causal, sp, 0.0)
                xs.append(zs - sp)
                sums.append(jnp.sum(sp, axis=1, keepdims=True))
                sufs.append(_dot_f32_by_01(sp, later, 2))
            for hh in range(2):
                cl = cl_scr[hh]
                a = jnp.exp2(xs[hh] - (sufs[hh] + jnp.concatenate([cl, cl], axis=1)))
                if masked:
                    a = jnp.where(causal, a, 0.0)
                a16 = a.astype(BF16)
                a_scr[slot, :, hh * SB_BK:(hh + 1) * SB_BK] = a16
                a_row[0, 0, j, :, hh * SB_BK:(hh + 1) * SB_BK] = a16
                cl_scr[hh] = cl + sums[hh]

        def add_av(slot, j):
            v0, v1 = _split_heads(v_ref[rows(j), :])
            acc_scr[...] += _dot_nn(a_scr[slot], jnp.concatenate([v0, v1], axis=0))

        acc_scr[...] = jnp.zeros_like(acc_scr)
        cl_scr[...] = jnp.zeros_like(cl_scr)
        scores_to(0, i)
        scores_to(1, jnp.maximum(i - 1, 0))
        weights(0, i, True)

        def step(j, prev, cur):
            scores_to(prev, jnp.maximum(j - 1, 0))
            add_av(prev, j + 1)
            weights(cur, j, False)

        def two_steps(u, _):
            j = i - 1 - 2 * u
            step(j, 0, 1)
            step(j - 1, 1, 0)
            return 0

        lax.fori_loop(0, i // 2, two_steps, 0)

        @pl.when(i % 2 == 1)
        def _():
            step(0, 0, 1)
            add_av(1, 0)

        @pl.when(i % 2 == 0)
        def _():
            add_av(0, 0)

        o_ref[...] = acc_scr[...]

    def full(which):
        return pl.BlockSpec((s, 128), lambda p, i: (0, SB_COL0 + 4 * which + p))

    return pl.pallas_call(
        body,
        name="sb_fwd",
        grid=(SB_PAIRS, nq),
        in_specs=[pl.BlockSpec((SB_BQ, 128), lambda p, i: (i, SB_COL0 + p)), full(1), full(2), ANY],
        out_specs=[pl.BlockSpec((SB_BQ, 128), lambda p, i: (i, p)),
                   pl.BlockSpec((1, 1, nkb, SB_BQ, 2 * SB_BK), lambda p, i: (p, i, 0, 0, 0)), ANY],
        out_shape=[jax.ShapeDtypeStruct((s, SB_W), F32),
                   jax.ShapeDtypeStruct((SB_PAIRS, nq, nkb, SB_BQ, 2 * SB_BK), BF16),
                   jax.ShapeDtypeStruct((N_CHIPS, 2, r_pack // 2, w_pack), shard_pack.dtype)],
        scratch_shapes=[pltpu.VMEM((2, 2, SB_BQ, SB_BK), F32), pltpu.VMEM((2, SB_BQ, 2 * SB_BK), BF16),
                        pltpu.VMEM((SB_BQ, 128), F32), pltpu.VMEM((2, SB_BQ, 128), F32),
                        pltpu.SemaphoreType.DMA((6,)), pltpu.SemaphoreType.DMA((6,))],
        compiler_params=_cparams(("arbitrary", "arbitrary")),
    )(qkv, qkv, qkv, shard_pack.reshape(2, r_pack // 2, w_pack))


def _sb_bwd(qkv, do, a_hbm, chip_sums):
    s = qkv.shape[0]
    nq, nkb = s // SB_BQ, s // SB_BK
    scale = 1.0 / math.sqrt(HEAD_DIM)
    zscale = LOG2E * scale

    def body(q_ref, k_ref, v_ref, do_ref, a_row, sums_ref, dq_ref, dk_ref, dv_ref, got_ref,
             zs_scr, da_scr, dz_scr, a_scr, cg_scr, send_sems, recv_sems):
        i = pl.program_id(1)
        pair = pl.program_id(0)
        first_step = (pair == 0) & (i == 0)
        last_step = (pair == SB_PAIRS - 1) & (i == nq - 1)

        @pl.when(first_step)
        def _():
            _exchange_start(sums_ref, got_ref, send_sems, recv_sems)

        @pl.when(i == 0)
        def _():
            dk_ref[...] = jnp.zeros_like(dk_ref)
            dv_ref[...] = jnp.zeros_like(dv_ref)

        causal, rr, cc, first, _ = _sb_consts(nkb)
        earlier = (rr < cc).astype(BF16)
        q2 = q_ref[...]
        qh = _split_heads(q2)
        do2 = do_ref[...].astype(BF16)
        doh = _split_heads(do2)

        def rows(j):
            return pl.ds(pl.multiple_of(j * SB_BK, SB_BK), SB_BK)

        def products_to(slot, j):
            kb, vb = k_ref[rows(j), :], v_ref[rows(j), :]
            for hh in range(2):
                zs_scr[slot, hh] = _dot_nt(qh[hh], kb) * zscale
                da_scr[slot, hh] = _dot_nt(doh[hh], vb)

        def by_head(t):
            return jnp.where(first, t[:SB_BK], t[SB_BK:])

        def apply(slot, j):
            k0, k1 = _split_heads(k_ref[rows(j), :])
            dq_ref[...] += _dot_nn(dz_scr[slot], jnp.concatenate([k0, k1], axis=0)) * scale
            dk_ref[rows(j), :] += by_head(_dot_tn(dz_scr[slot], q2)) * scale
            dv_ref[rows(j), :] += by_head(_dot_tn(a_scr[slot], do2))

        def grads(slot, j, masked):
            gs, gpres = [], []
            for hh in range(2):
                a16 = a_row[0, 0, j, :, hh * SB_BK:(hh + 1) * SB_BK]
                a_scr[slot, :, hh * SB_BK:(hh + 1) * SB_BK] = a16
                g = a16.astype(F32) * da_scr[slot, hh]
                gs.append(g)
                gpres.append(_dot_f32_by_01(g, earlier, 2))
            sigs = []
            for hh in range(2):
                zs = zs_scr[slot, hh]
                sigs.append(jnp.exp2(zs - _sb_softplus2(zs)))
            for hh in range(2):
                cg = cg_scr[hh]
                dz = gs[hh] - (gs[hh] + (gpres[hh] + jnp.concatenate([cg, cg], axis=1))) * sigs[hh]
                if masked:
                    dz = jnp.where(causal, dz, 0.0)
                dz_scr[slot, :, hh * SB_BK:(hh + 1) * SB_BK] = dz.astype(BF16)
                cg_scr[hh] = cg + jnp.sum(gs[hh], axis=1, keepdims=True)

        dq_ref[...] = jnp.zeros_like(dq_ref)
        cg_scr[...] = jnp.zeros_like(cg_scr)
        dz_scr[1] = jnp.zeros((SB_BQ, 2 * SB_BK), BF16)
        a_scr[1] = jnp.zeros((SB_BQ, 2 * SB_BK), BF16)
        products_to(0, 0)

        def step(j, cur, nxt):
            products_to(nxt, j + 1)
            apply(nxt, jnp.maximum(j - 1, 0))
            grads(cur, j, False)

        def two_steps(u, _):
            step(2 * u, 0, 1)
            step(2 * u + 1, 1, 0)
            return 0

        lax.fori_loop(0, i // 2, two_steps, 0)

        def last(cur, nxt):
            apply(nxt, jnp.maximum(i - 1, 0))
            grads(cur, i, True)
            apply(cur, i)

        @pl.when(i % 2 == 1)
        def _():
            step(i - 1, 0, 1)
            last(1, 0)

        @pl.when(i % 2 == 0)
        def _():
            last(0, 1)

        @pl.when(last_step)
        def _():
            _exchange_wait(sums_ref, got_ref, send_sems, recv_sems)

    def full(which):
        return pl.BlockSpec((s, 128), lambda p, i: (0, SB_COL0 + 4 * which + p))

    qblk = pl.BlockSpec((SB_BQ, 128), lambda p, i: (i, p))
    acc = pl.BlockSpec((s, 128), lambda p, i: (0, p))
    return pl.pallas_call(
        body,
        name="sb_bwd",
        grid=(SB_PAIRS, nq),
        in_specs=[pl.BlockSpec((SB_BQ, 128), lambda p, i: (i, SB_COL0 + p)), full(1), full(2), qblk,
                  pl.BlockSpec((1, 1, nkb, SB_BQ, 2 * SB_BK), lambda p, i: (p, i, 0, 0, 0)), ANY],
        out_specs=[qblk, acc, acc, ANY],
        out_shape=[jax.ShapeDtypeStruct((s, SB_W), F32)] * 3 + [jax.ShapeDtypeStruct(chip_sums.shape, chip_sums.dtype)],
        scratch_shapes=[pltpu.VMEM((2, 2, SB_BQ, SB_BK), F32), pltpu.VMEM((2, 2, SB_BQ, SB_BK), F32),
                        pltpu.VMEM((2, SB_BQ, 2 * SB_BK), BF16), pltpu.VMEM((2, SB_BQ, 2 * SB_BK), BF16),
                        pltpu.VMEM((2, SB_BQ, 128), F32),
                        pltpu.SemaphoreType.DMA((3,)), pltpu.SemaphoreType.DMA((3,))],
        compiler_params=_cparams(("arbitrary", "arbitrary")),
    )(qkv, qkv, qkv, do, a_hbm, chip_sums)


MERGE_TILE = 256


def _group_mix(lses):
    mx = jnp.maximum(jnp.maximum(lses[0], lses[1]), lses[2])
    es = [jnp.exp(t - mx) for t in lses]
    den = es[0] + es[1] + es[2]
    return [e / den for e in es]


def _merge_fwd(o_groups, lse_groups, o_sb, gl, b_gate, w_up_dil, w_up_sb):
    s = gl.shape[0]
    t = MERGE_TILE

    def body(o0, o1, o2, l0, l1, l2, ob_ref, gl_ref, bg_ref, wd_ref, ws_ref, merged_ref, oa_ref):
        w = _group_mix([l0[...], l1[...], l2[...]])
        oa = (w[0] * o0[...] + w[1] * o1[...] + w[2] * o2[...]).astype(BF16)
        ua = _dot_nn(oa, wd_ref[...])
        ub = _dot_nn(ob_ref[...].astype(BF16), ws_ref[...])
        gate = jax.nn.sigmoid(gl_ref[...] + bg_ref[...])
        merged_ref[...] = (gate[:, :D_MODEL] * ua + gate[:, D_MODEL:] * ub).astype(BF16)
        oa_ref[...] = oa

    dil = pl.BlockSpec((t, DIL_W), lambda i: (i, 0))
    const = lambda shape: pl.BlockSpec(shape, lambda i: (0, 0))
    return pl.pallas_call(
        body,
        name="merge_fwd",
        grid=(s // t,),
        in_specs=[dil] * 6 + [pl.BlockSpec((t, SB_W), lambda i: (i, 0)), pl.BlockSpec((t, GATE_W), lambda i: (i, 0)),
                              const((1, GATE_W)), const((DIL_W, D_MODEL)), const((SB_W, D_MODEL))],
        out_specs=[pl.BlockSpec((t, D_MODEL), lambda i: (i, 0)), dil],
        out_shape=[jax.ShapeDtypeStruct((s, D_MODEL), BF16), jax.ShapeDtypeStruct((s, DIL_W), BF16)],
        compiler_params=_cparams(("parallel",)),
    )(*o_groups, *lse_groups, o_sb, gl, b_gate, w_up_dil, w_up_sb)


def _merge_bwd(dmerged, o_groups, lse_groups, o_sb, gl, b_gate, w_up_dil, w_up_sb, swap):
    s = gl.shape[0]
    t = MERGE_TILE
    n_chunks, r_swap, w_swap = swap.shape
    swap = swap.reshape(n_chunks, 2, r_swap // 2, w_swap)

    def body(dm_ref, o0, o1, o2, l0, l1, l2, ob_ref, gl_ref, bg_ref, wd_ref, ws_ref, swap_ref,
             dua_ref, dub_ref, dgl_ref, dbg_ref, dosb_ref, d0, d1, d2, c0, c1, c2, got_ref, send_sem, recv_sem):
        i = pl.program_id(0)

        @pl.when(i == 0)
        def _():
            _swap_copy(swap_ref, got_ref, send_sem, recv_sem).start()

        @pl.when(i == pl.num_programs(0) - 1)
        def _():
            _swap_copy(swap_ref, got_ref, send_sem, recv_sem).wait()

        og = [o0[...], o1[...], o2[...]]
        w = _group_mix([l0[...], l1[...], l2[...]])
        oa = (w[0] * og[0] + w[1] * og[1] + w[2] * og[2]).astype(BF16)
        ua = _dot_nn(oa, wd_ref[...])
        ub = _dot_nn(ob_ref[...].astype(BF16), ws_ref[...])
        gate = jax.nn.sigmoid(gl_ref[...] + bg_ref[...])
        ga, gb = gate[:, :D_MODEL], gate[:, D_MODEL:]
        dm = dm_ref[...]
        dua = (dm * ga).astype(BF16)
        dub = (dm * gb).astype(BF16)
        dua_ref[...] = dua
        dub_ref[...] = dub
        dgl_a = dm * ua * ga * (1.0 - ga)
        dgl_b = dm * ub * gb * (1.0 - gb)
        dgl_ref[:, :D_MODEL] = dgl_a.astype(BF16)
        dgl_ref[:, D_MODEL:] = dgl_b.astype(BF16)
        part = jnp.concatenate([jnp.sum(dgl_a.reshape(t // 8, 8, D_MODEL), axis=0),
                                jnp.sum(dgl_b.reshape(t // 8, 8, D_MODEL), axis=0)], axis=1)

        @pl.when(i == 0)
        def _():
            dbg_ref[...] = part

        @pl.when(i > 0)
        def _():
            dbg_ref[...] += part

        dosb_ref[...] = _dot_nt(dub, ws_ref[...])
        doa = _dot_nt(dua, wd_ref[...])
        rr = lax.broadcasted_iota(jnp.int32, (DIL_W, DIL_W), 0) // HEAD_DIM
        cc = lax.broadcasted_iota(jnp.int32, (DIL_W, DIL_W), 1) // HEAD_DIM
        same_head = (rr == cc).astype(BF16)
        dw = [_dot_f32_by_01(doa * og[g], same_head) for g in range(3)]
        mean_dw = w[0] * dw[0] + w[1] * dw[1] + w[2] * dw[2]
        for g, (d_ref, c_ref) in enumerate(((d0, c0), (d1, c1), (d2, c2))):
            d_ref[...] = w[g] * doa
            c_ref[...] = -w[g] * mean_dw

    dil = pl.BlockSpec((t, DIL_W), lambda i: (i, 0))
    wide = pl.BlockSpec((t, D_MODEL), lambda i: (i, 0))
    gate2 = pl.BlockSpec((t, GATE_W), lambda i: (i, 0))
    sbw = pl.BlockSpec((t, SB_W), lambda i: (i, 0))
    const = lambda shape: pl.BlockSpec(shape, lambda i: (0, 0))
    return pl.pallas_call(
        body,
        name="merge_bwd",
        grid=(s // t,),
        in_specs=[wide] + [dil] * 6 + [sbw, gate2, const((1, GATE_W)), const((DIL_W, D_MODEL)), const((SB_W, D_MODEL)),
                                       ANY],
        out_specs=[wide, wide, gate2, const((8, GATE_W)), sbw] + [dil] * 6 + [ANY],
        out_shape=[jax.ShapeDtypeStruct((s, D_MODEL), BF16), jax.ShapeDtypeStruct((s, D_MODEL), BF16),
                   jax.ShapeDtypeStruct((s, GATE_W), BF16), jax.ShapeDtypeStruct((8, GATE_W), F32),
                   jax.ShapeDtypeStruct((s, SB_W), F32)] + [jax.ShapeDtypeStruct((s, DIL_W), F32)] * 6
        + [jax.ShapeDtypeStruct((n_chunks, r_swap // 2, w_swap), swap.dtype)],
        scratch_shapes=[pltpu.SemaphoreType.DMA, pltpu.SemaphoreType.DMA],
        compiler_params=_cparams(("arbitrary",)),
    )(dmerged, *o_groups, *lse_groups, o_sb, gl, b_gate, w_up_dil, w_up_sb, swap)


ANY = pl.BlockSpec(memory_space=pl.ANY)


def _place():
    x, y, c = lax.axis_index("x"), lax.axis_index("y"), lax.axis_index("c")
    other_chips = [(1 - x, y), (x, 1 - y), (1 - x, 1 - y)]
    return x, y, c, other_chips


def _gather_copies(p_ref, out_ref, send_sems, recv_sems):
    x, y, c, chips = _place()
    me, sibling = 2 * x + y, (x, y, 1 - c)
    idx = [2 * chip[0] + chip[1] for chip in chips]

    def copy(k, chip_idx, core, to, src=None):
        return pltpu.make_async_remote_copy(
            src_ref=out_ref.at[chip_idx, core] if src is None else src, dst_ref=out_ref.at[chip_idx, core],
            send_sem=send_sems.at[k], recv_sem=recv_sems.at[k], device_id=to, device_id_type=MESH)

    first = [copy(j, me, c, (*chip, c), src=p_ref.at[c]) for j, chip in enumerate(chips)]
    landed = [copy(j, idx[j], c, (x, y, c)) for j in range(3)]
    passed = [copy(3 + j, idx[j], c, sibling) for j in range(3)]
    handed = [copy(3 + j, idx[j], 1 - c, (x, y, c)) for j in range(3)]
    return first, landed, passed, handed


def _gather_start(*refs):
    for cp in _gather_copies(*refs)[0]:
        cp.start()


def _gather_pass_on(*refs):
    _, landed, passed, _ = _gather_copies(*refs)
    for arrived, onward in zip(landed, passed):
        arrived.wait_recv()
        onward.start()


def _gather_finish(*refs):
    first, _, passed, handed = _gather_copies(*refs)
    for cp in handed:
        cp.wait_recv()
    for cp in first + passed:
        cp.wait_send()


def _fill_own_slot(others, pack):
    n, _, rh, wd = others.shape
    me = 2 * lax.axis_index("x") + lax.axis_index("y")
    return lax.dynamic_update_slice(others, pack.reshape(1, 2, rh, wd), (me, 0, 0, 0)).reshape(n, 2 * rh, wd)


def _all_gather_weights(pack):
    r, wd = pack.shape
    rh = r // 2

    def body(p_ref, out_ref, send_sems, recv_sems):
        _gather_start(p_ref, out_ref, send_sems, recv_sems)
        _gather_pass_on(p_ref, out_ref, send_sems, recv_sems)
        _gather_finish(p_ref, out_ref, send_sems, recv_sems)

    others = pl.pallas_call(
        body,
        name="all_gather_weights",
        in_specs=[ANY],
        out_specs=ANY,
        out_shape=jax.ShapeDtypeStruct((N_CHIPS, 2, rh, wd), pack.dtype),
        scratch_shapes=[pltpu.SemaphoreType.DMA((6,)), pltpu.SemaphoreType.DMA((6,))],
    )(pack.reshape(2, rh, wd))
    return _fill_own_slot(others, pack)


def _swap_copy(g_ref, out_ref, send_sem, recv_sem):
    x, y, c, _ = _place()
    return pltpu.make_async_remote_copy(
        src_ref=g_ref.at[:, 1 - c], dst_ref=out_ref,
        send_sem=send_sem, recv_sem=recv_sem, device_id=(x, y, 1 - c), device_id_type=MESH)


def _swap_halves(g):
    n, r, wd = g.shape
    rh = r // 2
    g = g.reshape(n, 2, rh, wd)

    def body(g_ref, out_ref, send_sem, recv_sem):
        cp = _swap_copy(g_ref, out_ref, send_sem, recv_sem)
        cp.start()
        cp.wait()

    return pl.pallas_call(
        body,
        name="grad_swap_halves",
        in_specs=[ANY],
        out_specs=ANY,
        out_shape=jax.ShapeDtypeStruct((n, rh, wd), g.dtype),
        scratch_shapes=[pltpu.SemaphoreType.DMA, pltpu.SemaphoreType.DMA],
    )(g)


def _add_halves(g, got, core):
    n, r, wd = g.shape
    rh = r // 2
    t = rh // 4
    nt = rh // t

    def body(c_ref, a_ref, b_ref, o_ref):
        o_ref[...] = (a_ref[0] + b_ref[...]).astype(BF16)

    grid_spec = pltpu.PrefetchScalarGridSpec(
        num_scalar_prefetch=1,
        grid=(n, nt),
        in_specs=[pl.BlockSpec((1, 1, t, wd), lambda s, i, c: (s, c[0], i, 0)),
                  pl.BlockSpec((1, t, wd), lambda s, i, c: (s, i, 0))],
        out_specs=pl.BlockSpec((1, t, wd), lambda s, i, c: (s, i, 0)),
    )
    return pl.pallas_call(
        body,
        name="grad_add_halves",
        grid_spec=grid_spec,
        out_shape=jax.ShapeDtypeStruct((n, rh, wd), BF16),
        compiler_params=_cparams(("parallel", "parallel")),
    )(core, g.reshape(n, 2, rh, wd), got)


def _exchange_copies(h_ref, out_ref, send_sems, recv_sems):
    x, y, c, chips = _place()
    me = 2 * x + y
    sends, arrivals = [], []
    for j, chip in enumerate(chips):
        them = 2 * chip[0] + chip[1]
        sends.append(pltpu.make_async_remote_copy(
            src_ref=h_ref.at[them], dst_ref=out_ref.at[me],
            send_sem=send_sems.at[j], recv_sem=recv_sems.at[j], device_id=(*chip, c), device_id_type=MESH))
        arrivals.append(pltpu.make_async_remote_copy(
            src_ref=h_ref.at[them], dst_ref=out_ref.at[them],
            send_sem=send_sems.at[j], recv_sem=recv_sems.at[j], device_id=(*chip, c), device_id_type=MESH))
    return sends, arrivals


def _exchange_start(h_ref, out_ref, send_sems, recv_sems):
    for cp in _exchange_copies(h_ref, out_ref, send_sems, recv_sems)[0]:
        cp.start()


def _exchange_wait(h_ref, out_ref, send_sems, recv_sems):
    sends, arrivals = _exchange_copies(h_ref, out_ref, send_sems, recv_sems)
    for cp in arrivals:
        cp.wait_recv()
    for cp in sends:
        cp.wait_send()


def _exchange_chunks(h):
    n, rh, wd = h.shape

    def body(h_ref, out_ref, send_sems, recv_sems):
        _exchange_start(h_ref, out_ref, send_sems, recv_sems)
        _exchange_wait(h_ref, out_ref, send_sems, recv_sems)

    return pl.pallas_call(
        body,
        name="grad_exchange_chunks",
        in_specs=[ANY],
        out_specs=ANY,
        out_shape=jax.ShapeDtypeStruct((n, rh, wd), h.dtype),
        scratch_shapes=[pltpu.SemaphoreType.DMA((3,)), pltpu.SemaphoreType.DMA((3,))],
    )(h)


def _sum_chips(b, h, chip):
    n, rh, wd = b.shape
    t = rh // 4

    def body(chip_ref, b_ref, own_ref, o_ref):
        own = own_ref[0]
        s0, s1, s2, s3 = (jnp.where(chip_ref[0] == k, own, b_ref[k]).astype(F32) for k in range(n))
        o_ref[...] = ((s0 + s1) + s2) + s3

    grid_spec = pltpu.PrefetchScalarGridSpec(
        num_scalar_prefetch=1,
        grid=(rh // t,),
        in_specs=[pl.BlockSpec((n, t, wd), lambda i, chip: (0, i, 0)),
                  pl.BlockSpec((1, t, wd), lambda i, chip: (chip[0], i, 0))],
        out_specs=pl.BlockSpec((t, wd), lambda i, chip: (i, 0)),
    )
    return pl.pallas_call(
        body,
        name="grad_sum_chips",
        grid_spec=grid_spec,
        out_shape=jax.ShapeDtypeStruct((rh, wd), F32),
        compiler_params=_cparams(("parallel",)),
    )(chip, b, h)


def _join_halves(tc):
    rh, wd = tc.shape

    def body(t_ref, out_ref, send_sem, recv_sem):
        x, y, c, _ = _place()
        cp = pltpu.make_async_remote_copy(
            src_ref=t_ref, dst_ref=out_ref.at[c],
            send_sem=send_sem, recv_sem=recv_sem, device_id=(x, y, 1 - c), device_id_type=MESH)
        cp.start()
        cp.wait()

    halves = pl.pallas_call(
        body,
        name="grad_join_halves",
        in_specs=[ANY],
        out_specs=ANY,
        out_shape=jax.ShapeDtypeStruct((2, rh, wd), tc.dtype),
        scratch_shapes=[pltpu.SemaphoreType.DMA, pltpu.SemaphoreType.DMA],
    )(tc)
    return lax.dynamic_update_slice(halves, tc[None], (lax.axis_index("c"), 0, 0)).reshape(2 * rh, wd)


def _all_reduce_small(pack):
    rows, lanes = pack.shape

    def body(p_ref, out_ref, buf, send_sems, recv_sems):
        x, y, c, _ = _place()
        me = 4 * x + 2 * y + c
        buf[me] = p_ref[...]
        sends = []
        for k in range(1, N_DEV):
            peer = (x ^ (k >> 2), y ^ ((k >> 1) & 1), c ^ (k & 1))
            sends.append(pltpu.make_async_remote_copy(
                src_ref=p_ref, dst_ref=buf.at[me], send_sem=send_sems.at[k - 1], recv_sem=recv_sems.at[k - 1],
                device_id=peer, device_id_type=MESH))
        for cp in sends:
            cp.start()
        for k in range(1, N_DEV):
            pltpu.make_async_remote_copy(
                src_ref=p_ref, dst_ref=buf.at[me ^ k], send_sem=send_sems.at[k - 1], recv_sem=recv_sems.at[k - 1],
                device_id=(x, y, c), device_id_type=MESH).wait_recv()
        for cp in sends:
            cp.wait_send()
        total = buf[0]
        for d in range(1, N_DEV):
            total = total + buf[d]
        out_ref[...] = total

    vm = pl.BlockSpec(memory_space=pltpu.VMEM)
    return pl.pallas_call(
        body,
        name="all_reduce_small",
        in_specs=[vm],
        out_specs=vm,
        out_shape=jax.ShapeDtypeStruct((rows, lanes), F32),
        scratch_shapes=[pltpu.VMEM((N_DEV, rows, lanes), F32), pltpu.SemaphoreType.DMA((N_DEV - 1,)),
                        pltpu.SemaphoreType.DMA((N_DEV - 1,))],
    )(pack)


def _adamw(g, w, m, v, name):
    rows, cols = g.shape
    t = rows
    for cand in (256, 128, 64, 32, 16, 8):
        if rows % cand == 0:
            t = cand
            break

    def body(g_ref, w_ref, m_ref, v_ref, d_ref, nm_ref, nv_ref):
        gv = g_ref[...]
        mv = ADAM_B1 * m_ref[...] + (1.0 - ADAM_B1) * gv
        vv = ADAM_B2 * v_ref[...] + (1.0 - ADAM_B2) * (gv * gv)
        m_hat = mv / (1.0 - ADAM_B1 ** ADAM_STEP)
        v_hat = vv / (1.0 - ADAM_B2 ** ADAM_STEP)
        d_ref[...] = -ADAM_LR * (m_hat / (jnp.sqrt(v_hat) + ADAM_EPS) + ADAM_WD * w_ref[...])
        nm_ref[...] = mv
        nv_ref[...] = vv

    blk = pl.BlockSpec((t, cols), lambda i: (i, 0))
    return pl.pallas_call(
        body,
        name=name,
        grid=(rows // t,),
        in_specs=[blk] * 4,
        out_specs=[blk] * 3,
        out_shape=[jax.ShapeDtypeStruct((rows, cols), F32)] * 3,
        compiler_params=_cparams(("parallel",)),
    )(g, w, m, v)


PACK_W = 1024
BIG = (("w_in", (D_MODEL, IN_COLS), 1), ("w_up_dil", (DIL_W, D_MODEL), 1), ("w_up_sb", (SB_W, D_MODEL), 1),
       ("w_out", (D_MODEL, D_MODEL), 0), ("w_mlp_in", (D_MODEL, D_FF), 1), ("w_mlp_out", (D_FF, D_MODEL), 0))


def _shard_shape(shape, axis):
    return tuple(d // N_CHIPS if a == axis else d for a, d in enumerate(shape))


MIXER_GROUP, MLP_GROUP = BIG[:4], BIG[4:]


def _pack_rows(group=BIG):
    rows, at = {}, 0
    for name, shape, axis in group:
        n = math.prod(_shard_shape(shape, axis)) // PACK_W
        rows[name] = (at, n)
        at += n
    return rows, at


def _pack_shards(shards, group):
    return jnp.concatenate([shards[name].reshape(-1, PACK_W) for name, _, _ in group], axis=0)


def _unpack_full(gathered, group):
    rows, _ = _pack_rows(group)
    full = {}
    for name, shape, axis in group:
        at, n = rows[name]
        parts = gathered[:, at:at + n, :].reshape((N_CHIPS,) + _shard_shape(shape, axis))
        if axis == 0:
            full[name] = parts.reshape(shape)
        else:
            full[name] = jnp.transpose(parts, (1, 0, 2)).reshape(shape)
    return full


def _pack_full_grads(grads, group):
    chunks = []
    for name, shape, axis in group:
        g = grads[name]
        if axis == 0:
            parts = g.reshape((N_CHIPS, shape[0] // N_CHIPS, shape[1]))
        else:
            parts = jnp.transpose(g.reshape((shape[0], N_CHIPS, shape[1] // N_CHIPS)), (1, 0, 2))
        chunks.append(parts.reshape(N_CHIPS, -1, PACK_W))
    return jnp.concatenate(chunks, axis=1)


def _unpack_shard(packed, group):
    rows, _ = _pack_rows(group)
    return {name: packed[rows[name][0]:rows[name][0] + rows[name][1]].reshape(_shard_shape(shape, axis))
            for name, shape, axis in group}


def _local_step(x, target, w, mlp_shards, norm_mix_g, b_gate, norm_mlp_g, norm_final_g, core):
    w_in = w["w_in"]
    sb0 = 9 * DIL_W
    w_sb, w_gate = w_in[:, sb0:QKV_W], w_in[:, QKV_W:]
    w_dil = [jnp.concatenate([w_in[:, (3 * i + g) * DIL_W:(3 * i + g + 1) * DIL_W] for i in range(3)], axis=1)
             for g in range(3)]

    h = _rms_fwd(x, norm_mix_g, "norm_mix")
    qkv_dil = [_matmul(h, w_dil[g], mode="nn", out_dtypes=(BF16,), name=f"proj_dil_g{g}", tn=768)[0] for g in range(3)]
    (qkv_sb,) = _matmul(h, w_sb, mode="nn", out_dtypes=(BF16,), name="proj_sb", tn=768)
    (gl,) = _matmul(h, w_gate, mode="nn", out_dtypes=(F32,), name="proj_gate")
    dil = [_dil_fwd(qkv_dil[g], g) for g in range(3)]
    o_groups, lse_groups = [d[0] for d in dil], [d[1] for d in dil]
    o_sb, a_sb, mlp_others = _sb_fwd(qkv_sb, mlp_shards)
    w = {**w, **_unpack_full(_fill_own_slot(mlp_others, mlp_shards), MLP_GROUP)}
    merged, o_a = _merge_fwd(o_groups, lse_groups, o_sb, gl, b_gate, w["w_up_dil"], w["w_up_sb"])
    (x1,) = _matmul(merged, w["w_out"], mode="nn", out_dtypes=(F32,), name="out_proj",
                    extras=(x,), epilogue=lambda acc, res: (res + acc,))
    h2 = _rms_fwd(x1, norm_mlp_g, "norm_mlp")
    u, act = _matmul(h2, w["w_mlp_in"], mode="nn", out_dtypes=(F32, BF16), name="mlp_in",
                     epilogue=lambda acc: (acc, jnp.square(jnp.maximum(acc, 0.0))))
    (x2,) = _matmul(act, w["w_mlp_out"], mode="nn", out_dtypes=(F32,), name="mlp_out", tk=2048,
                    extras=(x1,), epilogue=lambda acc, res: (res + acc,))
    dx2, dg_final, loss_part = _loss_head(x2, norm_final_g.reshape(1, D_MODEL), target)

    (du,) = _matmul(dx2, w["w_mlp_out"], mode="nt", out_dtypes=(BF16,), name="mlp_out_dx",
                    extras=(u,), epilogue=lambda acc, uu: (acc * (2.0 * jnp.maximum(uu, 0.0)),))
    (g_mlp_out,) = _matmul(act, dx2, mode="tn", out_dtypes=(F32,), name="mlp_out_dw")
    (g_mlp_in,) = _matmul(h2, du, mode="tn", out_dtypes=(F32,), name="mlp_in_dw")
    (dh2,) = _matmul(du, w["w_mlp_in"], mode="nt", out_dtypes=(F32,), name="mlp_in_dx", tk=2048)
    dx1, dg_mlp = _rms_bwd(dh2, x1, norm_mlp_g, dx2, "norm_mlp_bwd")

    mlp_pack = _pack_full_grads({"w_mlp_in": g_mlp_in, "w_mlp_out": g_mlp_out}, MLP_GROUP)
    (dmerged,) = _matmul(dx1, w["w_out"], mode="nt", out_dtypes=(F32,), name="out_proj_dx")
    (g_out,) = _matmul(merged, dx1, mode="tn", out_dtypes=(F32,), name="out_proj_dw")
    mb = _merge_bwd(dmerged, o_groups, lse_groups, o_sb, gl, b_gate, w["w_up_dil"], w["w_up_sb"], mlp_pack)
    dua, dub, dgl, dbg, do_sb = mb[:5]
    do_groups, c_groups = mb[5:8], mb[8:11]
    mlp_sums = _add_halves(mlp_pack, mb[11], core)
    (g_up_dil,) = _matmul(o_a, dua, mode="tn", out_dtypes=(F32,), name="up_dil_dw")
    (g_up_sb,) = _matmul(o_sb, dub, mode="tn", out_dtypes=(F32,), name="up_sb_dw")
    dq_sb, dk_sb, dv_sb, mlp_got = _sb_bwd(qkv_sb, do_sb, a_sb, mlp_sums)
    dil_b = [_dil_bwd(qkv_dil[g], do_groups[g], lse_groups[g], c_groups[g], g) for g in range(3)]
    dproj = jnp.concatenate(
        [dil_b[g][i].astype(BF16) for i in range(3) for g in range(3)]
        + [t.astype(BF16) for t in (dq_sb, dk_sb, dv_sb)] + [dgl], axis=1)
    (g_in,) = _matmul(h, dproj, mode="tn", out_dtypes=(F32,), name="proj_dw", tm=512, tn=IN_COLS // 2)
    (dh,) = _matmul(dproj, w["w_in"], mode="nt", out_dtypes=(F32,), name="proj_dx", tk=IN_COLS // 2)
    grad_x, dg_mix = _rms_bwd(dh, x, norm_mix_g, dx1, "norm_mix_bwd")

    mixer = {"w_in": g_in, "w_up_dil": g_up_dil, "w_up_sb": g_up_sb, "w_out": g_out}
    small = (dg_mix, dbg, dg_mlp, dg_final, loss_part)
    return grad_x, mixer, (mlp_got, mlp_sums), small


def kernel(x, norm_mix_g, w_in, b_gate, w_up_dil, w_up_sb, w_out, norm_mlp_g, w_mlp_in, w_mlp_out, norm_final_g, loss_target, m_norm_mix_g, m_w_in, m_b_gate, m_w_up_dil, m_w_up_sb, m_w_out, m_norm_mlp_g, m_w_mlp_in, m_w_mlp_out, m_norm_final_g, v_norm_mix_g, v_w_in, v_b_gate, v_w_up_dil, v_w_up_sb, v_w_out, v_norm_mlp_g, v_w_mlp_in, v_w_mlp_out, v_norm_final_g):
    shards = {"w_in": w_in[0], "w_up_dil": w_up_dil[0], "w_up_sb": w_up_sb[0], "w_out": w_out[0],
              "w_mlp_in": w_mlp_in[0], "w_mlp_out": w_mlp_out[0]}
    moments_m = {"w_in": m_w_in[0], "w_up_dil": m_w_up_dil[0], "w_up_sb": m_w_up_sb[0], "w_out": m_w_out[0],
                 "w_mlp_in": m_w_mlp_in[0], "w_mlp_out": m_w_mlp_out[0]}
    moments_v = {"w_in": v_w_in[0], "w_up_dil": v_w_up_dil[0], "w_up_sb": v_w_up_sb[0], "w_out": v_w_out[0],
                 "w_mlp_in": v_w_mlp_in[0], "w_mlp_out": v_w_mlp_out[0]}

    shards16 = {n: s.astype(BF16) for n, s in shards.items()}
    full = _unpack_full(_all_gather_weights(_pack_shards(shards16, MIXER_GROUP)), MIXER_GROUP)
    mlp_shards = _pack_shards(shards16, MLP_GROUP)

    core = lax.axis_index("c").astype(jnp.int32).reshape(1)
    chip = (2 * lax.axis_index("x") + lax.axis_index("y")).astype(jnp.int32).reshape(1)
    grad_x, mixer, (mlp_got, mlp_sums), small = _local_step(
        x[0], loss_target[0], full, mlp_shards, norm_mix_g, b_gate, norm_mlp_g, norm_final_g, core)

    gpack = _pack_full_grads(mixer, MIXER_GROUP)
    chip_sum = _add_halves(gpack, _swap_halves(gpack), core)
    reduced = _join_halves(_sum_chips(_exchange_chunks(chip_sum), chip_sum, chip))
    reduced_mlp = _join_halves(_sum_chips(mlp_got, mlp_sums, chip))
    g_shard = {**_unpack_shard(reduced, MIXER_GROUP), **_unpack_shard(reduced_mlp, MLP_GROUP)}

    dg_mix, dbg, dg_mlp, dg_final, loss_part = small
    loss_row = jnp.sum(loss_part, axis=0, keepdims=True)
    small_pack = jnp.concatenate(
        [jnp.sum(dg_mix, axis=0, keepdims=True), jnp.sum(dbg, axis=0, keepdims=True),
         jnp.sum(dg_mlp, axis=0, keepdims=True), jnp.sum(dg_final, axis=0, keepdims=True), loss_row], axis=1)
    n_small = small_pack.shape[1]
    small_sum = _all_reduce_small(small_pack.reshape(n_small // 128, 128)).reshape(1, n_small)
    g_norm_mix = small_sum[:, :D_MODEL]
    g_b_gate = small_sum[:, D_MODEL:3 * D_MODEL]
    g_norm_mlp = small_sum[:, 3 * D_MODEL:4 * D_MODEL]
    g_norm_final = small_sum[:, 4 * D_MODEL:5 * D_MODEL]
    loss = jnp.sum(small_sum[:, 5 * D_MODEL:])

    names = ["norm_mix_g", "w_in", "b_gate", "w_up_dil", "w_up_sb", "w_out", "norm_mlp_g", "w_mlp_in", "w_mlp_out",
             "norm_final_g"]
    grads = dict(g_shard)
    grads.update(norm_mix_g=g_norm_mix, b_gate=g_b_gate, norm_mlp_g=g_norm_mlp, norm_final_g=g_norm_final)
    weights = dict(shards)
    weights.update(norm_mix_g=norm_mix_g, b_gate=b_gate, norm_mlp_g=norm_mlp_g, norm_final_g=norm_final_g.reshape(1, D_MODEL))
    ms = dict(moments_m)
    ms.update(norm_mix_g=m_norm_mix_g, b_gate=m_b_gate, norm_mlp_g=m_norm_mlp_g, norm_final_g=m_norm_final_g.reshape(1, D_MODEL))
    vs = dict(moments_v)
    vs.update(norm_mix_g=v_norm_mix_g, b_gate=v_b_gate, norm_mlp_g=v_norm_mlp_g, norm_final_g=v_norm_final_g.reshape(1, D_MODEL))

    out_shapes = {"norm_mix_g": norm_mix_g.shape, "w_in": w_in.shape, "b_gate": b_gate.shape, "w_up_dil": w_up_dil.shape,
                  "w_up_sb": w_up_sb.shape, "w_out": w_out.shape, "norm_mlp_g": norm_mlp_g.shape,
                  "w_mlp_in": w_mlp_in.shape, "w_mlp_out": w_mlp_out.shape, "norm_final_g": norm_final_g.shape}
    g_out, d_out, m_out, v_out = [], [], [], []
    for n in names:
        d, nm, nv = _adamw(grads[n], weights[n], ms[n], vs[n], "adamw_" + n)
        shape = out_shapes[n]
        g_out.append(grads[n].reshape(shape))
        d_out.append(d.reshape(shape))
        m_out.append(nm.reshape(shape))
        v_out.append(nv.reshape(shape))
    return (loss, grad_x.reshape(x.shape), *g_out, *d_out, *m_out, *v_out)
```

```python
import functools
import math

import jax
import jax.numpy as jnp
import numpy as np
from jax import lax
from jax.experimental import pallas as pl
from jax.experimental.pallas import tpu as pltpu

F32 = jnp.float32
BF16 = jnp.bfloat16
MESH = pl.DeviceIdType.MESH

D_MODEL = 1024
HEAD_DIM = 64
DIL_GROUPS = ((128, 1), (512, 4), (2048, 16))
DIL_HEADS = 4
DIL_W = 256
N_DIL_HEADS = 12
SB_HEADS = 8
SB_W = SB_HEADS * HEAD_DIM
QKV_W = 3 * 3 * DIL_W + 3 * SB_W
GATE_W = 2 * D_MODEL
IN_COLS = QKV_W + GATE_W
D_FF = 4 * D_MODEL
BLOCK = 128
RMS_EPS = 1e-6
NEG_INF = -1e30
N_CHIPS = 4
N_DEV = 8

ADAM_LR = 0.001
ADAM_B1 = 0.9
ADAM_B2 = 0.999
ADAM_EPS = 1e-08
ADAM_WD = 0.01
ADAM_STEP = 10

VMEM_LIMIT = 56 * 1024 * 1024

SB_BQ = 256
SB_BK = 256


def _cparams(sem=None):
    if sem is None:
        return pltpu.CompilerParams(vmem_limit_bytes=VMEM_LIMIT)
    return pltpu.CompilerParams(dimension_semantics=sem, vmem_limit_bytes=VMEM_LIMIT)


def _dot(a, b, dims):
    return lax.dot_general(a, b, (dims, ((), ())), preferred_element_type=F32)


def _dot_nn(a, b):
    return _dot(a, b, ((1,), (0,)))


def _dot_nt(a, b):
    return _dot(a, b, ((1,), (1,)))


def _dot_tn(a, b):
    return _dot(a, b, ((0,), (0,)))


def _dot_f32_by_01(x, m01, pieces=3):
    hi = x.astype(BF16)
    if pieces == 1:
        return _dot_nn(hi, m01)
    r1 = x - hi.astype(F32)
    mid = r1.astype(BF16)
    if pieces == 2:
        return _dot_nn(hi, m01) + _dot_nn(mid, m01)
    lo = (r1 - mid.astype(F32)).astype(BF16)
    return _dot_nn(hi, m01) + _dot_nn(mid, m01) + _dot_nn(lo, m01)


def _matmul(a, b, *, mode, out_dtypes, name, tm=1024, tn=1024, tk=1024, extras=(), epilogue=None, exchange=None):
    if mode == "nn":
        (m, k), (k2, n) = a.shape, b.shape
    elif mode == "nt":
        (m, k), (n, k2) = a.shape, b.shape
    else:
        (k, m), (k2, n) = a.shape, b.shape
    assert k == k2, (a.shape, b.shape, mode)
    tm, tn, tk = min(tm, m), min(tn, n), min(tk, k)
    assert m % tm == 0 and n % tn == 0 and k % tk == 0, (m, n, k, tm, tn, tk)
    nk = k // tk
    n_out = len(out_dtypes)
    n_ex = len(extras)

    if mode == "nn":
        a_spec = pl.BlockSpec((tm, tk), lambda i, j, kk: (i, kk))
        b_spec = pl.BlockSpec((tk, tn), lambda i, j, kk: (kk, j))
        dot = _dot_nn
    elif mode == "nt":
        a_spec = pl.BlockSpec((tm, tk), lambda i, j, kk: (i, kk))
        b_spec = pl.BlockSpec((tn, tk), lambda i, j, kk: (j, kk))
        dot = _dot_nt
    else:
        a_spec = pl.BlockSpec((tk, tm), lambda i, j, kk: (kk, i))
        b_spec = pl.BlockSpec((tk, tn), lambda i, j, kk: (kk, j))
        dot = _dot_tn
    mn_spec = pl.BlockSpec((tm, tn), lambda i, j, kk: (i, j))

    n_side = 0 if exchange is None else 1
    grid = (m // tm, n // tn, nk)

    def body(*refs):
        a_ref, b_ref = refs[0], refs[1]
        ex_refs = refs[2:2 + n_ex]
        out_refs = refs[2 + n_ex + n_side:2 + n_ex + n_side + n_out]
        scratch = refs[2 + n_ex + n_side + n_out + n_side:]
        acc_ref = scratch[0] if nk > 1 else None
        if exchange is not None:
            side = (refs[2 + n_ex], refs[2 + n_ex + n_side + n_out]) + tuple(scratch[-2:])
            step = (pl.program_id(0) * grid[1] + pl.program_id(1)) * grid[2] + pl.program_id(2)

            @pl.when(step == 0)
            def _():
                _exchange_start(*side)

            @pl.when(step == grid[0] * grid[1] * grid[2] - 1)
            def _():
                _exchange_wait(*side)

        part = dot(a_ref[...].astype(BF16), b_ref[...].astype(BF16))

        def finish(acc):
            if epilogue is None:
                outs = (acc,)
            else:
                outs = epilogue(acc, *[r[...] for r in ex_refs])
            for o_ref, o in zip(out_refs, outs):
                o_ref[...] = o.astype(o_ref.dtype)

        if nk == 1:
            finish(part)
        else:
            kk = pl.program_id(2)

            @pl.when(kk == 0)
            def _():
                acc_ref[...] = part

            @pl.when(kk > 0)
            def _():
                acc_ref[...] += part

            @pl.when(kk == nk - 1)
            def _():
                finish(acc_ref[...])

    side_in = [] if exchange is None else [exchange]
    outs = pl.pallas_call(
        body,
        name=name,
        grid=grid,
        in_specs=[a_spec, b_spec] + [mn_spec] * n_ex + [ANY] * n_side,
        out_specs=[mn_spec] * n_out + [ANY] * n_side,
        out_shape=[jax.ShapeDtypeStruct((m, n), dt) for dt in out_dtypes]
        + [jax.ShapeDtypeStruct(e.shape, e.dtype) for e in side_in],
        scratch_shapes=([pltpu.VMEM((tm, tn), F32)] if nk > 1 else [])
        + [pltpu.SemaphoreType.DMA((3,)), pltpu.SemaphoreType.DMA((3,))] * n_side,
        compiler_params=_cparams(("arbitrary",) * 3 if n_side else ("parallel", "parallel", "arbitrary")),
    )(a, b, *extras, *side_in)
    return outs


ROW_TILE = 512


def _rms_fwd(x, g, name):
    s, d = x.shape

    def body(x_ref, g_ref, h_ref):
        xv = x_ref[...]
        r = lax.rsqrt(jnp.mean(xv * xv, axis=-1, keepdims=True) + RMS_EPS)
        h_ref[...] = (xv * r * g_ref[...]).astype(BF16)

    return pl.pallas_call(
        body,
        name=name,
        grid=(s // ROW_TILE,),
        in_specs=[pl.BlockSpec((ROW_TILE, d), lambda i: (i, 0)), pl.BlockSpec((1, d), lambda i: (0, 0))],
        out_specs=pl.BlockSpec((ROW_TILE, d), lambda i: (i, 0)),
        out_shape=jax.ShapeDtypeStruct((s, d), BF16),
        compiler_params=_cparams(("parallel",)),
    )(x, g)


def _rms_bwd(dh, x, g, dres, name):
    s, d = x.shape

    def body(dh_ref, x_ref, g_ref, dres_ref, dx_ref, dg_ref):
        i = pl.program_id(0)
        xv = x_ref[...]
        r = lax.rsqrt(jnp.mean(xv * xv, axis=-1, keepdims=True) + RMS_EPS)
        xh = xv * r
        dhv = dh_ref[...]
        dxh = dhv * g_ref[...]
        dx = r * (dxh - xh * jnp.mean(dxh * xh, axis=-1, keepdims=True))
        dx_ref[...] = dres_ref[...] + dx
        part = jnp.sum((dhv * xh).reshape(ROW_TILE // 8, 8, d), axis=0)

        @pl.when(i == 0)
        def _():
            dg_ref[...] = part

        @pl.when(i > 0)
        def _():
            dg_ref[...] += part

    row = pl.BlockSpec((ROW_TILE, d), lambda i: (i, 0))
    return pl.pallas_call(
        body,
        name=name,
        grid=(s // ROW_TILE,),
        in_specs=[row, row, pl.BlockSpec((1, d), lambda i: (0, 0)), row],
        out_specs=[row, pl.BlockSpec((8, d), lambda i: (0, 0))],
        out_shape=[jax.ShapeDtypeStruct((s, d), F32), jax.ShapeDtypeStruct((8, d), F32)],
        compiler_params=_cparams(("arbitrary",)),
    )(dh, x, g, dres)


def _loss_head(x2, g, target):
    s, d = x2.shape

    def body(x_ref, g_ref, t_ref, dx_ref, dg_ref, loss_ref):
        i = pl.program_id(0)
        xv = x_ref[...]
        r = lax.rsqrt(jnp.mean(xv * xv, axis=-1, keepdims=True) + RMS_EPS)
        xh = xv * r
        gv = g_ref[...]
        err = xh * gv - t_ref[...]
        dy = err * (1.0 / d)
        dxh = dy * gv
        dx_ref[...] = r * (dxh - xh * jnp.mean(dxh * xh, axis=-1, keepdims=True))
        part_g = jnp.sum((dy * xh).reshape(ROW_TILE // 8, 8, d), axis=0)
        part_l = (0.5 / d) * jnp.sum((err * err).reshape(ROW_TILE // 8, 8, d), axis=0)

        @pl.when(i == 0)
        def _():
            dg_ref[...] = part_g
            loss_ref[...] = part_l

        @pl.when(i > 0)
        def _():
            dg_ref[...] += part_g
            loss_ref[...] += part_l

    row = pl.BlockSpec((ROW_TILE, d), lambda i: (i, 0))
    acc = pl.BlockSpec((8, d), lambda i: (0, 0))
    return pl.pallas_call(
        body,
        name="loss_head",
        grid=(s // ROW_TILE,),
        in_specs=[row, pl.BlockSpec((1, d), lambda i: (0, 0)), row],
        out_specs=[row, acc, acc],
        out_shape=[jax.ShapeDtypeStruct((s, d), F32), jax.ShapeDtypeStruct((8, d), F32),
                   jax.ShapeDtypeStruct((8, d), F32)],
        compiler_params=_cparams(("arbitrary",)),
    )(x2, g, target)


def _alibi_slopes():
    return np.exp2(np.float32(-8.0) * np.arange(1, N_DIL_HEADS + 1, dtype=np.float32) / np.float32(N_DIL_HEADS))


def _head_lane_mask(h, rows):
    lane = lax.broadcasted_iota(jnp.int32, (rows, DIL_W), 1)
    return (lane >= h * HEAD_DIM) & (lane < (h + 1) * HEAD_DIM)


def _band_terms(dil, has_prev):
    qi = lax.broadcasted_iota(jnp.int32, (BLOCK, 2 * BLOCK), 0)
    kj = lax.broadcasted_iota(jnp.int32, (BLOCK, 2 * BLOCK), 1)
    steps = qi + BLOCK - kj
    valid = (steps >= 0) & (steps <= BLOCK) & ((kj >= BLOCK) | has_prev)
    return valid, steps.astype(F32) * float(dil)


def _dil_fwd(qkv_g, group):
    _, dil = DIL_GROUPS[group]
    s = qkv_g.shape[0]
    sub = s // dil
    nb = sub // BLOCK
    view = qkv_g.reshape(sub, dil * 3 * DIL_W)
    slopes = _alibi_slopes()[group * DIL_HEADS:(group + 1) * DIL_HEADS]

    def col(which):
        return lambda r, n: (n, r * 3 + which)

    def col_prev(which):
        return lambda r, n: (jnp.maximum(n - 1, 0), r * 3 + which)

    def body(q_ref, kc_ref, kp_ref, vc_ref, vp_ref, o_ref, lse_ref):
        n = pl.program_id(1)
        valid, dist = _band_terms(dil, n > 0)
        q = q_ref[...]
        k2 = jnp.concatenate([kp_ref[...], kc_ref[...]], axis=0)
        v2 = jnp.concatenate([vp_ref[...], vc_ref[...]], axis=0)
        masks = [_head_lane_mask(h, BLOCK) for h in range(DIL_HEADS)]
        logits = [_dot_nt(jnp.where(masks[h], q, jnp.zeros_like(q)), k2) for h in range(DIL_HEADS)]
        ps, lses = [], []
        for h in range(DIL_HEADS):
            lg = jnp.where(valid, logits[h] * 0.125 - float(slopes[h]) * dist, NEG_INF)
            mx = jnp.max(lg, axis=1, keepdims=True)
            lse = mx + jnp.log(jnp.sum(jnp.exp(lg - mx), axis=1, keepdims=True))
            ps.append(jnp.exp(lg - lse).astype(BF16))
            lses.append(lse)
        o_acc = jnp.zeros((BLOCK, DIL_W), F32)
        lse_acc = jnp.zeros((BLOCK, DIL_W), F32)
        for h in range(DIL_HEADS):
            o_acc = jnp.where(masks[h], _dot_nn(ps[h], v2), o_acc)
            lse_acc = jnp.where(masks[h], lses[h], lse_acc)
        o_ref[...] = o_acc
        lse_ref[...] = lse_acc

    blk = (BLOCK, DIL_W)
    o, lse = pl.pallas_call(
        body,
        name=f"dil_fwd_g{group}",
        grid=(dil, nb),
        in_specs=[pl.BlockSpec(blk, col(0)), pl.BlockSpec(blk, col(1)), pl.BlockSpec(blk, col_prev(1)),
                  pl.BlockSpec(blk, col(2)), pl.BlockSpec(blk, col_prev(2))],
        out_specs=[pl.BlockSpec(blk, lambda r, n: (n, r))] * 2,
        out_shape=[jax.ShapeDtypeStruct((sub, dil * DIL_W), F32)] * 2,
        compiler_params=_cparams(("parallel", "parallel")),
    )(view, view, view, view, view)
    return o.reshape(s, DIL_W), lse.reshape(s, DIL_W)


def _dil_bwd(qkv, do, lse, cterm, group):
    _, dil = DIL_GROUPS[group]
    s = qkv.shape[0]
    sub = s // dil
    nb = sub // BLOCK
    view = qkv.reshape(sub, dil * 3 * DIL_W)
    slopes = _alibi_slopes()[group * DIL_HEADS:(group + 1) * DIL_HEADS]
    do_v, lse_v, c_v = (t.reshape(sub, dil * DIL_W) for t in (do, lse, cterm))

    def col(which, shift):
        if shift == 0:
            return lambda r, n: (n, r * 3 + which)
        if shift < 0:
            return lambda r, n: (jnp.maximum(n - 1, 0), r * 3 + which)
        return lambda r, n: (jnp.minimum(n + 1, nb - 1), r * 3 + which)

    def own(shift):
        if shift == 0:
            return lambda r, n: (n, r)
        return lambda r, n: (jnp.minimum(n + 1, nb - 1), r)

    def body(q_ref, qn_ref, kc_ref, kp_ref, vc_ref, vp_ref, do_ref, don_ref, lse_ref, lsen_ref, c_ref, cn_ref,
             dq_ref, dk_ref, dv_ref):
        n = pl.program_id(1)
        valid, dist = _band_terms(dil, n > 0)
        valid_n = _band_terms(dil, True)[0][:, :BLOCK] & (n < nb - 1)
        dist_n = dist[:, :BLOCK]
        q, qn = q_ref[...], qn_ref[...]
        kc, vc = kc_ref[...], vc_ref[...]
        k2 = jnp.concatenate([kp_ref[...], kc], axis=0)
        v2 = jnp.concatenate([vp_ref[...], vc], axis=0)
        dov, donv = do_ref[...], don_ref[...]
        lsev, lsenv, cv, cnv = lse_ref[...], lsen_ref[...], c_ref[...], cn_ref[...]
        masks = [_head_lane_mask(h, BLOCK) for h in range(DIL_HEADS)]

        def head_col(t, hm):
            return jnp.max(jnp.where(hm, t, NEG_INF), axis=1, keepdims=True)

        qhs = [jnp.where(hm, q, jnp.zeros_like(q)) for hm in masks]
        qnhs = [jnp.where(hm, qn, jnp.zeros_like(qn)) for hm in masks]
        dohs = [jnp.where(hm, dov, 0.0).astype(BF16) for hm in masks]
        donhs = [jnp.where(hm, donv, 0.0).astype(BF16) for hm in masks]
        logit = [_dot_nt(qhs[h], k2) for h in range(DIL_HEADS)]
        dp = [_dot_nt(dohs[h], v2) for h in range(DIL_HEADS)]
        logit_n = [_dot_nt(qnhs[h], kc) for h in range(DIL_HEADS)]
        dp_n = [_dot_nt(donhs[h], vc) for h in range(DIL_HEADS)]
        p16, dlog, pn16, dlog_n = [], [], [], []
        for h in range(DIL_HEADS):
            hm, slope = masks[h], float(slopes[h])
            p = jnp.where(valid, jnp.exp(logit[h] * 0.125 - slope * dist - head_col(lsev, hm)), 0.0)
            dlog.append((p * (dp[h] + head_col(cv, hm)) * 0.125).astype(BF16))
            p16.append(p.astype(BF16))
            pn = jnp.where(valid_n, jnp.exp(logit_n[h] * 0.125 - slope * dist_n - head_col(lsenv, hm)), 0.0)
            dlog_n.append((pn * (dp_n[h] + head_col(cnv, hm)) * 0.125).astype(BF16))
            pn16.append(pn.astype(BF16))
        dq_acc = jnp.zeros((BLOCK, DIL_W), F32)
        dk_acc = jnp.zeros((BLOCK, DIL_W), F32)
        dv_acc = jnp.zeros((BLOCK, DIL_W), F32)
        for h in range(DIL_HEADS):
            dq_acc = jnp.where(masks[h], _dot_nn(dlog[h], k2), dq_acc)
            dk_acc += _dot_tn(dlog[h][:, BLOCK:], qhs[h]) + _dot_tn(dlog_n[h], qnhs[h])
            dv_acc += _dot_tn(p16[h][:, BLOCK:], dohs[h]) + _dot_tn(pn16[h], donhs[h])
        dq_ref[...] = dq_acc.astype(BF16)
        dk_ref[...] = dk_acc.astype(BF16)
        dv_ref[...] = dv_acc.astype(BF16)

    blk = (BLOCK, DIL_W)
    outs = pl.pallas_call(
        body,
        name=f"dil_bwd_g{group}",
        grid=(dil, nb),
        in_specs=[pl.BlockSpec(blk, col(0, 0)), pl.BlockSpec(blk, col(0, 1)),
                  pl.BlockSpec(blk, col(1, 0)), pl.BlockSpec(blk, col(1, -1)),
                  pl.BlockSpec(blk, col(2, 0)), pl.BlockSpec(blk, col(2, -1)),
                  pl.BlockSpec(blk, own(0)), pl.BlockSpec(blk, own(1)),
                  pl.BlockSpec(blk, own(0)), pl.BlockSpec(blk, own(1)),
                  pl.BlockSpec(blk, own(0)), pl.BlockSpec(blk, own(1))],
        out_specs=[pl.BlockSpec(blk, lambda r, n: (n, r))] * 3,
        out_shape=[jax.ShapeDtypeStruct((sub, dil * DIL_W), BF16)] * 3,
        compiler_params=_cparams(("parallel", "parallel")),
    )(view, view, view, view, view, view, do_v, do_v, lse_v, lse_v, c_v, c_v)
    return tuple(t.reshape(s, DIL_W) for t in outs)


SB_PAIRS = SB_HEADS // 2
SB_COL0 = 0
LOG2E = 1.4426950408889634


SB_EXP_CLAMP = 64.0


def _sb_softplus2(zs):
    t = 1.0 + jnp.exp2(jnp.minimum(zs, SB_EXP_CLAMP))
    return jnp.maximum(jnp.log(t) * LOG2E, zs)


def _sb_consts(nkb):
    row = lax.broadcasted_iota(jnp.int32, (SB_BQ, SB_BK), 0)
    colk = lax.broadcasted_iota(jnp.int32, (SB_BQ, SB_BK), 1)
    rr = lax.broadcasted_iota(jnp.int32, (SB_BK, SB_BK), 0)
    cc = lax.broadcasted_iota(jnp.int32, (SB_BK, SB_BK), 1)
    lane = lax.broadcasted_iota(jnp.int32, (SB_BQ, 128), 1)
    assert 2 * nkb <= 128
    return colk < row, rr, cc, lane < HEAD_DIM, lane


def _split_heads(t):
    first = lax.broadcasted_iota(jnp.int32, t.shape, 1) < HEAD_DIM
    zero = jnp.zeros_like(t)
    return jnp.where(first, t, zero), jnp.where(first, zero, t)


def _sb_fwd(qkv, shard_pack):
    s = qkv.shape[0]
    nq, nkb = s // SB_BQ, s // SB_BK
    zscale = LOG2E / math.sqrt(HEAD_DIM)
    r_pack, w_pack = shard_pack.shape

    def body(q_ref, k_ref, v_ref, pack_ref, o_ref, a_row, others_ref, zs_scr, a_scr, acc_scr, cl_scr,
             send_sems, recv_sems):
        i = pl.program_id(1)
        pair = pl.program_id(0)
        gather = (pack_ref, others_ref, send_sems, recv_sems)

        @pl.when((pair == 0) & (i == 0))
        def _():
            _gather_start(*gather)

        @pl.when((pair == 1) & (i == 0))
        def _():
            _gather_pass_on(*gather)

        @pl.when((pair == SB_PAIRS - 1) & (i == nq - 1))
        def _():
            _gather_finish(*gather)

        causal, rr, cc, _, _ = _sb_consts(nkb)
        later = (rr > cc).astype(BF16)
        qh = _split_heads(q_ref[...])

        def rows(j):
            return pl.ds(pl.multiple_of(j * SB_BK, SB_BK), SB_BK)

        def scores_to(slot, j):
            kb = k_ref[rows(j), :]
            for hh in range(2):
                zs_scr[slot, hh] = _dot_nt(qh[hh], kb) * zscale

        def weights(slot, j, masked):
            xs, sums, sufs = [], [], []
            for hh in range(2):
                zs = zs_scr[slot, hh]
                sp = _sb_softplus2(zs)
                if masked:
                    sp = jnp.where(causal, sp, 0.0)
                xs.append(zs - sp)
                sums.append(jnp.sum(sp, axis=1, keepdims=True))
                sufs.append(_dot_f32_by_01(sp, later, 2))
            for hh in range(2):
                cl = cl_scr[hh]
                a = jnp.exp2(xs[hh] - (sufs[hh] + jnp.concatenate([cl, cl], axis=1)))
                if masked:
                    a = jnp.where(causal, a, 0.0)
                a16 = a.astype(BF16)
                a_scr[slot, :, hh * SB_BK:(hh + 1) * SB_BK] = a16
                a_row[0, 0, j, :, hh * SB_BK:(hh + 1) * SB_BK] = a16
                cl_scr[hh] = cl + sums[hh]

        def add_av(slot, j):
            v0, v1 = _split_heads(v_ref[rows(j), :])
            acc_scr[...] += _dot_nn(a_scr[slot], jnp.concatenate([v0, v1], axis=0))

        acc_scr[...] = jnp.zeros_like(acc_scr)
        cl_scr[...] = jnp.zeros_like(cl_scr)
        scores_to(0, i)
        scores_to(1, jnp.maximum(i - 1, 0))
        weights(0, i, True)

        def step(j, prev, cur):
            scores_to(prev, jnp.maximum(j - 1, 0))
            add_av(prev, j + 1)
            weights(cur, j, False)

        def two_steps(u, _):
            j = i - 1 - 2 * u
            step(j, 0, 1)
            step(j - 1, 1, 0)
            return 0

        lax.fori_loop(0, i // 2, two_steps, 0)

        @pl.when(i % 2 == 1)
        def _():
            step(0, 0, 1)
            add_av(1, 0)

        @pl.when(i % 2 == 0)
        def _():
            add_av(0, 0)

        o_ref[...] = acc_scr[...]

    def full(which):
        return pl.BlockSpec((s, 128), lambda p, i: (0, SB_COL0 + 4 * which + p))

    return pl.pallas_call(
        body,
        name="sb_fwd",
        grid=(SB_PAIRS, nq),
        in_specs=[pl.BlockSpec((SB_BQ, 128), lambda p, i: (i, SB_COL0 + p)), full(1), full(2), ANY],
        out_specs=[pl.BlockSpec((SB_BQ, 128), lambda p, i: (i, p)),
                   pl.BlockSpec((1, 1, nkb, SB_BQ, 2 * SB_BK), lambda p, i: (p, i, 0, 0, 0)), ANY],
        out_shape=[jax.ShapeDtypeStruct((s, SB_W), F32),
                   jax.ShapeDtypeStruct((SB_PAIRS, nq, nkb, SB_BQ, 2 * SB_BK), BF16),
                   jax.ShapeDtypeStruct((N_CHIPS, 2, r_pack // 2, w_pack), shard_pack.dtype)],
        scratch_shapes=[pltpu.VMEM((2, 2, SB_BQ, SB_BK), F32), pltpu.VMEM((2, SB_BQ, 2 * SB_BK), BF16),
                        pltpu.VMEM((SB_BQ, 128), F32), pltpu.VMEM((2, SB_BQ, 128), F32),
                        pltpu.SemaphoreType.DMA((6,)), pltpu.SemaphoreType.DMA((6,))],
        compiler_params=_cparams(("arbitrary", "arbitrary")),
    )(qkv, qkv, qkv, shard_pack.reshape(2, r_pack // 2, w_pack))


def _sb_bwd(qkv, do, a_hbm, chip_sums):
    s = qkv.shape[0]
    nq, nkb = s // SB_BQ, s // SB_BK
    scale = 1.0 / math.sqrt(HEAD_DIM)
    zscale = LOG2E * scale

    def body(q_ref, k_ref, v_ref, do_ref, a_row, sums_ref, dq_ref, dk_ref, dv_ref, got_ref,
             zs_scr, da_scr, dz_scr, a_scr, cg_scr, send_sems, recv_sems):
        i = pl.program_id(1)
        pair = pl.program_id(0)
        first_step = (pair == 0) & (i == 0)
        last_step = (pair == SB_PAIRS - 1) & (i == nq - 1)

        @pl.when(first_step)
        def _():
            _exchange_start(sums_ref, got_ref, send_sems, recv_sems)

        @pl.when(i == 0)
        def _():
            dk_ref[...] = jnp.zeros_like(dk_ref)
            dv_ref[...] = jnp.zeros_like(dv_ref)

        causal, rr, cc, first, _ = _sb_consts(nkb)
        earlier = (rr < cc).astype(BF16)
        q2 = q_ref[...]
        qh = _split_heads(q2)
        do2 = do_ref[...].astype(BF16)
        doh = _split_heads(do2)

        def rows(j):
            return pl.ds(pl.multiple_of(j * SB_BK, SB_BK), SB_BK)

        def products_to(slot, j):
            kb, vb = k_ref[rows(j), :], v_ref[rows(j), :]
            for hh in range(2):
                zs_scr[slot, hh] = _dot_nt(qh[hh], kb) * zscale
                da_scr[slot, hh] = _dot_nt(doh[hh], vb)

        def by_head(t):
            return jnp.where(first, t[:SB_BK], t[SB_BK:])

        def apply(slot, j):
            k0, k1 = _split_heads(k_ref[rows(j), :])
            dq_ref[...] += _dot_nn(dz_scr[slot], jnp.concatenate([k0, k1], axis=0)) * scale
            dk_ref[rows(j), :] += by_head(_dot_tn(dz_scr[slot], q2)) * scale
            dv_ref[rows(j), :] += by_head(_dot_tn(a_scr[slot], do2))

        def grads(slot, j, masked):
            gs, gpres = [], []
            for hh in range(2):
                a16 = a_row[0, 0, j, :, hh * SB_BK:(hh + 1) * SB_BK]
                a_scr[slot, :, hh * SB_BK:(hh + 1) * SB_BK] = a16
                g = a16.astype(F32) * da_scr[slot, hh]
                gs.append(g)
                gpres.append(_dot_f32_by_01(g, earlier, 1))
            sigs = []
            for hh in range(2):
                zs = zs_scr[slot, hh]
                sigs.append(jnp.exp2(zs - _sb_softplus2(zs)))
            for hh in range(2):
                cg = cg_scr[hh]
                dz = gs[hh] - (gs[hh] + (gpres[hh] + jnp.concatenate([cg, cg], axis=1))) * sigs[hh]
                if masked:
                    dz = jnp.where(causal, dz, 0.0)
                dz_scr[slot, :, hh * SB_BK:(hh + 1) * SB_BK] = dz.astype(BF16)
                cg_scr[hh] = cg + jnp.sum(gs[hh], axis=1, keepdims=True)

        dq_ref[...] = jnp.zeros_like(dq_ref)
        cg_scr[...] = jnp.zeros_like(cg_scr)
        dz_scr[1] = jnp.zeros((SB_BQ, 2 * SB_BK), BF16)
        a_scr[1] = jnp.zeros((SB_BQ, 2 * SB_BK), BF16)
        products_to(0, 0)

        def step(j, cur, nxt):
            products_to(nxt, j + 1)
            apply(nxt, jnp.maximum(j - 1, 0))
            grads(cur, j, False)

        def two_steps(u, _):
            step(2 * u, 0, 1)
            step(2 * u + 1, 1, 0)
            return 0

        lax.fori_loop(0, i // 2, two_steps, 0)

        def last(cur, nxt):
            apply(nxt, jnp.maximum(i - 1, 0))
            grads(cur, i, True)
            apply(cur, i)

        @pl.when(i % 2 == 1)
        def _():
            step(i - 1, 0, 1)
            last(1, 0)

        @pl.when(i % 2 == 0)
        def _():
            last(0, 1)

        @pl.when(last_step)
        def _():
            _exchange_wait(sums_ref, got_ref, send_sems, recv_sems)

    def full(which):
        return pl.BlockSpec((s, 128), lambda p, i: (0, SB_COL0 + 4 * which + p))

    qblk = pl.BlockSpec((SB_BQ, 128), lambda p, i: (i, p))
    acc = pl.BlockSpec((s, 128), lambda p, i: (0, p))
    return pl.pallas_call(
        body,
        name="sb_bwd",
        grid=(SB_PAIRS, nq),
        in_specs=[pl.BlockSpec((SB_BQ, 128), lambda p, i: (i, SB_COL0 + p)), full(1), full(2), qblk,
                  pl.BlockSpec((1, 1, nkb, SB_BQ, 2 * SB_BK), lambda p, i: (p, i, 0, 0, 0)), ANY],
        out_specs=[qblk, acc, acc, ANY],
        out_shape=[jax.ShapeDtypeStruct((s, SB_W), F32)] * 3 + [jax.ShapeDtypeStruct(chip_sums.shape, chip_sums.dtype)],
        scratch_shapes=[pltpu.VMEM((2, 2, SB_BQ, SB_BK), F32), pltpu.VMEM((2, 2, SB_BQ, SB_BK), F32),
                        pltpu.VMEM((2, SB_BQ, 2 * SB_BK), BF16), pltpu.VMEM((2, SB_BQ, 2 * SB_BK), BF16),
                        pltpu.VMEM((2, SB_BQ, 128), F32),
                        pltpu.SemaphoreType.DMA((3,)), pltpu.SemaphoreType.DMA((3,))],
        compiler_params=_cparams(("arbitrary", "arbitrary")),
    )(qkv, qkv, qkv, do, a_hbm, chip_sums)


MERGE_TILE = 256


def _group_mix(lses):
    mx = jnp.maximum(jnp.maximum(lses[0], lses[1]), lses[2])
    es = [jnp.exp(t - mx) for t in lses]
    den = es[0] + es[1] + es[2]
    return [e / den for e in es]


def _merge_fwd(o_groups, lse_groups, o_sb, gl, b_gate, w_up_dil, w_up_sb):
    s = gl.shape[0]
    t = MERGE_TILE

    def body(o0, o1, o2, l0, l1, l2, ob_ref, gl_ref, bg_ref, wd_ref, ws_ref, merged_ref, oa_ref):
        w = _group_mix([l0[...], l1[...], l2[...]])
        oa = (w[0] * o0[...] + w[1] * o1[...] + w[2] * o2[...]).astype(BF16)
        ua = _dot_nn(oa, wd_ref[...])
        ub = _dot_nn(ob_ref[...].astype(BF16), ws_ref[...])
        gate = jax.nn.sigmoid(gl_ref[...] + bg_ref[...])
        merged_ref[...] = (gate[:, :D_MODEL] * ua + gate[:, D_MODEL:] * ub).astype(BF16)
        oa_ref[...] = oa

    dil = pl.BlockSpec((t, DIL_W), lambda i: (i, 0))
    const = lambda shape: pl.BlockSpec(shape, lambda i: (0, 0))
    return pl.pallas_call(
        body,
        name="merge_fwd",
        grid=(s // t,),
        in_specs=[dil] * 6 + [pl.BlockSpec((t, SB_W), lambda i: (i, 0)), pl.BlockSpec((t, GATE_W), lambda i: (i, 0)),
                              const((1, GATE_W)), const((DIL_W, D_MODEL)), const((SB_W, D_MODEL))],
        out_specs=[pl.BlockSpec((t, D_MODEL), lambda i: (i, 0)), dil],
        out_shape=[jax.ShapeDtypeStruct((s, D_MODEL), BF16), jax.ShapeDtypeStruct((s, DIL_W), BF16)],
        compiler_params=_cparams(("parallel",)),
    )(*o_groups, *lse_groups, o_sb, gl, b_gate, w_up_dil, w_up_sb)


def _merge_bwd(dmerged, o_groups, lse_groups, o_sb, gl, b_gate, w_up_dil, w_up_sb, swap):
    s = gl.shape[0]
    t = MERGE_TILE
    n_chunks, r_swap, w_swap = swap.shape
    swap = swap.reshape(n_chunks, 2, r_swap // 2, w_swap)

    def body(dm_ref, o0, o1, o2, l0, l1, l2, ob_ref, gl_ref, bg_ref, wd_ref, ws_ref, swap_ref,
             dua_ref, dub_ref, dgl_ref, dbg_ref, dosb_ref, d0, d1, d2, c0, c1, c2, got_ref, send_sem, recv_sem):
        i = pl.program_id(0)

        @pl.when(i == 0)
        def _():
            _swap_copy(swap_ref, got_ref, send_sem, recv_sem).start()

        @pl.when(i == pl.num_programs(0) - 1)
        def _():
            _swap_copy(swap_ref, got_ref, send_sem, recv_sem).wait()

        og = [o0[...], o1[...], o2[...]]
        w = _group_mix([l0[...], l1[...], l2[...]])
        oa = (w[0] * og[0] + w[1] * og[1] + w[2] * og[2]).astype(BF16)
        ua = _dot_nn(oa, wd_ref[...])
        ub = _dot_nn(ob_ref[...].astype(BF16), ws_ref[...])
        gate = jax.nn.sigmoid(gl_ref[...] + bg_ref[...])
        ga, gb = gate[:, :D_MODEL], gate[:, D_MODEL:]
        dm = dm_ref[...]
        dua = (dm * ga).astype(BF16)
        dub = (dm * gb).astype(BF16)
        dua_ref[...] = dua
        dub_ref[...] = dub
        dgl_a = dm * ua * ga * (1.0 - ga)
        dgl_b = dm * ub * gb * (1.0 - gb)
        dgl_ref[:, :D_MODEL] = dgl_a.astype(BF16)
        dgl_ref[:, D_MODEL:] = dgl_b.astype(BF16)
        part = jnp.concatenate([jnp.sum(dgl_a.reshape(t // 8, 8, D_MODEL), axis=0),
                                jnp.sum(dgl_b.reshape(t // 8, 8, D_MODEL), axis=0)], axis=1)

        @pl.when(i == 0)
        def _():
            dbg_ref[...] = part

        @pl.when(i > 0)
        def _():
            dbg_ref[...] += part

        dosb_ref[...] = _dot_nt(dub, ws_ref[...])
        doa = _dot_nt(dua, wd_ref[...])
        rr = lax.broadcasted_iota(jnp.int32, (DIL_W, DIL_W), 0) // HEAD_DIM
        cc = lax.broadcasted_iota(jnp.int32, (DIL_W, DIL_W), 1) // HEAD_DIM
        same_head = (rr == cc).astype(BF16)
        dw = [_dot_f32_by_01(doa * og[g], same_head) for g in range(3)]
        mean_dw = w[0] * dw[0] + w[1] * dw[1] + w[2] * dw[2]
        for g, (d_ref, c_ref) in enumerate(((d0, c0), (d1, c1), (d2, c2))):
            d_ref[...] = w[g] * doa
            c_ref[...] = -w[g] * mean_dw

    dil = pl.BlockSpec((t, DIL_W), lambda i: (i, 0))
    wide = pl.BlockSpec((t, D_MODEL), lambda i: (i, 0))
    gate2 = pl.BlockSpec((t, GATE_W), lambda i: (i, 0))
    sbw = pl.BlockSpec((t, SB_W), lambda i: (i, 0))
    const = lambda shape: pl.BlockSpec(shape, lambda i: (0, 0))
    return pl.pallas_call(
        body,
        name="merge_bwd",
        grid=(s // t,),
        in_specs=[wide] + [dil] * 6 + [sbw, gate2, const((1, GATE_W)), const((DIL_W, D_MODEL)), const((SB_W, D_MODEL)),
                                       ANY],
        out_specs=[wide, wide, gate2, const((8, GATE_W)), sbw] + [dil] * 6 + [ANY],
        out_shape=[jax.ShapeDtypeStruct((s, D_MODEL), BF16), jax.ShapeDtypeStruct((s, D_MODEL), BF16),
                   jax.ShapeDtypeStruct((s, GATE_W), BF16), jax.ShapeDtypeStruct((8, GATE_W), F32),
                   jax.ShapeDtypeStruct((s, SB_W), F32)] + [jax.ShapeDtypeStruct((s, DIL_W), F32)] * 6
        + [jax.ShapeDtypeStruct((n_chunks, r_swap // 2, w_swap), swap.dtype)],
        scratch_shapes=[pltpu.SemaphoreType.DMA, pltpu.SemaphoreType.DMA],
        compiler_params=_cparams(("arbitrary",)),
    )(dmerged, *o_groups, *lse_groups, o_sb, gl, b_gate, w_up_dil, w_up_sb, swap)


ANY = pl.BlockSpec(memory_space=pl.ANY)


def _place():
    x, y, c = lax.axis_index("x"), lax.axis_index("y"), lax.axis_index("c")
    other_chips = [(1 - x, y), (x, 1 - y), (1 - x, 1 - y)]
    return x, y, c, other_chips


def _gather_copies(p_ref, out_ref, send_sems, recv_sems):
    x, y, c, chips = _place()
    me, sibling = 2 * x + y, (x, y, 1 - c)
    idx = [2 * chip[0] + chip[1] for chip in chips]

    def copy(k, chip_idx, core, to, src=None):
        return pltpu.make_async_remote_copy(
            src_ref=out_ref.at[chip_idx, core] if src is None else src, dst_ref=out_ref.at[chip_idx, core],
            send_sem=send_sems.at[k], recv_sem=recv_sems.at[k], device_id=to, device_id_type=MESH)

    first = lambda j: copy(j, me, c, (*chips[j], c), src=p_ref.at[c])
    landed = lambda j: copy(j, idx[j], c, (x, y, c))
    passed = lambda j: copy(3 + j, idx[j], c, sibling)
    handed = lambda j: copy(3 + j, idx[j], 1 - c, (x, y, c))
    return first, landed, passed, handed


def _gather_start(*refs):
    first = _gather_copies(*refs)[0]
    for j in range(3):
        first(j).start()


def _gather_pass_on(*refs):
    _, landed, passed, _ = _gather_copies(*refs)
    for j in range(3):
        landed(j).wait_recv()
        passed(j).start()


def _gather_finish(*refs):
    first, _, passed, handed = _gather_copies(*refs)
    for j in range(3):
        handed(j).wait_recv()
    for j in range(3):
        first(j).wait_send()
        passed(j).wait_send()


def _fill_own_slot(others, pack):
    n, _, rh, wd = others.shape
    me = 2 * lax.axis_index("x") + lax.axis_index("y")
    mine = lax.broadcasted_iota(jnp.int32, (n, 1, 1, 1), 0) == me
    return jnp.where(mine, pack.reshape(1, 2, rh, wd), others).reshape(n, 2 * rh, wd)


def _all_gather_weights(pack):
    r, wd = pack.shape
    rh = r // 2

    def body(p_ref, out_ref, send_sems, recv_sems):
        _gather_start(p_ref, out_ref, send_sems, recv_sems)
        _gather_pass_on(p_ref, out_ref, send_sems, recv_sems)
        _gather_finish(p_ref, out_ref, send_sems, recv_sems)

    others = pl.pallas_call(
        body,
        name="all_gather_weights",
        in_specs=[ANY],
        out_specs=ANY,
        out_shape=jax.ShapeDtypeStruct((N_CHIPS, 2, rh, wd), pack.dtype),
        scratch_shapes=[pltpu.SemaphoreType.DMA((6,)), pltpu.SemaphoreType.DMA((6,))],
    )(pack.reshape(2, rh, wd))
    return _fill_own_slot(others, pack)


def _swap_copy(g_ref, out_ref, send_sem, recv_sem):
    x, y, c, _ = _place()
    return pltpu.make_async_remote_copy(
        src_ref=g_ref.at[:, 1 - c], dst_ref=out_ref,
        send_sem=send_sem, recv_sem=recv_sem, device_id=(x, y, 1 - c), device_id_type=MESH)


def _swap_halves(g):
    n, r, wd = g.shape
    rh = r // 2
    g = g.reshape(n, 2, rh, wd)

    def body(g_ref, out_ref, send_sem, recv_sem):
        cp = _swap_copy(g_ref, out_ref, send_sem, recv_sem)
        cp.start()
        cp.wait()

    return pl.pallas_call(
        body,
        name="grad_swap_halves",
        in_specs=[ANY],
        out_specs=ANY,
        out_shape=jax.ShapeDtypeStruct((n, rh, wd), g.dtype),
        scratch_shapes=[pltpu.SemaphoreType.DMA, pltpu.SemaphoreType.DMA],
    )(g)


def _add_halves(g, got, core):
    n, r, wd = g.shape
    rh = r // 2
    t = rh // 4
    nt = rh // t

    def body(c_ref, a_ref, b_ref, o_ref):
        o_ref[...] = (a_ref[0] + b_ref[...]).astype(BF16)

    grid_spec = pltpu.PrefetchScalarGridSpec(
        num_scalar_prefetch=1,
        grid=(n, nt),
        in_specs=[pl.BlockSpec((1, 1, t, wd), lambda s, i, c: (s, c[0], i, 0)),
                  pl.BlockSpec((1, t, wd), lambda s, i, c: (s, i, 0))],
        out_specs=pl.BlockSpec((1, t, wd), lambda s, i, c: (s, i, 0)),
    )
    return pl.pallas_call(
        body,
        name="grad_add_halves",
        grid_spec=grid_spec,
        out_shape=jax.ShapeDtypeStruct((n, rh, wd), BF16),
        compiler_params=_cparams(("parallel", "parallel")),
    )(core, g.reshape(n, 2, rh, wd), got)


def _exchange_copies(h_ref, out_ref, send_sems, recv_sems):
    x, y, c, chips = _place()
    me = 2 * x + y

    def copy(j, slot):
        them = 2 * chips[j][0] + chips[j][1]
        return pltpu.make_async_remote_copy(
            src_ref=h_ref.at[them], dst_ref=out_ref.at[me if slot == "mine" else them],
            send_sem=send_sems.at[j], recv_sem=recv_sems.at[j], device_id=(*chips[j], c), device_id_type=MESH)

    return (lambda j: copy(j, "mine")), (lambda j: copy(j, "theirs"))


def _exchange_start(h_ref, out_ref, send_sems, recv_sems):
    send = _exchange_copies(h_ref, out_ref, send_sems, recv_sems)[0]
    for j in range(3):
        send(j).start()


def _exchange_wait(h_ref, out_ref, send_sems, recv_sems):
    send, arrival = _exchange_copies(h_ref, out_ref, send_sems, recv_sems)
    for j in range(3):
        arrival(j).wait_recv()
    for j in range(3):
        send(j).wait_send()


def _exchange_chunks(h):
    n, rh, wd = h.shape

    def body(h_ref, out_ref, send_sems, recv_sems):
        _exchange_start(h_ref, out_ref, send_sems, recv_sems)
        _exchange_wait(h_ref, out_ref, send_sems, recv_sems)

    return pl.pallas_call(
        body,
        name="grad_exchange_chunks",
        in_specs=[ANY],
        out_specs=ANY,
        out_shape=jax.ShapeDtypeStruct((n, rh, wd), h.dtype),
        scratch_shapes=[pltpu.SemaphoreType.DMA((3,)), pltpu.SemaphoreType.DMA((3,))],
    )(h)


def _sum_chips(b, h, chip):
    n, rh, wd = b.shape
    t = rh // 4

    def body(chip_ref, b_ref, own_ref, o_ref):
        own = own_ref[0]
        s0, s1, s2, s3 = (jnp.where(chip_ref[0] == k, own, b_ref[k]).astype(F32) for k in range(n))
        o_ref[...] = ((s0 + s1) + s2) + s3

    grid_spec = pltpu.PrefetchScalarGridSpec(
        num_scalar_prefetch=1,
        grid=(rh // t,),
        in_specs=[pl.BlockSpec((n, t, wd), lambda i, chip: (0, i, 0)),
                  pl.BlockSpec((1, t, wd), lambda i, chip: (chip[0], i, 0))],
        out_specs=pl.BlockSpec((t, wd), lambda i, chip: (i, 0)),
    )
    return pl.pallas_call(
        body,
        name="grad_sum_chips",
        grid_spec=grid_spec,
        out_shape=jax.ShapeDtypeStruct((rh, wd), F32),
        compiler_params=_cparams(("parallel",)),
    )(chip, b, h)


def _join_halves(tc):
    rh, wd = tc.shape

    def body(t_ref, out_ref, send_sem, recv_sem):
        x, y, c, _ = _place()
        cp = pltpu.make_async_remote_copy(
            src_ref=t_ref, dst_ref=out_ref.at[c],
            send_sem=send_sem, recv_sem=recv_sem, device_id=(x, y, 1 - c), device_id_type=MESH)
        cp.start()
        cp.wait()

    halves = pl.pallas_call(
        body,
        name="grad_join_halves",
        in_specs=[ANY],
        out_specs=ANY,
        out_shape=jax.ShapeDtypeStruct((2, rh, wd), tc.dtype),
        scratch_shapes=[pltpu.SemaphoreType.DMA, pltpu.SemaphoreType.DMA],
    )(tc)
    return lax.dynamic_update_slice(halves, tc[None], (lax.axis_index("c"), 0, 0)).reshape(2 * rh, wd)


def _all_reduce_small(pack):
    rows, lanes = pack.shape

    def body(p_ref, out_ref, buf, send_sems, recv_sems):
        x, y, c, _ = _place()
        me = 4 * x + 2 * y + c
        buf[me] = p_ref[...]
        sends = []
        for k in range(1, N_DEV):
            peer = (x ^ (k >> 2), y ^ ((k >> 1) & 1), c ^ (k & 1))
            sends.append(pltpu.make_async_remote_copy(
                src_ref=p_ref, dst_ref=buf.at[me], send_sem=send_sems.at[k - 1], recv_sem=recv_sems.at[k - 1],
                device_id=peer, device_id_type=MESH))
        for cp in sends:
            cp.start()
        for k in range(1, N_DEV):
            pltpu.make_async_remote_copy(
                src_ref=p_ref, dst_ref=buf.at[me ^ k], send_sem=send_sems.at[k - 1], recv_sem=recv_sems.at[k - 1],
                device_id=(x, y, c), device_id_type=MESH).wait_recv()
        for cp in sends:
            cp.wait_send()
        total = buf[0]
        for d in range(1, N_DEV):
            total = total + buf[d]
        out_ref[...] = total

    vm = pl.BlockSpec(memory_space=pltpu.VMEM)
    return pl.pallas_call(
        body,
        name="all_reduce_small",
        in_specs=[vm],
        out_specs=vm,
        out_shape=jax.ShapeDtypeStruct((rows, lanes), F32),
        scratch_shapes=[pltpu.VMEM((N_DEV, rows, lanes), F32), pltpu.SemaphoreType.DMA((N_DEV - 1,)),
                        pltpu.SemaphoreType.DMA((N_DEV - 1,))],
    )(pack)


def _adamw(g, w, m, v, name):
    rows, cols = g.shape
    t = rows
    for cand in (256, 128, 64, 32, 16, 8):
        if rows % cand == 0:
            t = cand
            break

    def body(g_ref, w_ref, m_ref, v_ref, d_ref, nm_ref, nv_ref):
        gv = g_ref[...]
        mv = ADAM_B1 * m_ref[...] + (1.0 - ADAM_B1) * gv
        vv = ADAM_B2 * v_ref[...] + (1.0 - ADAM_B2) * (gv * gv)
        m_hat = mv / (1.0 - ADAM_B1 ** ADAM_STEP)
        v_hat = vv / (1.0 - ADAM_B2 ** ADAM_STEP)
        d_ref[...] = -ADAM_LR * (m_hat / (jnp.sqrt(v_hat) + ADAM_EPS) + ADAM_WD * w_ref[...])
        nm_ref[...] = mv
        nv_ref[...] = vv

    blk = pl.BlockSpec((t, cols), lambda i: (i, 0))
    return pl.pallas_call(
        body,
        name=name,
        grid=(rows // t,),
        in_specs=[blk] * 4,
        out_specs=[blk] * 3,
        out_shape=[jax.ShapeDtypeStruct((rows, cols), F32)] * 3,
        compiler_params=_cparams(("parallel",)),
    )(g, w, m, v)


PACK_W = 1024
BIG = (("w_in", (D_MODEL, IN_COLS), 1), ("w_up_dil", (DIL_W, D_MODEL), 1), ("w_up_sb", (SB_W, D_MODEL), 1),
       ("w_out", (D_MODEL, D_MODEL), 0), ("w_mlp_in", (D_MODEL, D_FF), 1), ("w_mlp_out", (D_FF, D_MODEL), 0))


def _shard_shape(shape, axis):
    return tuple(d // N_CHIPS if a == axis else d for a, d in enumerate(shape))


MIXER_GROUP, MLP_GROUP = BIG[:4], BIG[4:]


def _pack_rows(group=BIG):
    rows, at = {}, 0
    for name, shape, axis in group:
        n = math.prod(_shard_shape(shape, axis)) // PACK_W
        rows[name] = (at, n)
        at += n
    return rows, at


def _pack_shards(shards, group):
    return jnp.concatenate([shards[name].reshape(-1, PACK_W) for name, _, _ in group], axis=0)


def _unpack_full(gathered, group):
    rows, _ = _pack_rows(group)
    full = {}
    for name, shape, axis in group:
        at, n = rows[name]
        parts = gathered[:, at:at + n, :].reshape((N_CHIPS,) + _shard_shape(shape, axis))
        if axis == 0:
            full[name] = parts.reshape(shape)
        else:
            full[name] = jnp.transpose(parts, (1, 0, 2)).reshape(shape)
    return full


def _pack_full_grads(grads, group):
    chunks = []
    for name, shape, axis in group:
        g = grads[name]
        if axis == 0:
            parts = g.reshape((N_CHIPS, shape[0] // N_CHIPS, shape[1]))
        else:
            parts = jnp.transpose(g.reshape((shape[0], N_CHIPS, shape[1] // N_CHIPS)), (1, 0, 2))
        chunks.append(parts.reshape(N_CHIPS, -1, PACK_W))
    return jnp.concatenate(chunks, axis=1)


def _unpack_shard(packed, group):
    rows, _ = _pack_rows(group)
    return {name: packed[rows[name][0]:rows[name][0] + rows[name][1]].reshape(_shard_shape(shape, axis))
            for name, shape, axis in group}


def _local_step(x, target, w, mlp_shards, norm_mix_g, b_gate, norm_mlp_g, norm_final_g, core):
    w_in = w["w_in"]
    sb0 = 9 * DIL_W
    w_sb, w_gate = w_in[:, sb0:QKV_W], w_in[:, QKV_W:]
    w_dil = [jnp.concatenate([w_in[:, (3 * i + g) * DIL_W:(3 * i + g + 1) * DIL_W] for i in range(3)], axis=1)
             for g in range(3)]

    h = _rms_fwd(x, norm_mix_g, "norm_mix")
    qkv_dil = [_matmul(h, w_dil[g], mode="nn", out_dtypes=(BF16,), name=f"proj_dil_g{g}", tn=768)[0] for g in range(3)]
    (qkv_sb,) = _matmul(h, w_sb, mode="nn", out_dtypes=(BF16,), name="proj_sb", tn=768)
    (gl,) = _matmul(h, w_gate, mode="nn", out_dtypes=(F32,), name="proj_gate")
    dil = [_dil_fwd(qkv_dil[g], g) for g in range(3)]
    o_groups, lse_groups = [d[0] for d in dil], [d[1] for d in dil]
    o_sb, a_sb, mlp_others = _sb_fwd(qkv_sb, mlp_shards)
    w = {**w, **_unpack_full(_fill_own_slot(mlp_others, mlp_shards), MLP_GROUP)}
    merged, o_a = _merge_fwd(o_groups, lse_groups, o_sb, gl, b_gate, w["w_up_dil"], w["w_up_sb"])
    (x1,) = _matmul(merged, w["w_out"], mode="nn", out_dtypes=(F32,), name="out_proj",
                    extras=(x,), epilogue=lambda acc, res: (res + acc,))
    h2 = _rms_fwd(x1, norm_mlp_g, "norm_mlp")
    u, act = _matmul(h2, w["w_mlp_in"], mode="nn", out_dtypes=(F32, BF16), name="mlp_in",
                     epilogue=lambda acc: (acc, jnp.square(jnp.maximum(acc, 0.0))))
    (x2,) = _matmul(act, w["w_mlp_out"], mode="nn", out_dtypes=(F32,), name="mlp_out", tk=2048,
                    extras=(x1,), epilogue=lambda acc, res: (res + acc,))
    dx2, dg_final, loss_part = _loss_head(x2, norm_final_g.reshape(1, D_MODEL), target)

    (du,) = _matmul(dx2, w["w_mlp_out"], mode="nt", out_dtypes=(BF16,), name="mlp_out_dx",
                    extras=(u,), epilogue=lambda acc, uu: (acc * (2.0 * jnp.maximum(uu, 0.0)),))
    (g_mlp_out,) = _matmul(act, dx2, mode="tn", out_dtypes=(F32,), name="mlp_out_dw")
    (g_mlp_in,) = _matmul(h2, du, mode="tn", out_dtypes=(F32,), name="mlp_in_dw")
    (dh2,) = _matmul(du, w["w_mlp_in"], mode="nt", out_dtypes=(F32,), name="mlp_in_dx", tk=2048)
    dx1, dg_mlp = _rms_bwd(dh2, x1, norm_mlp_g, dx2, "norm_mlp_bwd")

    mlp_pack = _pack_full_grads({"w_mlp_in": g_mlp_in, "w_mlp_out": g_mlp_out}, MLP_GROUP)
    (dmerged,) = _matmul(dx1, w["w_out"], mode="nt", out_dtypes=(F32,), name="out_proj_dx")
    (g_out,) = _matmul(merged, dx1, mode="tn", out_dtypes=(F32,), name="out_proj_dw")
    mb = _merge_bwd(dmerged, o_groups, lse_groups, o_sb, gl, b_gate, w["w_up_dil"], w["w_up_sb"], mlp_pack)
    dua, dub, dgl, dbg, do_sb = mb[:5]
    do_groups, c_groups = mb[5:8], mb[8:11]
    mlp_sums = _add_halves(mlp_pack, mb[11], core)
    (g_up_dil,) = _matmul(o_a, dua, mode="tn", out_dtypes=(F32,), name="up_dil_dw")
    (g_up_sb,) = _matmul(o_sb, dub, mode="tn", out_dtypes=(F32,), name="up_sb_dw")
    dq_sb, dk_sb, dv_sb, mlp_got = _sb_bwd(qkv_sb, do_sb, a_sb, mlp_sums)
    dil_b = [_dil_bwd(qkv_dil[g], do_groups[g], lse_groups[g], c_groups[g], g) for g in range(3)]
    dproj = jnp.concatenate(
        [dil_b[g][i].astype(BF16) for i in range(3) for g in range(3)]
        + [t.astype(BF16) for t in (dq_sb, dk_sb, dv_sb)] + [dgl], axis=1)
    (g_in,) = _matmul(h, dproj, mode="tn", out_dtypes=(F32,), name="proj_dw", tm=512, tn=IN_COLS // 2)
    mixer_pack = _pack_full_grads({"w_in": g_in, "w_up_dil": g_up_dil, "w_up_sb": g_up_sb, "w_out": g_out}, MIXER_GROUP)
    mixer_sums = _add_halves(mixer_pack, _swap_halves(mixer_pack), core)
    dh, mixer_got = _matmul(dproj, w["w_in"], mode="nt", out_dtypes=(F32,), name="proj_dx", tk=IN_COLS // 2,
                            exchange=mixer_sums)
    grad_x, dg_mix = _rms_bwd(dh, x, norm_mix_g, dx1, "norm_mix_bwd")

    small = (dg_mix, dbg, dg_mlp, dg_final, loss_part)
    return grad_x, (mixer_got, mixer_sums), (mlp_got, mlp_sums), small


def kernel(x, norm_mix_g, w_in, b_gate, w_up_dil, w_up_sb, w_out, norm_mlp_g, w_mlp_in, w_mlp_out, norm_final_g, loss_target, m_norm_mix_g, m_w_in, m_b_gate, m_w_up_dil, m_w_up_sb, m_w_out, m_norm_mlp_g, m_w_mlp_in, m_w_mlp_out, m_norm_final_g, v_norm_mix_g, v_w_in, v_b_gate, v_w_up_dil, v_w_up_sb, v_w_out, v_norm_mlp_g, v_w_mlp_in, v_w_mlp_out, v_norm_final_g):
    shards = {"w_in": w_in[0], "w_up_dil": w_up_dil[0], "w_up_sb": w_up_sb[0], "w_out": w_out[0],
              "w_mlp_in": w_mlp_in[0], "w_mlp_out": w_mlp_out[0]}
    moments_m = {"w_in": m_w_in[0], "w_up_dil": m_w_up_dil[0], "w_up_sb": m_w_up_sb[0], "w_out": m_w_out[0],
                 "w_mlp_in": m_w_mlp_in[0], "w_mlp_out": m_w_mlp_out[0]}
    moments_v = {"w_in": v_w_in[0], "w_up_dil": v_w_up_dil[0], "w_up_sb": v_w_up_sb[0], "w_out": v_w_out[0],
                 "w_mlp_in": v_w_mlp_in[0], "w_mlp_out": v_w_mlp_out[0]}

    shards16 = {n: s.astype(BF16) for n, s in shards.items()}
    full = _unpack_full(_all_gather_weights(_pack_shards(shards16, MIXER_GROUP)), MIXER_GROUP)
    mlp_shards = _pack_shards(shards16, MLP_GROUP)

    core = lax.axis_index("c").astype(jnp.int32).reshape(1)
    chip = (2 * lax.axis_index("x") + lax.axis_index("y")).astype(jnp.int32).reshape(1)
    grad_x, (mixer_got, mixer_sums), (mlp_got, mlp_sums), small = _local_step(
        x[0], loss_target[0], full, mlp_shards, norm_mix_g, b_gate, norm_mlp_g, norm_final_g, core)

    reduced = _join_halves(_sum_chips(mixer_got, mixer_sums, chip))
    reduced_mlp = _join_halves(_sum_chips(mlp_got, mlp_sums, chip))
    g_shard = {**_unpack_shard(reduced, MIXER_GROUP), **_unpack_shard(reduced_mlp, MLP_GROUP)}

    dg_mix, dbg, dg_mlp, dg_final, loss_part = small
    loss_row = jnp.sum(loss_part, axis=0, keepdims=True)
    small_pack = jnp.concatenate(
        [jnp.sum(dg_mix, axis=0, keepdims=True), jnp.sum(dbg, axis=0, keepdims=True),
         jnp.sum(dg_mlp, axis=0, keepdims=True), jnp.sum(dg_final, axis=0, keepdims=True), loss_row], axis=1)
    n_small = small_pack.shape[1]
    small_sum = _all_reduce_small(small_pack.reshape(n_small // 128, 128)).reshape(1, n_small)
    g_norm_mix = small_sum[:, :D_MODEL]
    g_b_gate = small_sum[:, D_MODEL:3 * D_MODEL]
    g_norm_mlp = small_sum[:, 3 * D_MODEL:4 * D_MODEL]
    g_norm_final = small_sum[:, 4 * D_MODEL:5 * D_MODEL]
    loss = jnp.sum(small_sum[:, 5 * D_MODEL:])

    names = ["norm_mix_g", "w_in", "b_gate", "w_up_dil", "w_up_sb", "w_out", "norm_mlp_g", "w_mlp_in", "w_mlp_out",
             "norm_final_g"]
    grads = dict(g_shard)
    grads.update(norm_mix_g=g_norm_mix, b_gate=g_b_gate, norm_mlp_g=g_norm_mlp, norm_final_g=g_norm_final)
    weights = dict(shards)
    weights.update(norm_mix_g=norm_mix_g, b_gate=b_gate, norm_mlp_g=norm_mlp_g, norm_final_g=norm_final_g.reshape(1, D_MODEL))
    ms = dict(moments_m)
    ms.update(norm_mix_g=m_norm_mix_g, b_gate=m_b_gate, norm_mlp_g=m_norm_mlp_g, norm_final_g=m_norm_final_g.reshape(1, D_MODEL))
    vs = dict(moments_v)
    vs.update(norm_mix_g=v_norm_mix_g, b_gate=v_b_gate, norm_mlp_g=v_norm_mlp_g, norm_final_g=v_norm_final_g.reshape(1, D_MODEL))

    out_shapes = {"norm_mix_g": norm_mix_g.shape, "w_in": w_in.shape, "b_gate": b_gate.shape, "w_up_dil": w_up_dil.shape,
                  "w_up_sb": w_up_sb.shape, "w_out": w_out.shape, "norm_mlp_g": norm_mlp_g.shape,
                  "w_mlp_in": w_mlp_in.shape, "w_mlp_out": w_mlp_out.shape, "norm_final_g": norm_final_g.shape}
    g_out, d_out, m_out, v_out = [], [], [], []
    for n in names:
        d, nm, nv = _adamw(grads[n], weights[n], ms[n], vs[n], "adamw_" + n)
        shape = out_shapes[n]
        g_out.append(grads[n].reshape(shape))
        d_out.append(d.reshape(shape))
        m_out.append(nm.reshape(shape))
        v_out.append(nv.reshape(shape))
    return (loss, grad_x.reshape(x.shape), *g_out, *d_out, *m_out, *v_out)
```

```python
import functools
import math

import jax
import jax.numpy as jnp
import numpy as np
from jax import lax
from jax.experimental import pallas as pl
from jax.experimental.pallas import tpu as pltpu

F32 = jnp.float32
BF16 = jnp.bfloat16
MESH = pl.DeviceIdType.MESH

D_MODEL = 1024
HEAD_DIM = 64
DIL_GROUPS = ((128, 1), (512, 4), (2048, 16))
DIL_HEADS = 4
DIL_W = 256
N_DIL_HEADS = 12
SB_HEADS = 8
SB_W = SB_HEADS * HEAD_DIM
QKV_W = 3 * 3 * DIL_W + 3 * SB_W
GATE_W = 2 * D_MODEL
IN_COLS = QKV_W + GATE_W
D_FF = 4 * D_MODEL
BLOCK = 128
RMS_EPS = 1e-6
NEG_INF = -1e30
N_CHIPS = 4
N_DEV = 8

ADAM_LR = 0.001
ADAM_B1 = 0.9
ADAM_B2 = 0.999
ADAM_EPS = 1e-08
ADAM_WD = 0.01
ADAM_STEP = 10

VMEM_LIMIT = 56 * 1024 * 1024

SB_BQ = 256
SB_BK = 256


def _cparams(sem=None):
    if sem is None:
        return pltpu.CompilerParams(vmem_limit_bytes=VMEM_LIMIT)
    return pltpu.CompilerParams(dimension_semantics=sem, vmem_limit_bytes=VMEM_LIMIT)


def _dot(a, b, dims):
    return lax.dot_general(a, b, (dims, ((), ())), preferred_element_type=F32)


def _dot_nn(a, b):
    return _dot(a, b, ((1,), (0,)))


def _dot_nt(a, b):
    return _dot(a, b, ((1,), (1,)))


def _dot_tn(a, b):
    return _dot(a, b, ((0,), (0,)))


def _dot_f32_by_01(x, m01, pieces=3):
    hi = x.astype(BF16)
    if pieces == 1:
        return _dot_nn(hi, m01)
    r1 = x - hi.astype(F32)
    mid = r1.astype(BF16)
    if pieces == 2:
        return _dot_nn(hi, m01) + _dot_nn(mid, m01)
    lo = (r1 - mid.astype(F32)).astype(BF16)
    return _dot_nn(hi, m01) + _dot_nn(mid, m01) + _dot_nn(lo, m01)


def _matmul(a, b, *, mode, out_dtypes, name, tm=1024, tn=1024, tk=1024, extras=(), epilogue=None, exchange=None):
    if mode == "nn":
        (m, k), (k2, n) = a.shape, b.shape
    elif mode == "nt":
        (m, k), (n, k2) = a.shape, b.shape
    else:
        (k, m), (k2, n) = a.shape, b.shape
    assert k == k2, (a.shape, b.shape, mode)
    tm, tn, tk = min(tm, m), min(tn, n), min(tk, k)
    assert m % tm == 0 and n % tn == 0 and k % tk == 0, (m, n, k, tm, tn, tk)
    nk = k // tk
    n_out = len(out_dtypes)
    n_ex = len(extras)

    if mode == "nn":
        a_spec = pl.BlockSpec((tm, tk), lambda i, j, kk: (i, kk))
        b_spec = pl.BlockSpec((tk, tn), lambda i, j, kk: (kk, j))
        dot = _dot_nn
    elif mode == "nt":
        a_spec = pl.BlockSpec((tm, tk), lambda i, j, kk: (i, kk))
        b_spec = pl.BlockSpec((tn, tk), lambda i, j, kk: (j, kk))
        dot = _dot_nt
    else:
        a_spec = pl.BlockSpec((tk, tm), lambda i, j, kk: (kk, i))
        b_spec = pl.BlockSpec((tk, tn), lambda i, j, kk: (kk, j))
        dot = _dot_tn
    mn_spec = pl.BlockSpec((tm, tn), lambda i, j, kk: (i, j))

    n_side = 0 if exchange is None else 1
    grid = (m // tm, n // tn, nk)

    def body(*refs):
        a_ref, b_ref = refs[0], refs[1]
        ex_refs = refs[2:2 + n_ex]
        out_refs = refs[2 + n_ex + n_side:2 + n_ex + n_side + n_out]
        scratch = refs[2 + n_ex + n_side + n_out + n_side:]
        acc_ref = scratch[0] if nk > 1 else None
        if exchange is not None:
            side = (refs[2 + n_ex], refs[2 + n_ex + n_side + n_out]) + tuple(scratch[-2:])
            step = (pl.program_id(0) * grid[1] + pl.program_id(1)) * grid[2] + pl.program_id(2)

            @pl.when(step == 0)
            def _():
                _exchange_start(*side)

            @pl.when(step == grid[0] * grid[1] * grid[2] - 1)
            def _():
                _exchange_wait(*side)

        part = dot(a_ref[...].astype(BF16), b_ref[...].astype(BF16))

        def finish(acc):
            if epilogue is None:
                outs = (acc,)
            else:
                outs = epilogue(acc, *[r[...] for r in ex_refs])
            for o_ref, o in zip(out_refs, outs):
                o_ref[...] = o.astype(o_ref.dtype)

        if nk == 1:
            finish(part)
        else:
            kk = pl.program_id(2)

            @pl.when(kk == 0)
            def _():
                acc_ref[...] = part

            @pl.when(kk > 0)
            def _():
                acc_ref[...] += part

            @pl.when(kk == nk - 1)
            def _():
                finish(acc_ref[...])

    side_in = [] if exchange is None else [exchange]
    outs = pl.pallas_call(
        body,
        name=name,
        grid=grid,
        in_specs=[a_spec, b_spec] + [mn_spec] * n_ex + [ANY] * n_side,
        out_specs=[mn_spec] * n_out + [ANY] * n_side,
        out_shape=[jax.ShapeDtypeStruct((m, n), dt) for dt in out_dtypes]
        + [jax.ShapeDtypeStruct(e.shape, e.dtype) for e in side_in],
        scratch_shapes=([pltpu.VMEM((tm, tn), F32)] if nk > 1 else [])
        + [pltpu.SemaphoreType.DMA((3,)), pltpu.SemaphoreType.DMA((3,))] * n_side,
        compiler_params=_cparams(("arbitrary",) * 3 if n_side else ("parallel", "parallel", "arbitrary")),
    )(a, b, *extras, *side_in)
    return outs


ROW_TILE = 512


def _rms_fwd(x, g, name):
    s, d = x.shape

    def body(x_ref, g_ref, h_ref):
        xv = x_ref[...]
        r = lax.rsqrt(jnp.mean(xv * xv, axis=-1, keepdims=True) + RMS_EPS)
        h_ref[...] = (xv * r * g_ref[...]).astype(BF16)

    return pl.pallas_call(
        body,
        name=name,
        grid=(s // ROW_TILE,),
        in_specs=[pl.BlockSpec((ROW_TILE, d), lambda i: (i, 0)), pl.BlockSpec((1, d), lambda i: (0, 0))],
        out_specs=pl.BlockSpec((ROW_TILE, d), lambda i: (i, 0)),
        out_shape=jax.ShapeDtypeStruct((s, d), BF16),
        compiler_params=_cparams(("parallel",)),
    )(x, g)


def _rms_bwd(dh, x, g, dres, name):
    s, d = x.shape

    def body(dh_ref, x_ref, g_ref, dres_ref, dx_ref, dg_ref):
        i = pl.program_id(0)
        xv = x_ref[...]
        r = lax.rsqrt(jnp.mean(xv * xv, axis=-1, keepdims=True) + RMS_EPS)
        xh = xv * r
        dhv = dh_ref[...]
        dxh = dhv * g_ref[...]
        dx = r * (dxh - xh * jnp.mean(dxh * xh, axis=-1, keepdims=True))
        dx_ref[...] = dres_ref[...] + dx
        part = jnp.sum((dhv * xh).reshape(ROW_TILE // 8, 8, d), axis=0)

        @pl.when(i == 0)
        def _():
            dg_ref[...] = part

        @pl.when(i > 0)
        def _():
            dg_ref[...] += part

    row = pl.BlockSpec((ROW_TILE, d), lambda i: (i, 0))
    return pl.pallas_call(
        body,
        name=name,
        grid=(s // ROW_TILE,),
        in_specs=[row, row, pl.BlockSpec((1, d), lambda i: (0, 0)), row],
        out_specs=[row, pl.BlockSpec((8, d), lambda i: (0, 0))],
        out_shape=[jax.ShapeDtypeStruct((s, d), F32), jax.ShapeDtypeStruct((8, d), F32)],
        compiler_params=_cparams(("arbitrary",)),
    )(dh, x, g, dres)


def _loss_head(x2, g, target):
    s, d = x2.shape

    def body(x_ref, g_ref, t_ref, dx_ref, dg_ref, loss_ref):
        i = pl.program_id(0)
        xv = x_ref[...]
        r = lax.rsqrt(jnp.mean(xv * xv, axis=-1, keepdims=True) + RMS_EPS)
        xh = xv * r
        gv = g_ref[...]
        err = xh * gv - t_ref[...]
        dy = err * (1.0 / d)
        dxh = dy * gv
        dx_ref[...] = r * (dxh - xh * jnp.mean(dxh * xh, axis=-1, keepdims=True))
        part_g = jnp.sum((dy * xh).reshape(ROW_TILE // 8, 8, d), axis=0)
        part_l = (0.5 / d) * jnp.sum((err * err).reshape(ROW_TILE // 8, 8, d), axis=0)

        @pl.when(i == 0)
        def _():
            dg_ref[...] = part_g
            loss_ref[...] = part_l

        @pl.when(i > 0)
        def _():
            dg_ref[...] += part_g
            loss_ref[...] += part_l

    row = pl.BlockSpec((ROW_TILE, d), lambda i: (i, 0))
    acc = pl.BlockSpec((8, d), lambda i: (0, 0))
    return pl.pallas_call(
        body,
        name="loss_head",
        grid=(s // ROW_TILE,),
        in_specs=[row, pl.BlockSpec((1, d), lambda i: (0, 0)), row],
        out_specs=[row, acc, acc],
        out_shape=[jax.ShapeDtypeStruct((s, d), F32), jax.ShapeDtypeStruct((8, d), F32),
                   jax.ShapeDtypeStruct((8, d), F32)],
        compiler_params=_cparams(("arbitrary",)),
    )(x2, g, target)


def _alibi_slopes():
    return np.exp2(np.float32(-8.0) * np.arange(1, N_DIL_HEADS + 1, dtype=np.float32) / np.float32(N_DIL_HEADS))


def _head_lane_mask(h, rows):
    lane = lax.broadcasted_iota(jnp.int32, (rows, DIL_W), 1)
    return (lane >= h * HEAD_DIM) & (lane < (h + 1) * HEAD_DIM)


def _band_terms(dil, has_prev):
    qi = lax.broadcasted_iota(jnp.int32, (BLOCK, 2 * BLOCK), 0)
    kj = lax.broadcasted_iota(jnp.int32, (BLOCK, 2 * BLOCK), 1)
    steps = qi + BLOCK - kj
    valid = (steps >= 0) & (steps <= BLOCK) & ((kj >= BLOCK) | has_prev)
    return valid, steps.astype(F32) * float(dil)


def _dil_fwd(qkv_g, group):
    _, dil = DIL_GROUPS[group]
    s = qkv_g.shape[0]
    sub = s // dil
    nb = sub // BLOCK
    view = qkv_g.reshape(sub, dil * 3 * DIL_W)
    slopes = _alibi_slopes()[group * DIL_HEADS:(group + 1) * DIL_HEADS]

    def col(which):
        return lambda r, n: (n, r * 3 + which)

    def col_prev(which):
        return lambda r, n: (jnp.maximum(n - 1, 0), r * 3 + which)

    def body(q_ref, kc_ref, kp_ref, vc_ref, vp_ref, o_ref, lse_ref):
        n = pl.program_id(1)
        valid, dist = _band_terms(dil, n > 0)
        q = q_ref[...]
        k2 = jnp.concatenate([kp_ref[...], kc_ref[...]], axis=0)
        v2 = jnp.concatenate([vp_ref[...], vc_ref[...]], axis=0)
        masks = [_head_lane_mask(h, BLOCK) for h in range(DIL_HEADS)]
        logits = [_dot_nt(jnp.where(masks[h], q, jnp.zeros_like(q)), k2) for h in range(DIL_HEADS)]
        ps, lses = [], []
        for h in range(DIL_HEADS):
            lg = jnp.where(valid, logits[h] * 0.125 - float(slopes[h]) * dist, NEG_INF)
            mx = jnp.max(lg, axis=1, keepdims=True)
            lse = mx + jnp.log(jnp.sum(jnp.exp(lg - mx), axis=1, keepdims=True))
            ps.append(jnp.exp(lg - lse).astype(BF16))
            lses.append(lse)
        o_acc = jnp.zeros((BLOCK, DIL_W), F32)
        lse_acc = jnp.zeros((BLOCK, DIL_W), F32)
        for h in range(DIL_HEADS):
            o_acc = jnp.where(masks[h], _dot_nn(ps[h], v2), o_acc)
            lse_acc = jnp.where(masks[h], lses[h], lse_acc)
        o_ref[...] = o_acc
        lse_ref[...] = lse_acc

    blk = (BLOCK, DIL_W)
    o, lse = pl.pallas_call(
        body,
        name=f"dil_fwd_g{group}",
        grid=(dil, nb),
        in_specs=[pl.BlockSpec(blk, col(0)), pl.BlockSpec(blk, col(1)), pl.BlockSpec(blk, col_prev(1)),
                  pl.BlockSpec(blk, col(2)), pl.BlockSpec(blk, col_prev(2))],
        out_specs=[pl.BlockSpec(blk, lambda r, n: (n, r))] * 2,
        out_shape=[jax.ShapeDtypeStruct((sub, dil * DIL_W), F32)] * 2,
        compiler_params=_cparams(("parallel", "parallel")),
    )(view, view, view, view, view)
    return o.reshape(s, DIL_W), lse.reshape(s, DIL_W)


def _dil_bwd(qkv, do, lse, cterm, group):
    _, dil = DIL_GROUPS[group]
    s = qkv.shape[0]
    sub = s // dil
    nb = sub // BLOCK
    view = qkv.reshape(sub, dil * 3 * DIL_W)
    slopes = _alibi_slopes()[group * DIL_HEADS:(group + 1) * DIL_HEADS]
    do_v, lse_v, c_v = (t.reshape(sub, dil * DIL_W) for t in (do, lse, cterm))

    def col(which, shift):
        if shift == 0:
            return lambda r, n: (n, r * 3 + which)
        if shift < 0:
            return lambda r, n: (jnp.maximum(n - 1, 0), r * 3 + which)
        return lambda r, n: (jnp.minimum(n + 1, nb - 1), r * 3 + which)

    def own(shift):
        if shift == 0:
            return lambda r, n: (n, r)
        return lambda r, n: (jnp.minimum(n + 1, nb - 1), r)

    def body(q_ref, qn_ref, kc_ref, kp_ref, vc_ref, vp_ref, do_ref, don_ref, lse_ref, lsen_ref, c_ref, cn_ref,
             dq_ref, dk_ref, dv_ref):
        n = pl.program_id(1)
        valid, dist = _band_terms(dil, n > 0)
        valid_n = _band_terms(dil, True)[0][:, :BLOCK] & (n < nb - 1)
        dist_n = dist[:, :BLOCK]
        q, qn = q_ref[...], qn_ref[...]
        kc, vc = kc_ref[...], vc_ref[...]
        k2 = jnp.concatenate([kp_ref[...], kc], axis=0)
        v2 = jnp.concatenate([vp_ref[...], vc], axis=0)
        dov, donv = do_ref[...], don_ref[...]
        lsev, lsenv, cv, cnv = lse_ref[...], lsen_ref[...], c_ref[...], cn_ref[...]
        masks = [_head_lane_mask(h, BLOCK) for h in range(DIL_HEADS)]

        def head_col(t, hm):
            return jnp.max(jnp.where(hm, t, NEG_INF), axis=1, keepdims=True)

        qhs = [jnp.where(hm, q, jnp.zeros_like(q)) for hm in masks]
        qnhs = [jnp.where(hm, qn, jnp.zeros_like(qn)) for hm in masks]
        dohs = [jnp.where(hm, dov, 0.0).astype(BF16) for hm in masks]
        donhs = [jnp.where(hm, donv, 0.0).astype(BF16) for hm in masks]
        logit = [_dot_nt(qhs[h], k2) for h in range(DIL_HEADS)]
        dp = [_dot_nt(dohs[h], v2) for h in range(DIL_HEADS)]
        logit_n = [_dot_nt(qnhs[h], kc) for h in range(DIL_HEADS)]
        dp_n = [_dot_nt(donhs[h], vc) for h in range(DIL_HEADS)]
        p16, dlog, pn16, dlog_n = [], [], [], []
        for h in range(DIL_HEADS):
            hm, slope = masks[h], float(slopes[h])
            p = jnp.where(valid, jnp.exp(logit[h] * 0.125 - slope * dist - head_col(lsev, hm)), 0.0)
            dlog.append((p * (dp[h] + head_col(cv, hm)) * 0.125).astype(BF16))
            p16.append(p.astype(BF16))
            pn = jnp.where(valid_n, jnp.exp(logit_n[h] * 0.125 - slope * dist_n - head_col(lsenv, hm)), 0.0)
            dlog_n.append((pn * (dp_n[h] + head_col(cnv, hm)) * 0.125).astype(BF16))
            pn16.append(pn.astype(BF16))
        dq_acc = jnp.zeros((BLOCK, DIL_W), F32)
        dk_acc = jnp.zeros((BLOCK, DIL_W), F32)
        dv_acc = jnp.zeros((BLOCK, DIL_W), F32)
        for h in range(DIL_HEADS):
            dq_acc = jnp.where(masks[h], _dot_nn(dlog[h], k2), dq_acc)
            dk_acc += _dot_tn(dlog[h][:, BLOCK:], qhs[h]) + _dot_tn(dlog_n[h], qnhs[h])
            dv_acc += _dot_tn(p16[h][:, BLOCK:], dohs[h]) + _dot_tn(pn16[h], donhs[h])
        dq_ref[...] = dq_acc.astype(BF16)
        dk_ref[...] = dk_acc.astype(BF16)
        dv_ref[...] = dv_acc.astype(BF16)

    blk = (BLOCK, DIL_W)
    outs = pl.pallas_call(
        body,
        name=f"dil_bwd_g{group}",
        grid=(dil, nb),
        in_specs=[pl.BlockSpec(blk, col(0, 0)), pl.BlockSpec(blk, col(0, 1)),
                  pl.BlockSpec(blk, col(1, 0)), pl.BlockSpec(blk, col(1, -1)),
                  pl.BlockSpec(blk, col(2, 0)), pl.BlockSpec(blk, col(2, -1)),
                  pl.BlockSpec(blk, own(0)), pl.BlockSpec(blk, own(1)),
                  pl.BlockSpec(blk, own(0)), pl.BlockSpec(blk, own(1)),
                  pl.BlockSpec(blk, own(0)), pl.BlockSpec(blk, own(1))],
        out_specs=[pl.BlockSpec(blk, lambda r, n: (n, r))] * 3,
        out_shape=[jax.ShapeDtypeStruct((sub, dil * DIL_W), BF16)] * 3,
        compiler_params=_cparams(("parallel", "parallel")),
    )(view, view, view, view, view, view, do_v, do_v, lse_v, lse_v, c_v, c_v)
    return tuple(t.reshape(s, DIL_W) for t in outs)


SB_PAIRS = SB_HEADS // 2
SB_COL0 = 0
LOG2E = 1.4426950408889634


SB_EXP_CLAMP = 64.0


def _sb_softplus2(zs):
    t = 1.0 + jnp.exp2(jnp.minimum(zs, SB_EXP_CLAMP))
    return jnp.maximum(jnp.log(t) * LOG2E, zs)


def _sb_consts(nkb):
    row = lax.broadcasted_iota(jnp.int32, (SB_BQ, SB_BK), 0)
    colk = lax.broadcasted_iota(jnp.int32, (SB_BQ, SB_BK), 1)
    rr = lax.broadcasted_iota(jnp.int32, (SB_BK, SB_BK), 0)
    cc = lax.broadcasted_iota(jnp.int32, (SB_BK, SB_BK), 1)
    lane = lax.broadcasted_iota(jnp.int32, (SB_BQ, 128), 1)
    assert 2 * nkb <= 128
    return colk < row, rr, cc, lane < HEAD_DIM, lane


def _split_heads(t):
    first = lax.broadcasted_iota(jnp.int32, t.shape, 1) < HEAD_DIM
    zero = jnp.zeros_like(t)
    return jnp.where(first, t, zero), jnp.where(first, zero, t)


def _sb_fwd(qkv, shard_pack):
    s = qkv.shape[0]
    nq, nkb = s // SB_BQ, s // SB_BK
    zscale = LOG2E / math.sqrt(HEAD_DIM)
    r_pack, w_pack = shard_pack.shape

    def body(q_ref, k_ref, v_ref, pack_ref, o_ref, a_row, others_ref, zs_scr, a_scr, acc_scr, cl_scr,
             send_sems, recv_sems):
        i = pl.program_id(1)
        pair = pl.program_id(0)
        gather = (pack_ref, others_ref, send_sems, recv_sems)

        @pl.when((pair == 0) & (i == 0))
        def _():
            _gather_start(*gather)

        @pl.when((pair == 1) & (i == 0))
        def _():
            _gather_pass_on(*gather)

        @pl.when((pair == SB_PAIRS - 1) & (i == nq - 1))
        def _():
            _gather_finish(*gather)

        causal, rr, cc, _, _ = _sb_consts(nkb)
        later = (rr > cc).astype(BF16)
        qh = _split_heads(q_ref[...])

        def rows(j):
            return pl.ds(pl.multiple_of(j * SB_BK, SB_BK), SB_BK)

        def scores_to(slot, j):
            kb = k_ref[rows(j), :]
            for hh in range(2):
                zs_scr[slot, hh] = _dot_nt(qh[hh], kb) * zscale

        def weights(slot, j, masked):
            xs, sums, sufs = [], [], []
            for hh in range(2):
                zs = zs_scr[slot, hh]
                sp = _sb_softplus2(zs)
                if masked:
                    sp = jnp.where(causal, sp, 0.0)
                xs.append(zs - sp)
                sums.append(jnp.sum(sp, axis=1, keepdims=True))
                sufs.append(_dot_f32_by_01(sp, later, 2))
            for hh in range(2):
                cl = cl_scr[hh]
                a = jnp.exp2(xs[hh] - (sufs[hh] + jnp.concatenate([cl, cl], axis=1)))
                if masked:
                    a = jnp.where(causal, a, 0.0)
                a16 = a.astype(BF16)
                a_scr[slot, :, hh * SB_BK:(hh + 1) * SB_BK] = a16
                a_row[0, 0, j, :, hh * SB_BK:(hh + 1) * SB_BK] = a16
                cl_scr[hh] = cl + sums[hh]

        def add_av(slot, j):
            v0, v1 = _split_heads(v_ref[rows(j), :])
            acc_scr[...] += _dot_nn(a_scr[slot], jnp.concatenate([v0, v1], axis=0))

        acc_scr[...] = jnp.zeros_like(acc_scr)
        cl_scr[...] = jnp.zeros_like(cl_scr)
        scores_to(0, i)
        scores_to(1, jnp.maximum(i - 1, 0))
        weights(0, i, True)

        def step(j, prev, cur):
            scores_to(prev, jnp.maximum(j - 1, 0))
            add_av(prev, j + 1)
            weights(cur, j, False)

        def two_steps(u, _):
            j = i - 1 - 2 * u
            step(j, 0, 1)
            step(j - 1, 1, 0)
            return 0

        lax.fori_loop(0, i // 2, two_steps, 0)

        @pl.when(i % 2 == 1)
        def _():
            step(0, 0, 1)
            add_av(1, 0)

        @pl.when(i % 2 == 0)
        def _():
            add_av(0, 0)

        o_ref[...] = acc_scr[...]

    def full(which):
        return pl.BlockSpec((s, 128), lambda p, i: (0, SB_COL0 + 4 * which + p))

    return pl.pallas_call(
        body,
        name="sb_fwd",
        grid=(SB_PAIRS, nq),
        in_specs=[pl.BlockSpec((SB_BQ, 128), lambda p, i: (i, SB_COL0 + p)), full(1), full(2), ANY],
        out_specs=[pl.BlockSpec((SB_BQ, 128), lambda p, i: (i, p)),
                   pl.BlockSpec((1, 1, nkb, SB_BQ, 2 * SB_BK), lambda p, i: (p, i, 0, 0, 0)), ANY],
        out_shape=[jax.ShapeDtypeStruct((s, SB_W), F32),
                   jax.ShapeDtypeStruct((SB_PAIRS, nq, nkb, SB_BQ, 2 * SB_BK), BF16),
                   jax.ShapeDtypeStruct((N_CHIPS, 2, r_pack // 2, w_pack), shard_pack.dtype)],
        scratch_shapes=[pltpu.VMEM((2, 2, SB_BQ, SB_BK), F32), pltpu.VMEM((2, SB_BQ, 2 * SB_BK), BF16),
                        pltpu.VMEM((SB_BQ, 128), F32), pltpu.VMEM((2, SB_BQ, 128), F32),
                        pltpu.SemaphoreType.DMA((6,)), pltpu.SemaphoreType.DMA((6,))],
        compiler_params=_cparams(("arbitrary", "arbitrary")),
    )(qkv, qkv, qkv, shard_pack.reshape(2, r_pack // 2, w_pack))


def _sb_bwd(qkv, do, a_hbm, chip_sums):
    s = qkv.shape[0]
    nq, nkb = s // SB_BQ, s // SB_BK
    scale = 1.0 / math.sqrt(HEAD_DIM)
    zscale = LOG2E * scale

    def body(q_ref, k_ref, v_ref, do_ref, a_row, sums_ref, dq_ref, dk_ref, dv_ref, got_ref,
             zs_scr, da_scr, dz_scr, a_scr, cg_scr, send_sems, recv_sems):
        i = pl.program_id(1)
        pair = pl.program_id(0)
        first_step = (pair == 0) & (i == 0)
        last_step = (pair == SB_PAIRS - 1) & (i == nq - 1)

        @pl.when(first_step)
        def _():
            _exchange_start(sums_ref, got_ref, send_sems, recv_sems)

        @pl.when(i == 0)
        def _():
            dk_ref[...] = jnp.zeros_like(dk_ref)
            dv_ref[...] = jnp.zeros_like(dv_ref)

        causal, rr, cc, first, _ = _sb_consts(nkb)
        earlier = (rr < cc).astype(BF16)
        q2 = q_ref[...]
        qh = _split_heads(q2)
        do2 = do_ref[...].astype(BF16)
        doh = _split_heads(do2)

        def rows(j):
            return pl.ds(pl.multiple_of(j * SB_BK, SB_BK), SB_BK)

        def products_to(slot, j):
            kb, vb = k_ref[rows(j), :], v_ref[rows(j), :]
            for hh in range(2):
                zs_scr[slot, hh] = _dot_nt(qh[hh], kb) * zscale
                da_scr[slot, hh] = _dot_nt(doh[hh], vb)

        def by_head(t):
            return jnp.where(first, t[:SB_BK], t[SB_BK:])

        def apply(slot, j):
            k0, k1 = _split_heads(k_ref[rows(j), :])
            dq_ref[...] += _dot_nn(dz_scr[slot], jnp.concatenate([k0, k1], axis=0)) * scale
            dk_ref[rows(j), :] += by_head(_dot_tn(dz_scr[slot], q2)) * scale
            dv_ref[rows(j), :] += by_head(_dot_tn(a_scr[slot], do2))

        def grads(slot, j, masked):
            gs, gpres = [], []
            for hh in range(2):
                a16 = a_row[0, 0, j, :, hh * SB_BK:(hh + 1) * SB_BK]
                a_scr[slot, :, hh * SB_BK:(hh + 1) * SB_BK] = a16
                g = a16.astype(F32) * da_scr[slot, hh]
                gs.append(g)
                gpres.append(_dot_f32_by_01(g, earlier, 1))
            sigs = []
            for hh in range(2):
                zs = zs_scr[slot, hh]
                sigs.append(jnp.exp2(zs - _sb_softplus2(zs)))
            for hh in range(2):
                cg = cg_scr[hh]
                dz = gs[hh] - (gs[hh] + (gpres[hh] + jnp.concatenate([cg, cg], axis=1))) * sigs[hh]
                if masked:
                    dz = jnp.where(causal, dz, 0.0)
                dz_scr[slot, :, hh * SB_BK:(hh + 1) * SB_BK] = dz.astype(BF16)
                cg_scr[hh] = cg + jnp.sum(gs[hh], axis=1, keepdims=True)

        dq_ref[...] = jnp.zeros_like(dq_ref)
        cg_scr[...] = jnp.zeros_like(cg_scr)
        dz_scr[1] = jnp.zeros((SB_BQ, 2 * SB_BK), BF16)
        a_scr[1] = jnp.zeros((SB_BQ, 2 * SB_BK), BF16)
        products_to(0, 0)

        def step(j, cur, nxt):
            products_to(nxt, j + 1)
            apply(nxt, jnp.maximum(j - 1, 0))
            grads(cur, j, False)

        def two_steps(u, _):
            step(2 * u, 0, 1)
            step(2 * u + 1, 1, 0)
            return 0

        lax.fori_loop(0, i // 2, two_steps, 0)

        def last(cur, nxt):
            apply(nxt, jnp.maximum(i - 1, 0))
            grads(cur, i, True)
            apply(cur, i)

        @pl.when(i % 2 == 1)
        def _():
            step(i - 1, 0, 1)
            last(1, 0)

        @pl.when(i % 2 == 0)
        def _():
            last(0, 1)

        @pl.when(last_step)
        def _():
            _exchange_wait(sums_ref, got_ref, send_sems, recv_sems)

    def full(which):
        return pl.BlockSpec((s, 128), lambda p, i: (0, SB_COL0 + 4 * which + p))

    qblk = pl.BlockSpec((SB_BQ, 128), lambda p, i: (i, p))
    acc = pl.BlockSpec((s, 128), lambda p, i: (0, p))
    return pl.pallas_call(
        body,
        name="sb_bwd",
        grid=(SB_PAIRS, nq),
        in_specs=[pl.BlockSpec((SB_BQ, 128), lambda p, i: (i, SB_COL0 + p)), full(1), full(2), qblk,
                  pl.BlockSpec((1, 1, nkb, SB_BQ, 2 * SB_BK), lambda p, i: (p, i, 0, 0, 0)), ANY],
        out_specs=[qblk, acc, acc, ANY],
        out_shape=[jax.ShapeDtypeStruct((s, SB_W), F32)] * 3 + [jax.ShapeDtypeStruct(chip_sums.shape, chip_sums.dtype)],
        scratch_shapes=[pltpu.VMEM((2, 2, SB_BQ, SB_BK), F32), pltpu.VMEM((2, 2, SB_BQ, SB_BK), F32),
                        pltpu.VMEM((2, SB_BQ, 2 * SB_BK), BF16), pltpu.VMEM((2, SB_BQ, 2 * SB_BK), BF16),
                        pltpu.VMEM((2, SB_BQ, 128), F32),
                        pltpu.SemaphoreType.DMA((3,)), pltpu.SemaphoreType.DMA((3,))],
        compiler_params=_cparams(("arbitrary", "arbitrary")),
    )(qkv, qkv, qkv, do, a_hbm, chip_sums)


MERGE_TILE = 256


def _group_mix(lses):
    mx = jnp.maximum(jnp.maximum(lses[0], lses[1]), lses[2])
    es = [jnp.exp(t - mx) for t in lses]
    den = es[0] + es[1] + es[2]
    return [e / den for e in es]


def _merge_fwd(o_groups, lse_groups, o_sb, gl, b_gate, w_up_dil, w_up_sb):
    s = gl.shape[0]
    t = MERGE_TILE

    def body(o0, o1, o2, l0, l1, l2, ob_ref, gl_ref, bg_ref, wd_ref, ws_ref, merged_ref, oa_ref):
        w = _group_mix([l0[...], l1[...], l2[...]])
        oa = (w[0] * o0[...] + w[1] * o1[...] + w[2] * o2[...]).astype(BF16)
        ua = _dot_nn(oa, wd_ref[...])
        ub = _dot_nn(ob_ref[...].astype(BF16), ws_ref[...])
        gate = jax.nn.sigmoid(gl_ref[...] + bg_ref[...])
        merged_ref[...] = (gate[:, :D_MODEL] * ua + gate[:, D_MODEL:] * ub).astype(BF16)
        oa_ref[...] = oa

    dil = pl.BlockSpec((t, DIL_W), lambda i: (i, 0))
    const = lambda shape: pl.BlockSpec(shape, lambda i: (0, 0))
    return pl.pallas_call(
        body,
        name="merge_fwd",
        grid=(s // t,),
        in_specs=[dil] * 6 + [pl.BlockSpec((t, SB_W), lambda i: (i, 0)), pl.BlockSpec((t, GATE_W), lambda i: (i, 0)),
                              const((1, GATE_W)), const((DIL_W, D_MODEL)), const((SB_W, D_MODEL))],
        out_specs=[pl.BlockSpec((t, D_MODEL), lambda i: (i, 0)), dil],
        out_shape=[jax.ShapeDtypeStruct((s, D_MODEL), BF16), jax.ShapeDtypeStruct((s, DIL_W), BF16)],
        compiler_params=_cparams(("parallel",)),
    )(*o_groups, *lse_groups, o_sb, gl, b_gate, w_up_dil, w_up_sb)


def _merge_bwd(dmerged, o_groups, lse_groups, o_sb, gl, b_gate, w_up_dil, w_up_sb, swap):
    s = gl.shape[0]
    t = MERGE_TILE
    n_chunks, r_swap, w_swap = swap.shape
    swap = swap.reshape(n_chunks, 2, r_swap // 2, w_swap)

    def body(dm_ref, o0, o1, o2, l0, l1, l2, ob_ref, gl_ref, bg_ref, wd_ref, ws_ref, swap_ref,
             dua_ref, dub_ref, dgl_ref, dbg_ref, dosb_ref, d0, d1, d2, c0, c1, c2, got_ref, send_sem, recv_sem):
        i = pl.program_id(0)

        @pl.when(i == 0)
        def _():
            _swap_copy(swap_ref, got_ref, send_sem, recv_sem).start()

        @pl.when(i == pl.num_programs(0) - 1)
        def _():
            _swap_copy(swap_ref, got_ref, send_sem, recv_sem).wait()

        og = [o0[...], o1[...], o2[...]]
        w = _group_mix([l0[...], l1[...], l2[...]])
        oa = (w[0] * og[0] + w[1] * og[1] + w[2] * og[2]).astype(BF16)
        ua = _dot_nn(oa, wd_ref[...])
        ub = _dot_nn(ob_ref[...].astype(BF16), ws_ref[...])
        gate = jax.nn.sigmoid(gl_ref[...] + bg_ref[...])
        ga, gb = gate[:, :D_MODEL], gate[:, D_MODEL:]
        dm = dm_ref[...]
        dua = (dm * ga).astype(BF16)
        dub = (dm * gb).astype(BF16)
        dua_ref[...] = dua
        dub_ref[...] = dub
        dgl_a = dm * ua * ga * (1.0 - ga)
        dgl_b = dm * ub * gb * (1.0 - gb)
        dgl_ref[:, :D_MODEL] = dgl_a.astype(BF16)
        dgl_ref[:, D_MODEL:] = dgl_b.astype(BF16)
        part = jnp.concatenate([jnp.sum(dgl_a.reshape(t // 8, 8, D_MODEL), axis=0),
                                jnp.sum(dgl_b.reshape(t // 8, 8, D_MODEL), axis=0)], axis=1)

        @pl.when(i == 0)
        def _():
            dbg_ref[...] = part

        @pl.when(i > 0)
        def _():
            dbg_ref[...] += part

        dosb_ref[...] = _dot_nt(dub, ws_ref[...])
        doa = _dot_nt(dua, wd_ref[...])
        rr = lax.broadcasted_iota(jnp.int32, (DIL_W, DIL_W), 0) // HEAD_DIM
        cc = lax.broadcasted_iota(jnp.int32, (DIL_W, DIL_W), 1) // HEAD_DIM
        same_head = (rr == cc).astype(BF16)
        dw = [_dot_f32_by_01(doa * og[g], same_head) for g in range(3)]
        mean_dw = w[0] * dw[0] + w[1] * dw[1] + w[2] * dw[2]
        for g, (d_ref, c_ref) in enumerate(((d0, c0), (d1, c1), (d2, c2))):
            d_ref[...] = w[g] * doa
            c_ref[...] = -w[g] * mean_dw

    dil = pl.BlockSpec((t, DIL_W), lambda i: (i, 0))
    wide = pl.BlockSpec((t, D_MODEL), lambda i: (i, 0))
    gate2 = pl.BlockSpec((t, GATE_W), lambda i: (i, 0))
    sbw = pl.BlockSpec((t, SB_W), lambda i: (i, 0))
    const = lambda shape: pl.BlockSpec(shape, lambda i: (0, 0))
    return pl.pallas_call(
        body,
        name="merge_bwd",
        grid=(s // t,),
        in_specs=[wide] + [dil] * 6 + [sbw, gate2, const((1, GATE_W)), const((DIL_W, D_MODEL)), const((SB_W, D_MODEL)),
                                       ANY],
        out_specs=[wide, wide, gate2, const((8, GATE_W)), sbw] + [dil] * 6 + [ANY],
        out_shape=[jax.ShapeDtypeStruct((s, D_MODEL), BF16), jax.ShapeDtypeStruct((s, D_MODEL), BF16),
                   jax.ShapeDtypeStruct((s, GATE_W), BF16), jax.ShapeDtypeStruct((8, GATE_W), F32),
                   jax.ShapeDtypeStruct((s, SB_W), F32)] + [jax.ShapeDtypeStruct((s, DIL_W), F32)] * 6
        + [jax.ShapeDtypeStruct((n_chunks, r_swap // 2, w_swap), swap.dtype)],
        scratch_shapes=[pltpu.SemaphoreType.DMA, pltpu.SemaphoreType.DMA],
        compiler_params=_cparams(("arbitrary",)),
    )(dmerged, *o_groups, *lse_groups, o_sb, gl, b_gate, w_up_dil, w_up_sb, swap)


ANY = pl.BlockSpec(memory_space=pl.ANY)


def _place():
    x, y, c = lax.axis_index("x"), lax.axis_index("y"), lax.axis_index("c")
    other_chips = [(1 - x, y), (x, 1 - y), (1 - x, 1 - y)]
    return x, y, c, other_chips


def _gather_copies(p_ref, out_ref, send_sems, recv_sems):
    x, y, c, chips = _place()
    me, sibling = 2 * x + y, (x, y, 1 - c)
    idx = [2 * chip[0] + chip[1] for chip in chips]

    def copy(k, chip_idx, core, to, src=None):
        return pltpu.make_async_remote_copy(
            src_ref=out_ref.at[chip_idx, core] if src is None else src, dst_ref=out_ref.at[chip_idx, core],
            send_sem=send_sems.at[k], recv_sem=recv_sems.at[k], device_id=to, device_id_type=MESH)

    first = lambda j: copy(j, me, c, (*chips[j], c), src=p_ref.at[c])
    landed = lambda j: copy(j, idx[j], c, (x, y, c))
    passed = lambda j: copy(3 + j, idx[j], c, sibling)
    handed = lambda j: copy(3 + j, idx[j], 1 - c, (x, y, c))
    return first, landed, passed, handed


def _gather_start(*refs):
    first = _gather_copies(*refs)[0]
    for j in range(3):
        first(j).start()


def _gather_pass_on(*refs):
    _, landed, passed, _ = _gather_copies(*refs)
    for j in range(3):
        landed(j).wait_recv()
        passed(j).start()


def _gather_finish(*refs):
    first, _, passed, handed = _gather_copies(*refs)
    for j in range(3):
        handed(j).wait_recv()
    for j in range(3):
        first(j).wait_send()
        passed(j).wait_send()


def _fill_own_slot(others, pack):
    n, _, rh, wd = others.shape
    me = 2 * lax.axis_index("x") + lax.axis_index("y")
    mine = lax.broadcasted_iota(jnp.int32, (n, 1, 1, 1), 0) == me
    return jnp.where(mine, pack.reshape(1, 2, rh, wd), others).reshape(n, 2 * rh, wd)


def _all_gather_weights(pack):
    r, wd = pack.shape
    rh = r // 2

    def body(p_ref, out_ref, send_sems, recv_sems):
        _gather_start(p_ref, out_ref, send_sems, recv_sems)
        _gather_pass_on(p_ref, out_ref, send_sems, recv_sems)
        _gather_finish(p_ref, out_ref, send_sems, recv_sems)

    others = pl.pallas_call(
        body,
        name="all_gather_weights",
        in_specs=[ANY],
        out_specs=ANY,
        out_shape=jax.ShapeDtypeStruct((N_CHIPS, 2, rh, wd), pack.dtype),
        scratch_shapes=[pltpu.SemaphoreType.DMA((6,)), pltpu.SemaphoreType.DMA((6,))],
    )(pack.reshape(2, rh, wd))
    return _fill_own_slot(others, pack)


def _swap_copy(g_ref, out_ref, send_sem, recv_sem):
    x, y, c, _ = _place()
    return pltpu.make_async_remote_copy(
        src_ref=g_ref.at[:, 1 - c], dst_ref=out_ref,
        send_sem=send_sem, recv_sem=recv_sem, device_id=(x, y, 1 - c), device_id_type=MESH)


def _swap_halves(g):
    n, r, wd = g.shape
    rh = r // 2
    g = g.reshape(n, 2, rh, wd)

    def body(g_ref, out_ref, send_sem, recv_sem):
        cp = _swap_copy(g_ref, out_ref, send_sem, recv_sem)
        cp.start()
        cp.wait()

    return pl.pallas_call(
        body,
        name="grad_swap_halves",
        in_specs=[ANY],
        out_specs=ANY,
        out_shape=jax.ShapeDtypeStruct((n, rh, wd), g.dtype),
        scratch_shapes=[pltpu.SemaphoreType.DMA, pltpu.SemaphoreType.DMA],
    )(g)


def _add_halves(g, got, core):
    n, r, wd = g.shape
    rh = r // 2
    t = rh // 4
    nt = rh // t

    def body(c_ref, a_ref, b_ref, o_ref):
        o_ref[...] = (a_ref[0] + b_ref[...]).astype(BF16)

    grid_spec = pltpu.PrefetchScalarGridSpec(
        num_scalar_prefetch=1,
        grid=(n, nt),
        in_specs=[pl.BlockSpec((1, 1, t, wd), lambda s, i, c: (s, c[0], i, 0)),
                  pl.BlockSpec((1, t, wd), lambda s, i, c: (s, i, 0))],
        out_specs=pl.BlockSpec((1, t, wd), lambda s, i, c: (s, i, 0)),
    )
    return pl.pallas_call(
        body,
        name="grad_add_halves",
        grid_spec=grid_spec,
        out_shape=jax.ShapeDtypeStruct((n, rh, wd), BF16),
        compiler_params=_cparams(("parallel", "parallel")),
    )(core, g.reshape(n, 2, rh, wd), got)


def _exchange_copies(h_ref, out_ref, send_sems, recv_sems):
    x, y, c, chips = _place()
    me = 2 * x + y

    def copy(j, slot):
        them = 2 * chips[j][0] + chips[j][1]
        return pltpu.make_async_remote_copy(
            src_ref=h_ref.at[them], dst_ref=out_ref.at[me if slot == "mine" else them],
            send_sem=send_sems.at[j], recv_sem=recv_sems.at[j], device_id=(*chips[j], c), device_id_type=MESH)

    return (lambda j: copy(j, "mine")), (lambda j: copy(j, "theirs"))


def _exchange_start(h_ref, out_ref, send_sems, recv_sems):
    send = _exchange_copies(h_ref, out_ref, send_sems, recv_sems)[0]
    for j in range(3):
        send(j).start()


def _exchange_wait(h_ref, out_ref, send_sems, recv_sems):
    send, arrival = _exchange_copies(h_ref, out_ref, send_sems, recv_sems)
    for j in range(3):
        arrival(j).wait_recv()
    for j in range(3):
        send(j).wait_send()


def _sum_chips(b, h, chip):
    n, rh, wd = b.shape
    t = rh // 4

    def body(chip_ref, b_ref, own_ref, o_ref):
        own = own_ref[0]
        s0, s1, s2, s3 = (jnp.where(chip_ref[0] == k, own, b_ref[k]).astype(F32) for k in range(n))
        o_ref[...] = ((s0 + s1) + s2) + s3

    grid_spec = pltpu.PrefetchScalarGridSpec(
        num_scalar_prefetch=1,
        grid=(rh // t,),
        in_specs=[pl.BlockSpec((n, t, wd), lambda i, chip: (0, i, 0)),
                  pl.BlockSpec((1, t, wd), lambda i, chip: (chip[0], i, 0))],
        out_specs=pl.BlockSpec((t, wd), lambda i, chip: (i, 0)),
    )
    return pl.pallas_call(
        body,
        name="grad_sum_chips",
        grid_spec=grid_spec,
        out_shape=jax.ShapeDtypeStruct((rh, wd), F32),
        compiler_params=_cparams(("parallel",)),
    )(chip, b, h)


def _join_halves(tc):
    rh, wd = tc.shape

    def body(t_ref, out_ref, send_sem, recv_sem):
        x, y, c, _ = _place()
        cp = pltpu.make_async_remote_copy(
            src_ref=t_ref, dst_ref=out_ref.at[c],
            send_sem=send_sem, recv_sem=recv_sem, device_id=(x, y, 1 - c), device_id_type=MESH)
        cp.start()
        cp.wait()

    halves = pl.pallas_call(
        body,
        name="grad_join_halves",
        in_specs=[ANY],
        out_specs=ANY,
        out_shape=jax.ShapeDtypeStruct((2, rh, wd), tc.dtype),
        scratch_shapes=[pltpu.SemaphoreType.DMA, pltpu.SemaphoreType.DMA],
    )(tc)
    return lax.dynamic_update_slice(halves, tc[None], (lax.axis_index("c"), 0, 0)).reshape(2 * rh, wd)


def _all_reduce_small(pack):
    rows, lanes = pack.shape

    def body(p_ref, out_ref, buf, send_sems, recv_sems):
        x, y, c, _ = _place()
        me = 4 * x + 2 * y + c
        buf[me] = p_ref[...]
        sends = []
        for k in range(1, N_DEV):
            peer = (x ^ (k >> 2), y ^ ((k >> 1) & 1), c ^ (k & 1))
            sends.append(pltpu.make_async_remote_copy(
                src_ref=p_ref, dst_ref=buf.at[me], send_sem=send_sems.at[k - 1], recv_sem=recv_sems.at[k - 1],
                device_id=peer, device_id_type=MESH))
        for cp in sends:
            cp.start()
        for k in range(1, N_DEV):
            pltpu.make_async_remote_copy(
                src_ref=p_ref, dst_ref=buf.at[me ^ k], send_sem=send_sems.at[k - 1], recv_sem=recv_sems.at[k - 1],
                device_id=(x, y, c), device_id_type=MESH).wait_recv()
        for cp in sends:
            cp.wait_send()
        total = buf[0]
        for d in range(1, N_DEV):
            total = total + buf[d]
        out_ref[...] = total

    vm = pl.BlockSpec(memory_space=pltpu.VMEM)
    return pl.pallas_call(
        body,
        name="all_reduce_small",
        in_specs=[vm],
        out_specs=vm,
        out_shape=jax.ShapeDtypeStruct((rows, lanes), F32),
        scratch_shapes=[pltpu.VMEM((N_DEV, rows, lanes), F32), pltpu.SemaphoreType.DMA((N_DEV - 1,)),
                        pltpu.SemaphoreType.DMA((N_DEV - 1,))],
    )(pack)


def _adamw(g, w, m, v, name):
    rows, cols = g.shape
    t = rows
    for cand in (256, 128, 64, 32, 16, 8):
        if rows % cand == 0:
            t = cand
            break

    def body(g_ref, w_ref, m_ref, v_ref, d_ref, nm_ref, nv_ref):
        gv = g_ref[...]
        mv = ADAM_B1 * m_ref[...] + (1.0 - ADAM_B1) * gv
        vv = ADAM_B2 * v_ref[...] + (1.0 - ADAM_B2) * (gv * gv)
        m_hat = mv / (1.0 - ADAM_B1 ** ADAM_STEP)
        v_hat = vv / (1.0 - ADAM_B2 ** ADAM_STEP)
        d_ref[...] = -ADAM_LR * (m_hat / (jnp.sqrt(v_hat) + ADAM_EPS) + ADAM_WD * w_ref[...])
        nm_ref[...] = mv
        nv_ref[...] = vv

    blk = pl.BlockSpec((t, cols), lambda i: (i, 0))
    return pl.pallas_call(
        body,
        name=name,
        grid=(rows // t,),
        in_specs=[blk] * 4,
        out_specs=[blk] * 3,
        out_shape=[jax.ShapeDtypeStruct((rows, cols), F32)] * 3,
        compiler_params=_cparams(("parallel",)),
    )(g, w, m, v)


PACK_W = 1024
BIG = (("w_in", (D_MODEL, IN_COLS), 1), ("w_up_dil", (DIL_W, D_MODEL), 1), ("w_up_sb", (SB_W, D_MODEL), 1),
       ("w_out", (D_MODEL, D_MODEL), 0), ("w_mlp_in", (D_MODEL, D_FF), 1), ("w_mlp_out", (D_FF, D_MODEL), 0))


def _shard_shape(shape, axis):
    return tuple(d // N_CHIPS if a == axis else d for a, d in enumerate(shape))


MIXER_GROUP, MLP_GROUP = BIG[:4], BIG[4:]


def _pack_rows(group=BIG):
    rows, at = {}, 0
    for name, shape, axis in group:
        n = math.prod(_shard_shape(shape, axis)) // PACK_W
        rows[name] = (at, n)
        at += n
    return rows, at


def _pack_shards(shards, group):
    return jnp.concatenate([shards[name].reshape(-1, PACK_W) for name, _, _ in group], axis=0)


def _unpack_full(gathered, group):
    rows, _ = _pack_rows(group)
    full = {}
    for name, shape, axis in group:
        at, n = rows[name]
        parts = gathered[:, at:at + n, :].reshape((N_CHIPS,) + _shard_shape(shape, axis))
        if axis == 0:
            full[name] = parts.reshape(shape)
        else:
            full[name] = jnp.transpose(parts, (1, 0, 2)).reshape(shape)
    return full


def _pack_full_grads(grads, group):
    chunks = []
    for name, shape, axis in group:
        g = grads[name]
        if axis == 0:
            parts = g.reshape((N_CHIPS, shape[0] // N_CHIPS, shape[1]))
        else:
            parts = jnp.transpose(g.reshape((shape[0], N_CHIPS, shape[1] // N_CHIPS)), (1, 0, 2))
        chunks.append(parts.reshape(N_CHIPS, -1, PACK_W))
    return jnp.concatenate(chunks, axis=1)


def _unpack_shard(packed, group):
    rows, _ = _pack_rows(group)
    return {name: packed[rows[name][0]:rows[name][0] + rows[name][1]].reshape(_shard_shape(shape, axis))
            for name, shape, axis in group}


def _local_step(x, target, w, mlp_shards, norm_mix_g, b_gate, norm_mlp_g, norm_final_g, core):
    w_in = w["w_in"]
    sb0 = 9 * DIL_W
    w_sb, w_gate = w_in[:, sb0:QKV_W], w_in[:, QKV_W:]
    w_dil = [jnp.concatenate([w_in[:, (3 * i + g) * DIL_W:(3 * i + g + 1) * DIL_W] for i in range(3)], axis=1)
             for g in range(3)]

    h = _rms_fwd(x, norm_mix_g, "norm_mix")
    qkv_dil = [_matmul(h, w_dil[g], mode="nn", out_dtypes=(BF16,), name=f"proj_dil_g{g}", tn=768)[0] for g in range(3)]
    (qkv_sb,) = _matmul(h, w_sb, mode="nn", out_dtypes=(BF16,), name="proj_sb", tn=768)
    (gl,) = _matmul(h, w_gate, mode="nn", out_dtypes=(F32,), name="proj_gate")
    dil = [_dil_fwd(qkv_dil[g], g) for g in range(3)]
    o_groups, lse_groups = [d[0] for d in dil], [d[1] for d in dil]
    o_sb, a_sb, mlp_others = _sb_fwd(qkv_sb, mlp_shards)
    w = {**w, **_unpack_full(_fill_own_slot(mlp_others, mlp_shards), MLP_GROUP)}
    merged, o_a = _merge_fwd(o_groups, lse_groups, o_sb, gl, b_gate, w["w_up_dil"], w["w_up_sb"])
    (x1,) = _matmul(merged, w["w_out"], mode="nn", out_dtypes=(F32,), name="out_proj",
                    extras=(x,), epilogue=lambda acc, res: (res + acc,))
    h2 = _rms_fwd(x1, norm_mlp_g, "norm_mlp")
    u, act = _matmul(h2, w["w_mlp_in"], mode="nn", out_dtypes=(BF16, BF16), name="mlp_in",
                     epilogue=lambda acc: (acc, jnp.square(jnp.maximum(acc, 0.0))))
    (x2,) = _matmul(act, w["w_mlp_out"], mode="nn", out_dtypes=(F32,), name="mlp_out", tk=2048,
                    extras=(x1,), epilogue=lambda acc, res: (res + acc,))
    dx2, dg_final, loss_part = _loss_head(x2, norm_final_g.reshape(1, D_MODEL), target)

    (du,) = _matmul(dx2, w["w_mlp_out"], mode="nt", out_dtypes=(BF16,), name="mlp_out_dx",
                    extras=(u,), epilogue=lambda acc, uu: (acc * (2.0 * jnp.maximum(uu.astype(F32), 0.0)),))
    (g_mlp_out,) = _matmul(act, dx2, mode="tn", out_dtypes=(F32,), name="mlp_out_dw")
    (g_mlp_in,) = _matmul(h2, du, mode="tn", out_dtypes=(F32,), name="mlp_in_dw")
    (dh2,) = _matmul(du, w["w_mlp_in"], mode="nt", out_dtypes=(F32,), name="mlp_in_dx", tk=2048)
    dx1, dg_mlp = _rms_bwd(dh2, x1, norm_mlp_g, dx2, "norm_mlp_bwd")

    mlp_pack = _pack_full_grads({"w_mlp_in": g_mlp_in, "w_mlp_out": g_mlp_out}, MLP_GROUP)
    (dmerged,) = _matmul(dx1, w["w_out"], mode="nt", out_dtypes=(F32,), name="out_proj_dx")
    (g_out,) = _matmul(merged, dx1, mode="tn", out_dtypes=(F32,), name="out_proj_dw")
    mb = _merge_bwd(dmerged, o_groups, lse_groups, o_sb, gl, b_gate, w["w_up_dil"], w["w_up_sb"], mlp_pack)
    dua, dub, dgl, dbg, do_sb = mb[:5]
    do_groups, c_groups = mb[5:8], mb[8:11]
    mlp_sums = _add_halves(mlp_pack, mb[11], core)
    (g_up_dil,) = _matmul(o_a, dua, mode="tn", out_dtypes=(F32,), name="up_dil_dw")
    (g_up_sb,) = _matmul(o_sb, dub, mode="tn", out_dtypes=(F32,), name="up_sb_dw")
    dq_sb, dk_sb, dv_sb, mlp_got = _sb_bwd(qkv_sb, do_sb, a_sb, mlp_sums)
    dil_b = [_dil_bwd(qkv_dil[g], do_groups[g], lse_groups[g], c_groups[g], g) for g in range(3)]
    dproj = jnp.concatenate(
        [dil_b[g][i].astype(BF16) for i in range(3) for g in range(3)]
        + [t.astype(BF16) for t in (dq_sb, dk_sb, dv_sb)] + [dgl], axis=1)
    (g_in,) = _matmul(h, dproj, mode="tn", out_dtypes=(F32,), name="proj_dw", tm=512, tn=IN_COLS // 2)
    mixer_pack = _pack_full_grads({"w_in": g_in, "w_up_dil": g_up_dil, "w_up_sb": g_up_sb, "w_out": g_out}, MIXER_GROUP)
    mixer_sums = _add_halves(mixer_pack, _swap_halves(mixer_pack), core)
    dh, mixer_got = _matmul(dproj, w["w_in"], mode="nt", out_dtypes=(F32,), name="proj_dx", tk=IN_COLS // 2,
                            exchange=mixer_sums)
    grad_x, dg_mix = _rms_bwd(dh, x, norm_mix_g, dx1, "norm_mix_bwd")

    small = (dg_mix, dbg, dg_mlp, dg_final, loss_part)
    return grad_x, (mixer_got, mixer_sums), (mlp_got, mlp_sums), small


def kernel(x, norm_mix_g, w_in, b_gate, w_up_dil, w_up_sb, w_out, norm_mlp_g, w_mlp_in, w_mlp_out, norm_final_g, loss_target, m_norm_mix_g, m_w_in, m_b_gate, m_w_up_dil, m_w_up_sb, m_w_out, m_norm_mlp_g, m_w_mlp_in, m_w_mlp_out, m_norm_final_g, v_norm_mix_g, v_w_in, v_b_gate, v_w_up_dil, v_w_up_sb, v_w_out, v_norm_mlp_g, v_w_mlp_in, v_w_mlp_out, v_norm_final_g):
    shards = {"w_in": w_in[0], "w_up_dil": w_up_dil[0], "w_up_sb": w_up_sb[0], "w_out": w_out[0],
              "w_mlp_in": w_mlp_in[0], "w_mlp_out": w_mlp_out[0]}
    moments_m = {"w_in": m_w_in[0], "w_up_dil": m_w_up_dil[0], "w_up_sb": m_w_up_sb[0], "w_out": m_w_out[0],
                 "w_mlp_in": m_w_mlp_in[0], "w_mlp_out": m_w_mlp_out[0]}
    moments_v = {"w_in": v_w_in[0], "w_up_dil": v_w_up_dil[0], "w_up_sb": v_w_up_sb[0], "w_out": v_w_out[0],
                 "w_mlp_in": v_w_mlp_in[0], "w_mlp_out": v_w_mlp_out[0]}

    shards16 = {n: s.astype(BF16) for n, s in shards.items()}
    full = _unpack_full(_all_gather_weights(_pack_shards(shards16, MIXER_GROUP)), MIXER_GROUP)
    mlp_shards = _pack_shards(shards16, MLP_GROUP)

    core = lax.axis_index("c").astype(jnp.int32).reshape(1)
    chip = (2 * lax.axis_index("x") + lax.axis_index("y")).astype(jnp.int32).reshape(1)
    grad_x, (mixer_got, mixer_sums), (mlp_got, mlp_sums), small = _local_step(
        x[0], loss_target[0], full, mlp_shards, norm_mix_g, b_gate, norm_mlp_g, norm_final_g, core)

    reduced = _join_halves(_sum_chips(mixer_got, mixer_sums, chip))
    reduced_mlp = _join_halves(_sum_chips(mlp_got, mlp_sums, chip))
    g_shard = {**_unpack_shard(reduced, MIXER_GROUP), **_unpack_shard(reduced_mlp, MLP_GROUP)}

    dg_mix, dbg, dg_mlp, dg_final, loss_part = small
    loss_row = jnp.sum(loss_part, axis=0, keepdims=True)
    small_pack = jnp.concatenate(
        [jnp.sum(dg_mix, axis=0, keepdims=True), jnp.sum(dbg, axis=0, keepdims=True),
         jnp.sum(dg_mlp, axis=0, keepdims=True), jnp.sum(dg_final, axis=0, keepdims=True), loss_row], axis=1)
    n_small = small_pack.shape[1]
    small_sum = _all_reduce_small(small_pack.reshape(n_small // 128, 128)).reshape(1, n_small)
    g_norm_mix = small_sum[:, :D_MODEL]
    g_b_gate = small_sum[:, D_MODEL:3 * D_MODEL]
    g_norm_mlp = small_sum[:, 3 * D_MODEL:4 * D_MODEL]
    g_norm_final = small_sum[:, 4 * D_MODEL:5 * D_MODEL]
    loss = jnp.sum(small_sum[:, 5 * D_MODEL:])

    names = ["norm_mix_g", "w_in", "b_gate", "w_up_dil", "w_up_sb", "w_out", "norm_mlp_g", "w_mlp_in", "w_mlp_out",
             "norm_final_g"]
    grads = dict(g_shard)
    grads.update(norm_mix_g=g_norm_mix, b_gate=g_b_gate, norm_mlp_g=g_norm_mlp, norm_final_g=g_norm_final)
    weights = dict(shards)
    weights.update(norm_mix_g=norm_mix_g, b_gate=b_gate, norm_mlp_g=norm_mlp_g, norm_final_g=norm_final_g.reshape(1, D_MODEL))
    ms = dict(moments_m)
    ms.update(norm_mix_g=m_norm_mix_g, b_gate=m_b_gate, norm_mlp_g=m_norm_mlp_g, norm_final_g=m_norm_final_g.reshape(1, D_MODEL))
    vs = dict(moments_v)
    vs.update(norm_mix_g=v_norm_mix_g, b_gate=v_b_gate, norm_mlp_g=v_norm_mlp_g, norm_final_g=v_norm_final_g.reshape(1, D_MODEL))

    out_shapes = {"norm_mix_g": norm_mix_g.shape, "w_in": w_in.shape, "b_gate": b_gate.shape, "w_up_dil": w_up_dil.shape,
                  "w_up_sb": w_up_sb.shape, "w_out": w_out.shape, "norm_mlp_g": norm_mlp_g.shape,
                  "w_mlp_in": w_mlp_in.shape, "w_mlp_out": w_mlp_out.shape, "norm_final_g": norm_final_g.shape}
    g_out, d_out, m_out, v_out = [], [], [], []
    for n in names:
        d, nm, nv = _adamw(grads[n], weights[n], ms[n], vs[n], "adamw_" + n)
        shape = out_shapes[n]
        g_out.append(grads[n].reshape(shape))
        d_out.append(d.reshape(shape))
        m_out.append(nm.reshape(shape))
        v_out.append(nv.reshape(shape))
    return (loss, grad_x.reshape(x.shape), *g_out, *d_out, *m_out, *v_out)
```

```python
import functools
import math

import jax
import jax.numpy as jnp
import numpy as np
from jax import lax
from jax.experimental import pallas as pl
from jax.experimental.pallas import tpu as pltpu

F32 = jnp.float32
BF16 = jnp.bfloat16
MESH = pl.DeviceIdType.MESH

D_MODEL = 1024
HEAD_DIM = 64
DIL_GROUPS = ((128, 1), (512, 4), (2048, 16))
DIL_HEADS = 4
DIL_W = 256
N_DIL_HEADS = 12
SB_HEADS = 8
SB_W = SB_HEADS * HEAD_DIM
QKV_W = 3 * 3 * DIL_W + 3 * SB_W
GATE_W = 2 * D_MODEL
IN_COLS = QKV_W + GATE_W
D_FF = 4 * D_MODEL
BLOCK = 128
RMS_EPS = 1e-6
NEG_INF = -1e30
N_CHIPS = 4
N_DEV = 8

ADAM_LR = 0.001
ADAM_B1 = 0.9
ADAM_B2 = 0.999
ADAM_EPS = 1e-08
ADAM_WD = 0.01
ADAM_STEP = 10

VMEM_LIMIT = 56 * 1024 * 1024

SB_BQ = 256
SB_BK = 256


def _cparams(sem=None):
    if sem is None:
        return pltpu.CompilerParams(vmem_limit_bytes=VMEM_LIMIT)
    return pltpu.CompilerParams(dimension_semantics=sem, vmem_limit_bytes=VMEM_LIMIT)


def _dot(a, b, dims):
    return lax.dot_general(a, b, (dims, ((), ())), preferred_element_type=F32)


def _dot_nn(a, b):
    return _dot(a, b, ((1,), (0,)))


def _dot_nt(a, b):
    return _dot(a, b, ((1,), (1,)))


def _dot_tn(a, b):
    return _dot(a, b, ((0,), (0,)))


def _dot_f32_by_01(x, m01, pieces=3):
    hi = x.astype(BF16)
    if pieces == 1:
        return _dot_nn(hi, m01)
    r1 = x - hi.astype(F32)
    mid = r1.astype(BF16)
    if pieces == 2:
        return _dot_nn(hi, m01) + _dot_nn(mid, m01)
    lo = (r1 - mid.astype(F32)).astype(BF16)
    return _dot_nn(hi, m01) + _dot_nn(mid, m01) + _dot_nn(lo, m01)


def _matmul(a, b, *, mode, out_dtypes, name, tm=1024, tn=1024, tk=1024, extras=(), epilogue=None, exchange=None):
    if mode == "nn":
        (m, k), (k2, n) = a.shape, b.shape
    elif mode == "nt":
        (m, k), (n, k2) = a.shape, b.shape
    else:
        (k, m), (k2, n) = a.shape, b.shape
    assert k == k2, (a.shape, b.shape, mode)
    tm, tn, tk = min(tm, m), min(tn, n), min(tk, k)
    assert m % tm == 0 and n % tn == 0 and k % tk == 0, (m, n, k, tm, tn, tk)
    nk = k // tk
    n_out = len(out_dtypes)
    n_ex = len(extras)

    if mode == "nn":
        a_spec = pl.BlockSpec((tm, tk), lambda i, j, kk: (i, kk))
        b_spec = pl.BlockSpec((tk, tn), lambda i, j, kk: (kk, j))
        dot = _dot_nn
    elif mode == "nt":
        a_spec = pl.BlockSpec((tm, tk), lambda i, j, kk: (i, kk))
        b_spec = pl.BlockSpec((tn, tk), lambda i, j, kk: (j, kk))
        dot = _dot_nt
    else:
        a_spec = pl.BlockSpec((tk, tm), lambda i, j, kk: (kk, i))
        b_spec = pl.BlockSpec((tk, tn), lambda i, j, kk: (kk, j))
        dot = _dot_tn
    mn_spec = pl.BlockSpec((tm, tn), lambda i, j, kk: (i, j))
    row_spec = pl.BlockSpec((1, tn), lambda i, j, kk: (0, j))
    part_spec = pl.BlockSpec((8, tn), lambda i, j, kk: (i, j))
    ex_specs = [row_spec if e.shape[0] == 1 else mn_spec for e in extras]
    is_part = [isinstance(dt, tuple) for dt in out_dtypes]
    out_dts = [dt[1] if p else dt for dt, p in zip(out_dtypes, is_part)]
    out_specs = [part_spec if p else mn_spec for p in is_part]
    out_shapes = [jax.ShapeDtypeStruct((8 * (m // tm), n) if p else (m, n), dt) for dt, p in zip(out_dts, is_part)]

    n_side = 0 if exchange is None else 1
    grid = (m // tm, n // tn, nk)

    def body(*refs):
        a_ref, b_ref = refs[0], refs[1]
        ex_refs = refs[2:2 + n_ex]
        out_refs = refs[2 + n_ex + n_side:2 + n_ex + n_side + n_out]
        scratch = refs[2 + n_ex + n_side + n_out + n_side:]
        acc_ref = scratch[0] if nk > 1 else None
        if exchange is not None:
            side = (refs[2 + n_ex], refs[2 + n_ex + n_side + n_out]) + tuple(scratch[-2:])
            step = (pl.program_id(0) * grid[1] + pl.program_id(1)) * grid[2] + pl.program_id(2)

            @pl.when(step == 0)
            def _():
                _exchange_start(*side)

            @pl.when(step == grid[0] * grid[1] * grid[2] - 1)
            def _():
                _exchange_wait(*side)

        part = dot(a_ref[...].astype(BF16), b_ref[...].astype(BF16))

        def finish(acc):
            if epilogue is None:
                outs = (acc,)
            else:
                outs = epilogue(acc, *[r[...] for r in ex_refs])
            for o_ref, o in zip(out_refs, outs):
                o_ref[...] = o.astype(o_ref.dtype)

        if nk == 1:
            finish(part)
        else:
            kk = pl.program_id(2)

            @pl.when(kk == 0)
            def _():
                acc_ref[...] = part

            @pl.when(kk > 0)
            def _():
                acc_ref[...] += part

            @pl.when(kk == nk - 1)
            def _():
                finish(acc_ref[...])

    side_in = [] if exchange is None else [exchange]
    outs = pl.pallas_call(
        body,
        name=name,
        grid=grid,
        in_specs=[a_spec, b_spec] + ex_specs + [ANY] * n_side,
        out_specs=out_specs + [ANY] * n_side,
        out_shape=out_shapes + [jax.ShapeDtypeStruct(e.shape, e.dtype) for e in side_in],
        scratch_shapes=([pltpu.VMEM((tm, tn), F32)] if nk > 1 else [])
        + [pltpu.SemaphoreType.DMA((3,)), pltpu.SemaphoreType.DMA((3,))] * n_side,
        compiler_params=_cparams(("arbitrary",) * 3 if n_side else ("parallel", "parallel", "arbitrary")),
    )(a, b, *extras, *side_in)
    return outs


ROW_TILE = 512


def _rows_sum8(t):
    rows, d = t.shape
    return jnp.sum(t.reshape(rows // 8, 8, d), axis=0)


def _rms_rows(x):
    r = lax.rsqrt(jnp.mean(x * x, axis=-1, keepdims=True) + RMS_EPS)
    return x * r, r


def _rms_bwd_rows(dh, x, g):
    xh, r = _rms_rows(x)
    dxh = dh * g
    return r * (dxh - xh * jnp.mean(dxh * xh, axis=-1, keepdims=True)), _rows_sum8(dh * xh)


def _rms_fwd(x, g, name):
    s, d = x.shape

    def body(x_ref, g_ref, h_ref):
        xv = x_ref[...]
        r = lax.rsqrt(jnp.mean(xv * xv, axis=-1, keepdims=True) + RMS_EPS)
        h_ref[...] = (xv * r * g_ref[...]).astype(BF16)

    return pl.pallas_call(
        body,
        name=name,
        grid=(s // ROW_TILE,),
        in_specs=[pl.BlockSpec((ROW_TILE, d), lambda i: (i, 0)), pl.BlockSpec((1, d), lambda i: (0, 0))],
        out_specs=pl.BlockSpec((ROW_TILE, d), lambda i: (i, 0)),
        out_shape=jax.ShapeDtypeStruct((s, d), BF16),
        compiler_params=_cparams(("parallel",)),
    )(x, g)


def _alibi_slopes():
    return np.exp2(np.float32(-8.0) * np.arange(1, N_DIL_HEADS + 1, dtype=np.float32) / np.float32(N_DIL_HEADS))


def _head_lane_mask(h, rows):
    lane = lax.broadcasted_iota(jnp.int32, (rows, DIL_W), 1)
    return (lane >= h * HEAD_DIM) & (lane < (h + 1) * HEAD_DIM)


def _band_terms(dil, has_prev):
    qi = lax.broadcasted_iota(jnp.int32, (BLOCK, 2 * BLOCK), 0)
    kj = lax.broadcasted_iota(jnp.int32, (BLOCK, 2 * BLOCK), 1)
    steps = qi + BLOCK - kj
    valid = (steps >= 0) & (steps <= BLOCK) & ((kj >= BLOCK) | has_prev)
    return valid, steps.astype(F32) * float(dil)


def _dil_fwd(qkv_g, group):
    _, dil = DIL_GROUPS[group]
    s = qkv_g.shape[0]
    sub = s // dil
    nb = sub // BLOCK
    view = qkv_g.reshape(sub, dil * 3 * DIL_W)
    slopes = _alibi_slopes()[group * DIL_HEADS:(group + 1) * DIL_HEADS]

    def col(which):
        return lambda r, n: (n, r * 3 + which)

    def col_prev(which):
        return lambda r, n: (jnp.maximum(n - 1, 0), r * 3 + which)

    def body(q_ref, kc_ref, kp_ref, vc_ref, vp_ref, o_ref, lse_ref):
        n = pl.program_id(1)
        valid, dist = _band_terms(dil, n > 0)
        q = q_ref[...]
        k2 = jnp.concatenate([kp_ref[...], kc_ref[...]], axis=0)
        v2 = jnp.concatenate([vp_ref[...], vc_ref[...]], axis=0)
        masks = [_head_lane_mask(h, BLOCK) for h in range(DIL_HEADS)]
        logits = [_dot_nt(jnp.where(masks[h], q, jnp.zeros_like(q)), k2) for h in range(DIL_HEADS)]
        ps, lses = [], []
        for h in range(DIL_HEADS):
            lg = jnp.where(valid, logits[h] * 0.125 - float(slopes[h]) * dist, NEG_INF)
            mx = jnp.max(lg, axis=1, keepdims=True)
            lse = mx + jnp.log(jnp.sum(jnp.exp(lg - mx), axis=1, keepdims=True))
            ps.append(jnp.exp(lg - lse).astype(BF16))
            lses.append(lse)
        o_acc = jnp.zeros((BLOCK, DIL_W), F32)
        lse_acc = jnp.zeros((BLOCK, DIL_W), F32)
        for h in range(DIL_HEADS):
            o_acc = jnp.where(masks[h], _dot_nn(ps[h], v2), o_acc)
            lse_acc = jnp.where(masks[h], lses[h], lse_acc)
        o_ref[...] = o_acc
        lse_ref[...] = lse_acc

    blk = (BLOCK, DIL_W)
    o, lse = pl.pallas_call(
        body,
        name=f"dil_fwd_g{group}",
        grid=(dil, nb),
        in_specs=[pl.BlockSpec(blk, col(0)), pl.BlockSpec(blk, col(1)), pl.BlockSpec(blk, col_prev(1)),
                  pl.BlockSpec(blk, col(2)), pl.BlockSpec(blk, col_prev(2))],
        out_specs=[pl.BlockSpec(blk, lambda r, n: (n, r))] * 2,
        out_shape=[jax.ShapeDtypeStruct((sub, dil * DIL_W), F32)] * 2,
        compiler_params=_cparams(("parallel", "parallel")),
    )(view, view, view, view, view)
    return o.reshape(s, DIL_W), lse.reshape(s, DIL_W)


def _dil_bwd(qkv, do, lse, cterm, group):
    _, dil = DIL_GROUPS[group]
    s = qkv.shape[0]
    sub = s // dil
    nb = sub // BLOCK
    view = qkv.reshape(sub, dil * 3 * DIL_W)
    slopes = _alibi_slopes()[group * DIL_HEADS:(group + 1) * DIL_HEADS]
    do_v, lse_v, c_v = (t.reshape(sub, dil * DIL_W) for t in (do, lse, cterm))

    def col(which, shift):
        if shift == 0:
            return lambda r, n: (n, r * 3 + which)
        if shift < 0:
            return lambda r, n: (jnp.maximum(n - 1, 0), r * 3 + which)
        return lambda r, n: (jnp.minimum(n + 1, nb - 1), r * 3 + which)

    def own(shift):
        if shift == 0:
            return lambda r, n: (n, r)
        return lambda r, n: (jnp.minimum(n + 1, nb - 1), r)

    def body(q_ref, qn_ref, kc_ref, kp_ref, vc_ref, vp_ref, do_ref, don_ref, lse_ref, lsen_ref, c_ref, cn_ref,
             dq_ref, dk_ref, dv_ref):
        n = pl.program_id(1)
        valid, dist = _band_terms(dil, n > 0)
        valid_n = _band_terms(dil, True)[0][:, :BLOCK] & (n < nb - 1)
        dist_n = dist[:, :BLOCK]
        q, qn = q_ref[...], qn_ref[...]
        kc, vc = kc_ref[...], vc_ref[...]
        k2 = jnp.concatenate([kp_ref[...], kc], axis=0)
        v2 = jnp.concatenate([vp_ref[...], vc], axis=0)
        dov, donv = do_ref[...], don_ref[...]
        lsev, lsenv, cv, cnv = lse_ref[...], lsen_ref[...], c_ref[...], cn_ref[...]
        masks = [_head_lane_mask(h, BLOCK) for h in range(DIL_HEADS)]

        def head_col(t, hm):
            return jnp.max(jnp.where(hm, t, NEG_INF), axis=1, keepdims=True)

        qhs = [jnp.where(hm, q, jnp.zeros_like(q)) for hm in masks]
        qnhs = [jnp.where(hm, qn, jnp.zeros_like(qn)) for hm in masks]
        dohs = [jnp.where(hm, dov, 0.0).astype(BF16) for hm in masks]
        donhs = [jnp.where(hm, donv, 0.0).astype(BF16) for hm in masks]
        logit = [_dot_nt(qhs[h], k2) for h in range(DIL_HEADS)]
        dp = [_dot_nt(dohs[h], v2) for h in range(DIL_HEADS)]
        logit_n = [_dot_nt(qnhs[h], kc) for h in range(DIL_HEADS)]
        dp_n = [_dot_nt(donhs[h], vc) for h in range(DIL_HEADS)]
        p16, dlog, pn16, dlog_n = [], [], [], []
        for h in range(DIL_HEADS):
            hm, slope = masks[h], float(slopes[h])
            p = jnp.where(valid, jnp.exp(logit[h] * 0.125 - slope * dist - head_col(lsev, hm)), 0.0)
            dlog.append((p * (dp[h] + head_col(cv, hm)) * 0.125).astype(BF16))
            p16.append(p.astype(BF16))
            pn = jnp.where(valid_n, jnp.exp(logit_n[h] * 0.125 - slope * dist_n - head_col(lsenv, hm)), 0.0)
            dlog_n.append((pn * (dp_n[h] + head_col(cnv, hm)) * 0.125).astype(BF16))
            pn16.append(pn.astype(BF16))
        dq_acc = jnp.zeros((BLOCK, DIL_W), F32)
        dk_acc = jnp.zeros((BLOCK, DIL_W), F32)
        dv_acc = jnp.zeros((BLOCK, DIL_W), F32)
        for h in range(DIL_HEADS):
            dq_acc = jnp.where(masks[h], _dot_nn(dlog[h], k2), dq_acc)
            dk_acc += _dot_tn(dlog[h][:, BLOCK:], qhs[h]) + _dot_tn(dlog_n[h], qnhs[h])
            dv_acc += _dot_tn(p16[h][:, BLOCK:], dohs[h]) + _dot_tn(pn16[h], donhs[h])
        dq_ref[...] = dq_acc.astype(BF16)
        dk_ref[...] = dk_acc.astype(BF16)
        dv_ref[...] = dv_acc.astype(BF16)

    blk = (BLOCK, DIL_W)
    outs = pl.pallas_call(
        body,
        name=f"dil_bwd_g{group}",
        grid=(dil, nb),
        in_specs=[pl.BlockSpec(blk, col(0, 0)), pl.BlockSpec(blk, col(0, 1)),
                  pl.BlockSpec(blk, col(1, 0)), pl.BlockSpec(blk, col(1, -1)),
                  pl.BlockSpec(blk, col(2, 0)), pl.BlockSpec(blk, col(2, -1)),
                  pl.BlockSpec(blk, own(0)), pl.BlockSpec(blk, own(1)),
                  pl.BlockSpec(blk, own(0)), pl.BlockSpec(blk, own(1)),
                  pl.BlockSpec(blk, own(0)), pl.BlockSpec(blk, own(1))],
        out_specs=[pl.BlockSpec(blk, lambda r, n: (n, r))] * 3,
        out_shape=[jax.ShapeDtypeStruct((sub, dil * DIL_W), BF16)] * 3,
        compiler_params=_cparams(("parallel", "parallel")),
    )(view, view, view, view, view, view, do_v, do_v, lse_v, lse_v, c_v, c_v)
    return tuple(t.reshape(s, DIL_W) for t in outs)


SB_PAIRS = SB_HEADS // 2
SB_COL0 = 0
LOG2E = 1.4426950408889634


SB_EXP_CLAMP = 64.0


def _sb_softplus2(zs):
    t = 1.0 + jnp.exp2(jnp.minimum(zs, SB_EXP_CLAMP))
    return jnp.maximum(jnp.log(t) * LOG2E, zs)


def _sb_consts(nkb):
    row = lax.broadcasted_iota(jnp.int32, (SB_BQ, SB_BK), 0)
    colk = lax.broadcasted_iota(jnp.int32, (SB_BQ, SB_BK), 1)
    rr = lax.broadcasted_iota(jnp.int32, (SB_BK, SB_BK), 0)
    cc = lax.broadcasted_iota(jnp.int32, (SB_BK, SB_BK), 1)
    lane = lax.broadcasted_iota(jnp.int32, (SB_BQ, 128), 1)
    assert 2 * nkb <= 128
    return colk < row, rr, cc, lane < HEAD_DIM, lane


def _split_heads(t):
    first = lax.broadcasted_iota(jnp.int32, t.shape, 1) < HEAD_DIM
    zero = jnp.zeros_like(t)
    return jnp.where(first, t, zero), jnp.where(first, zero, t)


def _sb_fwd(qkv, shard_pack):
    s = qkv.shape[0]
    nq, nkb = s // SB_BQ, s // SB_BK
    zscale = LOG2E / math.sqrt(HEAD_DIM)
    r_pack, w_pack = shard_pack.shape

    def body(q_ref, k_ref, v_ref, pack_ref, o_ref, a_row, others_ref, zs_scr, a_scr, acc_scr, cl_scr,
             send_sems, recv_sems):
        i = pl.program_id(1)
        pair = pl.program_id(0)
        gather = (pack_ref, others_ref, send_sems, recv_sems)

        @pl.when((pair == 0) & (i == 0))
        def _():
            _gather_start(*gather)

        @pl.when((pair == 1) & (i == 0))
        def _():
            _gather_pass_on(*gather)

        @pl.when((pair == SB_PAIRS - 1) & (i == nq - 1))
        def _():
            _gather_finish(*gather)

        causal, rr, cc, _, _ = _sb_consts(nkb)
        later = (rr > cc).astype(BF16)
        qh = _split_heads(q_ref[...])

        def rows(j):
            return pl.ds(pl.multiple_of(j * SB_BK, SB_BK), SB_BK)

        def scores_to(slot, j):
            kb = k_ref[rows(j), :]
            for hh in range(2):
                zs_scr[slot, hh] = _dot_nt(qh[hh], kb) * zscale

        def weights(slot, j, masked):
            xs, sums, sufs = [], [], []
            for hh in range(2):
                zs = zs_scr[slot, hh]
                sp = _sb_softplus2(zs)
                if masked:
                    sp = jnp.where(causal, sp, 0.0)
                xs.append(zs - sp)
                sums.append(jnp.sum(sp, axis=1, keepdims=True))
                sufs.append(_dot_f32_by_01(sp, later, 2))
            for hh in range(2):
                cl = cl_scr[hh]
                a = jnp.exp2(xs[hh] - (sufs[hh] + jnp.concatenate([cl, cl], axis=1)))
                if masked:
                    a = jnp.where(causal, a, 0.0)
                a16 = a.astype(BF16)
                a_scr[slot, :, hh * SB_BK:(hh + 1) * SB_BK] = a16
                a_row[0, 0, j, :, hh * SB_BK:(hh + 1) * SB_BK] = a16
                cl_scr[hh] = cl + sums[hh]

        def add_av(slot, j):
            v0, v1 = _split_heads(v_ref[rows(j), :])
            acc_scr[...] += _dot_nn(a_scr[slot], jnp.concatenate([v0, v1], axis=0))

        acc_scr[...] = jnp.zeros_like(acc_scr)
        cl_scr[...] = jnp.zeros_like(cl_scr)
        scores_to(0, i)
        scores_to(1, jnp.maximum(i - 1, 0))
        weights(0, i, True)

        def step(j, prev, cur):
            scores_to(prev, jnp.maximum(j - 1, 0))
            add_av(prev, j + 1)
            weights(cur, j, False)

        def two_steps(u, _):
            j = i - 1 - 2 * u
            step(j, 0, 1)
            step(j - 1, 1, 0)
            return 0

        lax.fori_loop(0, i // 2, two_steps, 0)

        @pl.when(i % 2 == 1)
        def _():
            step(0, 0, 1)
            add_av(1, 0)

        @pl.when(i % 2 == 0)
        def _():
            add_av(0, 0)

        o_ref[...] = acc_scr[...]

    def full(which):
        return pl.BlockSpec((s, 128), lambda p, i: (0, SB_COL0 + 4 * which + p))

    return pl.pallas_call(
        body,
        name="sb_fwd",
        grid=(SB_PAIRS, nq),
        in_specs=[pl.BlockSpec((SB_BQ, 128), lambda p, i: (i, SB_COL0 + p)), full(1), full(2), ANY],
        out_specs=[pl.BlockSpec((SB_BQ, 128), lambda p, i: (i, p)),
                   pl.BlockSpec((1, 1, nkb, SB_BQ, 2 * SB_BK), lambda p, i: (p, i, 0, 0, 0)), ANY],
        out_shape=[jax.ShapeDtypeStruct((s, SB_W), F32),
                   jax.ShapeDtypeStruct((SB_PAIRS, nq, nkb, SB_BQ, 2 * SB_BK), BF16),
                   jax.ShapeDtypeStruct((N_CHIPS, 2, r_pack // 2, w_pack), shard_pack.dtype)],
        scratch_shapes=[pltpu.VMEM((2, 2, SB_BQ, SB_BK), F32), pltpu.VMEM((2, SB_BQ, 2 * SB_BK), BF16),
                        pltpu.VMEM((SB_BQ, 128), F32), pltpu.VMEM((2, SB_BQ, 128), F32),
                        pltpu.SemaphoreType.DMA((6,)), pltpu.SemaphoreType.DMA((6,))],
        compiler_params=_cparams(("arbitrary", "arbitrary")),
    )(qkv, qkv, qkv, shard_pack.reshape(2, r_pack // 2, w_pack))


def _sb_bwd(qkv, do, a_hbm, chip_sums):
    s = qkv.shape[0]
    nq, nkb = s // SB_BQ, s // SB_BK
    scale = 1.0 / math.sqrt(HEAD_DIM)
    zscale = LOG2E * scale

    def body(q_ref, k_ref, v_ref, do_ref, a_row, sums_ref, dq_ref, dk_ref, dv_ref, got_ref,
             zs_scr, da_scr, dz_scr, a_scr, cg_scr, send_sems, recv_sems):
        i = pl.program_id(1)
        pair = pl.program_id(0)
        first_step = (pair == 0) & (i == 0)
        last_step = (pair == SB_PAIRS - 1) & (i == nq - 1)

        @pl.when(first_step)
        def _():
            _exchange_start(sums_ref, got_ref, send_sems, recv_sems)

        @pl.when(i == 0)
        def _():
            dk_ref[...] = jnp.zeros_like(dk_ref)
            dv_ref[...] = jnp.zeros_like(dv_ref)

        causal, rr, cc, first, _ = _sb_consts(nkb)
        earlier = (rr < cc).astype(BF16)
        q2 = q_ref[...]
        qh = _split_heads(q2)
        do2 = do_ref[...].astype(BF16)
        doh = _split_heads(do2)

        def rows(j):
            return pl.ds(pl.multiple_of(j * SB_BK, SB_BK), SB_BK)

        def products_to(slot, j):
            kb, vb = k_ref[rows(j), :], v_ref[rows(j), :]
            for hh in range(2):
                zs_scr[slot, hh] = _dot_nt(qh[hh], kb) * zscale
                da_scr[slot, hh] = _dot_nt(doh[hh], vb)

        def by_head(t):
            return jnp.where(first, t[:SB_BK], t[SB_BK:])

        def apply(slot, j):
            k0, k1 = _split_heads(k_ref[rows(j), :])
            dq_ref[...] += _dot_nn(dz_scr[slot], jnp.concatenate([k0, k1], axis=0)) * scale
            dk_ref[rows(j), :] += by_head(_dot_tn(dz_scr[slot], q2)) * scale
            dv_ref[rows(j), :] += by_head(_dot_tn(a_scr[slot], do2))

        def grads(slot, j, masked):
            gs, gpres = [], []
            for hh in range(2):
                a16 = a_row[0, 0, j, :, hh * SB_BK:(hh + 1) * SB_BK]
                a_scr[slot, :, hh * SB_BK:(hh + 1) * SB_BK] = a16
                g = a16.astype(F32) * da_scr[slot, hh]
                gs.append(g)
                gpres.append(_dot_f32_by_01(g, earlier, 1))
            sigs = []
            for hh in range(2):
                zs = zs_scr[slot, hh]
                sigs.append(jnp.exp2(zs - _sb_softplus2(zs)))
            for hh in range(2):
                cg = cg_scr[hh]
                dz = gs[hh] - (gs[hh] + (gpres[hh] + jnp.concatenate([cg, cg], axis=1))) * sigs[hh]
                if masked:
                    dz = jnp.where(causal, dz, 0.0)
                dz_scr[slot, :, hh * SB_BK:(hh + 1) * SB_BK] = dz.astype(BF16)
                cg_scr[hh] = cg + jnp.sum(gs[hh], axis=1, keepdims=True)

        dq_ref[...] = jnp.zeros_like(dq_ref)
        cg_scr[...] = jnp.zeros_like(cg_scr)
        dz_scr[1] = jnp.zeros((SB_BQ, 2 * SB_BK), BF16)
        a_scr[1] = jnp.zeros((SB_BQ, 2 * SB_BK), BF16)
        products_to(0, 0)

        def step(j, cur, nxt):
            products_to(nxt, j + 1)
            apply(nxt, jnp.maximum(j - 1, 0))
            grads(cur, j, False)

        def two_steps(u, _):
            step(2 * u, 0, 1)
            step(2 * u + 1, 1, 0)
            return 0

        lax.fori_loop(0, i // 2, two_steps, 0)

        def last(cur, nxt):
            apply(nxt, jnp.maximum(i - 1, 0))
            grads(cur, i, True)
            apply(cur, i)

        @pl.when(i % 2 == 1)
        def _():
            step(i - 1, 0, 1)
            last(1, 0)

        @pl.when(i % 2 == 0)
        def _():
            last(0, 1)

        @pl.when(last_step)
        def _():
            _exchange_wait(sums_ref, got_ref, send_sems, recv_sems)

    def full(which):
        return pl.BlockSpec((s, 128), lambda p, i: (0, SB_COL0 + 4 * which + p))

    qblk = pl.BlockSpec((SB_BQ, 128), lambda p, i: (i, p))
    acc = pl.BlockSpec((s, 128), lambda p, i: (0, p))
    return pl.pallas_call(
        body,
        name="sb_bwd",
        grid=(SB_PAIRS, nq),
        in_specs=[pl.BlockSpec((SB_BQ, 128), lambda p, i: (i, SB_COL0 + p)), full(1), full(2), qblk,
                  pl.BlockSpec((1, 1, nkb, SB_BQ, 2 * SB_BK), lambda p, i: (p, i, 0, 0, 0)), ANY],
        out_specs=[qblk, acc, acc, ANY],
        out_shape=[jax.ShapeDtypeStruct((s, SB_W), F32)] * 3 + [jax.ShapeDtypeStruct(chip_sums.shape, chip_sums.dtype)],
        scratch_shapes=[pltpu.VMEM((2, 2, SB_BQ, SB_BK), F32), pltpu.VMEM((2, 2, SB_BQ, SB_BK), F32),
                        pltpu.VMEM((2, SB_BQ, 2 * SB_BK), BF16), pltpu.VMEM((2, SB_BQ, 2 * SB_BK), BF16),
                        pltpu.VMEM((2, SB_BQ, 128), F32),
                        pltpu.SemaphoreType.DMA((3,)), pltpu.SemaphoreType.DMA((3,))],
        compiler_params=_cparams(("arbitrary", "arbitrary")),
    )(qkv, qkv, qkv, do, a_hbm, chip_sums)


MERGE_TILE = 256


def _group_mix(lses):
    mx = jnp.maximum(jnp.maximum(lses[0], lses[1]), lses[2])
    es = [jnp.exp(t - mx) for t in lses]
    den = es[0] + es[1] + es[2]
    return [e / den for e in es]


def _merge_fwd(o_groups, lse_groups, o_sb, gl, b_gate, w_up_dil, w_up_sb):
    s = gl.shape[0]
    t = MERGE_TILE

    def body(o0, o1, o2, l0, l1, l2, ob_ref, gl_ref, bg_ref, wd_ref, ws_ref, merged_ref, oa_ref):
        w = _group_mix([l0[...], l1[...], l2[...]])
        oa = (w[0] * o0[...] + w[1] * o1[...] + w[2] * o2[...]).astype(BF16)
        ua = _dot_nn(oa, wd_ref[...])
        ub = _dot_nn(ob_ref[...].astype(BF16), ws_ref[...])
        gate = jax.nn.sigmoid(gl_ref[...] + bg_ref[...])
        merged_ref[...] = (gate[:, :D_MODEL] * ua + gate[:, D_MODEL:] * ub).astype(BF16)
        oa_ref[...] = oa

    dil = pl.BlockSpec((t, DIL_W), lambda i: (i, 0))
    const = lambda shape: pl.BlockSpec(shape, lambda i: (0, 0))
    return pl.pallas_call(
        body,
        name="merge_fwd",
        grid=(s // t,),
        in_specs=[dil] * 6 + [pl.BlockSpec((t, SB_W), lambda i: (i, 0)), pl.BlockSpec((t, GATE_W), lambda i: (i, 0)),
                              const((1, GATE_W)), const((DIL_W, D_MODEL)), const((SB_W, D_MODEL))],
        out_specs=[pl.BlockSpec((t, D_MODEL), lambda i: (i, 0)), dil],
        out_shape=[jax.ShapeDtypeStruct((s, D_MODEL), BF16), jax.ShapeDtypeStruct((s, DIL_W), BF16)],
        compiler_params=_cparams(("parallel",)),
    )(*o_groups, *lse_groups, o_sb, gl, b_gate, w_up_dil, w_up_sb)


def _merge_bwd(dmerged, o_groups, lse_groups, o_sb, gl, b_gate, w_up_dil, w_up_sb, swap):
    s = gl.shape[0]
    t = MERGE_TILE
    n_chunks, r_swap, w_swap = swap.shape
    swap = swap.reshape(n_chunks, 2, r_swap // 2, w_swap)

    def body(dm_ref, o0, o1, o2, l0, l1, l2, ob_ref, gl_ref, bg_ref, wd_ref, ws_ref, swap_ref,
             dua_ref, dub_ref, dgl_ref, dbg_ref, dosb_ref, d0, d1, d2, c0, c1, c2, got_ref, send_sem, recv_sem):
        i = pl.program_id(0)

        @pl.when(i == 0)
        def _():
            _swap_copy(swap_ref, got_ref, send_sem, recv_sem).start()

        @pl.when(i == pl.num_programs(0) - 1)
        def _():
            _swap_copy(swap_ref, got_ref, send_sem, recv_sem).wait()

        og = [o0[...], o1[...], o2[...]]
        w = _group_mix([l0[...], l1[...], l2[...]])
        oa = (w[0] * og[0] + w[1] * og[1] + w[2] * og[2]).astype(BF16)
        ua = _dot_nn(oa, wd_ref[...])
        ub = _dot_nn(ob_ref[...].astype(BF16), ws_ref[...])
        gate = jax.nn.sigmoid(gl_ref[...] + bg_ref[...])
        ga, gb = gate[:, :D_MODEL], gate[:, D_MODEL:]
        dm = dm_ref[...]
        dua = (dm * ga).astype(BF16)
        dub = (dm * gb).astype(BF16)
        dua_ref[...] = dua
        dub_ref[...] = dub
        dgl_a = dm * ua * ga * (1.0 - ga)
        dgl_b = dm * ub * gb * (1.0 - gb)
        dgl_ref[:, :D_MODEL] = dgl_a.astype(BF16)
        dgl_ref[:, D_MODEL:] = dgl_b.astype(BF16)
        part = jnp.concatenate([jnp.sum(dgl_a.reshape(t // 8, 8, D_MODEL), axis=0),
                                jnp.sum(dgl_b.reshape(t // 8, 8, D_MODEL), axis=0)], axis=1)

        @pl.when(i == 0)
        def _():
            dbg_ref[...] = part

        @pl.when(i > 0)
        def _():
            dbg_ref[...] += part

        dosb_ref[...] = _dot_nt(dub, ws_ref[...])
        doa = _dot_nt(dua, wd_ref[...])
        rr = lax.broadcasted_iota(jnp.int32, (DIL_W, DIL_W), 0) // HEAD_DIM
        cc = lax.broadcasted_iota(jnp.int32, (DIL_W, DIL_W), 1) // HEAD_DIM
        same_head = (rr == cc).astype(BF16)
        dw = [_dot_f32_by_01(doa * og[g], same_head) for g in range(3)]
        mean_dw = w[0] * dw[0] + w[1] * dw[1] + w[2] * dw[2]
        for g, (d_ref, c_ref) in enumerate(((d0, c0), (d1, c1), (d2, c2))):
            d_ref[...] = w[g] * doa
            c_ref[...] = -w[g] * mean_dw

    dil = pl.BlockSpec((t, DIL_W), lambda i: (i, 0))
    wide = pl.BlockSpec((t, D_MODEL), lambda i: (i, 0))
    gate2 = pl.BlockSpec((t, GATE_W), lambda i: (i, 0))
    sbw = pl.BlockSpec((t, SB_W), lambda i: (i, 0))
    const = lambda shape: pl.BlockSpec(shape, lambda i: (0, 0))
    return pl.pallas_call(
        body,
        name="merge_bwd",
        grid=(s // t,),
        in_specs=[wide] + [dil] * 6 + [sbw, gate2, const((1, GATE_W)), const((DIL_W, D_MODEL)), const((SB_W, D_MODEL)),
                                       ANY],
        out_specs=[wide, wide, gate2, const((8, GATE_W)), sbw] + [dil] * 6 + [ANY],
        out_shape=[jax.ShapeDtypeStruct((s, D_MODEL), BF16), jax.ShapeDtypeStruct((s, D_MODEL), BF16),
                   jax.ShapeDtypeStruct((s, GATE_W), BF16), jax.ShapeDtypeStruct((8, GATE_W), F32),
                   jax.ShapeDtypeStruct((s, SB_W), F32)] + [jax.ShapeDtypeStruct((s, DIL_W), F32)] * 6
        + [jax.ShapeDtypeStruct((n_chunks, r_swap // 2, w_swap), swap.dtype)],
        scratch_shapes=[pltpu.SemaphoreType.DMA, pltpu.SemaphoreType.DMA],
        compiler_params=_cparams(("arbitrary",)),
    )(dmerged, *o_groups, *lse_groups, o_sb, gl, b_gate, w_up_dil, w_up_sb, swap)


ANY = pl.BlockSpec(memory_space=pl.ANY)


def _place():
    x, y, c = lax.axis_index("x"), lax.axis_index("y"), lax.axis_index("c")
    other_chips = [(1 - x, y), (x, 1 - y), (1 - x, 1 - y)]
    return x, y, c, other_chips


def _gather_copies(p_ref, out_ref, send_sems, recv_sems):
    x, y, c, chips = _place()
    me, sibling = 2 * x + y, (x, y, 1 - c)
    idx = [2 * chip[0] + chip[1] for chip in chips]

    def copy(k, chip_idx, core, to, src=None):
        return pltpu.make_async_remote_copy(
            src_ref=out_ref.at[chip_idx, core] if src is None else src, dst_ref=out_ref.at[chip_idx, core],
            send_sem=send_sems.at[k], recv_sem=recv_sems.at[k], device_id=to, device_id_type=MESH)

    first = lambda j: copy(j, me, c, (*chips[j], c), src=p_ref.at[c])
    landed = lambda j: copy(j, idx[j], c, (x, y, c))
    passed = lambda j: copy(3 + j, idx[j], c, sibling)
    handed = lambda j: copy(3 + j, idx[j], 1 - c, (x, y, c))
    return first, landed, passed, handed


def _gather_start(*refs):
    first = _gather_copies(*refs)[0]
    for j in range(3):
        first(j).start()


def _gather_pass_on(*refs):
    _, landed, passed, _ = _gather_copies(*refs)
    for j in range(3):
        landed(j).wait_recv()
        passed(j).start()


def _gather_finish(*refs):
    first, _, passed, handed = _gather_copies(*refs)
    for j in range(3):
        handed(j).wait_recv()
    for j in range(3):
        first(j).wait_send()
        passed(j).wait_send()


def _fill_own_slot(others, pack):
    n, _, rh, wd = others.shape
    me = 2 * lax.axis_index("x") + lax.axis_index("y")
    mine = lax.broadcasted_iota(jnp.int32, (n, 1, 1, 1), 0) == me
    return jnp.where(mine, pack.reshape(1, 2, rh, wd), others).reshape(n, 2 * rh, wd)


def _all_gather_weights(pack):
    r, wd = pack.shape
    rh = r // 2

    def body(p_ref, out_ref, send_sems, recv_sems):
        _gather_start(p_ref, out_ref, send_sems, recv_sems)
        _gather_pass_on(p_ref, out_ref, send_sems, recv_sems)
        _gather_finish(p_ref, out_ref, send_sems, recv_sems)

    others = pl.pallas_call(
        body,
        name="all_gather_weights",
        in_specs=[ANY],
        out_specs=ANY,
        out_shape=jax.ShapeDtypeStruct((N_CHIPS, 2, rh, wd), pack.dtype),
        scratch_shapes=[pltpu.SemaphoreType.DMA((6,)), pltpu.SemaphoreType.DMA((6,))],
    )(pack.reshape(2, rh, wd))
    return _fill_own_slot(others, pack)


def _swap_copy(g_ref, out_ref, send_sem, recv_sem):
    x, y, c, _ = _place()
    return pltpu.make_async_remote_copy(
        src_ref=g_ref.at[:, 1 - c], dst_ref=out_ref,
        send_sem=send_sem, recv_sem=recv_sem, device_id=(x, y, 1 - c), device_id_type=MESH)


def _swap_halves(g):
    n, r, wd = g.shape
    rh = r // 2
    g = g.reshape(n, 2, rh, wd)

    def body(g_ref, out_ref, send_sem, recv_sem):
        cp = _swap_copy(g_ref, out_ref, send_sem, recv_sem)
        cp.start()
        cp.wait()

    return pl.pallas_call(
        body,
        name="grad_swap_halves",
        in_specs=[ANY],
        out_specs=ANY,
        out_shape=jax.ShapeDtypeStruct((n, rh, wd), g.dtype),
        scratch_shapes=[pltpu.SemaphoreType.DMA, pltpu.SemaphoreType.DMA],
    )(g)


def _add_halves(g, got, core):
    n, r, wd = g.shape
    rh = r // 2
    t = rh // 4
    nt = rh // t

    def body(c_ref, a_ref, b_ref, o_ref):
        o_ref[...] = (a_ref[0] + b_ref[...]).astype(BF16)

    grid_spec = pltpu.PrefetchScalarGridSpec(
        num_scalar_prefetch=1,
        grid=(n, nt),
        in_specs=[pl.BlockSpec((1, 1, t, wd), lambda s, i, c: (s, c[0], i, 0)),
                  pl.BlockSpec((1, t, wd), lambda s, i, c: (s, i, 0))],
        out_specs=pl.BlockSpec((1, t, wd), lambda s, i, c: (s, i, 0)),
    )
    return pl.pallas_call(
        body,
        name="grad_add_halves",
        grid_spec=grid_spec,
        out_shape=jax.ShapeDtypeStruct((n, rh, wd), BF16),
        compiler_params=_cparams(("parallel", "parallel")),
    )(core, g.reshape(n, 2, rh, wd), got)


def _exchange_copies(h_ref, out_ref, send_sems, recv_sems):
    x, y, c, chips = _place()
    me = 2 * x + y

    def copy(j, slot):
        them = 2 * chips[j][0] + chips[j][1]
        return pltpu.make_async_remote_copy(
            src_ref=h_ref.at[them], dst_ref=out_ref.at[me if slot == "mine" else them],
            send_sem=send_sems.at[j], recv_sem=recv_sems.at[j], device_id=(*chips[j], c), device_id_type=MESH)

    return (lambda j: copy(j, "mine")), (lambda j: copy(j, "theirs"))


def _exchange_start(h_ref, out_ref, send_sems, recv_sems):
    send = _exchange_copies(h_ref, out_ref, send_sems, recv_sems)[0]
    for j in range(3):
        send(j).start()


def _exchange_wait(h_ref, out_ref, send_sems, recv_sems):
    send, arrival = _exchange_copies(h_ref, out_ref, send_sems, recv_sems)
    for j in range(3):
        arrival(j).wait_recv()
    for j in range(3):
        send(j).wait_send()


def _sum_chips(b, h, chip):
    n, rh, wd = b.shape
    t = rh // 4

    def body(chip_ref, b_ref, own_ref, o_ref):
        own = own_ref[0]
        s0, s1, s2, s3 = (jnp.where(chip_ref[0] == k, own, b_ref[k]).astype(F32) for k in range(n))
        o_ref[...] = ((s0 + s1) + s2) + s3

    grid_spec = pltpu.PrefetchScalarGridSpec(
        num_scalar_prefetch=1,
        grid=(rh // t,),
        in_specs=[pl.BlockSpec((n, t, wd), lambda i, chip: (0, i, 0)),
                  pl.BlockSpec((1, t, wd), lambda i, chip: (chip[0], i, 0))],
        out_specs=pl.BlockSpec((t, wd), lambda i, chip: (i, 0)),
    )
    return pl.pallas_call(
        body,
        name="grad_sum_chips",
        grid_spec=grid_spec,
        out_shape=jax.ShapeDtypeStruct((rh, wd), F32),
        compiler_params=_cparams(("parallel",)),
    )(chip, b, h)


def _join_halves(tc):
    rh, wd = tc.shape

    def body(t_ref, out_ref, send_sem, recv_sem):
        x, y, c, _ = _place()
        cp = pltpu.make_async_remote_copy(
            src_ref=t_ref, dst_ref=out_ref.at[c],
            send_sem=send_sem, recv_sem=recv_sem, device_id=(x, y, 1 - c), device_id_type=MESH)
        cp.start()
        cp.wait()

    halves = pl.pallas_call(
        body,
        name="grad_join_halves",
        in_specs=[ANY],
        out_specs=ANY,
        out_shape=jax.ShapeDtypeStruct((2, rh, wd), tc.dtype),
        scratch_shapes=[pltpu.SemaphoreType.DMA, pltpu.SemaphoreType.DMA],
    )(tc)
    return lax.dynamic_update_slice(halves, tc[None], (lax.axis_index("c"), 0, 0)).reshape(2 * rh, wd)


def _all_reduce_small(pack):
    rows, lanes = pack.shape

    def body(p_ref, out_ref, buf, send_sems, recv_sems):
        x, y, c, _ = _place()
        me = 4 * x + 2 * y + c
        buf[me] = p_ref[...]
        sends = []
        for k in range(1, N_DEV):
            peer = (x ^ (k >> 2), y ^ ((k >> 1) & 1), c ^ (k & 1))
            sends.append(pltpu.make_async_remote_copy(
                src_ref=p_ref, dst_ref=buf.at[me], send_sem=send_sems.at[k - 1], recv_sem=recv_sems.at[k - 1],
                device_id=peer, device_id_type=MESH))
        for cp in sends:
            cp.start()
        for k in range(1, N_DEV):
            pltpu.make_async_remote_copy(
                src_ref=p_ref, dst_ref=buf.at[me ^ k], send_sem=send_sems.at[k - 1], recv_sem=recv_sems.at[k - 1],
                device_id=(x, y, c), device_id_type=MESH).wait_recv()
        for cp in sends:
            cp.wait_send()
        total = buf[0]
        for d in range(1, N_DEV):
            total = total + buf[d]
        out_ref[...] = total

    vm = pl.BlockSpec(memory_space=pltpu.VMEM)
    return pl.pallas_call(
        body,
        name="all_reduce_small",
        in_specs=[vm],
        out_specs=vm,
        out_shape=jax.ShapeDtypeStruct((rows, lanes), F32),
        scratch_shapes=[pltpu.VMEM((N_DEV, rows, lanes), F32), pltpu.SemaphoreType.DMA((N_DEV - 1,)),
                        pltpu.SemaphoreType.DMA((N_DEV - 1,))],
    )(pack)


def _adamw(g, w, m, v, name):
    rows, cols = g.shape
    t = rows
    for cand in (256, 128, 64, 32, 16, 8):
        if rows % cand == 0:
            t = cand
            break

    def body(g_ref, w_ref, m_ref, v_ref, d_ref, nm_ref, nv_ref):
        gv = g_ref[...]
        mv = ADAM_B1 * m_ref[...] + (1.0 - ADAM_B1) * gv
        vv = ADAM_B2 * v_ref[...] + (1.0 - ADAM_B2) * (gv * gv)
        m_hat = mv / (1.0 - ADAM_B1 ** ADAM_STEP)
        v_hat = vv / (1.0 - ADAM_B2 ** ADAM_STEP)
        d_ref[...] = -ADAM_LR * (m_hat / (jnp.sqrt(v_hat) + ADAM_EPS) + ADAM_WD * w_ref[...])
        nm_ref[...] = mv
        nv_ref[...] = vv

    blk = pl.BlockSpec((t, cols), lambda i: (i, 0))
    return pl.pallas_call(
        body,
        name=name,
        grid=(rows // t,),
        in_specs=[blk] * 4,
        out_specs=[blk] * 3,
        out_shape=[jax.ShapeDtypeStruct((rows, cols), F32)] * 3,
        compiler_params=_cparams(("parallel",)),
    )(g, w, m, v)


PACK_W = 1024
BIG = (("w_in", (D_MODEL, IN_COLS), 1), ("w_up_dil", (DIL_W, D_MODEL), 1), ("w_up_sb", (SB_W, D_MODEL), 1),
       ("w_out", (D_MODEL, D_MODEL), 0), ("w_mlp_in", (D_MODEL, D_FF), 1), ("w_mlp_out", (D_FF, D_MODEL), 0))


def _shard_shape(shape, axis):
    return tuple(d // N_CHIPS if a == axis else d for a, d in enumerate(shape))


MIXER_GROUP, MLP_GROUP = BIG[:4], BIG[4:]


def _pack_rows(group=BIG):
    rows, at = {}, 0
    for name, shape, axis in group:
        n = math.prod(_shard_shape(shape, axis)) // PACK_W
        rows[name] = (at, n)
        at += n
    return rows, at


def _pack_shards(shards, group):
    return jnp.concatenate([shards[name].reshape(-1, PACK_W) for name, _, _ in group], axis=0)


def _unpack_full(gathered, group):
    rows, _ = _pack_rows(group)
    full = {}
    for name, shape, axis in group:
        at, n = rows[name]
        parts = gathered[:, at:at + n, :].reshape((N_CHIPS,) + _shard_shape(shape, axis))
        if axis == 0:
            full[name] = parts.reshape(shape)
        else:
            full[name] = jnp.transpose(parts, (1, 0, 2)).reshape(shape)
    return full


def _pack_full_grads(grads, group):
    chunks = []
    for name, shape, axis in group:
        g = grads[name]
        if axis == 0:
            parts = g.reshape((N_CHIPS, shape[0] // N_CHIPS, shape[1]))
        else:
            parts = jnp.transpose(g.reshape((shape[0], N_CHIPS, shape[1] // N_CHIPS)), (1, 0, 2))
        chunks.append(parts.reshape(N_CHIPS, -1, PACK_W))
    return jnp.concatenate(chunks, axis=1)


def _unpack_shard(packed, group):
    rows, _ = _pack_rows(group)
    return {name: packed[rows[name][0]:rows[name][0] + rows[name][1]].reshape(_shard_shape(shape, axis))
            for name, shape, axis in group}


def _local_step(x, target, w, mlp_shards, norm_mix_g, b_gate, norm_mlp_g, norm_final_g, core):
    w_in = w["w_in"]
    sb0 = 9 * DIL_W
    w_sb, w_gate = w_in[:, sb0:QKV_W], w_in[:, QKV_W:]
    w_dil = [jnp.concatenate([w_in[:, (3 * i + g) * DIL_W:(3 * i + g + 1) * DIL_W] for i in range(3)], axis=1)
             for g in range(3)]

    h = _rms_fwd(x, norm_mix_g, "norm_mix")
    qkv_dil = [_matmul(h, w_dil[g], mode="nn", out_dtypes=(BF16,), name=f"proj_dil_g{g}", tn=768)[0] for g in range(3)]
    (qkv_sb,) = _matmul(h, w_sb, mode="nn", out_dtypes=(BF16,), name="proj_sb", tn=768)
    (gl,) = _matmul(h, w_gate, mode="nn", out_dtypes=(F32,), name="proj_gate")
    dil = [_dil_fwd(qkv_dil[g], g) for g in range(3)]
    o_groups, lse_groups = [d[0] for d in dil], [d[1] for d in dil]
    o_sb, a_sb, mlp_others = _sb_fwd(qkv_sb, mlp_shards)
    w = {**w, **_unpack_full(_fill_own_slot(mlp_others, mlp_shards), MLP_GROUP)}
    merged, o_a = _merge_fwd(o_groups, lse_groups, o_sb, gl, b_gate, w["w_up_dil"], w["w_up_sb"])
    def residual_and_norm(acc, res, g):
        x1 = res + acc
        return x1, _rms_rows(x1)[0] * g

    x1, h2 = _matmul(merged, w["w_out"], mode="nn", out_dtypes=(F32, BF16), name="out_proj", tm=ROW_TILE,
                     extras=(x, norm_mlp_g), epilogue=residual_and_norm)
    u, act = _matmul(h2, w["w_mlp_in"], mode="nn", out_dtypes=(BF16, BF16), name="mlp_in",
                     epilogue=lambda acc: (acc, jnp.square(jnp.maximum(acc, 0.0))))

    def residual_and_loss(acc, res, tgt, g):
        xh, r = _rms_rows(res + acc)
        err = xh * g - tgt
        dy = err * (1.0 / D_MODEL)
        dxh = dy * g
        dx2 = r * (dxh - xh * jnp.mean(dxh * xh, axis=-1, keepdims=True))
        return dx2, _rows_sum8(dy * xh), (0.5 / D_MODEL) * _rows_sum8(err * err)

    dx2, dg_final, loss_part = _matmul(
        act, w["w_mlp_out"], mode="nn", out_dtypes=(F32, ("part", F32), ("part", F32)), name="mlp_out", tm=ROW_TILE,
        tk=2048, extras=(x1, target, norm_final_g.reshape(1, D_MODEL)), epilogue=residual_and_loss)

    (du,) = _matmul(dx2, w["w_mlp_out"], mode="nt", out_dtypes=(BF16,), name="mlp_out_dx",
                    extras=(u,), epilogue=lambda acc, uu: (acc * (2.0 * jnp.maximum(uu.astype(F32), 0.0)),))
    (g_mlp_out,) = _matmul(act, dx2, mode="tn", out_dtypes=(F32,), name="mlp_out_dw")
    (g_mlp_in,) = _matmul(h2, du, mode="tn", out_dtypes=(F32,), name="mlp_in_dw")

    def norm_bwd(acc, xx, dres, g):
        dx, dg = _rms_bwd_rows(acc, xx, g)
        return dres + dx, dg

    dx1, dg_mlp = _matmul(du, w["w_mlp_in"], mode="nt", out_dtypes=(F32, ("part", F32)), name="mlp_in_dx",
                          tm=ROW_TILE, tk=2048, extras=(x1, dx2, norm_mlp_g), epilogue=norm_bwd)

    mlp_pack = _pack_full_grads({"w_mlp_in": g_mlp_in, "w_mlp_out": g_mlp_out}, MLP_GROUP)
    (dmerged,) = _matmul(dx1, w["w_out"], mode="nt", out_dtypes=(F32,), name="out_proj_dx")
    (g_out,) = _matmul(merged, dx1, mode="tn", out_dtypes=(F32,), name="out_proj_dw")
    mb = _merge_bwd(dmerged, o_groups, lse_groups, o_sb, gl, b_gate, w["w_up_dil"], w["w_up_sb"], mlp_pack)
    dua, dub, dgl, dbg, do_sb = mb[:5]
    do_groups, c_groups = mb[5:8], mb[8:11]
    mlp_sums = _add_halves(mlp_pack, mb[11], core)
    (g_up_dil,) = _matmul(o_a, dua, mode="tn", out_dtypes=(F32,), name="up_dil_dw")
    (g_up_sb,) = _matmul(o_sb, dub, mode="tn", out_dtypes=(F32,), name="up_sb_dw")
    dq_sb, dk_sb, dv_sb, mlp_got = _sb_bwd(qkv_sb, do_sb, a_sb, mlp_sums)
    dil_b = [_dil_bwd(qkv_dil[g], do_groups[g], lse_groups[g], c_groups[g], g) for g in range(3)]
    dproj = jnp.concatenate(
        [dil_b[g][i].astype(BF16) for i in range(3) for g in range(3)]
        + [t.astype(BF16) for t in (dq_sb, dk_sb, dv_sb)] + [dgl], axis=1)
    (g_in,) = _matmul(h, dproj, mode="tn", out_dtypes=(F32,), name="proj_dw", tm=512, tn=IN_COLS // 2)
    mixer_pack = _pack_full_grads({"w_in": g_in, "w_up_dil": g_up_dil, "w_up_sb": g_up_sb, "w_out": g_out}, MIXER_GROUP)
    mixer_sums = _add_halves(mixer_pack, _swap_halves(mixer_pack), core)
    grad_x, dg_mix, mixer_got = _matmul(
        dproj, w["w_in"], mode="nt", out_dtypes=(F32, ("part", F32)), name="proj_dx", tm=ROW_TILE, tk=IN_COLS // 2,
        extras=(x, dx1, norm_mix_g), epilogue=norm_bwd, exchange=mixer_sums)

    small = (dg_mix, dbg, dg_mlp, dg_final, loss_part)
    return grad_x, (mixer_got, mixer_sums), (mlp_got, mlp_sums), small


def kernel(x, norm_mix_g, w_in, b_gate, w_up_dil, w_up_sb, w_out, norm_mlp_g, w_mlp_in, w_mlp_out, norm_final_g, loss_target, m_norm_mix_g, m_w_in, m_b_gate, m_w_up_dil, m_w_up_sb, m_w_out, m_norm_mlp_g, m_w_mlp_in, m_w_mlp_out, m_norm_final_g, v_norm_mix_g, v_w_in, v_b_gate, v_w_up_dil, v_w_up_sb, v_w_out, v_norm_mlp_g, v_w_mlp_in, v_w_mlp_out, v_norm_final_g):
    shards = {"w_in": w_in[0], "w_up_dil": w_up_dil[0], "w_up_sb": w_up_sb[0], "w_out": w_out[0],
              "w_mlp_in": w_mlp_in[0], "w_mlp_out": w_mlp_out[0]}
    moments_m = {"w_in": m_w_in[0], "w_up_dil": m_w_up_dil[0], "w_up_sb": m_w_up_sb[0], "w_out": m_w_out[0],
                 "w_mlp_in": m_w_mlp_in[0], "w_mlp_out": m_w_mlp_out[0]}
    moments_v = {"w_in": v_w_in[0], "w_up_dil": v_w_up_dil[0], "w_up_sb": v_w_up_sb[0], "w_out": v_w_out[0],
                 "w_mlp_in": v_w_mlp_in[0], "w_mlp_out": v_w_mlp_out[0]}

    shards16 = {n: s.astype(BF16) for n, s in shards.items()}
    full = _unpack_full(_all_gather_weights(_pack_shards(shards16, MIXER_GROUP)), MIXER_GROUP)
    mlp_shards = _pack_shards(shards16, MLP_GROUP)

    core = lax.axis_index("c").astype(jnp.int32).reshape(1)
    chip = (2 * lax.axis_index("x") + lax.axis_index("y")).astype(jnp.int32).reshape(1)
    grad_x, (mixer_got, mixer_sums), (mlp_got, mlp_sums), small = _local_step(
        x[0], loss_target[0], full, mlp_shards, norm_mix_g, b_gate, norm_mlp_g, norm_final_g, core)

    reduced = _join_halves(_sum_chips(mixer_got, mixer_sums, chip))
    reduced_mlp = _join_halves(_sum_chips(mlp_got, mlp_sums, chip))
    g_shard = {**_unpack_shard(reduced, MIXER_GROUP), **_unpack_shard(reduced_mlp, MLP_GROUP)}

    dg_mix, dbg, dg_mlp, dg_final, loss_part = small
    loss_row = jnp.sum(loss_part, axis=0, keepdims=True)
    small_pack = jnp.concatenate(
        [jnp.sum(dg_mix, axis=0, keepdims=True), jnp.sum(dbg, axis=0, keepdims=True),
         jnp.sum(dg_mlp, axis=0, keepdims=True), jnp.sum(dg_final, axis=0, keepdims=True), loss_row], axis=1)
    n_small = small_pack.shape[1]
    small_sum = _all_reduce_small(small_pack.reshape(n_small // 128, 128)).reshape(1, n_small)
    g_norm_mix = small_sum[:, :D_MODEL]
    g_b_gate = small_sum[:, D_MODEL:3 * D_MODEL]
    g_norm_mlp = small_sum[:, 3 * D_MODEL:4 * D_MODEL]
    g_norm_final = small_sum[:, 4 * D_MODEL:5 * D_MODEL]
    loss = jnp.sum(small_sum[:, 5 * D_MODEL:])

    names = ["norm_mix_g", "w_in", "b_gate", "w_up_dil", "w_up_sb", "w_out", "norm_mlp_g", "w_mlp_in", "w_mlp_out",
             "norm_final_g"]
    grads = dict(g_shard)
    grads.update(norm_mix_g=g_norm_mix, b_gate=g_b_gate, norm_mlp_g=g_norm_mlp, norm_final_g=g_norm_final)
    weights = dict(shards)
    weights.update(norm_mix_g=norm_mix_g, b_gate=b_gate, norm_mlp_g=norm_mlp_g, norm_final_g=norm_final_g.reshape(1, D_MODEL))
    ms = dict(moments_m)
    ms.update(norm_mix_g=m_norm_mix_g, b_gate=m_b_gate, norm_mlp_g=m_norm_mlp_g, norm_final_g=m_norm_final_g.reshape(1, D_MODEL))
    vs = dict(moments_v)
    vs.update(norm_mix_g=v_norm_mix_g, b_gate=v_b_gate, norm_mlp_g=v_norm_mlp_g, norm_final_g=v_norm_final_g.reshape(1, D_MODEL))

    out_shapes = {"norm_mix_g": norm_mix_g.shape, "w_in": w_in.shape, "b_gate": b_gate.shape, "w_up_dil": w_up_dil.shape,
                  "w_up_sb": w_up_sb.shape, "w_out": w_out.shape, "norm_mlp_g": norm_mlp_g.shape,
                  "w_mlp_in": w_mlp_in.shape, "w_mlp_out": w_mlp_out.shape, "norm_final_g": norm_final_g.shape}
    g_out, d_out, m_out, v_out = [], [], [], []
    for n in names:
        d, nm, nv = _adamw(grads[n], weights[n], ms[n], vs[n], "adamw_" + n)
        shape = out_shapes[n]
        g_out.append(grads[n].reshape(shape))
        d_out.append(d.reshape(shape))
        m_out.append(nm.reshape(shape))
        v_out.append(nv.reshape(shape))
    return (loss, grad_x.reshape(x.shape), *g_out, *d_out, *m_out, *v_out)
```

```python
import functools
import math

import jax
import jax.numpy as jnp
import numpy as np
from jax import lax
from jax.experimental import pallas as pl
from jax.experimental.pallas import tpu as pltpu

F32 = jnp.float32
BF16 = jnp.bfloat16
MESH = pl.DeviceIdType.MESH

D_MODEL = 1024
HEAD_DIM = 64
DIL_GROUPS = ((128, 1), (512, 4), (2048, 16))
DIL_HEADS = 4
DIL_W = 256
N_DIL_HEADS = 12
SB_HEADS = 8
SB_W = SB_HEADS * HEAD_DIM
QKV_W = 3 * 3 * DIL_W + 3 * SB_W
GATE_W = 2 * D_MODEL
IN_COLS = QKV_W + GATE_W
D_FF = 4 * D_MODEL
BLOCK = 128
RMS_EPS = 1e-6
NEG_INF = -1e30
N_CHIPS = 4
N_DEV = 8

ADAM_LR = 0.001
ADAM_B1 = 0.9
ADAM_B2 = 0.999
ADAM_EPS = 1e-08
ADAM_WD = 0.01
ADAM_STEP = 10

VMEM_LIMIT = 56 * 1024 * 1024

SB_BQ = 256
SB_BK = 256


def _cparams(sem=None):
    if sem is None:
        return pltpu.CompilerParams(vmem_limit_bytes=VMEM_LIMIT)
    return pltpu.CompilerParams(dimension_semantics=sem, vmem_limit_bytes=VMEM_LIMIT)


def _dot(a, b, dims):
    return lax.dot_general(a, b, (dims, ((), ())), preferred_element_type=F32)


def _dot_nn(a, b):
    return _dot(a, b, ((1,), (0,)))


def _dot_nt(a, b):
    return _dot(a, b, ((1,), (1,)))


def _dot_tn(a, b):
    return _dot(a, b, ((0,), (0,)))


def _dot_f32_by_01(x, m01, pieces=3):
    hi = x.astype(BF16)
    if pieces == 1:
        return _dot_nn(hi, m01)
    r1 = x - hi.astype(F32)
    mid = r1.astype(BF16)
    if pieces == 2:
        return _dot_nn(hi, m01) + _dot_nn(mid, m01)
    lo = (r1 - mid.astype(F32)).astype(BF16)
    return _dot_nn(hi, m01) + _dot_nn(mid, m01) + _dot_nn(lo, m01)


def _matmul(a, b, *, mode, out_dtypes, name, tm=1024, tn=1024, tk=1024, extras=(), epilogue=None, exchange=None):
    if mode == "nn":
        (m, k), (k2, n) = a.shape, b.shape
    elif mode == "nt":
        (m, k), (n, k2) = a.shape, b.shape
    else:
        (k, m), (k2, n) = a.shape, b.shape
    assert k == k2, (a.shape, b.shape, mode)
    tm, tn, tk = min(tm, m), min(tn, n), min(tk, k)
    assert m % tm == 0 and n % tn == 0 and k % tk == 0, (m, n, k, tm, tn, tk)
    nk = k // tk
    n_out = len(out_dtypes)
    n_ex = len(extras)

    if mode == "nn":
        a_spec = pl.BlockSpec((tm, tk), lambda i, j, kk: (i, kk))
        b_spec = pl.BlockSpec((tk, tn), lambda i, j, kk: (kk, j))
        dot = _dot_nn
    elif mode == "nt":
        a_spec = pl.BlockSpec((tm, tk), lambda i, j, kk: (i, kk))
        b_spec = pl.BlockSpec((tn, tk), lambda i, j, kk: (j, kk))
        dot = _dot_nt
    else:
        a_spec = pl.BlockSpec((tk, tm), lambda i, j, kk: (kk, i))
        b_spec = pl.BlockSpec((tk, tn), lambda i, j, kk: (kk, j))
        dot = _dot_tn
    mn_spec = pl.BlockSpec((tm, tn), lambda i, j, kk: (i, j))
    row_spec = pl.BlockSpec((1, tn), lambda i, j, kk: (0, j))
    part_spec = pl.BlockSpec((8, tn), lambda i, j, kk: (i, j))
    ex_specs = [row_spec if e.shape[0] == 1 else mn_spec for e in extras]
    is_part = [isinstance(dt, tuple) for dt in out_dtypes]
    out_dts = [dt[1] if p else dt for dt, p in zip(out_dtypes, is_part)]
    out_specs = [part_spec if p else mn_spec for p in is_part]
    out_shapes = [jax.ShapeDtypeStruct((8 * (m // tm), n) if p else (m, n), dt) for dt, p in zip(out_dts, is_part)]

    n_side = 0 if exchange is None else 1
    grid = (m // tm, n // tn, nk)

    def body(*refs):
        a_ref, b_ref = refs[0], refs[1]
        ex_refs = refs[2:2 + n_ex]
        out_refs = refs[2 + n_ex + n_side:2 + n_ex + n_side + n_out]
        scratch = refs[2 + n_ex + n_side + n_out + n_side:]
        acc_ref = scratch[0] if nk > 1 else None
        if exchange is not None:
            side = (refs[2 + n_ex], refs[2 + n_ex + n_side + n_out]) + tuple(scratch[-2:])
            step = (pl.program_id(0) * grid[1] + pl.program_id(1)) * grid[2] + pl.program_id(2)

            @pl.when(step == 0)
            def _():
                _exchange_start(*side)

            @pl.when(step == grid[0] * grid[1] * grid[2] - 1)
            def _():
                _exchange_wait(*side)

        part = dot(a_ref[...].astype(BF16), b_ref[...].astype(BF16))

        def finish(acc):
            if epilogue is None:
                outs = (acc,)
            else:
                outs = epilogue(acc, *[r[...] for r in ex_refs])
            for o_ref, o in zip(out_refs, outs):
                o_ref[...] = o.astype(o_ref.dtype)

        if nk == 1:
            finish(part)
        else:
            kk = pl.program_id(2)

            @pl.when(kk == 0)
            def _():
                acc_ref[...] = part

            @pl.when(kk > 0)
            def _():
                acc_ref[...] += part

            @pl.when(kk == nk - 1)
            def _():
                finish(acc_ref[...])

    side_in = [] if exchange is None else [exchange]
    outs = pl.pallas_call(
        body,
        name=name,
        grid=grid,
        in_specs=[a_spec, b_spec] + ex_specs + [ANY] * n_side,
        out_specs=out_specs + [ANY] * n_side,
        out_shape=out_shapes + [jax.ShapeDtypeStruct(e.shape, e.dtype) for e in side_in],
        scratch_shapes=([pltpu.VMEM((tm, tn), F32)] if nk > 1 else [])
        + [pltpu.SemaphoreType.DMA((3,)), pltpu.SemaphoreType.DMA((3,))] * n_side,
        compiler_params=_cparams(("arbitrary",) * 3 if n_side else ("parallel", "parallel", "arbitrary")),
    )(a, b, *extras, *side_in)
    return outs


ROW_TILE = 512


def _rows_sum8(t):
    rows, d = t.shape
    return jnp.sum(t.reshape(rows // 8, 8, d), axis=0)


def _rms_rows(x):
    r = lax.rsqrt(jnp.mean(x * x, axis=-1, keepdims=True) + RMS_EPS)
    return x * r, r


def _rms_bwd_rows(dh, x, g):
    xh, r = _rms_rows(x)
    dxh = dh * g
    return r * (dxh - xh * jnp.mean(dxh * xh, axis=-1, keepdims=True)), _rows_sum8(dh * xh)


def _rms_fwd(x, g, name):
    s, d = x.shape

    def body(x_ref, g_ref, h_ref):
        xv = x_ref[...]
        r = lax.rsqrt(jnp.mean(xv * xv, axis=-1, keepdims=True) + RMS_EPS)
        h_ref[...] = (xv * r * g_ref[...]).astype(BF16)

    return pl.pallas_call(
        body,
        name=name,
        grid=(s // ROW_TILE,),
        in_specs=[pl.BlockSpec((ROW_TILE, d), lambda i: (i, 0)), pl.BlockSpec((1, d), lambda i: (0, 0))],
        out_specs=pl.BlockSpec((ROW_TILE, d), lambda i: (i, 0)),
        out_shape=jax.ShapeDtypeStruct((s, d), BF16),
        compiler_params=_cparams(("parallel",)),
    )(x, g)


def _alibi_slopes():
    return np.exp2(np.float32(-8.0) * np.arange(1, N_DIL_HEADS + 1, dtype=np.float32) / np.float32(N_DIL_HEADS))


def _head_lane_mask(h, rows):
    lane = lax.broadcasted_iota(jnp.int32, (rows, DIL_W), 1)
    return (lane >= h * HEAD_DIM) & (lane < (h + 1) * HEAD_DIM)


def _band_terms(dil, has_prev):
    qi = lax.broadcasted_iota(jnp.int32, (BLOCK, 2 * BLOCK), 0)
    kj = lax.broadcasted_iota(jnp.int32, (BLOCK, 2 * BLOCK), 1)
    steps = qi + BLOCK - kj
    valid = (steps >= 0) & (steps <= BLOCK) & ((kj >= BLOCK) | has_prev)
    return valid, steps.astype(F32) * float(dil)


def _load_halves(ref, rows):
    return jnp.concatenate([ref[0, rows, :], ref[1, rows, :]], axis=1)


def _store_halves(ref, rows, value):
    ref[0, rows, :] = value[:, :128]
    ref[1, rows, :] = value[:, 128:]


def _dil_fwd(qkv_g, group):
    _, dil = DIL_GROUPS[group]
    s = qkv_g.shape[0]
    sub = s // dil
    nb = sub // BLOCK
    view = qkv_g.reshape(sub, dil * 3 * DIL_W)
    slopes = _alibi_slopes()[group * DIL_HEADS:(group + 1) * DIL_HEADS]

    def col(which):
        return lambda n, r: (n, r * 3 + which)

    def col_prev(which):
        return lambda n, r: (jnp.maximum(n - 1, 0), r * 3 + which)

    def body(q_ref, kc_ref, kp_ref, vc_ref, vp_ref, o_ref, lse_ref):
        n, r = pl.program_id(0), pl.program_id(1)
        mine = pl.ds(r, BLOCK, stride=dil) if dil > 1 else slice(None)
        valid, dist = _band_terms(dil, n > 0)
        q = q_ref[...]
        k2 = jnp.concatenate([kp_ref[...], kc_ref[...]], axis=0)
        v2 = jnp.concatenate([vp_ref[...], vc_ref[...]], axis=0)
        masks = [_head_lane_mask(h, BLOCK) for h in range(DIL_HEADS)]
        logits = [_dot_nt(jnp.where(masks[h], q, jnp.zeros_like(q)), k2) for h in range(DIL_HEADS)]
        ps, lses = [], []
        for h in range(DIL_HEADS):
            lg = jnp.where(valid, logits[h] * 0.125 - float(slopes[h]) * dist, NEG_INF)
            mx = jnp.max(lg, axis=1, keepdims=True)
            lse = mx + jnp.log(jnp.sum(jnp.exp(lg - mx), axis=1, keepdims=True))
            ps.append(jnp.exp(lg - lse).astype(BF16))
            lses.append(lse)
        o_acc = jnp.zeros((BLOCK, DIL_W), F32)
        lse_acc = jnp.zeros((BLOCK, DIL_W), F32)
        for h in range(DIL_HEADS):
            o_acc = jnp.where(masks[h], _dot_nn(ps[h], v2), o_acc)
            lse_acc = jnp.where(masks[h], lses[h], lse_acc)
        _store_halves(o_ref, mine, o_acc)
        _store_halves(lse_ref, mine, lse_acc)

    blk = (BLOCK, DIL_W)
    return pl.pallas_call(
        body,
        name=f"dil_fwd_g{group}",
        grid=(nb, dil),
        in_specs=[pl.BlockSpec(blk, col(0)), pl.BlockSpec(blk, col(1)), pl.BlockSpec(blk, col_prev(1)),
                  pl.BlockSpec(blk, col(2)), pl.BlockSpec(blk, col_prev(2))],
        out_specs=[pl.BlockSpec((2, BLOCK * dil, 128), lambda n, r: (0, n, 0))] * 2,
        out_shape=[jax.ShapeDtypeStruct((2, s, 128), F32)] * 2,
        compiler_params=_cparams(("parallel", "arbitrary")),
    )(view, view, view, view, view)


def _dil_bwd(qkv, do, lse, cterm, group):
    _, dil = DIL_GROUPS[group]
    s = qkv.shape[0]
    sub = s // dil
    nb = sub // BLOCK
    view = qkv.reshape(sub, dil * 3 * DIL_W)
    slopes = _alibi_slopes()[group * DIL_HEADS:(group + 1) * DIL_HEADS]

    def col(which, shift):
        if shift == 0:
            return lambda n, r: (n, r * 3 + which)
        if shift < 0:
            return lambda n, r: (jnp.maximum(n - 1, 0), r * 3 + which)
        return lambda n, r: (jnp.minimum(n + 1, nb - 1), r * 3 + which)

    def own(shift):
        if shift == 0:
            return pl.BlockSpec((2, BLOCK * dil, 128), lambda n, r: (0, n, 0))
        return pl.BlockSpec((2, BLOCK * dil, 128), lambda n, r: (0, jnp.minimum(n + 1, nb - 1), 0))

    def body(q_ref, qn_ref, kc_ref, kp_ref, vc_ref, vp_ref, do_ref, don_ref, lse_ref, lsen_ref, c_ref, cn_ref,
             dq_ref, dk_ref, dv_ref):
        n, r = pl.program_id(0), pl.program_id(1)
        mine = pl.ds(r, BLOCK, stride=dil) if dil > 1 else slice(None)
        valid, dist = _band_terms(dil, n > 0)
        valid_n = _band_terms(dil, True)[0][:, :BLOCK] & (n < nb - 1)
        dist_n = dist[:, :BLOCK]
        q, qn = q_ref[...], qn_ref[...]
        kc, vc = kc_ref[...], vc_ref[...]
        k2 = jnp.concatenate([kp_ref[...], kc], axis=0)
        v2 = jnp.concatenate([vp_ref[...], vc], axis=0)
        dov, donv = _load_halves(do_ref, mine), _load_halves(don_ref, mine)
        lsev, lsenv = _load_halves(lse_ref, mine), _load_halves(lsen_ref, mine)
        cv, cnv = _load_halves(c_ref, mine), _load_halves(cn_ref, mine)
        masks = [_head_lane_mask(h, BLOCK) for h in range(DIL_HEADS)]

        def head_col(t, hm):
            return jnp.max(jnp.where(hm, t, NEG_INF), axis=1, keepdims=True)

        qhs = [jnp.where(hm, q, jnp.zeros_like(q)) for hm in masks]
        qnhs = [jnp.where(hm, qn, jnp.zeros_like(qn)) for hm in masks]
        dohs = [jnp.where(hm, dov, 0.0).astype(BF16) for hm in masks]
        donhs = [jnp.where(hm, donv, 0.0).astype(BF16) for hm in masks]
        logit = [_dot_nt(qhs[h], k2) for h in range(DIL_HEADS)]
        dp = [_dot_nt(dohs[h], v2) for h in range(DIL_HEADS)]
        logit_n = [_dot_nt(qnhs[h], kc) for h in range(DIL_HEADS)]
        dp_n = [_dot_nt(donhs[h], vc) for h in range(DIL_HEADS)]
        p16, dlog, pn16, dlog_n = [], [], [], []
        for h in range(DIL_HEADS):
            hm, slope = masks[h], float(slopes[h])
            p = jnp.where(valid, jnp.exp(logit[h] * 0.125 - slope * dist - head_col(lsev, hm)), 0.0)
            dlog.append((p * (dp[h] + head_col(cv, hm)) * 0.125).astype(BF16))
            p16.append(p.astype(BF16))
            pn = jnp.where(valid_n, jnp.exp(logit_n[h] * 0.125 - slope * dist_n - head_col(lsenv, hm)), 0.0)
            dlog_n.append((pn * (dp_n[h] + head_col(cnv, hm)) * 0.125).astype(BF16))
            pn16.append(pn.astype(BF16))
        dq_acc = jnp.zeros((BLOCK, DIL_W), F32)
        dk_acc = jnp.zeros((BLOCK, DIL_W), F32)
        dv_acc = jnp.zeros((BLOCK, DIL_W), F32)
        for h in range(DIL_HEADS):
            dq_acc = jnp.where(masks[h], _dot_nn(dlog[h], k2), dq_acc)
            dk_acc += _dot_tn(dlog[h][:, BLOCK:], qhs[h]) + _dot_tn(dlog_n[h], qnhs[h])
            dv_acc += _dot_tn(p16[h][:, BLOCK:], dohs[h]) + _dot_tn(pn16[h], donhs[h])
        dq_ref[...] = dq_acc.astype(BF16)
        dk_ref[...] = dk_acc.astype(BF16)
        dv_ref[...] = dv_acc.astype(BF16)

    blk = (BLOCK, DIL_W)
    outs = pl.pallas_call(
        body,
        name=f"dil_bwd_g{group}",
        grid=(nb, dil),
        in_specs=[pl.BlockSpec(blk, col(0, 0)), pl.BlockSpec(blk, col(0, 1)),
                  pl.BlockSpec(blk, col(1, 0)), pl.BlockSpec(blk, col(1, -1)),
                  pl.BlockSpec(blk, col(2, 0)), pl.BlockSpec(blk, col(2, -1)),
                  own(0), own(1), own(0), own(1), own(0), own(1)],
        out_specs=[pl.BlockSpec(blk, lambda n, r: (n, r))] * 3,
        out_shape=[jax.ShapeDtypeStruct((sub, dil * DIL_W), BF16)] * 3,
        compiler_params=_cparams(("parallel", "parallel")),
    )(view, view, view, view, view, view, do, do, lse, lse, cterm, cterm)
    return tuple(t.reshape(s, DIL_W) for t in outs)


SB_PAIRS = SB_HEADS // 2
SB_COL0 = 0
LOG2E = 1.4426950408889634


SB_EXP_CLAMP = 64.0


def _sb_softplus2(zs):
    t = 1.0 + jnp.exp2(jnp.minimum(zs, SB_EXP_CLAMP))
    return jnp.maximum(jnp.log(t) * LOG2E, zs)


def _sb_consts(nkb):
    row = lax.broadcasted_iota(jnp.int32, (SB_BQ, SB_BK), 0)
    colk = lax.broadcasted_iota(jnp.int32, (SB_BQ, SB_BK), 1)
    rr = lax.broadcasted_iota(jnp.int32, (SB_BK, SB_BK), 0)
    cc = lax.broadcasted_iota(jnp.int32, (SB_BK, SB_BK), 1)
    lane = lax.broadcasted_iota(jnp.int32, (SB_BQ, 128), 1)
    assert 2 * nkb <= 128
    return colk < row, rr, cc, lane < HEAD_DIM, lane


def _split_heads(t):
    first = lax.broadcasted_iota(jnp.int32, t.shape, 1) < HEAD_DIM
    zero = jnp.zeros_like(t)
    return jnp.where(first, t, zero), jnp.where(first, zero, t)


def _sb_fwd(qkv, shard_pack):
    s = qkv.shape[0]
    nq, nkb = s // SB_BQ, s // SB_BK
    zscale = LOG2E / math.sqrt(HEAD_DIM)
    r_pack, w_pack = shard_pack.shape

    def body(q_ref, k_ref, v_ref, pack_ref, o_ref, a_row, others_ref, zs_scr, a_scr, acc_scr, cl_scr,
             send_sems, recv_sems):
        i = pl.program_id(1)
        pair = pl.program_id(0)
        gather = (pack_ref, others_ref, send_sems, recv_sems)

        @pl.when((pair == 0) & (i == 0))
        def _():
            _gather_start(*gather)

        @pl.when((pair == 1) & (i == 0))
        def _():
            _gather_pass_on(*gather)

        @pl.when((pair == SB_PAIRS - 1) & (i == nq - 1))
        def _():
            _gather_finish(*gather)

        causal, rr, cc, _, _ = _sb_consts(nkb)
        later = (rr > cc).astype(BF16)
        qh = _split_heads(q_ref[...])

        def rows(j):
            return pl.ds(pl.multiple_of(j * SB_BK, SB_BK), SB_BK)

        def scores_to(slot, j):
            kb = k_ref[rows(j), :]
            for hh in range(2):
                zs_scr[slot, hh] = _dot_nt(qh[hh], kb) * zscale

        def weights(slot, j, masked):
            xs, sums, sufs = [], [], []
            for hh in range(2):
                zs = zs_scr[slot, hh]
                sp = _sb_softplus2(zs)
                if masked:
                    sp = jnp.where(causal, sp, 0.0)
                xs.append(zs - sp)
                sums.append(jnp.sum(sp, axis=1, keepdims=True))
                sufs.append(_dot_f32_by_01(sp, later, 2))
            for hh in range(2):
                cl = cl_scr[hh]
                a = jnp.exp2(xs[hh] - (sufs[hh] + jnp.concatenate([cl, cl], axis=1)))
                if masked:
                    a = jnp.where(causal, a, 0.0)
                a16 = a.astype(BF16)
                a_scr[slot, :, hh * SB_BK:(hh + 1) * SB_BK] = a16
                a_row[0, 0, j, :, hh * SB_BK:(hh + 1) * SB_BK] = a16
                cl_scr[hh] = cl + sums[hh]

        def add_av(slot, j):
            v0, v1 = _split_heads(v_ref[rows(j), :])
            acc_scr[...] += _dot_nn(a_scr[slot], jnp.concatenate([v0, v1], axis=0))

        acc_scr[...] = jnp.zeros_like(acc_scr)
        cl_scr[...] = jnp.zeros_like(cl_scr)
        scores_to(0, i)
        scores_to(1, jnp.maximum(i - 1, 0))
        weights(0, i, True)

        def step(j, prev, cur):
            scores_to(prev, jnp.maximum(j - 1, 0))
            add_av(prev, j + 1)
            weights(cur, j, False)

        def two_steps(u, _):
            j = i - 1 - 2 * u
            step(j, 0, 1)
            step(j - 1, 1, 0)
            return 0

        lax.fori_loop(0, i // 2, two_steps, 0)

        @pl.when(i % 2 == 1)
        def _():
            step(0, 0, 1)
            add_av(1, 0)

        @pl.when(i % 2 == 0)
        def _():
            add_av(0, 0)

        o_ref[...] = acc_scr[...]

    def full(which):
        return pl.BlockSpec((s, 128), lambda p, i: (0, SB_COL0 + 4 * which + p))

    return pl.pallas_call(
        body,
        name="sb_fwd",
        grid=(SB_PAIRS, nq),
        in_specs=[pl.BlockSpec((SB_BQ, 128), lambda p, i: (i, SB_COL0 + p)), full(1), full(2), ANY],
        out_specs=[pl.BlockSpec((SB_BQ, 128), lambda p, i: (i, p)),
                   pl.BlockSpec((1, 1, nkb, SB_BQ, 2 * SB_BK), lambda p, i: (p, i, 0, 0, 0)), ANY],
        out_shape=[jax.ShapeDtypeStruct((s, SB_W), F32),
                   jax.ShapeDtypeStruct((SB_PAIRS, nq, nkb, SB_BQ, 2 * SB_BK), BF16),
                   jax.ShapeDtypeStruct((N_CHIPS, 2, r_pack // 2, w_pack), shard_pack.dtype)],
        scratch_shapes=[pltpu.VMEM((2, 2, SB_BQ, SB_BK), F32), pltpu.VMEM((2, SB_BQ, 2 * SB_BK), BF16),
                        pltpu.VMEM((SB_BQ, 128), F32), pltpu.VMEM((2, SB_BQ, 128), F32),
                        pltpu.SemaphoreType.DMA((6,)), pltpu.SemaphoreType.DMA((6,))],
        compiler_params=_cparams(("arbitrary", "arbitrary")),
    )(qkv, qkv, qkv, shard_pack.reshape(2, r_pack // 2, w_pack))


def _sb_bwd(qkv, do, a_hbm, chip_sums):
    s = qkv.shape[0]
    nq, nkb = s // SB_BQ, s // SB_BK
    scale = 1.0 / math.sqrt(HEAD_DIM)
    zscale = LOG2E * scale

    def body(q_ref, k_ref, v_ref, do_ref, a_row, sums_ref, dq_ref, dk_ref, dv_ref, got_ref,
             zs_scr, da_scr, dz_scr, a_scr, cg_scr, send_sems, recv_sems):
        i = pl.program_id(1)
        pair = pl.program_id(0)
        first_step = (pair == 0) & (i == 0)
        last_step = (pair == SB_PAIRS - 1) & (i == nq - 1)

        @pl.when(first_step)
        def _():
            _exchange_start(sums_ref, got_ref, send_sems, recv_sems)

        @pl.when(i == 0)
        def _():
            dk_ref[...] = jnp.zeros_like(dk_ref)
            dv_ref[...] = jnp.zeros_like(dv_ref)

        causal, rr, cc, first, _ = _sb_consts(nkb)
        earlier = (rr < cc).astype(BF16)
        q2 = q_ref[...]
        qh = _split_heads(q2)
        do2 = do_ref[...].astype(BF16)
        doh = _split_heads(do2)

        def rows(j):
            return pl.ds(pl.multiple_of(j * SB_BK, SB_BK), SB_BK)

        def products_to(slot, j):
            kb, vb = k_ref[rows(j), :], v_ref[rows(j), :]
            for hh in range(2):
                zs_scr[slot, hh] = _dot_nt(qh[hh], kb) * zscale
                da_scr[slot, hh] = _dot_nt(doh[hh], vb)

        def by_head(t):
            return jnp.where(first, t[:SB_BK], t[SB_BK:])

        def apply(slot, j):
            k0, k1 = _split_heads(k_ref[rows(j), :])
            dq_ref[...] += _dot_nn(dz_scr[slot], jnp.concatenate([k0, k1], axis=0)) * scale
            dk_ref[rows(j), :] += by_head(_dot_tn(dz_scr[slot], q2)) * scale
            dv_ref[rows(j), :] += by_head(_dot_tn(a_scr[slot], do2))

        def grads(slot, j, masked):
            gs, gpres = [], []
            for hh in range(2):
                a16 = a_row[0, 0, j, :, hh * SB_BK:(hh + 1) * SB_BK]
                a_scr[slot, :, hh * SB_BK:(hh + 1) * SB_BK] = a16
                g = a16.astype(F32) * da_scr[slot, hh]
                gs.append(g)
                gpres.append(_dot_f32_by_01(g, earlier, 1))
            sigs = []
            for hh in range(2):
                zs = zs_scr[slot, hh]
                sigs.append(jnp.exp2(zs - _sb_softplus2(zs)))
            for hh in range(2):
                cg = cg_scr[hh]
                dz = gs[hh] - (gs[hh] + (gpres[hh] + jnp.concatenate([cg, cg], axis=1))) * sigs[hh]
                if masked:
                    dz = jnp.where(causal, dz, 0.0)
                dz_scr[slot, :, hh * SB_BK:(hh + 1) * SB_BK] = dz.astype(BF16)
                cg_scr[hh] = cg + jnp.sum(gs[hh], axis=1, keepdims=True)

        dq_ref[...] = jnp.zeros_like(dq_ref)
        cg_scr[...] = jnp.zeros_like(cg_scr)
        dz_scr[1] = jnp.zeros((SB_BQ, 2 * SB_BK), BF16)
        a_scr[1] = jnp.zeros((SB_BQ, 2 * SB_BK), BF16)
        products_to(0, 0)

        def step(j, cur, nxt):
            products_to(nxt, j + 1)
            apply(nxt, jnp.maximum(j - 1, 0))
            grads(cur, j, False)

        def two_steps(u, _):
            step(2 * u, 0, 1)
            step(2 * u + 1, 1, 0)
            return 0

        lax.fori_loop(0, i // 2, two_steps, 0)

        def last(cur, nxt):
            apply(nxt, jnp.maximum(i - 1, 0))
            grads(cur, i, True)
            apply(cur, i)

        @pl.when(i % 2 == 1)
        def _():
            step(i - 1, 0, 1)
            last(1, 0)

        @pl.when(i % 2 == 0)
        def _():
            last(0, 1)

        @pl.when(last_step)
        def _():
            _exchange_wait(sums_ref, got_ref, send_sems, recv_sems)

    def full(which):
        return pl.BlockSpec((s, 128), lambda p, i: (0, SB_COL0 + 4 * which + p))

    qblk = pl.BlockSpec((SB_BQ, 128), lambda p, i: (i, p))
    acc = pl.BlockSpec((s, 128), lambda p, i: (0, p))
    return pl.pallas_call(
        body,
        name="sb_bwd",
        grid=(SB_PAIRS, nq),
        in_specs=[pl.BlockSpec((SB_BQ, 128), lambda p, i: (i, SB_COL0 + p)), full(1), full(2), qblk,
                  pl.BlockSpec((1, 1, nkb, SB_BQ, 2 * SB_BK), lambda p, i: (p, i, 0, 0, 0)), ANY],
        out_specs=[qblk, acc, acc, ANY],
        out_shape=[jax.ShapeDtypeStruct((s, SB_W), F32)] * 3 + [jax.ShapeDtypeStruct(chip_sums.shape, chip_sums.dtype)],
        scratch_shapes=[pltpu.VMEM((2, 2, SB_BQ, SB_BK), F32), pltpu.VMEM((2, 2, SB_BQ, SB_BK), F32),
                        pltpu.VMEM((2, SB_BQ, 2 * SB_BK), BF16), pltpu.VMEM((2, SB_BQ, 2 * SB_BK), BF16),
                        pltpu.VMEM((2, SB_BQ, 128), F32),
                        pltpu.SemaphoreType.DMA((3,)), pltpu.SemaphoreType.DMA((3,))],
        compiler_params=_cparams(("arbitrary", "arbitrary")),
    )(qkv, qkv, qkv, do, a_hbm, chip_sums)


MERGE_TILE = 256


def _group_mix(lses):
    mx = jnp.maximum(jnp.maximum(lses[0], lses[1]), lses[2])
    es = [jnp.exp(t - mx) for t in lses]
    den = es[0] + es[1] + es[2]
    return [e / den for e in es]


def _merge_fwd(o_groups, lse_groups, o_sb, gl, b_gate, w_up_dil, w_up_sb):
    s = gl.shape[0]
    t = MERGE_TILE

    def body(o0, o1, o2, l0, l1, l2, ob_ref, gl_ref, bg_ref, wd_ref, ws_ref, merged_ref, oa_ref):
        rows = slice(None)
        w = _group_mix([_load_halves(l, rows) for l in (l0, l1, l2)])
        og = [_load_halves(o, rows) for o in (o0, o1, o2)]
        oa = (w[0] * og[0] + w[1] * og[1] + w[2] * og[2]).astype(BF16)
        ua = _dot_nn(oa, wd_ref[...])
        ub = _dot_nn(ob_ref[...].astype(BF16), ws_ref[...])
        gate = jax.nn.sigmoid(gl_ref[...] + bg_ref[...])
        merged_ref[...] = (gate[:, :D_MODEL] * ua + gate[:, D_MODEL:] * ub).astype(BF16)
        oa_ref[...] = oa

    dil = pl.BlockSpec((t, DIL_W), lambda i: (i, 0))
    halves = pl.BlockSpec((2, t, 128), lambda i: (0, i, 0))
    const = lambda shape: pl.BlockSpec(shape, lambda i: (0, 0))
    return pl.pallas_call(
        body,
        name="merge_fwd",
        grid=(s // t,),
        in_specs=[halves] * 6 + [pl.BlockSpec((t, SB_W), lambda i: (i, 0)), pl.BlockSpec((t, GATE_W), lambda i: (i, 0)),
                                 const((1, GATE_W)), const((DIL_W, D_MODEL)), const((SB_W, D_MODEL))],
        out_specs=[pl.BlockSpec((t, D_MODEL), lambda i: (i, 0)), dil],
        out_shape=[jax.ShapeDtypeStruct((s, D_MODEL), BF16), jax.ShapeDtypeStruct((s, DIL_W), BF16)],
        compiler_params=_cparams(("parallel",)),
    )(*o_groups, *lse_groups, o_sb, gl, b_gate, w_up_dil, w_up_sb)


def _merge_bwd(dmerged, o_groups, lse_groups, o_sb, gl, b_gate, w_up_dil, w_up_sb, swap):
    s = gl.shape[0]
    t = MERGE_TILE
    n_chunks, r_swap, w_swap = swap.shape
    swap = swap.reshape(n_chunks, 2, r_swap // 2, w_swap)

    def body(dm_ref, o0, o1, o2, l0, l1, l2, ob_ref, gl_ref, bg_ref, wd_ref, ws_ref, swap_ref,
             dua_ref, dub_ref, dgl_ref, dbg_ref, dosb_ref, d0, d1, d2, c0, c1, c2, got_ref, send_sem, recv_sem):
        i = pl.program_id(0)

        @pl.when(i == 0)
        def _():
            _swap_copy(swap_ref, got_ref, send_sem, recv_sem).start()

        @pl.when(i == pl.num_programs(0) - 1)
        def _():
            _swap_copy(swap_ref, got_ref, send_sem, recv_sem).wait()

        rows = slice(None)
        og = [_load_halves(o, rows) for o in (o0, o1, o2)]
        w = _group_mix([_load_halves(l, rows) for l in (l0, l1, l2)])
        oa = (w[0] * og[0] + w[1] * og[1] + w[2] * og[2]).astype(BF16)
        ua = _dot_nn(oa, wd_ref[...])
        ub = _dot_nn(ob_ref[...].astype(BF16), ws_ref[...])
        gate = jax.nn.sigmoid(gl_ref[...] + bg_ref[...])
        ga, gb = gate[:, :D_MODEL], gate[:, D_MODEL:]
        dm = dm_ref[...]
        dua = (dm * ga).astype(BF16)
        dub = (dm * gb).astype(BF16)
        dua_ref[...] = dua
        dub_ref[...] = dub
        dgl_a = dm * ua * ga * (1.0 - ga)
        dgl_b = dm * ub * gb * (1.0 - gb)
        dgl_ref[:, :D_MODEL] = dgl_a.astype(BF16)
        dgl_ref[:, D_MODEL:] = dgl_b.astype(BF16)
        part = jnp.concatenate([jnp.sum(dgl_a.reshape(t // 8, 8, D_MODEL), axis=0),
                                jnp.sum(dgl_b.reshape(t // 8, 8, D_MODEL), axis=0)], axis=1)

        @pl.when(i == 0)
        def _():
            dbg_ref[...] = part

        @pl.when(i > 0)
        def _():
            dbg_ref[...] += part

        dosb_ref[...] = _dot_nt(dub, ws_ref[...])
        doa = _dot_nt(dua, wd_ref[...])
        rr = lax.broadcasted_iota(jnp.int32, (DIL_W, DIL_W), 0) // HEAD_DIM
        cc = lax.broadcasted_iota(jnp.int32, (DIL_W, DIL_W), 1) // HEAD_DIM
        same_head = (rr == cc).astype(BF16)
        dw = [_dot_f32_by_01(doa * og[g], same_head) for g in range(3)]
        mean_dw = w[0] * dw[0] + w[1] * dw[1] + w[2] * dw[2]
        for g, (d_ref, c_ref) in enumerate(((d0, c0), (d1, c1), (d2, c2))):
            _store_halves(d_ref, rows, w[g] * doa)
            _store_halves(c_ref, rows, -w[g] * mean_dw)

    dil = pl.BlockSpec((2, t, 128), lambda i: (0, i, 0))
    wide = pl.BlockSpec((t, D_MODEL), lambda i: (i, 0))
    gate2 = pl.BlockSpec((t, GATE_W), lambda i: (i, 0))
    sbw = pl.BlockSpec((t, SB_W), lambda i: (i, 0))
    const = lambda shape: pl.BlockSpec(shape, lambda i: (0, 0))
    return pl.pallas_call(
        body,
        name="merge_bwd",
        grid=(s // t,),
        in_specs=[wide] + [dil] * 6 + [sbw, gate2, const((1, GATE_W)), const((DIL_W, D_MODEL)), const((SB_W, D_MODEL)),
                                       ANY],
        out_specs=[wide, wide, gate2, const((8, GATE_W)), sbw] + [dil] * 6 + [ANY],
        out_shape=[jax.ShapeDtypeStruct((s, D_MODEL), BF16), jax.ShapeDtypeStruct((s, D_MODEL), BF16),
                   jax.ShapeDtypeStruct((s, GATE_W), BF16), jax.ShapeDtypeStruct((8, GATE_W), F32),
                   jax.ShapeDtypeStruct((s, SB_W), F32)] + [jax.ShapeDtypeStruct((2, s, 128), F32)] * 6
        + [jax.ShapeDtypeStruct((n_chunks, r_swap // 2, w_swap), swap.dtype)],
        scratch_shapes=[pltpu.SemaphoreType.DMA, pltpu.SemaphoreType.DMA],
        compiler_params=_cparams(("arbitrary",)),
    )(dmerged, *o_groups, *lse_groups, o_sb, gl, b_gate, w_up_dil, w_up_sb, swap)


ANY = pl.BlockSpec(memory_space=pl.ANY)


def _place():
    x, y, c = lax.axis_index("x"), lax.axis_index("y"), lax.axis_index("c")
    other_chips = [(1 - x, y), (x, 1 - y), (1 - x, 1 - y)]
    return x, y, c, other_chips


def _gather_copies(p_ref, out_ref, send_sems, recv_sems):
    x, y, c, chips = _place()
    me, sibling = 2 * x + y, (x, y, 1 - c)
    idx = [2 * chip[0] + chip[1] for chip in chips]

    def copy(k, chip_idx, core, to, src=None):
        return pltpu.make_async_remote_copy(
            src_ref=out_ref.at[chip_idx, core] if src is None else src, dst_ref=out_ref.at[chip_idx, core],
            send_sem=send_sems.at[k], recv_sem=recv_sems.at[k], device_id=to, device_id_type=MESH)

    first = lambda j: copy(j, me, c, (*chips[j], c), src=p_ref.at[c])
    landed = lambda j: copy(j, idx[j], c, (x, y, c))
    passed = lambda j: copy(3 + j, idx[j], c, sibling)
    handed = lambda j: copy(3 + j, idx[j], 1 - c, (x, y, c))
    return first, landed, passed, handed


def _gather_start(*refs):
    first = _gather_copies(*refs)[0]
    for j in range(3):
        first(j).start()


def _gather_pass_on(*refs):
    _, landed, passed, _ = _gather_copies(*refs)
    for j in range(3):
        landed(j).wait_recv()
        passed(j).start()


def _gather_finish(*refs):
    first, _, passed, handed = _gather_copies(*refs)
    for j in range(3):
        handed(j).wait_recv()
    for j in range(3):
        first(j).wait_send()
        passed(j).wait_send()


def _fill_own_slot(others, pack):
    n, _, rh, wd = others.shape
    me = 2 * lax.axis_index("x") + lax.axis_index("y")
    mine = lax.broadcasted_iota(jnp.int32, (n, 1, 1, 1), 0) == me
    return jnp.where(mine, pack.reshape(1, 2, rh, wd), others).reshape(n, 2 * rh, wd)


def _all_gather_weights(pack):
    r, wd = pack.shape
    rh = r // 2

    def body(p_ref, out_ref, send_sems, recv_sems):
        _gather_start(p_ref, out_ref, send_sems, recv_sems)
        _gather_pass_on(p_ref, out_ref, send_sems, recv_sems)
        _gather_finish(p_ref, out_ref, send_sems, recv_sems)

    others = pl.pallas_call(
        body,
        name="all_gather_weights",
        in_specs=[ANY],
        out_specs=ANY,
        out_shape=jax.ShapeDtypeStruct((N_CHIPS, 2, rh, wd), pack.dtype),
        scratch_shapes=[pltpu.SemaphoreType.DMA((6,)), pltpu.SemaphoreType.DMA((6,))],
    )(pack.reshape(2, rh, wd))
    return _fill_own_slot(others, pack)


def _swap_copy(g_ref, out_ref, send_sem, recv_sem):
    x, y, c, _ = _place()
    return pltpu.make_async_remote_copy(
        src_ref=g_ref.at[:, 1 - c], dst_ref=out_ref,
        send_sem=send_sem, recv_sem=recv_sem, device_id=(x, y, 1 - c), device_id_type=MESH)


def _swap_halves(g):
    n, r, wd = g.shape
    rh = r // 2
    g = g.reshape(n, 2, rh, wd)

    def body(g_ref, out_ref, send_sem, recv_sem):
        cp = _swap_copy(g_ref, out_ref, send_sem, recv_sem)
        cp.start()
        cp.wait()

    return pl.pallas_call(
        body,
        name="grad_swap_halves",
        in_specs=[ANY],
        out_specs=ANY,
        out_shape=jax.ShapeDtypeStruct((n, rh, wd), g.dtype),
        scratch_shapes=[pltpu.SemaphoreType.DMA, pltpu.SemaphoreType.DMA],
    )(g)


def _add_halves(g, got, core):
    n, r, wd = g.shape
    rh = r // 2
    t = rh // 4
    nt = rh // t

    def body(c_ref, a_ref, b_ref, o_ref):
        o_ref[...] = (a_ref[0] + b_ref[...]).astype(BF16)

    grid_spec = pltpu.PrefetchScalarGridSpec(
        num_scalar_prefetch=1,
        grid=(n, nt),
        in_specs=[pl.BlockSpec((1, 1, t, wd), lambda s, i, c: (s, c[0], i, 0)),
                  pl.BlockSpec((1, t, wd), lambda s, i, c: (s, i, 0))],
        out_specs=pl.BlockSpec((1, t, wd), lambda s, i, c: (s, i, 0)),
    )
    return pl.pallas_call(
        body,
        name="grad_add_halves",
        grid_spec=grid_spec,
        out_shape=jax.ShapeDtypeStruct((n, rh, wd), BF16),
        compiler_params=_cparams(("parallel", "parallel")),
    )(core, g.reshape(n, 2, rh, wd), got)


def _exchange_copies(h_ref, out_ref, send_sems, recv_sems):
    x, y, c, chips = _place()
    me = 2 * x + y

    def copy(j, slot):
        them = 2 * chips[j][0] + chips[j][1]
        return pltpu.make_async_remote_copy(
            src_ref=h_ref.at[them], dst_ref=out_ref.at[me if slot == "mine" else them],
            send_sem=send_sems.at[j], recv_sem=recv_sems.at[j], device_id=(*chips[j], c), device_id_type=MESH)

    return (lambda j: copy(j, "mine")), (lambda j: copy(j, "theirs"))


def _exchange_start(h_ref, out_ref, send_sems, recv_sems):
    send = _exchange_copies(h_ref, out_ref, send_sems, recv_sems)[0]
    for j in range(3):
        send(j).start()


def _exchange_wait(h_ref, out_ref, send_sems, recv_sems):
    send, arrival = _exchange_copies(h_ref, out_ref, send_sems, recv_sems)
    for j in range(3):
        arrival(j).wait_recv()
    for j in range(3):
        send(j).wait_send()


def _sum_chips(b, h, chip):
    n, rh, wd = b.shape
    t = rh // 4

    def body(chip_ref, b_ref, own_ref, o_ref):
        own = own_ref[0]
        s0, s1, s2, s3 = (jnp.where(chip_ref[0] == k, own, b_ref[k]).astype(F32) for k in range(n))
        o_ref[...] = ((s0 + s1) + s2) + s3

    grid_spec = pltpu.PrefetchScalarGridSpec(
        num_scalar_prefetch=1,
        grid=(rh // t,),
        in_specs=[pl.BlockSpec((n, t, wd), lambda i, chip: (0, i, 0)),
                  pl.BlockSpec((1, t, wd), lambda i, chip: (chip[0], i, 0))],
        out_specs=pl.BlockSpec((t, wd), lambda i, chip: (i, 0)),
    )
    return pl.pallas_call(
        body,
        name="grad_sum_chips",
        grid_spec=grid_spec,
        out_shape=jax.ShapeDtypeStruct((rh, wd), F32),
        compiler_params=_cparams(("parallel",)),
    )(chip, b, h)


def _join_halves(tc):
    rh, wd = tc.shape

    def body(t_ref, out_ref, send_sem, recv_sem):
        x, y, c, _ = _place()
        cp = pltpu.make_async_remote_copy(
            src_ref=t_ref, dst_ref=out_ref.at[c],
            send_sem=send_sem, recv_sem=recv_sem, device_id=(x, y, 1 - c), device_id_type=MESH)
        cp.start()
        cp.wait()

    halves = pl.pallas_call(
        body,
        name="grad_join_halves",
        in_specs=[ANY],
        out_specs=ANY,
        out_shape=jax.ShapeDtypeStruct((2, rh, wd), tc.dtype),
        scratch_shapes=[pltpu.SemaphoreType.DMA, pltpu.SemaphoreType.DMA],
    )(tc)
    return lax.dynamic_update_slice(halves, tc[None], (lax.axis_index("c"), 0, 0)).reshape(2 * rh, wd)


def _all_reduce_small(pack):
    rows, lanes = pack.shape

    def body(p_ref, out_ref, buf, send_sems, recv_sems):
        x, y, c, _ = _place()
        me = 4 * x + 2 * y + c
        buf[me] = p_ref[...]
        sends = []
        for k in range(1, N_DEV):
            peer = (x ^ (k >> 2), y ^ ((k >> 1) & 1), c ^ (k & 1))
            sends.append(pltpu.make_async_remote_copy(
                src_ref=p_ref, dst_ref=buf.at[me], send_sem=send_sems.at[k - 1], recv_sem=recv_sems.at[k - 1],
                device_id=peer, device_id_type=MESH))
        for cp in sends:
            cp.start()
        for k in range(1, N_DEV):
            pltpu.make_async_remote_copy(
                src_ref=p_ref, dst_ref=buf.at[me ^ k], send_sem=send_sems.at[k - 1], recv_sem=recv_sems.at[k - 1],
                device_id=(x, y, c), device_id_type=MESH).wait_recv()
        for cp in sends:
            cp.wait_send()
        total = buf[0]
        for d in range(1, N_DEV):
            total = total + buf[d]
        out_ref[...] = total

    vm = pl.BlockSpec(memory_space=pltpu.VMEM)
    return pl.pallas_call(
        body,
        name="all_reduce_small",
        in_specs=[vm],
        out_specs=vm,
        out_shape=jax.ShapeDtypeStruct((rows, lanes), F32),
        scratch_shapes=[pltpu.VMEM((N_DEV, rows, lanes), F32), pltpu.SemaphoreType.DMA((N_DEV - 1,)),
                        pltpu.SemaphoreType.DMA((N_DEV - 1,))],
    )(pack)


def _adamw(g, w, m, v, name):
    rows, cols = g.shape
    t = rows
    for cand in (256, 128, 64, 32, 16, 8):
        if rows % cand == 0:
            t = cand
            break

    def body(g_ref, w_ref, m_ref, v_ref, d_ref, nm_ref, nv_ref):
        gv = g_ref[...]
        mv = ADAM_B1 * m_ref[...] + (1.0 - ADAM_B1) * gv
        vv = ADAM_B2 * v_ref[...] + (1.0 - ADAM_B2) * (gv * gv)
        m_hat = mv / (1.0 - ADAM_B1 ** ADAM_STEP)
        v_hat = vv / (1.0 - ADAM_B2 ** ADAM_STEP)
        d_ref[...] = -ADAM_LR * (m_hat / (jnp.sqrt(v_hat) + ADAM_EPS) + ADAM_WD * w_ref[...])
        nm_ref[...] = mv
        nv_ref[...] = vv

    blk = pl.BlockSpec((t, cols), lambda i: (i, 0))
    return pl.pallas_call(
        body,
        name=name,
        grid=(rows // t,),
        in_specs=[blk] * 4,
        out_specs=[blk] * 3,
        out_shape=[jax.ShapeDtypeStruct((rows, cols), F32)] * 3,
        compiler_params=_cparams(("parallel",)),
    )(g, w, m, v)


PACK_W = 1024
BIG = (("w_in", (D_MODEL, IN_COLS), 1), ("w_up_dil", (DIL_W, D_MODEL), 1), ("w_up_sb", (SB_W, D_MODEL), 1),
       ("w_out", (D_MODEL, D_MODEL), 0), ("w_mlp_in", (D_MODEL, D_FF), 1), ("w_mlp_out", (D_FF, D_MODEL), 0))


def _shard_shape(shape, axis):
    return tuple(d // N_CHIPS if a == axis else d for a, d in enumerate(shape))


MIXER_GROUP, MLP_GROUP = BIG[:4], BIG[4:]


def _pack_rows(group=BIG):
    rows, at = {}, 0
    for name, shape, axis in group:
        n = math.prod(_shard_shape(shape, axis)) // PACK_W
        rows[name] = (at, n)
        at += n
    return rows, at


def _pack_shards(shards, group):
    return jnp.concatenate([shards[name].reshape(-1, PACK_W) for name, _, _ in group], axis=0)


def _unpack_full(gathered, group):
    rows, _ = _pack_rows(group)
    full = {}
    for name, shape, axis in group:
        at, n = rows[name]
        parts = gathered[:, at:at + n, :].reshape((N_CHIPS,) + _shard_shape(shape, axis))
        if axis == 0:
            full[name] = parts.reshape(shape)
        else:
            full[name] = jnp.transpose(parts, (1, 0, 2)).reshape(shape)
    return full


def _pack_full_grads(grads, group):
    chunks = []
    for name, shape, axis in group:
        g = grads[name]
        if axis == 0:
            parts = g.reshape((N_CHIPS, shape[0] // N_CHIPS, shape[1]))
        else:
            parts = jnp.transpose(g.reshape((shape[0], N_CHIPS, shape[1] // N_CHIPS)), (1, 0, 2))
        chunks.append(parts.reshape(N_CHIPS, -1, PACK_W))
    return jnp.concatenate(chunks, axis=1)


def _unpack_shard(packed, group):
    rows, _ = _pack_rows(group)
    return {name: packed[rows[name][0]:rows[name][0] + rows[name][1]].reshape(_shard_shape(shape, axis))
            for name, shape, axis in group}


def _local_step(x, target, w, mlp_shards, norm_mix_g, b_gate, norm_mlp_g, norm_final_g, core):
    w_in = w["w_in"]
    sb0 = 9 * DIL_W
    w_sb, w_gate = w_in[:, sb0:QKV_W], w_in[:, QKV_W:]
    w_dil = [jnp.concatenate([w_in[:, (3 * i + g) * DIL_W:(3 * i + g + 1) * DIL_W] for i in range(3)], axis=1)
             for g in range(3)]

    h = _rms_fwd(x, norm_mix_g, "norm_mix")
    qkv_dil = [_matmul(h, w_dil[g], mode="nn", out_dtypes=(BF16,), name=f"proj_dil_g{g}", tn=768)[0] for g in range(3)]
    (qkv_sb,) = _matmul(h, w_sb, mode="nn", out_dtypes=(BF16,), name="proj_sb", tn=768)
    (gl,) = _matmul(h, w_gate, mode="nn", out_dtypes=(F32,), name="proj_gate")
    dil = [_dil_fwd(qkv_dil[g], g) for g in range(3)]
    o_groups, lse_groups = [d[0] for d in dil], [d[1] for d in dil]
    o_sb, a_sb, mlp_others = _sb_fwd(qkv_sb, mlp_shards)
    w = {**w, **_unpack_full(_fill_own_slot(mlp_others, mlp_shards), MLP_GROUP)}
    merged, o_a = _merge_fwd(o_groups, lse_groups, o_sb, gl, b_gate, w["w_up_dil"], w["w_up_sb"])
    def residual_and_norm(acc, res, g):
        x1 = res + acc
        return x1, _rms_rows(x1)[0] * g

    x1, h2 = _matmul(merged, w["w_out"], mode="nn", out_dtypes=(F32, BF16), name="out_proj", tm=ROW_TILE,
                     extras=(x, norm_mlp_g), epilogue=residual_and_norm)
    u, act = _matmul(h2, w["w_mlp_in"], mode="nn", out_dtypes=(BF16, BF16), name="mlp_in",
                     epilogue=lambda acc: (acc, jnp.square(jnp.maximum(acc, 0.0))))

    def residual_and_loss(acc, res, tgt, g):
        xh, r = _rms_rows(res + acc)
        err = xh * g - tgt
        dy = err * (1.0 / D_MODEL)
        dxh = dy * g
        dx2 = r * (dxh - xh * jnp.mean(dxh * xh, axis=-1, keepdims=True))
        return dx2, _rows_sum8(dy * xh), (0.5 / D_MODEL) * _rows_sum8(err * err)

    dx2, dg_final, loss_part = _matmul(
        act, w["w_mlp_out"], mode="nn", out_dtypes=(F32, ("part", F32), ("part", F32)), name="mlp_out", tm=ROW_TILE,
        tk=2048, extras=(x1, target, norm_final_g.reshape(1, D_MODEL)), epilogue=residual_and_loss)

    (du,) = _matmul(dx2, w["w_mlp_out"], mode="nt", out_dtypes=(BF16,), name="mlp_out_dx",
                    extras=(u,), epilogue=lambda acc, uu: (acc * (2.0 * jnp.maximum(uu.astype(F32), 0.0)),))
    (g_mlp_out,) = _matmul(act, dx2, mode="tn", out_dtypes=(F32,), name="mlp_out_dw")
    (g_mlp_in,) = _matmul(h2, du, mode="tn", out_dtypes=(F32,), name="mlp_in_dw")

    def norm_bwd(acc, xx, dres, g):
        dx, dg = _rms_bwd_rows(acc, xx, g)
        return dres + dx, dg

    dx1, dg_mlp = _matmul(du, w["w_mlp_in"], mode="nt", out_dtypes=(F32, ("part", F32)), name="mlp_in_dx",
                          tm=ROW_TILE, tk=2048, extras=(x1, dx2, norm_mlp_g), epilogue=norm_bwd)

    mlp_pack = _pack_full_grads({"w_mlp_in": g_mlp_in, "w_mlp_out": g_mlp_out}, MLP_GROUP)
    (dmerged,) = _matmul(dx1, w["w_out"], mode="nt", out_dtypes=(F32,), name="out_proj_dx")
    (g_out,) = _matmul(merged, dx1, mode="tn", out_dtypes=(F32,), name="out_proj_dw")
    mb = _merge_bwd(dmerged, o_groups, lse_groups, o_sb, gl, b_gate, w["w_up_dil"], w["w_up_sb"], mlp_pack)
    dua, dub, dgl, dbg, do_sb = mb[:5]
    do_groups, c_groups = mb[5:8], mb[8:11]
    mlp_sums = _add_halves(mlp_pack, mb[11], core)
    (g_up_dil,) = _matmul(o_a, dua, mode="tn", out_dtypes=(F32,), name="up_dil_dw")
    (g_up_sb,) = _matmul(o_sb, dub, mode="tn", out_dtypes=(F32,), name="up_sb_dw")
    dq_sb, dk_sb, dv_sb, mlp_got = _sb_bwd(qkv_sb, do_sb, a_sb, mlp_sums)
    dil_b = [_dil_bwd(qkv_dil[g], do_groups[g], lse_groups[g], c_groups[g], g) for g in range(3)]
    dproj = jnp.concatenate(
        [dil_b[g][i].astype(BF16) for i in range(3) for g in range(3)]
        + [t.astype(BF16) for t in (dq_sb, dk_sb, dv_sb)] + [dgl], axis=1)
    (g_in,) = _matmul(h, dproj, mode="tn", out_dtypes=(F32,), name="proj_dw", tm=512, tn=IN_COLS // 2)
    mixer_pack = _pack_full_grads({"w_in": g_in, "w_up_dil": g_up_dil, "w_up_sb": g_up_sb, "w_out": g_out}, MIXER_GROUP)
    mixer_sums = _add_halves(mixer_pack, _swap_halves(mixer_pack), core)
    grad_x, dg_mix, mixer_got = _matmul(
        dproj, w["w_in"], mode="nt", out_dtypes=(F32, ("part", F32)), name="proj_dx", tm=ROW_TILE, tk=IN_COLS // 2,
        extras=(x, dx1, norm_mix_g), epilogue=norm_bwd, exchange=mixer_sums)

    small = (dg_mix, dbg, dg_mlp, dg_final, loss_part)
    return grad_x, (mixer_got, mixer_sums), (mlp_got, mlp_sums), small


def kernel(x, norm_mix_g, w_in, b_gate, w_up_dil, w_up_sb, w_out, norm_mlp_g, w_mlp_in, w_mlp_out, norm_final_g, loss_target, m_norm_mix_g, m_w_in, m_b_gate, m_w_up_dil, m_w_up_sb, m_w_out, m_norm_mlp_g, m_w_mlp_in, m_w_mlp_out, m_norm_final_g, v_norm_mix_g, v_w_in, v_b_gate, v_w_up_dil, v_w_up_sb, v_w_out, v_norm_mlp_g, v_w_mlp_in, v_w_mlp_out, v_norm_final_g):
    shards = {"w_in": w_in[0], "w_up_dil": w_up_dil[0], "w_up_sb": w_up_sb[0], "w_out": w_out[0],
              "w_mlp_in": w_mlp_in[0], "w_mlp_out": w_mlp_out[0]}
    moments_m = {"w_in": m_w_in[0], "w_up_dil": m_w_up_dil[0], "w_up_sb": m_w_up_sb[0], "w_out": m_w_out[0],
                 "w_mlp_in": m_w_mlp_in[0], "w_mlp_out": m_w_mlp_out[0]}
    moments_v = {"w_in": v_w_in[0], "w_up_dil": v_w_up_dil[0], "w_up_sb": v_w_up_sb[0], "w_out": v_w_out[0],
                 "w_mlp_in": v_w_mlp_in[0], "w_mlp_out": v_w_mlp_out[0]}

    shards16 = {n: s.astype(BF16) for n, s in shards.items()}
    full = _unpack_full(_all_gather_weights(_pack_shards(shards16, MIXER_GROUP)), MIXER_GROUP)
    mlp_shards = _pack_shards(shards16, MLP_GROUP)

    core = lax.axis_index("c").astype(jnp.int32).reshape(1)
    chip = (2 * lax.axis_index("x") + lax.axis_index("y")).astype(jnp.int32).reshape(1)
    grad_x, (mixer_got, mixer_sums), (mlp_got, mlp_sums), small = _local_step(
        x[0], loss_target[0], full, mlp_shards, norm_mix_g, b_gate, norm_mlp_g, norm_final_g, core)

    reduced = _join_halves(_sum_chips(mixer_got, mixer_sums, chip))
    reduced_mlp = _join_halves(_sum_chips(mlp_got, mlp_sums, chip))
    g_shard = {**_unpack_shard(reduced, MIXER_GROUP), **_unpack_shard(reduced_mlp, MLP_GROUP)}

    dg_mix, dbg, dg_mlp, dg_final, loss_part = small
    loss_row = jnp.sum(loss_part, axis=0, keepdims=True)
    small_pack = jnp.concatenate(
        [jnp.sum(dg_mix, axis=0, keepdims=True), jnp.sum(dbg, axis=0, keepdims=True),
         jnp.sum(dg_mlp, axis=0, keepdims=True), jnp.sum(dg_final, axis=0, keepdims=True), loss_row], axis=1)
    n_small = small_pack.shape[1]
    small_sum = _all_reduce_small(small_pack.reshape(n_small // 128, 128)).reshape(1, n_small)
    g_norm_mix = small_sum[:, :D_MODEL]
    g_b_gate = small_sum[:, D_MODEL:3 * D_MODEL]
    g_norm_mlp = small_sum[:, 3 * D_MODEL:4 * D_MODEL]
    g_norm_final = small_sum[:, 4 * D_MODEL:5 * D_MODEL]
    loss = jnp.sum(small_sum[:, 5 * D_MODEL:])

    names = ["norm_mix_g", "w_in", "b_gate", "w_up_dil", "w_up_sb", "w_out", "norm_mlp_g", "w_mlp_in", "w_mlp_out",
             "norm_final_g"]
    grads = dict(g_shard)
    grads.update(norm_mix_g=g_norm_mix, b_gate=g_b_gate, norm_mlp_g=g_norm_mlp, norm_final_g=g_norm_final)
    weights = dict(shards)
    weights.update(norm_mix_g=norm_mix_g, b_gate=b_gate, norm_mlp_g=norm_mlp_g, norm_final_g=norm_final_g.reshape(1, D_MODEL))
    ms = dict(moments_m)
    ms.update(norm_mix_g=m_norm_mix_g, b_gate=m_b_gate, norm_mlp_g=m_norm_mlp_g, norm_final_g=m_norm_final_g.reshape(1, D_MODEL))
    vs = dict(moments_v)
    vs.update(norm_mix_g=v_norm_mix_g, b_gate=v_b_gate, norm_mlp_g=v_norm_mlp_g, norm_final_g=v_norm_final_g.reshape(1, D_MODEL))

    out_shapes = {"norm_mix_g": norm_mix_g.shape, "w_in": w_in.shape, "b_gate": b_gate.shape, "w_up_dil": w_up_dil.shape,
                  "w_up_sb": w_up_sb.shape, "w_out": w_out.shape, "norm_mlp_g": norm_mlp_g.shape,
                  "w_mlp_in": w_mlp_in.shape, "w_mlp_out": w_mlp_out.shape, "norm_final_g": norm_final_g.shape}
    g_out, d_out, m_out, v_out = [], [], [], []
    for n in names:
        d, nm, nv = _adamw(grads[n], weights[n], ms[n], vs[n], "adamw_" + n)
        shape = out_shapes[n]
        g_out.append(grads[n].reshape(shape))
        d_out.append(d.reshape(shape))
        m_out.append(nm.reshape(shape))
        v_out.append(nv.reshape(shape))
    return (loss, grad_x.reshape(x.shape), *g_out, *d_out, *m_out, *v_out)
```

```python
import math

import jax
import jax.numpy as jnp
import numpy as np
from jax import lax
from jax.experimental import pallas as pl
from jax.experimental.pallas import tpu as pltpu

F32 = jnp.float32
BF16 = jnp.bfloat16
MESH = pl.DeviceIdType.MESH

D_MODEL = 1024
HEAD_DIM = 64
DIL_GROUPS = ((128, 1), (512, 4), (2048, 16))
DIL_HEADS = 4
DIL_W = 256
N_DIL_HEADS = 12
SB_HEADS = 8
SB_W = SB_HEADS * HEAD_DIM
QKV_W = 3 * 3 * DIL_W + 3 * SB_W
GATE_W = 2 * D_MODEL
IN_COLS = QKV_W + GATE_W
D_FF = 4 * D_MODEL
BLOCK = 128
RMS_EPS = 1e-6
NEG_INF = -1e30
N_CHIPS = 4
N_DEV = 8

ADAM_LR = 0.001
ADAM_B1 = 0.9
ADAM_B2 = 0.999
ADAM_EPS = 1e-08
ADAM_WD = 0.01
ADAM_STEP = 10

VMEM_LIMIT = 56 * 1024 * 1024

SB_BQ = 256
SB_BK = 256


def _cparams(sem=None):
    if sem is None:
        return pltpu.CompilerParams(vmem_limit_bytes=VMEM_LIMIT)
    return pltpu.CompilerParams(dimension_semantics=sem, vmem_limit_bytes=VMEM_LIMIT)


def _dot(a, b, dims):
    return lax.dot_general(a, b, (dims, ((), ())), preferred_element_type=F32)


def _dot_nn(a, b):
    return _dot(a, b, ((1,), (0,)))


def _dot_nt(a, b):
    return _dot(a, b, ((1,), (1,)))


def _dot_tn(a, b):
    return _dot(a, b, ((0,), (0,)))


def _dot_f32_by_01(x, m01, pieces=3):
    hi = x.astype(BF16)
    if pieces == 1:
        return _dot_nn(hi, m01)
    r1 = x - hi.astype(F32)
    mid = r1.astype(BF16)
    if pieces == 2:
        return _dot_nn(hi, m01) + _dot_nn(mid, m01)
    lo = (r1 - mid.astype(F32)).astype(BF16)
    return _dot_nn(hi, m01) + _dot_nn(mid, m01) + _dot_nn(lo, m01)


def _matmul(a, b, *, mode, out_dtypes, name, tm=1024, tn=1024, tk=1024, extras=(), epilogue=None, exchange=None):
    if mode == "nn":
        (m, k), (k2, n) = a.shape, b.shape
    elif mode == "nt":
        (m, k), (n, k2) = a.shape, b.shape
    else:
        (k, m), (k2, n) = a.shape, b.shape
    assert k == k2, (a.shape, b.shape, mode)
    tm, tn, tk = min(tm, m), min(tn, n), min(tk, k)
    assert m % tm == 0 and n % tn == 0 and k % tk == 0, (m, n, k, tm, tn, tk)
    nk = k // tk
    n_out = len(out_dtypes)
    n_ex = len(extras)

    if mode == "nn":
        a_spec = pl.BlockSpec((tm, tk), lambda i, j, kk: (i, kk))
        b_spec = pl.BlockSpec((tk, tn), lambda i, j, kk: (kk, j))
        dot = _dot_nn
    elif mode == "nt":
        a_spec = pl.BlockSpec((tm, tk), lambda i, j, kk: (i, kk))
        b_spec = pl.BlockSpec((tn, tk), lambda i, j, kk: (j, kk))
        dot = _dot_nt
    else:
        a_spec = pl.BlockSpec((tk, tm), lambda i, j, kk: (kk, i))
        b_spec = pl.BlockSpec((tk, tn), lambda i, j, kk: (kk, j))
        dot = _dot_tn
    mn_spec = pl.BlockSpec((tm, tn), lambda i, j, kk: (i, j))
    row_spec = pl.BlockSpec((1, tn), lambda i, j, kk: (0, j))
    part_spec = pl.BlockSpec((8, tn), lambda i, j, kk: (i, j))
    ex_specs = [row_spec if e.shape[0] == 1 else mn_spec for e in extras]
    is_part = [isinstance(dt, tuple) for dt in out_dtypes]
    out_dts = [dt[1] if p else dt for dt, p in zip(out_dtypes, is_part)]
    out_specs = [part_spec if p else mn_spec for p in is_part]
    out_shapes = [jax.ShapeDtypeStruct((8 * (m // tm), n) if p else (m, n), dt) for dt, p in zip(out_dts, is_part)]

    n_side = 0 if exchange is None else 1
    grid = (m // tm, n // tn, nk)

    def body(*refs):
        a_ref, b_ref = refs[0], refs[1]
        ex_refs = refs[2:2 + n_ex]
        out_refs = refs[2 + n_ex + n_side:2 + n_ex + n_side + n_out]
        scratch = refs[2 + n_ex + n_side + n_out + n_side:]
        acc_ref = scratch[0] if nk > 1 else None
        if exchange is not None:
            side = (refs[2 + n_ex], refs[2 + n_ex + n_side + n_out]) + tuple(scratch[-2:])
            step = (pl.program_id(0) * grid[1] + pl.program_id(1)) * grid[2] + pl.program_id(2)

            @pl.when(step == 0)
            def _():
                _exchange_start(*side)

            @pl.when(step == grid[0] * grid[1] * grid[2] - 1)
            def _():
                _exchange_wait(*side)

        part = dot(a_ref[...].astype(BF16), b_ref[...].astype(BF16))

        def finish(acc):
            if epilogue is None:
                outs = (acc,)
            else:
                outs = epilogue(acc, *[r[...] for r in ex_refs])
            for o_ref, o in zip(out_refs, outs):
                o_ref[...] = o.astype(o_ref.dtype)

        if nk == 1:
            finish(part)
        else:
            kk = pl.program_id(2)

            @pl.when(kk == 0)
            def _():
                acc_ref[...] = part

            @pl.when(kk > 0)
            def _():
                acc_ref[...] += part

            @pl.when(kk == nk - 1)
            def _():
                finish(acc_ref[...])

    side_in = [] if exchange is None else [exchange]
    outs = pl.pallas_call(
        body,
        name=name,
        grid=grid,
        in_specs=[a_spec, b_spec] + ex_specs + [ANY] * n_side,
        out_specs=out_specs + [ANY] * n_side,
        out_shape=out_shapes + [jax.ShapeDtypeStruct(e.shape, e.dtype) for e in side_in],
        scratch_shapes=([pltpu.VMEM((tm, tn), F32)] if nk > 1 else [])
        + [pltpu.SemaphoreType.DMA((3,)), pltpu.SemaphoreType.DMA((3,))] * n_side,
        compiler_params=_cparams(("arbitrary",) * 3 if n_side else ("parallel", "parallel", "arbitrary")),
    )(a, b, *extras, *side_in)
    return outs


ROW_TILE = 512


def _rows_sum8(t):
    rows, d = t.shape
    return jnp.sum(t.reshape(rows // 8, 8, d), axis=0)


def _rms_rows(x):
    r = lax.rsqrt(jnp.mean(x * x, axis=-1, keepdims=True) + RMS_EPS)
    return x * r, r


def _rms_bwd_rows(dh, x, g):
    xh, r = _rms_rows(x)
    dxh = dh * g
    return r * (dxh - xh * jnp.mean(dxh * xh, axis=-1, keepdims=True)), _rows_sum8(dh * xh)


def _rms_fwd_and_gather(x, g, shard_pack):
    s, d = x.shape
    r_pack, w_pack = shard_pack.shape
    steps = s // ROW_TILE

    def body(x_ref, g_ref, pack_ref, h_ref, others_ref, send_sems, recv_sems):
        i = pl.program_id(0)
        gather = (pack_ref, others_ref, send_sems, recv_sems)

        @pl.when(i == 0)
        def _():
            _gather_start(*gather)

        h_ref[...] = (_rms_rows(x_ref[...])[0] * g_ref[...]).astype(BF16)

        @pl.when(i == steps - 1)
        def _():
            _gather_pass_on(*gather)
            _gather_finish(*gather)

    h, others = pl.pallas_call(
        body,
        name="norm_mix",
        grid=(steps,),
        in_specs=[pl.BlockSpec((ROW_TILE, d), lambda i: (i, 0)), pl.BlockSpec((1, d), lambda i: (0, 0)), ANY],
        out_specs=[pl.BlockSpec((ROW_TILE, d), lambda i: (i, 0)), ANY],
        out_shape=[jax.ShapeDtypeStruct((s, d), BF16),
                   jax.ShapeDtypeStruct((N_CHIPS, 2, r_pack // 2, w_pack), shard_pack.dtype)],
        scratch_shapes=[pltpu.SemaphoreType.DMA((6,)), pltpu.SemaphoreType.DMA((6,))],
        compiler_params=_cparams(("arbitrary",)),
    )(x, g, shard_pack.reshape(2, r_pack // 2, w_pack))
    return h, _fill_own_slot(others, shard_pack)


def _alibi_slopes():
    return np.exp2(np.float32(-8.0) * np.arange(1, N_DIL_HEADS + 1, dtype=np.float32) / np.float32(N_DIL_HEADS))


def _head_lane_mask(h, rows):
    lane = lax.broadcasted_iota(jnp.int32, (rows, DIL_W), 1)
    return (lane >= h * HEAD_DIM) & (lane < (h + 1) * HEAD_DIM)


def _band_terms(dil, has_prev):
    qi = lax.broadcasted_iota(jnp.int32, (BLOCK, 2 * BLOCK), 0)
    kj = lax.broadcasted_iota(jnp.int32, (BLOCK, 2 * BLOCK), 1)
    steps = qi + BLOCK - kj
    valid = (steps >= 0) & (steps <= BLOCK) & ((kj >= BLOCK) | has_prev)
    return valid, steps.astype(F32) * float(dil)


def _load_halves(ref, rows):
    return jnp.concatenate([ref[0, rows, :], ref[1, rows, :]], axis=1)


def _store_halves(ref, rows, value):
    ref[0, rows, :] = value[:, :128]
    ref[1, rows, :] = value[:, 128:]


def _dil_fwd(qkv_g, group):
    _, dil = DIL_GROUPS[group]
    s = qkv_g.shape[0]
    sub = s // dil
    nb = sub // BLOCK
    view = qkv_g.reshape(sub, dil * 3 * DIL_W)
    slopes = _alibi_slopes()[group * DIL_HEADS:(group + 1) * DIL_HEADS]

    def col(which):
        return lambda n, r: (n, r * 3 + which)

    def col_prev(which):
        return lambda n, r: (jnp.maximum(n - 1, 0), r * 3 + which)

    def body(q_ref, kc_ref, kp_ref, vc_ref, vp_ref, o_ref, lse_ref):
        n, r = pl.program_id(0), pl.program_id(1)
        mine = pl.ds(r, BLOCK, stride=dil) if dil > 1 else slice(None)
        valid, dist = _band_terms(dil, n > 0)
        q = q_ref[...]
        k2 = jnp.concatenate([kp_ref[...], kc_ref[...]], axis=0)
        v2 = jnp.concatenate([vp_ref[...], vc_ref[...]], axis=0)
        masks = [_head_lane_mask(h, BLOCK) for h in range(DIL_HEADS)]
        logits = [_dot_nt(jnp.where(masks[h], q, jnp.zeros_like(q)), k2) for h in range(DIL_HEADS)]
        ps, lses = [], []
        for h in range(DIL_HEADS):
            lg = jnp.where(valid, logits[h] * 0.125 - float(slopes[h]) * dist, NEG_INF)
            mx = jnp.max(lg, axis=1, keepdims=True)
            lse = mx + jnp.log(jnp.sum(jnp.exp(lg - mx), axis=1, keepdims=True))
            ps.append(jnp.exp(lg - lse).astype(BF16))
            lses.append(lse)
        o_acc = jnp.zeros((BLOCK, DIL_W), F32)
        lse_acc = jnp.zeros((BLOCK, DIL_W), F32)
        for h in range(DIL_HEADS):
            o_acc = jnp.where(masks[h], _dot_nn(ps[h], v2), o_acc)
            lse_acc = jnp.where(masks[h], lses[h], lse_acc)
        _store_halves(o_ref, mine, o_acc)
        _store_halves(lse_ref, mine, lse_acc)

    blk = (BLOCK, DIL_W)
    return pl.pallas_call(
        body,
        name=f"dil_fwd_g{group}",
        grid=(nb, dil),
        in_specs=[pl.BlockSpec(blk, col(0)), pl.BlockSpec(blk, col(1)), pl.BlockSpec(blk, col_prev(1)),
                  pl.BlockSpec(blk, col(2)), pl.BlockSpec(blk, col_prev(2))],
        out_specs=[pl.BlockSpec((2, BLOCK * dil, 128), lambda n, r: (0, n, 0))] * 2,
        out_shape=[jax.ShapeDtypeStruct((2, s, 128), F32)] * 2,
        compiler_params=_cparams(("parallel", "arbitrary")),
    )(view, view, view, view, view)


def _dil_bwd(qkv, do, lse, cterm, group):
    _, dil = DIL_GROUPS[group]
    s = qkv.shape[0]
    sub = s // dil
    nb = sub // BLOCK
    view = qkv.reshape(sub, dil * 3 * DIL_W)
    slopes = _alibi_slopes()[group * DIL_HEADS:(group + 1) * DIL_HEADS]

    def col(which, shift):
        if shift == 0:
            return lambda n, r: (n, r * 3 + which)
        if shift < 0:
            return lambda n, r: (jnp.maximum(n - 1, 0), r * 3 + which)
        return lambda n, r: (jnp.minimum(n + 1, nb - 1), r * 3 + which)

    def own(shift):
        if shift == 0:
            return pl.BlockSpec((2, BLOCK * dil, 128), lambda n, r: (0, n, 0))
        return pl.BlockSpec((2, BLOCK * dil, 128), lambda n, r: (0, jnp.minimum(n + 1, nb - 1), 0))

    def body(q_ref, qn_ref, kc_ref, kp_ref, vc_ref, vp_ref, do_ref, don_ref, lse_ref, lsen_ref, c_ref, cn_ref,
             dq_ref, dk_ref, dv_ref):
        n, r = pl.program_id(0), pl.program_id(1)
        mine = pl.ds(r, BLOCK, stride=dil) if dil > 1 else slice(None)
        valid, dist = _band_terms(dil, n > 0)
        valid_n = _band_terms(dil, True)[0][:, :BLOCK] & (n < nb - 1)
        dist_n = dist[:, :BLOCK]
        q, qn = q_ref[...], qn_ref[...]
        kc, vc = kc_ref[...], vc_ref[...]
        k2 = jnp.concatenate([kp_ref[...], kc], axis=0)
        v2 = jnp.concatenate([vp_ref[...], vc], axis=0)
        dov, donv = _load_halves(do_ref, mine), _load_halves(don_ref, mine)
        lsev, lsenv = _load_halves(lse_ref, mine), _load_halves(lsen_ref, mine)
        cv, cnv = _load_halves(c_ref, mine), _load_halves(cn_ref, mine)
        masks = [_head_lane_mask(h, BLOCK) for h in range(DIL_HEADS)]

        def head_col(t, hm):
            return jnp.max(jnp.where(hm, t, NEG_INF), axis=1, keepdims=True)

        qhs = [jnp.where(hm, q, jnp.zeros_like(q)) for hm in masks]
        qnhs = [jnp.where(hm, qn, jnp.zeros_like(qn)) for hm in masks]
        dohs = [jnp.where(hm, dov, 0.0).astype(BF16) for hm in masks]
        donhs = [jnp.where(hm, donv, 0.0).astype(BF16) for hm in masks]
        logit = [_dot_nt(qhs[h], k2) for h in range(DIL_HEADS)]
        dp = [_dot_nt(dohs[h], v2) for h in range(DIL_HEADS)]
        logit_n = [_dot_nt(qnhs[h], kc) for h in range(DIL_HEADS)]
        dp_n = [_dot_nt(donhs[h], vc) for h in range(DIL_HEADS)]
        p16, dlog, pn16, dlog_n = [], [], [], []
        for h in range(DIL_HEADS):
            hm, slope = masks[h], float(slopes[h])
            p = jnp.where(valid, jnp.exp(logit[h] * 0.125 - slope * dist - head_col(lsev, hm)), 0.0)
            dlog.append((p * (dp[h] + head_col(cv, hm)) * 0.125).astype(BF16))
            p16.append(p.astype(BF16))
            pn = jnp.where(valid_n, jnp.exp(logit_n[h] * 0.125 - slope * dist_n - head_col(lsenv, hm)), 0.0)
            dlog_n.append((pn * (dp_n[h] + head_col(cnv, hm)) * 0.125).astype(BF16))
            pn16.append(pn.astype(BF16))
        dq_acc = jnp.zeros((BLOCK, DIL_W), F32)
        dk_acc = jnp.zeros((BLOCK, DIL_W), F32)
        dv_acc = jnp.zeros((BLOCK, DIL_W), F32)
        for h in range(DIL_HEADS):
            dq_acc = jnp.where(masks[h], _dot_nn(dlog[h], k2), dq_acc)
            dk_acc += _dot_tn(dlog[h][:, BLOCK:], qhs[h]) + _dot_tn(dlog_n[h], qnhs[h])
            dv_acc += _dot_tn(p16[h][:, BLOCK:], dohs[h]) + _dot_tn(pn16[h], donhs[h])
        dq_ref[...] = dq_acc.astype(BF16)
        dk_ref[...] = dk_acc.astype(BF16)
        dv_ref[...] = dv_acc.astype(BF16)

    blk = (BLOCK, DIL_W)
    outs = pl.pallas_call(
        body,
        name=f"dil_bwd_g{group}",
        grid=(nb, dil),
        in_specs=[pl.BlockSpec(blk, col(0, 0)), pl.BlockSpec(blk, col(0, 1)),
                  pl.BlockSpec(blk, col(1, 0)), pl.BlockSpec(blk, col(1, -1)),
                  pl.BlockSpec(blk, col(2, 0)), pl.BlockSpec(blk, col(2, -1)),
                  own(0), own(1), own(0), own(1), own(0), own(1)],
        out_specs=[pl.BlockSpec(blk, lambda n, r: (n, r))] * 3,
        out_shape=[jax.ShapeDtypeStruct((sub, dil * DIL_W), BF16)] * 3,
        compiler_params=_cparams(("parallel", "parallel")),
    )(view, view, view, view, view, view, do, do, lse, lse, cterm, cterm)
    return tuple(t.reshape(s, DIL_W) for t in outs)


SB_PAIRS = SB_HEADS // 2
SB_COL0 = 0
LOG2E = 1.4426950408889634


SB_EXP_CLAMP = 64.0


def _sb_softplus2(zs):
    t = 1.0 + jnp.exp2(jnp.minimum(zs, SB_EXP_CLAMP))
    return jnp.maximum(jnp.log(t) * LOG2E, zs)


def _sb_consts(nkb):
    row = lax.broadcasted_iota(jnp.int32, (SB_BQ, SB_BK), 0)
    colk = lax.broadcasted_iota(jnp.int32, (SB_BQ, SB_BK), 1)
    rr = lax.broadcasted_iota(jnp.int32, (SB_BK, SB_BK), 0)
    cc = lax.broadcasted_iota(jnp.int32, (SB_BK, SB_BK), 1)
    lane = lax.broadcasted_iota(jnp.int32, (SB_BQ, 128), 1)
    assert 2 * nkb <= 128
    return colk < row, rr, cc, lane < HEAD_DIM, lane


def _split_heads(t):
    first = lax.broadcasted_iota(jnp.int32, t.shape, 1) < HEAD_DIM
    zero = jnp.zeros_like(t)
    return jnp.where(first, t, zero), jnp.where(first, zero, t)


def _sb_fwd(qkv, shard_pack):
    s = qkv.shape[0]
    nq, nkb = s // SB_BQ, s // SB_BK
    zscale = LOG2E / math.sqrt(HEAD_DIM)
    r_pack, w_pack = shard_pack.shape

    def body(q_ref, k_ref, v_ref, pack_ref, o_ref, a_row, others_ref, zs_scr, a_scr, acc_scr, cl_scr,
             send_sems, recv_sems):
        i = pl.program_id(1)
        pair = pl.program_id(0)
        gather = (pack_ref, others_ref, send_sems, recv_sems)

        @pl.when((pair == 0) & (i == 0))
        def _():
            _gather_start(*gather)

        @pl.when((pair == 1) & (i == 0))
        def _():
            _gather_pass_on(*gather)

        @pl.when((pair == SB_PAIRS - 1) & (i == nq - 1))
        def _():
            _gather_finish(*gather)

        causal, rr, cc, _, _ = _sb_consts(nkb)
        later = (rr > cc).astype(BF16)
        qh = _split_heads(q_ref[...])

        def rows(j):
            return pl.ds(pl.multiple_of(j * SB_BK, SB_BK), SB_BK)

        def scores_to(slot, j):
            kb = k_ref[rows(j), :]
            for hh in range(2):
                zs_scr[slot, hh] = _dot_nt(qh[hh], kb) * zscale

        def weights(slot, j, masked):
            xs, sums, sufs = [], [], []
            for hh in range(2):
                zs = zs_scr[slot, hh]
                sp = _sb_softplus2(zs)
                if masked:
                    sp = jnp.where(causal, sp, 0.0)
                xs.append(zs - sp)
                sums.append(jnp.sum(sp, axis=1, keepdims=True))
                sufs.append(_dot_f32_by_01(sp, later, 2))
            for hh in range(2):
                cl = cl_scr[hh]
                a = jnp.exp2(xs[hh] - (sufs[hh] + jnp.concatenate([cl, cl], axis=1)))
                if masked:
                    a = jnp.where(causal, a, 0.0)
                a16 = a.astype(BF16)
                a_scr[slot, :, hh * SB_BK:(hh + 1) * SB_BK] = a16
                a_row[0, 0, j, :, hh * SB_BK:(hh + 1) * SB_BK] = a16
                cl_scr[hh] = cl + sums[hh]

        def add_av(slot, j):
            v0, v1 = _split_heads(v_ref[rows(j), :])
            acc_scr[...] += _dot_nn(a_scr[slot], jnp.concatenate([v0, v1], axis=0))

        acc_scr[...] = jnp.zeros_like(acc_scr)
        cl_scr[...] = jnp.zeros_like(cl_scr)
        scores_to(0, i)
        scores_to(1, jnp.maximum(i - 1, 0))
        weights(0, i, True)

        def step(j, prev, cur):
            scores_to(prev, jnp.maximum(j - 1, 0))
            add_av(prev, j + 1)
            weights(cur, j, False)

        def two_steps(u, _):
            j = i - 1 - 2 * u
            step(j, 0, 1)
            step(j - 1, 1, 0)
            return 0

        lax.fori_loop(0, i // 2, two_steps, 0)

        @pl.when(i % 2 == 1)
        def _():
            step(0, 0, 1)
            add_av(1, 0)

        @pl.when(i % 2 == 0)
        def _():
            add_av(0, 0)

        o_ref[...] = acc_scr[...]

    def full(which):
        return pl.BlockSpec((s, 128), lambda p, i: (0, SB_COL0 + 4 * which + p))

    return pl.pallas_call(
        body,
        name="sb_fwd",
        grid=(SB_PAIRS, nq),
        in_specs=[pl.BlockSpec((SB_BQ, 128), lambda p, i: (i, SB_COL0 + p)), full(1), full(2), ANY],
        out_specs=[pl.BlockSpec((SB_BQ, 128), lambda p, i: (i, p)),
                   pl.BlockSpec((1, 1, nkb, SB_BQ, 2 * SB_BK), lambda p, i: (p, i, 0, 0, 0)), ANY],
        out_shape=[jax.ShapeDtypeStruct((s, SB_W), F32),
                   jax.ShapeDtypeStruct((SB_PAIRS, nq, nkb, SB_BQ, 2 * SB_BK), BF16),
                   jax.ShapeDtypeStruct((N_CHIPS, 2, r_pack // 2, w_pack), shard_pack.dtype)],
        scratch_shapes=[pltpu.VMEM((2, 2, SB_BQ, SB_BK), F32), pltpu.VMEM((2, SB_BQ, 2 * SB_BK), BF16),
                        pltpu.VMEM((SB_BQ, 128), F32), pltpu.VMEM((2, SB_BQ, 128), F32),
                        pltpu.SemaphoreType.DMA((6,)), pltpu.SemaphoreType.DMA((6,))],
        compiler_params=_cparams(("arbitrary", "arbitrary")),
    )(qkv, qkv, qkv, shard_pack.reshape(2, r_pack // 2, w_pack))


def _sb_bwd(qkv, do, a_hbm, chip_sums):
    s = qkv.shape[0]
    nq, nkb = s // SB_BQ, s // SB_BK
    scale = 1.0 / math.sqrt(HEAD_DIM)
    zscale = LOG2E * scale

    def body(q_ref, k_ref, v_ref, do_ref, a_row, sums_ref, dq_ref, dk_ref, dv_ref, got_ref,
             zs_scr, da_scr, dz_scr, a_scr, cg_scr, send_sems, recv_sems):
        i = pl.program_id(1)
        pair = pl.program_id(0)
        first_step = (pair == 0) & (i == 0)
        last_step = (pair == SB_PAIRS - 1) & (i == nq - 1)

        @pl.when(first_step)
        def _():
            _exchange_start(sums_ref, got_ref, send_sems, recv_sems)

        @pl.when(i == 0)
        def _():
            dk_ref[...] = jnp.zeros_like(dk_ref)
            dv_ref[...] = jnp.zeros_like(dv_ref)

        causal, rr, cc, first, _ = _sb_consts(nkb)
        earlier = (rr < cc).astype(BF16)
        q2 = q_ref[...]
        qh = _split_heads(q2)
        do2 = do_ref[...].astype(BF16)
        doh = _split_heads(do2)

        def rows(j):
            return pl.ds(pl.multiple_of(j * SB_BK, SB_BK), SB_BK)

        def products_to(slot, j):
            kb, vb = k_ref[rows(j), :], v_ref[rows(j), :]
            for hh in range(2):
                zs_scr[slot, hh] = _dot_nt(qh[hh], kb) * zscale
                da_scr[slot, hh] = _dot_nt(doh[hh], vb)

        def by_head(t):
            return jnp.where(first, t[:SB_BK], t[SB_BK:])

        def apply(slot, j):
            k0, k1 = _split_heads(k_ref[rows(j), :])
            dq_ref[...] += _dot_nn(dz_scr[slot], jnp.concatenate([k0, k1], axis=0)) * scale
            dk_ref[rows(j), :] += by_head(_dot_tn(dz_scr[slot], q2)) * scale
            dv_ref[rows(j), :] += by_head(_dot_tn(a_scr[slot], do2))

        def grads(slot, j, masked):
            gs, gpres = [], []
            for hh in range(2):
                a16 = a_row[0, 0, j, :, hh * SB_BK:(hh + 1) * SB_BK]
                a_scr[slot, :, hh * SB_BK:(hh + 1) * SB_BK] = a16
                g = a16.astype(F32) * da_scr[slot, hh]
                gs.append(g)
                gpres.append(_dot_f32_by_01(g, earlier, 1))
            sigs = []
            for hh in range(2):
                zs = zs_scr[slot, hh]
                sigs.append(jnp.exp2(zs - _sb_softplus2(zs)))
            for hh in range(2):
                cg = cg_scr[hh]
                dz = gs[hh] - (gs[hh] + (gpres[hh] + jnp.concatenate([cg, cg], axis=1))) * sigs[hh]
                if masked:
                    dz = jnp.where(causal, dz, 0.0)
                dz_scr[slot, :, hh * SB_BK:(hh + 1) * SB_BK] = dz.astype(BF16)
                cg_scr[hh] = cg + jnp.sum(gs[hh], axis=1, keepdims=True)

        dq_ref[...] = jnp.zeros_like(dq_ref)
        cg_scr[...] = jnp.zeros_like(cg_scr)
        dz_scr[1] = jnp.zeros((SB_BQ, 2 * SB_BK), BF16)
        a_scr[1] = jnp.zeros((SB_BQ, 2 * SB_BK), BF16)
        products_to(0, 0)

        def step(j, cur, nxt):
            products_to(nxt, j + 1)
            apply(nxt, jnp.maximum(j - 1, 0))
            grads(cur, j, False)

        def two_steps(u, _):
            step(2 * u, 0, 1)
            step(2 * u + 1, 1, 0)
            return 0

        lax.fori_loop(0, i // 2, two_steps, 0)

        def last(cur, nxt):
            apply(nxt, jnp.maximum(i - 1, 0))
            grads(cur, i, True)
            apply(cur, i)

        @pl.when(i % 2 == 1)
        def _():
            step(i - 1, 0, 1)
            last(1, 0)

        @pl.when(i % 2 == 0)
        def _():
            last(0, 1)

        @pl.when(last_step)
        def _():
            _exchange_wait(sums_ref, got_ref, send_sems, recv_sems)

    def full(which):
        return pl.BlockSpec((s, 128), lambda p, i: (0, SB_COL0 + 4 * which + p))

    qblk = pl.BlockSpec((SB_BQ, 128), lambda p, i: (i, p))
    acc = pl.BlockSpec((s, 128), lambda p, i: (0, p))
    return pl.pallas_call(
        body,
        name="sb_bwd",
        grid=(SB_PAIRS, nq),
        in_specs=[pl.BlockSpec((SB_BQ, 128), lambda p, i: (i, SB_COL0 + p)), full(1), full(2), qblk,
                  pl.BlockSpec((1, 1, nkb, SB_BQ, 2 * SB_BK), lambda p, i: (p, i, 0, 0, 0)), ANY],
        out_specs=[qblk, acc, acc, ANY],
        out_shape=[jax.ShapeDtypeStruct((s, SB_W), F32)] * 3 + [jax.ShapeDtypeStruct(chip_sums.shape, chip_sums.dtype)],
        scratch_shapes=[pltpu.VMEM((2, 2, SB_BQ, SB_BK), F32), pltpu.VMEM((2, 2, SB_BQ, SB_BK), F32),
                        pltpu.VMEM((2, SB_BQ, 2 * SB_BK), BF16), pltpu.VMEM((2, SB_BQ, 2 * SB_BK), BF16),
                        pltpu.VMEM((2, SB_BQ, 128), F32),
                        pltpu.SemaphoreType.DMA((3,)), pltpu.SemaphoreType.DMA((3,))],
        compiler_params=_cparams(("arbitrary", "arbitrary")),
    )(qkv, qkv, qkv, do, a_hbm, chip_sums)


MERGE_TILE = 256


def _group_mix(lses):
    mx = jnp.maximum(jnp.maximum(lses[0], lses[1]), lses[2])
    es = [jnp.exp(t - mx) for t in lses]
    den = es[0] + es[1] + es[2]
    return [e / den for e in es]


def _merge_fwd(o_groups, lse_groups, o_sb, gl, b_gate, w_up_dil, w_up_sb):
    s = gl.shape[0]
    t = MERGE_TILE

    def body(o0, o1, o2, l0, l1, l2, ob_ref, gl_ref, bg_ref, wd_ref, ws_ref, merged_ref, oa_ref):
        rows = slice(None)
        w = _group_mix([_load_halves(l, rows) for l in (l0, l1, l2)])
        og = [_load_halves(o, rows) for o in (o0, o1, o2)]
        oa = (w[0] * og[0] + w[1] * og[1] + w[2] * og[2]).astype(BF16)
        ua = _dot_nn(oa, wd_ref[...])
        ub = _dot_nn(ob_ref[...].astype(BF16), ws_ref[...])
        gate = jax.nn.sigmoid(gl_ref[...] + bg_ref[...])
        merged_ref[...] = (gate[:, :D_MODEL] * ua + gate[:, D_MODEL:] * ub).astype(BF16)
        oa_ref[...] = oa

    dil = pl.BlockSpec((t, DIL_W), lambda i: (i, 0))
    halves = pl.BlockSpec((2, t, 128), lambda i: (0, i, 0))
    const = lambda shape: pl.BlockSpec(shape, lambda i: (0, 0))
    return pl.pallas_call(
        body,
        name="merge_fwd",
        grid=(s // t,),
        in_specs=[halves] * 6 + [pl.BlockSpec((t, SB_W), lambda i: (i, 0)), pl.BlockSpec((t, GATE_W), lambda i: (i, 0)),
                                 const((1, GATE_W)), const((DIL_W, D_MODEL)), const((SB_W, D_MODEL))],
        out_specs=[pl.BlockSpec((t, D_MODEL), lambda i: (i, 0)), dil],
        out_shape=[jax.ShapeDtypeStruct((s, D_MODEL), BF16), jax.ShapeDtypeStruct((s, DIL_W), BF16)],
        compiler_params=_cparams(("parallel",)),
    )(*o_groups, *lse_groups, o_sb, gl, b_gate, w_up_dil, w_up_sb)


def _merge_bwd(dmerged, o_groups, lse_groups, o_sb, gl, b_gate, w_up_dil, w_up_sb, swap):
    s = gl.shape[0]
    t = MERGE_TILE
    n_chunks, r_swap, w_swap = swap.shape
    swap = swap.reshape(n_chunks, 2, r_swap // 2, w_swap)

    def body(dm_ref, o0, o1, o2, l0, l1, l2, ob_ref, gl_ref, bg_ref, wd_ref, ws_ref, swap_ref,
             dua_ref, dub_ref, dgl_ref, dbg_ref, dosb_ref, d0, d1, d2, c0, c1, c2, got_ref, send_sem, recv_sem):
        i = pl.program_id(0)

        @pl.when(i == 0)
        def _():
            _swap_copy(swap_ref, got_ref, send_sem, recv_sem).start()

        @pl.when(i == pl.num_programs(0) - 1)
        def _():
            _swap_copy(swap_ref, got_ref, send_sem, recv_sem).wait()

        rows = slice(None)
        og = [_load_halves(o, rows) for o in (o0, o1, o2)]
        w = _group_mix([_load_halves(l, rows) for l in (l0, l1, l2)])
        oa = (w[0] * og[0] + w[1] * og[1] + w[2] * og[2]).astype(BF16)
        ua = _dot_nn(oa, wd_ref[...])
        ub = _dot_nn(ob_ref[...].astype(BF16), ws_ref[...])
        gate = jax.nn.sigmoid(gl_ref[...] + bg_ref[...])
        ga, gb = gate[:, :D_MODEL], gate[:, D_MODEL:]
        dm = dm_ref[...]
        dua = (dm * ga).astype(BF16)
        dub = (dm * gb).astype(BF16)
        dua_ref[...] = dua
        dub_ref[...] = dub
        dgl_a = dm * ua * ga * (1.0 - ga)
        dgl_b = dm * ub * gb * (1.0 - gb)
        dgl_ref[:, :D_MODEL] = dgl_a.astype(BF16)
        dgl_ref[:, D_MODEL:] = dgl_b.astype(BF16)
        part = jnp.concatenate([jnp.sum(dgl_a.reshape(t // 8, 8, D_MODEL), axis=0),
                                jnp.sum(dgl_b.reshape(t // 8, 8, D_MODEL), axis=0)], axis=1)

        @pl.when(i == 0)
        def _():
            dbg_ref[...] = part

        @pl.when(i > 0)
        def _():
            dbg_ref[...] += part

        dosb_ref[...] = _dot_nt(dub, ws_ref[...])
        doa = _dot_nt(dua, wd_ref[...])
        rr = lax.broadcasted_iota(jnp.int32, (DIL_W, DIL_W), 0) // HEAD_DIM
        cc = lax.broadcasted_iota(jnp.int32, (DIL_W, DIL_W), 1) // HEAD_DIM
        same_head = (rr == cc).astype(BF16)
        dw = [_dot_f32_by_01(doa * og[g], same_head) for g in range(3)]
        mean_dw = w[0] * dw[0] + w[1] * dw[1] + w[2] * dw[2]
        for g, (d_ref, c_ref) in enumerate(((d0, c0), (d1, c1), (d2, c2))):
            _store_halves(d_ref, rows, w[g] * doa)
            _store_halves(c_ref, rows, -w[g] * mean_dw)

    dil = pl.BlockSpec((2, t, 128), lambda i: (0, i, 0))
    wide = pl.BlockSpec((t, D_MODEL), lambda i: (i, 0))
    gate2 = pl.BlockSpec((t, GATE_W), lambda i: (i, 0))
    sbw = pl.BlockSpec((t, SB_W), lambda i: (i, 0))
    const = lambda shape: pl.BlockSpec(shape, lambda i: (0, 0))
    return pl.pallas_call(
        body,
        name="merge_bwd",
        grid=(s // t,),
        in_specs=[wide] + [dil] * 6 + [sbw, gate2, const((1, GATE_W)), const((DIL_W, D_MODEL)), const((SB_W, D_MODEL)),
                                       ANY],
        out_specs=[wide, wide, gate2, const((8, GATE_W)), sbw] + [dil] * 6 + [ANY],
        out_shape=[jax.ShapeDtypeStruct((s, D_MODEL), BF16), jax.ShapeDtypeStruct((s, D_MODEL), BF16),
                   jax.ShapeDtypeStruct((s, GATE_W), BF16), jax.ShapeDtypeStruct((8, GATE_W), F32),
                   jax.ShapeDtypeStruct((s, SB_W), F32)] + [jax.ShapeDtypeStruct((2, s, 128), F32)] * 6
        + [jax.ShapeDtypeStruct((n_chunks, r_swap // 2, w_swap), swap.dtype)],
        scratch_shapes=[pltpu.SemaphoreType.DMA, pltpu.SemaphoreType.DMA],
        compiler_params=_cparams(("arbitrary",)),
    )(dmerged, *o_groups, *lse_groups, o_sb, gl, b_gate, w_up_dil, w_up_sb, swap)


ANY = pl.BlockSpec(memory_space=pl.ANY)


def _place():
    x, y, c = lax.axis_index("x"), lax.axis_index("y"), lax.axis_index("c")
    other_chips = [(1 - x, y), (x, 1 - y), (1 - x, 1 - y)]
    return x, y, c, other_chips


def _gather_copies(p_ref, out_ref, send_sems, recv_sems):
    x, y, c, chips = _place()
    me, sibling = 2 * x + y, (x, y, 1 - c)
    idx = [2 * chip[0] + chip[1] for chip in chips]

    def copy(k, chip_idx, core, to, src=None):
        return pltpu.make_async_remote_copy(
            src_ref=out_ref.at[chip_idx, core] if src is None else src, dst_ref=out_ref.at[chip_idx, core],
            send_sem=send_sems.at[k], recv_sem=recv_sems.at[k], device_id=to, device_id_type=MESH)

    first = lambda j: copy(j, me, c, (*chips[j], c), src=p_ref.at[c])
    landed = lambda j: copy(j, idx[j], c, (x, y, c))
    passed = lambda j: copy(3 + j, idx[j], c, sibling)
    handed = lambda j: copy(3 + j, idx[j], 1 - c, (x, y, c))
    return first, landed, passed, handed


def _gather_start(*refs):
    first = _gather_copies(*refs)[0]
    for j in range(3):
        first(j).start()


def _gather_pass_on(*refs):
    _, landed, passed, _ = _gather_copies(*refs)
    for j in range(3):
        landed(j).wait_recv()
        passed(j).start()


def _gather_finish(*refs):
    first, _, passed, handed = _gather_copies(*refs)
    for j in range(3):
        handed(j).wait_recv()
    for j in range(3):
        first(j).wait_send()
        passed(j).wait_send()


def _fill_own_slot(others, pack):
    n, _, rh, wd = others.shape
    me = 2 * lax.axis_index("x") + lax.axis_index("y")
    mine = lax.broadcasted_iota(jnp.int32, (n, 1, 1, 1), 0) == me
    return jnp.where(mine, pack.reshape(1, 2, rh, wd), others).reshape(n, 2 * rh, wd)


def _swap_copy(g_ref, out_ref, send_sem, recv_sem):
    x, y, c, _ = _place()
    return pltpu.make_async_remote_copy(
        src_ref=g_ref.at[:, 1 - c], dst_ref=out_ref,
        send_sem=send_sem, recv_sem=recv_sem, device_id=(x, y, 1 - c), device_id_type=MESH)


def _swap_halves(g):
    n, r, wd = g.shape
    rh = r // 2
    g = g.reshape(n, 2, rh, wd)

    def body(g_ref, out_ref, send_sem, recv_sem):
        cp = _swap_copy(g_ref, out_ref, send_sem, recv_sem)
        cp.start()
        cp.wait()

    return pl.pallas_call(
        body,
        name="grad_swap_halves",
        in_specs=[ANY],
        out_specs=ANY,
        out_shape=jax.ShapeDtypeStruct((n, rh, wd), g.dtype),
        scratch_shapes=[pltpu.SemaphoreType.DMA, pltpu.SemaphoreType.DMA],
    )(g)


def _add_halves(g, got, core):
    n, r, wd = g.shape
    rh = r // 2
    t = rh // 4
    nt = rh // t

    def body(c_ref, a_ref, b_ref, o_ref):
        o_ref[...] = (a_ref[0] + b_ref[...]).astype(BF16)

    grid_spec = pltpu.PrefetchScalarGridSpec(
        num_scalar_prefetch=1,
        grid=(n, nt),
        in_specs=[pl.BlockSpec((1, 1, t, wd), lambda s, i, c: (s, c[0], i, 0)),
                  pl.BlockSpec((1, t, wd), lambda s, i, c: (s, i, 0))],
        out_specs=pl.BlockSpec((1, t, wd), lambda s, i, c: (s, i, 0)),
    )
    return pl.pallas_call(
        body,
        name="grad_add_halves",
        grid_spec=grid_spec,
        out_shape=jax.ShapeDtypeStruct((n, rh, wd), BF16),
        compiler_params=_cparams(("parallel", "parallel")),
    )(core, g.reshape(n, 2, rh, wd), got)


def _exchange_copies(h_ref, out_ref, send_sems, recv_sems):
    x, y, c, chips = _place()
    me = 2 * x + y

    def copy(j, slot):
        them = 2 * chips[j][0] + chips[j][1]
        return pltpu.make_async_remote_copy(
            src_ref=h_ref.at[them], dst_ref=out_ref.at[me if slot == "mine" else them],
            send_sem=send_sems.at[j], recv_sem=recv_sems.at[j], device_id=(*chips[j], c), device_id_type=MESH)

    return (lambda j: copy(j, "mine")), (lambda j: copy(j, "theirs"))


def _exchange_start(h_ref, out_ref, send_sems, recv_sems):
    send = _exchange_copies(h_ref, out_ref, send_sems, recv_sems)[0]
    for j in range(3):
        send(j).start()


def _exchange_wait(h_ref, out_ref, send_sems, recv_sems):
    send, arrival = _exchange_copies(h_ref, out_ref, send_sems, recv_sems)
    for j in range(3):
        arrival(j).wait_recv()
    for j in range(3):
        send(j).wait_send()


def _sum_chips(b, h, chip):
    n, rh, wd = b.shape
    t = rh // 4

    def body(chip_ref, b_ref, own_ref, o_ref):
        own = own_ref[0]
        s0, s1, s2, s3 = (jnp.where(chip_ref[0] == k, own, b_ref[k]).astype(F32) for k in range(n))
        o_ref[...] = ((s0 + s1) + s2) + s3

    grid_spec = pltpu.PrefetchScalarGridSpec(
        num_scalar_prefetch=1,
        grid=(rh // t,),
        in_specs=[pl.BlockSpec((n, t, wd), lambda i, chip: (0, i, 0)),
                  pl.BlockSpec((1, t, wd), lambda i, chip: (chip[0], i, 0))],
        out_specs=pl.BlockSpec((t, wd), lambda i, chip: (i, 0)),
    )
    return pl.pallas_call(
        body,
        name="grad_sum_chips",
        grid_spec=grid_spec,
        out_shape=jax.ShapeDtypeStruct((rh, wd), F32),
        compiler_params=_cparams(("parallel",)),
    )(chip, b, h)


def _join_halves(tc):
    rh, wd = tc.shape

    def body(t_ref, out_ref, send_sem, recv_sem):
        x, y, c, _ = _place()
        cp = pltpu.make_async_remote_copy(
            src_ref=t_ref, dst_ref=out_ref.at[c],
            send_sem=send_sem, recv_sem=recv_sem, device_id=(x, y, 1 - c), device_id_type=MESH)
        cp.start()
        cp.wait()

    halves = pl.pallas_call(
        body,
        name="grad_join_halves",
        in_specs=[ANY],
        out_specs=ANY,
        out_shape=jax.ShapeDtypeStruct((2, rh, wd), tc.dtype),
        scratch_shapes=[pltpu.SemaphoreType.DMA, pltpu.SemaphoreType.DMA],
    )(tc)
    return lax.dynamic_update_slice(halves, tc[None], (lax.axis_index("c"), 0, 0)).reshape(2 * rh, wd)


def _all_reduce_small(pack):
    rows, lanes = pack.shape

    def body(p_ref, out_ref, buf, send_sems, recv_sems):
        x, y, c, _ = _place()
        me = 4 * x + 2 * y + c
        buf[me] = p_ref[...]
        sends = []
        for k in range(1, N_DEV):
            peer = (x ^ (k >> 2), y ^ ((k >> 1) & 1), c ^ (k & 1))
            sends.append(pltpu.make_async_remote_copy(
                src_ref=p_ref, dst_ref=buf.at[me], send_sem=send_sems.at[k - 1], recv_sem=recv_sems.at[k - 1],
                device_id=peer, device_id_type=MESH))
        for cp in sends:
            cp.start()
        for k in range(1, N_DEV):
            pltpu.make_async_remote_copy(
                src_ref=p_ref, dst_ref=buf.at[me ^ k], send_sem=send_sems.at[k - 1], recv_sem=recv_sems.at[k - 1],
                device_id=(x, y, c), device_id_type=MESH).wait_recv()
        for cp in sends:
            cp.wait_send()
        total = buf[0]
        for d in range(1, N_DEV):
            total = total + buf[d]
        out_ref[...] = total

    vm = pl.BlockSpec(memory_space=pltpu.VMEM)
    return pl.pallas_call(
        body,
        name="all_reduce_small",
        in_specs=[vm],
        out_specs=vm,
        out_shape=jax.ShapeDtypeStruct((rows, lanes), F32),
        scratch_shapes=[pltpu.VMEM((N_DEV, rows, lanes), F32), pltpu.SemaphoreType.DMA((N_DEV - 1,)),
                        pltpu.SemaphoreType.DMA((N_DEV - 1,))],
    )(pack)


def _adamw(g, w, m, v, name):
    rows, cols = g.shape
    t = rows
    for cand in (256, 128, 64, 32, 16, 8):
        if rows % cand == 0:
            t = cand
            break

    def body(g_ref, w_ref, m_ref, v_ref, d_ref, nm_ref, nv_ref):
        gv = g_ref[...]
        mv = ADAM_B1 * m_ref[...] + (1.0 - ADAM_B1) * gv
        vv = ADAM_B2 * v_ref[...] + (1.0 - ADAM_B2) * (gv * gv)
        m_hat = mv / (1.0 - ADAM_B1 ** ADAM_STEP)
        v_hat = vv / (1.0 - ADAM_B2 ** ADAM_STEP)
        d_ref[...] = -ADAM_LR * (m_hat / (jnp.sqrt(v_hat) + ADAM_EPS) + ADAM_WD * w_ref[...])
        nm_ref[...] = mv
        nv_ref[...] = vv

    blk = pl.BlockSpec((t, cols), lambda i: (i, 0))
    return pl.pallas_call(
        body,
        name=name,
        grid=(rows // t,),
        in_specs=[blk] * 4,
        out_specs=[blk] * 3,
        out_shape=[jax.ShapeDtypeStruct((rows, cols), F32)] * 3,
        compiler_params=_cparams(("parallel",)),
    )(g, w, m, v)


PACK_W = 1024
BIG = (("w_in", (D_MODEL, IN_COLS), 1), ("w_up_dil", (DIL_W, D_MODEL), 1), ("w_up_sb", (SB_W, D_MODEL), 1),
       ("w_out", (D_MODEL, D_MODEL), 0), ("w_mlp_in", (D_MODEL, D_FF), 1), ("w_mlp_out", (D_FF, D_MODEL), 0))


def _shard_shape(shape, axis):
    return tuple(d // N_CHIPS if a == axis else d for a, d in enumerate(shape))


MIXER_GROUP, MLP_GROUP = BIG[:4], BIG[4:]
EARLY_WEIGHTS, LATE_WEIGHTS = BIG[:1], BIG[1:]


def _pack_rows(group=BIG):
    rows, at = {}, 0
    for name, shape, axis in group:
        n = math.prod(_shard_shape(shape, axis)) // PACK_W
        rows[name] = (at, n)
        at += n
    return rows, at


def _pack_shards(shards, group):
    return jnp.concatenate([shards[name].reshape(-1, PACK_W) for name, _, _ in group], axis=0)


def _unpack_full(gathered, group):
    rows, _ = _pack_rows(group)
    full = {}
    for name, shape, axis in group:
        at, n = rows[name]
        parts = gathered[:, at:at + n, :].reshape((N_CHIPS,) + _shard_shape(shape, axis))
        if axis == 0:
            full[name] = parts.reshape(shape)
        else:
            full[name] = jnp.transpose(parts, (1, 0, 2)).reshape(shape)
    return full


def _pack_full_grads(grads, group):
    chunks = []
    for name, shape, axis in group:
        g = grads[name]
        if axis == 0:
            parts = g.reshape((N_CHIPS, shape[0] // N_CHIPS, shape[1]))
        else:
            parts = jnp.transpose(g.reshape((shape[0], N_CHIPS, shape[1] // N_CHIPS)), (1, 0, 2))
        chunks.append(parts.reshape(N_CHIPS, -1, PACK_W))
    return jnp.concatenate(chunks, axis=1)


def _unpack_shard(packed, group):
    rows, _ = _pack_rows(group)
    return {name: packed[rows[name][0]:rows[name][0] + rows[name][1]].reshape(_shard_shape(shape, axis))
            for name, shape, axis in group}


def _local_step(x, target, early_shards, late_shards, norm_mix_g, b_gate, norm_mlp_g, norm_final_g, core):
    h, early = _rms_fwd_and_gather(x, norm_mix_g, early_shards)
    w = _unpack_full(early, EARLY_WEIGHTS)
    w_in = w["w_in"]
    sb0 = 9 * DIL_W
    w_sb, w_gate = w_in[:, sb0:QKV_W], w_in[:, QKV_W:]
    w_dil = [jnp.concatenate([w_in[:, (3 * i + g) * DIL_W:(3 * i + g + 1) * DIL_W] for i in range(3)], axis=1)
             for g in range(3)]

    qkv_dil = [_matmul(h, w_dil[g], mode="nn", out_dtypes=(BF16,), name=f"proj_dil_g{g}", tn=768)[0] for g in range(3)]
    (qkv_sb,) = _matmul(h, w_sb, mode="nn", out_dtypes=(BF16,), name="proj_sb", tn=768)
    (gl,) = _matmul(h, w_gate, mode="nn", out_dtypes=(F32,), name="proj_gate")
    dil = [_dil_fwd(qkv_dil[g], g) for g in range(3)]
    o_groups, lse_groups = [d[0] for d in dil], [d[1] for d in dil]
    o_sb, a_sb, late_others = _sb_fwd(qkv_sb, late_shards)
    w = {**w, **_unpack_full(_fill_own_slot(late_others, late_shards), LATE_WEIGHTS)}
    merged, o_a = _merge_fwd(o_groups, lse_groups, o_sb, gl, b_gate, w["w_up_dil"], w["w_up_sb"])
    def residual_and_norm(acc, res, g):
        x1 = res + acc
        return x1, _rms_rows(x1)[0] * g

    x1, h2 = _matmul(merged, w["w_out"], mode="nn", out_dtypes=(F32, BF16), name="out_proj", tm=ROW_TILE,
                     extras=(x, norm_mlp_g), epilogue=residual_and_norm)
    u, act = _matmul(h2, w["w_mlp_in"], mode="nn", out_dtypes=(BF16, BF16), name="mlp_in",
                     epilogue=lambda acc: (acc, jnp.square(jnp.maximum(acc, 0.0))))

    def residual_and_loss(acc, res, tgt, g):
        xh, r = _rms_rows(res + acc)
        err = xh * g - tgt
        dy = err * (1.0 / D_MODEL)
        dxh = dy * g
        dx2 = r * (dxh - xh * jnp.mean(dxh * xh, axis=-1, keepdims=True))
        return dx2, _rows_sum8(dy * xh), (0.5 / D_MODEL) * _rows_sum8(err * err)

    dx2, dg_final, loss_part = _matmul(
        act, w["w_mlp_out"], mode="nn", out_dtypes=(F32, ("part", F32), ("part", F32)), name="mlp_out", tm=ROW_TILE,
        tk=2048, extras=(x1, target, norm_final_g.reshape(1, D_MODEL)), epilogue=residual_and_loss)

    (du,) = _matmul(dx2, w["w_mlp_out"], mode="nt", out_dtypes=(BF16,), name="mlp_out_dx",
                    extras=(u,), epilogue=lambda acc, uu: (acc * (2.0 * jnp.maximum(uu.astype(F32), 0.0)),))
    (g_mlp_out,) = _matmul(act, dx2, mode="tn", out_dtypes=(F32,), name="mlp_out_dw")
    (g_mlp_in,) = _matmul(h2, du, mode="tn", out_dtypes=(F32,), name="mlp_in_dw")

    def norm_bwd(acc, xx, dres, g):
        dx, dg = _rms_bwd_rows(acc, xx, g)
        return dres + dx, dg

    dx1, dg_mlp = _matmul(du, w["w_mlp_in"], mode="nt", out_dtypes=(F32, ("part", F32)), name="mlp_in_dx",
                          tm=ROW_TILE, tk=2048, extras=(x1, dx2, norm_mlp_g), epilogue=norm_bwd)

    mlp_pack = _pack_full_grads({"w_mlp_in": g_mlp_in, "w_mlp_out": g_mlp_out}, MLP_GROUP)
    (dmerged,) = _matmul(dx1, w["w_out"], mode="nt", out_dtypes=(F32,), name="out_proj_dx")
    (g_out,) = _matmul(merged, dx1, mode="tn", out_dtypes=(F32,), name="out_proj_dw")
    mb = _merge_bwd(dmerged, o_groups, lse_groups, o_sb, gl, b_gate, w["w_up_dil"], w["w_up_sb"], mlp_pack)
    dua, dub, dgl, dbg, do_sb = mb[:5]
    do_groups, c_groups = mb[5:8], mb[8:11]
    mlp_sums = _add_halves(mlp_pack, mb[11], core)
    (g_up_dil,) = _matmul(o_a, dua, mode="tn", out_dtypes=(F32,), name="up_dil_dw")
    (g_up_sb,) = _matmul(o_sb, dub, mode="tn", out_dtypes=(F32,), name="up_sb_dw")
    dq_sb, dk_sb, dv_sb, mlp_got = _sb_bwd(qkv_sb, do_sb, a_sb, mlp_sums)
    dil_b = [_dil_bwd(qkv_dil[g], do_groups[g], lse_groups[g], c_groups[g], g) for g in range(3)]
    dproj = jnp.concatenate(
        [dil_b[g][i].astype(BF16) for i in range(3) for g in range(3)]
        + [t.astype(BF16) for t in (dq_sb, dk_sb, dv_sb)] + [dgl], axis=1)
    (g_in,) = _matmul(h, dproj, mode="tn", out_dtypes=(F32,), name="proj_dw", tm=512, tn=IN_COLS // 2)
    mixer_pack = _pack_full_grads({"w_in": g_in, "w_up_dil": g_up_dil, "w_up_sb": g_up_sb, "w_out": g_out}, MIXER_GROUP)
    mixer_sums = _add_halves(mixer_pack, _swap_halves(mixer_pack), core)
    grad_x, dg_mix, mixer_got = _matmul(
        dproj, w["w_in"], mode="nt", out_dtypes=(F32, ("part", F32)), name="proj_dx", tm=ROW_TILE, tk=IN_COLS // 2,
        extras=(x, dx1, norm_mix_g), epilogue=norm_bwd, exchange=mixer_sums)

    small = (dg_mix, dbg, dg_mlp, dg_final, loss_part)
    return grad_x, (mixer_got, mixer_sums), (mlp_got, mlp_sums), small


def kernel(x, norm_mix_g, w_in, b_gate, w_up_dil, w_up_sb, w_out, norm_mlp_g, w_mlp_in, w_mlp_out, norm_final_g, loss_target, m_norm_mix_g, m_w_in, m_b_gate, m_w_up_dil, m_w_up_sb, m_w_out, m_norm_mlp_g, m_w_mlp_in, m_w_mlp_out, m_norm_final_g, v_norm_mix_g, v_w_in, v_b_gate, v_w_up_dil, v_w_up_sb, v_w_out, v_norm_mlp_g, v_w_mlp_in, v_w_mlp_out, v_norm_final_g):
    shards = {"w_in": w_in[0], "w_up_dil": w_up_dil[0], "w_up_sb": w_up_sb[0], "w_out": w_out[0],
              "w_mlp_in": w_mlp_in[0], "w_mlp_out": w_mlp_out[0]}
    moments_m = {"w_in": m_w_in[0], "w_up_dil": m_w_up_dil[0], "w_up_sb": m_w_up_sb[0], "w_out": m_w_out[0],
                 "w_mlp_in": m_w_mlp_in[0], "w_mlp_out": m_w_mlp_out[0]}
    moments_v = {"w_in": v_w_in[0], "w_up_dil": v_w_up_dil[0], "w_up_sb": v_w_up_sb[0], "w_out": v_w_out[0],
                 "w_mlp_in": v_w_mlp_in[0], "w_mlp_out": v_w_mlp_out[0]}

    shards16 = {n: s.astype(BF16) for n, s in shards.items()}
    early_shards = _pack_shards(shards16, EARLY_WEIGHTS)
    late_shards = _pack_shards(shards16, LATE_WEIGHTS)

    core = lax.axis_index("c").astype(jnp.int32).reshape(1)
    chip = (2 * lax.axis_index("x") + lax.axis_index("y")).astype(jnp.int32).reshape(1)
    grad_x, (mixer_got, mixer_sums), (mlp_got, mlp_sums), small = _local_step(
        x[0], loss_target[0], early_shards, late_shards, norm_mix_g, b_gate, norm_mlp_g, norm_final_g, core)

    reduced = _join_halves(_sum_chips(mixer_got, mixer_sums, chip))
    reduced_mlp = _join_halves(_sum_chips(mlp_got, mlp_sums, chip))
    g_shard = {**_unpack_shard(reduced, MIXER_GROUP), **_unpack_shard(reduced_mlp, MLP_GROUP)}

    dg_mix, dbg, dg_mlp, dg_final, loss_part = small
    loss_row = jnp.sum(loss_part, axis=0, keepdims=True)
    small_pack = jnp.concatenate(
        [jnp.sum(dg_mix, axis=0, keepdims=True), jnp.sum(dbg, axis=0, keepdims=True),
         jnp.sum(dg_mlp, axis=0, keepdims=True), jnp.sum(dg_final, axis=0, keepdims=True), loss_row], axis=1)
    n_small = small_pack.shape[1]
    small_sum = _all_reduce_small(small_pack.reshape(n_small // 128, 128)).reshape(1, n_small)
    g_norm_mix = small_sum[:, :D_MODEL]
    g_b_gate = small_sum[:, D_MODEL:3 * D_MODEL]
    g_norm_mlp = small_sum[:, 3 * D_MODEL:4 * D_MODEL]
    g_norm_final = small_sum[:, 4 * D_MODEL:5 * D_MODEL]
    loss = jnp.sum(small_sum[:, 5 * D_MODEL:])

    names = ["norm_mix_g", "w_in", "b_gate", "w_up_dil", "w_up_sb", "w_out", "norm_mlp_g", "w_mlp_in", "w_mlp_out",
             "norm_final_g"]
    grads = dict(g_shard)
    grads.update(norm_mix_g=g_norm_mix, b_gate=g_b_gate, norm_mlp_g=g_norm_mlp, norm_final_g=g_norm_final)
    weights = dict(shards)
    weights.update(norm_mix_g=norm_mix_g, b_gate=b_gate, norm_mlp_g=norm_mlp_g, norm_final_g=norm_final_g.reshape(1, D_MODEL))
    ms = dict(moments_m)
    ms.update(norm_mix_g=m_norm_mix_g, b_gate=m_b_gate, norm_mlp_g=m_norm_mlp_g, norm_final_g=m_norm_final_g.reshape(1, D_MODEL))
    vs = dict(moments_v)
    vs.update(norm_mix_g=v_norm_mix_g, b_gate=v_b_gate, norm_mlp_g=v_norm_mlp_g, norm_final_g=v_norm_final_g.reshape(1, D_MODEL))

    out_shapes = {"norm_mix_g": norm_mix_g.shape, "w_in": w_in.shape, "b_gate": b_gate.shape, "w_up_dil": w_up_dil.shape,
                  "w_up_sb": w_up_sb.shape, "w_out": w_out.shape, "norm_mlp_g": norm_mlp_g.shape,
                  "w_mlp_in": w_mlp_in.shape, "w_mlp_out": w_mlp_out.shape, "norm_final_g": norm_final_g.shape}
    g_out, d_out, m_out, v_out = [], [], [], []
    for n in names:
        d, nm, nv = _adamw(grads[n], weights[n], ms[n], vs[n], "adamw_" + n)
        shape = out_shapes[n]
        g_out.append(grads[n].reshape(shape))
        d_out.append(d.reshape(shape))
        m_out.append(nm.reshape(shape))
        v_out.append(nv.reshape(shape))
    return (loss, grad_x.reshape(x.shape), *g_out, *d_out, *m_out, *v_out)
```

```python
import math

import jax
import jax.numpy as jnp
import numpy as np
from jax import lax
from jax.experimental import pallas as pl
from jax.experimental.pallas import tpu as pltpu

F32 = jnp.float32
BF16 = jnp.bfloat16
MESH = pl.DeviceIdType.MESH

D_MODEL = 1024
HEAD_DIM = 64
DIL_GROUPS = ((128, 1), (512, 4), (2048, 16))
DIL_HEADS = 4
DIL_W = 256
N_DIL_HEADS = 12
SB_HEADS = 8
SB_W = SB_HEADS * HEAD_DIM
QKV_W = 3 * 3 * DIL_W + 3 * SB_W
GATE_W = 2 * D_MODEL
IN_COLS = QKV_W + GATE_W
D_FF = 4 * D_MODEL
BLOCK = 128
RMS_EPS = 1e-6
NEG_INF = -1e30
N_CHIPS = 4
N_DEV = 8

ADAM_LR = 0.001
ADAM_B1 = 0.9
ADAM_B2 = 0.999
ADAM_EPS = 1e-08
ADAM_WD = 0.01
ADAM_STEP = 10

VMEM_LIMIT = 56 * 1024 * 1024

SB_BQ = 256
SB_BK = 256


def _cparams(sem=None):
    if sem is None:
        return pltpu.CompilerParams(vmem_limit_bytes=VMEM_LIMIT)
    return pltpu.CompilerParams(dimension_semantics=sem, vmem_limit_bytes=VMEM_LIMIT)


def _dot(a, b, dims):
    return lax.dot_general(a, b, (dims, ((), ())), preferred_element_type=F32)


def _dot_nn(a, b):
    return _dot(a, b, ((1,), (0,)))


def _dot_nt(a, b):
    return _dot(a, b, ((1,), (1,)))


def _dot_tn(a, b):
    return _dot(a, b, ((0,), (0,)))


def _dot_f32_by_01(x, m01, pieces=3):
    hi = x.astype(BF16)
    if pieces == 1:
        return _dot_nn(hi, m01)
    r1 = x - hi.astype(F32)
    mid = r1.astype(BF16)
    if pieces == 2:
        return _dot_nn(hi, m01) + _dot_nn(mid, m01)
    lo = (r1 - mid.astype(F32)).astype(BF16)
    return _dot_nn(hi, m01) + _dot_nn(mid, m01) + _dot_nn(lo, m01)


def _matmul(a, b, *, mode, out_dtypes, name, tm=1024, tn=1024, tk=1024, extras=(), epilogue=None, exchange=None):
    if mode == "nn":
        (m, k), (k2, n) = a.shape, b.shape
    elif mode == "nt":
        (m, k), (n, k2) = a.shape, b.shape
    else:
        (k, m), (k2, n) = a.shape, b.shape
    assert k == k2, (a.shape, b.shape, mode)
    tm, tn, tk = min(tm, m), min(tn, n), min(tk, k)
    assert m % tm == 0 and n % tn == 0 and k % tk == 0, (m, n, k, tm, tn, tk)
    nk = k // tk
    n_out = len(out_dtypes)
    n_ex = len(extras)

    if mode == "nn":
        a_spec = pl.BlockSpec((tm, tk), lambda i, j, kk: (i, kk))
        b_spec = pl.BlockSpec((tk, tn), lambda i, j, kk: (kk, j))
        dot = _dot_nn
    elif mode == "nt":
        a_spec = pl.BlockSpec((tm, tk), lambda i, j, kk: (i, kk))
        b_spec = pl.BlockSpec((tn, tk), lambda i, j, kk: (j, kk))
        dot = _dot_nt
    else:
        a_spec = pl.BlockSpec((tk, tm), lambda i, j, kk: (kk, i))
        b_spec = pl.BlockSpec((tk, tn), lambda i, j, kk: (kk, j))
        dot = _dot_tn
    mn_spec = pl.BlockSpec((tm, tn), lambda i, j, kk: (i, j))
    row_spec = pl.BlockSpec((1, tn), lambda i, j, kk: (0, j))
    part_spec = pl.BlockSpec((8, tn), lambda i, j, kk: (i, j))
    ex_specs = [row_spec if e.shape[0] == 1 else mn_spec for e in extras]
    is_part = [isinstance(dt, tuple) for dt in out_dtypes]
    out_dts = [dt[1] if p else dt for dt, p in zip(out_dtypes, is_part)]
    out_specs = [part_spec if p else mn_spec for p in is_part]
    out_shapes = [jax.ShapeDtypeStruct((8 * (m // tm), n) if p else (m, n), dt) for dt, p in zip(out_dts, is_part)]

    n_side = 0 if exchange is None else 1
    grid = (m // tm, n // tn, nk)

    def body(*refs):
        a_ref, b_ref = refs[0], refs[1]
        ex_refs = refs[2:2 + n_ex]
        out_refs = refs[2 + n_ex + n_side:2 + n_ex + n_side + n_out]
        scratch = refs[2 + n_ex + n_side + n_out + n_side:]
        acc_ref = scratch[0] if nk > 1 else None
        if exchange is not None:
            side = (refs[2 + n_ex], refs[2 + n_ex + n_side + n_out]) + tuple(scratch[-2:])
            step = (pl.program_id(0) * grid[1] + pl.program_id(1)) * grid[2] + pl.program_id(2)

            @pl.when(step == 0)
            def _():
                _exchange_start(*side)

            @pl.when(step == grid[0] * grid[1] * grid[2] - 1)
            def _():
                _exchange_wait(*side)

        part = dot(a_ref[...].astype(BF16), b_ref[...].astype(BF16))

        def finish(acc):
            if epilogue is None:
                outs = (acc,)
            else:
                outs = epilogue(acc, *[r[...] for r in ex_refs])
            for o_ref, o in zip(out_refs, outs):
                o_ref[...] = o.astype(o_ref.dtype)

        if nk == 1:
            finish(part)
        else:
            kk = pl.program_id(2)

            @pl.when(kk == 0)
            def _():
                acc_ref[...] = part

            @pl.when(kk > 0)
            def _():
                acc_ref[...] += part

            @pl.when(kk == nk - 1)
            def _():
                finish(acc_ref[...])

    side_in = [] if exchange is None else [exchange]
    outs = pl.pallas_call(
        body,
        name=name,
        grid=grid,
        in_specs=[a_spec, b_spec] + ex_specs + [ANY] * n_side,
        out_specs=out_specs + [ANY] * n_side,
        out_shape=out_shapes + [jax.ShapeDtypeStruct(e.shape, e.dtype) for e in side_in],
        scratch_shapes=([pltpu.VMEM((tm, tn), F32)] if nk > 1 else [])
        + [pltpu.SemaphoreType.DMA((3,)), pltpu.SemaphoreType.DMA((3,))] * n_side,
        compiler_params=_cparams(("arbitrary",) * 3 if n_side else ("parallel", "parallel", "arbitrary")),
    )(a, b, *extras, *side_in)
    return outs


ROW_TILE = 512


def _rows_sum8(t):
    rows, d = t.shape
    return jnp.sum(t.reshape(rows // 8, 8, d), axis=0)


def _rms_rows(x):
    r = lax.rsqrt(jnp.mean(x * x, axis=-1, keepdims=True) + RMS_EPS)
    return x * r, r


def _rms_bwd_rows(dh, x, g):
    xh, r = _rms_rows(x)
    dxh = dh * g
    return r * (dxh - xh * jnp.mean(dxh * xh, axis=-1, keepdims=True)), _rows_sum8(dh * xh)


def _rms_fwd_and_gather(x, g, shard_pack):
    s, d = x.shape
    r_pack, w_pack = shard_pack.shape
    steps = s // ROW_TILE

    def body(x_ref, g_ref, pack_ref, h_ref, others_ref, send_sems, recv_sems):
        i = pl.program_id(0)
        gather = (pack_ref, others_ref, send_sems, recv_sems)

        @pl.when(i == 0)
        def _():
            _gather_start(*gather)

        h_ref[...] = (_rms_rows(x_ref[...])[0] * g_ref[...]).astype(BF16)

        @pl.when(i == steps - 1)
        def _():
            _gather_pass_on(*gather)
            _gather_finish(*gather)

    h, others = pl.pallas_call(
        body,
        name="norm_mix",
        grid=(steps,),
        in_specs=[pl.BlockSpec((ROW_TILE, d), lambda i: (i, 0)), pl.BlockSpec((1, d), lambda i: (0, 0)), ANY],
        out_specs=[pl.BlockSpec((ROW_TILE, d), lambda i: (i, 0)), ANY],
        out_shape=[jax.ShapeDtypeStruct((s, d), BF16),
                   jax.ShapeDtypeStruct((N_CHIPS, 2, r_pack // 2, w_pack), shard_pack.dtype)],
        scratch_shapes=[pltpu.SemaphoreType.DMA((6,)), pltpu.SemaphoreType.DMA((6,))],
        compiler_params=_cparams(("arbitrary",)),
    )(x, g, shard_pack.reshape(2, r_pack // 2, w_pack))
    return h, _fill_own_slot(others, shard_pack)


def _alibi_slopes():
    return np.exp2(np.float32(-8.0) * np.arange(1, N_DIL_HEADS + 1, dtype=np.float32) / np.float32(N_DIL_HEADS))


def _head_lane_mask(h, rows):
    lane = lax.broadcasted_iota(jnp.int32, (rows, DIL_W), 1)
    return (lane >= h * HEAD_DIM) & (lane < (h + 1) * HEAD_DIM)


def _band_terms(dil, has_prev):
    qi = lax.broadcasted_iota(jnp.int32, (BLOCK, 2 * BLOCK), 0)
    kj = lax.broadcasted_iota(jnp.int32, (BLOCK, 2 * BLOCK), 1)
    steps = qi + BLOCK - kj
    valid = (steps >= 0) & (steps <= BLOCK) & ((kj >= BLOCK) | has_prev)
    return valid, steps.astype(F32) * float(dil)


def _load_halves(ref, rows):
    return jnp.concatenate([ref[0, rows, :], ref[1, rows, :]], axis=1)


def _store_halves(ref, rows, value):
    ref[0, rows, :] = value[:, :128]
    ref[1, rows, :] = value[:, 128:]


def _dil_fwd(qkv_g, group):
    _, dil = DIL_GROUPS[group]
    s = qkv_g.shape[0]
    sub = s // dil
    nb = sub // BLOCK
    view = qkv_g.reshape(sub, dil * 3 * DIL_W)
    slopes = _alibi_slopes()[group * DIL_HEADS:(group + 1) * DIL_HEADS]

    def col(which):
        return lambda n, r: (n, r * 3 + which)

    def col_prev(which):
        return lambda n, r: (jnp.maximum(n - 1, 0), r * 3 + which)

    def body(q_ref, kc_ref, kp_ref, vc_ref, vp_ref, o_ref, lse_ref):
        n, r = pl.program_id(0), pl.program_id(1)
        mine = pl.ds(r, BLOCK, stride=dil) if dil > 1 else slice(None)
        valid, dist = _band_terms(dil, n > 0)
        q = q_ref[...]
        k2 = jnp.concatenate([kp_ref[...], kc_ref[...]], axis=0)
        v2 = jnp.concatenate([vp_ref[...], vc_ref[...]], axis=0)
        masks = [_head_lane_mask(h, BLOCK) for h in range(DIL_HEADS)]
        logits = [_dot_nt(jnp.where(masks[h], q, jnp.zeros_like(q)), k2) for h in range(DIL_HEADS)]
        ps, lses = [], []
        for h in range(DIL_HEADS):
            lg = jnp.where(valid, logits[h] * 0.125 - float(slopes[h]) * dist, NEG_INF)
            mx = jnp.max(lg, axis=1, keepdims=True)
            lse = mx + jnp.log(jnp.sum(jnp.exp(lg - mx), axis=1, keepdims=True))
            ps.append(jnp.exp(lg - lse).astype(BF16))
            lses.append(lse)
        o_acc = jnp.zeros((BLOCK, DIL_W), F32)
        lse_acc = jnp.zeros((BLOCK, DIL_W), F32)
        for h in range(DIL_HEADS):
            o_acc = jnp.where(masks[h], _dot_nn(ps[h], v2), o_acc)
            lse_acc = jnp.where(masks[h], lses[h], lse_acc)
        _store_halves(o_ref, mine, o_acc)
        _store_halves(lse_ref, mine, lse_acc)

    blk = (BLOCK, DIL_W)
    return pl.pallas_call(
        body,
        name=f"dil_fwd_g{group}",
        grid=(nb, dil),
        in_specs=[pl.BlockSpec(blk, col(0)), pl.BlockSpec(blk, col(1)), pl.BlockSpec(blk, col_prev(1)),
                  pl.BlockSpec(blk, col(2)), pl.BlockSpec(blk, col_prev(2))],
        out_specs=[pl.BlockSpec((2, BLOCK * dil, 128), lambda n, r: (0, n, 0))] * 2,
        out_shape=[jax.ShapeDtypeStruct((2, s, 128), F32)] * 2,
        compiler_params=_cparams(("parallel", "arbitrary")),
    )(view, view, view, view, view)


def _dil_bwd(qkv, do, lse, cterm, group):
    _, dil = DIL_GROUPS[group]
    s = qkv.shape[0]
    sub = s // dil
    nb = sub // BLOCK
    view = qkv.reshape(sub, dil * 3 * DIL_W)
    slopes = _alibi_slopes()[group * DIL_HEADS:(group + 1) * DIL_HEADS]

    def col(which, shift):
        if shift == 0:
            return lambda n, r: (n, r * 3 + which)
        if shift < 0:
            return lambda n, r: (jnp.maximum(n - 1, 0), r * 3 + which)
        return lambda n, r: (jnp.minimum(n + 1, nb - 1), r * 3 + which)

    def own(shift):
        if shift == 0:
            return pl.BlockSpec((2, BLOCK * dil, 128), lambda n, r: (0, n, 0))
        return pl.BlockSpec((2, BLOCK * dil, 128), lambda n, r: (0, jnp.minimum(n + 1, nb - 1), 0))

    def body(q_ref, qn_ref, kc_ref, kp_ref, vc_ref, vp_ref, do_ref, don_ref, lse_ref, lsen_ref, c_ref, cn_ref,
             dq_ref, dk_ref, dv_ref):
        n, r = pl.program_id(0), pl.program_id(1)
        mine = pl.ds(r, BLOCK, stride=dil) if dil > 1 else slice(None)
        valid, dist = _band_terms(dil, n > 0)
        valid_n = _band_terms(dil, True)[0][:, :BLOCK] & (n < nb - 1)
        dist_n = dist[:, :BLOCK]
        q, qn = q_ref[...], qn_ref[...]
        kc, vc = kc_ref[...], vc_ref[...]
        k2 = jnp.concatenate([kp_ref[...], kc], axis=0)
        v2 = jnp.concatenate([vp_ref[...], vc], axis=0)
        dov, donv = _load_halves(do_ref, mine), _load_halves(don_ref, mine)
        lsev, lsenv = _load_halves(lse_ref, mine), _load_halves(lsen_ref, mine)
        cv, cnv = _load_halves(c_ref, mine), _load_halves(cn_ref, mine)
        masks = [_head_lane_mask(h, BLOCK) for h in range(DIL_HEADS)]

        def head_col(t, hm):
            return jnp.max(jnp.where(hm, t, NEG_INF), axis=1, keepdims=True)

        qhs = [jnp.where(hm, q, jnp.zeros_like(q)) for hm in masks]
        qnhs = [jnp.where(hm, qn, jnp.zeros_like(qn)) for hm in masks]
        dohs = [jnp.where(hm, dov, 0.0).astype(BF16) for hm in masks]
        donhs = [jnp.where(hm, donv, 0.0).astype(BF16) for hm in masks]
        logit = [_dot_nt(qhs[h], k2) for h in range(DIL_HEADS)]
        dp = [_dot_nt(dohs[h], v2) for h in range(DIL_HEADS)]
        logit_n = [_dot_nt(qnhs[h], kc) for h in range(DIL_HEADS)]
        dp_n = [_dot_nt(donhs[h], vc) for h in range(DIL_HEADS)]
        p16, dlog, pn16, dlog_n = [], [], [], []
        for h in range(DIL_HEADS):
            hm, slope = masks[h], float(slopes[h])
            p = jnp.where(valid, jnp.exp(logit[h] * 0.125 - slope * dist - head_col(lsev, hm)), 0.0)
            dlog.append((p * (dp[h] + head_col(cv, hm)) * 0.125).astype(BF16))
            p16.append(p.astype(BF16))
            pn = jnp.where(valid_n, jnp.exp(logit_n[h] * 0.125 - slope * dist_n - head_col(lsenv, hm)), 0.0)
            dlog_n.append((pn * (dp_n[h] + head_col(cnv, hm)) * 0.125).astype(BF16))
            pn16.append(pn.astype(BF16))
        dq_acc = jnp.zeros((BLOCK, DIL_W), F32)
        dk_acc = jnp.zeros((BLOCK, DIL_W), F32)
        dv_acc = jnp.zeros((BLOCK, DIL_W), F32)
        for h in range(DIL_HEADS):
            dq_acc = jnp.where(masks[h], _dot_nn(dlog[h], k2), dq_acc)
            dk_acc += _dot_tn(dlog[h][:, BLOCK:], qhs[h]) + _dot_tn(dlog_n[h], qnhs[h])
            dv_acc += _dot_tn(p16[h][:, BLOCK:], dohs[h]) + _dot_tn(pn16[h], donhs[h])
        dq_ref[...] = dq_acc.astype(BF16)
        dk_ref[...] = dk_acc.astype(BF16)
        dv_ref[...] = dv_acc.astype(BF16)

    blk = (BLOCK, DIL_W)
    outs = pl.pallas_call(
        body,
        name=f"dil_bwd_g{group}",
        grid=(nb, dil),
        in_specs=[pl.BlockSpec(blk, col(0, 0)), pl.BlockSpec(blk, col(0, 1)),
                  pl.BlockSpec(blk, col(1, 0)), pl.BlockSpec(blk, col(1, -1)),
                  pl.BlockSpec(blk, col(2, 0)), pl.BlockSpec(blk, col(2, -1)),
                  own(0), own(1), own(0), own(1), own(0), own(1)],
        out_specs=[pl.BlockSpec(blk, lambda n, r: (n, r))] * 3,
        out_shape=[jax.ShapeDtypeStruct((sub, dil * DIL_W), BF16)] * 3,
        compiler_params=_cparams(("parallel", "parallel")),
    )(view, view, view, view, view, view, do, do, lse, lse, cterm, cterm)
    return tuple(t.reshape(s, DIL_W) for t in outs)


SB_PAIRS = SB_HEADS // 2
SB_COL0 = 0
LOG2E = 1.4426950408889634


SB_EXP_CLAMP = 64.0


def _sb_softplus2(zs):
    t = 1.0 + jnp.exp2(jnp.minimum(zs, SB_EXP_CLAMP))
    return jnp.maximum(jnp.log(t) * LOG2E, zs)


def _sb_consts(nkb):
    row = lax.broadcasted_iota(jnp.int32, (SB_BQ, SB_BK), 0)
    colk = lax.broadcasted_iota(jnp.int32, (SB_BQ, SB_BK), 1)
    rr = lax.broadcasted_iota(jnp.int32, (SB_BK, SB_BK), 0)
    cc = lax.broadcasted_iota(jnp.int32, (SB_BK, SB_BK), 1)
    lane = lax.broadcasted_iota(jnp.int32, (SB_BQ, 128), 1)
    assert 2 * nkb <= 128
    return colk < row, rr, cc, lane < HEAD_DIM, lane


def _split_heads(t):
    first = lax.broadcasted_iota(jnp.int32, t.shape, 1) < HEAD_DIM
    zero = jnp.zeros_like(t)
    return jnp.where(first, t, zero), jnp.where(first, zero, t)


def _sb_fwd(qkv, shard_pack):
    s = qkv.shape[0]
    nq, nkb = s // SB_BQ, s // SB_BK
    zscale = LOG2E / math.sqrt(HEAD_DIM)
    r_pack, w_pack = shard_pack.shape

    def body(q_ref, k_ref, v_ref, pack_ref, o_ref, a_row, others_ref, zs_scr, a_scr, acc_scr, cl_scr,
             send_sems, recv_sems):
        i = pl.program_id(1)
        pair = pl.program_id(0)
        gather = (pack_ref, others_ref, send_sems, recv_sems)

        @pl.when((pair == 0) & (i == 0))
        def _():
            _gather_start(*gather)

        @pl.when((pair == 1) & (i == 0))
        def _():
            _gather_pass_on(*gather)

        @pl.when((pair == SB_PAIRS - 1) & (i == nq - 1))
        def _():
            _gather_finish(*gather)

        causal, rr, cc, _, _ = _sb_consts(nkb)
        later = (rr > cc).astype(BF16)
        qh = _split_heads(q_ref[...])

        def rows(j):
            return pl.ds(pl.multiple_of(j * SB_BK, SB_BK), SB_BK)

        def scores_to(slot, j):
            kb = k_ref[rows(j), :]
            for hh in range(2):
                zs_scr[slot, hh] = _dot_nt(qh[hh], kb) * zscale

        def weights(slot, j, masked):
            xs, sums, sufs = [], [], []
            for hh in range(2):
                zs = zs_scr[slot, hh]
                sp = _sb_softplus2(zs)
                if masked:
                    sp = jnp.where(causal, sp, 0.0)
                xs.append(zs - sp)
                sums.append(jnp.sum(sp, axis=1, keepdims=True))
                sufs.append(_dot_f32_by_01(sp, later, 2))
            for hh in range(2):
                cl = cl_scr[hh]
                a = jnp.exp2(xs[hh] - (sufs[hh] + jnp.concatenate([cl, cl], axis=1)))
                if masked:
                    a = jnp.where(causal, a, 0.0)
                a16 = a.astype(BF16)
                a_scr[slot, :, hh * SB_BK:(hh + 1) * SB_BK] = a16
                a_row[0, 0, j, :, hh * SB_BK:(hh + 1) * SB_BK] = a16
                cl_scr[hh] = cl + sums[hh]

        def add_av(slot, j):
            v0, v1 = _split_heads(v_ref[rows(j), :])
            acc_scr[...] += _dot_nn(a_scr[slot], jnp.concatenate([v0, v1], axis=0))

        acc_scr[...] = jnp.zeros_like(acc_scr)
        cl_scr[...] = jnp.zeros_like(cl_scr)
        scores_to(0, i)
        scores_to(1, jnp.maximum(i - 1, 0))
        weights(0, i, True)

        def step(j, prev, cur):
            scores_to(prev, jnp.maximum(j - 1, 0))
            add_av(prev, j + 1)
            weights(cur, j, False)

        def two_steps(u, _):
            j = i - 1 - 2 * u
            step(j, 0, 1)
            step(j - 1, 1, 0)
            return 0

        lax.fori_loop(0, i // 2, two_steps, 0)

        @pl.when(i % 2 == 1)
        def _():
            step(0, 0, 1)
            add_av(1, 0)

        @pl.when(i % 2 == 0)
        def _():
            add_av(0, 0)

        o_ref[...] = acc_scr[...]

    def full(which):
        return pl.BlockSpec((s, 128), lambda p, i: (0, SB_COL0 + 4 * which + p))

    return pl.pallas_call(
        body,
        name="sb_fwd",
        grid=(SB_PAIRS, nq),
        in_specs=[pl.BlockSpec((SB_BQ, 128), lambda p, i: (i, SB_COL0 + p)), full(1), full(2), ANY],
        out_specs=[pl.BlockSpec((SB_BQ, 128), lambda p, i: (i, p)),
                   pl.BlockSpec((1, 1, nkb, SB_BQ, 2 * SB_BK), lambda p, i: (p, i, 0, 0, 0)), ANY],
        out_shape=[jax.ShapeDtypeStruct((s, SB_W), F32),
                   jax.ShapeDtypeStruct((SB_PAIRS, nq, nkb, SB_BQ, 2 * SB_BK), BF16),
                   jax.ShapeDtypeStruct((N_CHIPS, 2, r_pack // 2, w_pack), shard_pack.dtype)],
        scratch_shapes=[pltpu.VMEM((2, 2, SB_BQ, SB_BK), F32), pltpu.VMEM((2, SB_BQ, 2 * SB_BK), BF16),
                        pltpu.VMEM((SB_BQ, 128), F32), pltpu.VMEM((2, SB_BQ, 128), F32),
                        pltpu.SemaphoreType.DMA((6,)), pltpu.SemaphoreType.DMA((6,))],
        compiler_params=_cparams(("arbitrary", "arbitrary")),
    )(qkv, qkv, qkv, shard_pack.reshape(2, r_pack // 2, w_pack))


def _sb_bwd(qkv, do, a_hbm, chip_sums):
    s = qkv.shape[0]
    nq, nkb = s // SB_BQ, s // SB_BK
    scale = 1.0 / math.sqrt(HEAD_DIM)
    zscale = LOG2E * scale

    def body(q_ref, k_ref, v_ref, do_ref, a_row, sums_ref, dq_ref, dk_ref, dv_ref, got_ref,
             zs_scr, da_scr, dz_scr, a_scr, cg_scr, send_sems, recv_sems):
        i = pl.program_id(1)
        pair = pl.program_id(0)
        first_step = (pair == 0) & (i == 0)
        last_step = (pair == SB_PAIRS - 1) & (i == nq - 1)

        @pl.when(first_step)
        def _():
            _exchange_start(sums_ref, got_ref, send_sems, recv_sems)

        @pl.when(i == 0)
        def _():
            dk_ref[...] = jnp.zeros_like(dk_ref)
            dv_ref[...] = jnp.zeros_like(dv_ref)

        causal, rr, cc, first, _ = _sb_consts(nkb)
        earlier = (rr < cc).astype(BF16)
        q2 = q_ref[...]
        qh = _split_heads(q2)
        do2 = do_ref[...].astype(BF16)
        doh = _split_heads(do2)

        def rows(j):
            return pl.ds(pl.multiple_of(j * SB_BK, SB_BK), SB_BK)

        def products_to(slot, j):
            kb, vb = k_ref[rows(j), :], v_ref[rows(j), :]
            for hh in range(2):
                zs_scr[slot, hh] = _dot_nt(qh[hh], kb) * zscale
                da_scr[slot, hh] = _dot_nt(doh[hh], vb)

        head0_rows = lax.broadcasted_iota(jnp.int32, (128, SB_BK), 0) < HEAD_DIM

        def by_head(t):
            return jnp.where(head0_rows, t[:, :SB_BK], t[:, SB_BK:])

        def apply(slot, j):
            k0, k1 = _split_heads(k_ref[rows(j), :])
            dq_ref[...] += _dot_nn(dz_scr[slot], jnp.concatenate([k0, k1], axis=0)) * scale
            dk_ref[0, j] += by_head(_dot_tn(q2, dz_scr[slot])) * scale
            dv_ref[0, j] += by_head(_dot_tn(do2, a_scr[slot]))

        def grads(slot, j, masked):
            gs, gpres = [], []
            for hh in range(2):
                a16 = a_row[0, 0, j, :, hh * SB_BK:(hh + 1) * SB_BK]
                a_scr[slot, :, hh * SB_BK:(hh + 1) * SB_BK] = a16
                g = a16.astype(F32) * da_scr[slot, hh]
                gs.append(g)
                gpres.append(_dot_f32_by_01(g, earlier, 1))
            sigs = []
            for hh in range(2):
                zs = zs_scr[slot, hh]
                sigs.append(jnp.exp2(zs - _sb_softplus2(zs)))
            for hh in range(2):
                cg = cg_scr[hh]
                dz = gs[hh] - (gs[hh] + (gpres[hh] + jnp.concatenate([cg, cg], axis=1))) * sigs[hh]
                if masked:
                    dz = jnp.where(causal, dz, 0.0)
                dz_scr[slot, :, hh * SB_BK:(hh + 1) * SB_BK] = dz.astype(BF16)
                cg_scr[hh] = cg + jnp.sum(gs[hh], axis=1, keepdims=True)

        dq_ref[...] = jnp.zeros_like(dq_ref)
        cg_scr[...] = jnp.zeros_like(cg_scr)
        dz_scr[1] = jnp.zeros((SB_BQ, 2 * SB_BK), BF16)
        a_scr[1] = jnp.zeros((SB_BQ, 2 * SB_BK), BF16)
        products_to(0, 0)

        def step(j, cur, nxt):
            products_to(nxt, j + 1)
            apply(nxt, jnp.maximum(j - 1, 0))
            grads(cur, j, False)

        def two_steps(u, _):
            step(2 * u, 0, 1)
            step(2 * u + 1, 1, 0)
            return 0

        lax.fori_loop(0, i // 2, two_steps, 0)

        def last(cur, nxt):
            apply(nxt, jnp.maximum(i - 1, 0))
            grads(cur, i, True)
            apply(cur, i)

        @pl.when(i % 2 == 1)
        def _():
            step(i - 1, 0, 1)
            last(1, 0)

        @pl.when(i % 2 == 0)
        def _():
            last(0, 1)

        @pl.when(last_step)
        def _():
            _exchange_wait(sums_ref, got_ref, send_sems, recv_sems)

    def full(which):
        return pl.BlockSpec((s, 128), lambda p, i: (0, SB_COL0 + 4 * which + p))

    qblk = pl.BlockSpec((SB_BQ, 128), lambda p, i: (i, p))
    acc = pl.BlockSpec((1, nkb, 128, SB_BK), lambda p, i: (p, 0, 0, 0))
    acc_shape = jax.ShapeDtypeStruct((SB_PAIRS, nkb, 128, SB_BK), F32)
    dq, dk_t, dv_t, got = pl.pallas_call(
        body,
        name="sb_bwd",
        grid=(SB_PAIRS, nq),
        in_specs=[pl.BlockSpec((SB_BQ, 128), lambda p, i: (i, SB_COL0 + p)), full(1), full(2), qblk,
                  pl.BlockSpec((1, 1, nkb, SB_BQ, 2 * SB_BK), lambda p, i: (p, i, 0, 0, 0)), ANY],
        out_specs=[qblk, acc, acc, ANY],
        out_shape=[jax.ShapeDtypeStruct((s, SB_W), F32), acc_shape, acc_shape,
                   jax.ShapeDtypeStruct(chip_sums.shape, chip_sums.dtype)],
        scratch_shapes=[pltpu.VMEM((2, 2, SB_BQ, SB_BK), F32), pltpu.VMEM((2, 2, SB_BQ, SB_BK), F32),
                        pltpu.VMEM((2, SB_BQ, 2 * SB_BK), BF16), pltpu.VMEM((2, SB_BQ, 2 * SB_BK), BF16),
                        pltpu.VMEM((2, SB_BQ, 128), F32),
                        pltpu.SemaphoreType.DMA((3,)), pltpu.SemaphoreType.DMA((3,))],
        compiler_params=_cparams(("arbitrary", "arbitrary")),
    )(qkv, qkv, qkv, do, a_hbm, chip_sums)

    def untranspose(t):
        return jnp.transpose(t, (1, 3, 0, 2)).reshape(s, SB_W)

    return dq, untranspose(dk_t), untranspose(dv_t), got


MERGE_TILE = 256


def _group_mix(lses):
    mx = jnp.maximum(jnp.maximum(lses[0], lses[1]), lses[2])
    es = [jnp.exp(t - mx) for t in lses]
    den = es[0] + es[1] + es[2]
    return [e / den for e in es]


def _merge_fwd(o_groups, lse_groups, o_sb, gl, b_gate, w_up_dil, w_up_sb):
    s = gl.shape[0]
    t = MERGE_TILE

    def body(o0, o1, o2, l0, l1, l2, ob_ref, gl_ref, bg_ref, wd_ref, ws_ref, merged_ref, oa_ref):
        rows = slice(None)
        w = _group_mix([_load_halves(l, rows) for l in (l0, l1, l2)])
        og = [_load_halves(o, rows) for o in (o0, o1, o2)]
        oa = (w[0] * og[0] + w[1] * og[1] + w[2] * og[2]).astype(BF16)
        ua = _dot_nn(oa, wd_ref[...])
        ub = _dot_nn(ob_ref[...].astype(BF16), ws_ref[...])
        gate = jax.nn.sigmoid(gl_ref[...] + bg_ref[...])
        merged_ref[...] = (gate[:, :D_MODEL] * ua + gate[:, D_MODEL:] * ub).astype(BF16)
        oa_ref[...] = oa

    dil = pl.BlockSpec((t, DIL_W), lambda i: (i, 0))
    halves = pl.BlockSpec((2, t, 128), lambda i: (0, i, 0))
    const = lambda shape: pl.BlockSpec(shape, lambda i: (0, 0))
    return pl.pallas_call(
        body,
        name="merge_fwd",
        grid=(s // t,),
        in_specs=[halves] * 6 + [pl.BlockSpec((t, SB_W), lambda i: (i, 0)), pl.BlockSpec((t, GATE_W), lambda i: (i, 0)),
                                 const((1, GATE_W)), const((DIL_W, D_MODEL)), const((SB_W, D_MODEL))],
        out_specs=[pl.BlockSpec((t, D_MODEL), lambda i: (i, 0)), dil],
        out_shape=[jax.ShapeDtypeStruct((s, D_MODEL), BF16), jax.ShapeDtypeStruct((s, DIL_W), BF16)],
        compiler_params=_cparams(("parallel",)),
    )(*o_groups, *lse_groups, o_sb, gl, b_gate, w_up_dil, w_up_sb)


def _merge_bwd(dmerged, o_groups, lse_groups, o_sb, gl, b_gate, w_up_dil, w_up_sb, swap):
    s = gl.shape[0]
    t = MERGE_TILE
    n_chunks, r_swap, w_swap = swap.shape
    swap = swap.reshape(n_chunks, 2, r_swap // 2, w_swap)

    def body(dm_ref, o0, o1, o2, l0, l1, l2, ob_ref, gl_ref, bg_ref, wd_ref, ws_ref, swap_ref,
             dua_ref, dub_ref, dgl_ref, dbg_ref, dosb_ref, d0, d1, d2, c0, c1, c2, got_ref, send_sem, recv_sem):
        i = pl.program_id(0)

        @pl.when(i == 0)
        def _():
            _swap_copy(swap_ref, got_ref, send_sem, recv_sem).start()

        @pl.when(i == pl.num_programs(0) - 1)
        def _():
            _swap_copy(swap_ref, got_ref, send_sem, recv_sem).wait()

        rows = slice(None)
        og = [_load_halves(o, rows) for o in (o0, o1, o2)]
        w = _group_mix([_load_halves(l, rows) for l in (l0, l1, l2)])
        oa = (w[0] * og[0] + w[1] * og[1] + w[2] * og[2]).astype(BF16)
        ua = _dot_nn(oa, wd_ref[...])
        ub = _dot_nn(ob_ref[...].astype(BF16), ws_ref[...])
        gate = jax.nn.sigmoid(gl_ref[...] + bg_ref[...])
        ga, gb = gate[:, :D_MODEL], gate[:, D_MODEL:]
        dm = dm_ref[...]
        dua = (dm * ga).astype(BF16)
        dub = (dm * gb).astype(BF16)
        dua_ref[...] = dua
        dub_ref[...] = dub
        dgl_a = dm * ua * ga * (1.0 - ga)
        dgl_b = dm * ub * gb * (1.0 - gb)
        dgl_ref[:, :D_MODEL] = dgl_a.astype(BF16)
        dgl_ref[:, D_MODEL:] = dgl_b.astype(BF16)
        part = jnp.concatenate([jnp.sum(dgl_a.reshape(t // 8, 8, D_MODEL), axis=0),
                                jnp.sum(dgl_b.reshape(t // 8, 8, D_MODEL), axis=0)], axis=1)

        @pl.when(i == 0)
        def _():
            dbg_ref[...] = part

        @pl.when(i > 0)
        def _():
            dbg_ref[...] += part

        dosb_ref[...] = _dot_nt(dub, ws_ref[...])
        doa = _dot_nt(dua, wd_ref[...])
        rr = lax.broadcasted_iota(jnp.int32, (DIL_W, DIL_W), 0) // HEAD_DIM
        cc = lax.broadcasted_iota(jnp.int32, (DIL_W, DIL_W), 1) // HEAD_DIM
        same_head = (rr == cc).astype(BF16)
        dw = [_dot_f32_by_01(doa * og[g], same_head) for g in range(3)]
        mean_dw = w[0] * dw[0] + w[1] * dw[1] + w[2] * dw[2]
        for g, (d_ref, c_ref) in enumerate(((d0, c0), (d1, c1), (d2, c2))):
            _store_halves(d_ref, rows, w[g] * doa)
            _store_halves(c_ref, rows, -w[g] * mean_dw)

    dil = pl.BlockSpec((2, t, 128), lambda i: (0, i, 0))
    wide = pl.BlockSpec((t, D_MODEL), lambda i: (i, 0))
    gate2 = pl.BlockSpec((t, GATE_W), lambda i: (i, 0))
    sbw = pl.BlockSpec((t, SB_W), lambda i: (i, 0))
    const = lambda shape: pl.BlockSpec(shape, lambda i: (0, 0))
    return pl.pallas_call(
        body,
        name="merge_bwd",
        grid=(s // t,),
        in_specs=[wide] + [dil] * 6 + [sbw, gate2, const((1, GATE_W)), const((DIL_W, D_MODEL)), const((SB_W, D_MODEL)),
                                       ANY],
        out_specs=[wide, wide, gate2, const((8, GATE_W)), sbw] + [dil] * 6 + [ANY],
        out_shape=[jax.ShapeDtypeStruct((s, D_MODEL), BF16), jax.ShapeDtypeStruct((s, D_MODEL), BF16),
                   jax.ShapeDtypeStruct((s, GATE_W), BF16), jax.ShapeDtypeStruct((8, GATE_W), F32),
                   jax.ShapeDtypeStruct((s, SB_W), F32)] + [jax.ShapeDtypeStruct((2, s, 128), F32)] * 6
        + [jax.ShapeDtypeStruct((n_chunks, r_swap // 2, w_swap), swap.dtype)],
        scratch_shapes=[pltpu.SemaphoreType.DMA, pltpu.SemaphoreType.DMA],
        compiler_params=_cparams(("arbitrary",)),
    )(dmerged, *o_groups, *lse_groups, o_sb, gl, b_gate, w_up_dil, w_up_sb, swap)


ANY = pl.BlockSpec(memory_space=pl.ANY)


def _place():
    x, y, c = lax.axis_index("x"), lax.axis_index("y"), lax.axis_index("c")
    other_chips = [(1 - x, y), (x, 1 - y), (1 - x, 1 - y)]
    return x, y, c, other_chips


def _gather_copies(p_ref, out_ref, send_sems, recv_sems):
    x, y, c, chips = _place()
    me, sibling = 2 * x + y, (x, y, 1 - c)
    idx = [2 * chip[0] + chip[1] for chip in chips]

    def copy(k, chip_idx, core, to, src=None):
        return pltpu.make_async_remote_copy(
            src_ref=out_ref.at[chip_idx, core] if src is None else src, dst_ref=out_ref.at[chip_idx, core],
            send_sem=send_sems.at[k], recv_sem=recv_sems.at[k], device_id=to, device_id_type=MESH)

    first = lambda j: copy(j, me, c, (*chips[j], c), src=p_ref.at[c])
    landed = lambda j: copy(j, idx[j], c, (x, y, c))
    passed = lambda j: copy(3 + j, idx[j], c, sibling)
    handed = lambda j: copy(3 + j, idx[j], 1 - c, (x, y, c))
    return first, landed, passed, handed


def _gather_start(*refs):
    first = _gather_copies(*refs)[0]
    for j in range(3):
        first(j).start()


def _gather_pass_on(*refs):
    _, landed, passed, _ = _gather_copies(*refs)
    for j in range(3):
        landed(j).wait_recv()
        passed(j).start()


def _gather_finish(*refs):
    first, _, passed, handed = _gather_copies(*refs)
    for j in range(3):
        handed(j).wait_recv()
    for j in range(3):
        first(j).wait_send()
        passed(j).wait_send()


def _fill_own_slot(others, pack):
    n, _, rh, wd = others.shape
    me = 2 * lax.axis_index("x") + lax.axis_index("y")
    mine = lax.broadcasted_iota(jnp.int32, (n, 1, 1, 1), 0) == me
    return jnp.where(mine, pack.reshape(1, 2, rh, wd), others).reshape(n, 2 * rh, wd)


def _swap_copy(g_ref, out_ref, send_sem, recv_sem):
    x, y, c, _ = _place()
    return pltpu.make_async_remote_copy(
        src_ref=g_ref.at[:, 1 - c], dst_ref=out_ref,
        send_sem=send_sem, recv_sem=recv_sem, device_id=(x, y, 1 - c), device_id_type=MESH)


def _swap_halves(g):
    n, r, wd = g.shape
    rh = r // 2
    g = g.reshape(n, 2, rh, wd)

    def body(g_ref, out_ref, send_sem, recv_sem):
        cp = _swap_copy(g_ref, out_ref, send_sem, recv_sem)
        cp.start()
        cp.wait()

    return pl.pallas_call(
        body,
        name="grad_swap_halves",
        in_specs=[ANY],
        out_specs=ANY,
        out_shape=jax.ShapeDtypeStruct((n, rh, wd), g.dtype),
        scratch_shapes=[pltpu.SemaphoreType.DMA, pltpu.SemaphoreType.DMA],
    )(g)


def _add_halves(g, got, core):
    n, r, wd = g.shape
    rh = r // 2
    t = rh // 4
    nt = rh // t

    def body(c_ref, a_ref, b_ref, o_ref):
        o_ref[...] = (a_ref[0] + b_ref[...]).astype(BF16)

    grid_spec = pltpu.PrefetchScalarGridSpec(
        num_scalar_prefetch=1,
        grid=(n, nt),
        in_specs=[pl.BlockSpec((1, 1, t, wd), lambda s, i, c: (s, c[0], i, 0)),
                  pl.BlockSpec((1, t, wd), lambda s, i, c: (s, i, 0))],
        out_specs=pl.BlockSpec((1, t, wd), lambda s, i, c: (s, i, 0)),
    )
    return pl.pallas_call(
        body,
        name="grad_add_halves",
        grid_spec=grid_spec,
        out_shape=jax.ShapeDtypeStruct((n, rh, wd), BF16),
        compiler_params=_cparams(("parallel", "parallel")),
    )(core, g.reshape(n, 2, rh, wd), got)


def _exchange_copies(h_ref, out_ref, send_sems, recv_sems):
    x, y, c, chips = _place()
    me = 2 * x + y

    def copy(j, slot):
        them = 2 * chips[j][0] + chips[j][1]
        return pltpu.make_async_remote_copy(
            src_ref=h_ref.at[them], dst_ref=out_ref.at[me if slot == "mine" else them],
            send_sem=send_sems.at[j], recv_sem=recv_sems.at[j], device_id=(*chips[j], c), device_id_type=MESH)

    return (lambda j: copy(j, "mine")), (lambda j: copy(j, "theirs"))


def _exchange_start(h_ref, out_ref, send_sems, recv_sems):
    send = _exchange_copies(h_ref, out_ref, send_sems, recv_sems)[0]
    for j in range(3):
        send(j).start()


def _exchange_wait(h_ref, out_ref, send_sems, recv_sems):
    send, arrival = _exchange_copies(h_ref, out_ref, send_sems, recv_sems)
    for j in range(3):
        arrival(j).wait_recv()
    for j in range(3):
        send(j).wait_send()


def _sum_chips(b, h, chip):
    n, rh, wd = b.shape
    t = rh // 4

    def body(chip_ref, b_ref, own_ref, o_ref):
        own = own_ref[0]
        s0, s1, s2, s3 = (jnp.where(chip_ref[0] == k, own, b_ref[k]).astype(F32) for k in range(n))
        o_ref[...] = ((s0 + s1) + s2) + s3

    grid_spec = pltpu.PrefetchScalarGridSpec(
        num_scalar_prefetch=1,
        grid=(rh // t,),
        in_specs=[pl.BlockSpec((n, t, wd), lambda i, chip: (0, i, 0)),
                  pl.BlockSpec((1, t, wd), lambda i, chip: (chip[0], i, 0))],
        out_specs=pl.BlockSpec((t, wd), lambda i, chip: (i, 0)),
    )
    return pl.pallas_call(
        body,
        name="grad_sum_chips",
        grid_spec=grid_spec,
        out_shape=jax.ShapeDtypeStruct((rh, wd), F32),
        compiler_params=_cparams(("parallel",)),
    )(chip, b, h)


def _join_halves(tc):
    rh, wd = tc.shape

    def body(t_ref, out_ref, send_sem, recv_sem):
        x, y, c, _ = _place()
        cp = pltpu.make_async_remote_copy(
            src_ref=t_ref, dst_ref=out_ref.at[c],
            send_sem=send_sem, recv_sem=recv_sem, device_id=(x, y, 1 - c), device_id_type=MESH)
        cp.start()
        cp.wait()

    halves = pl.pallas_call(
        body,
        name="grad_join_halves",
        in_specs=[ANY],
        out_specs=ANY,
        out_shape=jax.ShapeDtypeStruct((2, rh, wd), tc.dtype),
        scratch_shapes=[pltpu.SemaphoreType.DMA, pltpu.SemaphoreType.DMA],
    )(tc)
    return lax.dynamic_update_slice(halves, tc[None], (lax.axis_index("c"), 0, 0)).reshape(2 * rh, wd)


def _all_reduce_small(pack):
    rows, lanes = pack.shape

    def body(p_ref, out_ref, buf, send_sems, recv_sems):
        x, y, c, _ = _place()
        me = 4 * x + 2 * y + c
        buf[me] = p_ref[...]
        sends = []
        for k in range(1, N_DEV):
            peer = (x ^ (k >> 2), y ^ ((k >> 1) & 1), c ^ (k & 1))
            sends.append(pltpu.make_async_remote_copy(
                src_ref=p_ref, dst_ref=buf.at[me], send_sem=send_sems.at[k - 1], recv_sem=recv_sems.at[k - 1],
                device_id=peer, device_id_type=MESH))
        for cp in sends:
            cp.start()
        for k in range(1, N_DEV):
            pltpu.make_async_remote_copy(
                src_ref=p_ref, dst_ref=buf.at[me ^ k], send_sem=send_sems.at[k - 1], recv_sem=recv_sems.at[k - 1],
                device_id=(x, y, c), device_id_type=MESH).wait_recv()
        for cp in sends:
            cp.wait_send()
        total = buf[0]
        for d in range(1, N_DEV):
            total = total + buf[d]
        out_ref[...] = total

    vm = pl.BlockSpec(memory_space=pltpu.VMEM)
    return pl.pallas_call(
        body,
        name="all_reduce_small",
        in_specs=[vm],
        out_specs=vm,
        out_shape=jax.ShapeDtypeStruct((rows, lanes), F32),
        scratch_shapes=[pltpu.VMEM((N_DEV, rows, lanes), F32), pltpu.SemaphoreType.DMA((N_DEV - 1,)),
                        pltpu.SemaphoreType.DMA((N_DEV - 1,))],
    )(pack)


def _adamw(g, w, m, v, name):
    rows, cols = g.shape
    t = rows
    for cand in (256, 128, 64, 32, 16, 8):
        if rows % cand == 0:
            t = cand
            break

    def body(g_ref, w_ref, m_ref, v_ref, d_ref, nm_ref, nv_ref):
        gv = g_ref[...]
        mv = ADAM_B1 * m_ref[...] + (1.0 - ADAM_B1) * gv
        vv = ADAM_B2 * v_ref[...] + (1.0 - ADAM_B2) * (gv * gv)
        m_hat = mv / (1.0 - ADAM_B1 ** ADAM_STEP)
        v_hat = vv / (1.0 - ADAM_B2 ** ADAM_STEP)
        d_ref[...] = -ADAM_LR * (m_hat / (jnp.sqrt(v_hat) + ADAM_EPS) + ADAM_WD * w_ref[...])
        nm_ref[...] = mv
        nv_ref[...] = vv

    blk = pl.BlockSpec((t, cols), lambda i: (i, 0))
    return pl.pallas_call(
        body,
        name=name,
        grid=(rows // t,),
        in_specs=[blk] * 4,
        out_specs=[blk] * 3,
        out_shape=[jax.ShapeDtypeStruct((rows, cols), F32)] * 3,
        compiler_params=_cparams(("parallel",)),
    )(g, w, m, v)


PACK_W = 1024
BIG = (("w_in", (D_MODEL, IN_COLS), 1), ("w_up_dil", (DIL_W, D_MODEL), 1), ("w_up_sb", (SB_W, D_MODEL), 1),
       ("w_out", (D_MODEL, D_MODEL), 0), ("w_mlp_in", (D_MODEL, D_FF), 1), ("w_mlp_out", (D_FF, D_MODEL), 0))


def _shard_shape(shape, axis):
    return tuple(d // N_CHIPS if a == axis else d for a, d in enumerate(shape))


MIXER_GROUP, MLP_GROUP = BIG[:4], BIG[4:]
EARLY_WEIGHTS, LATE_WEIGHTS = BIG[:1], BIG[1:]


def _pack_rows(group=BIG):
    rows, at = {}, 0
    for name, shape, axis in group:
        n = math.prod(_shard_shape(shape, axis)) // PACK_W
        rows[name] = (at, n)
        at += n
    return rows, at


def _pack_shards(shards, group):
    return jnp.concatenate([shards[name].reshape(-1, PACK_W) for name, _, _ in group], axis=0)


def _unpack_full(gathered, group):
    rows, _ = _pack_rows(group)
    full = {}
    for name, shape, axis in group:
        at, n = rows[name]
        parts = gathered[:, at:at + n, :].reshape((N_CHIPS,) + _shard_shape(shape, axis))
        if axis == 0:
            full[name] = parts.reshape(shape)
        else:
            full[name] = jnp.transpose(parts, (1, 0, 2)).reshape(shape)
    return full


def _pack_full_grads(grads, group):
    chunks = []
    for name, shape, axis in group:
        g = grads[name]
        if axis == 0:
            parts = g.reshape((N_CHIPS, shape[0] // N_CHIPS, shape[1]))
        else:
            parts = jnp.transpose(g.reshape((shape[0], N_CHIPS, shape[1] // N_CHIPS)), (1, 0, 2))
        chunks.append(parts.reshape(N_CHIPS, -1, PACK_W))
    return jnp.concatenate(chunks, axis=1)


def _unpack_shard(packed, group):
    rows, _ = _pack_rows(group)
    return {name: packed[rows[name][0]:rows[name][0] + rows[name][1]].reshape(_shard_shape(shape, axis))
            for name, shape, axis in group}


def _local_step(x, target, early_shards, late_shards, norm_mix_g, b_gate, norm_mlp_g, norm_final_g, core):
    h, early = _rms_fwd_and_gather(x, norm_mix_g, early_shards)
    w = _unpack_full(early, EARLY_WEIGHTS)
    w_in = w["w_in"]
    sb0 = 9 * DIL_W
    w_sb, w_gate = w_in[:, sb0:QKV_W], w_in[:, QKV_W:]
    w_dil = [jnp.concatenate([w_in[:, (3 * i + g) * DIL_W:(3 * i + g + 1) * DIL_W] for i in range(3)], axis=1)
             for g in range(3)]

    qkv_dil = [_matmul(h, w_dil[g], mode="nn", out_dtypes=(BF16,), name=f"proj_dil_g{g}", tn=768)[0] for g in range(3)]
    (qkv_sb,) = _matmul(h, w_sb, mode="nn", out_dtypes=(BF16,), name="proj_sb", tn=768)
    (gl,) = _matmul(h, w_gate, mode="nn", out_dtypes=(F32,), name="proj_gate")
    dil = [_dil_fwd(qkv_dil[g], g) for g in range(3)]
    o_groups, lse_groups = [d[0] for d in dil], [d[1] for d in dil]
    o_sb, a_sb, late_others = _sb_fwd(qkv_sb, late_shards)
    w = {**w, **_unpack_full(_fill_own_slot(late_others, late_shards), LATE_WEIGHTS)}
    merged, o_a = _merge_fwd(o_groups, lse_groups, o_sb, gl, b_gate, w["w_up_dil"], w["w_up_sb"])
    def residual_and_norm(acc, res, g):
        x1 = res + acc
        return x1, _rms_rows(x1)[0] * g

    x1, h2 = _matmul(merged, w["w_out"], mode="nn", out_dtypes=(F32, BF16), name="out_proj", tm=ROW_TILE,
                     extras=(x, norm_mlp_g), epilogue=residual_and_norm)
    u, act = _matmul(h2, w["w_mlp_in"], mode="nn", out_dtypes=(BF16, BF16), name="mlp_in",
                     epilogue=lambda acc: (acc, jnp.square(jnp.maximum(acc, 0.0))))

    def residual_and_loss(acc, res, tgt, g):
        xh, r = _rms_rows(res + acc)
        err = xh * g - tgt
        dy = err * (1.0 / D_MODEL)
        dxh = dy * g
        dx2 = r * (dxh - xh * jnp.mean(dxh * xh, axis=-1, keepdims=True))
        return dx2, _rows_sum8(dy * xh), (0.5 / D_MODEL) * _rows_sum8(err * err)

    dx2, dg_final, loss_part = _matmul(
        act, w["w_mlp_out"], mode="nn", out_dtypes=(F32, ("part", F32), ("part", F32)), name="mlp_out", tm=ROW_TILE,
        tk=2048, extras=(x1, target, norm_final_g.reshape(1, D_MODEL)), epilogue=residual_and_loss)

    (du,) = _matmul(dx2, w["w_mlp_out"], mode="nt", out_dtypes=(BF16,), name="mlp_out_dx",
                    extras=(u,), epilogue=lambda acc, uu: (acc * (2.0 * jnp.maximum(uu.astype(F32), 0.0)),))
    (g_mlp_out,) = _matmul(act, dx2, mode="tn", out_dtypes=(F32,), name="mlp_out_dw")
    (g_mlp_in,) = _matmul(h2, du, mode="tn", out_dtypes=(F32,), name="mlp_in_dw")

    def norm_bwd(acc, xx, dres, g):
        dx, dg = _rms_bwd_rows(acc, xx, g)
        return dres + dx, dg

    dx1, dg_mlp = _matmul(du, w["w_mlp_in"], mode="nt", out_dtypes=(F32, ("part", F32)), name="mlp_in_dx",
                          tm=ROW_TILE, tk=2048, extras=(x1, dx2, norm_mlp_g), epilogue=norm_bwd)

    mlp_pack = _pack_full_grads({"w_mlp_in": g_mlp_in, "w_mlp_out": g_mlp_out}, MLP_GROUP)
    (dmerged,) = _matmul(dx1, w["w_out"], mode="nt", out_dtypes=(F32,), name="out_proj_dx")
    (g_out,) = _matmul(merged, dx1, mode="tn", out_dtypes=(F32,), name="out_proj_dw")
    mb = _merge_bwd(dmerged, o_groups, lse_groups, o_sb, gl, b_gate, w["w_up_dil"], w["w_up_sb"], mlp_pack)
    dua, dub, dgl, dbg, do_sb = mb[:5]
    do_groups, c_groups = mb[5:8], mb[8:11]
    mlp_sums = _add_halves(mlp_pack, mb[11], core)
    (g_up_dil,) = _matmul(o_a, dua, mode="tn", out_dtypes=(F32,), name="up_dil_dw")
    (g_up_sb,) = _matmul(o_sb, dub, mode="tn", out_dtypes=(F32,), name="up_sb_dw")
    dq_sb, dk_sb, dv_sb, mlp_got = _sb_bwd(qkv_sb, do_sb, a_sb, mlp_sums)
    dil_b = [_dil_bwd(qkv_dil[g], do_groups[g], lse_groups[g], c_groups[g], g) for g in range(3)]
    dproj = jnp.concatenate(
        [dil_b[g][i].astype(BF16) for i in range(3) for g in range(3)]
        + [t.astype(BF16) for t in (dq_sb, dk_sb, dv_sb)] + [dgl], axis=1)
    (g_in,) = _matmul(h, dproj, mode="tn", out_dtypes=(F32,), name="proj_dw", tm=512, tn=IN_COLS // 2)
    mixer_pack = _pack_full_grads({"w_in": g_in, "w_up_dil": g_up_dil, "w_up_sb": g_up_sb, "w_out": g_out}, MIXER_GROUP)
    mixer_sums = _add_halves(mixer_pack, _swap_halves(mixer_pack), core)
    grad_x, dg_mix, mixer_got = _matmul(
        dproj, w["w_in"], mode="nt", out_dtypes=(F32, ("part", F32)), name="proj_dx", tm=ROW_TILE, tk=IN_COLS // 2,
        extras=(x, dx1, norm_mix_g), epilogue=norm_bwd, exchange=mixer_sums)

    small = (dg_mix, dbg, dg_mlp, dg_final, loss_part)
    return grad_x, (mixer_got, mixer_sums), (mlp_got, mlp_sums), small


def kernel(x, norm_mix_g, w_in, b_gate, w_up_dil, w_up_sb, w_out, norm_mlp_g, w_mlp_in, w_mlp_out, norm_final_g, loss_target, m_norm_mix_g, m_w_in, m_b_gate, m_w_up_dil, m_w_up_sb, m_w_out, m_norm_mlp_g, m_w_mlp_in, m_w_mlp_out, m_norm_final_g, v_norm_mix_g, v_w_in, v_b_gate, v_w_up_dil, v_w_up_sb, v_w_out, v_norm_mlp_g, v_w_mlp_in, v_w_mlp_out, v_norm_final_g):
    shards = {"w_in": w_in[0], "w_up_dil": w_up_dil[0], "w_up_sb": w_up_sb[0], "w_out": w_out[0],
              "w_mlp_in": w_mlp_in[0], "w_mlp_out": w_mlp_out[0]}
    moments_m = {"w_in": m_w_in[0], "w_up_dil": m_w_up_dil[0], "w_up_sb": m_w_up_sb[0], "w_out": m_w_out[0],
                 "w_mlp_in": m_w_mlp_in[0], "w_mlp_out": m_w_mlp_out[0]}
    moments_v = {"w_in": v_w_in[0], "w_up_dil": v_w_up_dil[0], "w_up_sb": v_w_up_sb[0], "w_out": v_w_out[0],
                 "w_mlp_in": v_w_mlp_in[0], "w_mlp_out": v_w_mlp_out[0]}

    shards16 = {n: s.astype(BF16) for n, s in shards.items()}
    early_shards = _pack_shards(shards16, EARLY_WEIGHTS)
    late_shards = _pack_shards(shards16, LATE_WEIGHTS)

    core = lax.axis_index("c").astype(jnp.int32).reshape(1)
    chip = (2 * lax.axis_index("x") + lax.axis_index("y")).astype(jnp.int32).reshape(1)
    grad_x, (mixer_got, mixer_sums), (mlp_got, mlp_sums), small = _local_step(
        x[0], loss_target[0], early_shards, late_shards, norm_mix_g, b_gate, norm_mlp_g, norm_final_g, core)

    reduced = _join_halves(_sum_chips(mixer_got, mixer_sums, chip))
    reduced_mlp = _join_halves(_sum_chips(mlp_got, mlp_sums, chip))
    g_shard = {**_unpack_shard(reduced, MIXER_GROUP), **_unpack_shard(reduced_mlp, MLP_GROUP)}

    dg_mix, dbg, dg_mlp, dg_final, loss_part = small
    loss_row = jnp.sum(loss_part, axis=0, keepdims=True)
    small_pack = jnp.concatenate(
        [jnp.sum(dg_mix, axis=0, keepdims=True), jnp.sum(dbg, axis=0, keepdims=True),
         jnp.sum(dg_mlp, axis=0, keepdims=True), jnp.sum(dg_final, axis=0, keepdims=True), loss_row], axis=1)
    n_small = small_pack.shape[1]
    small_sum = _all_reduce_small(small_pack.reshape(n_small // 128, 128)).reshape(1, n_small)
    g_norm_mix = small_sum[:, :D_MODEL]
    g_b_gate = small_sum[:, D_MODEL:3 * D_MODEL]
    g_norm_mlp = small_sum[:, 3 * D_MODEL:4 * D_MODEL]
    g_norm_final = small_sum[:, 4 * D_MODEL:5 * D_MODEL]
    loss = jnp.sum(small_sum[:, 5 * D_MODEL:])

    names = ["norm_mix_g", "w_in", "b_gate", "w_up_dil", "w_up_sb", "w_out", "norm_mlp_g", "w_mlp_in", "w_mlp_out",
             "norm_final_g"]
    grads = dict(g_shard)
    grads.update(norm_mix_g=g_norm_mix, b_gate=g_b_gate, norm_mlp_g=g_norm_mlp, norm_final_g=g_norm_final)
    weights = dict(shards)
    weights.update(norm_mix_g=norm_mix_g, b_gate=b_gate, norm_mlp_g=norm_mlp_g, norm_final_g=norm_final_g.reshape(1, D_MODEL))
    ms = dict(moments_m)
    ms.update(norm_mix_g=m_norm_mix_g, b_gate=m_b_gate, norm_mlp_g=m_norm_mlp_g, norm_final_g=m_norm_final_g.reshape(1, D_MODEL))
    vs = dict(moments_v)
    vs.update(norm_mix_g=v_norm_mix_g, b_gate=v_b_gate, norm_mlp_g=v_norm_mlp_g, norm_final_g=v_norm_final_g.reshape(1, D_MODEL))

    out_shapes = {"norm_mix_g": norm_mix_g.shape, "w_in": w_in.shape, "b_gate": b_gate.shape, "w_up_dil": w_up_dil.shape,
                  "w_up_sb": w_up_sb.shape, "w_out": w_out.shape, "norm_mlp_g": norm_mlp_g.shape,
                  "w_mlp_in": w_mlp_in.shape, "w_mlp_out": w_mlp_out.shape, "norm_final_g": norm_final_g.shape}
    g_out, d_out, m_out, v_out = [], [], [], []
    for n in names:
        d, nm, nv = _adamw(grads[n], weights[n], ms[n], vs[n], "adamw_" + n)
        shape = out_shapes[n]
        g_out.append(grads[n].reshape(shape))
        d_out.append(d.reshape(shape))
        m_out.append(nm.reshape(shape))
        v_out.append(nv.reshape(shape))
    return (loss, grad_x.reshape(x.shape), *g_out, *d_out, *m_out, *v_out)
```

```python
import math

import jax
import jax.numpy as jnp
import numpy as np
from jax import lax
from jax.experimental import pallas as pl
from jax.experimental.pallas import tpu as pltpu

F32 = jnp.float32
BF16 = jnp.bfloat16
MESH = pl.DeviceIdType.MESH

D_MODEL = 1024
HEAD_DIM = 64
DIL_GROUPS = ((128, 1), (512, 4), (2048, 16))
DIL_HEADS = 4
DIL_W = 256
N_DIL_HEADS = 12
SB_HEADS = 8
SB_W = SB_HEADS * HEAD_DIM
QKV_W = 3 * 3 * DIL_W + 3 * SB_W
GATE_W = 2 * D_MODEL
IN_COLS = QKV_W + GATE_W
D_FF = 4 * D_MODEL
BLOCK = 128
RMS_EPS = 1e-6
NEG_INF = -1e30
N_CHIPS = 4
N_DEV = 8

ADAM_LR = 0.001
ADAM_B1 = 0.9
ADAM_B2 = 0.999
ADAM_EPS = 1e-08
ADAM_WD = 0.01
ADAM_STEP = 10

VMEM_LIMIT = 56 * 1024 * 1024

SB_BQ = 256
SB_BK = 256


def _cparams(sem=None):
    if sem is None:
        return pltpu.CompilerParams(vmem_limit_bytes=VMEM_LIMIT)
    return pltpu.CompilerParams(dimension_semantics=sem, vmem_limit_bytes=VMEM_LIMIT)


def _dot(a, b, dims):
    return lax.dot_general(a, b, (dims, ((), ())), preferred_element_type=F32)


def _dot_nn(a, b):
    return _dot(a, b, ((1,), (0,)))


def _dot_nt(a, b):
    return _dot(a, b, ((1,), (1,)))


def _dot_tn(a, b):
    return _dot(a, b, ((0,), (0,)))


def _dot_f32_by_01(x, m01, pieces=3):
    hi = x.astype(BF16)
    if pieces == 1:
        return _dot_nn(hi, m01)
    r1 = x - hi.astype(F32)
    mid = r1.astype(BF16)
    if pieces == 2:
        return _dot_nn(hi, m01) + _dot_nn(mid, m01)
    lo = (r1 - mid.astype(F32)).astype(BF16)
    return _dot_nn(hi, m01) + _dot_nn(mid, m01) + _dot_nn(lo, m01)


def _matmul(a, b, *, mode, out_dtypes, name, tm=1024, tn=1024, tk=1024, extras=(), epilogue=None, exchange=None):
    if mode == "nn":
        (m, k), (k2, n) = a.shape, b.shape
    elif mode == "nt":
        (m, k), (n, k2) = a.shape, b.shape
    else:
        (k, m), (k2, n) = a.shape, b.shape
    assert k == k2, (a.shape, b.shape, mode)
    tm, tn, tk = min(tm, m), min(tn, n), min(tk, k)
    assert m % tm == 0 and n % tn == 0 and k % tk == 0, (m, n, k, tm, tn, tk)
    nk = k // tk
    n_out = len(out_dtypes)
    n_ex = len(extras)

    if mode == "nn":
        a_spec = pl.BlockSpec((tm, tk), lambda i, j, kk: (i, kk))
        b_spec = pl.BlockSpec((tk, tn), lambda i, j, kk: (kk, j))
        dot = _dot_nn
    elif mode == "nt":
        a_spec = pl.BlockSpec((tm, tk), lambda i, j, kk: (i, kk))
        b_spec = pl.BlockSpec((tn, tk), lambda i, j, kk: (j, kk))
        dot = _dot_nt
    else:
        a_spec = pl.BlockSpec((tk, tm), lambda i, j, kk: (kk, i))
        b_spec = pl.BlockSpec((tk, tn), lambda i, j, kk: (kk, j))
        dot = _dot_tn
    mn_spec = pl.BlockSpec((tm, tn), lambda i, j, kk: (i, j))
    row_spec = pl.BlockSpec((1, tn), lambda i, j, kk: (0, j))
    part_spec = pl.BlockSpec((8, tn), lambda i, j, kk: (i, j))
    ex_specs = [row_spec if e.shape[0] == 1 else mn_spec for e in extras]
    is_part = [isinstance(dt, tuple) for dt in out_dtypes]
    out_dts = [dt[1] if p else dt for dt, p in zip(out_dtypes, is_part)]
    out_specs = [part_spec if p else mn_spec for p in is_part]
    out_shapes = [jax.ShapeDtypeStruct((8 * (m // tm), n) if p else (m, n), dt) for dt, p in zip(out_dts, is_part)]

    n_side = 0 if exchange is None else 1
    grid = (m // tm, n // tn, nk)

    def body(*refs):
        a_ref, b_ref = refs[0], refs[1]
        ex_refs = refs[2:2 + n_ex]
        out_refs = refs[2 + n_ex + n_side:2 + n_ex + n_side + n_out]
        scratch = refs[2 + n_ex + n_side + n_out + n_side:]
        acc_ref = scratch[0] if nk > 1 else None
        if exchange is not None:
            side = (refs[2 + n_ex], refs[2 + n_ex + n_side + n_out]) + tuple(scratch[-2:])
            step = (pl.program_id(0) * grid[1] + pl.program_id(1)) * grid[2] + pl.program_id(2)

            @pl.when(step == 0)
            def _():
                _exchange_start(*side)

            @pl.when(step == grid[0] * grid[1] * grid[2] - 1)
            def _():
                _exchange_wait(*side)

        part = dot(a_ref[...].astype(BF16), b_ref[...].astype(BF16))

        def finish(acc):
            if epilogue is None:
                outs = (acc,)
            else:
                outs = epilogue(acc, *[r[...] for r in ex_refs])
            for o_ref, o in zip(out_refs, outs):
                o_ref[...] = o.astype(o_ref.dtype)

        if nk == 1:
            finish(part)
        else:
            kk = pl.program_id(2)

            @pl.when(kk == 0)
            def _():
                acc_ref[...] = part

            @pl.when(kk > 0)
            def _():
                acc_ref[...] += part

            @pl.when(kk == nk - 1)
            def _():
                finish(acc_ref[...])

    side_in = [] if exchange is None else [exchange]
    outs = pl.pallas_call(
        body,
        name=name,
        grid=grid,
        in_specs=[a_spec, b_spec] + ex_specs + [ANY] * n_side,
        out_specs=out_specs + [ANY] * n_side,
        out_shape=out_shapes + [jax.ShapeDtypeStruct(e.shape, e.dtype) for e in side_in],
        scratch_shapes=([pltpu.VMEM((tm, tn), F32)] if nk > 1 else [])
        + [pltpu.SemaphoreType.DMA((3,)), pltpu.SemaphoreType.DMA((3,))] * n_side,
        compiler_params=_cparams(("arbitrary",) * 3 if n_side else ("parallel", "parallel", "arbitrary")),
    )(a, b, *extras, *side_in)
    return outs


ROW_TILE = 512


def _rows_sum8(t):
    rows, d = t.shape
    return jnp.sum(t.reshape(rows // 8, 8, d), axis=0)


def _rms_rows(x):
    r = lax.rsqrt(jnp.mean(x * x, axis=-1, keepdims=True) + RMS_EPS)
    return x * r, r


def _rms_bwd_rows(dh, x, g):
    xh, r = _rms_rows(x)
    dxh = dh * g
    return r * (dxh - xh * jnp.mean(dxh * xh, axis=-1, keepdims=True)), _rows_sum8(dh * xh)


def _rms_fwd_and_gather(x, g, shard_pack):
    s, d = x.shape
    r_pack, w_pack = shard_pack.shape
    steps = s // ROW_TILE

    def body(x_ref, g_ref, pack_ref, h_ref, others_ref, send_sems, recv_sems):
        i = pl.program_id(0)
        gather = (pack_ref, others_ref, send_sems, recv_sems)

        @pl.when(i == 0)
        def _():
            _gather_start(*gather)

        h_ref[...] = (_rms_rows(x_ref[...])[0] * g_ref[...]).astype(BF16)

        @pl.when(i == steps - 1)
        def _():
            _gather_pass_on(*gather)
            _gather_finish(*gather)

    h, others = pl.pallas_call(
        body,
        name="norm_mix",
        grid=(steps,),
        in_specs=[pl.BlockSpec((ROW_TILE, d), lambda i: (i, 0)), pl.BlockSpec((1, d), lambda i: (0, 0)), ANY],
        out_specs=[pl.BlockSpec((ROW_TILE, d), lambda i: (i, 0)), ANY],
        out_shape=[jax.ShapeDtypeStruct((s, d), BF16),
                   jax.ShapeDtypeStruct((N_CHIPS, 2, r_pack // 2, w_pack), shard_pack.dtype)],
        scratch_shapes=[pltpu.SemaphoreType.DMA((6,)), pltpu.SemaphoreType.DMA((6,))],
        compiler_params=_cparams(("arbitrary",)),
    )(x, g, shard_pack.reshape(2, r_pack // 2, w_pack))
    return h, _fill_own_slot(others, shard_pack)


def _alibi_slopes():
    return np.exp2(np.float32(-8.0) * np.arange(1, N_DIL_HEADS + 1, dtype=np.float32) / np.float32(N_DIL_HEADS))


def _head_lane_mask(h, rows):
    lane = lax.broadcasted_iota(jnp.int32, (rows, DIL_W), 1)
    return (lane >= h * HEAD_DIM) & (lane < (h + 1) * HEAD_DIM)


def _band_terms(dil, has_prev):
    qi = lax.broadcasted_iota(jnp.int32, (BLOCK, 2 * BLOCK), 0)
    kj = lax.broadcasted_iota(jnp.int32, (BLOCK, 2 * BLOCK), 1)
    steps = qi + BLOCK - kj
    valid = (steps >= 0) & (steps <= BLOCK) & ((kj >= BLOCK) | has_prev)
    return valid, steps.astype(F32) * float(dil)


def _load_halves(ref, rows):
    return jnp.concatenate([ref[0, rows, :], ref[1, rows, :]], axis=1)


def _store_halves(ref, rows, value):
    ref[0, rows, :] = value[:, :128]
    ref[1, rows, :] = value[:, 128:]


def _dil_fwd(qkv_g, group):
    _, dil = DIL_GROUPS[group]
    s = qkv_g.shape[0]
    sub = s // dil
    nb = sub // BLOCK
    view = qkv_g.reshape(sub, dil * 3 * DIL_W)
    slopes = _alibi_slopes()[group * DIL_HEADS:(group + 1) * DIL_HEADS]

    def col(which):
        return lambda n, r: (n, r * 3 + which)

    def col_prev(which):
        return lambda n, r: (jnp.maximum(n - 1, 0), r * 3 + which)

    def body(q_ref, kc_ref, kp_ref, vc_ref, vp_ref, o_ref, lse_ref):
        n, r = pl.program_id(0), pl.program_id(1)
        mine = pl.ds(r, BLOCK, stride=dil) if dil > 1 else slice(None)
        valid, dist = _band_terms(dil, n > 0)
        q = q_ref[...]
        k2 = jnp.concatenate([kp_ref[...], kc_ref[...]], axis=0)
        v2 = jnp.concatenate([vp_ref[...], vc_ref[...]], axis=0)
        masks = [_head_lane_mask(h, BLOCK) for h in range(DIL_HEADS)]
        logits = [_dot_nt(jnp.where(masks[h], q, jnp.zeros_like(q)), k2) for h in range(DIL_HEADS)]
        ps, lses = [], []
        for h in range(DIL_HEADS):
            lg = jnp.where(valid, logits[h] * 0.125 - float(slopes[h]) * dist, NEG_INF)
            mx = jnp.max(lg, axis=1, keepdims=True)
            lse = mx + jnp.log(jnp.sum(jnp.exp(lg - mx), axis=1, keepdims=True))
            ps.append(jnp.exp(lg - lse).astype(BF16))
            lses.append(lse)
        o_acc = jnp.zeros((BLOCK, DIL_W), F32)
        lse_acc = jnp.zeros((BLOCK, DIL_W), F32)
        for h in range(DIL_HEADS):
            o_acc = jnp.where(masks[h], _dot_nn(ps[h], v2), o_acc)
            lse_acc = jnp.where(masks[h], lses[h], lse_acc)
        _store_halves(o_ref, mine, o_acc)
        _store_halves(lse_ref, mine, lse_acc)

    blk = (BLOCK, DIL_W)
    return pl.pallas_call(
        body,
        name=f"dil_fwd_g{group}",
        grid=(nb, dil),
        in_specs=[pl.BlockSpec(blk, col(0)), pl.BlockSpec(blk, col(1)), pl.BlockSpec(blk, col_prev(1)),
                  pl.BlockSpec(blk, col(2)), pl.BlockSpec(blk, col_prev(2))],
        out_specs=[pl.BlockSpec((2, BLOCK * dil, 128), lambda n, r: (0, n, 0))] * 2,
        out_shape=[jax.ShapeDtypeStruct((2, s, 128), F32)] * 2,
        compiler_params=_cparams(("parallel", "arbitrary")),
    )(view, view, view, view, view)


def _dil_bwd(qkv, do, lse, cterm, group):
    _, dil = DIL_GROUPS[group]
    s = qkv.shape[0]
    sub = s // dil
    nb = sub // BLOCK
    view = qkv.reshape(sub, dil * 3 * DIL_W)
    slopes = _alibi_slopes()[group * DIL_HEADS:(group + 1) * DIL_HEADS]

    def col(which, shift):
        if shift == 0:
            return lambda n, r: (n, r * 3 + which)
        if shift < 0:
            return lambda n, r: (jnp.maximum(n - 1, 0), r * 3 + which)
        return lambda n, r: (jnp.minimum(n + 1, nb - 1), r * 3 + which)

    def own(shift):
        if shift == 0:
            return pl.BlockSpec((2, BLOCK * dil, 128), lambda n, r: (0, n, 0))
        return pl.BlockSpec((2, BLOCK * dil, 128), lambda n, r: (0, jnp.minimum(n + 1, nb - 1), 0))

    def body(q_ref, qn_ref, kc_ref, kp_ref, vc_ref, vp_ref, do_ref, don_ref, lse_ref, lsen_ref, c_ref, cn_ref,
             dq_ref, dk_ref, dv_ref):
        n, r = pl.program_id(0), pl.program_id(1)
        mine = pl.ds(r, BLOCK, stride=dil) if dil > 1 else slice(None)
        valid, dist = _band_terms(dil, n > 0)
        valid_n = _band_terms(dil, True)[0][:, :BLOCK] & (n < nb - 1)
        dist_n = dist[:, :BLOCK]
        q, qn = q_ref[...], qn_ref[...]
        kc, vc = kc_ref[...], vc_ref[...]
        k2 = jnp.concatenate([kp_ref[...], kc], axis=0)
        v2 = jnp.concatenate([vp_ref[...], vc], axis=0)
        dov, donv = _load_halves(do_ref, mine), _load_halves(don_ref, mine)
        lsev, lsenv = _load_halves(lse_ref, mine), _load_halves(lsen_ref, mine)
        cv, cnv = _load_halves(c_ref, mine), _load_halves(cn_ref, mine)
        masks = [_head_lane_mask(h, BLOCK) for h in range(DIL_HEADS)]

        def head_col(t, hm):
            return jnp.max(jnp.where(hm, t, NEG_INF), axis=1, keepdims=True)

        qhs = [jnp.where(hm, q, jnp.zeros_like(q)) for hm in masks]
        qnhs = [jnp.where(hm, qn, jnp.zeros_like(qn)) for hm in masks]
        dohs = [jnp.where(hm, dov, 0.0).astype(BF16) for hm in masks]
        donhs = [jnp.where(hm, donv, 0.0).astype(BF16) for hm in masks]
        logit = [_dot_nt(qhs[h], k2) for h in range(DIL_HEADS)]
        dp = [_dot_nt(dohs[h], v2) for h in range(DIL_HEADS)]
        logit_n = [_dot_nt(qnhs[h], kc) for h in range(DIL_HEADS)]
        dp_n = [_dot_nt(donhs[h], vc) for h in range(DIL_HEADS)]
        p16, dlog, pn16, dlog_n = [], [], [], []
        for h in range(DIL_HEADS):
            hm, slope = masks[h], float(slopes[h])
            p = jnp.where(valid, jnp.exp(logit[h] * 0.125 - slope * dist - head_col(lsev, hm)), 0.0)
            dlog.append((p * (dp[h] + head_col(cv, hm)) * 0.125).astype(BF16))
            p16.append(p.astype(BF16))
            pn = jnp.where(valid_n, jnp.exp(logit_n[h] * 0.125 - slope * dist_n - head_col(lsenv, hm)), 0.0)
            dlog_n.append((pn * (dp_n[h] + head_col(cnv, hm)) * 0.125).astype(BF16))
            pn16.append(pn.astype(BF16))
        dq_acc = jnp.zeros((BLOCK, DIL_W), F32)
        dk_acc = jnp.zeros((BLOCK, DIL_W), F32)
        dv_acc = jnp.zeros((BLOCK, DIL_W), F32)
        for h in range(DIL_HEADS):
            dq_acc = jnp.where(masks[h], _dot_nn(dlog[h], k2), dq_acc)
            dk_acc += _dot_tn(dlog[h][:, BLOCK:], qhs[h]) + _dot_tn(dlog_n[h], qnhs[h])
            dv_acc += _dot_tn(p16[h][:, BLOCK:], dohs[h]) + _dot_tn(pn16[h], donhs[h])
        dq_ref[...] = dq_acc.astype(BF16)
        dk_ref[...] = dk_acc.astype(BF16)
        dv_ref[...] = dv_acc.astype(BF16)

    blk = (BLOCK, DIL_W)
    outs = pl.pallas_call(
        body,
        name=f"dil_bwd_g{group}",
        grid=(nb, dil),
        in_specs=[pl.BlockSpec(blk, col(0, 0)), pl.BlockSpec(blk, col(0, 1)),
                  pl.BlockSpec(blk, col(1, 0)), pl.BlockSpec(blk, col(1, -1)),
                  pl.BlockSpec(blk, col(2, 0)), pl.BlockSpec(blk, col(2, -1)),
                  own(0), own(1), own(0), own(1), own(0), own(1)],
        out_specs=[pl.BlockSpec(blk, lambda n, r: (n, r))] * 3,
        out_shape=[jax.ShapeDtypeStruct((sub, dil * DIL_W), BF16)] * 3,
        compiler_params=_cparams(("parallel", "parallel")),
    )(view, view, view, view, view, view, do, do, lse, lse, cterm, cterm)
    return tuple(t.reshape(s, DIL_W) for t in outs)


SB_PAIRS = SB_HEADS // 2
SB_COL0 = 0
LOG2E = 1.4426950408889634


SB_EXP_CLAMP = 64.0


def _sb_softplus2(zs):
    t = 1.0 + jnp.exp2(jnp.minimum(zs, SB_EXP_CLAMP))
    return jnp.maximum(jnp.log(t) * LOG2E, zs)


def _sb_consts(nkb):
    row = lax.broadcasted_iota(jnp.int32, (SB_BQ, SB_BK), 0)
    colk = lax.broadcasted_iota(jnp.int32, (SB_BQ, SB_BK), 1)
    rr = lax.broadcasted_iota(jnp.int32, (SB_BK, SB_BK), 0)
    cc = lax.broadcasted_iota(jnp.int32, (SB_BK, SB_BK), 1)
    lane = lax.broadcasted_iota(jnp.int32, (SB_BQ, 128), 1)
    assert 2 * nkb <= 128
    return colk < row, rr, cc, lane < HEAD_DIM, lane


def _split_heads(t):
    first = lax.broadcasted_iota(jnp.int32, t.shape, 1) < HEAD_DIM
    zero = jnp.zeros_like(t)
    return jnp.where(first, t, zero), jnp.where(first, zero, t)


def _sb_fwd(qkv, shard_pack):
    s = qkv.shape[0]
    nq, nkb = s // SB_BQ, s // SB_BK
    zscale = LOG2E / math.sqrt(HEAD_DIM)
    r_pack, w_pack = shard_pack.shape

    def body(q_ref, k_ref, v_ref, pack_ref, o_ref, a_row, others_ref, zs_scr, a_scr, acc_scr, cl_scr,
             send_sems, recv_sems):
        i = pl.program_id(1)
        pair = pl.program_id(0)
        gather = (pack_ref, others_ref, send_sems, recv_sems)

        @pl.when((pair == 0) & (i == 0))
        def _():
            _gather_start(*gather)

        @pl.when((pair == 1) & (i == 0))
        def _():
            _gather_pass_on(*gather)

        @pl.when((pair == SB_PAIRS - 1) & (i == nq - 1))
        def _():
            _gather_finish(*gather)

        causal, rr, cc, _, _ = _sb_consts(nkb)
        later = (rr > cc).astype(BF16)
        qh = _split_heads(q_ref[...])

        def rows(j):
            return pl.ds(pl.multiple_of(j * SB_BK, SB_BK), SB_BK)

        def scores_to(slot, j):
            kb = k_ref[rows(j), :]
            for hh in range(2):
                zs_scr[slot, hh] = _dot_nt(qh[hh], kb) * zscale

        def weights(slot, j, masked):
            xs, sums, sufs = [], [], []
            for hh in range(2):
                zs = zs_scr[slot, hh]
                sp = _sb_softplus2(zs)
                if masked:
                    sp = jnp.where(causal, sp, 0.0)
                xs.append(zs - sp)
                sums.append(jnp.sum(sp, axis=1, keepdims=True))
                sufs.append(_dot_f32_by_01(sp, later, 2))
            for hh in range(2):
                cl = cl_scr[hh]
                a = jnp.exp2(xs[hh] - (sufs[hh] + jnp.concatenate([cl, cl], axis=1)))
                if masked:
                    a = jnp.where(causal, a, 0.0)
                a16 = a.astype(BF16)
                a_scr[slot, :, hh * SB_BK:(hh + 1) * SB_BK] = a16
                a_row[0, 0, j, :, hh * SB_BK:(hh + 1) * SB_BK] = a16
                cl_scr[hh] = cl + sums[hh]

        def add_av(slot, j):
            v0, v1 = _split_heads(v_ref[rows(j), :])
            acc_scr[...] += _dot_nn(a_scr[slot], jnp.concatenate([v0, v1], axis=0))

        acc_scr[...] = jnp.zeros_like(acc_scr)
        cl_scr[...] = jnp.zeros_like(cl_scr)
        scores_to(0, i)
        scores_to(1, jnp.maximum(i - 1, 0))
        weights(0, i, True)

        def step(j, prev, cur):
            scores_to(prev, jnp.maximum(j - 1, 0))
            add_av(prev, j + 1)
            weights(cur, j, False)

        def two_steps(u, _):
            j = i - 1 - 2 * u
            step(j, 0, 1)
            step(j - 1, 1, 0)
            return 0

        lax.fori_loop(0, i // 2, two_steps, 0)

        @pl.when(i % 2 == 1)
        def _():
            step(0, 0, 1)
            add_av(1, 0)

        @pl.when(i % 2 == 0)
        def _():
            add_av(0, 0)

        o_ref[...] = acc_scr[...]

    def full(which):
        return pl.BlockSpec((s, 128), lambda p, i: (0, SB_COL0 + 4 * which + p))

    return pl.pallas_call(
        body,
        name="sb_fwd",
        grid=(SB_PAIRS, nq),
        in_specs=[pl.BlockSpec((SB_BQ, 128), lambda p, i: (i, SB_COL0 + p)), full(1), full(2), ANY],
        out_specs=[pl.BlockSpec((SB_BQ, 128), lambda p, i: (i, p)),
                   pl.BlockSpec((1, 1, nkb, SB_BQ, 2 * SB_BK), lambda p, i: (p, i, 0, 0, 0)), ANY],
        out_shape=[jax.ShapeDtypeStruct((s, SB_W), F32),
                   jax.ShapeDtypeStruct((SB_PAIRS, nq, nkb, SB_BQ, 2 * SB_BK), BF16),
                   jax.ShapeDtypeStruct((N_CHIPS, 2, r_pack // 2, w_pack), shard_pack.dtype)],
        scratch_shapes=[pltpu.VMEM((2, 2, SB_BQ, SB_BK), F32), pltpu.VMEM((2, SB_BQ, 2 * SB_BK), BF16),
                        pltpu.VMEM((SB_BQ, 128), F32), pltpu.VMEM((2, SB_BQ, 128), F32),
                        pltpu.SemaphoreType.DMA((6,)), pltpu.SemaphoreType.DMA((6,))],
        compiler_params=_cparams(("arbitrary", "arbitrary")),
    )(qkv, qkv, qkv, shard_pack.reshape(2, r_pack // 2, w_pack))


def _sb_bwd(qkv, do, a_hbm, chip_sums):
    s = qkv.shape[0]
    nq, nkb = s // SB_BQ, s // SB_BK
    scale = 1.0 / math.sqrt(HEAD_DIM)
    zscale = LOG2E * scale

    def body(q_ref, k_ref, v_ref, do_ref, a_row, sums_ref, dq_ref, dk_ref, dv_ref, got_ref,
             zs_scr, da_scr, dz_scr, a_scr, cg_scr, send_sems, recv_sems):
        i = pl.program_id(1)
        pair = pl.program_id(0)
        first_step = (pair == 0) & (i == 0)
        last_step = (pair == SB_PAIRS - 1) & (i == nq - 1)

        @pl.when(first_step)
        def _():
            _exchange_start(sums_ref, got_ref, send_sems, recv_sems)

        @pl.when(i == 0)
        def _():
            dk_ref[...] = jnp.zeros_like(dk_ref)
            dv_ref[...] = jnp.zeros_like(dv_ref)

        causal, rr, cc, first, _ = _sb_consts(nkb)
        earlier = (rr < cc).astype(BF16)
        q2 = q_ref[...]
        qh = _split_heads(q2)
        do2 = do_ref[...].astype(BF16)
        doh = _split_heads(do2)

        def rows(j):
            return pl.ds(pl.multiple_of(j * SB_BK, SB_BK), SB_BK)

        def products_to(slot, j):
            kb, vb = k_ref[rows(j), :], v_ref[rows(j), :]
            for hh in range(2):
                zs_scr[slot, hh] = _dot_nt(qh[hh], kb) * (-zscale)
                da_scr[slot, hh] = _dot_nt(doh[hh], vb)

        head0_rows = lax.broadcasted_iota(jnp.int32, (128, SB_BK), 0) < HEAD_DIM

        def by_head(t):
            return jnp.where(head0_rows, t[:, :SB_BK], t[:, SB_BK:])

        def apply(slot, j):
            k0, k1 = _split_heads(k_ref[rows(j), :])
            dq_ref[...] += _dot_nn(dz_scr[slot], jnp.concatenate([k0, k1], axis=0)) * scale
            dk_ref[0, j] += by_head(_dot_tn(q2, dz_scr[slot])) * scale
            dv_ref[0, j] += by_head(_dot_tn(do2, a_scr[slot]))

        def grads(slot, j, masked):
            gs, gpres = [], []
            for hh in range(2):
                a16 = a_row[0, 0, j, :, hh * SB_BK:(hh + 1) * SB_BK]
                a_scr[slot, :, hh * SB_BK:(hh + 1) * SB_BK] = a16
                g = a16.astype(F32) * da_scr[slot, hh]
                gs.append(g)
                gpres.append(_dot_f32_by_01(g, earlier, 1))
            sigs = []
            for hh in range(2):
                e = jnp.exp2(jnp.minimum(zs_scr[slot, hh], SB_EXP_CLAMP))
                sigs.append(pl.reciprocal(1.0 + e, approx=True))
            for hh in range(2):
                cg = cg_scr[hh]
                dz = gs[hh] - (gs[hh] + (gpres[hh] + jnp.concatenate([cg, cg], axis=1))) * sigs[hh]
                if masked:
                    dz = jnp.where(causal, dz, 0.0)
                dz_scr[slot, :, hh * SB_BK:(hh + 1) * SB_BK] = dz.astype(BF16)
                cg_scr[hh] = cg + jnp.sum(gs[hh], axis=1, keepdims=True)

        dq_ref[...] = jnp.zeros_like(dq_ref)
        cg_scr[...] = jnp.zeros_like(cg_scr)
        dz_scr[1] = jnp.zeros((SB_BQ, 2 * SB_BK), BF16)
        a_scr[1] = jnp.zeros((SB_BQ, 2 * SB_BK), BF16)
        products_to(0, 0)

        def step(j, cur, nxt):
            products_to(nxt, j + 1)
            apply(nxt, jnp.maximum(j - 1, 0))
            grads(cur, j, False)

        def two_steps(u, _):
            step(2 * u, 0, 1)
            step(2 * u + 1, 1, 0)
            return 0

        lax.fori_loop(0, i // 2, two_steps, 0)

        def last(cur, nxt):
            apply(nxt, jnp.maximum(i - 1, 0))
            grads(cur, i, True)
            apply(cur, i)

        @pl.when(i % 2 == 1)
        def _():
            step(i - 1, 0, 1)
            last(1, 0)

        @pl.when(i % 2 == 0)
        def _():
            last(0, 1)

        @pl.when(last_step)
        def _():
            _exchange_wait(sums_ref, got_ref, send_sems, recv_sems)

    def full(which):
        return pl.BlockSpec((s, 128), lambda p, i: (0, SB_COL0 + 4 * which + p))

    qblk = pl.BlockSpec((SB_BQ, 128), lambda p, i: (i, p))
    acc = pl.BlockSpec((1, nkb, 128, SB_BK), lambda p, i: (p, 0, 0, 0))
    acc_shape = jax.ShapeDtypeStruct((SB_PAIRS, nkb, 128, SB_BK), F32)
    dq, dk_t, dv_t, got = pl.pallas_call(
        body,
        name="sb_bwd",
        grid=(SB_PAIRS, nq),
        in_specs=[pl.BlockSpec((SB_BQ, 128), lambda p, i: (i, SB_COL0 + p)), full(1), full(2), qblk,
                  pl.BlockSpec((1, 1, nkb, SB_BQ, 2 * SB_BK), lambda p, i: (p, i, 0, 0, 0)), ANY],
        out_specs=[qblk, acc, acc, ANY],
        out_shape=[jax.ShapeDtypeStruct((s, SB_W), F32), acc_shape, acc_shape,
                   jax.ShapeDtypeStruct(chip_sums.shape, chip_sums.dtype)],
        scratch_shapes=[pltpu.VMEM((2, 2, SB_BQ, SB_BK), F32), pltpu.VMEM((2, 2, SB_BQ, SB_BK), F32),
                        pltpu.VMEM((2, SB_BQ, 2 * SB_BK), BF16), pltpu.VMEM((2, SB_BQ, 2 * SB_BK), BF16),
                        pltpu.VMEM((2, SB_BQ, 128), F32),
                        pltpu.SemaphoreType.DMA((3,)), pltpu.SemaphoreType.DMA((3,))],
        compiler_params=_cparams(("arbitrary", "arbitrary")),
    )(qkv, qkv, qkv, do, a_hbm, chip_sums)

    def untranspose(t):
        return jnp.transpose(t, (1, 3, 0, 2)).reshape(s, SB_W)

    return dq, untranspose(dk_t), untranspose(dv_t), got


MERGE_TILE = 256


def _group_mix(lses):
    mx = jnp.maximum(jnp.maximum(lses[0], lses[1]), lses[2])
    es = [jnp.exp(t - mx) for t in lses]
    den = es[0] + es[1] + es[2]
    return [e / den for e in es]


def _merge_fwd(o_groups, lse_groups, o_sb, gl, b_gate, w_up_dil, w_up_sb):
    s = gl.shape[0]
    t = MERGE_TILE

    def body(o0, o1, o2, l0, l1, l2, ob_ref, gl_ref, bg_ref, wd_ref, ws_ref, merged_ref, oa_ref):
        rows = slice(None)
        w = _group_mix([_load_halves(l, rows) for l in (l0, l1, l2)])
        og = [_load_halves(o, rows) for o in (o0, o1, o2)]
        oa = (w[0] * og[0] + w[1] * og[1] + w[2] * og[2]).astype(BF16)
        ua = _dot_nn(oa, wd_ref[...])
        ub = _dot_nn(ob_ref[...].astype(BF16), ws_ref[...])
        gate = jax.nn.sigmoid(gl_ref[...] + bg_ref[...])
        merged_ref[...] = (gate[:, :D_MODEL] * ua + gate[:, D_MODEL:] * ub).astype(BF16)
        oa_ref[...] = oa

    dil = pl.BlockSpec((t, DIL_W), lambda i: (i, 0))
    halves = pl.BlockSpec((2, t, 128), lambda i: (0, i, 0))
    const = lambda shape: pl.BlockSpec(shape, lambda i: (0, 0))
    return pl.pallas_call(
        body,
        name="merge_fwd",
        grid=(s // t,),
        in_specs=[halves] * 6 + [pl.BlockSpec((t, SB_W), lambda i: (i, 0)), pl.BlockSpec((t, GATE_W), lambda i: (i, 0)),
                                 const((1, GATE_W)), const((DIL_W, D_MODEL)), const((SB_W, D_MODEL))],
        out_specs=[pl.BlockSpec((t, D_MODEL), lambda i: (i, 0)), dil],
        out_shape=[jax.ShapeDtypeStruct((s, D_MODEL), BF16), jax.ShapeDtypeStruct((s, DIL_W), BF16)],
        compiler_params=_cparams(("parallel",)),
    )(*o_groups, *lse_groups, o_sb, gl, b_gate, w_up_dil, w_up_sb)


def _merge_bwd(dmerged, o_groups, lse_groups, o_sb, gl, b_gate, w_up_dil, w_up_sb, swap):
    s = gl.shape[0]
    t = MERGE_TILE
    n_chunks, r_swap, w_swap = swap.shape
    swap = swap.reshape(n_chunks, 2, r_swap // 2, w_swap)

    def body(dm_ref, o0, o1, o2, l0, l1, l2, ob_ref, gl_ref, bg_ref, wd_ref, ws_ref, swap_ref,
             dua_ref, dub_ref, dgl_ref, dbg_ref, dosb_ref, d0, d1, d2, c0, c1, c2, got_ref, send_sem, recv_sem):
        i = pl.program_id(0)

        @pl.when(i == 0)
        def _():
            _swap_copy(swap_ref, got_ref, send_sem, recv_sem).start()

        @pl.when(i == pl.num_programs(0) - 1)
        def _():
            _swap_copy(swap_ref, got_ref, send_sem, recv_sem).wait()

        rows = slice(None)
        og = [_load_halves(o, rows) for o in (o0, o1, o2)]
        w = _group_mix([_load_halves(l, rows) for l in (l0, l1, l2)])
        oa = (w[0] * og[0] + w[1] * og[1] + w[2] * og[2]).astype(BF16)
        ua = _dot_nn(oa, wd_ref[...])
        ub = _dot_nn(ob_ref[...].astype(BF16), ws_ref[...])
        gate = jax.nn.sigmoid(gl_ref[...] + bg_ref[...])
        ga, gb = gate[:, :D_MODEL], gate[:, D_MODEL:]
        dm = dm_ref[...]
        dua = (dm * ga).astype(BF16)
        dub = (dm * gb).astype(BF16)
        dua_ref[...] = dua
        dub_ref[...] = dub
        dgl_a = dm * ua * ga * (1.0 - ga)
        dgl_b = dm * ub * gb * (1.0 - gb)
        dgl_ref[:, :D_MODEL] = dgl_a.astype(BF16)
        dgl_ref[:, D_MODEL:] = dgl_b.astype(BF16)
        part = jnp.concatenate([jnp.sum(dgl_a.reshape(t // 8, 8, D_MODEL), axis=0),
                                jnp.sum(dgl_b.reshape(t // 8, 8, D_MODEL), axis=0)], axis=1)

        @pl.when(i == 0)
        def _():
            dbg_ref[...] = part

        @pl.when(i > 0)
        def _():
            dbg_ref[...] += part

        dosb_ref[...] = _dot_nt(dub, ws_ref[...])
        doa = _dot_nt(dua, wd_ref[...])
        rr = lax.broadcasted_iota(jnp.int32, (DIL_W, DIL_W), 0) // HEAD_DIM
        cc = lax.broadcasted_iota(jnp.int32, (DIL_W, DIL_W), 1) // HEAD_DIM
        same_head = (rr == cc).astype(BF16)
        dw = [_dot_f32_by_01(doa * og[g], same_head) for g in range(3)]
        mean_dw = w[0] * dw[0] + w[1] * dw[1] + w[2] * dw[2]
        for g, (d_ref, c_ref) in enumerate(((d0, c0), (d1, c1), (d2, c2))):
            _store_halves(d_ref, rows, w[g] * doa)
            _store_halves(c_ref, rows, -w[g] * mean_dw)

    dil = pl.BlockSpec((2, t, 128), lambda i: (0, i, 0))
    wide = pl.BlockSpec((t, D_MODEL), lambda i: (i, 0))
    gate2 = pl.BlockSpec((t, GATE_W), lambda i: (i, 0))
    sbw = pl.BlockSpec((t, SB_W), lambda i: (i, 0))
    const = lambda shape: pl.BlockSpec(shape, lambda i: (0, 0))
    return pl.pallas_call(
        body,
        name="merge_bwd",
        grid=(s // t,),
        in_specs=[wide] + [dil] * 6 + [sbw, gate2, const((1, GATE_W)), const((DIL_W, D_MODEL)), const((SB_W, D_MODEL)),
                                       ANY],
        out_specs=[wide, wide, gate2, const((8, GATE_W)), sbw] + [dil] * 6 + [ANY],
        out_shape=[jax.ShapeDtypeStruct((s, D_MODEL), BF16), jax.ShapeDtypeStruct((s, D_MODEL), BF16),
                   jax.ShapeDtypeStruct((s, GATE_W), BF16), jax.ShapeDtypeStruct((8, GATE_W), F32),
                   jax.ShapeDtypeStruct((s, SB_W), F32)] + [jax.ShapeDtypeStruct((2, s, 128), F32)] * 6
        + [jax.ShapeDtypeStruct((n_chunks, r_swap // 2, w_swap), swap.dtype)],
        scratch_shapes=[pltpu.SemaphoreType.DMA, pltpu.SemaphoreType.DMA],
        compiler_params=_cparams(("arbitrary",)),
    )(dmerged, *o_groups, *lse_groups, o_sb, gl, b_gate, w_up_dil, w_up_sb, swap)


ANY = pl.BlockSpec(memory_space=pl.ANY)


def _place():
    x, y, c = lax.axis_index("x"), lax.axis_index("y"), lax.axis_index("c")
    other_chips = [(1 - x, y), (x, 1 - y), (1 - x, 1 - y)]
    return x, y, c, other_chips


def _gather_copies(p_ref, out_ref, send_sems, recv_sems):
    x, y, c, chips = _place()
    me, sibling = 2 * x + y, (x, y, 1 - c)
    idx = [2 * chip[0] + chip[1] for chip in chips]

    def copy(k, chip_idx, core, to, src=None):
        return pltpu.make_async_remote_copy(
            src_ref=out_ref.at[chip_idx, core] if src is None else src, dst_ref=out_ref.at[chip_idx, core],
            send_sem=send_sems.at[k], recv_sem=recv_sems.at[k], device_id=to, device_id_type=MESH)

    first = lambda j: copy(j, me, c, (*chips[j], c), src=p_ref.at[c])
    landed = lambda j: copy(j, idx[j], c, (x, y, c))
    passed = lambda j: copy(3 + j, idx[j], c, sibling)
    handed = lambda j: copy(3 + j, idx[j], 1 - c, (x, y, c))
    return first, landed, passed, handed


def _gather_start(*refs):
    first = _gather_copies(*refs)[0]
    for j in range(3):
        first(j).start()


def _gather_pass_on(*refs):
    _, landed, passed, _ = _gather_copies(*refs)
    for j in range(3):
        landed(j).wait_recv()
        passed(j).start()


def _gather_finish(*refs):
    first, _, passed, handed = _gather_copies(*refs)
    for j in range(3):
        handed(j).wait_recv()
    for j in range(3):
        first(j).wait_send()
        passed(j).wait_send()


def _fill_own_slot(others, pack):
    n, _, rh, wd = others.shape
    me = 2 * lax.axis_index("x") + lax.axis_index("y")
    mine = lax.broadcasted_iota(jnp.int32, (n, 1, 1, 1), 0) == me
    return jnp.where(mine, pack.reshape(1, 2, rh, wd), others).reshape(n, 2 * rh, wd)


def _swap_copy(g_ref, out_ref, send_sem, recv_sem):
    x, y, c, _ = _place()
    return pltpu.make_async_remote_copy(
        src_ref=g_ref.at[:, 1 - c], dst_ref=out_ref,
        send_sem=send_sem, recv_sem=recv_sem, device_id=(x, y, 1 - c), device_id_type=MESH)


def _swap_halves(g):
    n, r, wd = g.shape
    rh = r // 2
    g = g.reshape(n, 2, rh, wd)

    def body(g_ref, out_ref, send_sem, recv_sem):
        cp = _swap_copy(g_ref, out_ref, send_sem, recv_sem)
        cp.start()
        cp.wait()

    return pl.pallas_call(
        body,
        name="grad_swap_halves",
        in_specs=[ANY],
        out_specs=ANY,
        out_shape=jax.ShapeDtypeStruct((n, rh, wd), g.dtype),
        scratch_shapes=[pltpu.SemaphoreType.DMA, pltpu.SemaphoreType.DMA],
    )(g)


def _add_halves(g, got, core):
    n, r, wd = g.shape
    rh = r // 2
    t = rh // 4
    nt = rh // t

    def body(c_ref, a_ref, b_ref, o_ref):
        o_ref[...] = (a_ref[0] + b_ref[...]).astype(BF16)

    grid_spec = pltpu.PrefetchScalarGridSpec(
        num_scalar_prefetch=1,
        grid=(n, nt),
        in_specs=[pl.BlockSpec((1, 1, t, wd), lambda s, i, c: (s, c[0], i, 0)),
                  pl.BlockSpec((1, t, wd), lambda s, i, c: (s, i, 0))],
        out_specs=pl.BlockSpec((1, t, wd), lambda s, i, c: (s, i, 0)),
    )
    return pl.pallas_call(
        body,
        name="grad_add_halves",
        grid_spec=grid_spec,
        out_shape=jax.ShapeDtypeStruct((n, rh, wd), BF16),
        compiler_params=_cparams(("parallel", "parallel")),
    )(core, g.reshape(n, 2, rh, wd), got)


def _exchange_copies(h_ref, out_ref, send_sems, recv_sems):
    x, y, c, chips = _place()
    me = 2 * x + y

    def copy(j, slot):
        them = 2 * chips[j][0] + chips[j][1]
        return pltpu.make_async_remote_copy(
            src_ref=h_ref.at[them], dst_ref=out_ref.at[me if slot == "mine" else them],
            send_sem=send_sems.at[j], recv_sem=recv_sems.at[j], device_id=(*chips[j], c), device_id_type=MESH)

    return (lambda j: copy(j, "mine")), (lambda j: copy(j, "theirs"))


def _exchange_start(h_ref, out_ref, send_sems, recv_sems):
    send = _exchange_copies(h_ref, out_ref, send_sems, recv_sems)[0]
    for j in range(3):
        send(j).start()


def _exchange_wait(h_ref, out_ref, send_sems, recv_sems):
    send, arrival = _exchange_copies(h_ref, out_ref, send_sems, recv_sems)
    for j in range(3):
        arrival(j).wait_recv()
    for j in range(3):
        send(j).wait_send()


def _sum_chips(b, h, chip):
    n, rh, wd = b.shape
    t = rh // 4

    def body(chip_ref, b_ref, own_ref, o_ref):
        own = own_ref[0]
        s0, s1, s2, s3 = (jnp.where(chip_ref[0] == k, own, b_ref[k]).astype(F32) for k in range(n))
        o_ref[...] = ((s0 + s1) + s2) + s3

    grid_spec = pltpu.PrefetchScalarGridSpec(
        num_scalar_prefetch=1,
        grid=(rh // t,),
        in_specs=[pl.BlockSpec((n, t, wd), lambda i, chip: (0, i, 0)),
                  pl.BlockSpec((1, t, wd), lambda i, chip: (chip[0], i, 0))],
        out_specs=pl.BlockSpec((t, wd), lambda i, chip: (i, 0)),
    )
    return pl.pallas_call(
        body,
        name="grad_sum_chips",
        grid_spec=grid_spec,
        out_shape=jax.ShapeDtypeStruct((rh, wd), F32),
        compiler_params=_cparams(("parallel",)),
    )(chip, b, h)


def _join_halves(tc):
    rh, wd = tc.shape

    def body(t_ref, out_ref, send_sem, recv_sem):
        x, y, c, _ = _place()
        cp = pltpu.make_async_remote_copy(
            src_ref=t_ref, dst_ref=out_ref.at[c],
            send_sem=send_sem, recv_sem=recv_sem, device_id=(x, y, 1 - c), device_id_type=MESH)
        cp.start()
        cp.wait()

    halves = pl.pallas_call(
        body,
        name="grad_join_halves",
        in_specs=[ANY],
        out_specs=ANY,
        out_shape=jax.ShapeDtypeStruct((2, rh, wd), tc.dtype),
        scratch_shapes=[pltpu.SemaphoreType.DMA, pltpu.SemaphoreType.DMA],
    )(tc)
    return lax.dynamic_update_slice(halves, tc[None], (lax.axis_index("c"), 0, 0)).reshape(2 * rh, wd)


def _all_reduce_small(pack):
    rows, lanes = pack.shape

    def body(p_ref, out_ref, buf, send_sems, recv_sems):
        x, y, c, _ = _place()
        me = 4 * x + 2 * y + c
        buf[me] = p_ref[...]
        sends = []
        for k in range(1, N_DEV):
            peer = (x ^ (k >> 2), y ^ ((k >> 1) & 1), c ^ (k & 1))
            sends.append(pltpu.make_async_remote_copy(
                src_ref=p_ref, dst_ref=buf.at[me], send_sem=send_sems.at[k - 1], recv_sem=recv_sems.at[k - 1],
                device_id=peer, device_id_type=MESH))
        for cp in sends:
            cp.start()
        for k in range(1, N_DEV):
            pltpu.make_async_remote_copy(
                src_ref=p_ref, dst_ref=buf.at[me ^ k], send_sem=send_sems.at[k - 1], recv_sem=recv_sems.at[k - 1],
                device_id=(x, y, c), device_id_type=MESH).wait_recv()
        for cp in sends:
            cp.wait_send()
        total = buf[0]
        for d in range(1, N_DEV):
            total = total + buf[d]
        out_ref[...] = total

    vm = pl.BlockSpec(memory_space=pltpu.VMEM)
    return pl.pallas_call(
        body,
        name="all_reduce_small",
        in_specs=[vm],
        out_specs=vm,
        out_shape=jax.ShapeDtypeStruct((rows, lanes), F32),
        scratch_shapes=[pltpu.VMEM((N_DEV, rows, lanes), F32), pltpu.SemaphoreType.DMA((N_DEV - 1,)),
                        pltpu.SemaphoreType.DMA((N_DEV - 1,))],
    )(pack)


def _adamw(g, w, m, v, name):
    rows, cols = g.shape
    t = rows
    for cand in (256, 128, 64, 32, 16, 8):
        if rows % cand == 0:
            t = cand
            break

    def body(g_ref, w_ref, m_ref, v_ref, d_ref, nm_ref, nv_ref):
        gv = g_ref[...]
        mv = ADAM_B1 * m_ref[...] + (1.0 - ADAM_B1) * gv
        vv = ADAM_B2 * v_ref[...] + (1.0 - ADAM_B2) * (gv * gv)
        m_hat = mv / (1.0 - ADAM_B1 ** ADAM_STEP)
        v_hat = vv / (1.0 - ADAM_B2 ** ADAM_STEP)
        d_ref[...] = -ADAM_LR * (m_hat / (jnp.sqrt(v_hat) + ADAM_EPS) + ADAM_WD * w_ref[...])
        nm_ref[...] = mv
        nv_ref[...] = vv

    blk = pl.BlockSpec((t, cols), lambda i: (i, 0))
    return pl.pallas_call(
        body,
        name=name,
        grid=(rows // t,),
        in_specs=[blk] * 4,
        out_specs=[blk] * 3,
        out_shape=[jax.ShapeDtypeStruct((rows, cols), F32)] * 3,
        compiler_params=_cparams(("parallel",)),
    )(g, w, m, v)


PACK_W = 1024
BIG = (("w_in", (D_MODEL, IN_COLS), 1), ("w_up_dil", (DIL_W, D_MODEL), 1), ("w_up_sb", (SB_W, D_MODEL), 1),
       ("w_out", (D_MODEL, D_MODEL), 0), ("w_mlp_in", (D_MODEL, D_FF), 1), ("w_mlp_out", (D_FF, D_MODEL), 0))


def _shard_shape(shape, axis):
    return tuple(d // N_CHIPS if a == axis else d for a, d in enumerate(shape))


MIXER_GROUP, MLP_GROUP = BIG[:4], BIG[4:]
EARLY_WEIGHTS, LATE_WEIGHTS = BIG[:1], BIG[1:]


def _pack_rows(group=BIG):
    rows, at = {}, 0
    for name, shape, axis in group:
        n = math.prod(_shard_shape(shape, axis)) // PACK_W
        rows[name] = (at, n)
        at += n
    return rows, at


def _pack_shards(shards, group):
    return jnp.concatenate([shards[name].reshape(-1, PACK_W) for name, _, _ in group], axis=0)


def _unpack_full(gathered, group):
    rows, _ = _pack_rows(group)
    full = {}
    for name, shape, axis in group:
        at, n = rows[name]
        parts = gathered[:, at:at + n, :].reshape((N_CHIPS,) + _shard_shape(shape, axis))
        if axis == 0:
            full[name] = parts.reshape(shape)
        else:
            full[name] = jnp.transpose(parts, (1, 0, 2)).reshape(shape)
    return full


def _pack_full_grads(grads, group):
    chunks = []
    for name, shape, axis in group:
        g = grads[name]
        if axis == 0:
            parts = g.reshape((N_CHIPS, shape[0] // N_CHIPS, shape[1]))
        else:
            parts = jnp.transpose(g.reshape((shape[0], N_CHIPS, shape[1] // N_CHIPS)), (1, 0, 2))
        chunks.append(parts.reshape(N_CHIPS, -1, PACK_W))
    return jnp.concatenate(chunks, axis=1)


def _unpack_shard(packed, group):
    rows, _ = _pack_rows(group)
    return {name: packed[rows[name][0]:rows[name][0] + rows[name][1]].reshape(_shard_shape(shape, axis))
            for name, shape, axis in group}


def _local_step(x, target, early_shards, late_shards, norm_mix_g, b_gate, norm_mlp_g, norm_final_g, core):
    h, early = _rms_fwd_and_gather(x, norm_mix_g, early_shards)
    w = _unpack_full(early, EARLY_WEIGHTS)
    w_in = w["w_in"]
    sb0 = 9 * DIL_W
    w_sb, w_gate = w_in[:, sb0:QKV_W], w_in[:, QKV_W:]
    w_dil = [jnp.concatenate([w_in[:, (3 * i + g) * DIL_W:(3 * i + g + 1) * DIL_W] for i in range(3)], axis=1)
             for g in range(3)]

    qkv_dil = [_matmul(h, w_dil[g], mode="nn", out_dtypes=(BF16,), name=f"proj_dil_g{g}", tn=768)[0] for g in range(3)]
    (qkv_sb,) = _matmul(h, w_sb, mode="nn", out_dtypes=(BF16,), name="proj_sb", tn=768)
    (gl,) = _matmul(h, w_gate, mode="nn", out_dtypes=(F32,), name="proj_gate")
    dil = [_dil_fwd(qkv_dil[g], g) for g in range(3)]
    o_groups, lse_groups = [d[0] for d in dil], [d[1] for d in dil]
    o_sb, a_sb, late_others = _sb_fwd(qkv_sb, late_shards)
    w = {**w, **_unpack_full(_fill_own_slot(late_others, late_shards), LATE_WEIGHTS)}
    merged, o_a = _merge_fwd(o_groups, lse_groups, o_sb, gl, b_gate, w["w_up_dil"], w["w_up_sb"])
    def residual_and_norm(acc, res, g):
        x1 = res + acc
        return x1, _rms_rows(x1)[0] * g

    x1, h2 = _matmul(merged, w["w_out"], mode="nn", out_dtypes=(F32, BF16), name="out_proj", tm=ROW_TILE,
                     extras=(x, norm_mlp_g), epilogue=residual_and_norm)
    u, act = _matmul(h2, w["w_mlp_in"], mode="nn", out_dtypes=(BF16, BF16), name="mlp_in",
                     epilogue=lambda acc: (acc, jnp.square(jnp.maximum(acc, 0.0))))

    def residual_and_loss(acc, res, tgt, g):
        xh, r = _rms_rows(res + acc)
        err = xh * g - tgt
        dy = err * (1.0 / D_MODEL)
        dxh = dy * g
        dx2 = r * (dxh - xh * jnp.mean(dxh * xh, axis=-1, keepdims=True))
        return dx2, _rows_sum8(dy * xh), (0.5 / D_MODEL) * _rows_sum8(err * err)

    dx2, dg_final, loss_part = _matmul(
        act, w["w_mlp_out"], mode="nn", out_dtypes=(F32, ("part", F32), ("part", F32)), name="mlp_out", tm=ROW_TILE,
        tk=2048, extras=(x1, target, norm_final_g.reshape(1, D_MODEL)), epilogue=residual_and_loss)

    (du,) = _matmul(dx2, w["w_mlp_out"], mode="nt", out_dtypes=(BF16,), name="mlp_out_dx",
                    extras=(u,), epilogue=lambda acc, uu: (acc * (2.0 * jnp.maximum(uu.astype(F32), 0.0)),))
    (g_mlp_out,) = _matmul(act, dx2, mode="tn", out_dtypes=(F32,), name="mlp_out_dw")
    (g_mlp_in,) = _matmul(h2, du, mode="tn", out_dtypes=(F32,), name="mlp_in_dw")

    def norm_bwd(acc, xx, dres, g):
        dx, dg = _rms_bwd_rows(acc, xx, g)
        return dres + dx, dg

    dx1, dg_mlp = _matmul(du, w["w_mlp_in"], mode="nt", out_dtypes=(F32, ("part", F32)), name="mlp_in_dx",
                          tm=ROW_TILE, tk=2048, extras=(x1, dx2, norm_mlp_g), epilogue=norm_bwd)

    mlp_pack = _pack_full_grads({"w_mlp_in": g_mlp_in, "w_mlp_out": g_mlp_out}, MLP_GROUP)
    (dmerged,) = _matmul(dx1, w["w_out"], mode="nt", out_dtypes=(F32,), name="out_proj_dx")
    (g_out,) = _matmul(merged, dx1, mode="tn", out_dtypes=(F32,), name="out_proj_dw")
    mb = _merge_bwd(dmerged, o_groups, lse_groups, o_sb, gl, b_gate, w["w_up_dil"], w["w_up_sb"], mlp_pack)
    dua, dub, dgl, dbg, do_sb = mb[:5]
    do_groups, c_groups = mb[5:8], mb[8:11]
    mlp_sums = _add_halves(mlp_pack, mb[11], core)
    (g_up_dil,) = _matmul(o_a, dua, mode="tn", out_dtypes=(F32,), name="up_dil_dw")
    (g_up_sb,) = _matmul(o_sb, dub, mode="tn", out_dtypes=(F32,), name="up_sb_dw")
    dq_sb, dk_sb, dv_sb, mlp_got = _sb_bwd(qkv_sb, do_sb, a_sb, mlp_sums)
    dil_b = [_dil_bwd(qkv_dil[g], do_groups[g], lse_groups[g], c_groups[g], g) for g in range(3)]
    dproj = jnp.concatenate(
        [dil_b[g][i].astype(BF16) for i in range(3) for g in range(3)]
        + [t.astype(BF16) for t in (dq_sb, dk_sb, dv_sb)] + [dgl], axis=1)
    (g_in,) = _matmul(h, dproj, mode="tn", out_dtypes=(F32,), name="proj_dw", tm=512, tn=IN_COLS // 2)
    mixer_pack = _pack_full_grads({"w_in": g_in, "w_up_dil": g_up_dil, "w_up_sb": g_up_sb, "w_out": g_out}, MIXER_GROUP)
    mixer_sums = _add_halves(mixer_pack, _swap_halves(mixer_pack), core)
    grad_x, dg_mix, mixer_got = _matmul(
        dproj, w["w_in"], mode="nt", out_dtypes=(F32, ("part", F32)), name="proj_dx", tm=ROW_TILE, tk=IN_COLS // 2,
        extras=(x, dx1, norm_mix_g), epilogue=norm_bwd, exchange=mixer_sums)

    small = (dg_mix, dbg, dg_mlp, dg_final, loss_part)
    return grad_x, (mixer_got, mixer_sums), (mlp_got, mlp_sums), small


def kernel(x, norm_mix_g, w_in, b_gate, w_up_dil, w_up_sb, w_out, norm_mlp_g, w_mlp_in, w_mlp_out, norm_final_g, loss_target, m_norm_mix_g, m_w_in, m_b_gate, m_w_up_dil, m_w_up_sb, m_w_out, m_norm_mlp_g, m_w_mlp_in, m_w_mlp_out, m_norm_final_g, v_norm_mix_g, v_w_in, v_b_gate, v_w_up_dil, v_w_up_sb, v_w_out, v_norm_mlp_g, v_w_mlp_in, v_w_mlp_out, v_norm_final_g):
    shards = {"w_in": w_in[0], "w_up_dil": w_up_dil[0], "w_up_sb": w_up_sb[0], "w_out": w_out[0],
              "w_mlp_in": w_mlp_in[0], "w_mlp_out": w_mlp_out[0]}
    moments_m = {"w_in": m_w_in[0], "w_up_dil": m_w_up_dil[0], "w_up_sb": m_w_up_sb[0], "w_out": m_w_out[0],
                 "w_mlp_in": m_w_mlp_in[0], "w_mlp_out": m_w_mlp_out[0]}
    moments_v = {"w_in": v_w_in[0], "w_up_dil": v_w_up_dil[0], "w_up_sb": v_w_up_sb[0], "w_out": v_w_out[0],
                 "w_mlp_in": v_w_mlp_in[0], "w_mlp_out": v_w_mlp_out[0]}

    shards16 = {n: s.astype(BF16) for n, s in shards.items()}
    early_shards = _pack_shards(shards16, EARLY_WEIGHTS)
    late_shards = _pack_shards(shards16, LATE_WEIGHTS)

    core = lax.axis_index("c").astype(jnp.int32).reshape(1)
    chip = (2 * lax.axis_index("x") + lax.axis_index("y")).astype(jnp.int32).reshape(1)
    grad_x, (mixer_got, mixer_sums), (mlp_got, mlp_sums), small = _local_step(
        x[0], loss_target[0], early_shards, late_shards, norm_mix_g, b_gate, norm_mlp_g, norm_final_g, core)

    reduced = _join_halves(_sum_chips(mixer_got, mixer_sums, chip))
    reduced_mlp = _join_halves(_sum_chips(mlp_got, mlp_sums, chip))
    g_shard = {**_unpack_shard(reduced, MIXER_GROUP), **_unpack_shard(reduced_mlp, MLP_GROUP)}

    dg_mix, dbg, dg_mlp, dg_final, loss_part = small
    loss_row = jnp.sum(loss_part, axis=0, keepdims=True)
    small_pack = jnp.concatenate(
        [jnp.sum(dg_mix, axis=0, keepdims=True), jnp.sum(dbg, axis=0, keepdims=True),
         jnp.sum(dg_mlp, axis=0, keepdims=True), jnp.sum(dg_final, axis=0, keepdims=True), loss_row], axis=1)
    n_small = small_pack.shape[1]
    small_sum = _all_reduce_small(small_pack.reshape(n_small // 128, 128)).reshape(1, n_small)
    g_norm_mix = small_sum[:, :D_MODEL]
    g_b_gate = small_sum[:, D_MODEL:3 * D_MODEL]
    g_norm_mlp = small_sum[:, 3 * D_MODEL:4 * D_MODEL]
    g_norm_final = small_sum[:, 4 * D_MODEL:5 * D_MODEL]
    loss = jnp.sum(small_sum[:, 5 * D_MODEL:])

    names = ["norm_mix_g", "w_in", "b_gate", "w_up_dil", "w_up_sb", "w_out", "norm_mlp_g", "w_mlp_in", "w_mlp_out",
             "norm_final_g"]
    grads = dict(g_shard)
    grads.update(norm_mix_g=g_norm_mix, b_gate=g_b_gate, norm_mlp_g=g_norm_mlp, norm_final_g=g_norm_final)
    weights = dict(shards)
    weights.update(norm_mix_g=norm_mix_g, b_gate=b_gate, norm_mlp_g=norm_mlp_g, norm_final_g=norm_final_g.reshape(1, D_MODEL))
    ms = dict(moments_m)
    ms.update(norm_mix_g=m_norm_mix_g, b_gate=m_b_gate, norm_mlp_g=m_norm_mlp_g, norm_final_g=m_norm_final_g.reshape(1, D_MODEL))
    vs = dict(moments_v)
    vs.update(norm_mix_g=v_norm_mix_g, b_gate=v_b_gate, norm_mlp_g=v_norm_mlp_g, norm_final_g=v_norm_final_g.reshape(1, D_MODEL))

    out_shapes = {"norm_mix_g": norm_mix_g.shape, "w_in": w_in.shape, "b_gate": b_gate.shape, "w_up_dil": w_up_dil.shape,
                  "w_up_sb": w_up_sb.shape, "w_out": w_out.shape, "norm_mlp_g": norm_mlp_g.shape,
                  "w_mlp_in": w_mlp_in.shape, "w_mlp_out": w_mlp_out.shape, "norm_final_g": norm_final_g.shape}
    g_out, d_out, m_out, v_out = [], [], [], []
    for n in names:
        d, nm, nv = _adamw(grads[n], weights[n], ms[n], vs[n], "adamw_" + n)
        shape = out_shapes[n]
        g_out.append(grads[n].reshape(shape))
        d_out.append(d.reshape(shape))
        m_out.append(nm.reshape(shape))
        v_out.append(nv.reshape(shape))
    return (loss, grad_x.reshape(x.shape), *g_out, *d_out, *m_out, *v_out)
```

```python
import math

import jax
import jax.numpy as jnp
import numpy as np
from jax import lax
from jax.experimental import pallas as pl
from jax.experimental.pallas import tpu as pltpu

F32 = jnp.float32
BF16 = jnp.bfloat16
MESH = pl.DeviceIdType.MESH

D_MODEL = 1024
HEAD_DIM = 64
DIL_GROUPS = ((128, 1), (512, 4), (2048, 16))
DIL_HEADS = 4
DIL_W = 256
N_DIL_HEADS = 12
SB_HEADS = 8
SB_W = SB_HEADS * HEAD_DIM
QKV_W = 3 * 3 * DIL_W + 3 * SB_W
GATE_W = 2 * D_MODEL
IN_COLS = QKV_W + GATE_W
D_FF = 4 * D_MODEL
BLOCK = 128
RMS_EPS = 1e-6
NEG_INF = -1e30
N_CHIPS = 4
N_DEV = 8

ADAM_LR = 0.001
ADAM_B1 = 0.9
ADAM_B2 = 0.999
ADAM_EPS = 1e-08
ADAM_WD = 0.01
ADAM_STEP = 10

VMEM_LIMIT = 56 * 1024 * 1024

SB_BQ = 256
SB_BK = 256


def _cparams(sem=None):
    if sem is None:
        return pltpu.CompilerParams(vmem_limit_bytes=VMEM_LIMIT)
    return pltpu.CompilerParams(dimension_semantics=sem, vmem_limit_bytes=VMEM_LIMIT)


def _dot(a, b, dims):
    return lax.dot_general(a, b, (dims, ((), ())), preferred_element_type=F32)


def _dot_nn(a, b):
    return _dot(a, b, ((1,), (0,)))


def _dot_nt(a, b):
    return _dot(a, b, ((1,), (1,)))


def _dot_tn(a, b):
    return _dot(a, b, ((0,), (0,)))


def _dot_f32_by_01(x, m01, pieces=3):
    hi = x.astype(BF16)
    if pieces == 1:
        return _dot_nn(hi, m01)
    r1 = x - hi.astype(F32)
    mid = r1.astype(BF16)
    if pieces == 2:
        return _dot_nn(hi, m01) + _dot_nn(mid, m01)
    lo = (r1 - mid.astype(F32)).astype(BF16)
    return _dot_nn(hi, m01) + _dot_nn(mid, m01) + _dot_nn(lo, m01)


def _matmul(a, b, *, mode, out_dtypes, name, tm=1024, tn=1024, tk=1024, extras=(), epilogue=None, exchange=None,
            into=None):
    if mode == "nn":
        (m, k), (k2, n) = a.shape, b.shape
    elif mode == "nt":
        (m, k), (n, k2) = a.shape, b.shape
    else:
        (k, m), (k2, n) = a.shape, b.shape
    assert k == k2, (a.shape, b.shape, mode)
    tm, tn, tk = min(tm, m), min(tn, n), min(tk, k)
    assert m % tm == 0 and n % tn == 0 and k % tk == 0, (m, n, k, tm, tn, tk)
    nk = k // tk
    n_out = len(out_dtypes)
    n_ex = len(extras)

    if mode == "nn":
        a_spec = pl.BlockSpec((tm, tk), lambda i, j, kk: (i, kk))
        b_spec = pl.BlockSpec((tk, tn), lambda i, j, kk: (kk, j))
        dot = _dot_nn
    elif mode == "nt":
        a_spec = pl.BlockSpec((tm, tk), lambda i, j, kk: (i, kk))
        b_spec = pl.BlockSpec((tn, tk), lambda i, j, kk: (j, kk))
        dot = _dot_nt
    else:
        a_spec = pl.BlockSpec((tk, tm), lambda i, j, kk: (kk, i))
        b_spec = pl.BlockSpec((tk, tn), lambda i, j, kk: (kk, j))
        dot = _dot_tn
    mn_spec = pl.BlockSpec((tm, tn), lambda i, j, kk: (i, j))
    row_spec = pl.BlockSpec((1, tn), lambda i, j, kk: (0, j))
    part_spec = pl.BlockSpec((8, tn), lambda i, j, kk: (i, j))
    ex_specs = [row_spec if e.shape[0] == 1 else mn_spec for e in extras]
    is_part = [isinstance(dt, tuple) for dt in out_dtypes]
    out_dts = [dt[1] if p else dt for dt, p in zip(out_dtypes, is_part)]
    out_specs = [part_spec if p else mn_spec for p in is_part]
    out_shapes = [jax.ShapeDtypeStruct((8 * (m // tm), n) if p else (m, n), dt) for dt, p in zip(out_dts, is_part)]

    n_side = 0 if exchange is None else 1
    grid = (m // tm, n // tn, nk)
    prior = []
    if into is not None:
        assert n_out == 1 and not extras and exchange is None
        into_shape, into_map, into_prior = into
        out_specs = [pl.BlockSpec((1, tm, tn), lambda i, j, kk: into_map(i, j))]
        out_shapes = [jax.ShapeDtypeStruct(into_shape, out_dts[0])]
        prior = [] if into_prior is None else [into_prior]

    def body(*refs):
        a_ref, b_ref = refs[0], refs[1]
        ex_refs = refs[2:2 + n_ex]
        n_in = 2 + n_ex + n_side + len(prior)
        out_refs = refs[n_in:n_in + n_out]
        scratch = refs[n_in + n_out + n_side:]
        acc_ref = scratch[0] if nk > 1 else None
        if exchange is not None:
            side = (refs[2 + n_ex], refs[2 + n_ex + n_side + n_out]) + tuple(scratch[-2:])
            step = (pl.program_id(0) * grid[1] + pl.program_id(1)) * grid[2] + pl.program_id(2)

            @pl.when(step == 0)
            def _():
                _exchange_start(*side)

            @pl.when(step == grid[0] * grid[1] * grid[2] - 1)
            def _():
                _exchange_wait(*side)

        part = dot(a_ref[...].astype(BF16), b_ref[...].astype(BF16))

        def finish(acc):
            if epilogue is None:
                outs = (acc,)
            else:
                outs = epilogue(acc, *[r[...] for r in ex_refs])
            for o_ref, o in zip(out_refs, outs):
                if into is None:
                    o_ref[...] = o.astype(o_ref.dtype)
                else:
                    o_ref[0] = o.astype(o_ref.dtype)

        if nk == 1:
            finish(part)
        else:
            kk = pl.program_id(2)

            @pl.when(kk == 0)
            def _():
                acc_ref[...] = part

            @pl.when(kk > 0)
            def _():
                acc_ref[...] += part

            @pl.when(kk == nk - 1)
            def _():
                finish(acc_ref[...])

    side_in = [] if exchange is None else [exchange]
    outs = pl.pallas_call(
        body,
        name=name,
        grid=grid,
        in_specs=[a_spec, b_spec] + ex_specs + [ANY] * (n_side + len(prior)),
        out_specs=out_specs + [ANY] * n_side,
        out_shape=out_shapes + [jax.ShapeDtypeStruct(e.shape, e.dtype) for e in side_in],
        scratch_shapes=([pltpu.VMEM((tm, tn), F32)] if nk > 1 else [])
        + [pltpu.SemaphoreType.DMA((3,)), pltpu.SemaphoreType.DMA((3,))] * n_side,
        input_output_aliases={2: 0} if prior else {},
        compiler_params=_cparams(("arbitrary",) * 3 if n_side else ("parallel", "parallel", "arbitrary")),
    )(a, b, *extras, *side_in, *prior)
    return outs


ROW_TILE = 512


def _rows_sum8(t):
    rows, d = t.shape
    return jnp.sum(t.reshape(rows // 8, 8, d), axis=0)


def _rms_rows(x):
    r = lax.rsqrt(jnp.mean(x * x, axis=-1, keepdims=True) + RMS_EPS)
    return x * r, r


def _rms_bwd_rows(dh, x, g):
    xh, r = _rms_rows(x)
    dxh = dh * g
    return r * (dxh - xh * jnp.mean(dxh * xh, axis=-1, keepdims=True)), _rows_sum8(dh * xh)


def _rms_fwd_and_gather(x, g, shard_pack):
    s, d = x.shape
    r_pack, w_pack = shard_pack.shape
    steps = s // ROW_TILE

    def body(x_ref, g_ref, pack_ref, h_ref, others_ref, send_sems, recv_sems):
        i = pl.program_id(0)
        gather = (pack_ref, others_ref, send_sems, recv_sems)

        @pl.when(i == 0)
        def _():
            _gather_start(*gather)

        h_ref[...] = (_rms_rows(x_ref[...])[0] * g_ref[...]).astype(BF16)

        @pl.when(i == steps - 1)
        def _():
            _gather_pass_on(*gather)
            _gather_finish(*gather)

    h, others = pl.pallas_call(
        body,
        name="norm_mix",
        grid=(steps,),
        in_specs=[pl.BlockSpec((ROW_TILE, d), lambda i: (i, 0)), pl.BlockSpec((1, d), lambda i: (0, 0)), ANY],
        out_specs=[pl.BlockSpec((ROW_TILE, d), lambda i: (i, 0)), ANY],
        out_shape=[jax.ShapeDtypeStruct((s, d), BF16),
                   jax.ShapeDtypeStruct((N_CHIPS, 2, r_pack // 2, w_pack), shard_pack.dtype)],
        scratch_shapes=[pltpu.SemaphoreType.DMA((6,)), pltpu.SemaphoreType.DMA((6,))],
        compiler_params=_cparams(("arbitrary",)),
    )(x, g, shard_pack.reshape(2, r_pack // 2, w_pack))
    return h, _fill_own_slot(others, shard_pack)


def _alibi_slopes():
    return np.exp2(np.float32(-8.0) * np.arange(1, N_DIL_HEADS + 1, dtype=np.float32) / np.float32(N_DIL_HEADS))


def _head_lane_mask(h, rows):
    lane = lax.broadcasted_iota(jnp.int32, (rows, DIL_W), 1)
    return (lane >= h * HEAD_DIM) & (lane < (h + 1) * HEAD_DIM)


def _band_terms(dil, has_prev):
    qi = lax.broadcasted_iota(jnp.int32, (BLOCK, 2 * BLOCK), 0)
    kj = lax.broadcasted_iota(jnp.int32, (BLOCK, 2 * BLOCK), 1)
    steps = qi + BLOCK - kj
    valid = (steps >= 0) & (steps <= BLOCK) & ((kj >= BLOCK) | has_prev)
    return valid, steps.astype(F32) * float(dil)


def _load_halves(ref, rows):
    return jnp.concatenate([ref[0, rows, :], ref[1, rows, :]], axis=1)


def _store_halves(ref, rows, value):
    ref[0, rows, :] = value[:, :128]
    ref[1, rows, :] = value[:, 128:]


def _dil_fwd(qkv_g, group):
    _, dil = DIL_GROUPS[group]
    s = qkv_g.shape[0]
    sub = s // dil
    nb = sub // BLOCK
    view = qkv_g.reshape(sub, dil * 3 * DIL_W)
    slopes = _alibi_slopes()[group * DIL_HEADS:(group + 1) * DIL_HEADS]

    def col(which):
        return lambda n, r: (n, r * 3 + which)

    def col_prev(which):
        return lambda n, r: (jnp.maximum(n - 1, 0), r * 3 + which)

    def body(q_ref, kc_ref, kp_ref, vc_ref, vp_ref, o_ref, lse_ref):
        n, r = pl.program_id(0), pl.program_id(1)
        mine = pl.ds(r, BLOCK, stride=dil) if dil > 1 else slice(None)
        valid, dist = _band_terms(dil, n > 0)
        q = q_ref[...]
        k2 = jnp.concatenate([kp_ref[...], kc_ref[...]], axis=0)
        v2 = jnp.concatenate([vp_ref[...], vc_ref[...]], axis=0)
        masks = [_head_lane_mask(h, BLOCK) for h in range(DIL_HEADS)]
        logits = [_dot_nt(jnp.where(masks[h], q, jnp.zeros_like(q)), k2) for h in range(DIL_HEADS)]
        ps, lses = [], []
        for h in range(DIL_HEADS):
            lg = jnp.where(valid, logits[h] * 0.125 - float(slopes[h]) * dist, NEG_INF)
            mx = jnp.max(lg, axis=1, keepdims=True)
            lse = mx + jnp.log(jnp.sum(jnp.exp(lg - mx), axis=1, keepdims=True))
            ps.append(jnp.exp(lg - lse).astype(BF16))
            lses.append(lse)
        o_acc = jnp.zeros((BLOCK, DIL_W), F32)
        lse_acc = jnp.zeros((BLOCK, DIL_W), F32)
        for h in range(DIL_HEADS):
            o_acc = jnp.where(masks[h], _dot_nn(ps[h], v2), o_acc)
            lse_acc = jnp.where(masks[h], lses[h], lse_acc)
        _store_halves(o_ref, mine, o_acc)
        _store_halves(lse_ref, mine, lse_acc)

    blk = (BLOCK, DIL_W)
    return pl.pallas_call(
        body,
        name=f"dil_fwd_g{group}",
        grid=(nb, dil),
        in_specs=[pl.BlockSpec(blk, col(0)), pl.BlockSpec(blk, col(1)), pl.BlockSpec(blk, col_prev(1)),
                  pl.BlockSpec(blk, col(2)), pl.BlockSpec(blk, col_prev(2))],
        out_specs=[pl.BlockSpec((2, BLOCK * dil, 128), lambda n, r: (0, n, 0))] * 2,
        out_shape=[jax.ShapeDtypeStruct((2, s, 128), F32)] * 2,
        compiler_params=_cparams(("parallel", "arbitrary")),
    )(view, view, view, view, view)


def _dil_bwd(qkv, do, lse, cterm, group):
    _, dil = DIL_GROUPS[group]
    s = qkv.shape[0]
    sub = s // dil
    nb = sub // BLOCK
    view = qkv.reshape(sub, dil * 3 * DIL_W)
    slopes = _alibi_slopes()[group * DIL_HEADS:(group + 1) * DIL_HEADS]

    def col(which, shift):
        if shift == 0:
            return lambda n, r: (n, r * 3 + which)
        if shift < 0:
            return lambda n, r: (jnp.maximum(n - 1, 0), r * 3 + which)
        return lambda n, r: (jnp.minimum(n + 1, nb - 1), r * 3 + which)

    def own(shift):
        if shift == 0:
            return pl.BlockSpec((2, BLOCK * dil, 128), lambda n, r: (0, n, 0))
        return pl.BlockSpec((2, BLOCK * dil, 128), lambda n, r: (0, jnp.minimum(n + 1, nb - 1), 0))

    def body(q_ref, qn_ref, kc_ref, kp_ref, vc_ref, vp_ref, do_ref, don_ref, lse_ref, lsen_ref, c_ref, cn_ref,
             dq_ref, dk_ref, dv_ref):
        n, r = pl.program_id(0), pl.program_id(1)
        mine = pl.ds(r, BLOCK, stride=dil) if dil > 1 else slice(None)
        valid, dist = _band_terms(dil, n > 0)
        valid_n = _band_terms(dil, True)[0][:, :BLOCK] & (n < nb - 1)
        dist_n = dist[:, :BLOCK]
        q, qn = q_ref[...], qn_ref[...]
        kc, vc = kc_ref[...], vc_ref[...]
        k2 = jnp.concatenate([kp_ref[...], kc], axis=0)
        v2 = jnp.concatenate([vp_ref[...], vc], axis=0)
        dov, donv = _load_halves(do_ref, mine), _load_halves(don_ref, mine)
        lsev, lsenv = _load_halves(lse_ref, mine), _load_halves(lsen_ref, mine)
        cv, cnv = _load_halves(c_ref, mine), _load_halves(cn_ref, mine)
        masks = [_head_lane_mask(h, BLOCK) for h in range(DIL_HEADS)]

        def head_col(t, hm):
            return jnp.max(jnp.where(hm, t, NEG_INF), axis=1, keepdims=True)

        qhs = [jnp.where(hm, q, jnp.zeros_like(q)) for hm in masks]
        qnhs = [jnp.where(hm, qn, jnp.zeros_like(qn)) for hm in masks]
        dohs = [jnp.where(hm, dov, 0.0).astype(BF16) for hm in masks]
        donhs = [jnp.where(hm, donv, 0.0).astype(BF16) for hm in masks]
        logit = [_dot_nt(qhs[h], k2) for h in range(DIL_HEADS)]
        dp = [_dot_nt(dohs[h], v2) for h in range(DIL_HEADS)]
        logit_n = [_dot_nt(qnhs[h], kc) for h in range(DIL_HEADS)]
        dp_n = [_dot_nt(donhs[h], vc) for h in range(DIL_HEADS)]
        p16, dlog, pn16, dlog_n = [], [], [], []
        for h in range(DIL_HEADS):
            hm, slope = masks[h], float(slopes[h])
            p = jnp.where(valid, jnp.exp(logit[h] * 0.125 - slope * dist - head_col(lsev, hm)), 0.0)
            dlog.append((p * (dp[h] + head_col(cv, hm)) * 0.125).astype(BF16))
            p16.append(p.astype(BF16))
            pn = jnp.where(valid_n, jnp.exp(logit_n[h] * 0.125 - slope * dist_n - head_col(lsenv, hm)), 0.0)
            dlog_n.append((pn * (dp_n[h] + head_col(cnv, hm)) * 0.125).astype(BF16))
            pn16.append(pn.astype(BF16))
        dq_acc = jnp.zeros((BLOCK, DIL_W), F32)
        dk_acc = jnp.zeros((BLOCK, DIL_W), F32)
        dv_acc = jnp.zeros((BLOCK, DIL_W), F32)
        for h in range(DIL_HEADS):
            dq_acc = jnp.where(masks[h], _dot_nn(dlog[h], k2), dq_acc)
            dk_acc += _dot_tn(dlog[h][:, BLOCK:], qhs[h]) + _dot_tn(dlog_n[h], qnhs[h])
            dv_acc += _dot_tn(p16[h][:, BLOCK:], dohs[h]) + _dot_tn(pn16[h], donhs[h])
        dq_ref[...] = dq_acc.astype(BF16)
        dk_ref[...] = dk_acc.astype(BF16)
        dv_ref[...] = dv_acc.astype(BF16)

    blk = (BLOCK, DIL_W)
    outs = pl.pallas_call(
        body,
        name=f"dil_bwd_g{group}",
        grid=(nb, dil),
        in_specs=[pl.BlockSpec(blk, col(0, 0)), pl.BlockSpec(blk, col(0, 1)),
                  pl.BlockSpec(blk, col(1, 0)), pl.BlockSpec(blk, col(1, -1)),
                  pl.BlockSpec(blk, col(2, 0)), pl.BlockSpec(blk, col(2, -1)),
                  own(0), own(1), own(0), own(1), own(0), own(1)],
        out_specs=[pl.BlockSpec(blk, lambda n, r: (n, r))] * 3,
        out_shape=[jax.ShapeDtypeStruct((sub, dil * DIL_W), BF16)] * 3,
        compiler_params=_cparams(("parallel", "parallel")),
    )(view, view, view, view, view, view, do, do, lse, lse, cterm, cterm)
    return tuple(t.reshape(s, DIL_W) for t in outs)


SB_PAIRS = SB_HEADS // 2
SB_COL0 = 0
LOG2E = 1.4426950408889634


SB_EXP_CLAMP = 64.0


def _sb_softplus2(zs):
    t = 1.0 + jnp.exp2(jnp.minimum(zs, SB_EXP_CLAMP))
    return jnp.maximum(jnp.log(t) * LOG2E, zs)


def _sb_consts(nkb):
    row = lax.broadcasted_iota(jnp.int32, (SB_BQ, SB_BK), 0)
    colk = lax.broadcasted_iota(jnp.int32, (SB_BQ, SB_BK), 1)
    rr = lax.broadcasted_iota(jnp.int32, (SB_BK, SB_BK), 0)
    cc = lax.broadcasted_iota(jnp.int32, (SB_BK, SB_BK), 1)
    lane = lax.broadcasted_iota(jnp.int32, (SB_BQ, 128), 1)
    assert 2 * nkb <= 128
    return colk < row, rr, cc, lane < HEAD_DIM, lane


def _split_heads(t):
    first = lax.broadcasted_iota(jnp.int32, t.shape, 1) < HEAD_DIM
    zero = jnp.zeros_like(t)
    return jnp.where(first, t, zero), jnp.where(first, zero, t)


def _sb_fwd(qkv, shard_pack):
    s = qkv.shape[0]
    nq, nkb = s // SB_BQ, s // SB_BK
    zscale = LOG2E / math.sqrt(HEAD_DIM)
    r_pack, w_pack = shard_pack.shape

    def body(q_ref, k_ref, v_ref, pack_ref, o_ref, a_row, others_ref, zs_scr, a_scr, acc_scr, cl_scr,
             send_sems, recv_sems):
        i = pl.program_id(1)
        pair = pl.program_id(0)
        gather = (pack_ref, others_ref, send_sems, recv_sems)

        @pl.when((pair == 0) & (i == 0))
        def _():
            _gather_start(*gather)

        @pl.when((pair == 1) & (i == 0))
        def _():
            _gather_pass_on(*gather)

        @pl.when((pair == SB_PAIRS - 1) & (i == nq - 1))
        def _():
            _gather_finish(*gather)

        causal, rr, cc, _, _ = _sb_consts(nkb)
        later = (rr > cc).astype(BF16)
        qh = _split_heads(q_ref[...])

        def rows(j):
            return pl.ds(pl.multiple_of(j * SB_BK, SB_BK), SB_BK)

        def scores_to(slot, j):
            kb = k_ref[rows(j), :]
            for hh in range(2):
                zs_scr[slot, hh] = _dot_nt(qh[hh], kb) * zscale

        def weights(slot, j, masked):
            xs, sums, sufs = [], [], []
            for hh in range(2):
                zs = zs_scr[slot, hh]
                sp = _sb_softplus2(zs)
                if masked:
                    sp = jnp.where(causal, sp, 0.0)
                xs.append(zs - sp)
                sums.append(jnp.sum(sp, axis=1, keepdims=True))
                sufs.append(_dot_f32_by_01(sp, later, 2))
            for hh in range(2):
                cl = cl_scr[hh]
                a = jnp.exp2(xs[hh] - (sufs[hh] + jnp.concatenate([cl, cl], axis=1)))
                if masked:
                    a = jnp.where(causal, a, 0.0)
                a16 = a.astype(BF16)
                a_scr[slot, :, hh * SB_BK:(hh + 1) * SB_BK] = a16
                a_row[0, 0, j, :, hh * SB_BK:(hh + 1) * SB_BK] = a16
                cl_scr[hh] = cl + sums[hh]

        def add_av(slot, j):
            v0, v1 = _split_heads(v_ref[rows(j), :])
            acc_scr[...] += _dot_nn(a_scr[slot], jnp.concatenate([v0, v1], axis=0))

        acc_scr[...] = jnp.zeros_like(acc_scr)
        cl_scr[...] = jnp.zeros_like(cl_scr)
        scores_to(0, i)
        scores_to(1, jnp.maximum(i - 1, 0))
        weights(0, i, True)

        def step(j, prev, cur):
            scores_to(prev, jnp.maximum(j - 1, 0))
            add_av(prev, j + 1)
            weights(cur, j, False)

        def two_steps(u, _):
            j = i - 1 - 2 * u
            step(j, 0, 1)
            step(j - 1, 1, 0)
            return 0

        lax.fori_loop(0, i // 2, two_steps, 0)

        @pl.when(i % 2 == 1)
        def _():
            step(0, 0, 1)
            add_av(1, 0)

        @pl.when(i % 2 == 0)
        def _():
            add_av(0, 0)

        o_ref[...] = acc_scr[...]

    def full(which):
        return pl.BlockSpec((s, 128), lambda p, i: (0, SB_COL0 + 4 * which + p))

    return pl.pallas_call(
        body,
        name="sb_fwd",
        grid=(SB_PAIRS, nq),
        in_specs=[pl.BlockSpec((SB_BQ, 128), lambda p, i: (i, SB_COL0 + p)), full(1), full(2), ANY],
        out_specs=[pl.BlockSpec((SB_BQ, 128), lambda p, i: (i, p)),
                   pl.BlockSpec((1, 1, nkb, SB_BQ, 2 * SB_BK), lambda p, i: (p, i, 0, 0, 0)), ANY],
        out_shape=[jax.ShapeDtypeStruct((s, SB_W), F32),
                   jax.ShapeDtypeStruct((SB_PAIRS, nq, nkb, SB_BQ, 2 * SB_BK), BF16),
                   jax.ShapeDtypeStruct((N_CHIPS, 2, r_pack // 2, w_pack), shard_pack.dtype)],
        scratch_shapes=[pltpu.VMEM((2, 2, SB_BQ, SB_BK), F32), pltpu.VMEM((2, SB_BQ, 2 * SB_BK), BF16),
                        pltpu.VMEM((SB_BQ, 128), F32), pltpu.VMEM((2, SB_BQ, 128), F32),
                        pltpu.SemaphoreType.DMA((6,)), pltpu.SemaphoreType.DMA((6,))],
        compiler_params=_cparams(("arbitrary", "arbitrary")),
    )(qkv, qkv, qkv, shard_pack.reshape(2, r_pack // 2, w_pack))


def _sb_bwd(qkv, do, a_hbm, chip_sums):
    s = qkv.shape[0]
    nq, nkb = s // SB_BQ, s // SB_BK
    scale = 1.0 / math.sqrt(HEAD_DIM)
    zscale = LOG2E * scale

    def body(q_ref, k_ref, v_ref, do_ref, a_row, sums_ref, dq_ref, dk_ref, dv_ref, got_ref,
             zs_scr, da_scr, dz_scr, a_scr, cg_scr, send_sems, recv_sems):
        i = pl.program_id(1)
        pair = pl.program_id(0)
        first_step = (pair == 0) & (i == 0)
        last_step = (pair == SB_PAIRS - 1) & (i == nq - 1)

        @pl.when(first_step)
        def _():
            _exchange_start(sums_ref, got_ref, send_sems, recv_sems)

        @pl.when(i == 0)
        def _():
            dk_ref[...] = jnp.zeros_like(dk_ref)
            dv_ref[...] = jnp.zeros_like(dv_ref)

        causal, rr, cc, first, _ = _sb_consts(nkb)
        earlier = (rr < cc).astype(BF16)
        q2 = q_ref[...]
        qh = _split_heads(q2)
        do2 = do_ref[...].astype(BF16)
        doh = _split_heads(do2)

        def rows(j):
            return pl.ds(pl.multiple_of(j * SB_BK, SB_BK), SB_BK)

        def products_to(slot, j):
            kb, vb = k_ref[rows(j), :], v_ref[rows(j), :]
            for hh in range(2):
                zs_scr[slot, hh] = _dot_nt(qh[hh], kb) * (-zscale)
                da_scr[slot, hh] = _dot_nt(doh[hh], vb)

        head0_rows = lax.broadcasted_iota(jnp.int32, (128, SB_BK), 0) < HEAD_DIM

        def by_head(t):
            return jnp.where(head0_rows, t[:, :SB_BK], t[:, SB_BK:])

        def apply(slot, j):
            k0, k1 = _split_heads(k_ref[rows(j), :])
            dq_ref[...] += _dot_nn(dz_scr[slot], jnp.concatenate([k0, k1], axis=0)) * scale
            dk_ref[0, j] += by_head(_dot_tn(q2, dz_scr[slot])) * scale
            dv_ref[0, j] += by_head(_dot_tn(do2, a_scr[slot]))

        def grads(slot, j, masked):
            gs, gpres = [], []
            for hh in range(2):
                a16 = a_row[0, 0, j, :, hh * SB_BK:(hh + 1) * SB_BK]
                a_scr[slot, :, hh * SB_BK:(hh + 1) * SB_BK] = a16
                g = a16.astype(F32) * da_scr[slot, hh]
                gs.append(g)
                gpres.append(_dot_f32_by_01(g, earlier, 1))
            sigs = []
            for hh in range(2):
                e = jnp.exp2(jnp.minimum(zs_scr[slot, hh], SB_EXP_CLAMP))
                sigs.append(pl.reciprocal(1.0 + e, approx=True))
            for hh in range(2):
                cg = cg_scr[hh]
                dz = gs[hh] - (gs[hh] + (gpres[hh] + jnp.concatenate([cg, cg], axis=1))) * sigs[hh]
                if masked:
                    dz = jnp.where(causal, dz, 0.0)
                dz_scr[slot, :, hh * SB_BK:(hh + 1) * SB_BK] = dz.astype(BF16)
                cg_scr[hh] = cg + jnp.sum(gs[hh], axis=1, keepdims=True)

        dq_ref[...] = jnp.zeros_like(dq_ref)
        cg_scr[...] = jnp.zeros_like(cg_scr)
        dz_scr[1] = jnp.zeros((SB_BQ, 2 * SB_BK), BF16)
        a_scr[1] = jnp.zeros((SB_BQ, 2 * SB_BK), BF16)
        products_to(0, 0)

        def step(j, cur, nxt):
            products_to(nxt, j + 1)
            apply(nxt, jnp.maximum(j - 1, 0))
            grads(cur, j, False)

        def two_steps(u, _):
            step(2 * u, 0, 1)
            step(2 * u + 1, 1, 0)
            return 0

        lax.fori_loop(0, i // 2, two_steps, 0)

        def last(cur, nxt):
            apply(nxt, jnp.maximum(i - 1, 0))
            grads(cur, i, True)
            apply(cur, i)

        @pl.when(i % 2 == 1)
        def _():
            step(i - 1, 0, 1)
            last(1, 0)

        @pl.when(i % 2 == 0)
        def _():
            last(0, 1)

        @pl.when(last_step)
        def _():
            _exchange_wait(sums_ref, got_ref, send_sems, recv_sems)

    def full(which):
        return pl.BlockSpec((s, 128), lambda p, i: (0, SB_COL0 + 4 * which + p))

    qblk = pl.BlockSpec((SB_BQ, 128), lambda p, i: (i, p))
    acc = pl.BlockSpec((1, nkb, 128, SB_BK), lambda p, i: (p, 0, 0, 0))
    acc_shape = jax.ShapeDtypeStruct((SB_PAIRS, nkb, 128, SB_BK), F32)
    dq, dk_t, dv_t, got = pl.pallas_call(
        body,
        name="sb_bwd",
        grid=(SB_PAIRS, nq),
        in_specs=[pl.BlockSpec((SB_BQ, 128), lambda p, i: (i, SB_COL0 + p)), full(1), full(2), qblk,
                  pl.BlockSpec((1, 1, nkb, SB_BQ, 2 * SB_BK), lambda p, i: (p, i, 0, 0, 0)), ANY],
        out_specs=[qblk, acc, acc, ANY],
        out_shape=[jax.ShapeDtypeStruct((s, SB_W), F32), acc_shape, acc_shape,
                   jax.ShapeDtypeStruct(chip_sums.shape, chip_sums.dtype)],
        scratch_shapes=[pltpu.VMEM((2, 2, SB_BQ, SB_BK), F32), pltpu.VMEM((2, 2, SB_BQ, SB_BK), F32),
                        pltpu.VMEM((2, SB_BQ, 2 * SB_BK), BF16), pltpu.VMEM((2, SB_BQ, 2 * SB_BK), BF16),
                        pltpu.VMEM((2, SB_BQ, 128), F32),
                        pltpu.SemaphoreType.DMA((3,)), pltpu.SemaphoreType.DMA((3,))],
        compiler_params=_cparams(("arbitrary", "arbitrary")),
    )(qkv, qkv, qkv, do, a_hbm, chip_sums)

    def untranspose(t):
        return jnp.transpose(t, (1, 3, 0, 2)).reshape(s, SB_W)

    return dq, untranspose(dk_t), untranspose(dv_t), got


MERGE_TILE = 256


def _group_mix(lses):
    mx = jnp.maximum(jnp.maximum(lses[0], lses[1]), lses[2])
    es = [jnp.exp(t - mx) for t in lses]
    den = es[0] + es[1] + es[2]
    return [e / den for e in es]


def _merge_fwd(o_groups, lse_groups, o_sb, gl, b_gate, w_up_dil, w_up_sb):
    s = gl.shape[0]
    t = MERGE_TILE

    def body(o0, o1, o2, l0, l1, l2, ob_ref, gl_ref, bg_ref, wd_ref, ws_ref, merged_ref, oa_ref):
        rows = slice(None)
        w = _group_mix([_load_halves(l, rows) for l in (l0, l1, l2)])
        og = [_load_halves(o, rows) for o in (o0, o1, o2)]
        oa = (w[0] * og[0] + w[1] * og[1] + w[2] * og[2]).astype(BF16)
        ua = _dot_nn(oa, wd_ref[...])
        ub = _dot_nn(ob_ref[...].astype(BF16), ws_ref[...])
        gate = jax.nn.sigmoid(gl_ref[...] + bg_ref[...])
        merged_ref[...] = (gate[:, :D_MODEL] * ua + gate[:, D_MODEL:] * ub).astype(BF16)
        oa_ref[...] = oa

    dil = pl.BlockSpec((t, DIL_W), lambda i: (i, 0))
    halves = pl.BlockSpec((2, t, 128), lambda i: (0, i, 0))
    const = lambda shape: pl.BlockSpec(shape, lambda i: (0, 0))
    return pl.pallas_call(
        body,
        name="merge_fwd",
        grid=(s // t,),
        in_specs=[halves] * 6 + [pl.BlockSpec((t, SB_W), lambda i: (i, 0)), pl.BlockSpec((t, GATE_W), lambda i: (i, 0)),
                                 const((1, GATE_W)), const((DIL_W, D_MODEL)), const((SB_W, D_MODEL))],
        out_specs=[pl.BlockSpec((t, D_MODEL), lambda i: (i, 0)), dil],
        out_shape=[jax.ShapeDtypeStruct((s, D_MODEL), BF16), jax.ShapeDtypeStruct((s, DIL_W), BF16)],
        compiler_params=_cparams(("parallel",)),
    )(*o_groups, *lse_groups, o_sb, gl, b_gate, w_up_dil, w_up_sb)


def _merge_bwd(dmerged, o_groups, lse_groups, o_sb, gl, b_gate, w_up_dil, w_up_sb, swap):
    s = gl.shape[0]
    t = MERGE_TILE
    n_chunks, r_swap, w_swap = swap.shape
    swap = swap.reshape(n_chunks, 2, r_swap // 2, w_swap)

    def body(dm_ref, o0, o1, o2, l0, l1, l2, ob_ref, gl_ref, bg_ref, wd_ref, ws_ref, swap_ref,
             dua_ref, dub_ref, dgl_ref, dbg_ref, dosb_ref, d0, d1, d2, c0, c1, c2, got_ref, send_sem, recv_sem):
        i = pl.program_id(0)

        @pl.when(i == 0)
        def _():
            _swap_copy(swap_ref, got_ref, send_sem, recv_sem).start()

        @pl.when(i == pl.num_programs(0) - 1)
        def _():
            _swap_copy(swap_ref, got_ref, send_sem, recv_sem).wait()

        rows = slice(None)
        og = [_load_halves(o, rows) for o in (o0, o1, o2)]
        w = _group_mix([_load_halves(l, rows) for l in (l0, l1, l2)])
        oa = (w[0] * og[0] + w[1] * og[1] + w[2] * og[2]).astype(BF16)
        ua = _dot_nn(oa, wd_ref[...])
        ub = _dot_nn(ob_ref[...].astype(BF16), ws_ref[...])
        gate = jax.nn.sigmoid(gl_ref[...] + bg_ref[...])
        ga, gb = gate[:, :D_MODEL], gate[:, D_MODEL:]
        dm = dm_ref[...]
        dua = (dm * ga).astype(BF16)
        dub = (dm * gb).astype(BF16)
        dua_ref[...] = dua
        dub_ref[...] = dub
        dgl_a = dm * ua * ga * (1.0 - ga)
        dgl_b = dm * ub * gb * (1.0 - gb)
        dgl_ref[:, :D_MODEL] = dgl_a.astype(BF16)
        dgl_ref[:, D_MODEL:] = dgl_b.astype(BF16)
        part = jnp.concatenate([jnp.sum(dgl_a.reshape(t // 8, 8, D_MODEL), axis=0),
                                jnp.sum(dgl_b.reshape(t // 8, 8, D_MODEL), axis=0)], axis=1)

        @pl.when(i == 0)
        def _():
            dbg_ref[...] = part

        @pl.when(i > 0)
        def _():
            dbg_ref[...] += part

        dosb_ref[...] = _dot_nt(dub, ws_ref[...])
        doa = _dot_nt(dua, wd_ref[...])
        rr = lax.broadcasted_iota(jnp.int32, (DIL_W, DIL_W), 0) // HEAD_DIM
        cc = lax.broadcasted_iota(jnp.int32, (DIL_W, DIL_W), 1) // HEAD_DIM
        same_head = (rr == cc).astype(BF16)
        dw = [_dot_f32_by_01(doa * og[g], same_head) for g in range(3)]
        mean_dw = w[0] * dw[0] + w[1] * dw[1] + w[2] * dw[2]
        for g, (d_ref, c_ref) in enumerate(((d0, c0), (d1, c1), (d2, c2))):
            _store_halves(d_ref, rows, w[g] * doa)
            _store_halves(c_ref, rows, -w[g] * mean_dw)

    dil = pl.BlockSpec((2, t, 128), lambda i: (0, i, 0))
    wide = pl.BlockSpec((t, D_MODEL), lambda i: (i, 0))
    gate2 = pl.BlockSpec((t, GATE_W), lambda i: (i, 0))
    sbw = pl.BlockSpec((t, SB_W), lambda i: (i, 0))
    const = lambda shape: pl.BlockSpec(shape, lambda i: (0, 0))
    return pl.pallas_call(
        body,
        name="merge_bwd",
        grid=(s // t,),
        in_specs=[wide] + [dil] * 6 + [sbw, gate2, const((1, GATE_W)), const((DIL_W, D_MODEL)), const((SB_W, D_MODEL)),
                                       ANY],
        out_specs=[wide, wide, gate2, const((8, GATE_W)), sbw] + [dil] * 6 + [ANY],
        out_shape=[jax.ShapeDtypeStruct((s, D_MODEL), BF16), jax.ShapeDtypeStruct((s, D_MODEL), BF16),
                   jax.ShapeDtypeStruct((s, GATE_W), BF16), jax.ShapeDtypeStruct((8, GATE_W), F32),
                   jax.ShapeDtypeStruct((s, SB_W), F32)] + [jax.ShapeDtypeStruct((2, s, 128), F32)] * 6
        + [jax.ShapeDtypeStruct((n_chunks, r_swap // 2, w_swap), swap.dtype)],
        scratch_shapes=[pltpu.SemaphoreType.DMA, pltpu.SemaphoreType.DMA],
        compiler_params=_cparams(("arbitrary",)),
    )(dmerged, *o_groups, *lse_groups, o_sb, gl, b_gate, w_up_dil, w_up_sb, swap)


ANY = pl.BlockSpec(memory_space=pl.ANY)


def _place():
    x, y, c = lax.axis_index("x"), lax.axis_index("y"), lax.axis_index("c")
    other_chips = [(1 - x, y), (x, 1 - y), (1 - x, 1 - y)]
    return x, y, c, other_chips


def _gather_copies(p_ref, out_ref, send_sems, recv_sems):
    x, y, c, chips = _place()
    me, sibling = 2 * x + y, (x, y, 1 - c)
    idx = [2 * chip[0] + chip[1] for chip in chips]

    def copy(k, chip_idx, core, to, src=None):
        return pltpu.make_async_remote_copy(
            src_ref=out_ref.at[chip_idx, core] if src is None else src, dst_ref=out_ref.at[chip_idx, core],
            send_sem=send_sems.at[k], recv_sem=recv_sems.at[k], device_id=to, device_id_type=MESH)

    first = lambda j: copy(j, me, c, (*chips[j], c), src=p_ref.at[c])
    landed = lambda j: copy(j, idx[j], c, (x, y, c))
    passed = lambda j: copy(3 + j, idx[j], c, sibling)
    handed = lambda j: copy(3 + j, idx[j], 1 - c, (x, y, c))
    return first, landed, passed, handed


def _gather_start(*refs):
    first = _gather_copies(*refs)[0]
    for j in range(3):
        first(j).start()


def _gather_pass_on(*refs):
    _, landed, passed, _ = _gather_copies(*refs)
    for j in range(3):
        landed(j).wait_recv()
        passed(j).start()


def _gather_finish(*refs):
    first, _, passed, handed = _gather_copies(*refs)
    for j in range(3):
        handed(j).wait_recv()
    for j in range(3):
        first(j).wait_send()
        passed(j).wait_send()


def _fill_own_slot(others, pack):
    n, _, rh, wd = others.shape
    me = 2 * lax.axis_index("x") + lax.axis_index("y")
    mine = lax.broadcasted_iota(jnp.int32, (n, 1, 1, 1), 0) == me
    return jnp.where(mine, pack.reshape(1, 2, rh, wd), others).reshape(n, 2 * rh, wd)


def _swap_copy(g_ref, out_ref, send_sem, recv_sem):
    x, y, c, _ = _place()
    return pltpu.make_async_remote_copy(
        src_ref=g_ref.at[:, 1 - c], dst_ref=out_ref,
        send_sem=send_sem, recv_sem=recv_sem, device_id=(x, y, 1 - c), device_id_type=MESH)


def _swap_halves(g):
    n, r, wd = g.shape
    rh = r // 2
    g = g.reshape(n, 2, rh, wd)

    def body(g_ref, out_ref, send_sem, recv_sem):
        cp = _swap_copy(g_ref, out_ref, send_sem, recv_sem)
        cp.start()
        cp.wait()

    return pl.pallas_call(
        body,
        name="grad_swap_halves",
        in_specs=[ANY],
        out_specs=ANY,
        out_shape=jax.ShapeDtypeStruct((n, rh, wd), g.dtype),
        scratch_shapes=[pltpu.SemaphoreType.DMA, pltpu.SemaphoreType.DMA],
    )(g)


def _add_halves(g, got, core):
    n, r, wd = g.shape
    rh = r // 2
    t = rh // 4
    nt = rh // t

    def body(c_ref, a_ref, b_ref, o_ref):
        o_ref[...] = (a_ref[0] + b_ref[...]).astype(BF16)

    grid_spec = pltpu.PrefetchScalarGridSpec(
        num_scalar_prefetch=1,
        grid=(n, nt),
        in_specs=[pl.BlockSpec((1, 1, t, wd), lambda s, i, c: (s, c[0], i, 0)),
                  pl.BlockSpec((1, t, wd), lambda s, i, c: (s, i, 0))],
        out_specs=pl.BlockSpec((1, t, wd), lambda s, i, c: (s, i, 0)),
    )
    return pl.pallas_call(
        body,
        name="grad_add_halves",
        grid_spec=grid_spec,
        out_shape=jax.ShapeDtypeStruct((n, rh, wd), BF16),
        compiler_params=_cparams(("parallel", "parallel")),
    )(core, g.reshape(n, 2, rh, wd), got)


def _exchange_copies(h_ref, out_ref, send_sems, recv_sems):
    x, y, c, chips = _place()
    me = 2 * x + y

    def copy(j, slot):
        them = 2 * chips[j][0] + chips[j][1]
        return pltpu.make_async_remote_copy(
            src_ref=h_ref.at[them], dst_ref=out_ref.at[me if slot == "mine" else them],
            send_sem=send_sems.at[j], recv_sem=recv_sems.at[j], device_id=(*chips[j], c), device_id_type=MESH)

    return (lambda j: copy(j, "mine")), (lambda j: copy(j, "theirs"))


def _exchange_start(h_ref, out_ref, send_sems, recv_sems):
    send = _exchange_copies(h_ref, out_ref, send_sems, recv_sems)[0]
    for j in range(3):
        send(j).start()


def _exchange_wait(h_ref, out_ref, send_sems, recv_sems):
    send, arrival = _exchange_copies(h_ref, out_ref, send_sems, recv_sems)
    for j in range(3):
        arrival(j).wait_recv()
    for j in range(3):
        send(j).wait_send()


def _sum_chips(b, h, chip):
    n, rh, wd = b.shape
    t = rh // 4

    def body(chip_ref, b_ref, own_ref, o_ref):
        own = own_ref[0]
        s0, s1, s2, s3 = (jnp.where(chip_ref[0] == k, own, b_ref[k]).astype(F32) for k in range(n))
        o_ref[...] = ((s0 + s1) + s2) + s3

    grid_spec = pltpu.PrefetchScalarGridSpec(
        num_scalar_prefetch=1,
        grid=(rh // t,),
        in_specs=[pl.BlockSpec((n, t, wd), lambda i, chip: (0, i, 0)),
                  pl.BlockSpec((1, t, wd), lambda i, chip: (chip[0], i, 0))],
        out_specs=pl.BlockSpec((t, wd), lambda i, chip: (i, 0)),
    )
    return pl.pallas_call(
        body,
        name="grad_sum_chips",
        grid_spec=grid_spec,
        out_shape=jax.ShapeDtypeStruct((rh, wd), F32),
        compiler_params=_cparams(("parallel",)),
    )(chip, b, h)


def _join_halves(tc):
    rh, wd = tc.shape

    def body(t_ref, out_ref, send_sem, recv_sem):
        x, y, c, _ = _place()
        cp = pltpu.make_async_remote_copy(
            src_ref=t_ref, dst_ref=out_ref.at[c],
            send_sem=send_sem, recv_sem=recv_sem, device_id=(x, y, 1 - c), device_id_type=MESH)
        cp.start()
        cp.wait()

    halves = pl.pallas_call(
        body,
        name="grad_join_halves",
        in_specs=[ANY],
        out_specs=ANY,
        out_shape=jax.ShapeDtypeStruct((2, rh, wd), tc.dtype),
        scratch_shapes=[pltpu.SemaphoreType.DMA, pltpu.SemaphoreType.DMA],
    )(tc)
    return lax.dynamic_update_slice(halves, tc[None], (lax.axis_index("c"), 0, 0)).reshape(2 * rh, wd)


def _all_reduce_small(pack):
    rows, lanes = pack.shape

    def body(p_ref, out_ref, buf, send_sems, recv_sems):
        x, y, c, _ = _place()
        me = 4 * x + 2 * y + c
        buf[me] = p_ref[...]
        sends = []
        for k in range(1, N_DEV):
            peer = (x ^ (k >> 2), y ^ ((k >> 1) & 1), c ^ (k & 1))
            sends.append(pltpu.make_async_remote_copy(
                src_ref=p_ref, dst_ref=buf.at[me], send_sem=send_sems.at[k - 1], recv_sem=recv_sems.at[k - 1],
                device_id=peer, device_id_type=MESH))
        for cp in sends:
            cp.start()
        for k in range(1, N_DEV):
            pltpu.make_async_remote_copy(
                src_ref=p_ref, dst_ref=buf.at[me ^ k], send_sem=send_sems.at[k - 1], recv_sem=recv_sems.at[k - 1],
                device_id=(x, y, c), device_id_type=MESH).wait_recv()
        for cp in sends:
            cp.wait_send()
        total = buf[0]
        for d in range(1, N_DEV):
            total = total + buf[d]
        out_ref[...] = total

    vm = pl.BlockSpec(memory_space=pltpu.VMEM)
    return pl.pallas_call(
        body,
        name="all_reduce_small",
        in_specs=[vm],
        out_specs=vm,
        out_shape=jax.ShapeDtypeStruct((rows, lanes), F32),
        scratch_shapes=[pltpu.VMEM((N_DEV, rows, lanes), F32), pltpu.SemaphoreType.DMA((N_DEV - 1,)),
                        pltpu.SemaphoreType.DMA((N_DEV - 1,))],
    )(pack)


def _adamw(g, w, m, v, name):
    rows, cols = g.shape
    t = rows
    for cand in (256, 128, 64, 32, 16, 8):
        if rows % cand == 0:
            t = cand
            break

    def body(g_ref, w_ref, m_ref, v_ref, d_ref, nm_ref, nv_ref):
        gv = g_ref[...]
        mv = ADAM_B1 * m_ref[...] + (1.0 - ADAM_B1) * gv
        vv = ADAM_B2 * v_ref[...] + (1.0 - ADAM_B2) * (gv * gv)
        m_hat = mv / (1.0 - ADAM_B1 ** ADAM_STEP)
        v_hat = vv / (1.0 - ADAM_B2 ** ADAM_STEP)
        d_ref[...] = -ADAM_LR * (m_hat / (jnp.sqrt(v_hat) + ADAM_EPS) + ADAM_WD * w_ref[...])
        nm_ref[...] = mv
        nv_ref[...] = vv

    blk = pl.BlockSpec((t, cols), lambda i: (i, 0))
    return pl.pallas_call(
        body,
        name=name,
        grid=(rows // t,),
        in_specs=[blk] * 4,
        out_specs=[blk] * 3,
        out_shape=[jax.ShapeDtypeStruct((rows, cols), F32)] * 3,
        compiler_params=_cparams(("parallel",)),
    )(g, w, m, v)


PACK_W = 1024
BIG = (("w_in", (D_MODEL, IN_COLS), 1), ("w_up_dil", (DIL_W, D_MODEL), 1), ("w_up_sb", (SB_W, D_MODEL), 1),
       ("w_out", (D_MODEL, D_MODEL), 0), ("w_mlp_in", (D_MODEL, D_FF), 1), ("w_mlp_out", (D_FF, D_MODEL), 0))


def _shard_shape(shape, axis):
    return tuple(d // N_CHIPS if a == axis else d for a, d in enumerate(shape))


MIXER_GROUP, MLP_GROUP = BIG[:4], BIG[4:]
LATE_WEIGHTS = BIG[1:]


def _pack_rows(group=BIG):
    rows, at = {}, 0
    for name, shape, axis in group:
        n = math.prod(_shard_shape(shape, axis)) // PACK_W
        rows[name] = (at, n)
        at += n
    return rows, at


def _pack_shards(shards, group):
    return jnp.concatenate([shards[name].reshape(-1, PACK_W) for name, _, _ in group], axis=0)


def _unpack_full(gathered, group):
    rows, _ = _pack_rows(group)
    full = {}
    for name, shape, axis in group:
        at, n = rows[name]
        parts = gathered[:, at:at + n, :].reshape((N_CHIPS,) + _shard_shape(shape, axis))
        if axis == 0:
            full[name] = parts.reshape(shape)
        else:
            full[name] = jnp.transpose(parts, (1, 0, 2)).reshape(shape)
    return full


def _pack_full_grads(grads, group):
    chunks = []
    for name, shape, axis in group:
        g = grads[name]
        if axis == 0:
            parts = g.reshape((N_CHIPS, shape[0] // N_CHIPS, shape[1]))
        else:
            parts = jnp.transpose(g.reshape((shape[0], N_CHIPS, shape[1] // N_CHIPS)), (1, 0, 2))
        chunks.append(parts.reshape(N_CHIPS, -1, PACK_W))
    return jnp.concatenate(chunks, axis=1)


def _unpack_shard(packed, group):
    rows, _ = _pack_rows(group)
    return {name: packed[rows[name][0]:rows[name][0] + rows[name][1]].reshape(_shard_shape(shape, axis))
            for name, shape, axis in group}


def _local_step(x, target, early_shards, late_shards, norm_mix_g, b_gate, norm_mlp_g, norm_final_g, core):
    h, early = _rms_fwd_and_gather(x, norm_mix_g, early_shards)
    w = {"w_in": jnp.transpose(early, (1, 0, 2)).reshape(D_MODEL, IN_COLS)}
    w_in = w["w_in"]
    sb0 = 9 * DIL_W
    w_sb, w_gate = w_in[:, sb0:QKV_W], w_in[:, QKV_W:]
    w_dil = [jnp.concatenate([w_in[:, (3 * i + g) * DIL_W:(3 * i + g + 1) * DIL_W] for i in range(3)], axis=1)
             for g in range(3)]

    qkv_dil = [_matmul(h, w_dil[g], mode="nn", out_dtypes=(BF16,), name=f"proj_dil_g{g}", tn=768)[0] for g in range(3)]
    (qkv_sb,) = _matmul(h, w_sb, mode="nn", out_dtypes=(BF16,), name="proj_sb", tn=768)
    (gl,) = _matmul(h, w_gate, mode="nn", out_dtypes=(F32,), name="proj_gate")
    dil = [_dil_fwd(qkv_dil[g], g) for g in range(3)]
    o_groups, lse_groups = [d[0] for d in dil], [d[1] for d in dil]
    o_sb, a_sb, late_others = _sb_fwd(qkv_sb, late_shards)
    w = {**w, **_unpack_full(_fill_own_slot(late_others, late_shards), LATE_WEIGHTS)}
    merged, o_a = _merge_fwd(o_groups, lse_groups, o_sb, gl, b_gate, w["w_up_dil"], w["w_up_sb"])
    def residual_and_norm(acc, res, g):
        x1 = res + acc
        return x1, _rms_rows(x1)[0] * g

    x1, h2 = _matmul(merged, w["w_out"], mode="nn", out_dtypes=(F32, BF16), name="out_proj", tm=ROW_TILE,
                     extras=(x, norm_mlp_g), epilogue=residual_and_norm)
    u, act = _matmul(h2, w["w_mlp_in"], mode="nn", out_dtypes=(BF16, BF16), name="mlp_in",
                     epilogue=lambda acc: (acc, jnp.square(jnp.maximum(acc, 0.0))))

    def residual_and_loss(acc, res, tgt, g):
        xh, r = _rms_rows(res + acc)
        err = xh * g - tgt
        dy = err * (1.0 / D_MODEL)
        dxh = dy * g
        dx2 = r * (dxh - xh * jnp.mean(dxh * xh, axis=-1, keepdims=True))
        return dx2, _rows_sum8(dy * xh), (0.5 / D_MODEL) * _rows_sum8(err * err)

    dx2, dg_final, loss_part = _matmul(
        act, w["w_mlp_out"], mode="nn", out_dtypes=(F32, ("part", F32), ("part", F32)), name="mlp_out", tm=ROW_TILE,
        tk=2048, extras=(x1, target, norm_final_g.reshape(1, D_MODEL)), epilogue=residual_and_loss)

    (du,) = _matmul(dx2, w["w_mlp_out"], mode="nt", out_dtypes=(BF16,), name="mlp_out_dx",
                    extras=(u,), epilogue=lambda acc, uu: (acc * (2.0 * jnp.maximum(uu.astype(F32), 0.0)),))
    pack_shape = (N_CHIPS, 2 * D_MODEL, D_MODEL)
    (half_pack,) = _matmul(act, dx2, mode="tn", out_dtypes=(F32,), name="mlp_out_dw",
                           into=(pack_shape, lambda i, j: (i, 1, 0), None))
    (mlp_pack,) = _matmul(h2, du, mode="tn", out_dtypes=(F32,), name="mlp_in_dw",
                          into=(pack_shape, lambda i, j: (j, 0, 0), half_pack))

    def norm_bwd(acc, xx, dres, g):
        dx, dg = _rms_bwd_rows(acc, xx, g)
        return dres + dx, dg

    dx1, dg_mlp = _matmul(du, w["w_mlp_in"], mode="nt", out_dtypes=(F32, ("part", F32)), name="mlp_in_dx",
                          tm=ROW_TILE, tk=2048, extras=(x1, dx2, norm_mlp_g), epilogue=norm_bwd)

    (dmerged,) = _matmul(dx1, w["w_out"], mode="nt", out_dtypes=(F32,), name="out_proj_dx")
    (g_out,) = _matmul(merged, dx1, mode="tn", out_dtypes=(F32,), name="out_proj_dw")
    mb = _merge_bwd(dmerged, o_groups, lse_groups, o_sb, gl, b_gate, w["w_up_dil"], w["w_up_sb"], mlp_pack)
    dua, dub, dgl, dbg, do_sb = mb[:5]
    do_groups, c_groups = mb[5:8], mb[8:11]
    mlp_sums = _add_halves(mlp_pack, mb[11], core)
    (g_up_dil,) = _matmul(o_a, dua, mode="tn", out_dtypes=(F32,), name="up_dil_dw")
    (g_up_sb,) = _matmul(o_sb, dub, mode="tn", out_dtypes=(F32,), name="up_sb_dw")
    dq_sb, dk_sb, dv_sb, mlp_got = _sb_bwd(qkv_sb, do_sb, a_sb, mlp_sums)
    dil_b = [_dil_bwd(qkv_dil[g], do_groups[g], lse_groups[g], c_groups[g], g) for g in range(3)]
    dproj = jnp.concatenate(
        [dil_b[g][i].astype(BF16) for i in range(3) for g in range(3)]
        + [t.astype(BF16) for t in (dq_sb, dk_sb, dv_sb)] + [dgl], axis=1)
    (g_in,) = _matmul(h, dproj, mode="tn", out_dtypes=(F32,), name="proj_dw", tm=512, tn=IN_COLS // 2)
    mixer_pack = _pack_full_grads({"w_in": g_in, "w_up_dil": g_up_dil, "w_up_sb": g_up_sb, "w_out": g_out}, MIXER_GROUP)
    mixer_sums = _add_halves(mixer_pack, _swap_halves(mixer_pack), core)
    grad_x, dg_mix, mixer_got = _matmul(
        dproj, w["w_in"], mode="nt", out_dtypes=(F32, ("part", F32)), name="proj_dx", tm=ROW_TILE, tk=IN_COLS // 2,
        extras=(x, dx1, norm_mix_g), epilogue=norm_bwd, exchange=mixer_sums)

    small = (dg_mix, dbg, dg_mlp, dg_final, loss_part)
    return grad_x, (mixer_got, mixer_sums), (mlp_got, mlp_sums), small


def kernel(x, norm_mix_g, w_in, b_gate, w_up_dil, w_up_sb, w_out, norm_mlp_g, w_mlp_in, w_mlp_out, norm_final_g, loss_target, m_norm_mix_g, m_w_in, m_b_gate, m_w_up_dil, m_w_up_sb, m_w_out, m_norm_mlp_g, m_w_mlp_in, m_w_mlp_out, m_norm_final_g, v_norm_mix_g, v_w_in, v_b_gate, v_w_up_dil, v_w_up_sb, v_w_out, v_norm_mlp_g, v_w_mlp_in, v_w_mlp_out, v_norm_final_g):
    shards = {"w_in": w_in[0], "w_up_dil": w_up_dil[0], "w_up_sb": w_up_sb[0], "w_out": w_out[0],
              "w_mlp_in": w_mlp_in[0], "w_mlp_out": w_mlp_out[0]}
    moments_m = {"w_in": m_w_in[0], "w_up_dil": m_w_up_dil[0], "w_up_sb": m_w_up_sb[0], "w_out": m_w_out[0],
                 "w_mlp_in": m_w_mlp_in[0], "w_mlp_out": m_w_mlp_out[0]}
    moments_v = {"w_in": v_w_in[0], "w_up_dil": v_w_up_dil[0], "w_up_sb": v_w_up_sb[0], "w_out": v_w_out[0],
                 "w_mlp_in": v_w_mlp_in[0], "w_mlp_out": v_w_mlp_out[0]}

    shards16 = {n: s.astype(BF16) for n, s in shards.items()}
    early_shards = shards16["w_in"]
    late_shards = _pack_shards(shards16, LATE_WEIGHTS)

    core = lax.axis_index("c").astype(jnp.int32).reshape(1)
    chip = (2 * lax.axis_index("x") + lax.axis_index("y")).astype(jnp.int32).reshape(1)
    grad_x, (mixer_got, mixer_sums), (mlp_got, mlp_sums), small = _local_step(
        x[0], loss_target[0], early_shards, late_shards, norm_mix_g, b_gate, norm_mlp_g, norm_final_g, core)

    reduced = _join_halves(_sum_chips(mixer_got, mixer_sums, chip))
    reduced_mlp = _join_halves(_sum_chips(mlp_got, mlp_sums, chip))
    g_shard = {**_unpack_shard(reduced, MIXER_GROUP), **_unpack_shard(reduced_mlp, MLP_GROUP)}

    dg_mix, dbg, dg_mlp, dg_final, loss_part = small
    loss_row = jnp.sum(loss_part, axis=0, keepdims=True)
    small_pack = jnp.concatenate(
        [jnp.sum(dg_mix, axis=0, keepdims=True), jnp.sum(dbg, axis=0, keepdims=True),
         jnp.sum(dg_mlp, axis=0, keepdims=True), jnp.sum(dg_final, axis=0, keepdims=True), loss_row], axis=1)
    n_small = small_pack.shape[1]
    small_sum = _all_reduce_small(small_pack.reshape(n_small // 128, 128)).reshape(1, n_small)
    g_norm_mix = small_sum[:, :D_MODEL]
    g_b_gate = small_sum[:, D_MODEL:3 * D_MODEL]
    g_norm_mlp = small_sum[:, 3 * D_MODEL:4 * D_MODEL]
    g_norm_final = small_sum[:, 4 * D_MODEL:5 * D_MODEL]
    loss = jnp.sum(small_sum[:, 5 * D_MODEL:])

    names = ["norm_mix_g", "w_in", "b_gate", "w_up_dil", "w_up_sb", "w_out", "norm_mlp_g", "w_mlp_in", "w_mlp_out",
             "norm_final_g"]
    grads = dict(g_shard)
    grads.update(norm_mix_g=g_norm_mix, b_gate=g_b_gate, norm_mlp_g=g_norm_mlp, norm_final_g=g_norm_final)
    weights = dict(shards)
    weights.update(norm_mix_g=norm_mix_g, b_gate=b_gate, norm_mlp_g=norm_mlp_g, norm_final_g=norm_final_g.reshape(1, D_MODEL))
    ms = dict(moments_m)
    ms.update(norm_mix_g=m_norm_mix_g, b_gate=m_b_gate, norm_mlp_g=m_norm_mlp_g, norm_final_g=m_norm_final_g.reshape(1, D_MODEL))
    vs = dict(moments_v)
    vs.update(norm_mix_g=v_norm_mix_g, b_gate=v_b_gate, norm_mlp_g=v_norm_mlp_g, norm_final_g=v_norm_final_g.reshape(1, D_MODEL))

    out_shapes = {"norm_mix_g": norm_mix_g.shape, "w_in": w_in.shape, "b_gate": b_gate.shape, "w_up_dil": w_up_dil.shape,
                  "w_up_sb": w_up_sb.shape, "w_out": w_out.shape, "norm_mlp_g": norm_mlp_g.shape,
                  "w_mlp_in": w_mlp_in.shape, "w_mlp_out": w_mlp_out.shape, "norm_final_g": norm_final_g.shape}
    g_out, d_out, m_out, v_out = [], [], [], []
    for n in names:
        d, nm, nv = _adamw(grads[n], weights[n], ms[n], vs[n], "adamw_" + n)
        shape = out_shapes[n]
        g_out.append(grads[n].reshape(shape))
        d_out.append(d.reshape(shape))
        m_out.append(nm.reshape(shape))
        v_out.append(nv.reshape(shape))
    return (loss, grad_x.reshape(x.shape), *g_out, *d_out, *m_out, *v_out)
```

```python
import math

import jax
import jax.numpy as jnp
import numpy as np
from jax import lax
from jax.experimental import pallas as pl
from jax.experimental.pallas import tpu as pltpu

F32 = jnp.float32
BF16 = jnp.bfloat16
MESH = pl.DeviceIdType.MESH

D_MODEL = 1024
HEAD_DIM = 64
DIL_GROUPS = ((128, 1), (512, 4), (2048, 16))
DIL_HEADS = 4
DIL_W = 256
N_DIL_HEADS = 12
SB_HEADS = 8
SB_W = SB_HEADS * HEAD_DIM
QKV_W = 3 * 3 * DIL_W + 3 * SB_W
GATE_W = 2 * D_MODEL
IN_COLS = QKV_W + GATE_W
D_FF = 4 * D_MODEL
BLOCK = 128
RMS_EPS = 1e-6
NEG_INF = -1e30
N_CHIPS = 4
N_DEV = 8

ADAM_LR = 0.001
ADAM_B1 = 0.9
ADAM_B2 = 0.999
ADAM_EPS = 1e-08
ADAM_WD = 0.01
ADAM_STEP = 10

VMEM_LIMIT = 56 * 1024 * 1024

SB_BQ = 256
SB_BK = 256


def _cparams(sem=None):
    if sem is None:
        return pltpu.CompilerParams(vmem_limit_bytes=VMEM_LIMIT)
    return pltpu.CompilerParams(dimension_semantics=sem, vmem_limit_bytes=VMEM_LIMIT)


def _dot(a, b, dims):
    return lax.dot_general(a, b, (dims, ((), ())), preferred_element_type=F32)


def _dot_nn(a, b):
    return _dot(a, b, ((1,), (0,)))


def _dot_nt(a, b):
    return _dot(a, b, ((1,), (1,)))


def _dot_tn(a, b):
    return _dot(a, b, ((0,), (0,)))


def _dot_f32_by_01(x, m01, pieces=3):
    hi = x.astype(BF16)
    if pieces == 1:
        return _dot_nn(hi, m01)
    r1 = x - hi.astype(F32)
    mid = r1.astype(BF16)
    if pieces == 2:
        return _dot_nn(hi, m01) + _dot_nn(mid, m01)
    lo = (r1 - mid.astype(F32)).astype(BF16)
    return _dot_nn(hi, m01) + _dot_nn(mid, m01) + _dot_nn(lo, m01)


def _matmul(a, b, *, mode, out_dtypes, name, tm=1024, tn=1024, tk=1024, extras=(), epilogue=None, exchange=None,
            into=None):
    if mode == "nn":
        (m, k), (k2, n) = a.shape, b.shape
    elif mode == "nt":
        (m, k), (n, k2) = a.shape, b.shape
    else:
        (k, m), (k2, n) = a.shape, b.shape
    assert k == k2, (a.shape, b.shape, mode)
    tm, tn, tk = min(tm, m), min(tn, n), min(tk, k)
    assert m % tm == 0 and n % tn == 0 and k % tk == 0, (m, n, k, tm, tn, tk)
    nk = k // tk
    n_out = len(out_dtypes)
    n_ex = len(extras)

    if mode == "nn":
        a_spec = pl.BlockSpec((tm, tk), lambda i, j, kk: (i, kk))
        b_spec = pl.BlockSpec((tk, tn), lambda i, j, kk: (kk, j))
        dot = _dot_nn
    elif mode == "nt":
        a_spec = pl.BlockSpec((tm, tk), lambda i, j, kk: (i, kk))
        b_spec = pl.BlockSpec((tn, tk), lambda i, j, kk: (j, kk))
        dot = _dot_nt
    else:
        a_spec = pl.BlockSpec((tk, tm), lambda i, j, kk: (kk, i))
        b_spec = pl.BlockSpec((tk, tn), lambda i, j, kk: (kk, j))
        dot = _dot_tn
    mn_spec = pl.BlockSpec((tm, tn), lambda i, j, kk: (i, j))
    row_spec = pl.BlockSpec((1, tn), lambda i, j, kk: (0, j))
    part_spec = pl.BlockSpec((8, tn), lambda i, j, kk: (i, j))
    ex_specs = [row_spec if e.shape[0] == 1 else mn_spec for e in extras]
    is_part = [isinstance(dt, tuple) for dt in out_dtypes]
    out_dts = [dt[1] if p else dt for dt, p in zip(out_dtypes, is_part)]
    out_specs = [part_spec if p else mn_spec for p in is_part]
    out_shapes = [jax.ShapeDtypeStruct((8 * (m // tm), n) if p else (m, n), dt) for dt, p in zip(out_dts, is_part)]

    n_side = 0 if exchange is None else 1
    grid = (m // tm, n // tn, nk)
    prior = []
    if into is not None:
        assert n_out == 1 and not extras and exchange is None
        into_shape, into_map, into_prior = into
        out_specs = [pl.BlockSpec((1, tm, tn), lambda i, j, kk: into_map(i, j))]
        out_shapes = [jax.ShapeDtypeStruct(into_shape, out_dts[0])]
        prior = [] if into_prior is None else [into_prior]

    def body(*refs):
        a_ref, b_ref = refs[0], refs[1]
        ex_refs = refs[2:2 + n_ex]
        n_in = 2 + n_ex + n_side + len(prior)
        out_refs = refs[n_in:n_in + n_out]
        scratch = refs[n_in + n_out + n_side:]
        acc_ref = scratch[0] if nk > 1 else None
        if exchange is not None:
            side = (refs[2 + n_ex], refs[2 + n_ex + n_side + n_out]) + tuple(scratch[-2:])
            step = (pl.program_id(0) * grid[1] + pl.program_id(1)) * grid[2] + pl.program_id(2)

            @pl.when(step == 0)
            def _():
                _exchange_start(*side)

            @pl.when(step == grid[0] * grid[1] * grid[2] - 1)
            def _():
                _exchange_wait(*side)

        part = dot(a_ref[...].astype(BF16), b_ref[...].astype(BF16))

        def finish(acc):
            if epilogue is None:
                outs = (acc,)
            else:
                outs = epilogue(acc, *[r[...] for r in ex_refs])
            for o_ref, o in zip(out_refs, outs):
                if into is None:
                    o_ref[...] = o.astype(o_ref.dtype)
                else:
                    o_ref[0] = o.astype(o_ref.dtype)

        if nk == 1:
            finish(part)
        else:
            kk = pl.program_id(2)

            @pl.when(kk == 0)
            def _():
                acc_ref[...] = part

            @pl.when(kk > 0)
            def _():
                acc_ref[...] += part

            @pl.when(kk == nk - 1)
            def _():
                finish(acc_ref[...])

    side_in = [] if exchange is None else [exchange]
    outs = pl.pallas_call(
        body,
        name=name,
        grid=grid,
        in_specs=[a_spec, b_spec] + ex_specs + [ANY] * (n_side + len(prior)),
        out_specs=out_specs + [ANY] * n_side,
        out_shape=out_shapes + [jax.ShapeDtypeStruct(e.shape, e.dtype) for e in side_in],
        scratch_shapes=([pltpu.VMEM((tm, tn), F32)] if nk > 1 else [])
        + [pltpu.SemaphoreType.DMA((3,)), pltpu.SemaphoreType.DMA((3,))] * n_side,
        input_output_aliases={2: 0} if prior else {},
        compiler_params=_cparams(("arbitrary",) * 3 if n_side else ("parallel", "parallel", "arbitrary")),
    )(a, b, *extras, *side_in, *prior)
    return outs


ROW_TILE = 512


def _rows_sum8(t):
    rows, d = t.shape
    return jnp.sum(t.reshape(rows // 8, 8, d), axis=0)


def _rms_rows(x):
    r = lax.rsqrt(jnp.mean(x * x, axis=-1, keepdims=True) + RMS_EPS)
    return x * r, r


def _rms_bwd_rows(dh, x, g):
    xh, r = _rms_rows(x)
    dxh = dh * g
    return r * (dxh - xh * jnp.mean(dxh * xh, axis=-1, keepdims=True)), _rows_sum8(dh * xh)


def _rms_bwd_residual(dh, x, g, dres):
    s, d = x.shape

    def body(dh_ref, x_ref, g_ref, dres_ref, dx_ref, dg_ref):
        dx, dg = _rms_bwd_rows(dh_ref[...], x_ref[...], g_ref[...])
        dx_ref[...] = dres_ref[...] + dx
        dg_ref[...] = dg

    row = pl.BlockSpec((ROW_TILE, d), lambda i: (i, 0))
    return pl.pallas_call(
        body,
        name="norm_mix_bwd",
        grid=(s // ROW_TILE,),
        in_specs=[row, row, pl.BlockSpec((1, d), lambda i: (0, 0)), row],
        out_specs=[row, pl.BlockSpec((8, d), lambda i: (i, 0))],
        out_shape=[jax.ShapeDtypeStruct((s, d), F32), jax.ShapeDtypeStruct((8 * (s // ROW_TILE), d), F32)],
        compiler_params=_cparams(("parallel",)),
    )(dh, x, g, dres)


def _rms_fwd_and_gather(x, g, shard_pack):
    s, d = x.shape
    r_pack, w_pack = shard_pack.shape
    steps = s // ROW_TILE

    def body(x_ref, g_ref, pack_ref, h_ref, others_ref, send_sems, recv_sems):
        i = pl.program_id(0)
        gather = (pack_ref, others_ref, send_sems, recv_sems)

        @pl.when(i == 0)
        def _():
            _gather_start(*gather)

        h_ref[...] = (_rms_rows(x_ref[...])[0] * g_ref[...]).astype(BF16)

        @pl.when(i == steps - 1)
        def _():
            _gather_pass_on(*gather)
            _gather_finish(*gather)

    h, others = pl.pallas_call(
        body,
        name="norm_mix",
        grid=(steps,),
        in_specs=[pl.BlockSpec((ROW_TILE, d), lambda i: (i, 0)), pl.BlockSpec((1, d), lambda i: (0, 0)), ANY],
        out_specs=[pl.BlockSpec((ROW_TILE, d), lambda i: (i, 0)), ANY],
        out_shape=[jax.ShapeDtypeStruct((s, d), BF16),
                   jax.ShapeDtypeStruct((N_CHIPS, 2, r_pack // 2, w_pack), shard_pack.dtype)],
        scratch_shapes=[pltpu.SemaphoreType.DMA((6,)), pltpu.SemaphoreType.DMA((6,))],
        compiler_params=_cparams(("arbitrary",)),
    )(x, g, shard_pack.reshape(2, r_pack // 2, w_pack))
    return h, _fill_own_slot(others, shard_pack)


def _alibi_slopes():
    return np.exp2(np.float32(-8.0) * np.arange(1, N_DIL_HEADS + 1, dtype=np.float32) / np.float32(N_DIL_HEADS))


def _head_lane_mask(h, rows):
    lane = lax.broadcasted_iota(jnp.int32, (rows, DIL_W), 1)
    return (lane >= h * HEAD_DIM) & (lane < (h + 1) * HEAD_DIM)


def _band_terms(dil, has_prev):
    qi = lax.broadcasted_iota(jnp.int32, (BLOCK, 2 * BLOCK), 0)
    kj = lax.broadcasted_iota(jnp.int32, (BLOCK, 2 * BLOCK), 1)
    steps = qi + BLOCK - kj
    valid = (steps >= 0) & (steps <= BLOCK) & ((kj >= BLOCK) | has_prev)
    return valid, steps.astype(F32) * float(dil)


def _load_halves(ref, rows):
    return jnp.concatenate([ref[0, rows, :], ref[1, rows, :]], axis=1)


def _store_halves(ref, rows, value):
    ref[0, rows, :] = value[:, :128]
    ref[1, rows, :] = value[:, 128:]


def _dil_fwd(qkv_g, group):
    _, dil = DIL_GROUPS[group]
    s = qkv_g.shape[0]
    sub = s // dil
    nb = sub // BLOCK
    view = qkv_g.reshape(sub, dil * 3 * DIL_W)
    slopes = _alibi_slopes()[group * DIL_HEADS:(group + 1) * DIL_HEADS]

    def col(which):
        return lambda n, r: (n, r * 3 + which)

    def col_prev(which):
        return lambda n, r: (jnp.maximum(n - 1, 0), r * 3 + which)

    def body(q_ref, kc_ref, kp_ref, vc_ref, vp_ref, o_ref, lse_ref):
        n, r = pl.program_id(0), pl.program_id(1)
        mine = pl.ds(r, BLOCK, stride=dil) if dil > 1 else slice(None)
        valid, dist = _band_terms(dil, n > 0)
        q = q_ref[...]
        k2 = jnp.concatenate([kp_ref[...], kc_ref[...]], axis=0)
        v2 = jnp.concatenate([vp_ref[...], vc_ref[...]], axis=0)
        masks = [_head_lane_mask(h, BLOCK) for h in range(DIL_HEADS)]
        logits = [_dot_nt(jnp.where(masks[h], q, jnp.zeros_like(q)), k2) for h in range(DIL_HEADS)]
        ps, lses = [], []
        for h in range(DIL_HEADS):
            lg = jnp.where(valid, logits[h] * 0.125 - float(slopes[h]) * dist, NEG_INF)
            mx = jnp.max(lg, axis=1, keepdims=True)
            lse = mx + jnp.log(jnp.sum(jnp.exp(lg - mx), axis=1, keepdims=True))
            ps.append(jnp.exp(lg - lse).astype(BF16))
            lses.append(lse)
        o_acc = jnp.zeros((BLOCK, DIL_W), F32)
        lse_acc = jnp.zeros((BLOCK, DIL_W), F32)
        for h in range(DIL_HEADS):
            o_acc = jnp.where(masks[h], _dot_nn(ps[h], v2), o_acc)
            lse_acc = jnp.where(masks[h], lses[h], lse_acc)
        _store_halves(o_ref, mine, o_acc)
        _store_halves(lse_ref, mine, lse_acc)

    blk = (BLOCK, DIL_W)
    return pl.pallas_call(
        body,
        name=f"dil_fwd_g{group}",
        grid=(nb, dil),
        in_specs=[pl.BlockSpec(blk, col(0)), pl.BlockSpec(blk, col(1)), pl.BlockSpec(blk, col_prev(1)),
                  pl.BlockSpec(blk, col(2)), pl.BlockSpec(blk, col_prev(2))],
        out_specs=[pl.BlockSpec((2, BLOCK * dil, 128), lambda n, r: (0, n, 0))] * 2,
        out_shape=[jax.ShapeDtypeStruct((2, s, 128), F32)] * 2,
        compiler_params=_cparams(("parallel", "arbitrary")),
    )(view, view, view, view, view)


def _dil_bwd(qkv, do, lse, cterm, group):
    _, dil = DIL_GROUPS[group]
    s = qkv.shape[0]
    sub = s // dil
    nb = sub // BLOCK
    view = qkv.reshape(sub, dil * 3 * DIL_W)
    slopes = _alibi_slopes()[group * DIL_HEADS:(group + 1) * DIL_HEADS]

    def col(which, shift):
        if shift == 0:
            return lambda n, r: (n, r * 3 + which)
        if shift < 0:
            return lambda n, r: (jnp.maximum(n - 1, 0), r * 3 + which)
        return lambda n, r: (jnp.minimum(n + 1, nb - 1), r * 3 + which)

    def own(shift):
        if shift == 0:
            return pl.BlockSpec((2, BLOCK * dil, 128), lambda n, r: (0, n, 0))
        return pl.BlockSpec((2, BLOCK * dil, 128), lambda n, r: (0, jnp.minimum(n + 1, nb - 1), 0))

    def body(q_ref, qn_ref, kc_ref, kp_ref, vc_ref, vp_ref, do_ref, don_ref, lse_ref, lsen_ref, c_ref, cn_ref,
             dq_ref, dk_ref, dv_ref):
        n, r = pl.program_id(0), pl.program_id(1)
        mine = pl.ds(r, BLOCK, stride=dil) if dil > 1 else slice(None)
        valid, dist = _band_terms(dil, n > 0)
        valid_n = _band_terms(dil, True)[0][:, :BLOCK] & (n < nb - 1)
        dist_n = dist[:, :BLOCK]
        q, qn = q_ref[...], qn_ref[...]
        kc, vc = kc_ref[...], vc_ref[...]
        k2 = jnp.concatenate([kp_ref[...], kc], axis=0)
        v2 = jnp.concatenate([vp_ref[...], vc], axis=0)
        dov, donv = _load_halves(do_ref, mine), _load_halves(don_ref, mine)
        lsev, lsenv = _load_halves(lse_ref, mine), _load_halves(lsen_ref, mine)
        cv, cnv = _load_halves(c_ref, mine), _load_halves(cn_ref, mine)
        masks = [_head_lane_mask(h, BLOCK) for h in range(DIL_HEADS)]

        def head_col(t, hm):
            return jnp.max(jnp.where(hm, t, NEG_INF), axis=1, keepdims=True)

        qhs = [jnp.where(hm, q, jnp.zeros_like(q)) for hm in masks]
        qnhs = [jnp.where(hm, qn, jnp.zeros_like(qn)) for hm in masks]
        dohs = [jnp.where(hm, dov, 0.0).astype(BF16) for hm in masks]
        donhs = [jnp.where(hm, donv, 0.0).astype(BF16) for hm in masks]
        logit = [_dot_nt(qhs[h], k2) for h in range(DIL_HEADS)]
        dp = [_dot_nt(dohs[h], v2) for h in range(DIL_HEADS)]
        logit_n = [_dot_nt(qnhs[h], kc) for h in range(DIL_HEADS)]
        dp_n = [_dot_nt(donhs[h], vc) for h in range(DIL_HEADS)]
        p16, dlog, pn16, dlog_n = [], [], [], []
        for h in range(DIL_HEADS):
            hm, slope = masks[h], float(slopes[h])
            p = jnp.where(valid, jnp.exp(logit[h] * 0.125 - slope * dist - head_col(lsev, hm)), 0.0)
            dlog.append((p * (dp[h] + head_col(cv, hm)) * 0.125).astype(BF16))
            p16.append(p.astype(BF16))
            pn = jnp.where(valid_n, jnp.exp(logit_n[h] * 0.125 - slope * dist_n - head_col(lsenv, hm)), 0.0)
            dlog_n.append((pn * (dp_n[h] + head_col(cnv, hm)) * 0.125).astype(BF16))
            pn16.append(pn.astype(BF16))
        dq_acc = jnp.zeros((BLOCK, DIL_W), F32)
        dk_acc = jnp.zeros((BLOCK, DIL_W), F32)
        dv_acc = jnp.zeros((BLOCK, DIL_W), F32)
        for h in range(DIL_HEADS):
            dq_acc = jnp.where(masks[h], _dot_nn(dlog[h], k2), dq_acc)
            dk_acc += _dot_tn(dlog[h][:, BLOCK:], qhs[h]) + _dot_tn(dlog_n[h], qnhs[h])
            dv_acc += _dot_tn(p16[h][:, BLOCK:], dohs[h]) + _dot_tn(pn16[h], donhs[h])
        dq_ref[...] = dq_acc.astype(BF16)
        dk_ref[...] = dk_acc.astype(BF16)
        dv_ref[...] = dv_acc.astype(BF16)

    blk = (BLOCK, DIL_W)
    outs = pl.pallas_call(
        body,
        name=f"dil_bwd_g{group}",
        grid=(nb, dil),
        in_specs=[pl.BlockSpec(blk, col(0, 0)), pl.BlockSpec(blk, col(0, 1)),
                  pl.BlockSpec(blk, col(1, 0)), pl.BlockSpec(blk, col(1, -1)),
                  pl.BlockSpec(blk, col(2, 0)), pl.BlockSpec(blk, col(2, -1)),
                  own(0), own(1), own(0), own(1), own(0), own(1)],
        out_specs=[pl.BlockSpec(blk, lambda n, r: (n, r))] * 3,
        out_shape=[jax.ShapeDtypeStruct((sub, dil * DIL_W), BF16)] * 3,
        compiler_params=_cparams(("parallel", "parallel")),
    )(view, view, view, view, view, view, do, do, lse, lse, cterm, cterm)
    return tuple(t.reshape(s, DIL_W) for t in outs)


SB_PAIRS = SB_HEADS // 2
SB_COL0 = 0
LOG2E = 1.4426950408889634


SB_EXP_CLAMP = 64.0


def _sb_softplus2(zs):
    t = 1.0 + jnp.exp2(jnp.minimum(zs, SB_EXP_CLAMP))
    return jnp.maximum(jnp.log(t) * LOG2E, zs)


def _sb_consts(nkb):
    row = lax.broadcasted_iota(jnp.int32, (SB_BQ, SB_BK), 0)
    colk = lax.broadcasted_iota(jnp.int32, (SB_BQ, SB_BK), 1)
    rr = lax.broadcasted_iota(jnp.int32, (SB_BK, SB_BK), 0)
    cc = lax.broadcasted_iota(jnp.int32, (SB_BK, SB_BK), 1)
    lane = lax.broadcasted_iota(jnp.int32, (SB_BQ, 128), 1)
    assert 2 * nkb <= 128
    return colk < row, rr, cc, lane < HEAD_DIM, lane


def _split_heads(t):
    first = lax.broadcasted_iota(jnp.int32, t.shape, 1) < HEAD_DIM
    zero = jnp.zeros_like(t)
    return jnp.where(first, t, zero), jnp.where(first, zero, t)


def _sb_fwd(qkv, shard_pack):
    s = qkv.shape[0]
    nq, nkb = s // SB_BQ, s // SB_BK
    zscale = LOG2E / math.sqrt(HEAD_DIM)
    r_pack, w_pack = shard_pack.shape

    def body(q_ref, k_ref, v_ref, pack_ref, o_ref, a_row, others_ref, zs_scr, a_scr, acc_scr, cl_scr,
             send_sems, recv_sems):
        i = pl.program_id(1)
        pair = pl.program_id(0)
        gather = (pack_ref, others_ref, send_sems, recv_sems)

        @pl.when((pair == 0) & (i == 0))
        def _():
            _gather_start(*gather)

        @pl.when((pair == 1) & (i == 0))
        def _():
            _gather_pass_on(*gather)

        @pl.when((pair == SB_PAIRS - 1) & (i == nq - 1))
        def _():
            _gather_finish(*gather)

        causal, rr, cc, _, _ = _sb_consts(nkb)
        later = (rr > cc).astype(BF16)
        qh = _split_heads(q_ref[...])

        def rows(j):
            return pl.ds(pl.multiple_of(j * SB_BK, SB_BK), SB_BK)

        def scores_to(slot, j):
            kb = k_ref[rows(j), :]
            for hh in range(2):
                zs_scr[slot, hh] = _dot_nt(qh[hh], kb) * zscale

        def weights(slot, j, masked):
            xs, sums, sufs = [], [], []
            for hh in range(2):
                zs = zs_scr[slot, hh]
                sp = _sb_softplus2(zs)
                if masked:
                    sp = jnp.where(causal, sp, 0.0)
                xs.append(zs - sp)
                sums.append(jnp.sum(sp, axis=1, keepdims=True))
                sufs.append(_dot_f32_by_01(sp, later, 2))
            for hh in range(2):
                cl = cl_scr[hh]
                a = jnp.exp2(xs[hh] - (sufs[hh] + jnp.concatenate([cl, cl], axis=1)))
                if masked:
                    a = jnp.where(causal, a, 0.0)
                a16 = a.astype(BF16)
                a_scr[slot, :, hh * SB_BK:(hh + 1) * SB_BK] = a16
                a_row[0, 0, j, :, hh * SB_BK:(hh + 1) * SB_BK] = a16
                cl_scr[hh] = cl + sums[hh]

        def add_av(slot, j):
            v0, v1 = _split_heads(v_ref[rows(j), :])
            acc_scr[...] += _dot_nn(a_scr[slot], jnp.concatenate([v0, v1], axis=0))

        acc_scr[...] = jnp.zeros_like(acc_scr)
        cl_scr[...] = jnp.zeros_like(cl_scr)
        scores_to(0, i)
        scores_to(1, jnp.maximum(i - 1, 0))
        weights(0, i, True)

        def step(j, prev, cur):
            scores_to(prev, jnp.maximum(j - 1, 0))
            add_av(prev, j + 1)
            weights(cur, j, False)

        def two_steps(u, _):
            j = i - 1 - 2 * u
            step(j, 0, 1)
            step(j - 1, 1, 0)
            return 0

        lax.fori_loop(0, i // 2, two_steps, 0)

        @pl.when(i % 2 == 1)
        def _():
            step(0, 0, 1)
            add_av(1, 0)

        @pl.when(i % 2 == 0)
        def _():
            add_av(0, 0)

        o_ref[...] = acc_scr[...]

    def full(which):
        return pl.BlockSpec((s, 128), lambda p, i: (0, SB_COL0 + 4 * which + p))

    return pl.pallas_call(
        body,
        name="sb_fwd",
        grid=(SB_PAIRS, nq),
        in_specs=[pl.BlockSpec((SB_BQ, 128), lambda p, i: (i, SB_COL0 + p)), full(1), full(2), ANY],
        out_specs=[pl.BlockSpec((SB_BQ, 128), lambda p, i: (i, p)),
                   pl.BlockSpec((1, 1, nkb, SB_BQ, 2 * SB_BK), lambda p, i: (p, i, 0, 0, 0)), ANY],
        out_shape=[jax.ShapeDtypeStruct((s, SB_W), F32),
                   jax.ShapeDtypeStruct((SB_PAIRS, nq, nkb, SB_BQ, 2 * SB_BK), BF16),
                   jax.ShapeDtypeStruct((N_CHIPS, 2, r_pack // 2, w_pack), shard_pack.dtype)],
        scratch_shapes=[pltpu.VMEM((2, 2, SB_BQ, SB_BK), F32), pltpu.VMEM((2, SB_BQ, 2 * SB_BK), BF16),
                        pltpu.VMEM((SB_BQ, 128), F32), pltpu.VMEM((2, SB_BQ, 128), F32),
                        pltpu.SemaphoreType.DMA((6,)), pltpu.SemaphoreType.DMA((6,))],
        compiler_params=_cparams(("arbitrary", "arbitrary")),
    )(qkv, qkv, qkv, shard_pack.reshape(2, r_pack // 2, w_pack))


def _sb_bwd(qkv, do, a_hbm, chip_sums):
    s = qkv.shape[0]
    nq, nkb = s // SB_BQ, s // SB_BK
    scale = 1.0 / math.sqrt(HEAD_DIM)
    zscale = LOG2E * scale

    def body(q_ref, k_ref, v_ref, do_ref, a_row, sums_ref, dq_ref, dk_ref, dv_ref, got_ref,
             zs_scr, da_scr, dz_scr, a_scr, cg_scr, send_sems, recv_sems):
        i = pl.program_id(1)
        pair = pl.program_id(0)
        first_step = (pair == 0) & (i == 0)
        last_step = (pair == SB_PAIRS - 1) & (i == nq - 1)

        @pl.when(first_step)
        def _():
            _exchange_start(sums_ref, got_ref, send_sems, recv_sems)

        @pl.when(i == 0)
        def _():
            dk_ref[...] = jnp.zeros_like(dk_ref)
            dv_ref[...] = jnp.zeros_like(dv_ref)

        causal, rr, cc, first, _ = _sb_consts(nkb)
        earlier = (rr < cc).astype(BF16)
        q2 = q_ref[...]
        qh = _split_heads(q2)
        do2 = do_ref[...].astype(BF16)
        doh = _split_heads(do2)

        def rows(j):
            return pl.ds(pl.multiple_of(j * SB_BK, SB_BK), SB_BK)

        def products_to(slot, j):
            kb, vb = k_ref[rows(j), :], v_ref[rows(j), :]
            for hh in range(2):
                zs_scr[slot, hh] = _dot_nt(qh[hh], kb) * (-zscale)
                da_scr[slot, hh] = _dot_nt(doh[hh], vb)

        head0_rows = lax.broadcasted_iota(jnp.int32, (128, SB_BK), 0) < HEAD_DIM

        def by_head(t):
            return jnp.where(head0_rows, t[:, :SB_BK], t[:, SB_BK:])

        def apply(slot, j):
            k0, k1 = _split_heads(k_ref[rows(j), :])
            dq_ref[...] += _dot_nn(dz_scr[slot], jnp.concatenate([k0, k1], axis=0)) * scale
            dk_ref[0, j] += by_head(_dot_tn(q2, dz_scr[slot])) * scale
            dv_ref[0, j] += by_head(_dot_tn(do2, a_scr[slot]))

        def grads(slot, j, masked):
            gs, gpres = [], []
            for hh in range(2):
                a16 = a_row[0, 0, j, :, hh * SB_BK:(hh + 1) * SB_BK]
                a_scr[slot, :, hh * SB_BK:(hh + 1) * SB_BK] = a16
                g = a16.astype(F32) * da_scr[slot, hh]
                gs.append(g)
                gpres.append(_dot_f32_by_01(g, earlier, 1))
            sigs = []
            for hh in range(2):
                e = jnp.exp2(jnp.minimum(zs_scr[slot, hh], SB_EXP_CLAMP))
                sigs.append(pl.reciprocal(1.0 + e, approx=True))
            for hh in range(2):
                cg = cg_scr[hh]
                dz = gs[hh] - (gs[hh] + (gpres[hh] + jnp.concatenate([cg, cg], axis=1))) * sigs[hh]
                if masked:
                    dz = jnp.where(causal, dz, 0.0)
                dz_scr[slot, :, hh * SB_BK:(hh + 1) * SB_BK] = dz.astype(BF16)
                cg_scr[hh] = cg + jnp.sum(gs[hh], axis=1, keepdims=True)

        dq_ref[...] = jnp.zeros_like(dq_ref)
        cg_scr[...] = jnp.zeros_like(cg_scr)
        dz_scr[1] = jnp.zeros((SB_BQ, 2 * SB_BK), BF16)
        a_scr[1] = jnp.zeros((SB_BQ, 2 * SB_BK), BF16)
        products_to(0, 0)

        def step(j, cur, nxt):
            products_to(nxt, j + 1)
            apply(nxt, jnp.maximum(j - 1, 0))
            grads(cur, j, False)

        def two_steps(u, _):
            step(2 * u, 0, 1)
            step(2 * u + 1, 1, 0)
            return 0

        lax.fori_loop(0, i // 2, two_steps, 0)

        def last(cur, nxt):
            apply(nxt, jnp.maximum(i - 1, 0))
            grads(cur, i, True)
            apply(cur, i)

        @pl.when(i % 2 == 1)
        def _():
            step(i - 1, 0, 1)
            last(1, 0)

        @pl.when(i % 2 == 0)
        def _():
            last(0, 1)

        @pl.when(last_step)
        def _():
            _exchange_wait(sums_ref, got_ref, send_sems, recv_sems)

    def full(which):
        return pl.BlockSpec((s, 128), lambda p, i: (0, SB_COL0 + 4 * which + p))

    qblk = pl.BlockSpec((SB_BQ, 128), lambda p, i: (i, p))
    acc = pl.BlockSpec((1, nkb, 128, SB_BK), lambda p, i: (p, 0, 0, 0))
    acc_shape = jax.ShapeDtypeStruct((SB_PAIRS, nkb, 128, SB_BK), F32)
    dq, dk_t, dv_t, got = pl.pallas_call(
        body,
        name="sb_bwd",
        grid=(SB_PAIRS, nq),
        in_specs=[pl.BlockSpec((SB_BQ, 128), lambda p, i: (i, SB_COL0 + p)), full(1), full(2), qblk,
                  pl.BlockSpec((1, 1, nkb, SB_BQ, 2 * SB_BK), lambda p, i: (p, i, 0, 0, 0)), ANY],
        out_specs=[qblk, acc, acc, ANY],
        out_shape=[jax.ShapeDtypeStruct((s, SB_W), F32), acc_shape, acc_shape,
                   jax.ShapeDtypeStruct(chip_sums.shape, chip_sums.dtype)],
        scratch_shapes=[pltpu.VMEM((2, 2, SB_BQ, SB_BK), F32), pltpu.VMEM((2, 2, SB_BQ, SB_BK), F32),
                        pltpu.VMEM((2, SB_BQ, 2 * SB_BK), BF16), pltpu.VMEM((2, SB_BQ, 2 * SB_BK), BF16),
                        pltpu.VMEM((2, SB_BQ, 128), F32),
                        pltpu.SemaphoreType.DMA((3,)), pltpu.SemaphoreType.DMA((3,))],
        compiler_params=_cparams(("arbitrary", "arbitrary")),
    )(qkv, qkv, qkv, do, a_hbm, chip_sums)

    def untranspose(t):
        return jnp.transpose(t, (1, 3, 0, 2)).reshape(s, SB_W)

    return dq, untranspose(dk_t), untranspose(dv_t), got


MERGE_TILE = 256


def _group_mix(lses):
    mx = jnp.maximum(jnp.maximum(lses[0], lses[1]), lses[2])
    es = [jnp.exp(t - mx) for t in lses]
    den = es[0] + es[1] + es[2]
    return [e / den for e in es]


def _merge_fwd(o_groups, lse_groups, o_sb, gl, b_gate, w_up_dil, w_up_sb):
    s = gl.shape[0]
    t = MERGE_TILE

    def body(o0, o1, o2, l0, l1, l2, ob_ref, gl_ref, bg_ref, wd_ref, ws_ref, merged_ref, oa_ref):
        rows = slice(None)
        w = _group_mix([_load_halves(l, rows) for l in (l0, l1, l2)])
        og = [_load_halves(o, rows) for o in (o0, o1, o2)]
        oa = (w[0] * og[0] + w[1] * og[1] + w[2] * og[2]).astype(BF16)
        ua = _dot_nn(oa, wd_ref[...])
        ub = _dot_nn(ob_ref[...].astype(BF16), ws_ref[...])
        gate = jax.nn.sigmoid(gl_ref[...] + bg_ref[...])
        merged_ref[...] = (gate[:, :D_MODEL] * ua + gate[:, D_MODEL:] * ub).astype(BF16)
        oa_ref[...] = oa

    dil = pl.BlockSpec((t, DIL_W), lambda i: (i, 0))
    halves = pl.BlockSpec((2, t, 128), lambda i: (0, i, 0))
    const = lambda shape: pl.BlockSpec(shape, lambda i: (0, 0))
    return pl.pallas_call(
        body,
        name="merge_fwd",
        grid=(s // t,),
        in_specs=[halves] * 6 + [pl.BlockSpec((t, SB_W), lambda i: (i, 0)), pl.BlockSpec((t, GATE_W), lambda i: (i, 0)),
                                 const((1, GATE_W)), const((DIL_W, D_MODEL)), const((SB_W, D_MODEL))],
        out_specs=[pl.BlockSpec((t, D_MODEL), lambda i: (i, 0)), dil],
        out_shape=[jax.ShapeDtypeStruct((s, D_MODEL), BF16), jax.ShapeDtypeStruct((s, DIL_W), BF16)],
        compiler_params=_cparams(("parallel",)),
    )(*o_groups, *lse_groups, o_sb, gl, b_gate, w_up_dil, w_up_sb)


def _merge_bwd(dmerged, o_groups, lse_groups, o_sb, gl, b_gate, w_up_dil, w_up_sb, swap):
    s = gl.shape[0]
    t = MERGE_TILE
    n_chunks, r_swap, w_swap = swap.shape
    swap = swap.reshape(n_chunks, 2, r_swap // 2, w_swap)

    def body(dm_ref, o0, o1, o2, l0, l1, l2, ob_ref, gl_ref, bg_ref, wd_ref, ws_ref, swap_ref,
             dua_ref, dub_ref, dgl_ref, dbg_ref, dosb_ref, d0, d1, d2, c0, c1, c2, got_ref, send_sem, recv_sem):
        i = pl.program_id(0)

        @pl.when(i == 0)
        def _():
            _swap_copy(swap_ref, got_ref, send_sem, recv_sem).start()

        @pl.when(i == pl.num_programs(0) - 1)
        def _():
            _swap_copy(swap_ref, got_ref, send_sem, recv_sem).wait()

        rows = slice(None)
        og = [_load_halves(o, rows) for o in (o0, o1, o2)]
        w = _group_mix([_load_halves(l, rows) for l in (l0, l1, l2)])
        oa = (w[0] * og[0] + w[1] * og[1] + w[2] * og[2]).astype(BF16)
        ua = _dot_nn(oa, wd_ref[...])
        ub = _dot_nn(ob_ref[...].astype(BF16), ws_ref[...])
        gate = jax.nn.sigmoid(gl_ref[...] + bg_ref[...])
        ga, gb = gate[:, :D_MODEL], gate[:, D_MODEL:]
        dm = dm_ref[...]
        dua = (dm * ga).astype(BF16)
        dub = (dm * gb).astype(BF16)
        dua_ref[...] = dua
        dub_ref[...] = dub
        dgl_a = dm * ua * ga * (1.0 - ga)
        dgl_b = dm * ub * gb * (1.0 - gb)
        dgl_ref[:, :D_MODEL] = dgl_a.astype(BF16)
        dgl_ref[:, D_MODEL:] = dgl_b.astype(BF16)
        part = jnp.concatenate([jnp.sum(dgl_a.reshape(t // 8, 8, D_MODEL), axis=0),
                                jnp.sum(dgl_b.reshape(t // 8, 8, D_MODEL), axis=0)], axis=1)

        @pl.when(i == 0)
        def _():
            dbg_ref[...] = part

        @pl.when(i > 0)
        def _():
            dbg_ref[...] += part

        dosb_ref[...] = _dot_nt(dub, ws_ref[...])
        doa = _dot_nt(dua, wd_ref[...])
        rr = lax.broadcasted_iota(jnp.int32, (DIL_W, DIL_W), 0) // HEAD_DIM
        cc = lax.broadcasted_iota(jnp.int32, (DIL_W, DIL_W), 1) // HEAD_DIM
        same_head = (rr == cc).astype(BF16)
        dw = [_dot_f32_by_01(doa * og[g], same_head, 2) for g in range(3)]
        mean_dw = w[0] * dw[0] + w[1] * dw[1] + w[2] * dw[2]
        for g, (d_ref, c_ref) in enumerate(((d0, c0), (d1, c1), (d2, c2))):
            _store_halves(d_ref, rows, w[g] * doa)
            _store_halves(c_ref, rows, -w[g] * mean_dw)

    dil = pl.BlockSpec((2, t, 128), lambda i: (0, i, 0))
    wide = pl.BlockSpec((t, D_MODEL), lambda i: (i, 0))
    gate2 = pl.BlockSpec((t, GATE_W), lambda i: (i, 0))
    sbw = pl.BlockSpec((t, SB_W), lambda i: (i, 0))
    const = lambda shape: pl.BlockSpec(shape, lambda i: (0, 0))
    return pl.pallas_call(
        body,
        name="merge_bwd",
        grid=(s // t,),
        in_specs=[wide] + [dil] * 6 + [sbw, gate2, const((1, GATE_W)), const((DIL_W, D_MODEL)), const((SB_W, D_MODEL)),
                                       ANY],
        out_specs=[wide, wide, gate2, const((8, GATE_W)), sbw] + [dil] * 6 + [ANY],
        out_shape=[jax.ShapeDtypeStruct((s, D_MODEL), BF16), jax.ShapeDtypeStruct((s, D_MODEL), BF16),
                   jax.ShapeDtypeStruct((s, GATE_W), BF16), jax.ShapeDtypeStruct((8, GATE_W), F32),
                   jax.ShapeDtypeStruct((s, SB_W), F32)] + [jax.ShapeDtypeStruct((2, s, 128), F32)] * 6
        + [jax.ShapeDtypeStruct((n_chunks, r_swap // 2, w_swap), swap.dtype)],
        scratch_shapes=[pltpu.SemaphoreType.DMA, pltpu.SemaphoreType.DMA],
        compiler_params=_cparams(("arbitrary",)),
    )(dmerged, *o_groups, *lse_groups, o_sb, gl, b_gate, w_up_dil, w_up_sb, swap)


ANY = pl.BlockSpec(memory_space=pl.ANY)


def _place():
    x, y, c = lax.axis_index("x"), lax.axis_index("y"), lax.axis_index("c")
    other_chips = [(1 - x, y), (x, 1 - y), (1 - x, 1 - y)]
    return x, y, c, other_chips


def _gather_copies(p_ref, out_ref, send_sems, recv_sems):
    x, y, c, chips = _place()
    me, sibling = 2 * x + y, (x, y, 1 - c)
    idx = [2 * chip[0] + chip[1] for chip in chips]

    def copy(k, chip_idx, core, to, src=None):
        return pltpu.make_async_remote_copy(
            src_ref=out_ref.at[chip_idx, core] if src is None else src, dst_ref=out_ref.at[chip_idx, core],
            send_sem=send_sems.at[k], recv_sem=recv_sems.at[k], device_id=to, device_id_type=MESH)

    first = lambda j: copy(j, me, c, (*chips[j], c), src=p_ref.at[c])
    landed = lambda j: copy(j, idx[j], c, (x, y, c))
    passed = lambda j: copy(3 + j, idx[j], c, sibling)
    handed = lambda j: copy(3 + j, idx[j], 1 - c, (x, y, c))
    return first, landed, passed, handed


def _gather_start(*refs):
    first = _gather_copies(*refs)[0]
    for j in range(3):
        first(j).start()


def _gather_pass_on(*refs):
    _, landed, passed, _ = _gather_copies(*refs)
    for j in range(3):
        landed(j).wait_recv()
        passed(j).start()


def _gather_finish(*refs):
    first, _, passed, handed = _gather_copies(*refs)
    for j in range(3):
        handed(j).wait_recv()
    for j in range(3):
        first(j).wait_send()
        passed(j).wait_send()


def _fill_own_slot(others, pack):
    n, _, rh, wd = others.shape
    me = 2 * lax.axis_index("x") + lax.axis_index("y")
    mine = lax.broadcasted_iota(jnp.int32, (n, 1, 1, 1), 0) == me
    return jnp.where(mine, pack.reshape(1, 2, rh, wd), others).reshape(n, 2 * rh, wd)


def _swap_copy(g_ref, out_ref, send_sem, recv_sem):
    x, y, c, _ = _place()
    return pltpu.make_async_remote_copy(
        src_ref=g_ref.at[:, 1 - c], dst_ref=out_ref,
        send_sem=send_sem, recv_sem=recv_sem, device_id=(x, y, 1 - c), device_id_type=MESH)


def _swap_halves(g):
    n, r, wd = g.shape
    rh = r // 2
    g = g.reshape(n, 2, rh, wd)

    def body(g_ref, out_ref, send_sem, recv_sem):
        cp = _swap_copy(g_ref, out_ref, send_sem, recv_sem)
        cp.start()
        cp.wait()

    return pl.pallas_call(
        body,
        name="grad_swap_halves",
        in_specs=[ANY],
        out_specs=ANY,
        out_shape=jax.ShapeDtypeStruct((n, rh, wd), g.dtype),
        scratch_shapes=[pltpu.SemaphoreType.DMA, pltpu.SemaphoreType.DMA],
    )(g)


def _add_halves(g, got, core):
    n, r, wd = g.shape
    rh = r // 2
    t = rh // 4
    nt = rh // t

    def body(c_ref, a_ref, b_ref, o_ref):
        o_ref[...] = (a_ref[0] + b_ref[...]).astype(BF16)

    grid_spec = pltpu.PrefetchScalarGridSpec(
        num_scalar_prefetch=1,
        grid=(n, nt),
        in_specs=[pl.BlockSpec((1, 1, t, wd), lambda s, i, c: (s, c[0], i, 0)),
                  pl.BlockSpec((1, t, wd), lambda s, i, c: (s, i, 0))],
        out_specs=pl.BlockSpec((1, t, wd), lambda s, i, c: (s, i, 0)),
    )
    return pl.pallas_call(
        body,
        name="grad_add_halves",
        grid_spec=grid_spec,
        out_shape=jax.ShapeDtypeStruct((n, rh, wd), BF16),
        compiler_params=_cparams(("parallel", "parallel")),
    )(core, g.reshape(n, 2, rh, wd), got)


def _exchange_copies(h_ref, out_ref, send_sems, recv_sems):
    x, y, c, chips = _place()
    me = 2 * x + y

    def copy(j, slot):
        them = 2 * chips[j][0] + chips[j][1]
        return pltpu.make_async_remote_copy(
            src_ref=h_ref.at[them], dst_ref=out_ref.at[me if slot == "mine" else them],
            send_sem=send_sems.at[j], recv_sem=recv_sems.at[j], device_id=(*chips[j], c), device_id_type=MESH)

    return (lambda j: copy(j, "mine")), (lambda j: copy(j, "theirs"))


def _exchange_start(h_ref, out_ref, send_sems, recv_sems):
    send = _exchange_copies(h_ref, out_ref, send_sems, recv_sems)[0]
    for j in range(3):
        send(j).start()


def _exchange_wait(h_ref, out_ref, send_sems, recv_sems):
    send, arrival = _exchange_copies(h_ref, out_ref, send_sems, recv_sems)
    for j in range(3):
        arrival(j).wait_recv()
    for j in range(3):
        send(j).wait_send()


def _sum_chips(b, h, chip):
    n, rh, wd = b.shape
    t = rh // 4

    def body(chip_ref, b_ref, own_ref, o_ref):
        own = own_ref[0]
        s0, s1, s2, s3 = (jnp.where(chip_ref[0] == k, own, b_ref[k]).astype(F32) for k in range(n))
        o_ref[...] = ((s0 + s1) + s2) + s3

    grid_spec = pltpu.PrefetchScalarGridSpec(
        num_scalar_prefetch=1,
        grid=(rh // t,),
        in_specs=[pl.BlockSpec((n, t, wd), lambda i, chip: (0, i, 0)),
                  pl.BlockSpec((1, t, wd), lambda i, chip: (chip[0], i, 0))],
        out_specs=pl.BlockSpec((t, wd), lambda i, chip: (i, 0)),
    )
    return pl.pallas_call(
        body,
        name="grad_sum_chips",
        grid_spec=grid_spec,
        out_shape=jax.ShapeDtypeStruct((rh, wd), F32),
        compiler_params=_cparams(("parallel",)),
    )(chip, b, h)


def _join_halves(tc):
    rh, wd = tc.shape

    def body(t_ref, out_ref, send_sem, recv_sem):
        x, y, c, _ = _place()
        cp = pltpu.make_async_remote_copy(
            src_ref=t_ref, dst_ref=out_ref.at[c],
            send_sem=send_sem, recv_sem=recv_sem, device_id=(x, y, 1 - c), device_id_type=MESH)
        cp.start()
        cp.wait()

    halves = pl.pallas_call(
        body,
        name="grad_join_halves",
        in_specs=[ANY],
        out_specs=ANY,
        out_shape=jax.ShapeDtypeStruct((2, rh, wd), tc.dtype),
        scratch_shapes=[pltpu.SemaphoreType.DMA, pltpu.SemaphoreType.DMA],
    )(tc)
    return lax.dynamic_update_slice(halves, tc[None], (lax.axis_index("c"), 0, 0)).reshape(2 * rh, wd)


def _all_reduce_small(pack):
    rows, lanes = pack.shape

    def body(p_ref, out_ref, buf, send_sems, recv_sems):
        x, y, c, _ = _place()
        me = 4 * x + 2 * y + c
        buf[me] = p_ref[...]
        sends = []
        for k in range(1, N_DEV):
            peer = (x ^ (k >> 2), y ^ ((k >> 1) & 1), c ^ (k & 1))
            sends.append(pltpu.make_async_remote_copy(
                src_ref=p_ref, dst_ref=buf.at[me], send_sem=send_sems.at[k - 1], recv_sem=recv_sems.at[k - 1],
                device_id=peer, device_id_type=MESH))
        for cp in sends:
            cp.start()
        for k in range(1, N_DEV):
            pltpu.make_async_remote_copy(
                src_ref=p_ref, dst_ref=buf.at[me ^ k], send_sem=send_sems.at[k - 1], recv_sem=recv_sems.at[k - 1],
                device_id=(x, y, c), device_id_type=MESH).wait_recv()
        for cp in sends:
            cp.wait_send()
        total = buf[0]
        for d in range(1, N_DEV):
            total = total + buf[d]
        out_ref[...] = total

    vm = pl.BlockSpec(memory_space=pltpu.VMEM)
    return pl.pallas_call(
        body,
        name="all_reduce_small",
        in_specs=[vm],
        out_specs=vm,
        out_shape=jax.ShapeDtypeStruct((rows, lanes), F32),
        scratch_shapes=[pltpu.VMEM((N_DEV, rows, lanes), F32), pltpu.SemaphoreType.DMA((N_DEV - 1,)),
                        pltpu.SemaphoreType.DMA((N_DEV - 1,))],
    )(pack)


def _adamw(g, w, m, v, name):
    rows, cols = g.shape
    t = rows
    for cand in (256, 128, 64, 32, 16, 8):
        if rows % cand == 0:
            t = cand
            break

    def body(g_ref, w_ref, m_ref, v_ref, d_ref, nm_ref, nv_ref):
        gv = g_ref[...]
        mv = ADAM_B1 * m_ref[...] + (1.0 - ADAM_B1) * gv
        vv = ADAM_B2 * v_ref[...] + (1.0 - ADAM_B2) * (gv * gv)
        m_hat = mv / (1.0 - ADAM_B1 ** ADAM_STEP)
        v_hat = vv / (1.0 - ADAM_B2 ** ADAM_STEP)
        d_ref[...] = -ADAM_LR * (m_hat / (jnp.sqrt(v_hat) + ADAM_EPS) + ADAM_WD * w_ref[...])
        nm_ref[...] = mv
        nv_ref[...] = vv

    blk = pl.BlockSpec((t, cols), lambda i: (i, 0))
    return pl.pallas_call(
        body,
        name=name,
        grid=(rows // t,),
        in_specs=[blk] * 4,
        out_specs=[blk] * 3,
        out_shape=[jax.ShapeDtypeStruct((rows, cols), F32)] * 3,
        compiler_params=_cparams(("parallel",)),
    )(g, w, m, v)


PACK_W = 1024
BIG = (("w_in", (D_MODEL, IN_COLS), 1), ("w_up_dil", (DIL_W, D_MODEL), 1), ("w_up_sb", (SB_W, D_MODEL), 1),
       ("w_out", (D_MODEL, D_MODEL), 0), ("w_mlp_in", (D_MODEL, D_FF), 1), ("w_mlp_out", (D_FF, D_MODEL), 0))


def _shard_shape(shape, axis):
    return tuple(d // N_CHIPS if a == axis else d for a, d in enumerate(shape))


MIXER_GROUP, MLP_GROUP = BIG[:4], BIG[4:]
LATE_WEIGHTS = BIG[1:]


def _pack_rows(group=BIG):
    rows, at = {}, 0
    for name, shape, axis in group:
        n = math.prod(_shard_shape(shape, axis)) // PACK_W
        rows[name] = (at, n)
        at += n
    return rows, at


def _pack_shards(shards, group):
    return jnp.concatenate([shards[name].reshape(-1, PACK_W) for name, _, _ in group], axis=0)


def _unpack_full(gathered, group):
    rows, _ = _pack_rows(group)
    full = {}
    for name, shape, axis in group:
        at, n = rows[name]
        parts = gathered[:, at:at + n, :].reshape((N_CHIPS,) + _shard_shape(shape, axis))
        if axis == 0:
            full[name] = parts.reshape(shape)
        else:
            full[name] = jnp.transpose(parts, (1, 0, 2)).reshape(shape)
    return full


def _pack_full_grads(grads, group):
    chunks = []
    for name, shape, axis in group:
        g = grads[name]
        if axis == 0:
            parts = g.reshape((N_CHIPS, shape[0] // N_CHIPS, shape[1]))
        else:
            parts = jnp.transpose(g.reshape((shape[0], N_CHIPS, shape[1] // N_CHIPS)), (1, 0, 2))
        chunks.append(parts.reshape(N_CHIPS, -1, PACK_W))
    return jnp.concatenate(chunks, axis=1)


def _unpack_shard(packed, group):
    rows, _ = _pack_rows(group)
    return {name: packed[rows[name][0]:rows[name][0] + rows[name][1]].reshape(_shard_shape(shape, axis))
            for name, shape, axis in group}


def _local_step(x, target, early_shards, late_shards, norm_mix_g, b_gate, norm_mlp_g, norm_final_g, core):
    h, early = _rms_fwd_and_gather(x, norm_mix_g, early_shards)
    w = {"w_in": jnp.transpose(early, (1, 0, 2)).reshape(D_MODEL, IN_COLS)}
    w_in = w["w_in"]
    sb0 = 9 * DIL_W
    w_sb, w_gate = w_in[:, sb0:QKV_W], w_in[:, QKV_W:]
    w_dil = [jnp.concatenate([w_in[:, (3 * i + g) * DIL_W:(3 * i + g + 1) * DIL_W] for i in range(3)], axis=1)
             for g in range(3)]

    qkv_dil = [_matmul(h, w_dil[g], mode="nn", out_dtypes=(BF16,), name=f"proj_dil_g{g}", tn=768)[0] for g in range(3)]
    (qkv_sb,) = _matmul(h, w_sb, mode="nn", out_dtypes=(BF16,), name="proj_sb", tn=768)
    (gl,) = _matmul(h, w_gate, mode="nn", out_dtypes=(F32,), name="proj_gate")
    dil = [_dil_fwd(qkv_dil[g], g) for g in range(3)]
    o_groups, lse_groups = [d[0] for d in dil], [d[1] for d in dil]
    o_sb, a_sb, late_others = _sb_fwd(qkv_sb, late_shards)
    w = {**w, **_unpack_full(_fill_own_slot(late_others, late_shards), LATE_WEIGHTS)}
    merged, o_a = _merge_fwd(o_groups, lse_groups, o_sb, gl, b_gate, w["w_up_dil"], w["w_up_sb"])
    def residual_and_norm(acc, res, g):
        x1 = res + acc
        return x1, _rms_rows(x1)[0] * g

    x1, h2 = _matmul(merged, w["w_out"], mode="nn", out_dtypes=(F32, BF16), name="out_proj", tm=ROW_TILE,
                     extras=(x, norm_mlp_g), epilogue=residual_and_norm)
    u, act = _matmul(h2, w["w_mlp_in"], mode="nn", out_dtypes=(BF16, BF16), name="mlp_in",
                     epilogue=lambda acc: (acc, jnp.square(jnp.maximum(acc, 0.0))))

    def residual_and_loss(acc, res, tgt, g):
        xh, r = _rms_rows(res + acc)
        err = xh * g - tgt
        dy = err * (1.0 / D_MODEL)
        dxh = dy * g
        dx2 = r * (dxh - xh * jnp.mean(dxh * xh, axis=-1, keepdims=True))
        return dx2, _rows_sum8(dy * xh), (0.5 / D_MODEL) * _rows_sum8(err * err)

    dx2, dg_final, loss_part = _matmul(
        act, w["w_mlp_out"], mode="nn", out_dtypes=(F32, ("part", F32), ("part", F32)), name="mlp_out", tm=ROW_TILE,
        tk=2048, extras=(x1, target, norm_final_g.reshape(1, D_MODEL)), epilogue=residual_and_loss)

    (du,) = _matmul(dx2, w["w_mlp_out"], mode="nt", out_dtypes=(BF16,), name="mlp_out_dx",
                    extras=(u,), epilogue=lambda acc, uu: (acc * (2.0 * jnp.maximum(uu.astype(F32), 0.0)),))
    pack_shape = (N_CHIPS, 2 * D_MODEL, D_MODEL)
    (half_pack,) = _matmul(act, dx2, mode="tn", out_dtypes=(F32,), name="mlp_out_dw",
                           into=(pack_shape, lambda i, j: (i, 1, 0), None))
    (mlp_pack,) = _matmul(h2, du, mode="tn", out_dtypes=(F32,), name="mlp_in_dw",
                          into=(pack_shape, lambda i, j: (j, 0, 0), half_pack))

    def norm_bwd(acc, xx, dres, g):
        dx, dg = _rms_bwd_rows(acc, xx, g)
        return dres + dx, dg

    dx1, dg_mlp = _matmul(du, w["w_mlp_in"], mode="nt", out_dtypes=(F32, ("part", F32)), name="mlp_in_dx",
                          tm=ROW_TILE, tk=2048, extras=(x1, dx2, norm_mlp_g), epilogue=norm_bwd)

    (dmerged,) = _matmul(dx1, w["w_out"], mode="nt", out_dtypes=(F32,), name="out_proj_dx")
    (g_out,) = _matmul(merged, dx1, mode="tn", out_dtypes=(F32,), name="out_proj_dw")
    mb = _merge_bwd(dmerged, o_groups, lse_groups, o_sb, gl, b_gate, w["w_up_dil"], w["w_up_sb"], mlp_pack)
    dua, dub, dgl, dbg, do_sb = mb[:5]
    do_groups, c_groups = mb[5:8], mb[8:11]
    mlp_sums = _add_halves(mlp_pack, mb[11], core)
    (g_up_dil,) = _matmul(o_a, dua, mode="tn", out_dtypes=(F32,), name="up_dil_dw")
    (g_up_sb,) = _matmul(o_sb, dub, mode="tn", out_dtypes=(F32,), name="up_sb_dw")
    dq_sb, dk_sb, dv_sb, mlp_got = _sb_bwd(qkv_sb, do_sb, a_sb, mlp_sums)
    dil_b = [_dil_bwd(qkv_dil[g], do_groups[g], lse_groups[g], c_groups[g], g) for g in range(3)]
    dproj = jnp.concatenate(
        [dil_b[g][i].astype(BF16) for i in range(3) for g in range(3)]
        + [t.astype(BF16) for t in (dq_sb, dk_sb, dv_sb)] + [dgl], axis=1)
    (g_in,) = _matmul(h, dproj, mode="tn", out_dtypes=(F32,), name="proj_dw", tm=512, tn=IN_COLS // 2)
    mixer_pack = _pack_full_grads({"w_in": g_in, "w_up_dil": g_up_dil, "w_up_sb": g_up_sb, "w_out": g_out}, MIXER_GROUP)
    mixer_sums = _add_halves(mixer_pack, _swap_halves(mixer_pack), core)
    dh, mixer_got = _matmul(dproj, w["w_in"], mode="nt", out_dtypes=(F32,), name="proj_dx", tk=IN_COLS // 2,
                            exchange=mixer_sums)
    grad_x, dg_mix = _rms_bwd_residual(dh, x, norm_mix_g, dx1)

    small = (dg_mix, dbg, dg_mlp, dg_final, loss_part)
    return grad_x, (mixer_got, mixer_sums), (mlp_got, mlp_sums), small


def kernel(x, norm_mix_g, w_in, b_gate, w_up_dil, w_up_sb, w_out, norm_mlp_g, w_mlp_in, w_mlp_out, norm_final_g, loss_target, m_norm_mix_g, m_w_in, m_b_gate, m_w_up_dil, m_w_up_sb, m_w_out, m_norm_mlp_g, m_w_mlp_in, m_w_mlp_out, m_norm_final_g, v_norm_mix_g, v_w_in, v_b_gate, v_w_up_dil, v_w_up_sb, v_w_out, v_norm_mlp_g, v_w_mlp_in, v_w_mlp_out, v_norm_final_g):
    shards = {"w_in": w_in[0], "w_up_dil": w_up_dil[0], "w_up_sb": w_up_sb[0], "w_out": w_out[0],
              "w_mlp_in": w_mlp_in[0], "w_mlp_out": w_mlp_out[0]}
    moments_m = {"w_in": m_w_in[0], "w_up_dil": m_w_up_dil[0], "w_up_sb": m_w_up_sb[0], "w_out": m_w_out[0],
                 "w_mlp_in": m_w_mlp_in[0], "w_mlp_out": m_w_mlp_out[0]}
    moments_v = {"w_in": v_w_in[0], "w_up_dil": v_w_up_dil[0], "w_up_sb": v_w_up_sb[0], "w_out": v_w_out[0],
                 "w_mlp_in": v_w_mlp_in[0], "w_mlp_out": v_w_mlp_out[0]}

    shards16 = {n: s.astype(BF16) for n, s in shards.items()}
    early_shards = shards16["w_in"]
    late_shards = _pack_shards(shards16, LATE_WEIGHTS)

    core = lax.axis_index("c").astype(jnp.int32).reshape(1)
    chip = (2 * lax.axis_index("x") + lax.axis_index("y")).astype(jnp.int32).reshape(1)
    grad_x, (mixer_got, mixer_sums), (mlp_got, mlp_sums), small = _local_step(
        x[0], loss_target[0], early_shards, late_shards, norm_mix_g, b_gate, norm_mlp_g, norm_final_g, core)

    reduced = _join_halves(_sum_chips(mixer_got, mixer_sums, chip))
    reduced_mlp = _join_halves(_sum_chips(mlp_got, mlp_sums, chip))
    g_shard = {**_unpack_shard(reduced, MIXER_GROUP), **_unpack_shard(reduced_mlp, MLP_GROUP)}

    dg_mix, dbg, dg_mlp, dg_final, loss_part = small
    loss_row = jnp.sum(loss_part, axis=0, keepdims=True)
    small_pack = jnp.concatenate(
        [jnp.sum(dg_mix, axis=0, keepdims=True), jnp.sum(dbg, axis=0, keepdims=True),
         jnp.sum(dg_mlp, axis=0, keepdims=True), jnp.sum(dg_final, axis=0, keepdims=True), loss_row], axis=1)
    n_small = small_pack.shape[1]
    small_sum = _all_reduce_small(small_pack.reshape(n_small // 128, 128)).reshape(1, n_small)
    g_norm_mix = small_sum[:, :D_MODEL]
    g_b_gate = small_sum[:, D_MODEL:3 * D_MODEL]
    g_norm_mlp = small_sum[:, 3 * D_MODEL:4 * D_MODEL]
    g_norm_final = small_sum[:, 4 * D_MODEL:5 * D_MODEL]
    loss = jnp.sum(small_sum[:, 5 * D_MODEL:])

    names = ["norm_mix_g", "w_in", "b_gate", "w_up_dil", "w_up_sb", "w_out", "norm_mlp_g", "w_mlp_in", "w_mlp_out",
             "norm_final_g"]
    grads = dict(g_shard)
    grads.update(norm_mix_g=g_norm_mix, b_gate=g_b_gate, norm_mlp_g=g_norm_mlp, norm_final_g=g_norm_final)
    weights = dict(shards)
    weights.update(norm_mix_g=norm_mix_g, b_gate=b_gate, norm_mlp_g=norm_mlp_g, norm_final_g=norm_final_g.reshape(1, D_MODEL))
    ms = dict(moments_m)
    ms.update(norm_mix_g=m_norm_mix_g, b_gate=m_b_gate, norm_mlp_g=m_norm_mlp_g, norm_final_g=m_norm_final_g.reshape(1, D_MODEL))
    vs = dict(moments_v)
    vs.update(norm_mix_g=v_norm_mix_g, b_gate=v_b_gate, norm_mlp_g=v_norm_mlp_g, norm_final_g=v_norm_final_g.reshape(1, D_MODEL))

    out_shapes = {"norm_mix_g": norm_mix_g.shape, "w_in": w_in.shape, "b_gate": b_gate.shape, "w_up_dil": w_up_dil.shape,
                  "w_up_sb": w_up_sb.shape, "w_out": w_out.shape, "norm_mlp_g": norm_mlp_g.shape,
                  "w_mlp_in": w_mlp_in.shape, "w_mlp_out": w_mlp_out.shape, "norm_final_g": norm_final_g.shape}
    g_out, d_out, m_out, v_out = [], [], [], []
    for n in names:
        d, nm, nv = _adamw(grads[n], weights[n], ms[n], vs[n], "adamw_" + n)
        shape = out_shapes[n]
        g_out.append(grads[n].reshape(shape))
        d_out.append(d.reshape(shape))
        m_out.append(nm.reshape(shape))
        v_out.append(nv.reshape(shape))
    return (loss, grad_x.reshape(x.shape), *g_out, *d_out, *m_out, *v_out)
```

```python
import functools
import math

import jax
import jax.numpy as jnp
import numpy as np
from jax import lax
from jax.experimental import pallas as pl
from jax.experimental.pallas import tpu as pltpu

F32 = jnp.float32
BF16 = jnp.bfloat16
MESH = pl.DeviceIdType.MESH

D_MODEL = 1024
HEAD_DIM = 64
DIL_GROUPS = ((128, 1), (512, 4), (2048, 16))
DIL_HEADS = 4
DIL_W = 256
N_DIL_HEADS = 12
SB_HEADS = 8
SB_W = SB_HEADS * HEAD_DIM
QKV_W = 3 * 3 * DIL_W + 3 * SB_W
GATE_W = 2 * D_MODEL
IN_COLS = QKV_W + GATE_W
D_FF = 4 * D_MODEL
BLOCK = 128
RMS_EPS = 1e-6
NEG_INF = -1e30
N_CHIPS = 4
N_DEV = 8

ADAM_LR = 0.001
ADAM_B1 = 0.9
ADAM_B2 = 0.999
ADAM_EPS = 1e-08
ADAM_WD = 0.01
ADAM_STEP = 10

VMEM_LIMIT = 56 * 1024 * 1024

SB_BQ = 256
SB_BK = 256


def _cparams(sem=None):
    if sem is None:
        return pltpu.CompilerParams(vmem_limit_bytes=VMEM_LIMIT)
    return pltpu.CompilerParams(dimension_semantics=sem, vmem_limit_bytes=VMEM_LIMIT)


def _dot(a, b, dims):
    return lax.dot_general(a, b, (dims, ((), ())), preferred_element_type=F32)


def _dot_nn(a, b):
    return _dot(a, b, ((1,), (0,)))


def _dot_nt(a, b):
    return _dot(a, b, ((1,), (1,)))


def _dot_tn(a, b):
    return _dot(a, b, ((0,), (0,)))


def _dot_f32_by_01(x, m01, pieces=3):
    hi = x.astype(BF16)
    if pieces == 1:
        return _dot_nn(hi, m01)
    r1 = x - hi.astype(F32)
    mid = r1.astype(BF16)
    if pieces == 2:
        return _dot_nn(hi, m01) + _dot_nn(mid, m01)
    lo = (r1 - mid.astype(F32)).astype(BF16)
    return _dot_nn(hi, m01) + _dot_nn(mid, m01) + _dot_nn(lo, m01)


def _matmul(a, b, *, mode, out_dtypes, name, tm=1024, tn=1024, tk=1024, extras=(), epilogue=None, exchange=None,
            into=None):
    if mode == "nn":
        (m, k), (k2, n) = a.shape, b.shape
    elif mode == "nt":
        (m, k), (n, k2) = a.shape, b.shape
    else:
        (k, m), (k2, n) = a.shape, b.shape
    assert k == k2, (a.shape, b.shape, mode)
    tm, tn, tk = min(tm, m), min(tn, n), min(tk, k)
    assert m % tm == 0 and n % tn == 0 and k % tk == 0, (m, n, k, tm, tn, tk)
    nk = k // tk
    n_out = len(out_dtypes)
    n_ex = len(extras)

    if mode == "nn":
        a_spec = pl.BlockSpec((tm, tk), lambda i, j, kk: (i, kk))
        b_spec = pl.BlockSpec((tk, tn), lambda i, j, kk: (kk, j))
        dot = _dot_nn
    elif mode == "nt":
        a_spec = pl.BlockSpec((tm, tk), lambda i, j, kk: (i, kk))
        b_spec = pl.BlockSpec((tn, tk), lambda i, j, kk: (j, kk))
        dot = _dot_nt
    else:
        a_spec = pl.BlockSpec((tk, tm), lambda i, j, kk: (kk, i))
        b_spec = pl.BlockSpec((tk, tn), lambda i, j, kk: (kk, j))
        dot = _dot_tn
    mn_spec = pl.BlockSpec((tm, tn), lambda i, j, kk: (i, j))
    row_spec = pl.BlockSpec((1, tn), lambda i, j, kk: (0, j))
    part_spec = pl.BlockSpec((8, tn), lambda i, j, kk: (i, j))
    ex_specs = [row_spec if e.shape[0] == 1 else mn_spec for e in extras]
    is_part = [isinstance(dt, tuple) for dt in out_dtypes]
    out_dts = [dt[1] if p else dt for dt, p in zip(out_dtypes, is_part)]
    out_specs = [part_spec if p else mn_spec for p in is_part]
    out_shapes = [jax.ShapeDtypeStruct((8 * (m // tm), n) if p else (m, n), dt) for dt, p in zip(out_dts, is_part)]

    n_side = 0 if exchange is None else 1
    grid = (m // tm, n // tn, nk)
    prior = []
    if into is not None:
        assert n_out == 1 and not extras and exchange is None
        into_shape, into_map, into_prior = into
        out_specs = [pl.BlockSpec((1, tm, tn), lambda i, j, kk: into_map(i, j))]
        out_shapes = [jax.ShapeDtypeStruct(into_shape, out_dts[0])]
        prior = [] if into_prior is None else [into_prior]

    def body(*refs):
        a_ref, b_ref = refs[0], refs[1]
        ex_refs = refs[2:2 + n_ex]
        n_in = 2 + n_ex + n_side + len(prior)
        out_refs = refs[n_in:n_in + n_out]
        scratch = refs[n_in + n_out + n_side:]
        acc_ref = scratch[0] if nk > 1 else None
        if exchange is not None:
            side = (refs[2 + n_ex], refs[2 + n_ex + n_side + n_out]) + tuple(scratch[-2:])
            step = (pl.program_id(0) * grid[1] + pl.program_id(1)) * grid[2] + pl.program_id(2)

            @pl.when(step == 0)
            def _():
                _exchange_start(*side)

            @pl.when(step == grid[0] * grid[1] * grid[2] - 1)
            def _():
                _exchange_wait(*side)

        part = dot(a_ref[...].astype(BF16), b_ref[...].astype(BF16))

        def finish(acc):
            if epilogue is None:
                outs = (acc,)
            else:
                outs = epilogue(acc, *[r[...] for r in ex_refs])
            for o_ref, o in zip(out_refs, outs):
                if into is None:
                    o_ref[...] = o.astype(o_ref.dtype)
                else:
                    o_ref[0] = o.astype(o_ref.dtype)

        if nk == 1:
            finish(part)
        else:
            kk = pl.program_id(2)

            @pl.when(kk == 0)
            def _():
                acc_ref[...] = part

            @pl.when(kk > 0)
            def _():
                acc_ref[...] += part

            @pl.when(kk == nk - 1)
            def _():
                finish(acc_ref[...])

    side_in = [] if exchange is None else [exchange]
    outs = pl.pallas_call(
        body,
        name=name,
        grid=grid,
        in_specs=[a_spec, b_spec] + ex_specs + [ANY] * (n_side + len(prior)),
        out_specs=out_specs + [ANY] * n_side,
        out_shape=out_shapes + [jax.ShapeDtypeStruct(e.shape, e.dtype) for e in side_in],
        scratch_shapes=([pltpu.VMEM((tm, tn), F32)] if nk > 1 else [])
        + [pltpu.SemaphoreType.DMA((3,)), pltpu.SemaphoreType.DMA((3,))] * n_side,
        input_output_aliases={2: 0} if prior else {},
        compiler_params=_cparams(("arbitrary",) * 3 if n_side else ("parallel", "parallel", "arbitrary")),
    )(a, b, *extras, *side_in, *prior)
    return outs


ROW_TILE = 512


def _rows_sum8(t):
    rows, d = t.shape
    return jnp.sum(t.reshape(rows // 8, 8, d), axis=0)


def _rms_rows(x):
    r = lax.rsqrt(jnp.mean(x * x, axis=-1, keepdims=True) + RMS_EPS)
    return x * r, r


def _rms_bwd_rows(dh, x, g):
    xh, r = _rms_rows(x)
    dxh = dh * g
    return r * (dxh - xh * jnp.mean(dxh * xh, axis=-1, keepdims=True)), _rows_sum8(dh * xh)


def _rms_bwd_residual(dh, x, g, dres):
    s, d = x.shape

    def body(dh_ref, x_ref, g_ref, dres_ref, dx_ref, dg_ref):
        dx, dg = _rms_bwd_rows(dh_ref[...], x_ref[...], g_ref[...])
        dx_ref[...] = dres_ref[...] + dx
        dg_ref[...] = dg

    row = pl.BlockSpec((ROW_TILE, d), lambda i: (i, 0))
    return pl.pallas_call(
        body,
        name="norm_mix_bwd",
        grid=(s // ROW_TILE,),
        in_specs=[row, row, pl.BlockSpec((1, d), lambda i: (0, 0)), row],
        out_specs=[row, pl.BlockSpec((8, d), lambda i: (i, 0))],
        out_shape=[jax.ShapeDtypeStruct((s, d), F32), jax.ShapeDtypeStruct((8 * (s // ROW_TILE), d), F32)],
        compiler_params=_cparams(("parallel",)),
    )(dh, x, g, dres)


def _rms_fwd_and_gather(x, g, shard_pack):
    s, d = x.shape
    r_pack, w_pack = shard_pack.shape
    steps = s // ROW_TILE

    def body(x_ref, g_ref, pack_ref, h_ref, others_ref, send_sems, recv_sems):
        i = pl.program_id(0)
        gather = (pack_ref, others_ref, send_sems, recv_sems)

        @pl.when(i == 0)
        def _():
            _gather_start(*gather)

        h_ref[...] = (_rms_rows(x_ref[...])[0] * g_ref[...]).astype(BF16)

        @pl.when(i == steps - 1)
        def _():
            _gather_pass_on(*gather)
            _gather_finish(*gather)

    h, others = pl.pallas_call(
        body,
        name="norm_mix",
        grid=(steps,),
        in_specs=[pl.BlockSpec((ROW_TILE, d), lambda i: (i, 0)), pl.BlockSpec((1, d), lambda i: (0, 0)), ANY],
        out_specs=[pl.BlockSpec((ROW_TILE, d), lambda i: (i, 0)), ANY],
        out_shape=[jax.ShapeDtypeStruct((s, d), BF16),
                   jax.ShapeDtypeStruct((N_CHIPS, 2, r_pack // 2, w_pack), shard_pack.dtype)],
        scratch_shapes=[pltpu.SemaphoreType.DMA((6,)), pltpu.SemaphoreType.DMA((6,))],
        compiler_params=_cparams(("arbitrary",)),
    )(x, g, shard_pack.reshape(2, r_pack // 2, w_pack))
    return h, _fill_own_slot(others, shard_pack)


def _alibi_slopes():
    return np.exp2(np.float32(-8.0) * np.arange(1, N_DIL_HEADS + 1, dtype=np.float32) / np.float32(N_DIL_HEADS))


def _head_lane_mask(h, rows):
    lane = lax.broadcasted_iota(jnp.int32, (rows, DIL_W), 1)
    return (lane >= h * HEAD_DIM) & (lane < (h + 1) * HEAD_DIM)


def _band_terms(dil, has_prev):
    qi = lax.broadcasted_iota(jnp.int32, (BLOCK, 2 * BLOCK), 0)
    kj = lax.broadcasted_iota(jnp.int32, (BLOCK, 2 * BLOCK), 1)
    steps = qi + BLOCK - kj
    valid = (steps >= 0) & (steps <= BLOCK) & ((kj >= BLOCK) | has_prev)
    return valid, steps.astype(F32) * float(dil)


def _load_halves(ref, rows):
    return jnp.concatenate([ref[0, rows, :], ref[1, rows, :]], axis=1)


def _store_halves(ref, rows, value):
    ref[0, rows, :] = value[:, :128]
    ref[1, rows, :] = value[:, 128:]


def _dil_fwd_all(qkv_groups):
    s = qkv_groups[0].shape[0]
    steps = s // BLOCK
    n_groups = len(DIL_GROUPS)
    dils = [d for _, d in DIL_GROUPS]
    views = [qkv_groups[g].reshape(s // dils[g], dils[g] * 3 * DIL_W) for g in range(n_groups)]
    slopes = _alibi_slopes()

    def spec(dil, which, prev):
        def index(t):
            n = t // dil
            return (jnp.maximum(n - 1, 0) if prev else n, (t % dil) * 3 + which)
        return pl.BlockSpec((BLOCK, DIL_W), index)

    def body(*refs):
        t = pl.program_id(0)
        ins, outs = refs[:5 * n_groups], refs[5 * n_groups:]
        masks = [_head_lane_mask(h, BLOCK) for h in range(DIL_HEADS)]
        work = []
        for g in range(n_groups):
            q_ref, kc_ref, kp_ref, vc_ref, vp_ref = ins[5 * g:5 * g + 5]
            q = q_ref[...]
            k2 = jnp.concatenate([kp_ref[...], kc_ref[...]], axis=0)
            v2 = jnp.concatenate([vp_ref[...], vc_ref[...]], axis=0)
            logits = [_dot_nt(jnp.where(masks[h], q, jnp.zeros_like(q)), k2) for h in range(DIL_HEADS)]
            work.append((v2, logits))
        probs = []
        for g in range(n_groups):
            valid, dist = _band_terms(dils[g], t // dils[g] > 0)
            ps, lses = [], []
            for h in range(DIL_HEADS):
                slope = float(slopes[g * DIL_HEADS + h])
                lg = jnp.where(valid, work[g][1][h] * 0.125 - slope * dist, NEG_INF)
                mx = jnp.max(lg, axis=1, keepdims=True)
                lse = mx + jnp.log(jnp.sum(jnp.exp(lg - mx), axis=1, keepdims=True))
                ps.append(jnp.exp(lg - lse).astype(BF16))
                lses.append(lse)
            probs.append((ps, lses))
        for g in range(n_groups):
            dil = dils[g]
            mine = pl.ds(t % dil, BLOCK, stride=dil) if dil > 1 else slice(None)
            o_acc = jnp.zeros((BLOCK, DIL_W), F32)
            lse_acc = jnp.zeros((BLOCK, DIL_W), F32)
            for h in range(DIL_HEADS):
                o_acc = jnp.where(masks[h], _dot_nn(probs[g][0][h], work[g][0]), o_acc)
                lse_acc = jnp.where(masks[h], probs[g][1][h], lse_acc)
            _store_halves(outs[2 * g], mine, o_acc)
            _store_halves(outs[2 * g + 1], mine, lse_acc)

    in_specs, out_specs, operands = [], [], []
    for g, dil in enumerate(dils):
        in_specs += [spec(dil, 0, False), spec(dil, 1, False), spec(dil, 1, True), spec(dil, 2, False),
                     spec(dil, 2, True)]
        out_specs += [pl.BlockSpec((2, BLOCK * dil, 128), functools.partial(lambda d, t: (0, t // d, 0), dil))] * 2
        operands += [views[g]] * 5
    res = pl.pallas_call(
        body,
        name="dil_fwd",
        grid=(steps,),
        in_specs=in_specs,
        out_specs=out_specs,
        out_shape=[jax.ShapeDtypeStruct((2, s, 128), F32)] * (2 * n_groups),
        compiler_params=_cparams(("arbitrary",)),
    )(*operands)
    return [(res[2 * g], res[2 * g + 1]) for g in range(n_groups)]


def _dil_bwd(qkv, do, lse, cterm, group):
    _, dil = DIL_GROUPS[group]
    s = qkv.shape[0]
    sub = s // dil
    nb = sub // BLOCK
    view = qkv.reshape(sub, dil * 3 * DIL_W)
    slopes = _alibi_slopes()[group * DIL_HEADS:(group + 1) * DIL_HEADS]

    def col(which, shift):
        if shift == 0:
            return lambda n, r: (n, r * 3 + which)
        if shift < 0:
            return lambda n, r: (jnp.maximum(n - 1, 0), r * 3 + which)
        return lambda n, r: (jnp.minimum(n + 1, nb - 1), r * 3 + which)

    def own(shift):
        if shift == 0:
            return pl.BlockSpec((2, BLOCK * dil, 128), lambda n, r: (0, n, 0))
        return pl.BlockSpec((2, BLOCK * dil, 128), lambda n, r: (0, jnp.minimum(n + 1, nb - 1), 0))

    def body(q_ref, qn_ref, kc_ref, kp_ref, vc_ref, vp_ref, do_ref, don_ref, lse_ref, lsen_ref, c_ref, cn_ref,
             dq_ref, dk_ref, dv_ref):
        n, r = pl.program_id(0), pl.program_id(1)
        mine = pl.ds(r, BLOCK, stride=dil) if dil > 1 else slice(None)
        valid, dist = _band_terms(dil, n > 0)
        valid_n = _band_terms(dil, True)[0][:, :BLOCK] & (n < nb - 1)
        dist_n = dist[:, :BLOCK]
        q, qn = q_ref[...], qn_ref[...]
        kc, vc = kc_ref[...], vc_ref[...]
        k2 = jnp.concatenate([kp_ref[...], kc], axis=0)
        v2 = jnp.concatenate([vp_ref[...], vc], axis=0)
        dov, donv = _load_halves(do_ref, mine), _load_halves(don_ref, mine)
        lsev, lsenv = _load_halves(lse_ref, mine), _load_halves(lsen_ref, mine)
        cv, cnv = _load_halves(c_ref, mine), _load_halves(cn_ref, mine)
        masks = [_head_lane_mask(h, BLOCK) for h in range(DIL_HEADS)]

        def head_col(t, hm):
            return jnp.max(jnp.where(hm, t, NEG_INF), axis=1, keepdims=True)

        qhs = [jnp.where(hm, q, jnp.zeros_like(q)) for hm in masks]
        qnhs = [jnp.where(hm, qn, jnp.zeros_like(qn)) for hm in masks]
        dohs = [jnp.where(hm, dov, 0.0).astype(BF16) for hm in masks]
        donhs = [jnp.where(hm, donv, 0.0).astype(BF16) for hm in masks]
        logit = [_dot_nt(qhs[h], k2) for h in range(DIL_HEADS)]
        dp = [_dot_nt(dohs[h], v2) for h in range(DIL_HEADS)]
        logit_n = [_dot_nt(qnhs[h], kc) for h in range(DIL_HEADS)]
        dp_n = [_dot_nt(donhs[h], vc) for h in range(DIL_HEADS)]
        p16, dlog, pn16, dlog_n = [], [], [], []
        for h in range(DIL_HEADS):
            hm, slope = masks[h], float(slopes[h])
            p = jnp.where(valid, jnp.exp(logit[h] * 0.125 - slope * dist - head_col(lsev, hm)), 0.0)
            dlog.append((p * (dp[h] + head_col(cv, hm)) * 0.125).astype(BF16))
            p16.append(p.astype(BF16))
            pn = jnp.where(valid_n, jnp.exp(logit_n[h] * 0.125 - slope * dist_n - head_col(lsenv, hm)), 0.0)
            dlog_n.append((pn * (dp_n[h] + head_col(cnv, hm)) * 0.125).astype(BF16))
            pn16.append(pn.astype(BF16))
        dq_acc = jnp.zeros((BLOCK, DIL_W), F32)
        dk_acc = jnp.zeros((BLOCK, DIL_W), F32)
        dv_acc = jnp.zeros((BLOCK, DIL_W), F32)
        for h in range(DIL_HEADS):
            dq_acc = jnp.where(masks[h], _dot_nn(dlog[h], k2), dq_acc)
            dk_acc += _dot_tn(dlog[h][:, BLOCK:], qhs[h]) + _dot_tn(dlog_n[h], qnhs[h])
            dv_acc += _dot_tn(p16[h][:, BLOCK:], dohs[h]) + _dot_tn(pn16[h], donhs[h])
        dq_ref[...] = dq_acc.astype(BF16)
        dk_ref[...] = dk_acc.astype(BF16)
        dv_ref[...] = dv_acc.astype(BF16)

    blk = (BLOCK, DIL_W)
    outs = pl.pallas_call(
        body,
        name=f"dil_bwd_g{group}",
        grid=(nb, dil),
        in_specs=[pl.BlockSpec(blk, col(0, 0)), pl.BlockSpec(blk, col(0, 1)),
                  pl.BlockSpec(blk, col(1, 0)), pl.BlockSpec(blk, col(1, -1)),
                  pl.BlockSpec(blk, col(2, 0)), pl.BlockSpec(blk, col(2, -1)),
                  own(0), own(1), own(0), own(1), own(0), own(1)],
        out_specs=[pl.BlockSpec(blk, lambda n, r: (n, r))] * 3,
        out_shape=[jax.ShapeDtypeStruct((sub, dil * DIL_W), BF16)] * 3,
        compiler_params=_cparams(("parallel", "parallel")),
    )(view, view, view, view, view, view, do, do, lse, lse, cterm, cterm)
    return tuple(t.reshape(s, DIL_W) for t in outs)


SB_PAIRS = SB_HEADS // 2
SB_COL0 = 0
LOG2E = 1.4426950408889634


SB_EXP_CLAMP = 64.0


def _sb_softplus2(zs):
    t = 1.0 + jnp.exp2(jnp.minimum(zs, SB_EXP_CLAMP))
    return jnp.maximum(jnp.log(t) * LOG2E, zs)


def _sb_consts(nkb):
    row = lax.broadcasted_iota(jnp.int32, (SB_BQ, SB_BK), 0)
    colk = lax.broadcasted_iota(jnp.int32, (SB_BQ, SB_BK), 1)
    rr = lax.broadcasted_iota(jnp.int32, (SB_BK, SB_BK), 0)
    cc = lax.broadcasted_iota(jnp.int32, (SB_BK, SB_BK), 1)
    lane = lax.broadcasted_iota(jnp.int32, (SB_BQ, 128), 1)
    assert 2 * nkb <= 128
    return colk < row, rr, cc, lane < HEAD_DIM, lane


def _split_heads(t):
    first = lax.broadcasted_iota(jnp.int32, t.shape, 1) < HEAD_DIM
    zero = jnp.zeros_like(t)
    return jnp.where(first, t, zero), jnp.where(first, zero, t)


def _sb_fwd(qkv, shard_pack):
    s = qkv.shape[0]
    nq, nkb = s // SB_BQ, s // SB_BK
    zscale = LOG2E / math.sqrt(HEAD_DIM)
    r_pack, w_pack = shard_pack.shape

    def body(q_ref, k_ref, v_ref, pack_ref, o_ref, a_row, others_ref, zs_scr, a_scr, acc_scr, cl_scr,
             send_sems, recv_sems):
        i = pl.program_id(1)
        pair = pl.program_id(0)
        gather = (pack_ref, others_ref, send_sems, recv_sems)

        @pl.when((pair == 0) & (i == 0))
        def _():
            _gather_start(*gather)

        @pl.when((pair == 1) & (i == 0))
        def _():
            _gather_pass_on(*gather)

        @pl.when((pair == SB_PAIRS - 1) & (i == nq - 1))
        def _():
            _gather_finish(*gather)

        causal, rr, cc, _, _ = _sb_consts(nkb)
        later = (rr > cc).astype(BF16)
        qh = _split_heads(q_ref[...])

        def rows(j):
            return pl.ds(pl.multiple_of(j * SB_BK, SB_BK), SB_BK)

        def scores_to(slot, j):
            kb = k_ref[rows(j), :]
            for hh in range(2):
                zs_scr[slot, hh] = _dot_nt(qh[hh], kb) * zscale

        def weights(slot, j, masked):
            xs, sums, sufs = [], [], []
            for hh in range(2):
                zs = zs_scr[slot, hh]
                sp = _sb_softplus2(zs)
                if masked:
                    sp = jnp.where(causal, sp, 0.0)
                xs.append(zs - sp)
                sums.append(jnp.sum(sp, axis=1, keepdims=True))
                sufs.append(_dot_f32_by_01(sp, later, 2))
            for hh in range(2):
                cl = cl_scr[hh]
                a = jnp.exp2(xs[hh] - (sufs[hh] + jnp.concatenate([cl, cl], axis=1)))
                if masked:
                    a = jnp.where(causal, a, 0.0)
                a16 = a.astype(BF16)
                a_scr[slot, :, hh * SB_BK:(hh + 1) * SB_BK] = a16
                a_row[0, 0, j, :, hh * SB_BK:(hh + 1) * SB_BK] = a16
                cl_scr[hh] = cl + sums[hh]

        def add_av(slot, j):
            v0, v1 = _split_heads(v_ref[rows(j), :])
            acc_scr[...] += _dot_nn(a_scr[slot], jnp.concatenate([v0, v1], axis=0))

        acc_scr[...] = jnp.zeros_like(acc_scr)
        cl_scr[...] = jnp.zeros_like(cl_scr)
        scores_to(0, i)
        scores_to(1, jnp.maximum(i - 1, 0))
        weights(0, i, True)

        def step(j, prev, cur):
            scores_to(prev, jnp.maximum(j - 1, 0))
            add_av(prev, j + 1)
            weights(cur, j, False)

        def two_steps(u, _):
            j = i - 1 - 2 * u
            step(j, 0, 1)
            step(j - 1, 1, 0)
            return 0

        lax.fori_loop(0, i // 2, two_steps, 0)

        @pl.when(i % 2 == 1)
        def _():
            step(0, 0, 1)
            add_av(1, 0)

        @pl.when(i % 2 == 0)
        def _():
            add_av(0, 0)

        o_ref[...] = acc_scr[...]

    def full(which):
        return pl.BlockSpec((s, 128), lambda p, i: (0, SB_COL0 + 4 * which + p))

    return pl.pallas_call(
        body,
        name="sb_fwd",
        grid=(SB_PAIRS, nq),
        in_specs=[pl.BlockSpec((SB_BQ, 128), lambda p, i: (i, SB_COL0 + p)), full(1), full(2), ANY],
        out_specs=[pl.BlockSpec((SB_BQ, 128), lambda p, i: (i, p)),
                   pl.BlockSpec((1, 1, nkb, SB_BQ, 2 * SB_BK), lambda p, i: (p, i, 0, 0, 0)), ANY],
        out_shape=[jax.ShapeDtypeStruct((s, SB_W), F32),
                   jax.ShapeDtypeStruct((SB_PAIRS, nq, nkb, SB_BQ, 2 * SB_BK), BF16),
                   jax.ShapeDtypeStruct((N_CHIPS, 2, r_pack // 2, w_pack), shard_pack.dtype)],
        scratch_shapes=[pltpu.VMEM((2, 2, SB_BQ, SB_BK), F32), pltpu.VMEM((2, SB_BQ, 2 * SB_BK), BF16),
                        pltpu.VMEM((SB_BQ, 128), F32), pltpu.VMEM((2, SB_BQ, 128), F32),
                        pltpu.SemaphoreType.DMA((6,)), pltpu.SemaphoreType.DMA((6,))],
        compiler_params=_cparams(("arbitrary", "arbitrary")),
    )(qkv, qkv, qkv, shard_pack.reshape(2, r_pack // 2, w_pack))


def _sb_bwd(qkv, do, a_hbm, chip_sums):
    s = qkv.shape[0]
    nq, nkb = s // SB_BQ, s // SB_BK
    scale = 1.0 / math.sqrt(HEAD_DIM)
    zscale = LOG2E * scale

    def body(q_ref, k_ref, v_ref, do_ref, a_row, sums_ref, dq_ref, dk_ref, dv_ref, got_ref,
             zs_scr, da_scr, dz_scr, a_scr, cg_scr, send_sems, recv_sems):
        i = pl.program_id(1)
        pair = pl.program_id(0)
        first_step = (pair == 0) & (i == 0)
        last_step = (pair == SB_PAIRS - 1) & (i == nq - 1)

        @pl.when(first_step)
        def _():
            _exchange_start(sums_ref, got_ref, send_sems, recv_sems)

        @pl.when(i == 0)
        def _():
            dk_ref[...] = jnp.zeros_like(dk_ref)
            dv_ref[...] = jnp.zeros_like(dv_ref)

        causal, rr, cc, first, _ = _sb_consts(nkb)
        earlier = (rr < cc).astype(BF16)
        q2 = q_ref[...]
        qh = _split_heads(q2)
        do2 = do_ref[...].astype(BF16)
        doh = _split_heads(do2)

        def rows(j):
            return pl.ds(pl.multiple_of(j * SB_BK, SB_BK), SB_BK)

        def products_to(slot, j):
            kb, vb = k_ref[rows(j), :], v_ref[rows(j), :]
            for hh in range(2):
                zs_scr[slot, hh] = _dot_nt(qh[hh], kb) * (-zscale)
                da_scr[slot, hh] = _dot_nt(doh[hh], vb)

        head0_rows = lax.broadcasted_iota(jnp.int32, (128, SB_BK), 0) < HEAD_DIM

        def by_head(t):
            return jnp.where(head0_rows, t[:, :SB_BK], t[:, SB_BK:])

        def apply(slot, j):
            k0, k1 = _split_heads(k_ref[rows(j), :])
            dq_ref[...] += _dot_nn(dz_scr[slot], jnp.concatenate([k0, k1], axis=0)) * scale
            dk_ref[0, j] += by_head(_dot_tn(q2, dz_scr[slot])) * scale
            dv_ref[0, j] += by_head(_dot_tn(do2, a_scr[slot]))

        def grads(slot, j, masked):
            gs, gpres = [], []
            for hh in range(2):
                a16 = a_row[0, 0, j, :, hh * SB_BK:(hh + 1) * SB_BK]
                a_scr[slot, :, hh * SB_BK:(hh + 1) * SB_BK] = a16
                g = a16.astype(F32) * da_scr[slot, hh]
                gs.append(g)
                gpres.append(_dot_f32_by_01(g, earlier, 1))
            sigs = []
            for hh in range(2):
                e = jnp.exp2(jnp.minimum(zs_scr[slot, hh], SB_EXP_CLAMP))
                sigs.append(pl.reciprocal(1.0 + e, approx=True))
            for hh in range(2):
                cg = cg_scr[hh]
                dz = gs[hh] - (gs[hh] + (gpres[hh] + jnp.concatenate([cg, cg], axis=1))) * sigs[hh]
                if masked:
                    dz = jnp.where(causal, dz, 0.0)
                dz_scr[slot, :, hh * SB_BK:(hh + 1) * SB_BK] = dz.astype(BF16)
                cg_scr[hh] = cg + jnp.sum(gs[hh], axis=1, keepdims=True)

        dq_ref[...] = jnp.zeros_like(dq_ref)
        cg_scr[...] = jnp.zeros_like(cg_scr)
        dz_scr[1] = jnp.zeros((SB_BQ, 2 * SB_BK), BF16)
        a_scr[1] = jnp.zeros((SB_BQ, 2 * SB_BK), BF16)
        products_to(0, 0)

        def step(j, cur, nxt):
            products_to(nxt, j + 1)
            apply(nxt, jnp.maximum(j - 1, 0))
            grads(cur, j, False)

        def two_steps(u, _):
            step(2 * u, 0, 1)
            step(2 * u + 1, 1, 0)
            return 0

        lax.fori_loop(0, i // 2, two_steps, 0)

        def last(cur, nxt):
            apply(nxt, jnp.maximum(i - 1, 0))
            grads(cur, i, True)
            apply(cur, i)

        @pl.when(i % 2 == 1)
        def _():
            step(i - 1, 0, 1)
            last(1, 0)

        @pl.when(i % 2 == 0)
        def _():
            last(0, 1)

        @pl.when(last_step)
        def _():
            _exchange_wait(sums_ref, got_ref, send_sems, recv_sems)

    def full(which):
        return pl.BlockSpec((s, 128), lambda p, i: (0, SB_COL0 + 4 * which + p))

    qblk = pl.BlockSpec((SB_BQ, 128), lambda p, i: (i, p))
    acc = pl.BlockSpec((1, nkb, 128, SB_BK), lambda p, i: (p, 0, 0, 0))
    acc_shape = jax.ShapeDtypeStruct((SB_PAIRS, nkb, 128, SB_BK), F32)
    dq, dk_t, dv_t, got = pl.pallas_call(
        body,
        name="sb_bwd",
        grid=(SB_PAIRS, nq),
        in_specs=[pl.BlockSpec((SB_BQ, 128), lambda p, i: (i, SB_COL0 + p)), full(1), full(2), qblk,
                  pl.BlockSpec((1, 1, nkb, SB_BQ, 2 * SB_BK), lambda p, i: (p, i, 0, 0, 0)), ANY],
        out_specs=[qblk, acc, acc, ANY],
        out_shape=[jax.ShapeDtypeStruct((s, SB_W), F32), acc_shape, acc_shape,
                   jax.ShapeDtypeStruct(chip_sums.shape, chip_sums.dtype)],
        scratch_shapes=[pltpu.VMEM((2, 2, SB_BQ, SB_BK), F32), pltpu.VMEM((2, 2, SB_BQ, SB_BK), F32),
                        pltpu.VMEM((2, SB_BQ, 2 * SB_BK), BF16), pltpu.VMEM((2, SB_BQ, 2 * SB_BK), BF16),
                        pltpu.VMEM((2, SB_BQ, 128), F32),
                        pltpu.SemaphoreType.DMA((3,)), pltpu.SemaphoreType.DMA((3,))],
        compiler_params=_cparams(("arbitrary", "arbitrary")),
    )(qkv, qkv, qkv, do, a_hbm, chip_sums)

    def untranspose(t):
        return jnp.transpose(t, (1, 3, 0, 2)).reshape(s, SB_W)

    return dq, untranspose(dk_t), untranspose(dv_t), got


MERGE_TILE = 256


def _group_mix(lses):
    mx = jnp.maximum(jnp.maximum(lses[0], lses[1]), lses[2])
    es = [jnp.exp(t - mx) for t in lses]
    den = es[0] + es[1] + es[2]
    return [e / den for e in es]


def _merge_fwd(o_groups, lse_groups, o_sb, gl, b_gate, w_up_dil, w_up_sb):
    s = gl.shape[0]
    t = MERGE_TILE

    def body(o0, o1, o2, l0, l1, l2, ob_ref, gl_ref, bg_ref, wd_ref, ws_ref, merged_ref, oa_ref):
        rows = slice(None)
        w = _group_mix([_load_halves(l, rows) for l in (l0, l1, l2)])
        og = [_load_halves(o, rows) for o in (o0, o1, o2)]
        oa = (w[0] * og[0] + w[1] * og[1] + w[2] * og[2]).astype(BF16)
        ua = _dot_nn(oa, wd_ref[...])
        ub = _dot_nn(ob_ref[...].astype(BF16), ws_ref[...])
        gate = jax.nn.sigmoid(gl_ref[...] + bg_ref[...])
        merged_ref[...] = (gate[:, :D_MODEL] * ua + gate[:, D_MODEL:] * ub).astype(BF16)
        oa_ref[...] = oa

    dil = pl.BlockSpec((t, DIL_W), lambda i: (i, 0))
    halves = pl.BlockSpec((2, t, 128), lambda i: (0, i, 0))
    const = lambda shape: pl.BlockSpec(shape, lambda i: (0, 0))
    return pl.pallas_call(
        body,
        name="merge_fwd",
        grid=(s // t,),
        in_specs=[halves] * 6 + [pl.BlockSpec((t, SB_W), lambda i: (i, 0)), pl.BlockSpec((t, GATE_W), lambda i: (i, 0)),
                                 const((1, GATE_W)), const((DIL_W, D_MODEL)), const((SB_W, D_MODEL))],
        out_specs=[pl.BlockSpec((t, D_MODEL), lambda i: (i, 0)), dil],
        out_shape=[jax.ShapeDtypeStruct((s, D_MODEL), BF16), jax.ShapeDtypeStruct((s, DIL_W), BF16)],
        compiler_params=_cparams(("parallel",)),
    )(*o_groups, *lse_groups, o_sb, gl, b_gate, w_up_dil, w_up_sb)


def _merge_bwd(dmerged, o_groups, lse_groups, o_sb, gl, b_gate, w_up_dil, w_up_sb, swap):
    s = gl.shape[0]
    t = MERGE_TILE
    n_chunks, r_swap, w_swap = swap.shape
    swap = swap.reshape(n_chunks, 2, r_swap // 2, w_swap)

    def body(dm_ref, o0, o1, o2, l0, l1, l2, ob_ref, gl_ref, bg_ref, wd_ref, ws_ref, swap_ref,
             dua_ref, dub_ref, dgl_ref, dbg_ref, dosb_ref, d0, d1, d2, c0, c1, c2, got_ref, send_sem, recv_sem):
        i = pl.program_id(0)

        @pl.when(i == 0)
        def _():
            _swap_copy(swap_ref, got_ref, send_sem, recv_sem).start()

        @pl.when(i == pl.num_programs(0) - 1)
        def _():
            _swap_copy(swap_ref, got_ref, send_sem, recv_sem).wait()

        rows = slice(None)
        og = [_load_halves(o, rows) for o in (o0, o1, o2)]
        w = _group_mix([_load_halves(l, rows) for l in (l0, l1, l2)])
        oa = (w[0] * og[0] + w[1] * og[1] + w[2] * og[2]).astype(BF16)
        ua = _dot_nn(oa, wd_ref[...])
        ub = _dot_nn(ob_ref[...].astype(BF16), ws_ref[...])
        gate = jax.nn.sigmoid(gl_ref[...] + bg_ref[...])
        ga, gb = gate[:, :D_MODEL], gate[:, D_MODEL:]
        dm = dm_ref[...]
        dua = (dm * ga).astype(BF16)
        dub = (dm * gb).astype(BF16)
        dua_ref[...] = dua
        dub_ref[...] = dub
        dgl_a = dm * ua * ga * (1.0 - ga)
        dgl_b = dm * ub * gb * (1.0 - gb)
        dgl_ref[:, :D_MODEL] = dgl_a.astype(BF16)
        dgl_ref[:, D_MODEL:] = dgl_b.astype(BF16)
        part = jnp.concatenate([jnp.sum(dgl_a.reshape(t // 8, 8, D_MODEL), axis=0),
                                jnp.sum(dgl_b.reshape(t // 8, 8, D_MODEL), axis=0)], axis=1)

        @pl.when(i == 0)
        def _():
            dbg_ref[...] = part

        @pl.when(i > 0)
        def _():
            dbg_ref[...] += part

        dosb_ref[...] = _dot_nt(dub, ws_ref[...])
        doa = _dot_nt(dua, wd_ref[...])
        rr = lax.broadcasted_iota(jnp.int32, (DIL_W, DIL_W), 0) // HEAD_DIM
        cc = lax.broadcasted_iota(jnp.int32, (DIL_W, DIL_W), 1) // HEAD_DIM
        same_head = (rr == cc).astype(BF16)
        dw = [_dot_f32_by_01(doa * og[g], same_head, 2) for g in range(3)]
        mean_dw = w[0] * dw[0] + w[1] * dw[1] + w[2] * dw[2]
        for g, (d_ref, c_ref) in enumerate(((d0, c0), (d1, c1), (d2, c2))):
            _store_halves(d_ref, rows, w[g] * doa)
            _store_halves(c_ref, rows, -w[g] * mean_dw)

    dil = pl.BlockSpec((2, t, 128), lambda i: (0, i, 0))
    wide = pl.BlockSpec((t, D_MODEL), lambda i: (i, 0))
    gate2 = pl.BlockSpec((t, GATE_W), lambda i: (i, 0))
    sbw = pl.BlockSpec((t, SB_W), lambda i: (i, 0))
    const = lambda shape: pl.BlockSpec(shape, lambda i: (0, 0))
    return pl.pallas_call(
        body,
        name="merge_bwd",
        grid=(s // t,),
        in_specs=[wide] + [dil] * 6 + [sbw, gate2, const((1, GATE_W)), const((DIL_W, D_MODEL)), const((SB_W, D_MODEL)),
                                       ANY],
        out_specs=[wide, wide, gate2, const((8, GATE_W)), sbw] + [dil] * 6 + [ANY],
        out_shape=[jax.ShapeDtypeStruct((s, D_MODEL), BF16), jax.ShapeDtypeStruct((s, D_MODEL), BF16),
                   jax.ShapeDtypeStruct((s, GATE_W), BF16), jax.ShapeDtypeStruct((8, GATE_W), F32),
                   jax.ShapeDtypeStruct((s, SB_W), F32)] + [jax.ShapeDtypeStruct((2, s, 128), F32)] * 6
        + [jax.ShapeDtypeStruct((n_chunks, r_swap // 2, w_swap), swap.dtype)],
        scratch_shapes=[pltpu.SemaphoreType.DMA, pltpu.SemaphoreType.DMA],
        compiler_params=_cparams(("arbitrary",)),
    )(dmerged, *o_groups, *lse_groups, o_sb, gl, b_gate, w_up_dil, w_up_sb, swap)


ANY = pl.BlockSpec(memory_space=pl.ANY)


def _place():
    x, y, c = lax.axis_index("x"), lax.axis_index("y"), lax.axis_index("c")
    other_chips = [(1 - x, y), (x, 1 - y), (1 - x, 1 - y)]
    return x, y, c, other_chips


def _gather_copies(p_ref, out_ref, send_sems, recv_sems):
    x, y, c, chips = _place()
    me, sibling = 2 * x + y, (x, y, 1 - c)
    idx = [2 * chip[0] + chip[1] for chip in chips]

    def copy(k, chip_idx, core, to, src=None):
        return pltpu.make_async_remote_copy(
            src_ref=out_ref.at[chip_idx, core] if src is None else src, dst_ref=out_ref.at[chip_idx, core],
            send_sem=send_sems.at[k], recv_sem=recv_sems.at[k], device_id=to, device_id_type=MESH)

    first = lambda j: copy(j, me, c, (*chips[j], c), src=p_ref.at[c])
    landed = lambda j: copy(j, idx[j], c, (x, y, c))
    passed = lambda j: copy(3 + j, idx[j], c, sibling)
    handed = lambda j: copy(3 + j, idx[j], 1 - c, (x, y, c))
    return first, landed, passed, handed


def _gather_start(*refs):
    first = _gather_copies(*refs)[0]
    for j in range(3):
        first(j).start()


def _gather_pass_on(*refs):
    _, landed, passed, _ = _gather_copies(*refs)
    for j in range(3):
        landed(j).wait_recv()
        passed(j).start()


def _gather_finish(*refs):
    first, _, passed, handed = _gather_copies(*refs)
    for j in range(3):
        handed(j).wait_recv()
    for j in range(3):
        first(j).wait_send()
        passed(j).wait_send()


def _fill_own_slot(others, pack):
    n, _, rh, wd = others.shape
    me = 2 * lax.axis_index("x") + lax.axis_index("y")
    mine = lax.broadcasted_iota(jnp.int32, (n, 1, 1, 1), 0) == me
    return jnp.where(mine, pack.reshape(1, 2, rh, wd), others).reshape(n, 2 * rh, wd)


def _swap_copy(g_ref, out_ref, send_sem, recv_sem):
    x, y, c, _ = _place()
    return pltpu.make_async_remote_copy(
        src_ref=g_ref.at[:, 1 - c], dst_ref=out_ref,
        send_sem=send_sem, recv_sem=recv_sem, device_id=(x, y, 1 - c), device_id_type=MESH)


def _swap_halves(g):
    n, r, wd = g.shape
    rh = r // 2
    g = g.reshape(n, 2, rh, wd)

    def body(g_ref, out_ref, send_sem, recv_sem):
        cp = _swap_copy(g_ref, out_ref, send_sem, recv_sem)
        cp.start()
        cp.wait()

    return pl.pallas_call(
        body,
        name="grad_swap_halves",
        in_specs=[ANY],
        out_specs=ANY,
        out_shape=jax.ShapeDtypeStruct((n, rh, wd), g.dtype),
        scratch_shapes=[pltpu.SemaphoreType.DMA, pltpu.SemaphoreType.DMA],
    )(g)


def _add_halves(g, got, core):
    n, r, wd = g.shape
    rh = r // 2
    t = rh // 4
    nt = rh // t

    def body(c_ref, a_ref, b_ref, o_ref):
        o_ref[...] = (a_ref[0] + b_ref[...]).astype(BF16)

    grid_spec = pltpu.PrefetchScalarGridSpec(
        num_scalar_prefetch=1,
        grid=(n, nt),
        in_specs=[pl.BlockSpec((1, 1, t, wd), lambda s, i, c: (s, c[0], i, 0)),
                  pl.BlockSpec((1, t, wd), lambda s, i, c: (s, i, 0))],
        out_specs=pl.BlockSpec((1, t, wd), lambda s, i, c: (s, i, 0)),
    )
    return pl.pallas_call(
        body,
        name="grad_add_halves",
        grid_spec=grid_spec,
        out_shape=jax.ShapeDtypeStruct((n, rh, wd), BF16),
        compiler_params=_cparams(("parallel", "parallel")),
    )(core, g.reshape(n, 2, rh, wd), got)


def _exchange_copies(h_ref, out_ref, send_sems, recv_sems):
    x, y, c, chips = _place()
    me = 2 * x + y

    def copy(j, slot):
        them = 2 * chips[j][0] + chips[j][1]
        return pltpu.make_async_remote_copy(
            src_ref=h_ref.at[them], dst_ref=out_ref.at[me if slot == "mine" else them],
            send_sem=send_sems.at[j], recv_sem=recv_sems.at[j], device_id=(*chips[j], c), device_id_type=MESH)

    return (lambda j: copy(j, "mine")), (lambda j: copy(j, "theirs"))


def _exchange_start(h_ref, out_ref, send_sems, recv_sems):
    send = _exchange_copies(h_ref, out_ref, send_sems, recv_sems)[0]
    for j in range(3):
        send(j).start()


def _exchange_wait(h_ref, out_ref, send_sems, recv_sems):
    send, arrival = _exchange_copies(h_ref, out_ref, send_sems, recv_sems)
    for j in range(3):
        arrival(j).wait_recv()
    for j in range(3):
        send(j).wait_send()


def _sum_chips(b, h, chip):
    n, rh, wd = b.shape
    t = rh // 4

    def body(chip_ref, b_ref, own_ref, o_ref):
        own = own_ref[0]
        s0, s1, s2, s3 = (jnp.where(chip_ref[0] == k, own, b_ref[k]).astype(F32) for k in range(n))
        o_ref[...] = ((s0 + s1) + s2) + s3

    grid_spec = pltpu.PrefetchScalarGridSpec(
        num_scalar_prefetch=1,
        grid=(rh // t,),
        in_specs=[pl.BlockSpec((n, t, wd), lambda i, chip: (0, i, 0)),
                  pl.BlockSpec((1, t, wd), lambda i, chip: (chip[0], i, 0))],
        out_specs=pl.BlockSpec((t, wd), lambda i, chip: (i, 0)),
    )
    return pl.pallas_call(
        body,
        name="grad_sum_chips",
        grid_spec=grid_spec,
        out_shape=jax.ShapeDtypeStruct((rh, wd), F32),
        compiler_params=_cparams(("parallel",)),
    )(chip, b, h)


def _join_halves(tc):
    rh, wd = tc.shape

    def body(t_ref, out_ref, send_sem, recv_sem):
        x, y, c, _ = _place()
        cp = pltpu.make_async_remote_copy(
            src_ref=t_ref, dst_ref=out_ref.at[c],
            send_sem=send_sem, recv_sem=recv_sem, device_id=(x, y, 1 - c), device_id_type=MESH)
        cp.start()
        cp.wait()

    halves = pl.pallas_call(
        body,
        name="grad_join_halves",
        in_specs=[ANY],
        out_specs=ANY,
        out_shape=jax.ShapeDtypeStruct((2, rh, wd), tc.dtype),
        scratch_shapes=[pltpu.SemaphoreType.DMA, pltpu.SemaphoreType.DMA],
    )(tc)
    return lax.dynamic_update_slice(halves, tc[None], (lax.axis_index("c"), 0, 0)).reshape(2 * rh, wd)


def _all_reduce_small(pack):
    rows, lanes = pack.shape

    def body(p_ref, out_ref, buf, send_sems, recv_sems):
        x, y, c, _ = _place()
        me = 4 * x + 2 * y + c
        buf[me] = p_ref[...]
        sends = []
        for k in range(1, N_DEV):
            peer = (x ^ (k >> 2), y ^ ((k >> 1) & 1), c ^ (k & 1))
            sends.append(pltpu.make_async_remote_copy(
                src_ref=p_ref, dst_ref=buf.at[me], send_sem=send_sems.at[k - 1], recv_sem=recv_sems.at[k - 1],
                device_id=peer, device_id_type=MESH))
        for cp in sends:
            cp.start()
        for k in range(1, N_DEV):
            pltpu.make_async_remote_copy(
                src_ref=p_ref, dst_ref=buf.at[me ^ k], send_sem=send_sems.at[k - 1], recv_sem=recv_sems.at[k - 1],
                device_id=(x, y, c), device_id_type=MESH).wait_recv()
        for cp in sends:
            cp.wait_send()
        total = buf[0]
        for d in range(1, N_DEV):
            total = total + buf[d]
        out_ref[...] = total

    vm = pl.BlockSpec(memory_space=pltpu.VMEM)
    return pl.pallas_call(
        body,
        name="all_reduce_small",
        in_specs=[vm],
        out_specs=vm,
        out_shape=jax.ShapeDtypeStruct((rows, lanes), F32),
        scratch_shapes=[pltpu.VMEM((N_DEV, rows, lanes), F32), pltpu.SemaphoreType.DMA((N_DEV - 1,)),
                        pltpu.SemaphoreType.DMA((N_DEV - 1,))],
    )(pack)


def _adamw(g, w, m, v, name):
    rows, cols = g.shape
    t = rows
    for cand in (256, 128, 64, 32, 16, 8):
        if rows % cand == 0:
            t = cand
            break

    def body(g_ref, w_ref, m_ref, v_ref, d_ref, nm_ref, nv_ref):
        gv = g_ref[...]
        mv = ADAM_B1 * m_ref[...] + (1.0 - ADAM_B1) * gv
        vv = ADAM_B2 * v_ref[...] + (1.0 - ADAM_B2) * (gv * gv)
        m_hat = mv / (1.0 - ADAM_B1 ** ADAM_STEP)
        v_hat = vv / (1.0 - ADAM_B2 ** ADAM_STEP)
        d_ref[...] = -ADAM_LR * (m_hat / (jnp.sqrt(v_hat) + ADAM_EPS) + ADAM_WD * w_ref[...])
        nm_ref[...] = mv
        nv_ref[...] = vv

    blk = pl.BlockSpec((t, cols), lambda i: (i, 0))
    return pl.pallas_call(
        body,
        name=name,
        grid=(rows // t,),
        in_specs=[blk] * 4,
        out_specs=[blk] * 3,
        out_shape=[jax.ShapeDtypeStruct((rows, cols), F32)] * 3,
        compiler_params=_cparams(("parallel",)),
    )(g, w, m, v)


PACK_W = 1024
BIG = (("w_in", (D_MODEL, IN_COLS), 1), ("w_up_dil", (DIL_W, D_MODEL), 1), ("w_up_sb", (SB_W, D_MODEL), 1),
       ("w_out", (D_MODEL, D_MODEL), 0), ("w_mlp_in", (D_MODEL, D_FF), 1), ("w_mlp_out", (D_FF, D_MODEL), 0))


def _shard_shape(shape, axis):
    return tuple(d // N_CHIPS if a == axis else d for a, d in enumerate(shape))


MIXER_GROUP, MLP_GROUP = BIG[:4], BIG[4:]
LATE_WEIGHTS = BIG[1:]


def _pack_rows(group=BIG):
    rows, at = {}, 0
    for name, shape, axis in group:
        n = math.prod(_shard_shape(shape, axis)) // PACK_W
        rows[name] = (at, n)
        at += n
    return rows, at


def _pack_shards(shards, group):
    return jnp.concatenate([shards[name].reshape(-1, PACK_W) for name, _, _ in group], axis=0)


def _unpack_full(gathered, group):
    rows, _ = _pack_rows(group)
    full = {}
    for name, shape, axis in group:
        at, n = rows[name]
        parts = gathered[:, at:at + n, :].reshape((N_CHIPS,) + _shard_shape(shape, axis))
        if axis == 0:
            full[name] = parts.reshape(shape)
        else:
            full[name] = jnp.transpose(parts, (1, 0, 2)).reshape(shape)
    return full


def _pack_full_grads(grads, group):
    chunks = []
    for name, shape, axis in group:
        g = grads[name]
        if axis == 0:
            parts = g.reshape((N_CHIPS, shape[0] // N_CHIPS, shape[1]))
        else:
            parts = jnp.transpose(g.reshape((shape[0], N_CHIPS, shape[1] // N_CHIPS)), (1, 0, 2))
        chunks.append(parts.reshape(N_CHIPS, -1, PACK_W))
    return jnp.concatenate(chunks, axis=1)


def _unpack_shard(packed, group):
    rows, _ = _pack_rows(group)
    return {name: packed[rows[name][0]:rows[name][0] + rows[name][1]].reshape(_shard_shape(shape, axis))
            for name, shape, axis in group}


def _local_step(x, target, early_shards, late_shards, norm_mix_g, b_gate, norm_mlp_g, norm_final_g, core):
    h, early = _rms_fwd_and_gather(x, norm_mix_g, early_shards)
    w = {"w_in": jnp.transpose(early, (1, 0, 2)).reshape(D_MODEL, IN_COLS)}
    w_in = w["w_in"]
    sb0 = 9 * DIL_W
    w_sb, w_gate = w_in[:, sb0:QKV_W], w_in[:, QKV_W:]
    w_dil = [jnp.concatenate([w_in[:, (3 * i + g) * DIL_W:(3 * i + g + 1) * DIL_W] for i in range(3)], axis=1)
             for g in range(3)]

    qkv_dil = [_matmul(h, w_dil[g], mode="nn", out_dtypes=(BF16,), name=f"proj_dil_g{g}", tn=768)[0] for g in range(3)]
    (qkv_sb,) = _matmul(h, w_sb, mode="nn", out_dtypes=(BF16,), name="proj_sb", tn=768)
    (gl,) = _matmul(h, w_gate, mode="nn", out_dtypes=(F32,), name="proj_gate")
    dil = _dil_fwd_all(qkv_dil)
    o_groups, lse_groups = [d[0] for d in dil], [d[1] for d in dil]
    o_sb, a_sb, late_others = _sb_fwd(qkv_sb, late_shards)
    w = {**w, **_unpack_full(_fill_own_slot(late_others, late_shards), LATE_WEIGHTS)}
    merged, o_a = _merge_fwd(o_groups, lse_groups, o_sb, gl, b_gate, w["w_up_dil"], w["w_up_sb"])
    def residual_and_norm(acc, res, g):
        x1 = res + acc
        return x1, _rms_rows(x1)[0] * g

    x1, h2 = _matmul(merged, w["w_out"], mode="nn", out_dtypes=(F32, BF16), name="out_proj", tm=ROW_TILE,
                     extras=(x, norm_mlp_g), epilogue=residual_and_norm)
    u, act = _matmul(h2, w["w_mlp_in"], mode="nn", out_dtypes=(BF16, BF16), name="mlp_in",
                     epilogue=lambda acc: (acc, jnp.square(jnp.maximum(acc, 0.0))))

    def residual_and_loss(acc, res, tgt, g):
        xh, r = _rms_rows(res + acc)
        err = xh * g - tgt
        dy = err * (1.0 / D_MODEL)
        dxh = dy * g
        dx2 = r * (dxh - xh * jnp.mean(dxh * xh, axis=-1, keepdims=True))
        return dx2, _rows_sum8(dy * xh), (0.5 / D_MODEL) * _rows_sum8(err * err)

    dx2, dg_final, loss_part = _matmul(
        act, w["w_mlp_out"], mode="nn", out_dtypes=(F32, ("part", F32), ("part", F32)), name="mlp_out", tm=ROW_TILE,
        tk=2048, extras=(x1, target, norm_final_g.reshape(1, D_MODEL)), epilogue=residual_and_loss)

    (du,) = _matmul(dx2, w["w_mlp_out"], mode="nt", out_dtypes=(BF16,), name="mlp_out_dx",
                    extras=(u,), epilogue=lambda acc, uu: (acc * (2.0 * jnp.maximum(uu.astype(F32), 0.0)),))
    pack_shape = (N_CHIPS, 2 * D_MODEL, D_MODEL)
    (half_pack,) = _matmul(act, dx2, mode="tn", out_dtypes=(F32,), name="mlp_out_dw",
                           into=(pack_shape, lambda i, j: (i, 1, 0), None))
    (mlp_pack,) = _matmul(h2, du, mode="tn", out_dtypes=(F32,), name="mlp_in_dw",
                          into=(pack_shape, lambda i, j: (j, 0, 0), half_pack))

    def norm_bwd(acc, xx, dres, g):
        dx, dg = _rms_bwd_rows(acc, xx, g)
        return dres + dx, dg

    dx1, dg_mlp = _matmul(du, w["w_mlp_in"], mode="nt", out_dtypes=(F32, ("part", F32)), name="mlp_in_dx",
                          tm=ROW_TILE, tk=2048, extras=(x1, dx2, norm_mlp_g), epilogue=norm_bwd)

    (dmerged,) = _matmul(dx1, w["w_out"], mode="nt", out_dtypes=(F32,), name="out_proj_dx")
    (g_out,) = _matmul(merged, dx1, mode="tn", out_dtypes=(F32,), name="out_proj_dw")
    mb = _merge_bwd(dmerged, o_groups, lse_groups, o_sb, gl, b_gate, w["w_up_dil"], w["w_up_sb"], mlp_pack)
    dua, dub, dgl, dbg, do_sb = mb[:5]
    do_groups, c_groups = mb[5:8], mb[8:11]
    mlp_sums = _add_halves(mlp_pack, mb[11], core)
    (g_up_dil,) = _matmul(o_a, dua, mode="tn", out_dtypes=(F32,), name="up_dil_dw")
    (g_up_sb,) = _matmul(o_sb, dub, mode="tn", out_dtypes=(F32,), name="up_sb_dw")
    dq_sb, dk_sb, dv_sb, mlp_got = _sb_bwd(qkv_sb, do_sb, a_sb, mlp_sums)
    dil_b = [_dil_bwd(qkv_dil[g], do_groups[g], lse_groups[g], c_groups[g], g) for g in range(3)]
    dproj = jnp.concatenate(
        [dil_b[g][i].astype(BF16) for i in range(3) for g in range(3)]
        + [t.astype(BF16) for t in (dq_sb, dk_sb, dv_sb)] + [dgl], axis=1)
    (g_in,) = _matmul(h, dproj, mode="tn", out_dtypes=(F32,), name="proj_dw", tm=512, tn=IN_COLS // 2)
    mixer_pack = _pack_full_grads({"w_in": g_in, "w_up_dil": g_up_dil, "w_up_sb": g_up_sb, "w_out": g_out}, MIXER_GROUP)
    mixer_sums = _add_halves(mixer_pack, _swap_halves(mixer_pack), core)
    dh, mixer_got = _matmul(dproj, w["w_in"], mode="nt", out_dtypes=(F32,), name="proj_dx", tk=IN_COLS // 2,
                            exchange=mixer_sums)
    grad_x, dg_mix = _rms_bwd_residual(dh, x, norm_mix_g, dx1)

    small = (dg_mix, dbg, dg_mlp, dg_final, loss_part)
    return grad_x, (mixer_got, mixer_sums), (mlp_got, mlp_sums), small


def kernel(x, norm_mix_g, w_in, b_gate, w_up_dil, w_up_sb, w_out, norm_mlp_g, w_mlp_in, w_mlp_out, norm_final_g, loss_target, m_norm_mix_g, m_w_in, m_b_gate, m_w_up_dil, m_w_up_sb, m_w_out, m_norm_mlp_g, m_w_mlp_in, m_w_mlp_out, m_norm_final_g, v_norm_mix_g, v_w_in, v_b_gate, v_w_up_dil, v_w_up_sb, v_w_out, v_norm_mlp_g, v_w_mlp_in, v_w_mlp_out, v_norm_final_g):
    shards = {"w_in": w_in[0], "w_up_dil": w_up_dil[0], "w_up_sb": w_up_sb[0], "w_out": w_out[0],
              "w_mlp_in": w_mlp_in[0], "w_mlp_out": w_mlp_out[0]}
    moments_m = {"w_in": m_w_in[0], "w_up_dil": m_w_up_dil[0], "w_up_sb": m_w_up_sb[0], "w_out": m_w_out[0],
                 "w_mlp_in": m_w_mlp_in[0], "w_mlp_out": m_w_mlp_out[0]}
    moments_v = {"w_in": v_w_in[0], "w_up_dil": v_w_up_dil[0], "w_up_sb": v_w_up_sb[0], "w_out": v_w_out[0],
                 "w_mlp_in": v_w_mlp_in[0], "w_mlp_out": v_w_mlp_out[0]}

    shards16 = {n: s.astype(BF16) for n, s in shards.items()}
    early_shards = shards16["w_in"]
    late_shards = _pack_shards(shards16, LATE_WEIGHTS)

    core = lax.axis_index("c").astype(jnp.int32).reshape(1)
    chip = (2 * lax.axis_index("x") + lax.axis_index("y")).astype(jnp.int32).reshape(1)
    grad_x, (mixer_got, mixer_sums), (mlp_got, mlp_sums), small = _local_step(
        x[0], loss_target[0], early_shards, late_shards, norm_mix_g, b_gate, norm_mlp_g, norm_final_g, core)

    reduced = _join_halves(_sum_chips(mixer_got, mixer_sums, chip))
    reduced_mlp = _join_halves(_sum_chips(mlp_got, mlp_sums, chip))
    g_shard = {**_unpack_shard(reduced, MIXER_GROUP), **_unpack_shard(reduced_mlp, MLP_GROUP)}

    dg_mix, dbg, dg_mlp, dg_final, loss_part = small
    loss_row = jnp.sum(loss_part, axis=0, keepdims=True)
    small_pack = jnp.concatenate(
        [jnp.sum(dg_mix, axis=0, keepdims=True), jnp.sum(dbg, axis=0, keepdims=True),
         jnp.sum(dg_mlp, axis=0, keepdims=True), jnp.sum(dg_final, axis=0, keepdims=True), loss_row], axis=1)
    n_small = small_pack.shape[1]
    small_sum = _all_reduce_small(small_pack.reshape(n_small // 128, 128)).reshape(1, n_small)
    g_norm_mix = small_sum[:, :D_MODEL]
    g_b_gate = small_sum[:, D_MODEL:3 * D_MODEL]
    g_norm_mlp = small_sum[:, 3 * D_MODEL:4 * D_MODEL]
    g_norm_final = small_sum[:, 4 * D_MODEL:5 * D_MODEL]
    loss = jnp.sum(small_sum[:, 5 * D_MODEL:])

    names = ["norm_mix_g", "w_in", "b_gate", "w_up_dil", "w_up_sb", "w_out", "norm_mlp_g", "w_mlp_in", "w_mlp_out",
             "norm_final_g"]
    grads = dict(g_shard)
    grads.update(norm_mix_g=g_norm_mix, b_gate=g_b_gate, norm_mlp_g=g_norm_mlp, norm_final_g=g_norm_final)
    weights = dict(shards)
    weights.update(norm_mix_g=norm_mix_g, b_gate=b_gate, norm_mlp_g=norm_mlp_g, norm_final_g=norm_final_g.reshape(1, D_MODEL))
    ms = dict(moments_m)
    ms.update(norm_mix_g=m_norm_mix_g, b_gate=m_b_gate, norm_mlp_g=m_norm_mlp_g, norm_final_g=m_norm_final_g.reshape(1, D_MODEL))
    vs = dict(moments_v)
    vs.update(norm_mix_g=v_norm_mix_g, b_gate=v_b_gate, norm_mlp_g=v_norm_mlp_g, norm_final_g=v_norm_final_g.reshape(1, D_MODEL))

    out_shapes = {"norm_mix_g": norm_mix_g.shape, "w_in": w_in.shape, "b_gate": b_gate.shape, "w_up_dil": w_up_dil.shape,
                  "w_up_sb": w_up_sb.shape, "w_out": w_out.shape, "norm_mlp_g": norm_mlp_g.shape,
                  "w_mlp_in": w_mlp_in.shape, "w_mlp_out": w_mlp_out.shape, "norm_final_g": norm_final_g.shape}
    g_out, d_out, m_out, v_out = [], [], [], []
    for n in names:
        d, nm, nv = _adamw(grads[n], weights[n], ms[n], vs[n], "adamw_" + n)
        shape = out_shapes[n]
        g_out.append(grads[n].reshape(shape))
        d_out.append(d.reshape(shape))
        m_out.append(nm.reshape(shape))
        v_out.append(nv.reshape(shape))
    return (loss, grad_x.reshape(x.shape), *g_out, *d_out, *m_out, *v_out)
```

```python
import functools
import math

import jax
import jax.numpy as jnp
import numpy as np
from jax import lax
from jax.experimental import pallas as pl
from jax.experimental.pallas import tpu as pltpu

F32 = jnp.float32
BF16 = jnp.bfloat16
MESH = pl.DeviceIdType.MESH

D_MODEL = 1024
HEAD_DIM = 64
DIL_GROUPS = ((128, 1), (512, 4), (2048, 16))
DIL_HEADS = 4
DIL_W = 256
N_DIL_HEADS = 12
SB_HEADS = 8
SB_W = SB_HEADS * HEAD_DIM
QKV_W = 3 * 3 * DIL_W + 3 * SB_W
GATE_W = 2 * D_MODEL
IN_COLS = QKV_W + GATE_W
D_FF = 4 * D_MODEL
BLOCK = 128
RMS_EPS = 1e-6
NEG_INF = -1e30
N_CHIPS = 4
N_DEV = 8

ADAM_LR = 0.001
ADAM_B1 = 0.9
ADAM_B2 = 0.999
ADAM_EPS = 1e-08
ADAM_WD = 0.01
ADAM_STEP = 10

VMEM_LIMIT = 56 * 1024 * 1024

SB_BQ = 256
SB_BK = 256


def _cparams(sem=None):
    if sem is None:
        return pltpu.CompilerParams(vmem_limit_bytes=VMEM_LIMIT)
    return pltpu.CompilerParams(dimension_semantics=sem, vmem_limit_bytes=VMEM_LIMIT)


def _dot(a, b, dims):
    return lax.dot_general(a, b, (dims, ((), ())), preferred_element_type=F32)


def _dot_nn(a, b):
    return _dot(a, b, ((1,), (0,)))


def _dot_nt(a, b):
    return _dot(a, b, ((1,), (1,)))


def _dot_tn(a, b):
    return _dot(a, b, ((0,), (0,)))


def _dot_f32_by_01(x, m01, pieces=3):
    hi = x.astype(BF16)
    if pieces == 1:
        return _dot_nn(hi, m01)
    r1 = x - hi.astype(F32)
    mid = r1.astype(BF16)
    if pieces == 2:
        return _dot_nn(hi, m01) + _dot_nn(mid, m01)
    lo = (r1 - mid.astype(F32)).astype(BF16)
    return _dot_nn(hi, m01) + _dot_nn(mid, m01) + _dot_nn(lo, m01)


def _matmul(a, b, *, mode, out_dtypes, name, tm=1024, tn=1024, tk=1024, extras=(), epilogue=None, exchange=None,
            into=None):
    if mode == "nn":
        (m, k), (k2, n) = a.shape, b.shape
    elif mode == "nt":
        (m, k), (n, k2) = a.shape, b.shape
    else:
        (k, m), (k2, n) = a.shape, b.shape
    assert k == k2, (a.shape, b.shape, mode)
    tm, tn, tk = min(tm, m), min(tn, n), min(tk, k)
    assert m % tm == 0 and n % tn == 0 and k % tk == 0, (m, n, k, tm, tn, tk)
    nk = k // tk
    n_out = len(out_dtypes)
    n_ex = len(extras)

    if mode == "nn":
        a_spec = pl.BlockSpec((tm, tk), lambda i, j, kk: (i, kk))
        b_spec = pl.BlockSpec((tk, tn), lambda i, j, kk: (kk, j))
        dot = _dot_nn
    elif mode == "nt":
        a_spec = pl.BlockSpec((tm, tk), lambda i, j, kk: (i, kk))
        b_spec = pl.BlockSpec((tn, tk), lambda i, j, kk: (j, kk))
        dot = _dot_nt
    else:
        a_spec = pl.BlockSpec((tk, tm), lambda i, j, kk: (kk, i))
        b_spec = pl.BlockSpec((tk, tn), lambda i, j, kk: (kk, j))
        dot = _dot_tn
    mn_spec = pl.BlockSpec((tm, tn), lambda i, j, kk: (i, j))
    row_spec = pl.BlockSpec((1, tn), lambda i, j, kk: (0, j))
    part_spec = pl.BlockSpec((8, tn), lambda i, j, kk: (i, j))
    ex_specs = [row_spec if e.shape[0] == 1 else mn_spec for e in extras]
    is_part = [isinstance(dt, tuple) for dt in out_dtypes]
    out_dts = [dt[1] if p else dt for dt, p in zip(out_dtypes, is_part)]
    out_specs = [part_spec if p else mn_spec for p in is_part]
    out_shapes = [jax.ShapeDtypeStruct((8 * (m // tm), n) if p else (m, n), dt) for dt, p in zip(out_dts, is_part)]

    n_side = 0 if exchange is None else 1
    grid = (m // tm, n // tn, nk)
    prior = []
    if into is not None:
        assert n_out == 1 and not extras and exchange is None
        into_shape, into_map, into_prior = into
        out_specs = [pl.BlockSpec((1, tm, tn), lambda i, j, kk: into_map(i, j))]
        out_shapes = [jax.ShapeDtypeStruct(into_shape, out_dts[0])]
        prior = [] if into_prior is None else [into_prior]

    def body(*refs):
        a_ref, b_ref = refs[0], refs[1]
        ex_refs = refs[2:2 + n_ex]
        n_in = 2 + n_ex + n_side + len(prior)
        out_refs = refs[n_in:n_in + n_out]
        scratch = refs[n_in + n_out + n_side:]
        acc_ref = scratch[0] if nk > 1 else None
        if exchange is not None:
            side = (refs[2 + n_ex], refs[2 + n_ex + n_side + n_out]) + tuple(scratch[-2:])
            step = (pl.program_id(0) * grid[1] + pl.program_id(1)) * grid[2] + pl.program_id(2)

            @pl.when(step == 0)
            def _():
                _exchange_start(*side)

            @pl.when(step == grid[0] * grid[1] * grid[2] - 1)
            def _():
                _exchange_wait(*side)

        part = dot(a_ref[...].astype(BF16), b_ref[...].astype(BF16))

        def finish(acc):
            if epilogue is None:
                outs = (acc,)
            else:
                outs = epilogue(acc, *[r[...] for r in ex_refs])
            for o_ref, o in zip(out_refs, outs):
                if into is None:
                    o_ref[...] = o.astype(o_ref.dtype)
                else:
                    o_ref[0] = o.astype(o_ref.dtype)

        if nk == 1:
            finish(part)
        else:
            kk = pl.program_id(2)

            @pl.when(kk == 0)
            def _():
                acc_ref[...] = part

            @pl.when(kk > 0)
            def _():
                acc_ref[...] += part

            @pl.when(kk == nk - 1)
            def _():
                finish(acc_ref[...])

    side_in = [] if exchange is None else [exchange]
    outs = pl.pallas_call(
        body,
        name=name,
        grid=grid,
        in_specs=[a_spec, b_spec] + ex_specs + [ANY] * (n_side + len(prior)),
        out_specs=out_specs + [ANY] * n_side,
        out_shape=out_shapes + [jax.ShapeDtypeStruct(e.shape, e.dtype) for e in side_in],
        scratch_shapes=([pltpu.VMEM((tm, tn), F32)] if nk > 1 else [])
        + [pltpu.SemaphoreType.DMA((3,)), pltpu.SemaphoreType.DMA((3,))] * n_side,
        input_output_aliases={2: 0} if prior else {},
        compiler_params=_cparams(("arbitrary",) * 3 if n_side else ("parallel", "parallel", "arbitrary")),
    )(a, b, *extras, *side_in, *prior)
    return outs


ROW_TILE = 512


def _rows_sum8(t):
    rows, d = t.shape
    return jnp.sum(t.reshape(rows // 8, 8, d), axis=0)


def _rms_rows(x):
    r = lax.rsqrt(jnp.mean(x * x, axis=-1, keepdims=True) + RMS_EPS)
    return x * r, r


def _rms_bwd_rows(dh, x, g):
    xh, r = _rms_rows(x)
    dxh = dh * g
    return r * (dxh - xh * jnp.mean(dxh * xh, axis=-1, keepdims=True)), _rows_sum8(dh * xh)


def _rms_bwd_residual(dh, x, g, dres):
    s, d = x.shape

    def body(dh_ref, x_ref, g_ref, dres_ref, dx_ref, dg_ref):
        dx, dg = _rms_bwd_rows(dh_ref[...], x_ref[...], g_ref[...])
        dx_ref[...] = dres_ref[...] + dx
        dg_ref[...] = dg

    row = pl.BlockSpec((ROW_TILE, d), lambda i: (i, 0))
    return pl.pallas_call(
        body,
        name="norm_mix_bwd",
        grid=(s // ROW_TILE,),
        in_specs=[row, row, pl.BlockSpec((1, d), lambda i: (0, 0)), row],
        out_specs=[row, pl.BlockSpec((8, d), lambda i: (i, 0))],
        out_shape=[jax.ShapeDtypeStruct((s, d), F32), jax.ShapeDtypeStruct((8 * (s // ROW_TILE), d), F32)],
        compiler_params=_cparams(("parallel",)),
    )(dh, x, g, dres)


def _rms_fwd_and_gather(x, g, shard_pack):
    s, d = x.shape
    r_pack, w_pack = shard_pack.shape
    steps = s // ROW_TILE

    def body(x_ref, g_ref, pack_ref, h_ref, others_ref, send_sems, recv_sems):
        i = pl.program_id(0)
        gather = (pack_ref, others_ref, send_sems, recv_sems)

        @pl.when(i == 0)
        def _():
            _gather_start(*gather)

        h_ref[...] = (_rms_rows(x_ref[...])[0] * g_ref[...]).astype(BF16)

        @pl.when(i == steps - 1)
        def _():
            _gather_pass_on(*gather)
            _gather_finish(*gather)

    h, others = pl.pallas_call(
        body,
        name="norm_mix",
        grid=(steps,),
        in_specs=[pl.BlockSpec((ROW_TILE, d), lambda i: (i, 0)), pl.BlockSpec((1, d), lambda i: (0, 0)), ANY],
        out_specs=[pl.BlockSpec((ROW_TILE, d), lambda i: (i, 0)), ANY],
        out_shape=[jax.ShapeDtypeStruct((s, d), BF16),
                   jax.ShapeDtypeStruct((N_CHIPS, 2, r_pack // 2, w_pack), shard_pack.dtype)],
        scratch_shapes=[pltpu.SemaphoreType.DMA((6,)), pltpu.SemaphoreType.DMA((6,))],
        compiler_params=_cparams(("arbitrary",)),
    )(x, g, shard_pack.reshape(2, r_pack // 2, w_pack))
    return h, _fill_own_slot(others, shard_pack)


def _alibi_slopes():
    return np.exp2(np.float32(-8.0) * np.arange(1, N_DIL_HEADS + 1, dtype=np.float32) / np.float32(N_DIL_HEADS))


def _head_lane_mask(h, rows):
    lane = lax.broadcasted_iota(jnp.int32, (rows, DIL_W), 1)
    return (lane >= h * HEAD_DIM) & (lane < (h + 1) * HEAD_DIM)


def _band_terms(dil, has_prev):
    qi = lax.broadcasted_iota(jnp.int32, (BLOCK, 2 * BLOCK), 0)
    kj = lax.broadcasted_iota(jnp.int32, (BLOCK, 2 * BLOCK), 1)
    steps = qi + BLOCK - kj
    valid = (steps >= 0) & (steps <= BLOCK) & ((kj >= BLOCK) | has_prev)
    return valid, steps.astype(F32) * float(dil)


def _load_halves(ref, rows):
    return jnp.concatenate([ref[0, rows, :], ref[1, rows, :]], axis=1)


def _store_halves(ref, rows, value):
    ref[0, rows, :] = value[:, :128]
    ref[1, rows, :] = value[:, 128:]


def _dil_fwd_all(qkv_groups):
    s = qkv_groups[0].shape[0]
    steps = s // BLOCK
    n_groups = len(DIL_GROUPS)
    dils = [d for _, d in DIL_GROUPS]
    views = [qkv_groups[g].reshape(s // dils[g], dils[g] * 3 * DIL_W) for g in range(n_groups)]
    slopes = _alibi_slopes()

    def spec(dil, which, prev):
        def index(t):
            n = t // dil
            return (jnp.maximum(n - 1, 0) if prev else n, (t % dil) * 3 + which)
        return pl.BlockSpec((BLOCK, DIL_W), index)

    def body(*refs):
        t = pl.program_id(0)
        ins, outs = refs[:5 * n_groups], refs[5 * n_groups:]
        masks = [_head_lane_mask(h, BLOCK) for h in range(DIL_HEADS)]
        work = []
        for g in range(n_groups):
            q_ref, kc_ref, kp_ref, vc_ref, vp_ref = ins[5 * g:5 * g + 5]
            q = q_ref[...]
            k2 = jnp.concatenate([kp_ref[...], kc_ref[...]], axis=0)
            v2 = jnp.concatenate([vp_ref[...], vc_ref[...]], axis=0)
            logits = [_dot_nt(jnp.where(masks[h], q, jnp.zeros_like(q)), k2) for h in range(DIL_HEADS)]
            work.append((v2, logits))
        probs = []
        for g in range(n_groups):
            valid, dist = _band_terms(dils[g], t // dils[g] > 0)
            ps, lses = [], []
            for h in range(DIL_HEADS):
                slope = float(slopes[g * DIL_HEADS + h])
                lg = jnp.where(valid, work[g][1][h] * 0.125 - slope * dist, NEG_INF)
                mx = jnp.max(lg, axis=1, keepdims=True)
                lse = mx + jnp.log(jnp.sum(jnp.exp(lg - mx), axis=1, keepdims=True))
                ps.append(jnp.exp(lg - lse).astype(BF16))
                lses.append(lse)
            probs.append((ps, lses))
        for g in range(n_groups):
            dil = dils[g]
            mine = pl.ds(t % dil, BLOCK, stride=dil) if dil > 1 else slice(None)
            o_acc = jnp.zeros((BLOCK, DIL_W), F32)
            lse_acc = jnp.zeros((BLOCK, DIL_W), F32)
            for h in range(DIL_HEADS):
                o_acc = jnp.where(masks[h], _dot_nn(probs[g][0][h], work[g][0]), o_acc)
                lse_acc = jnp.where(masks[h], probs[g][1][h], lse_acc)
            _store_halves(outs[2 * g], mine, o_acc)
            _store_halves(outs[2 * g + 1], mine, lse_acc)

    in_specs, out_specs, operands = [], [], []
    for g, dil in enumerate(dils):
        in_specs += [spec(dil, 0, False), spec(dil, 1, False), spec(dil, 1, True), spec(dil, 2, False),
                     spec(dil, 2, True)]
        out_specs += [pl.BlockSpec((2, BLOCK * dil, 128), functools.partial(lambda d, t: (0, t // d, 0), dil))] * 2
        operands += [views[g]] * 5
    res = pl.pallas_call(
        body,
        name="dil_fwd",
        grid=(steps,),
        in_specs=in_specs,
        out_specs=out_specs,
        out_shape=[jax.ShapeDtypeStruct((2, s, 128), F32)] * (2 * n_groups),
        compiler_params=_cparams(("arbitrary",)),
    )(*operands)
    return [(res[2 * g], res[2 * g + 1]) for g in range(n_groups)]


def _dil_bwd_all(qkv_groups, dos, lses, cterms):
    s = qkv_groups[0].shape[0]
    steps = s // BLOCK
    n_groups = len(DIL_GROUPS)
    dils = [d for _, d in DIL_GROUPS]
    views = [qkv_groups[g].reshape(s // dils[g], dils[g] * 3 * DIL_W) for g in range(n_groups)]
    slopes = _alibi_slopes()
    n_in = 12

    def block_of(dil, shift):
        nb = steps // dil

        def index(t):
            n = t // dil
            if shift < 0:
                return jnp.maximum(n - 1, 0)
            return n if shift == 0 else jnp.minimum(n + 1, nb - 1)
        return index

    def col(dil, which, shift):
        index = block_of(dil, shift)
        return pl.BlockSpec((BLOCK, DIL_W), lambda t: (index(t), (t % dil) * 3 + which))

    def own(dil, shift):
        index = block_of(dil, shift)
        return pl.BlockSpec((2, BLOCK * dil, 128), lambda t: (0, index(t), 0))

    def head_col(v, hm):
        return jnp.max(jnp.where(hm, v, NEG_INF), axis=1, keepdims=True)

    def body(*refs):
        t = pl.program_id(0)
        masks = [_head_lane_mask(h, BLOCK) for h in range(DIL_HEADS)]
        work = []
        for g, dil in enumerate(dils):
            (q_ref, qn_ref, kc_ref, kp_ref, vc_ref, vp_ref, do_ref, don_ref, lse_ref, lsen_ref, c_ref,
             cn_ref) = refs[n_in * g:n_in * (g + 1)]
            mine = pl.ds(t % dil, BLOCK, stride=dil) if dil > 1 else slice(None)
            q, qn = q_ref[...], qn_ref[...]
            kc, vc = kc_ref[...], vc_ref[...]
            k2 = jnp.concatenate([kp_ref[...], kc], axis=0)
            v2 = jnp.concatenate([vp_ref[...], vc], axis=0)
            dov, donv = _load_halves(do_ref, mine), _load_halves(don_ref, mine)
            side = (_load_halves(lse_ref, mine), _load_halves(lsen_ref, mine),
                    _load_halves(c_ref, mine), _load_halves(cn_ref, mine))
            qhs = [jnp.where(hm, q, jnp.zeros_like(q)) for hm in masks]
            qnhs = [jnp.where(hm, qn, jnp.zeros_like(qn)) for hm in masks]
            dohs = [jnp.where(hm, dov, 0.0).astype(BF16) for hm in masks]
            donhs = [jnp.where(hm, donv, 0.0).astype(BF16) for hm in masks]
            prods = ([_dot_nt(qhs[h], k2) for h in range(DIL_HEADS)], [_dot_nt(dohs[h], v2) for h in range(DIL_HEADS)],
                     [_dot_nt(qnhs[h], kc) for h in range(DIL_HEADS)], [_dot_nt(donhs[h], vc) for h in range(DIL_HEADS)])
            work.append((k2, qhs, qnhs, dohs, donhs, side, prods))
        grads = []
        for g, dil in enumerate(dils):
            n, nb = t // dil, steps // dil
            valid, dist = _band_terms(dil, n > 0)
            valid_n = _band_terms(dil, True)[0][:, :BLOCK] & (n < nb - 1)
            dist_n = dist[:, :BLOCK]
            (lsev, lsenv, cv, cnv), (logit, dp, logit_n, dp_n) = work[g][5], work[g][6]
            p16, dlog, pn16, dlog_n = [], [], [], []
            for h in range(DIL_HEADS):
                hm, slope = masks[h], float(slopes[g * DIL_HEADS + h])
                p = jnp.where(valid, jnp.exp(logit[h] * 0.125 - slope * dist - head_col(lsev, hm)), 0.0)
                dlog.append((p * (dp[h] + head_col(cv, hm)) * 0.125).astype(BF16))
                p16.append(p.astype(BF16))
                pn = jnp.where(valid_n, jnp.exp(logit_n[h] * 0.125 - slope * dist_n - head_col(lsenv, hm)), 0.0)
                dlog_n.append((pn * (dp_n[h] + head_col(cnv, hm)) * 0.125).astype(BF16))
                pn16.append(pn.astype(BF16))
            grads.append((p16, dlog, pn16, dlog_n))
        for g in range(n_groups):
            k2, qhs, qnhs, dohs, donhs = work[g][:5]
            p16, dlog, pn16, dlog_n = grads[g]
            dq_acc = jnp.zeros((BLOCK, DIL_W), F32)
            dk_acc = jnp.zeros((BLOCK, DIL_W), F32)
            dv_acc = jnp.zeros((BLOCK, DIL_W), F32)
            for h in range(DIL_HEADS):
                dq_acc = jnp.where(masks[h], _dot_nn(dlog[h], k2), dq_acc)
                dk_acc += _dot_tn(dlog[h][:, BLOCK:], qhs[h]) + _dot_tn(dlog_n[h], qnhs[h])
                dv_acc += _dot_tn(p16[h][:, BLOCK:], dohs[h]) + _dot_tn(pn16[h], donhs[h])
            dq_ref, dk_ref, dv_ref = refs[n_in * n_groups + 3 * g:n_in * n_groups + 3 * g + 3]
            dq_ref[...] = dq_acc.astype(BF16)
            dk_ref[...] = dk_acc.astype(BF16)
            dv_ref[...] = dv_acc.astype(BF16)

    in_specs, out_specs, out_shape, operands = [], [], [], []
    for g, dil in enumerate(dils):
        in_specs += [col(dil, 0, 0), col(dil, 0, 1), col(dil, 1, 0), col(dil, 1, -1), col(dil, 2, 0), col(dil, 2, -1),
                     own(dil, 0), own(dil, 1), own(dil, 0), own(dil, 1), own(dil, 0), own(dil, 1)]
        operands += [views[g]] * 6 + [dos[g], dos[g], lses[g], lses[g], cterms[g], cterms[g]]
        out_specs += [pl.BlockSpec((BLOCK, DIL_W), functools.partial(lambda d, t: (t // d, t % d), dil))] * 3
        out_shape += [jax.ShapeDtypeStruct((s // dil, dil * DIL_W), BF16)] * 3
    outs = pl.pallas_call(
        body,
        name="dil_bwd",
        grid=(steps,),
        in_specs=in_specs,
        out_specs=out_specs,
        out_shape=out_shape,
        compiler_params=_cparams(("parallel",)),
    )(*operands)
    return [tuple(v.reshape(s, DIL_W) for v in outs[3 * g:3 * g + 3]) for g in range(n_groups)]


SB_PAIRS = SB_HEADS // 2
SB_COL0 = 0
LOG2E = 1.4426950408889634


SB_EXP_CLAMP = 64.0


def _sb_softplus2(zs):
    t = 1.0 + jnp.exp2(jnp.minimum(zs, SB_EXP_CLAMP))
    return jnp.maximum(jnp.log(t) * LOG2E, zs)


def _sb_consts(nkb):
    row = lax.broadcasted_iota(jnp.int32, (SB_BQ, SB_BK), 0)
    colk = lax.broadcasted_iota(jnp.int32, (SB_BQ, SB_BK), 1)
    rr = lax.broadcasted_iota(jnp.int32, (SB_BK, SB_BK), 0)
    cc = lax.broadcasted_iota(jnp.int32, (SB_BK, SB_BK), 1)
    lane = lax.broadcasted_iota(jnp.int32, (SB_BQ, 128), 1)
    assert 2 * nkb <= 128
    return colk < row, rr, cc, lane < HEAD_DIM, lane


def _split_heads(t):
    first = lax.broadcasted_iota(jnp.int32, t.shape, 1) < HEAD_DIM
    zero = jnp.zeros_like(t)
    return jnp.where(first, t, zero), jnp.where(first, zero, t)


def _sb_fwd(qkv, shard_pack):
    s = qkv.shape[0]
    nq, nkb = s // SB_BQ, s // SB_BK
    zscale = LOG2E / math.sqrt(HEAD_DIM)
    r_pack, w_pack = shard_pack.shape

    def body(q_ref, k_ref, v_ref, pack_ref, o_ref, a_row, others_ref, zs_scr, a_scr, acc_scr, cl_scr,
             send_sems, recv_sems):
        i = pl.program_id(1)
        pair = pl.program_id(0)
        gather = (pack_ref, others_ref, send_sems, recv_sems)

        @pl.when((pair == 0) & (i == 0))
        def _():
            _gather_start(*gather)

        @pl.when((pair == 1) & (i == 0))
        def _():
            _gather_pass_on(*gather)

        @pl.when((pair == SB_PAIRS - 1) & (i == nq - 1))
        def _():
            _gather_finish(*gather)

        causal, rr, cc, _, _ = _sb_consts(nkb)
        later = (rr > cc).astype(BF16)
        qh = _split_heads(q_ref[...])

        def rows(j):
            return pl.ds(pl.multiple_of(j * SB_BK, SB_BK), SB_BK)

        def scores_to(slot, j):
            kb = k_ref[rows(j), :]
            for hh in range(2):
                zs_scr[slot, hh] = _dot_nt(qh[hh], kb) * zscale

        def weights(slot, j, masked):
            xs, sums, sufs = [], [], []
            for hh in range(2):
                zs = zs_scr[slot, hh]
                sp = _sb_softplus2(zs)
                if masked:
                    sp = jnp.where(causal, sp, 0.0)
                xs.append(zs - sp)
                sums.append(jnp.sum(sp, axis=1, keepdims=True))
                sufs.append(_dot_f32_by_01(sp, later, 2))
            for hh in range(2):
                cl = cl_scr[hh]
                a = jnp.exp2(xs[hh] - (sufs[hh] + jnp.concatenate([cl, cl], axis=1)))
                if masked:
                    a = jnp.where(causal, a, 0.0)
                a16 = a.astype(BF16)
                a_scr[slot, :, hh * SB_BK:(hh + 1) * SB_BK] = a16
                a_row[0, 0, j, :, hh * SB_BK:(hh + 1) * SB_BK] = a16
                cl_scr[hh] = cl + sums[hh]

        def add_av(slot, j):
            v0, v1 = _split_heads(v_ref[rows(j), :])
            acc_scr[...] += _dot_nn(a_scr[slot], jnp.concatenate([v0, v1], axis=0))

        acc_scr[...] = jnp.zeros_like(acc_scr)
        cl_scr[...] = jnp.zeros_like(cl_scr)
        scores_to(0, i)
        scores_to(1, jnp.maximum(i - 1, 0))
        weights(0, i, True)

        def step(j, prev, cur):
            scores_to(prev, jnp.maximum(j - 1, 0))
            add_av(prev, j + 1)
            weights(cur, j, False)

        def two_steps(u, _):
            j = i - 1 - 2 * u
            step(j, 0, 1)
            step(j - 1, 1, 0)
            return 0

        lax.fori_loop(0, i // 2, two_steps, 0)

        @pl.when(i % 2 == 1)
        def _():
            step(0, 0, 1)
            add_av(1, 0)

        @pl.when(i % 2 == 0)
        def _():
            add_av(0, 0)

        o_ref[...] = acc_scr[...]

    def full(which):
        return pl.BlockSpec((s, 128), lambda p, i: (0, SB_COL0 + 4 * which + p))

    return pl.pallas_call(
        body,
        name="sb_fwd",
        grid=(SB_PAIRS, nq),
        in_specs=[pl.BlockSpec((SB_BQ, 128), lambda p, i: (i, SB_COL0 + p)), full(1), full(2), ANY],
        out_specs=[pl.BlockSpec((SB_BQ, 128), lambda p, i: (i, p)),
                   pl.BlockSpec((1, 1, nkb, SB_BQ, 2 * SB_BK), lambda p, i: (p, i, 0, 0, 0)), ANY],
        out_shape=[jax.ShapeDtypeStruct((s, SB_W), F32),
                   jax.ShapeDtypeStruct((SB_PAIRS, nq, nkb, SB_BQ, 2 * SB_BK), BF16),
                   jax.ShapeDtypeStruct((N_CHIPS, 2, r_pack // 2, w_pack), shard_pack.dtype)],
        scratch_shapes=[pltpu.VMEM((2, 2, SB_BQ, SB_BK), F32), pltpu.VMEM((2, SB_BQ, 2 * SB_BK), BF16),
                        pltpu.VMEM((SB_BQ, 128), F32), pltpu.VMEM((2, SB_BQ, 128), F32),
                        pltpu.SemaphoreType.DMA((6,)), pltpu.SemaphoreType.DMA((6,))],
        compiler_params=_cparams(("arbitrary", "arbitrary")),
    )(qkv, qkv, qkv, shard_pack.reshape(2, r_pack // 2, w_pack))


def _sb_bwd(qkv, do, a_hbm, chip_sums):
    s = qkv.shape[0]
    nq, nkb = s // SB_BQ, s // SB_BK
    scale = 1.0 / math.sqrt(HEAD_DIM)
    zscale = LOG2E * scale

    def body(q_ref, k_ref, v_ref, do_ref, a_row, sums_ref, dq_ref, dk_ref, dv_ref, got_ref,
             zs_scr, da_scr, dz_scr, a_scr, cg_scr, send_sems, recv_sems):
        i = pl.program_id(1)
        pair = pl.program_id(0)
        first_step = (pair == 0) & (i == 0)
        last_step = (pair == SB_PAIRS - 1) & (i == nq - 1)

        @pl.when(first_step)
        def _():
            _exchange_start(sums_ref, got_ref, send_sems, recv_sems)

        @pl.when(i == 0)
        def _():
            dk_ref[...] = jnp.zeros_like(dk_ref)
            dv_ref[...] = jnp.zeros_like(dv_ref)

        causal, rr, cc, first, _ = _sb_consts(nkb)
        earlier = (rr < cc).astype(BF16)
        q2 = q_ref[...]
        qh = _split_heads(q2)
        do2 = do_ref[...].astype(BF16)
        doh = _split_heads(do2)

        def rows(j):
            return pl.ds(pl.multiple_of(j * SB_BK, SB_BK), SB_BK)

        def products_to(slot, j):
            kb, vb = k_ref[rows(j), :], v_ref[rows(j), :]
            for hh in range(2):
                zs_scr[slot, hh] = _dot_nt(qh[hh], kb) * (-zscale)
                da_scr[slot, hh] = _dot_nt(doh[hh], vb)

        head0_rows = lax.broadcasted_iota(jnp.int32, (128, SB_BK), 0) < HEAD_DIM

        def by_head(t):
            return jnp.where(head0_rows, t[:, :SB_BK], t[:, SB_BK:])

        def apply(slot, j):
            k0, k1 = _split_heads(k_ref[rows(j), :])
            dq_ref[...] += _dot_nn(dz_scr[slot], jnp.concatenate([k0, k1], axis=0)) * scale
            dk_ref[0, j] += by_head(_dot_tn(q2, dz_scr[slot])) * scale
            dv_ref[0, j] += by_head(_dot_tn(do2, a_scr[slot]))

        def grads(slot, j, masked):
            gs, gpres = [], []
            for hh in range(2):
                a16 = a_row[0, 0, j, :, hh * SB_BK:(hh + 1) * SB_BK]
                a_scr[slot, :, hh * SB_BK:(hh + 1) * SB_BK] = a16
                g = a16.astype(F32) * da_scr[slot, hh]
                gs.append(g)
                gpres.append(_dot_f32_by_01(g, earlier, 1))
            sigs = []
            for hh in range(2):
                e = jnp.exp2(jnp.minimum(zs_scr[slot, hh], SB_EXP_CLAMP))
                sigs.append(pl.reciprocal(1.0 + e, approx=True))
            for hh in range(2):
                cg = cg_scr[hh]
                dz = gs[hh] - (gs[hh] + (gpres[hh] + jnp.concatenate([cg, cg], axis=1))) * sigs[hh]
                if masked:
                    dz = jnp.where(causal, dz, 0.0)
                dz_scr[slot, :, hh * SB_BK:(hh + 1) * SB_BK] = dz.astype(BF16)
                cg_scr[hh] = cg + jnp.sum(gs[hh], axis=1, keepdims=True)

        dq_ref[...] = jnp.zeros_like(dq_ref)
        cg_scr[...] = jnp.zeros_like(cg_scr)
        dz_scr[1] = jnp.zeros((SB_BQ, 2 * SB_BK), BF16)
        a_scr[1] = jnp.zeros((SB_BQ, 2 * SB_BK), BF16)
        products_to(0, 0)

        def step(j, cur, nxt):
            products_to(nxt, j + 1)
            apply(nxt, jnp.maximum(j - 1, 0))
            grads(cur, j, False)

        def two_steps(u, _):
            step(2 * u, 0, 1)
            step(2 * u + 1, 1, 0)
            return 0

        lax.fori_loop(0, i // 2, two_steps, 0)

        def last(cur, nxt):
            apply(nxt, jnp.maximum(i - 1, 0))
            grads(cur, i, True)
            apply(cur, i)

        @pl.when(i % 2 == 1)
        def _():
            step(i - 1, 0, 1)
            last(1, 0)

        @pl.when(i % 2 == 0)
        def _():
            last(0, 1)

        @pl.when(last_step)
        def _():
            _exchange_wait(sums_ref, got_ref, send_sems, recv_sems)

    def full(which):
        return pl.BlockSpec((s, 128), lambda p, i: (0, SB_COL0 + 4 * which + p))

    qblk = pl.BlockSpec((SB_BQ, 128), lambda p, i: (i, p))
    acc = pl.BlockSpec((1, nkb, 128, SB_BK), lambda p, i: (p, 0, 0, 0))
    acc_shape = jax.ShapeDtypeStruct((SB_PAIRS, nkb, 128, SB_BK), F32)
    dq, dk_t, dv_t, got = pl.pallas_call(
        body,
        name="sb_bwd",
        grid=(SB_PAIRS, nq),
        in_specs=[pl.BlockSpec((SB_BQ, 128), lambda p, i: (i, SB_COL0 + p)), full(1), full(2), qblk,
                  pl.BlockSpec((1, 1, nkb, SB_BQ, 2 * SB_BK), lambda p, i: (p, i, 0, 0, 0)), ANY],
        out_specs=[qblk, acc, acc, ANY],
        out_shape=[jax.ShapeDtypeStruct((s, SB_W), F32), acc_shape, acc_shape,
                   jax.ShapeDtypeStruct(chip_sums.shape, chip_sums.dtype)],
        scratch_shapes=[pltpu.VMEM((2, 2, SB_BQ, SB_BK), F32), pltpu.VMEM((2, 2, SB_BQ, SB_BK), F32),
                        pltpu.VMEM((2, SB_BQ, 2 * SB_BK), BF16), pltpu.VMEM((2, SB_BQ, 2 * SB_BK), BF16),
                        pltpu.VMEM((2, SB_BQ, 128), F32),
                        pltpu.SemaphoreType.DMA((3,)), pltpu.SemaphoreType.DMA((3,))],
        compiler_params=_cparams(("arbitrary", "arbitrary")),
    )(qkv, qkv, qkv, do, a_hbm, chip_sums)

    def untranspose(t):
        return jnp.transpose(t, (1, 3, 0, 2)).reshape(s, SB_W)

    return dq, untranspose(dk_t), untranspose(dv_t), got


MERGE_TILE = 256


def _group_mix(lses):
    mx = jnp.maximum(jnp.maximum(lses[0], lses[1]), lses[2])
    es = [jnp.exp(t - mx) for t in lses]
    den = es[0] + es[1] + es[2]
    return [e / den for e in es]


def _merge_fwd(o_groups, lse_groups, o_sb, gl, b_gate, w_up_dil, w_up_sb):
    s = gl.shape[0]
    t = MERGE_TILE

    def body(o0, o1, o2, l0, l1, l2, ob_ref, gl_ref, bg_ref, wd_ref, ws_ref, merged_ref, oa_ref):
        rows = slice(None)
        w = _group_mix([_load_halves(l, rows) for l in (l0, l1, l2)])
        og = [_load_halves(o, rows) for o in (o0, o1, o2)]
        oa = (w[0] * og[0] + w[1] * og[1] + w[2] * og[2]).astype(BF16)
        ua = _dot_nn(oa, wd_ref[...])
        ub = _dot_nn(ob_ref[...].astype(BF16), ws_ref[...])
        gate = jax.nn.sigmoid(gl_ref[...] + bg_ref[...])
        merged_ref[...] = (gate[:, :D_MODEL] * ua + gate[:, D_MODEL:] * ub).astype(BF16)
        oa_ref[...] = oa

    dil = pl.BlockSpec((t, DIL_W), lambda i: (i, 0))
    halves = pl.BlockSpec((2, t, 128), lambda i: (0, i, 0))
    const = lambda shape: pl.BlockSpec(shape, lambda i: (0, 0))
    return pl.pallas_call(
        body,
        name="merge_fwd",
        grid=(s // t,),
        in_specs=[halves] * 6 + [pl.BlockSpec((t, SB_W), lambda i: (i, 0)), pl.BlockSpec((t, GATE_W), lambda i: (i, 0)),
                                 const((1, GATE_W)), const((DIL_W, D_MODEL)), const((SB_W, D_MODEL))],
        out_specs=[pl.BlockSpec((t, D_MODEL), lambda i: (i, 0)), dil],
        out_shape=[jax.ShapeDtypeStruct((s, D_MODEL), BF16), jax.ShapeDtypeStruct((s, DIL_W), BF16)],
        compiler_params=_cparams(("parallel",)),
    )(*o_groups, *lse_groups, o_sb, gl, b_gate, w_up_dil, w_up_sb)


def _merge_bwd(dmerged, o_groups, lse_groups, o_sb, gl, b_gate, w_up_dil, w_up_sb, swap):
    s = gl.shape[0]
    t = MERGE_TILE
    n_chunks, r_swap, w_swap = swap.shape
    swap = swap.reshape(n_chunks, 2, r_swap // 2, w_swap)

    def body(dm_ref, o0, o1, o2, l0, l1, l2, ob_ref, gl_ref, bg_ref, wd_ref, ws_ref, swap_ref,
             dua_ref, dub_ref, dgl_ref, dbg_ref, dosb_ref, d0, d1, d2, c0, c1, c2, got_ref, send_sem, recv_sem):
        i = pl.program_id(0)

        @pl.when(i == 0)
        def _():
            _swap_copy(swap_ref, got_ref, send_sem, recv_sem).start()

        @pl.when(i == pl.num_programs(0) - 1)
        def _():
            _swap_copy(swap_ref, got_ref, send_sem, recv_sem).wait()

        rows = slice(None)
        og = [_load_halves(o, rows) for o in (o0, o1, o2)]
        w = _group_mix([_load_halves(l, rows) for l in (l0, l1, l2)])
        oa = (w[0] * og[0] + w[1] * og[1] + w[2] * og[2]).astype(BF16)
        ua = _dot_nn(oa, wd_ref[...])
        ub = _dot_nn(ob_ref[...].astype(BF16), ws_ref[...])
        gate = jax.nn.sigmoid(gl_ref[...] + bg_ref[...])
        ga, gb = gate[:, :D_MODEL], gate[:, D_MODEL:]
        dm = dm_ref[...]
        dua = (dm * ga).astype(BF16)
        dub = (dm * gb).astype(BF16)
        dua_ref[...] = dua
        dub_ref[...] = dub
        dgl_a = dm * ua * ga * (1.0 - ga)
        dgl_b = dm * ub * gb * (1.0 - gb)
        dgl_ref[:, :D_MODEL] = dgl_a.astype(BF16)
        dgl_ref[:, D_MODEL:] = dgl_b.astype(BF16)
        part = jnp.concatenate([jnp.sum(dgl_a.reshape(t // 8, 8, D_MODEL), axis=0),
                                jnp.sum(dgl_b.reshape(t // 8, 8, D_MODEL), axis=0)], axis=1)

        @pl.when(i == 0)
        def _():
            dbg_ref[...] = part

        @pl.when(i > 0)
        def _():
            dbg_ref[...] += part

        dosb_ref[...] = _dot_nt(dub, ws_ref[...])
        doa = _dot_nt(dua, wd_ref[...])
        rr = lax.broadcasted_iota(jnp.int32, (DIL_W, DIL_W), 0) // HEAD_DIM
        cc = lax.broadcasted_iota(jnp.int32, (DIL_W, DIL_W), 1) // HEAD_DIM
        same_head = (rr == cc).astype(BF16)
        dw = [_dot_f32_by_01(doa * og[g], same_head, 2) for g in range(3)]
        mean_dw = w[0] * dw[0] + w[1] * dw[1] + w[2] * dw[2]
        for g, (d_ref, c_ref) in enumerate(((d0, c0), (d1, c1), (d2, c2))):
            _store_halves(d_ref, rows, w[g] * doa)
            _store_halves(c_ref, rows, -w[g] * mean_dw)

    dil = pl.BlockSpec((2, t, 128), lambda i: (0, i, 0))
    wide = pl.BlockSpec((t, D_MODEL), lambda i: (i, 0))
    gate2 = pl.BlockSpec((t, GATE_W), lambda i: (i, 0))
    sbw = pl.BlockSpec((t, SB_W), lambda i: (i, 0))
    const = lambda shape: pl.BlockSpec(shape, lambda i: (0, 0))
    return pl.pallas_call(
        body,
        name="merge_bwd",
        grid=(s // t,),
        in_specs=[wide] + [dil] * 6 + [sbw, gate2, const((1, GATE_W)), const((DIL_W, D_MODEL)), const((SB_W, D_MODEL)),
                                       ANY],
        out_specs=[wide, wide, gate2, const((8, GATE_W)), sbw] + [dil] * 6 + [ANY],
        out_shape=[jax.ShapeDtypeStruct((s, D_MODEL), BF16), jax.ShapeDtypeStruct((s, D_MODEL), BF16),
                   jax.ShapeDtypeStruct((s, GATE_W), BF16), jax.ShapeDtypeStruct((8, GATE_W), F32),
                   jax.ShapeDtypeStruct((s, SB_W), F32)] + [jax.ShapeDtypeStruct((2, s, 128), F32)] * 6
        + [jax.ShapeDtypeStruct((n_chunks, r_swap // 2, w_swap), swap.dtype)],
        scratch_shapes=[pltpu.SemaphoreType.DMA, pltpu.SemaphoreType.DMA],
        compiler_params=_cparams(("arbitrary",)),
    )(dmerged, *o_groups, *lse_groups, o_sb, gl, b_gate, w_up_dil, w_up_sb, swap)


ANY = pl.BlockSpec(memory_space=pl.ANY)


def _place():
    x, y, c = lax.axis_index("x"), lax.axis_index("y"), lax.axis_index("c")
    other_chips = [(1 - x, y), (x, 1 - y), (1 - x, 1 - y)]
    return x, y, c, other_chips


def _gather_copies(p_ref, out_ref, send_sems, recv_sems):
    x, y, c, chips = _place()
    me, sibling = 2 * x + y, (x, y, 1 - c)
    idx = [2 * chip[0] + chip[1] for chip in chips]

    def copy(k, chip_idx, core, to, src=None):
        return pltpu.make_async_remote_copy(
            src_ref=out_ref.at[chip_idx, core] if src is None else src, dst_ref=out_ref.at[chip_idx, core],
            send_sem=send_sems.at[k], recv_sem=recv_sems.at[k], device_id=to, device_id_type=MESH)

    first = lambda j: copy(j, me, c, (*chips[j], c), src=p_ref.at[c])
    landed = lambda j: copy(j, idx[j], c, (x, y, c))
    passed = lambda j: copy(3 + j, idx[j], c, sibling)
    handed = lambda j: copy(3 + j, idx[j], 1 - c, (x, y, c))
    return first, landed, passed, handed


def _gather_start(*refs):
    first = _gather_copies(*refs)[0]
    for j in range(3):
        first(j).start()


def _gather_pass_on(*refs):
    _, landed, passed, _ = _gather_copies(*refs)
    for j in range(3):
        landed(j).wait_recv()
        passed(j).start()


def _gather_finish(*refs):
    first, _, passed, handed = _gather_copies(*refs)
    for j in range(3):
        handed(j).wait_recv()
    for j in range(3):
        first(j).wait_send()
        passed(j).wait_send()


def _fill_own_slot(others, pack):
    n, _, rh, wd = others.shape
    me = 2 * lax.axis_index("x") + lax.axis_index("y")
    mine = lax.broadcasted_iota(jnp.int32, (n, 1, 1, 1), 0) == me
    return jnp.where(mine, pack.reshape(1, 2, rh, wd), others).reshape(n, 2 * rh, wd)


def _swap_copy(g_ref, out_ref, send_sem, recv_sem):
    x, y, c, _ = _place()
    return pltpu.make_async_remote_copy(
        src_ref=g_ref.at[:, 1 - c], dst_ref=out_ref,
        send_sem=send_sem, recv_sem=recv_sem, device_id=(x, y, 1 - c), device_id_type=MESH)


def _swap_halves(g):
    n, r, wd = g.shape
    rh = r // 2
    g = g.reshape(n, 2, rh, wd)

    def body(g_ref, out_ref, send_sem, recv_sem):
        cp = _swap_copy(g_ref, out_ref, send_sem, recv_sem)
        cp.start()
        cp.wait()

    return pl.pallas_call(
        body,
        name="grad_swap_halves",
        in_specs=[ANY],
        out_specs=ANY,
        out_shape=jax.ShapeDtypeStruct((n, rh, wd), g.dtype),
        scratch_shapes=[pltpu.SemaphoreType.DMA, pltpu.SemaphoreType.DMA],
    )(g)


def _add_halves(g, got, core):
    n, r, wd = g.shape
    rh = r // 2
    t = rh // 4
    nt = rh // t

    def body(c_ref, a_ref, b_ref, o_ref):
        o_ref[...] = (a_ref[0] + b_ref[...]).astype(BF16)

    grid_spec = pltpu.PrefetchScalarGridSpec(
        num_scalar_prefetch=1,
        grid=(n, nt),
        in_specs=[pl.BlockSpec((1, 1, t, wd), lambda s, i, c: (s, c[0], i, 0)),
                  pl.BlockSpec((1, t, wd), lambda s, i, c: (s, i, 0))],
        out_specs=pl.BlockSpec((1, t, wd), lambda s, i, c: (s, i, 0)),
    )
    return pl.pallas_call(
        body,
        name="grad_add_halves",
        grid_spec=grid_spec,
        out_shape=jax.ShapeDtypeStruct((n, rh, wd), BF16),
        compiler_params=_cparams(("parallel", "parallel")),
    )(core, g.reshape(n, 2, rh, wd), got)


def _exchange_copies(h_ref, out_ref, send_sems, recv_sems):
    x, y, c, chips = _place()
    me = 2 * x + y

    def copy(j, slot):
        them = 2 * chips[j][0] + chips[j][1]
        return pltpu.make_async_remote_copy(
            src_ref=h_ref.at[them], dst_ref=out_ref.at[me if slot == "mine" else them],
            send_sem=send_sems.at[j], recv_sem=recv_sems.at[j], device_id=(*chips[j], c), device_id_type=MESH)

    return (lambda j: copy(j, "mine")), (lambda j: copy(j, "theirs"))


def _exchange_start(h_ref, out_ref, send_sems, recv_sems):
    send = _exchange_copies(h_ref, out_ref, send_sems, recv_sems)[0]
    for j in range(3):
        send(j).start()


def _exchange_wait(h_ref, out_ref, send_sems, recv_sems):
    send, arrival = _exchange_copies(h_ref, out_ref, send_sems, recv_sems)
    for j in range(3):
        arrival(j).wait_recv()
    for j in range(3):
        send(j).wait_send()


def _sum_chips(b, h, chip):
    n, rh, wd = b.shape
    t = rh // 4

    def body(chip_ref, b_ref, own_ref, o_ref):
        own = own_ref[0]
        s0, s1, s2, s3 = (jnp.where(chip_ref[0] == k, own, b_ref[k]).astype(F32) for k in range(n))
        o_ref[...] = ((s0 + s1) + s2) + s3

    grid_spec = pltpu.PrefetchScalarGridSpec(
        num_scalar_prefetch=1,
        grid=(rh // t,),
        in_specs=[pl.BlockSpec((n, t, wd), lambda i, chip: (0, i, 0)),
                  pl.BlockSpec((1, t, wd), lambda i, chip: (chip[0], i, 0))],
        out_specs=pl.BlockSpec((t, wd), lambda i, chip: (i, 0)),
    )
    return pl.pallas_call(
        body,
        name="grad_sum_chips",
        grid_spec=grid_spec,
        out_shape=jax.ShapeDtypeStruct((rh, wd), F32),
        compiler_params=_cparams(("parallel",)),
    )(chip, b, h)


def _join_halves(tc):
    rh, wd = tc.shape

    def body(t_ref, out_ref, send_sem, recv_sem):
        x, y, c, _ = _place()
        cp = pltpu.make_async_remote_copy(
            src_ref=t_ref, dst_ref=out_ref.at[c],
            send_sem=send_sem, recv_sem=recv_sem, device_id=(x, y, 1 - c), device_id_type=MESH)
        cp.start()
        cp.wait()

    halves = pl.pallas_call(
        body,
        name="grad_join_halves",
        in_specs=[ANY],
        out_specs=ANY,
        out_shape=jax.ShapeDtypeStruct((2, rh, wd), tc.dtype),
        scratch_shapes=[pltpu.SemaphoreType.DMA, pltpu.SemaphoreType.DMA],
    )(tc)
    return lax.dynamic_update_slice(halves, tc[None], (lax.axis_index("c"), 0, 0)).reshape(2 * rh, wd)


def _all_reduce_small(pack):
    rows, lanes = pack.shape

    def body(p_ref, out_ref, buf, send_sems, recv_sems):
        x, y, c, _ = _place()
        me = 4 * x + 2 * y + c
        buf[me] = p_ref[...]
        sends = []
        for k in range(1, N_DEV):
            peer = (x ^ (k >> 2), y ^ ((k >> 1) & 1), c ^ (k & 1))
            sends.append(pltpu.make_async_remote_copy(
                src_ref=p_ref, dst_ref=buf.at[me], send_sem=send_sems.at[k - 1], recv_sem=recv_sems.at[k - 1],
                device_id=peer, device_id_type=MESH))
        for cp in sends:
            cp.start()
        for k in range(1, N_DEV):
            pltpu.make_async_remote_copy(
                src_ref=p_ref, dst_ref=buf.at[me ^ k], send_sem=send_sems.at[k - 1], recv_sem=recv_sems.at[k - 1],
                device_id=(x, y, c), device_id_type=MESH).wait_recv()
        for cp in sends:
            cp.wait_send()
        total = buf[0]
        for d in range(1, N_DEV):
            total = total + buf[d]
        out_ref[...] = total

    vm = pl.BlockSpec(memory_space=pltpu.VMEM)
    return pl.pallas_call(
        body,
        name="all_reduce_small",
        in_specs=[vm],
        out_specs=vm,
        out_shape=jax.ShapeDtypeStruct((rows, lanes), F32),
        scratch_shapes=[pltpu.VMEM((N_DEV, rows, lanes), F32), pltpu.SemaphoreType.DMA((N_DEV - 1,)),
                        pltpu.SemaphoreType.DMA((N_DEV - 1,))],
    )(pack)


def _adamw(g, w, m, v, name):
    rows, cols = g.shape
    t = rows
    for cand in (256, 128, 64, 32, 16, 8):
        if rows % cand == 0:
            t = cand
            break

    def body(g_ref, w_ref, m_ref, v_ref, d_ref, nm_ref, nv_ref):
        gv = g_ref[...]
        mv = ADAM_B1 * m_ref[...] + (1.0 - ADAM_B1) * gv
        vv = ADAM_B2 * v_ref[...] + (1.0 - ADAM_B2) * (gv * gv)
        m_hat = mv / (1.0 - ADAM_B1 ** ADAM_STEP)
        v_hat = vv / (1.0 - ADAM_B2 ** ADAM_STEP)
        d_ref[...] = -ADAM_LR * (m_hat / (jnp.sqrt(v_hat) + ADAM_EPS) + ADAM_WD * w_ref[...])
        nm_ref[...] = mv
        nv_ref[...] = vv

    blk = pl.BlockSpec((t, cols), lambda i: (i, 0))
    return pl.pallas_call(
        body,
        name=name,
        grid=(rows // t,),
        in_specs=[blk] * 4,
        out_specs=[blk] * 3,
        out_shape=[jax.ShapeDtypeStruct((rows, cols), F32)] * 3,
        compiler_params=_cparams(("parallel",)),
    )(g, w, m, v)


PACK_W = 1024
BIG = (("w_in", (D_MODEL, IN_COLS), 1), ("w_up_dil", (DIL_W, D_MODEL), 1), ("w_up_sb", (SB_W, D_MODEL), 1),
       ("w_out", (D_MODEL, D_MODEL), 0), ("w_mlp_in", (D_MODEL, D_FF), 1), ("w_mlp_out", (D_FF, D_MODEL), 0))


def _shard_shape(shape, axis):
    return tuple(d // N_CHIPS if a == axis else d for a, d in enumerate(shape))


MIXER_GROUP, MLP_GROUP = BIG[:4], BIG[4:]
LATE_WEIGHTS = BIG[1:]


def _pack_rows(group=BIG):
    rows, at = {}, 0
    for name, shape, axis in group:
        n = math.prod(_shard_shape(shape, axis)) // PACK_W
        rows[name] = (at, n)
        at += n
    return rows, at


def _pack_shards(shards, group):
    return jnp.concatenate([shards[name].reshape(-1, PACK_W) for name, _, _ in group], axis=0)


def _unpack_full(gathered, group):
    rows, _ = _pack_rows(group)
    full = {}
    for name, shape, axis in group:
        at, n = rows[name]
        parts = gathered[:, at:at + n, :].reshape((N_CHIPS,) + _shard_shape(shape, axis))
        if axis == 0:
            full[name] = parts.reshape(shape)
        else:
            full[name] = jnp.transpose(parts, (1, 0, 2)).reshape(shape)
    return full


def _pack_full_grads(grads, group):
    chunks = []
    for name, shape, axis in group:
        g = grads[name]
        if axis == 0:
            parts = g.reshape((N_CHIPS, shape[0] // N_CHIPS, shape[1]))
        else:
            parts = jnp.transpose(g.reshape((shape[0], N_CHIPS, shape[1] // N_CHIPS)), (1, 0, 2))
        chunks.append(parts.reshape(N_CHIPS, -1, PACK_W))
    return jnp.concatenate(chunks, axis=1)


def _unpack_shard(packed, group):
    rows, _ = _pack_rows(group)
    return {name: packed[rows[name][0]:rows[name][0] + rows[name][1]].reshape(_shard_shape(shape, axis))
            for name, shape, axis in group}


def _local_step(x, target, early_shards, late_shards, norm_mix_g, b_gate, norm_mlp_g, norm_final_g, core):
    h, early = _rms_fwd_and_gather(x, norm_mix_g, early_shards)
    w = {"w_in": jnp.transpose(early, (1, 0, 2)).reshape(D_MODEL, IN_COLS)}
    w_in = w["w_in"]
    sb0 = 9 * DIL_W
    w_sb, w_gate = w_in[:, sb0:QKV_W], w_in[:, QKV_W:]
    w_dil = [jnp.concatenate([w_in[:, (3 * i + g) * DIL_W:(3 * i + g + 1) * DIL_W] for i in range(3)], axis=1)
             for g in range(3)]

    qkv_dil = [_matmul(h, w_dil[g], mode="nn", out_dtypes=(BF16,), name=f"proj_dil_g{g}", tn=768)[0] for g in range(3)]
    (qkv_sb,) = _matmul(h, w_sb, mode="nn", out_dtypes=(BF16,), name="proj_sb", tn=768)
    (gl,) = _matmul(h, w_gate, mode="nn", out_dtypes=(F32,), name="proj_gate")
    dil = _dil_fwd_all(qkv_dil)
    o_groups, lse_groups = [d[0] for d in dil], [d[1] for d in dil]
    o_sb, a_sb, late_others = _sb_fwd(qkv_sb, late_shards)
    w = {**w, **_unpack_full(_fill_own_slot(late_others, late_shards), LATE_WEIGHTS)}
    merged, o_a = _merge_fwd(o_groups, lse_groups, o_sb, gl, b_gate, w["w_up_dil"], w["w_up_sb"])
    def residual_and_norm(acc, res, g):
        x1 = res + acc
        return x1, _rms_rows(x1)[0] * g

    x1, h2 = _matmul(merged, w["w_out"], mode="nn", out_dtypes=(F32, BF16), name="out_proj", tm=ROW_TILE,
                     extras=(x, norm_mlp_g), epilogue=residual_and_norm)
    u, act = _matmul(h2, w["w_mlp_in"], mode="nn", out_dtypes=(BF16, BF16), name="mlp_in",
                     epilogue=lambda acc: (acc, jnp.square(jnp.maximum(acc, 0.0))))

    def residual_and_loss(acc, res, tgt, g):
        xh, r = _rms_rows(res + acc)
        err = xh * g - tgt
        dy = err * (1.0 / D_MODEL)
        dxh = dy * g
        dx2 = r * (dxh - xh * jnp.mean(dxh * xh, axis=-1, keepdims=True))
        return dx2, _rows_sum8(dy * xh), (0.5 / D_MODEL) * _rows_sum8(err * err)

    dx2, dg_final, loss_part = _matmul(
        act, w["w_mlp_out"], mode="nn", out_dtypes=(F32, ("part", F32), ("part", F32)), name="mlp_out", tm=ROW_TILE,
        tk=2048, extras=(x1, target, norm_final_g.reshape(1, D_MODEL)), epilogue=residual_and_loss)

    (du,) = _matmul(dx2, w["w_mlp_out"], mode="nt", out_dtypes=(BF16,), name="mlp_out_dx",
                    extras=(u,), epilogue=lambda acc, uu: (acc * (2.0 * jnp.maximum(uu.astype(F32), 0.0)),))
    pack_shape = (N_CHIPS, 2 * D_MODEL, D_MODEL)
    (half_pack,) = _matmul(act, dx2, mode="tn", out_dtypes=(F32,), name="mlp_out_dw",
                           into=(pack_shape, lambda i, j: (i, 1, 0), None))
    (mlp_pack,) = _matmul(h2, du, mode="tn", out_dtypes=(F32,), name="mlp_in_dw",
                          into=(pack_shape, lambda i, j: (j, 0, 0), half_pack))

    def norm_bwd(acc, xx, dres, g):
        dx, dg = _rms_bwd_rows(acc, xx, g)
        return dres + dx, dg

    dx1, dg_mlp = _matmul(du, w["w_mlp_in"], mode="nt", out_dtypes=(F32, ("part", F32)), name="mlp_in_dx",
                          tm=ROW_TILE, tk=2048, extras=(x1, dx2, norm_mlp_g), epilogue=norm_bwd)

    (dmerged,) = _matmul(dx1, w["w_out"], mode="nt", out_dtypes=(F32,), name="out_proj_dx")
    (g_out,) = _matmul(merged, dx1, mode="tn", out_dtypes=(F32,), name="out_proj_dw")
    mb = _merge_bwd(dmerged, o_groups, lse_groups, o_sb, gl, b_gate, w["w_up_dil"], w["w_up_sb"], mlp_pack)
    dua, dub, dgl, dbg, do_sb = mb[:5]
    do_groups, c_groups = mb[5:8], mb[8:11]
    mlp_sums = _add_halves(mlp_pack, mb[11], core)
    (g_up_dil,) = _matmul(o_a, dua, mode="tn", out_dtypes=(F32,), name="up_dil_dw")
    (g_up_sb,) = _matmul(o_sb, dub, mode="tn", out_dtypes=(F32,), name="up_sb_dw")
    dq_sb, dk_sb, dv_sb, mlp_got = _sb_bwd(qkv_sb, do_sb, a_sb, mlp_sums)
    dil_b = _dil_bwd_all(qkv_dil, do_groups, lse_groups, c_groups)
    dproj = jnp.concatenate(
        [dil_b[g][i].astype(BF16) for i in range(3) for g in range(3)]
        + [t.astype(BF16) for t in (dq_sb, dk_sb, dv_sb)] + [dgl], axis=1)
    (g_in,) = _matmul(h, dproj, mode="tn", out_dtypes=(F32,), name="proj_dw", tm=512, tn=IN_COLS // 2)
    mixer_pack = _pack_full_grads({"w_in": g_in, "w_up_dil": g_up_dil, "w_up_sb": g_up_sb, "w_out": g_out}, MIXER_GROUP)
    mixer_sums = _add_halves(mixer_pack, _swap_halves(mixer_pack), core)
    dh, mixer_got = _matmul(dproj, w["w_in"], mode="nt", out_dtypes=(F32,), name="proj_dx", tk=IN_COLS // 2,
                            exchange=mixer_sums)
    grad_x, dg_mix = _rms_bwd_residual(dh, x, norm_mix_g, dx1)

    small = (dg_mix, dbg, dg_mlp, dg_final, loss_part)
    return grad_x, (mixer_got, mixer_sums), (mlp_got, mlp_sums), small


def kernel(x, norm_mix_g, w_in, b_gate, w_up_dil, w_up_sb, w_out, norm_mlp_g, w_mlp_in, w_mlp_out, norm_final_g, loss_target, m_norm_mix_g, m_w_in, m_b_gate, m_w_up_dil, m_w_up_sb, m_w_out, m_norm_mlp_g, m_w_mlp_in, m_w_mlp_out, m_norm_final_g, v_norm_mix_g, v_w_in, v_b_gate, v_w_up_dil, v_w_up_sb, v_w_out, v_norm_mlp_g, v_w_mlp_in, v_w_mlp_out, v_norm_final_g):
    shards = {"w_in": w_in[0], "w_up_dil": w_up_dil[0], "w_up_sb": w_up_sb[0], "w_out": w_out[0],
              "w_mlp_in": w_mlp_in[0], "w_mlp_out": w_mlp_out[0]}
    moments_m = {"w_in": m_w_in[0], "w_up_dil": m_w_up_dil[0], "w_up_sb": m_w_up_sb[0], "w_out": m_w_out[0],
                 "w_mlp_in": m_w_mlp_in[0], "w_mlp_out": m_w_mlp_out[0]}
    moments_v = {"w_in": v_w_in[0], "w_up_dil": v_w_up_dil[0], "w_up_sb": v_w_up_sb[0], "w_out": v_w_out[0],
                 "w_mlp_in": v_w_mlp_in[0], "w_mlp_out": v_w_mlp_out[0]}

    shards16 = {n: s.astype(BF16) for n, s in shards.items()}
    early_shards = shards16["w_in"]
    late_shards = _pack_shards(shards16, LATE_WEIGHTS)

    core = lax.axis_index("c").astype(jnp.int32).reshape(1)
    chip = (2 * lax.axis_index("x") + lax.axis_index("y")).astype(jnp.int32).reshape(1)
    grad_x, (mixer_got, mixer_sums), (mlp_got, mlp_sums), small = _local_step(
        x[0], loss_target[0], early_shards, late_shards, norm_mix_g, b_gate, norm_mlp_g, norm_final_g, core)

    reduced = _join_halves(_sum_chips(mixer_got, mixer_sums, chip))
    reduced_mlp = _join_halves(_sum_chips(mlp_got, mlp_sums, chip))
    g_shard = {**_unpack_shard(reduced, MIXER_GROUP), **_unpack_shard(reduced_mlp, MLP_GROUP)}

    dg_mix, dbg, dg_mlp, dg_final, loss_part = small
    loss_row = jnp.sum(loss_part, axis=0, keepdims=True)
    small_pack = jnp.concatenate(
        [jnp.sum(dg_mix, axis=0, keepdims=True), jnp.sum(dbg, axis=0, keepdims=True),
         jnp.sum(dg_mlp, axis=0, keepdims=True), jnp.sum(dg_final, axis=0, keepdims=True), loss_row], axis=1)
    n_small = small_pack.shape[1]
    small_sum = _all_reduce_small(small_pack.reshape(n_small // 128, 128)).reshape(1, n_small)
    g_norm_mix = small_sum[:, :D_MODEL]
    g_b_gate = small_sum[:, D_MODEL:3 * D_MODEL]
    g_norm_mlp = small_sum[:, 3 * D_MODEL:4 * D_MODEL]
    g_norm_final = small_sum[:, 4 * D_MODEL:5 * D_MODEL]
    loss = jnp.sum(small_sum[:, 5 * D_MODEL:])

    names = ["norm_mix_g", "w_in", "b_gate", "w_up_dil", "w_up_sb", "w_out", "norm_mlp_g", "w_mlp_in", "w_mlp_out",
             "norm_final_g"]
    grads = dict(g_shard)
    grads.update(norm_mix_g=g_norm_mix, b_gate=g_b_gate, norm_mlp_g=g_norm_mlp, norm_final_g=g_norm_final)
    weights = dict(shards)
    weights.update(norm_mix_g=norm_mix_g, b_gate=b_gate, norm_mlp_g=norm_mlp_g, norm_final_g=norm_final_g.reshape(1, D_MODEL))
    ms = dict(moments_m)
    ms.update(norm_mix_g=m_norm_mix_g, b_gate=m_b_gate, norm_mlp_g=m_norm_mlp_g, norm_final_g=m_norm_final_g.reshape(1, D_MODEL))
    vs = dict(moments_v)
    vs.update(norm_mix_g=v_norm_mix_g, b_gate=v_b_gate, norm_mlp_g=v_norm_mlp_g, norm_final_g=v_norm_final_g.reshape(1, D_MODEL))

    out_shapes = {"norm_mix_g": norm_mix_g.shape, "w_in": w_in.shape, "b_gate": b_gate.shape, "w_up_dil": w_up_dil.shape,
                  "w_up_sb": w_up_sb.shape, "w_out": w_out.shape, "norm_mlp_g": norm_mlp_g.shape,
                  "w_mlp_in": w_mlp_in.shape, "w_mlp_out": w_mlp_out.shape, "norm_final_g": norm_final_g.shape}
    g_out, d_out, m_out, v_out = [], [], [], []
    for n in names:
        d, nm, nv = _adamw(grads[n], weights[n], ms[n], vs[n], "adamw_" + n)
        shape = out_shapes[n]
        g_out.append(grads[n].reshape(shape))
        d_out.append(d.reshape(shape))
        m_out.append(nm.reshape(shape))
        v_out.append(nv.reshape(shape))
    return (loss, grad_x.reshape(x.shape), *g_out, *d_out, *m_out, *v_out)
```

```python
import functools
import math

import jax
import jax.numpy as jnp
import numpy as np
from jax import lax
from jax.experimental import pallas as pl
from jax.experimental.pallas import tpu as pltpu

F32 = jnp.float32
BF16 = jnp.bfloat16
MESH = pl.DeviceIdType.MESH

D_MODEL = 1024
HEAD_DIM = 64
DIL_GROUPS = ((128, 1), (512, 4), (2048, 16))
DIL_HEADS = 4
DIL_W = 256
N_DIL_HEADS = 12
SB_HEADS = 8
SB_W = SB_HEADS * HEAD_DIM
QKV_W = 3 * 3 * DIL_W + 3 * SB_W
GATE_W = 2 * D_MODEL
IN_COLS = QKV_W + GATE_W
D_FF = 4 * D_MODEL
BLOCK = 128
RMS_EPS = 1e-6
NEG_INF = -1e30
N_CHIPS = 4
N_DEV = 8

ADAM_LR = 0.001
ADAM_B1 = 0.9
ADAM_B2 = 0.999
ADAM_EPS = 1e-08
ADAM_WD = 0.01
ADAM_STEP = 10

VMEM_LIMIT = 56 * 1024 * 1024

SB_BQ = 256
SB_BK = 256


def _cparams(sem=None):
    if sem is None:
        return pltpu.CompilerParams(vmem_limit_bytes=VMEM_LIMIT)
    return pltpu.CompilerParams(dimension_semantics=sem, vmem_limit_bytes=VMEM_LIMIT)


def _dot(a, b, dims):
    return lax.dot_general(a, b, (dims, ((), ())), preferred_element_type=F32)


def _dot_nn(a, b):
    return _dot(a, b, ((1,), (0,)))


def _dot_nt(a, b):
    return _dot(a, b, ((1,), (1,)))


def _dot_tn(a, b):
    return _dot(a, b, ((0,), (0,)))


def _dot_f32_by_01(x, m01, pieces=3):
    hi = x.astype(BF16)
    if pieces == 1:
        return _dot_nn(hi, m01)
    r1 = x - hi.astype(F32)
    mid = r1.astype(BF16)
    if pieces == 2:
        return _dot_nn(hi, m01) + _dot_nn(mid, m01)
    lo = (r1 - mid.astype(F32)).astype(BF16)
    return _dot_nn(hi, m01) + _dot_nn(mid, m01) + _dot_nn(lo, m01)


def _matmul(a, b, *, mode, out_dtypes, name, tm=1024, tn=1024, tk=1024, extras=(), epilogue=None, exchange=None,
            into=None):
    if mode == "nn":
        (m, k), (k2, n) = a.shape, b.shape
    elif mode == "nt":
        (m, k), (n, k2) = a.shape, b.shape
    else:
        (k, m), (k2, n) = a.shape, b.shape
    assert k == k2, (a.shape, b.shape, mode)
    tm, tn, tk = min(tm, m), min(tn, n), min(tk, k)
    assert m % tm == 0 and n % tn == 0 and k % tk == 0, (m, n, k, tm, tn, tk)
    nk = k // tk
    n_out = len(out_dtypes)
    n_ex = len(extras)

    if mode == "nn":
        a_spec = pl.BlockSpec((tm, tk), lambda i, j, kk: (i, kk))
        b_spec = pl.BlockSpec((tk, tn), lambda i, j, kk: (kk, j))
        dot = _dot_nn
    elif mode == "nt":
        a_spec = pl.BlockSpec((tm, tk), lambda i, j, kk: (i, kk))
        b_spec = pl.BlockSpec((tn, tk), lambda i, j, kk: (j, kk))
        dot = _dot_nt
    else:
        a_spec = pl.BlockSpec((tk, tm), lambda i, j, kk: (kk, i))
        b_spec = pl.BlockSpec((tk, tn), lambda i, j, kk: (kk, j))
        dot = _dot_tn
    mn_spec = pl.BlockSpec((tm, tn), lambda i, j, kk: (i, j))
    row_spec = pl.BlockSpec((1, tn), lambda i, j, kk: (0, j))
    part_spec = pl.BlockSpec((8, tn), lambda i, j, kk: (i, j))
    ex_specs = [row_spec if e.shape[0] == 1 else mn_spec for e in extras]
    is_part = [isinstance(dt, tuple) for dt in out_dtypes]
    out_dts = [dt[1] if p else dt for dt, p in zip(out_dtypes, is_part)]
    out_specs = [part_spec if p else mn_spec for p in is_part]
    out_shapes = [jax.ShapeDtypeStruct((8 * (m // tm), n) if p else (m, n), dt) for dt, p in zip(out_dts, is_part)]

    n_side = 0 if exchange is None else 1
    grid = (m // tm, n // tn, nk)
    prior = []
    if into is not None:
        assert n_out == 1 and not extras and exchange is None
        into_shape, into_map, into_prior = into
        out_specs = [pl.BlockSpec((1, tm, tn), lambda i, j, kk: into_map(i, j))]
        out_shapes = [jax.ShapeDtypeStruct(into_shape, out_dts[0])]
        prior = [] if into_prior is None else [into_prior]

    def body(*refs):
        a_ref, b_ref = refs[0], refs[1]
        ex_refs = refs[2:2 + n_ex]
        n_in = 2 + n_ex + n_side + len(prior)
        out_refs = refs[n_in:n_in + n_out]
        scratch = refs[n_in + n_out + n_side:]
        acc_ref = scratch[0] if nk > 1 else None
        if exchange is not None:
            side = (refs[2 + n_ex], refs[2 + n_ex + n_side + n_out]) + tuple(scratch[-2:])
            step = (pl.program_id(0) * grid[1] + pl.program_id(1)) * grid[2] + pl.program_id(2)

            @pl.when(step == 0)
            def _():
                _exchange_start(*side)

            @pl.when(step == grid[0] * grid[1] * grid[2] - 1)
            def _():
                _exchange_wait(*side)

        part = dot(a_ref[...].astype(BF16), b_ref[...].astype(BF16))

        def finish(acc):
            if epilogue is None:
                outs = (acc,)
            else:
                outs = epilogue(acc, *[r[...] for r in ex_refs])
            for o_ref, o in zip(out_refs, outs):
                if into is None:
                    o_ref[...] = o.astype(o_ref.dtype)
                else:
                    o_ref[0] = o.astype(o_ref.dtype)

        if nk == 1:
            finish(part)
        else:
            kk = pl.program_id(2)

            @pl.when(kk == 0)
            def _():
                acc_ref[...] = part

            @pl.when(kk > 0)
            def _():
                acc_ref[...] += part

            @pl.when(kk == nk - 1)
            def _():
                finish(acc_ref[...])

    side_in = [] if exchange is None else [exchange]
    outs = pl.pallas_call(
        body,
        name=name,
        grid=grid,
        in_specs=[a_spec, b_spec] + ex_specs + [ANY] * (n_side + len(prior)),
        out_specs=out_specs + [ANY] * n_side,
        out_shape=out_shapes + [jax.ShapeDtypeStruct(e.shape, e.dtype) for e in side_in],
        scratch_shapes=([pltpu.VMEM((tm, tn), F32)] if nk > 1 else [])
        + [pltpu.SemaphoreType.DMA((3,)), pltpu.SemaphoreType.DMA((3,))] * n_side,
        input_output_aliases={2: 0} if prior else {},
        compiler_params=_cparams(("arbitrary",) * 3 if n_side else ("parallel", "parallel", "arbitrary")),
    )(a, b, *extras, *side_in, *prior)
    return outs


ROW_TILE = 512


def _rows_sum8(t):
    rows, d = t.shape
    return jnp.sum(t.reshape(rows // 8, 8, d), axis=0)


def _rms_rows(x):
    r = lax.rsqrt(jnp.mean(x * x, axis=-1, keepdims=True) + RMS_EPS)
    return x * r, r


def _rms_bwd_rows(dh, x, g):
    xh, r = _rms_rows(x)
    dxh = dh * g
    return r * (dxh - xh * jnp.mean(dxh * xh, axis=-1, keepdims=True)), _rows_sum8(dh * xh)


def _rms_bwd_residual(dh, x, g, dres):
    s, d = x.shape

    def body(dh_ref, x_ref, g_ref, dres_ref, dx_ref, dg_ref):
        dx, dg = _rms_bwd_rows(dh_ref[...], x_ref[...], g_ref[...])
        dx_ref[...] = dres_ref[...] + dx
        dg_ref[...] = dg

    row = pl.BlockSpec((ROW_TILE, d), lambda i: (i, 0))
    return pl.pallas_call(
        body,
        name="norm_mix_bwd",
        grid=(s // ROW_TILE,),
        in_specs=[row, row, pl.BlockSpec((1, d), lambda i: (0, 0)), row],
        out_specs=[row, pl.BlockSpec((8, d), lambda i: (i, 0))],
        out_shape=[jax.ShapeDtypeStruct((s, d), F32), jax.ShapeDtypeStruct((8 * (s // ROW_TILE), d), F32)],
        compiler_params=_cparams(("parallel",)),
    )(dh, x, g, dres)


def _rms_fwd_and_gather(x, g, shard_pack):
    s, d = x.shape
    r_pack, w_pack = shard_pack.shape
    steps = s // ROW_TILE

    def body(x_ref, g_ref, pack_ref, h_ref, others_ref, send_sems, recv_sems):
        i = pl.program_id(0)
        gather = (pack_ref, others_ref, send_sems, recv_sems)

        @pl.when(i == 0)
        def _():
            _gather_start(*gather)

        h_ref[...] = (_rms_rows(x_ref[...])[0] * g_ref[...]).astype(BF16)

        @pl.when(i == steps - 1)
        def _():
            _gather_pass_on(*gather)
            _gather_finish(*gather)

    h, others = pl.pallas_call(
        body,
        name="norm_mix",
        grid=(steps,),
        in_specs=[pl.BlockSpec((ROW_TILE, d), lambda i: (i, 0)), pl.BlockSpec((1, d), lambda i: (0, 0)), ANY],
        out_specs=[pl.BlockSpec((ROW_TILE, d), lambda i: (i, 0)), ANY],
        out_shape=[jax.ShapeDtypeStruct((s, d), BF16),
                   jax.ShapeDtypeStruct((N_CHIPS, 2, r_pack // 2, w_pack), shard_pack.dtype)],
        scratch_shapes=[pltpu.SemaphoreType.DMA((6,)), pltpu.SemaphoreType.DMA((6,))],
        compiler_params=_cparams(("arbitrary",)),
    )(x, g, shard_pack.reshape(2, r_pack // 2, w_pack))
    return h, _fill_own_slot(others, shard_pack)


def _alibi_slopes():
    return np.exp2(np.float32(-8.0) * np.arange(1, N_DIL_HEADS + 1, dtype=np.float32) / np.float32(N_DIL_HEADS))


def _head_lane_mask(h, rows):
    lane = lax.broadcasted_iota(jnp.int32, (rows, DIL_W), 1)
    return (lane >= h * HEAD_DIM) & (lane < (h + 1) * HEAD_DIM)


def _pair_lanes(h):
    return slice((h // 2) * 128, (h // 2 + 1) * 128)


def _only_head(t, h):
    part = t[:, _pair_lanes(h)]
    lane = lax.broadcasted_iota(jnp.int32, part.shape, 1)
    keep = (lane < HEAD_DIM) if h % 2 == 0 else (lane >= HEAD_DIM)
    return jnp.where(keep, part, jnp.zeros_like(part))


def _band_terms(dil, has_prev):
    qi = lax.broadcasted_iota(jnp.int32, (BLOCK, 2 * BLOCK), 0)
    kj = lax.broadcasted_iota(jnp.int32, (BLOCK, 2 * BLOCK), 1)
    steps = qi + BLOCK - kj
    valid = (steps >= 0) & (steps <= BLOCK) & ((kj >= BLOCK) | has_prev)
    return valid, steps.astype(F32) * float(dil)


def _load_halves(ref, rows):
    return jnp.concatenate([ref[0, rows, :], ref[1, rows, :]], axis=1)


def _store_halves(ref, rows, value):
    ref[0, rows, :] = value[:, :128]
    ref[1, rows, :] = value[:, 128:]


def _dil_fwd_all(qkv_groups):
    s = qkv_groups[0].shape[0]
    steps = s // BLOCK
    n_groups = len(DIL_GROUPS)
    dils = [d for _, d in DIL_GROUPS]
    views = [qkv_groups[g].reshape(s // dils[g], dils[g] * 3 * DIL_W) for g in range(n_groups)]
    slopes = _alibi_slopes()

    def spec(dil, which, prev):
        def index(t):
            n = t // dil
            return (jnp.maximum(n - 1, 0) if prev else n, (t % dil) * 3 + which)
        return pl.BlockSpec((BLOCK, DIL_W), index)

    def body(*refs):
        t = pl.program_id(0)
        ins, outs = refs[:5 * n_groups], refs[5 * n_groups:]
        masks = [_head_lane_mask(h, BLOCK) for h in range(DIL_HEADS)]
        work = []
        for g in range(n_groups):
            q_ref, kc_ref, kp_ref, vc_ref, vp_ref = ins[5 * g:5 * g + 5]
            q = q_ref[...]
            k2 = jnp.concatenate([kp_ref[...], kc_ref[...]], axis=0)
            v2 = jnp.concatenate([vp_ref[...], vc_ref[...]], axis=0)
            logits = [_dot_nt(_only_head(q, h), k2[:, _pair_lanes(h)]) for h in range(DIL_HEADS)]
            work.append((v2, logits))
        probs = []
        for g in range(n_groups):
            valid, dist = _band_terms(dils[g], t // dils[g] > 0)
            ps, lses = [], []
            for h in range(DIL_HEADS):
                slope = float(slopes[g * DIL_HEADS + h])
                lg = jnp.where(valid, work[g][1][h] * 0.125 - slope * dist, NEG_INF)
                mx = jnp.max(lg, axis=1, keepdims=True)
                lse = mx + jnp.log(jnp.sum(jnp.exp(lg - mx), axis=1, keepdims=True))
                ps.append(jnp.exp(lg - lse).astype(BF16))
                lses.append(lse)
            probs.append((ps, lses))
        for g in range(n_groups):
            dil = dils[g]
            mine = pl.ds(t % dil, BLOCK, stride=dil) if dil > 1 else slice(None)
            o_acc = jnp.zeros((BLOCK, DIL_W), F32)
            lse_acc = jnp.zeros((BLOCK, DIL_W), F32)
            for h in range(DIL_HEADS):
                o_acc = jnp.where(masks[h], _dot_nn(probs[g][0][h], work[g][0]), o_acc)
                lse_acc = jnp.where(masks[h], probs[g][1][h], lse_acc)
            _store_halves(outs[2 * g], mine, o_acc)
            _store_halves(outs[2 * g + 1], mine, lse_acc)

    in_specs, out_specs, operands = [], [], []
    for g, dil in enumerate(dils):
        in_specs += [spec(dil, 0, False), spec(dil, 1, False), spec(dil, 1, True), spec(dil, 2, False),
                     spec(dil, 2, True)]
        out_specs += [pl.BlockSpec((2, BLOCK * dil, 128), functools.partial(lambda d, t: (0, t // d, 0), dil))] * 2
        operands += [views[g]] * 5
    res = pl.pallas_call(
        body,
        name="dil_fwd",
        grid=(steps,),
        in_specs=in_specs,
        out_specs=out_specs,
        out_shape=[jax.ShapeDtypeStruct((2, s, 128), F32)] * (2 * n_groups),
        compiler_params=_cparams(("arbitrary",)),
    )(*operands)
    return [(res[2 * g], res[2 * g + 1]) for g in range(n_groups)]


def _dil_bwd_all(qkv_groups, dos, lses, cterms):
    s = qkv_groups[0].shape[0]
    steps = s // BLOCK
    n_groups = len(DIL_GROUPS)
    dils = [d for _, d in DIL_GROUPS]
    views = [qkv_groups[g].reshape(s // dils[g], dils[g] * 3 * DIL_W) for g in range(n_groups)]
    slopes = _alibi_slopes()
    n_in = 12

    def block_of(dil, shift):
        nb = steps // dil

        def index(t):
            n = t // dil
            if shift < 0:
                return jnp.maximum(n - 1, 0)
            return n if shift == 0 else jnp.minimum(n + 1, nb - 1)
        return index

    def col(dil, which, shift):
        index = block_of(dil, shift)
        return pl.BlockSpec((BLOCK, DIL_W), lambda t: (index(t), (t % dil) * 3 + which))

    def own(dil, shift):
        index = block_of(dil, shift)
        return pl.BlockSpec((2, BLOCK * dil, 128), lambda t: (0, index(t), 0))

    def head_col(v, hm):
        return jnp.max(jnp.where(hm, v, NEG_INF), axis=1, keepdims=True)

    def body(*refs):
        t = pl.program_id(0)
        masks = [_head_lane_mask(h, BLOCK) for h in range(DIL_HEADS)]
        work = []
        for g, dil in enumerate(dils):
            (q_ref, qn_ref, kc_ref, kp_ref, vc_ref, vp_ref, do_ref, don_ref, lse_ref, lsen_ref, c_ref,
             cn_ref) = refs[n_in * g:n_in * (g + 1)]
            mine = pl.ds(t % dil, BLOCK, stride=dil) if dil > 1 else slice(None)
            q, qn = q_ref[...], qn_ref[...]
            kc, vc = kc_ref[...], vc_ref[...]
            k2 = jnp.concatenate([kp_ref[...], kc], axis=0)
            v2 = jnp.concatenate([vp_ref[...], vc], axis=0)
            dov, donv = _load_halves(do_ref, mine), _load_halves(don_ref, mine)
            side = (_load_halves(lse_ref, mine), _load_halves(lsen_ref, mine),
                    _load_halves(c_ref, mine), _load_halves(cn_ref, mine))
            heads = range(DIL_HEADS)
            qhs = [_only_head(q, h) for h in heads]
            qnhs = [_only_head(qn, h) for h in heads]
            dohs = [_only_head(dov, h).astype(BF16) for h in heads]
            donhs = [_only_head(donv, h).astype(BF16) for h in heads]
            prods = ([_dot_nt(qhs[h], k2[:, _pair_lanes(h)]) for h in heads],
                     [_dot_nt(dohs[h], v2[:, _pair_lanes(h)]) for h in heads],
                     [_dot_nt(qnhs[h], kc[:, _pair_lanes(h)]) for h in heads],
                     [_dot_nt(donhs[h], vc[:, _pair_lanes(h)]) for h in heads])
            work.append((k2, qhs, qnhs, dohs, donhs, side, prods))
        grads = []
        for g, dil in enumerate(dils):
            n, nb = t // dil, steps // dil
            valid, dist = _band_terms(dil, n > 0)
            valid_n = _band_terms(dil, True)[0][:, :BLOCK] & (n < nb - 1)
            dist_n = dist[:, :BLOCK]
            (lsev, lsenv, cv, cnv), (logit, dp, logit_n, dp_n) = work[g][5], work[g][6]
            p16, dlog, pn16, dlog_n = [], [], [], []
            for h in range(DIL_HEADS):
                hm, slope = masks[h], float(slopes[g * DIL_HEADS + h])
                p = jnp.where(valid, jnp.exp(logit[h] * 0.125 - slope * dist - head_col(lsev, hm)), 0.0)
                dlog.append((p * (dp[h] + head_col(cv, hm)) * 0.125).astype(BF16))
                p16.append(p.astype(BF16))
                pn = jnp.where(valid_n, jnp.exp(logit_n[h] * 0.125 - slope * dist_n - head_col(lsenv, hm)), 0.0)
                dlog_n.append((pn * (dp_n[h] + head_col(cnv, hm)) * 0.125).astype(BF16))
                pn16.append(pn.astype(BF16))
            grads.append((p16, dlog, pn16, dlog_n))
        for g in range(n_groups):
            k2, qhs, qnhs, dohs, donhs = work[g][:5]
            p16, dlog, pn16, dlog_n = grads[g]
            dq_acc = jnp.zeros((BLOCK, DIL_W), F32)
            dk_pairs = [jnp.zeros((BLOCK, 128), F32)] * 2
            dv_pairs = [jnp.zeros((BLOCK, 128), F32)] * 2
            for h in range(DIL_HEADS):
                dq_acc = jnp.where(masks[h], _dot_nn(dlog[h], k2), dq_acc)
                dk_pairs[h // 2] = dk_pairs[h // 2] + _dot_tn(dlog[h][:, BLOCK:], qhs[h]) + _dot_tn(dlog_n[h], qnhs[h])
                dv_pairs[h // 2] = dv_pairs[h // 2] + _dot_tn(p16[h][:, BLOCK:], dohs[h]) + _dot_tn(pn16[h], donhs[h])
            dq_ref, dk_ref, dv_ref = refs[n_in * n_groups + 3 * g:n_in * n_groups + 3 * g + 3]
            dq_ref[...] = dq_acc.astype(BF16)
            dk_ref[...] = jnp.concatenate(dk_pairs, axis=1).astype(BF16)
            dv_ref[...] = jnp.concatenate(dv_pairs, axis=1).astype(BF16)

    in_specs, out_specs, out_shape, operands = [], [], [], []
    for g, dil in enumerate(dils):
        in_specs += [col(dil, 0, 0), col(dil, 0, 1), col(dil, 1, 0), col(dil, 1, -1), col(dil, 2, 0), col(dil, 2, -1),
                     own(dil, 0), own(dil, 1), own(dil, 0), own(dil, 1), own(dil, 0), own(dil, 1)]
        operands += [views[g]] * 6 + [dos[g], dos[g], lses[g], lses[g], cterms[g], cterms[g]]
        out_specs += [pl.BlockSpec((BLOCK, DIL_W), functools.partial(lambda d, t: (t // d, t % d), dil))] * 3
        out_shape += [jax.ShapeDtypeStruct((s // dil, dil * DIL_W), BF16)] * 3
    outs = pl.pallas_call(
        body,
        name="dil_bwd",
        grid=(steps,),
        in_specs=in_specs,
        out_specs=out_specs,
        out_shape=out_shape,
        compiler_params=_cparams(("parallel",)),
    )(*operands)
    return [tuple(v.reshape(s, DIL_W) for v in outs[3 * g:3 * g + 3]) for g in range(n_groups)]


SB_PAIRS = SB_HEADS // 2
SB_COL0 = 0
LOG2E = 1.4426950408889634


SB_EXP_CLAMP = 64.0


def _sb_softplus2(zs):
    t = 1.0 + jnp.exp2(jnp.minimum(zs, SB_EXP_CLAMP))
    return jnp.maximum(jnp.log(t) * LOG2E, zs)


def _sb_consts(nkb):
    row = lax.broadcasted_iota(jnp.int32, (SB_BQ, SB_BK), 0)
    colk = lax.broadcasted_iota(jnp.int32, (SB_BQ, SB_BK), 1)
    rr = lax.broadcasted_iota(jnp.int32, (SB_BK, SB_BK), 0)
    cc = lax.broadcasted_iota(jnp.int32, (SB_BK, SB_BK), 1)
    lane = lax.broadcasted_iota(jnp.int32, (SB_BQ, 128), 1)
    assert 2 * nkb <= 128
    return colk < row, rr, cc, lane < HEAD_DIM, lane


def _split_heads(t):
    first = lax.broadcasted_iota(jnp.int32, t.shape, 1) < HEAD_DIM
    zero = jnp.zeros_like(t)
    return jnp.where(first, t, zero), jnp.where(first, zero, t)


def _sb_fwd(qkv, shard_pack):
    s = qkv.shape[0]
    nq, nkb = s // SB_BQ, s // SB_BK
    zscale = LOG2E / math.sqrt(HEAD_DIM)
    r_pack, w_pack = shard_pack.shape

    def body(q_ref, k_ref, v_ref, pack_ref, o_ref, a_row, others_ref, zs_scr, a_scr, acc_scr, cl_scr,
             send_sems, recv_sems):
        i = pl.program_id(1)
        pair = pl.program_id(0)
        gather = (pack_ref, others_ref, send_sems, recv_sems)

        @pl.when((pair == 0) & (i == 0))
        def _():
            _gather_start(*gather)

        @pl.when((pair == 1) & (i == 0))
        def _():
            _gather_pass_on(*gather)

        @pl.when((pair == SB_PAIRS - 1) & (i == nq - 1))
        def _():
            _gather_finish(*gather)

        causal, rr, cc, _, _ = _sb_consts(nkb)
        later = (rr > cc).astype(BF16)
        qh = _split_heads(q_ref[...])

        def rows(j):
            return pl.ds(pl.multiple_of(j * SB_BK, SB_BK), SB_BK)

        def scores_to(slot, j):
            kb = k_ref[rows(j), :]
            for hh in range(2):
                zs_scr[slot, hh] = _dot_nt(qh[hh], kb) * zscale

        def weights(slot, j, masked):
            xs, sums, sufs = [], [], []
            for hh in range(2):
                zs = zs_scr[slot, hh]
                sp = _sb_softplus2(zs)
                if masked:
                    sp = jnp.where(causal, sp, 0.0)
                xs.append(zs - sp)
                sums.append(jnp.sum(sp, axis=1, keepdims=True))
                sufs.append(_dot_f32_by_01(sp, later, 2))
            for hh in range(2):
                cl = cl_scr[hh]
                a = jnp.exp2(xs[hh] - (sufs[hh] + jnp.concatenate([cl, cl], axis=1)))
                if masked:
                    a = jnp.where(causal, a, 0.0)
                a16 = a.astype(BF16)
                a_scr[slot, :, hh * SB_BK:(hh + 1) * SB_BK] = a16
                a_row[0, 0, j, :, hh * SB_BK:(hh + 1) * SB_BK] = a16
                cl_scr[hh] = cl + sums[hh]

        def add_av(slot, j):
            v0, v1 = _split_heads(v_ref[rows(j), :])
            acc_scr[...] += _dot_nn(a_scr[slot], jnp.concatenate([v0, v1], axis=0))

        acc_scr[...] = jnp.zeros_like(acc_scr)
        cl_scr[...] = jnp.zeros_like(cl_scr)
        scores_to(0, i)
        scores_to(1, jnp.maximum(i - 1, 0))
        weights(0, i, True)

        def step(j, prev, cur):
            scores_to(prev, jnp.maximum(j - 1, 0))
            add_av(prev, j + 1)
            weights(cur, j, False)

        def two_steps(u, _):
            j = i - 1 - 2 * u
            step(j, 0, 1)
            step(j - 1, 1, 0)
            return 0

        lax.fori_loop(0, i // 2, two_steps, 0)

        @pl.when(i % 2 == 1)
        def _():
            step(0, 0, 1)
            add_av(1, 0)

        @pl.when(i % 2 == 0)
        def _():
            add_av(0, 0)

        o_ref[...] = acc_scr[...]

    def full(which):
        return pl.BlockSpec((s, 128), lambda p, i: (0, SB_COL0 + 4 * which + p))

    return pl.pallas_call(
        body,
        name="sb_fwd",
        grid=(SB_PAIRS, nq),
        in_specs=[pl.BlockSpec((SB_BQ, 128), lambda p, i: (i, SB_COL0 + p)), full(1), full(2), ANY],
        out_specs=[pl.BlockSpec((SB_BQ, 128), lambda p, i: (i, p)),
                   pl.BlockSpec((1, 1, nkb, SB_BQ, 2 * SB_BK), lambda p, i: (p, i, 0, 0, 0)), ANY],
        out_shape=[jax.ShapeDtypeStruct((s, SB_W), F32),
                   jax.ShapeDtypeStruct((SB_PAIRS, nq, nkb, SB_BQ, 2 * SB_BK), BF16),
                   jax.ShapeDtypeStruct((N_CHIPS, 2, r_pack // 2, w_pack), shard_pack.dtype)],
        scratch_shapes=[pltpu.VMEM((2, 2, SB_BQ, SB_BK), F32), pltpu.VMEM((2, SB_BQ, 2 * SB_BK), BF16),
                        pltpu.VMEM((SB_BQ, 128), F32), pltpu.VMEM((2, SB_BQ, 128), F32),
                        pltpu.SemaphoreType.DMA((6,)), pltpu.SemaphoreType.DMA((6,))],
        compiler_params=_cparams(("arbitrary", "arbitrary")),
    )(qkv, qkv, qkv, shard_pack.reshape(2, r_pack // 2, w_pack))


def _sb_bwd(qkv, do, a_hbm, chip_sums):
    s = qkv.shape[0]
    nq, nkb = s // SB_BQ, s // SB_BK
    scale = 1.0 / math.sqrt(HEAD_DIM)
    zscale = LOG2E * scale

    def body(q_ref, k_ref, v_ref, do_ref, a_row, sums_ref, dq_ref, dk_ref, dv_ref, got_ref,
             zs_scr, da_scr, dz_scr, a_scr, cg_scr, send_sems, recv_sems):
        i = pl.program_id(1)
        pair = pl.program_id(0)
        first_step = (pair == 0) & (i == 0)
        last_step = (pair == SB_PAIRS - 1) & (i == nq - 1)

        @pl.when(first_step)
        def _():
            _exchange_start(sums_ref, got_ref, send_sems, recv_sems)

        @pl.when(i == 0)
        def _():
            dk_ref[...] = jnp.zeros_like(dk_ref)
            dv_ref[...] = jnp.zeros_like(dv_ref)

        causal, rr, cc, first, _ = _sb_consts(nkb)
        earlier = (rr < cc).astype(BF16)
        q2 = q_ref[...]
        qh = _split_heads(q2)
        do2 = do_ref[...].astype(BF16)
        doh = _split_heads(do2)

        def rows(j):
            return pl.ds(pl.multiple_of(j * SB_BK, SB_BK), SB_BK)

        def products_to(slot, j):
            kb, vb = k_ref[rows(j), :], v_ref[rows(j), :]
            for hh in range(2):
                zs_scr[slot, hh] = _dot_nt(qh[hh], kb) * (-zscale)
                da_scr[slot, hh] = _dot_nt(doh[hh], vb)

        head0_rows = lax.broadcasted_iota(jnp.int32, (128, SB_BK), 0) < HEAD_DIM

        def by_head(t):
            return jnp.where(head0_rows, t[:, :SB_BK], t[:, SB_BK:])

        def apply(slot, j):
            k0, k1 = _split_heads(k_ref[rows(j), :])
            dq_ref[...] += _dot_nn(dz_scr[slot], jnp.concatenate([k0, k1], axis=0)) * scale
            dk_ref[0, j] += by_head(_dot_tn(q2, dz_scr[slot])) * scale
            dv_ref[0, j] += by_head(_dot_tn(do2, a_scr[slot]))

        def grads(slot, j, masked):
            gs, gpres = [], []
            for hh in range(2):
                a16 = a_row[0, 0, j, :, hh * SB_BK:(hh + 1) * SB_BK]
                a_scr[slot, :, hh * SB_BK:(hh + 1) * SB_BK] = a16
                g = a16.astype(F32) * da_scr[slot, hh]
                gs.append(g)
                gpres.append(_dot_f32_by_01(g, earlier, 1))
            sigs = []
            for hh in range(2):
                e = jnp.exp2(jnp.minimum(zs_scr[slot, hh], SB_EXP_CLAMP))
                sigs.append(pl.reciprocal(1.0 + e, approx=True))
            for hh in range(2):
                cg = cg_scr[hh]
                dz = gs[hh] - (gs[hh] + (gpres[hh] + jnp.concatenate([cg, cg], axis=1))) * sigs[hh]
                if masked:
                    dz = jnp.where(causal, dz, 0.0)
                dz_scr[slot, :, hh * SB_BK:(hh + 1) * SB_BK] = dz.astype(BF16)
                cg_scr[hh] = cg + jnp.sum(gs[hh], axis=1, keepdims=True)

        dq_ref[...] = jnp.zeros_like(dq_ref)
        cg_scr[...] = jnp.zeros_like(cg_scr)
        dz_scr[1] = jnp.zeros((SB_BQ, 2 * SB_BK), BF16)
        a_scr[1] = jnp.zeros((SB_BQ, 2 * SB_BK), BF16)
        products_to(0, 0)

        def step(j, cur, nxt):
            products_to(nxt, j + 1)
            apply(nxt, jnp.maximum(j - 1, 0))
            grads(cur, j, False)

        def two_steps(u, _):
            step(2 * u, 0, 1)
            step(2 * u + 1, 1, 0)
            return 0

        lax.fori_loop(0, i // 2, two_steps, 0)

        def last(cur, nxt):
            apply(nxt, jnp.maximum(i - 1, 0))
            grads(cur, i, True)
            apply(cur, i)

        @pl.when(i % 2 == 1)
        def _():
            step(i - 1, 0, 1)
            last(1, 0)

        @pl.when(i % 2 == 0)
        def _():
            last(0, 1)

        @pl.when(last_step)
        def _():
            _exchange_wait(sums_ref, got_ref, send_sems, recv_sems)

    def full(which):
        return pl.BlockSpec((s, 128), lambda p, i: (0, SB_COL0 + 4 * which + p))

    qblk = pl.BlockSpec((SB_BQ, 128), lambda p, i: (i, p))
    acc = pl.BlockSpec((1, nkb, 128, SB_BK), lambda p, i: (p, 0, 0, 0))
    acc_shape = jax.ShapeDtypeStruct((SB_PAIRS, nkb, 128, SB_BK), F32)
    dq, dk_t, dv_t, got = pl.pallas_call(
        body,
        name="sb_bwd",
        grid=(SB_PAIRS, nq),
        in_specs=[pl.BlockSpec((SB_BQ, 128), lambda p, i: (i, SB_COL0 + p)), full(1), full(2), qblk,
                  pl.BlockSpec((1, 1, nkb, SB_BQ, 2 * SB_BK), lambda p, i: (p, i, 0, 0, 0)), ANY],
        out_specs=[qblk, acc, acc, ANY],
        out_shape=[jax.ShapeDtypeStruct((s, SB_W), F32), acc_shape, acc_shape,
                   jax.ShapeDtypeStruct(chip_sums.shape, chip_sums.dtype)],
        scratch_shapes=[pltpu.VMEM((2, 2, SB_BQ, SB_BK), F32), pltpu.VMEM((2, 2, SB_BQ, SB_BK), F32),
                        pltpu.VMEM((2, SB_BQ, 2 * SB_BK), BF16), pltpu.VMEM((2, SB_BQ, 2 * SB_BK), BF16),
                        pltpu.VMEM((2, SB_BQ, 128), F32),
                        pltpu.SemaphoreType.DMA((3,)), pltpu.SemaphoreType.DMA((3,))],
        compiler_params=_cparams(("arbitrary", "arbitrary")),
    )(qkv, qkv, qkv, do, a_hbm, chip_sums)

    def untranspose(t):
        return jnp.transpose(t, (1, 3, 0, 2)).reshape(s, SB_W)

    return dq, untranspose(dk_t), untranspose(dv_t), got


MERGE_TILE = 256


def _group_mix(lses):
    mx = jnp.maximum(jnp.maximum(lses[0], lses[1]), lses[2])
    es = [jnp.exp(t - mx) for t in lses]
    den = es[0] + es[1] + es[2]
    return [e / den for e in es]


def _merge_fwd(o_groups, lse_groups, o_sb, gl, b_gate, w_up_dil, w_up_sb):
    s = gl.shape[0]
    t = MERGE_TILE

    def body(o0, o1, o2, l0, l1, l2, ob_ref, gl_ref, bg_ref, wd_ref, ws_ref, merged_ref, oa_ref):
        rows = slice(None)
        w = _group_mix([_load_halves(l, rows) for l in (l0, l1, l2)])
        og = [_load_halves(o, rows) for o in (o0, o1, o2)]
        oa = (w[0] * og[0] + w[1] * og[1] + w[2] * og[2]).astype(BF16)
        ua = _dot_nn(oa, wd_ref[...])
        ub = _dot_nn(ob_ref[...].astype(BF16), ws_ref[...])
        gate = jax.nn.sigmoid(gl_ref[...] + bg_ref[...])
        merged_ref[...] = (gate[:, :D_MODEL] * ua + gate[:, D_MODEL:] * ub).astype(BF16)
        oa_ref[...] = oa

    dil = pl.BlockSpec((t, DIL_W), lambda i: (i, 0))
    halves = pl.BlockSpec((2, t, 128), lambda i: (0, i, 0))
    const = lambda shape: pl.BlockSpec(shape, lambda i: (0, 0))
    return pl.pallas_call(
        body,
        name="merge_fwd",
        grid=(s // t,),
        in_specs=[halves] * 6 + [pl.BlockSpec((t, SB_W), lambda i: (i, 0)), pl.BlockSpec((t, GATE_W), lambda i: (i, 0)),
                                 const((1, GATE_W)), const((DIL_W, D_MODEL)), const((SB_W, D_MODEL))],
        out_specs=[pl.BlockSpec((t, D_MODEL), lambda i: (i, 0)), dil],
        out_shape=[jax.ShapeDtypeStruct((s, D_MODEL), BF16), jax.ShapeDtypeStruct((s, DIL_W), BF16)],
        compiler_params=_cparams(("parallel",)),
    )(*o_groups, *lse_groups, o_sb, gl, b_gate, w_up_dil, w_up_sb)


def _merge_bwd(dmerged, o_groups, lse_groups, o_sb, gl, b_gate, w_up_dil, w_up_sb, swap):
    s = gl.shape[0]
    t = MERGE_TILE
    n_chunks, r_swap, w_swap = swap.shape
    swap = swap.reshape(n_chunks, 2, r_swap // 2, w_swap)

    def body(dm_ref, o0, o1, o2, l0, l1, l2, ob_ref, gl_ref, bg_ref, wd_ref, ws_ref, swap_ref,
             dua_ref, dub_ref, dgl_ref, dbg_ref, dosb_ref, d0, d1, d2, c0, c1, c2, got_ref, send_sem, recv_sem):
        i = pl.program_id(0)

        @pl.when(i == 0)
        def _():
            _swap_copy(swap_ref, got_ref, send_sem, recv_sem).start()

        @pl.when(i == pl.num_programs(0) - 1)
        def _():
            _swap_copy(swap_ref, got_ref, send_sem, recv_sem).wait()

        rows = slice(None)
        og = [_load_halves(o, rows) for o in (o0, o1, o2)]
        w = _group_mix([_load_halves(l, rows) for l in (l0, l1, l2)])
        oa = (w[0] * og[0] + w[1] * og[1] + w[2] * og[2]).astype(BF16)
        ua = _dot_nn(oa, wd_ref[...])
        ub = _dot_nn(ob_ref[...].astype(BF16), ws_ref[...])
        gate = jax.nn.sigmoid(gl_ref[...] + bg_ref[...])
        ga, gb = gate[:, :D_MODEL], gate[:, D_MODEL:]
        dm = dm_ref[...]
        dua = (dm * ga).astype(BF16)
        dub = (dm * gb).astype(BF16)
        dua_ref[...] = dua
        dub_ref[...] = dub
        dgl_a = dm * ua * ga * (1.0 - ga)
        dgl_b = dm * ub * gb * (1.0 - gb)
        dgl_ref[:, :D_MODEL] = dgl_a.astype(BF16)
        dgl_ref[:, D_MODEL:] = dgl_b.astype(BF16)
        part = jnp.concatenate([jnp.sum(dgl_a.reshape(t // 8, 8, D_MODEL), axis=0),
                                jnp.sum(dgl_b.reshape(t // 8, 8, D_MODEL), axis=0)], axis=1)

        @pl.when(i == 0)
        def _():
            dbg_ref[...] = part

        @pl.when(i > 0)
        def _():
            dbg_ref[...] += part

        dosb_ref[...] = _dot_nt(dub, ws_ref[...])
        doa = _dot_nt(dua, wd_ref[...])
        rr = lax.broadcasted_iota(jnp.int32, (DIL_W, DIL_W), 0) // HEAD_DIM
        cc = lax.broadcasted_iota(jnp.int32, (DIL_W, DIL_W), 1) // HEAD_DIM
        same_head = (rr == cc).astype(BF16)
        dw = [_dot_f32_by_01(doa * og[g], same_head, 2) for g in range(3)]
        mean_dw = w[0] * dw[0] + w[1] * dw[1] + w[2] * dw[2]
        for g, (d_ref, c_ref) in enumerate(((d0, c0), (d1, c1), (d2, c2))):
            _store_halves(d_ref, rows, w[g] * doa)
            _store_halves(c_ref, rows, -w[g] * mean_dw)

    dil = pl.BlockSpec((2, t, 128), lambda i: (0, i, 0))
    wide = pl.BlockSpec((t, D_MODEL), lambda i: (i, 0))
    gate2 = pl.BlockSpec((t, GATE_W), lambda i: (i, 0))
    sbw = pl.BlockSpec((t, SB_W), lambda i: (i, 0))
    const = lambda shape: pl.BlockSpec(shape, lambda i: (0, 0))
    return pl.pallas_call(
        body,
        name="merge_bwd",
        grid=(s // t,),
        in_specs=[wide] + [dil] * 6 + [sbw, gate2, const((1, GATE_W)), const((DIL_W, D_MODEL)), const((SB_W, D_MODEL)),
                                       ANY],
        out_specs=[wide, wide, gate2, const((8, GATE_W)), sbw] + [dil] * 6 + [ANY],
        out_shape=[jax.ShapeDtypeStruct((s, D_MODEL), BF16), jax.ShapeDtypeStruct((s, D_MODEL), BF16),
                   jax.ShapeDtypeStruct((s, GATE_W), BF16), jax.ShapeDtypeStruct((8, GATE_W), F32),
                   jax.ShapeDtypeStruct((s, SB_W), F32)] + [jax.ShapeDtypeStruct((2, s, 128), F32)] * 6
        + [jax.ShapeDtypeStruct((n_chunks, r_swap // 2, w_swap), swap.dtype)],
        scratch_shapes=[pltpu.SemaphoreType.DMA, pltpu.SemaphoreType.DMA],
        compiler_params=_cparams(("arbitrary",)),
    )(dmerged, *o_groups, *lse_groups, o_sb, gl, b_gate, w_up_dil, w_up_sb, swap)


ANY = pl.BlockSpec(memory_space=pl.ANY)


def _place():
    x, y, c = lax.axis_index("x"), lax.axis_index("y"), lax.axis_index("c")
    other_chips = [(1 - x, y), (x, 1 - y), (1 - x, 1 - y)]
    return x, y, c, other_chips


def _gather_copies(p_ref, out_ref, send_sems, recv_sems):
    x, y, c, chips = _place()
    me, sibling = 2 * x + y, (x, y, 1 - c)
    idx = [2 * chip[0] + chip[1] for chip in chips]

    def copy(k, chip_idx, core, to, src=None):
        return pltpu.make_async_remote_copy(
            src_ref=out_ref.at[chip_idx, core] if src is None else src, dst_ref=out_ref.at[chip_idx, core],
            send_sem=send_sems.at[k], recv_sem=recv_sems.at[k], device_id=to, device_id_type=MESH)

    first = lambda j: copy(j, me, c, (*chips[j], c), src=p_ref.at[c])
    landed = lambda j: copy(j, idx[j], c, (x, y, c))
    passed = lambda j: copy(3 + j, idx[j], c, sibling)
    handed = lambda j: copy(3 + j, idx[j], 1 - c, (x, y, c))
    return first, landed, passed, handed


def _gather_start(*refs):
    first = _gather_copies(*refs)[0]
    for j in range(3):
        first(j).start()


def _gather_pass_on(*refs):
    _, landed, passed, _ = _gather_copies(*refs)
    for j in range(3):
        landed(j).wait_recv()
        passed(j).start()


def _gather_finish(*refs):
    first, _, passed, handed = _gather_copies(*refs)
    for j in range(3):
        handed(j).wait_recv()
    for j in range(3):
        first(j).wait_send()
        passed(j).wait_send()


def _fill_own_slot(others, pack):
    n, _, rh, wd = others.shape
    me = 2 * lax.axis_index("x") + lax.axis_index("y")
    mine = lax.broadcasted_iota(jnp.int32, (n, 1, 1, 1), 0) == me
    return jnp.where(mine, pack.reshape(1, 2, rh, wd), others).reshape(n, 2 * rh, wd)


def _swap_copy(g_ref, out_ref, send_sem, recv_sem):
    x, y, c, _ = _place()
    return pltpu.make_async_remote_copy(
        src_ref=g_ref.at[:, 1 - c], dst_ref=out_ref,
        send_sem=send_sem, recv_sem=recv_sem, device_id=(x, y, 1 - c), device_id_type=MESH)


def _swap_halves(g):
    n, r, wd = g.shape
    rh = r // 2
    g = g.reshape(n, 2, rh, wd)

    def body(g_ref, out_ref, send_sem, recv_sem):
        cp = _swap_copy(g_ref, out_ref, send_sem, recv_sem)
        cp.start()
        cp.wait()

    return pl.pallas_call(
        body,
        name="grad_swap_halves",
        in_specs=[ANY],
        out_specs=ANY,
        out_shape=jax.ShapeDtypeStruct((n, rh, wd), g.dtype),
        scratch_shapes=[pltpu.SemaphoreType.DMA, pltpu.SemaphoreType.DMA],
    )(g)


def _add_halves(g, got, core):
    n, r, wd = g.shape
    rh = r // 2
    t = rh // 4
    nt = rh // t

    def body(c_ref, a_ref, b_ref, o_ref):
        o_ref[...] = (a_ref[0] + b_ref[...]).astype(BF16)

    grid_spec = pltpu.PrefetchScalarGridSpec(
        num_scalar_prefetch=1,
        grid=(n, nt),
        in_specs=[pl.BlockSpec((1, 1, t, wd), lambda s, i, c: (s, c[0], i, 0)),
                  pl.BlockSpec((1, t, wd), lambda s, i, c: (s, i, 0))],
        out_specs=pl.BlockSpec((1, t, wd), lambda s, i, c: (s, i, 0)),
    )
    return pl.pallas_call(
        body,
        name="grad_add_halves",
        grid_spec=grid_spec,
        out_shape=jax.ShapeDtypeStruct((n, rh, wd), BF16),
        compiler_params=_cparams(("parallel", "parallel")),
    )(core, g.reshape(n, 2, rh, wd), got)


def _exchange_copies(h_ref, out_ref, send_sems, recv_sems):
    x, y, c, chips = _place()
    me = 2 * x + y

    def copy(j, slot):
        them = 2 * chips[j][0] + chips[j][1]
        return pltpu.make_async_remote_copy(
            src_ref=h_ref.at[them], dst_ref=out_ref.at[me if slot == "mine" else them],
            send_sem=send_sems.at[j], recv_sem=recv_sems.at[j], device_id=(*chips[j], c), device_id_type=MESH)

    return (lambda j: copy(j, "mine")), (lambda j: copy(j, "theirs"))


def _exchange_start(h_ref, out_ref, send_sems, recv_sems):
    send = _exchange_copies(h_ref, out_ref, send_sems, recv_sems)[0]
    for j in range(3):
        send(j).start()


def _exchange_wait(h_ref, out_ref, send_sems, recv_sems):
    send, arrival = _exchange_copies(h_ref, out_ref, send_sems, recv_sems)
    for j in range(3):
        arrival(j).wait_recv()
    for j in range(3):
        send(j).wait_send()


def _sum_chips(b, h, chip):
    n, rh, wd = b.shape
    t = rh // 4

    def body(chip_ref, b_ref, own_ref, o_ref):
        own = own_ref[0]
        s0, s1, s2, s3 = (jnp.where(chip_ref[0] == k, own, b_ref[k]).astype(F32) for k in range(n))
        o_ref[...] = ((s0 + s1) + s2) + s3

    grid_spec = pltpu.PrefetchScalarGridSpec(
        num_scalar_prefetch=1,
        grid=(rh // t,),
        in_specs=[pl.BlockSpec((n, t, wd), lambda i, chip: (0, i, 0)),
                  pl.BlockSpec((1, t, wd), lambda i, chip: (chip[0], i, 0))],
        out_specs=pl.BlockSpec((t, wd), lambda i, chip: (i, 0)),
    )
    return pl.pallas_call(
        body,
        name="grad_sum_chips",
        grid_spec=grid_spec,
        out_shape=jax.ShapeDtypeStruct((rh, wd), F32),
        compiler_params=_cparams(("parallel",)),
    )(chip, b, h)


def _join_halves(tc):
    rh, wd = tc.shape

    def body(t_ref, out_ref, send_sem, recv_sem):
        x, y, c, _ = _place()
        cp = pltpu.make_async_remote_copy(
            src_ref=t_ref, dst_ref=out_ref.at[c],
            send_sem=send_sem, recv_sem=recv_sem, device_id=(x, y, 1 - c), device_id_type=MESH)
        cp.start()
        cp.wait()

    halves = pl.pallas_call(
        body,
        name="grad_join_halves",
        in_specs=[ANY],
        out_specs=ANY,
        out_shape=jax.ShapeDtypeStruct((2, rh, wd), tc.dtype),
        scratch_shapes=[pltpu.SemaphoreType.DMA, pltpu.SemaphoreType.DMA],
    )(tc)
    return lax.dynamic_update_slice(halves, tc[None], (lax.axis_index("c"), 0, 0)).reshape(2 * rh, wd)


def _all_reduce_small(pack):
    rows, lanes = pack.shape

    def body(p_ref, out_ref, buf, send_sems, recv_sems):
        x, y, c, _ = _place()
        me = 4 * x + 2 * y + c
        buf[me] = p_ref[...]
        sends = []
        for k in range(1, N_DEV):
            peer = (x ^ (k >> 2), y ^ ((k >> 1) & 1), c ^ (k & 1))
            sends.append(pltpu.make_async_remote_copy(
                src_ref=p_ref, dst_ref=buf.at[me], send_sem=send_sems.at[k - 1], recv_sem=recv_sems.at[k - 1],
                device_id=peer, device_id_type=MESH))
        for cp in sends:
            cp.start()
        for k in range(1, N_DEV):
            pltpu.make_async_remote_copy(
                src_ref=p_ref, dst_ref=buf.at[me ^ k], send_sem=send_sems.at[k - 1], recv_sem=recv_sems.at[k - 1],
                device_id=(x, y, c), device_id_type=MESH).wait_recv()
        for cp in sends:
            cp.wait_send()
        total = buf[0]
        for d in range(1, N_DEV):
            total = total + buf[d]
        out_ref[...] = total

    vm = pl.BlockSpec(memory_space=pltpu.VMEM)
    return pl.pallas_call(
        body,
        name="all_reduce_small",
        in_specs=[vm],
        out_specs=vm,
        out_shape=jax.ShapeDtypeStruct((rows, lanes), F32),
        scratch_shapes=[pltpu.VMEM((N_DEV, rows, lanes), F32), pltpu.SemaphoreType.DMA((N_DEV - 1,)),
                        pltpu.SemaphoreType.DMA((N_DEV - 1,))],
    )(pack)


def _adamw(g, w, m, v, name):
    rows, cols = g.shape
    t = rows
    for cand in (256, 128, 64, 32, 16, 8):
        if rows % cand == 0:
            t = cand
            break

    def body(g_ref, w_ref, m_ref, v_ref, d_ref, nm_ref, nv_ref):
        gv = g_ref[...]
        mv = ADAM_B1 * m_ref[...] + (1.0 - ADAM_B1) * gv
        vv = ADAM_B2 * v_ref[...] + (1.0 - ADAM_B2) * (gv * gv)
        m_hat = mv / (1.0 - ADAM_B1 ** ADAM_STEP)
        v_hat = vv / (1.0 - ADAM_B2 ** ADAM_STEP)
        d_ref[...] = -ADAM_LR * (m_hat / (jnp.sqrt(v_hat) + ADAM_EPS) + ADAM_WD * w_ref[...])
        nm_ref[...] = mv
        nv_ref[...] = vv

    blk = pl.BlockSpec((t, cols), lambda i: (i, 0))
    return pl.pallas_call(
        body,
        name=name,
        grid=(rows // t,),
        in_specs=[blk] * 4,
        out_specs=[blk] * 3,
        out_shape=[jax.ShapeDtypeStruct((rows, cols), F32)] * 3,
        compiler_params=_cparams(("parallel",)),
    )(g, w, m, v)


PACK_W = 1024
BIG = (("w_in", (D_MODEL, IN_COLS), 1), ("w_up_dil", (DIL_W, D_MODEL), 1), ("w_up_sb", (SB_W, D_MODEL), 1),
       ("w_out", (D_MODEL, D_MODEL), 0), ("w_mlp_in", (D_MODEL, D_FF), 1), ("w_mlp_out", (D_FF, D_MODEL), 0))


def _shard_shape(shape, axis):
    return tuple(d // N_CHIPS if a == axis else d for a, d in enumerate(shape))


MIXER_GROUP, MLP_GROUP = BIG[:4], BIG[4:]
LATE_WEIGHTS = BIG[1:]


def _pack_rows(group=BIG):
    rows, at = {}, 0
    for name, shape, axis in group:
        n = math.prod(_shard_shape(shape, axis)) // PACK_W
        rows[name] = (at, n)
        at += n
    return rows, at


def _pack_shards(shards, group):
    return jnp.concatenate([shards[name].reshape(-1, PACK_W) for name, _, _ in group], axis=0)


def _unpack_full(gathered, group):
    rows, _ = _pack_rows(group)
    full = {}
    for name, shape, axis in group:
        at, n = rows[name]
        parts = gathered[:, at:at + n, :].reshape((N_CHIPS,) + _shard_shape(shape, axis))
        if axis == 0:
            full[name] = parts.reshape(shape)
        else:
            full[name] = jnp.transpose(parts, (1, 0, 2)).reshape(shape)
    return full


def _pack_full_grads(grads, group):
    chunks = []
    for name, shape, axis in group:
        g = grads[name]
        if axis == 0:
            parts = g.reshape((N_CHIPS, shape[0] // N_CHIPS, shape[1]))
        else:
            parts = jnp.transpose(g.reshape((shape[0], N_CHIPS, shape[1] // N_CHIPS)), (1, 0, 2))
        chunks.append(parts.reshape(N_CHIPS, -1, PACK_W))
    return jnp.concatenate(chunks, axis=1)


def _unpack_shard(packed, group):
    rows, _ = _pack_rows(group)
    return {name: packed[rows[name][0]:rows[name][0] + rows[name][1]].reshape(_shard_shape(shape, axis))
            for name, shape, axis in group}


def _local_step(x, target, early_shards, late_shards, norm_mix_g, b_gate, norm_mlp_g, norm_final_g, core):
    h, early = _rms_fwd_and_gather(x, norm_mix_g, early_shards)
    w = {"w_in": jnp.transpose(early, (1, 0, 2)).reshape(D_MODEL, IN_COLS)}
    w_in = w["w_in"]
    sb0 = 9 * DIL_W
    w_sb, w_gate = w_in[:, sb0:QKV_W], w_in[:, QKV_W:]
    w_dil = [jnp.concatenate([w_in[:, (3 * i + g) * DIL_W:(3 * i + g + 1) * DIL_W] for i in range(3)], axis=1)
             for g in range(3)]

    qkv_dil = [_matmul(h, w_dil[g], mode="nn", out_dtypes=(BF16,), name=f"proj_dil_g{g}", tn=768)[0] for g in range(3)]
    (qkv_sb,) = _matmul(h, w_sb, mode="nn", out_dtypes=(BF16,), name="proj_sb", tn=768)
    (gl,) = _matmul(h, w_gate, mode="nn", out_dtypes=(F32,), name="proj_gate")
    dil = _dil_fwd_all(qkv_dil)
    o_groups, lse_groups = [d[0] for d in dil], [d[1] for d in dil]
    o_sb, a_sb, late_others = _sb_fwd(qkv_sb, late_shards)
    w = {**w, **_unpack_full(_fill_own_slot(late_others, late_shards), LATE_WEIGHTS)}
    merged, o_a = _merge_fwd(o_groups, lse_groups, o_sb, gl, b_gate, w["w_up_dil"], w["w_up_sb"])
    def residual_and_norm(acc, res, g):
        x1 = res + acc
        return x1, _rms_rows(x1)[0] * g

    x1, h2 = _matmul(merged, w["w_out"], mode="nn", out_dtypes=(F32, BF16), name="out_proj", tm=ROW_TILE,
                     extras=(x, norm_mlp_g), epilogue=residual_and_norm)
    u, act = _matmul(h2, w["w_mlp_in"], mode="nn", out_dtypes=(BF16, BF16), name="mlp_in",
                     epilogue=lambda acc: (acc, jnp.square(jnp.maximum(acc, 0.0))))

    def residual_and_loss(acc, res, tgt, g):
        xh, r = _rms_rows(res + acc)
        err = xh * g - tgt
        dy = err * (1.0 / D_MODEL)
        dxh = dy * g
        dx2 = r * (dxh - xh * jnp.mean(dxh * xh, axis=-1, keepdims=True))
        return dx2, _rows_sum8(dy * xh), (0.5 / D_MODEL) * _rows_sum8(err * err)

    dx2, dg_final, loss_part = _matmul(
        act, w["w_mlp_out"], mode="nn", out_dtypes=(F32, ("part", F32), ("part", F32)), name="mlp_out", tm=ROW_TILE,
        tk=2048, extras=(x1, target, norm_final_g.reshape(1, D_MODEL)), epilogue=residual_and_loss)

    (du,) = _matmul(dx2, w["w_mlp_out"], mode="nt", out_dtypes=(BF16,), name="mlp_out_dx",
                    extras=(u,), epilogue=lambda acc, uu: (acc * (2.0 * jnp.maximum(uu.astype(F32), 0.0)),))
    pack_shape = (N_CHIPS, 2 * D_MODEL, D_MODEL)
    (half_pack,) = _matmul(act, dx2, mode="tn", out_dtypes=(F32,), name="mlp_out_dw",
                           into=(pack_shape, lambda i, j: (i, 1, 0), None))
    (mlp_pack,) = _matmul(h2, du, mode="tn", out_dtypes=(F32,), name="mlp_in_dw",
                          into=(pack_shape, lambda i, j: (j, 0, 0), half_pack))

    def norm_bwd(acc, xx, dres, g):
        dx, dg = _rms_bwd_rows(acc, xx, g)
        return dres + dx, dg

    dx1, dg_mlp = _matmul(du, w["w_mlp_in"], mode="nt", out_dtypes=(F32, ("part", F32)), name="mlp_in_dx",
                          tm=ROW_TILE, tk=2048, extras=(x1, dx2, norm_mlp_g), epilogue=norm_bwd)

    (dmerged,) = _matmul(dx1, w["w_out"], mode="nt", out_dtypes=(F32,), name="out_proj_dx")
    (g_out,) = _matmul(merged, dx1, mode="tn", out_dtypes=(F32,), name="out_proj_dw")
    mb = _merge_bwd(dmerged, o_groups, lse_groups, o_sb, gl, b_gate, w["w_up_dil"], w["w_up_sb"], mlp_pack)
    dua, dub, dgl, dbg, do_sb = mb[:5]
    do_groups, c_groups = mb[5:8], mb[8:11]
    mlp_sums = _add_halves(mlp_pack, mb[11], core)
    (g_up_dil,) = _matmul(o_a, dua, mode="tn", out_dtypes=(F32,), name="up_dil_dw")
    (g_up_sb,) = _matmul(o_sb, dub, mode="tn", out_dtypes=(F32,), name="up_sb_dw")
    dq_sb, dk_sb, dv_sb, mlp_got = _sb_bwd(qkv_sb, do_sb, a_sb, mlp_sums)
    dil_b = _dil_bwd_all(qkv_dil, do_groups, lse_groups, c_groups)
    dproj = jnp.concatenate(
        [dil_b[g][i].astype(BF16) for i in range(3) for g in range(3)]
        + [t.astype(BF16) for t in (dq_sb, dk_sb, dv_sb)] + [dgl], axis=1)
    (g_in,) = _matmul(h, dproj, mode="tn", out_dtypes=(F32,), name="proj_dw", tm=512, tn=IN_COLS // 2)
    mixer_pack = _pack_full_grads({"w_in": g_in, "w_up_dil": g_up_dil, "w_up_sb": g_up_sb, "w_out": g_out}, MIXER_GROUP)
    mixer_sums = _add_halves(mixer_pack, _swap_halves(mixer_pack), core)
    dh, mixer_got = _matmul(dproj, w["w_in"], mode="nt", out_dtypes=(F32,), name="proj_dx", tk=IN_COLS // 2,
                            exchange=mixer_sums)
    grad_x, dg_mix = _rms_bwd_residual(dh, x, norm_mix_g, dx1)

    small = (dg_mix, dbg, dg_mlp, dg_final, loss_part)
    return grad_x, (mixer_got, mixer_sums), (mlp_got, mlp_sums), small


def kernel(x, norm_mix_g, w_in, b_gate, w_up_dil, w_up_sb, w_out, norm_mlp_g, w_mlp_in, w_mlp_out, norm_final_g, loss_target, m_norm_mix_g, m_w_in, m_b_gate, m_w_up_dil, m_w_up_sb, m_w_out, m_norm_mlp_g, m_w_mlp_in, m_w_mlp_out, m_norm_final_g, v_norm_mix_g, v_w_in, v_b_gate, v_w_up_dil, v_w_up_sb, v_w_out, v_norm_mlp_g, v_w_mlp_in, v_w_mlp_out, v_norm_final_g):
    shards = {"w_in": w_in[0], "w_up_dil": w_up_dil[0], "w_up_sb": w_up_sb[0], "w_out": w_out[0],
              "w_mlp_in": w_mlp_in[0], "w_mlp_out": w_mlp_out[0]}
    moments_m = {"w_in": m_w_in[0], "w_up_dil": m_w_up_dil[0], "w_up_sb": m_w_up_sb[0], "w_out": m_w_out[0],
                 "w_mlp_in": m_w_mlp_in[0], "w_mlp_out": m_w_mlp_out[0]}
    moments_v = {"w_in": v_w_in[0], "w_up_dil": v_w_up_dil[0], "w_up_sb": v_w_up_sb[0], "w_out": v_w_out[0],
                 "w_mlp_in": v_w_mlp_in[0], "w_mlp_out": v_w_mlp_out[0]}

    shards16 = {n: s.astype(BF16) for n, s in shards.items()}
    early_shards = shards16["w_in"]
    late_shards = _pack_shards(shards16, LATE_WEIGHTS)

    core = lax.axis_index("c").astype(jnp.int32).reshape(1)
    chip = (2 * lax.axis_index("x") + lax.axis_index("y")).astype(jnp.int32).reshape(1)
    grad_x, (mixer_got, mixer_sums), (mlp_got, mlp_sums), small = _local_step(
        x[0], loss_target[0], early_shards, late_shards, norm_mix_g, b_gate, norm_mlp_g, norm_final_g, core)

    reduced = _join_halves(_sum_chips(mixer_got, mixer_sums, chip))
    reduced_mlp = _join_halves(_sum_chips(mlp_got, mlp_sums, chip))
    g_shard = {**_unpack_shard(reduced, MIXER_GROUP), **_unpack_shard(reduced_mlp, MLP_GROUP)}

    dg_mix, dbg, dg_mlp, dg_final, loss_part = small
    loss_row = jnp.sum(loss_part, axis=0, keepdims=True)
    small_pack = jnp.concatenate(
        [jnp.sum(dg_mix, axis=0, keepdims=True), jnp.sum(dbg, axis=0, keepdims=True),
         jnp.sum(dg_mlp, axis=0, keepdims=True), jnp.sum(dg_final, axis=0, keepdims=True), loss_row], axis=1)
    n_small = small_pack.shape[1]
    small_sum = _all_reduce_small(small_pack.reshape(n_small // 128, 128)).reshape(1, n_small)
    g_norm_mix = small_sum[:, :D_MODEL]
    g_b_gate = small_sum[:, D_MODEL:3 * D_MODEL]
    g_norm_mlp = small_sum[:, 3 * D_MODEL:4 * D_MODEL]
    g_norm_final = small_sum[:, 4 * D_MODEL:5 * D_MODEL]
    loss = jnp.sum(small_sum[:, 5 * D_MODEL:])

    names = ["norm_mix_g", "w_in", "b_gate", "w_up_dil", "w_up_sb", "w_out", "norm_mlp_g", "w_mlp_in", "w_mlp_out",
             "norm_final_g"]
    grads = dict(g_shard)
    grads.update(norm_mix_g=g_norm_mix, b_gate=g_b_gate, norm_mlp_g=g_norm_mlp, norm_final_g=g_norm_final)
    weights = dict(shards)
    weights.update(norm_mix_g=norm_mix_g, b_gate=b_gate, norm_mlp_g=norm_mlp_g, norm_final_g=norm_final_g.reshape(1, D_MODEL))
    ms = dict(moments_m)
    ms.update(norm_mix_g=m_norm_mix_g, b_gate=m_b_gate, norm_mlp_g=m_norm_mlp_g, norm_final_g=m_norm_final_g.reshape(1, D_MODEL))
    vs = dict(moments_v)
    vs.update(norm_mix_g=v_norm_mix_g, b_gate=v_b_gate, norm_mlp_g=v_norm_mlp_g, norm_final_g=v_norm_final_g.reshape(1, D_MODEL))

    out_shapes = {"norm_mix_g": norm_mix_g.shape, "w_in": w_in.shape, "b_gate": b_gate.shape, "w_up_dil": w_up_dil.shape,
                  "w_up_sb": w_up_sb.shape, "w_out": w_out.shape, "norm_mlp_g": norm_mlp_g.shape,
                  "w_mlp_in": w_mlp_in.shape, "w_mlp_out": w_mlp_out.shape, "norm_final_g": norm_final_g.shape}
    g_out, d_out, m_out, v_out = [], [], [], []
    for n in names:
        d, nm, nv = _adamw(grads[n], weights[n], ms[n], vs[n], "adamw_" + n)
        shape = out_shapes[n]
        g_out.append(grads[n].reshape(shape))
        d_out.append(d.reshape(shape))
        m_out.append(nm.reshape(shape))
        v_out.append(nv.reshape(shape))
    return (loss, grad_x.reshape(x.shape), *g_out, *d_out, *m_out, *v_out)
```

```python
import functools
import math

import jax
import jax.numpy as jnp
import numpy as np
from jax import lax
from jax.experimental import pallas as pl
from jax.experimental.pallas import tpu as pltpu

F32 = jnp.float32
BF16 = jnp.bfloat16
MESH = pl.DeviceIdType.MESH

D_MODEL = 1024
HEAD_DIM = 64
DIL_GROUPS = ((128, 1), (512, 4), (2048, 16))
DIL_HEADS = 4
DIL_W = 256
N_DIL_HEADS = 12
SB_HEADS = 8
SB_W = SB_HEADS * HEAD_DIM
QKV_W = 3 * 3 * DIL_W + 3 * SB_W
GATE_W = 2 * D_MODEL
IN_COLS = QKV_W + GATE_W
D_FF = 4 * D_MODEL
BLOCK = 128
RMS_EPS = 1e-6
NEG_INF = -1e30
N_CHIPS = 4
N_DEV = 8

ADAM_LR = 0.001
ADAM_B1 = 0.9
ADAM_B2 = 0.999
ADAM_EPS = 1e-08
ADAM_WD = 0.01
ADAM_STEP = 10

VMEM_LIMIT = 56 * 1024 * 1024

SB_BQ = 256
SB_BK = 256


def _cparams(sem=None):
    if sem is None:
        return pltpu.CompilerParams(vmem_limit_bytes=VMEM_LIMIT)
    return pltpu.CompilerParams(dimension_semantics=sem, vmem_limit_bytes=VMEM_LIMIT)


def _dot(a, b, dims):
    return lax.dot_general(a, b, (dims, ((), ())), preferred_element_type=F32)


def _dot_nn(a, b):
    return _dot(a, b, ((1,), (0,)))


def _dot_nt(a, b):
    return _dot(a, b, ((1,), (1,)))


def _dot_tn(a, b):
    return _dot(a, b, ((0,), (0,)))


def _dot_f32_by_01(x, m01, pieces=3):
    hi = x.astype(BF16)
    if pieces == 1:
        return _dot_nn(hi, m01)
    r1 = x - hi.astype(F32)
    mid = r1.astype(BF16)
    if pieces == 2:
        return _dot_nn(hi, m01) + _dot_nn(mid, m01)
    lo = (r1 - mid.astype(F32)).astype(BF16)
    return _dot_nn(hi, m01) + _dot_nn(mid, m01) + _dot_nn(lo, m01)


def _matmul(a, b, *, mode, out_dtypes, name, tm=1024, tn=1024, tk=1024, extras=(), epilogue=None, exchange=None,
            into=None):
    if mode == "nn":
        (m, k), (k2, n) = a.shape, b.shape
    elif mode == "nt":
        (m, k), (n, k2) = a.shape, b.shape
    else:
        (k, m), (k2, n) = a.shape, b.shape
    assert k == k2, (a.shape, b.shape, mode)
    tm, tn, tk = min(tm, m), min(tn, n), min(tk, k)
    assert m % tm == 0 and n % tn == 0 and k % tk == 0, (m, n, k, tm, tn, tk)
    nk = k // tk
    n_out = len(out_dtypes)
    n_ex = len(extras)

    if mode == "nn":
        a_spec = pl.BlockSpec((tm, tk), lambda i, j, kk: (i, kk))
        b_spec = pl.BlockSpec((tk, tn), lambda i, j, kk: (kk, j))
        dot = _dot_nn
    elif mode == "nt":
        a_spec = pl.BlockSpec((tm, tk), lambda i, j, kk: (i, kk))
        b_spec = pl.BlockSpec((tn, tk), lambda i, j, kk: (j, kk))
        dot = _dot_nt
    else:
        a_spec = pl.BlockSpec((tk, tm), lambda i, j, kk: (kk, i))
        b_spec = pl.BlockSpec((tk, tn), lambda i, j, kk: (kk, j))
        dot = _dot_tn
    mn_spec = pl.BlockSpec((tm, tn), lambda i, j, kk: (i, j))
    row_spec = pl.BlockSpec((1, tn), lambda i, j, kk: (0, j))
    part_spec = pl.BlockSpec((8, tn), lambda i, j, kk: (i, j))
    ex_specs = [row_spec if e.shape[0] == 1 else mn_spec for e in extras]
    is_part = [isinstance(dt, tuple) for dt in out_dtypes]
    out_dts = [dt[1] if p else dt for dt, p in zip(out_dtypes, is_part)]
    out_specs = [part_spec if p else mn_spec for p in is_part]
    out_shapes = [jax.ShapeDtypeStruct((8 * (m // tm), n) if p else (m, n), dt) for dt, p in zip(out_dts, is_part)]

    n_side = 0 if exchange is None else 1
    grid = (m // tm, n // tn, nk)
    prior = []
    if into is not None:
        assert n_out == 1 and not extras and exchange is None
        into_shape, into_map, into_prior = into
        out_specs = [pl.BlockSpec((1, tm, tn), lambda i, j, kk: into_map(i, j))]
        out_shapes = [jax.ShapeDtypeStruct(into_shape, out_dts[0])]
        prior = [] if into_prior is None else [into_prior]

    def body(*refs):
        a_ref, b_ref = refs[0], refs[1]
        ex_refs = refs[2:2 + n_ex]
        n_in = 2 + n_ex + n_side + len(prior)
        out_refs = refs[n_in:n_in + n_out]
        scratch = refs[n_in + n_out + n_side:]
        acc_ref = scratch[0] if nk > 1 else None
        if exchange is not None:
            side = (refs[2 + n_ex], refs[2 + n_ex + n_side + n_out]) + tuple(scratch[-2:])
            step = (pl.program_id(0) * grid[1] + pl.program_id(1)) * grid[2] + pl.program_id(2)

            @pl.when(step == 0)
            def _():
                _exchange_start(*side)

            @pl.when(step == grid[0] * grid[1] * grid[2] - 1)
            def _():
                _exchange_wait(*side)

        part = dot(a_ref[...].astype(BF16), b_ref[...].astype(BF16))

        def finish(acc):
            if epilogue is None:
                outs = (acc,)
            else:
                outs = epilogue(acc, *[r[...] for r in ex_refs])
            for o_ref, o in zip(out_refs, outs):
                if into is None:
                    o_ref[...] = o.astype(o_ref.dtype)
                else:
                    o_ref[0] = o.astype(o_ref.dtype)

        if nk == 1:
            finish(part)
        else:
            kk = pl.program_id(2)

            @pl.when(kk == 0)
            def _():
                acc_ref[...] = part

            @pl.when(kk > 0)
            def _():
                acc_ref[...] += part

            @pl.when(kk == nk - 1)
            def _():
                finish(acc_ref[...])

    side_in = [] if exchange is None else [exchange]
    outs = pl.pallas_call(
        body,
        name=name,
        grid=grid,
        in_specs=[a_spec, b_spec] + ex_specs + [ANY] * (n_side + len(prior)),
        out_specs=out_specs + [ANY] * n_side,
        out_shape=out_shapes + [jax.ShapeDtypeStruct(e.shape, e.dtype) for e in side_in],
        scratch_shapes=([pltpu.VMEM((tm, tn), F32)] if nk > 1 else [])
        + [pltpu.SemaphoreType.DMA((3,)), pltpu.SemaphoreType.DMA((3,))] * n_side,
        input_output_aliases={2: 0} if prior else {},
        compiler_params=_cparams(("arbitrary",) * 3 if n_side else ("parallel", "parallel", "arbitrary")),
    )(a, b, *extras, *side_in, *prior)
    return outs


ROW_TILE = 512


def _rows_sum8(t):
    rows, d = t.shape
    return jnp.sum(t.reshape(rows // 8, 8, d), axis=0)


def _rms_rows(x):
    r = lax.rsqrt(jnp.mean(x * x, axis=-1, keepdims=True) + RMS_EPS)
    return x * r, r


def _rms_bwd_rows(dh, x, g):
    xh, r = _rms_rows(x)
    dxh = dh * g
    return r * (dxh - xh * jnp.mean(dxh * xh, axis=-1, keepdims=True)), _rows_sum8(dh * xh)


def _rms_bwd_residual(dh, x, g, dres):
    s, d = x.shape

    def body(dh_ref, x_ref, g_ref, dres_ref, dx_ref, dg_ref):
        dx, dg = _rms_bwd_rows(dh_ref[...], x_ref[...], g_ref[...])
        dx_ref[...] = dres_ref[...] + dx
        dg_ref[...] = dg

    row = pl.BlockSpec((ROW_TILE, d), lambda i: (i, 0))
    return pl.pallas_call(
        body,
        name="norm_mix_bwd",
        grid=(s // ROW_TILE,),
        in_specs=[row, row, pl.BlockSpec((1, d), lambda i: (0, 0)), row],
        out_specs=[row, pl.BlockSpec((8, d), lambda i: (i, 0))],
        out_shape=[jax.ShapeDtypeStruct((s, d), F32), jax.ShapeDtypeStruct((8 * (s // ROW_TILE), d), F32)],
        compiler_params=_cparams(("parallel",)),
    )(dh, x, g, dres)


def _rms_fwd_and_gather(x, g, shard_pack):
    s, d = x.shape
    r_pack, w_pack = shard_pack.shape
    steps = s // ROW_TILE

    def body(x_ref, g_ref, pack_ref, h_ref, ht_ref, others_ref, send_sems, recv_sems):
        i = pl.program_id(0)
        gather = (pack_ref, others_ref, send_sems, recv_sems)

        @pl.when(i == 0)
        def _():
            _gather_start(*gather)

        hv = _rms_rows(x_ref[...])[0] * g_ref[...]
        h_ref[...] = hv.astype(BF16)
        ht_ref[...] = hv.T.astype(BF16)

        @pl.when(i == steps - 1)
        def _():
            _gather_pass_on(*gather)
            _gather_finish(*gather)

    h, ht, others = pl.pallas_call(
        body,
        name="norm_mix",
        grid=(steps,),
        in_specs=[pl.BlockSpec((ROW_TILE, d), lambda i: (i, 0)), pl.BlockSpec((1, d), lambda i: (0, 0)), ANY],
        out_specs=[pl.BlockSpec((ROW_TILE, d), lambda i: (i, 0)), pl.BlockSpec((d, ROW_TILE), lambda i: (0, i)), ANY],
        out_shape=[jax.ShapeDtypeStruct((s, d), BF16), jax.ShapeDtypeStruct((d, s), BF16),
                   jax.ShapeDtypeStruct((N_CHIPS, 2, r_pack // 2, w_pack), shard_pack.dtype)],
        scratch_shapes=[pltpu.SemaphoreType.DMA((6,)), pltpu.SemaphoreType.DMA((6,))],
        compiler_params=_cparams(("arbitrary",)),
    )(x, g, shard_pack.reshape(2, r_pack // 2, w_pack))
    return h, ht, _fill_own_slot(others, shard_pack)


def _alibi_slopes():
    return np.exp2(np.float32(-8.0) * np.arange(1, N_DIL_HEADS + 1, dtype=np.float32) / np.float32(N_DIL_HEADS))


def _head_lane_mask(h, rows):
    lane = lax.broadcasted_iota(jnp.int32, (rows, DIL_W), 1)
    return (lane >= h * HEAD_DIM) & (lane < (h + 1) * HEAD_DIM)


def _pair_lanes(h):
    return slice((h // 2) * 128, (h // 2 + 1) * 128)


def _only_head(t, h):
    part = t[:, _pair_lanes(h)]
    lane = lax.broadcasted_iota(jnp.int32, part.shape, 1)
    keep = (lane < HEAD_DIM) if h % 2 == 0 else (lane >= HEAD_DIM)
    return jnp.where(keep, part, jnp.zeros_like(part))


def _band_terms(dil, has_prev):
    qi = lax.broadcasted_iota(jnp.int32, (BLOCK, 2 * BLOCK), 0)
    kj = lax.broadcasted_iota(jnp.int32, (BLOCK, 2 * BLOCK), 1)
    steps = qi + BLOCK - kj
    valid = (steps >= 0) & (steps <= BLOCK) & ((kj >= BLOCK) | has_prev)
    return valid, steps.astype(F32) * float(dil)


def _load_halves(ref, rows):
    return jnp.concatenate([ref[0, rows, :], ref[1, rows, :]], axis=1)


def _store_halves(ref, rows, value):
    ref[0, rows, :] = value[:, :128]
    ref[1, rows, :] = value[:, 128:]


def _dil_fwd_all(qkv_groups):
    s = qkv_groups[0].shape[0]
    steps = s // BLOCK
    n_groups = len(DIL_GROUPS)
    dils = [d for _, d in DIL_GROUPS]
    views = [qkv_groups[g].reshape(s // dils[g], dils[g] * 3 * DIL_W) for g in range(n_groups)]
    slopes = _alibi_slopes()

    def spec(dil, which, prev):
        def index(t):
            n = t // dil
            return (jnp.maximum(n - 1, 0) if prev else n, (t % dil) * 3 + which)
        return pl.BlockSpec((BLOCK, DIL_W), index)

    def body(*refs):
        t = pl.program_id(0)
        ins, outs = refs[:5 * n_groups], refs[5 * n_groups:]
        masks = [_head_lane_mask(h, BLOCK) for h in range(DIL_HEADS)]
        work = []
        for g in range(n_groups):
            q_ref, kc_ref, kp_ref, vc_ref, vp_ref = ins[5 * g:5 * g + 5]
            q = q_ref[...]
            k2 = jnp.concatenate([kp_ref[...], kc_ref[...]], axis=0)
            v2 = jnp.concatenate([vp_ref[...], vc_ref[...]], axis=0)
            logits = [_dot_nt(_only_head(q, h), k2[:, _pair_lanes(h)]) for h in range(DIL_HEADS)]
            work.append((v2, logits))
        probs = []
        for g in range(n_groups):
            valid, dist = _band_terms(dils[g], t // dils[g] > 0)
            ps, lses = [], []
            for h in range(DIL_HEADS):
                slope = float(slopes[g * DIL_HEADS + h])
                lg = jnp.where(valid, work[g][1][h] * 0.125 - slope * dist, NEG_INF)
                mx = jnp.max(lg, axis=1, keepdims=True)
                lse = mx + jnp.log(jnp.sum(jnp.exp(lg - mx), axis=1, keepdims=True))
                ps.append(jnp.exp(lg - lse).astype(BF16))
                lses.append(lse)
            probs.append((ps, lses))
        for g in range(n_groups):
            dil = dils[g]
            mine = pl.ds(t % dil, BLOCK, stride=dil) if dil > 1 else slice(None)
            o_acc = jnp.zeros((BLOCK, DIL_W), F32)
            lse_acc = jnp.zeros((BLOCK, DIL_W), F32)
            for h in range(DIL_HEADS):
                o_acc = jnp.where(masks[h], _dot_nn(probs[g][0][h], work[g][0]), o_acc)
                lse_acc = jnp.where(masks[h], probs[g][1][h], lse_acc)
            _store_halves(outs[2 * g], mine, o_acc)
            _store_halves(outs[2 * g + 1], mine, lse_acc)

    in_specs, out_specs, operands = [], [], []
    for g, dil in enumerate(dils):
        in_specs += [spec(dil, 0, False), spec(dil, 1, False), spec(dil, 1, True), spec(dil, 2, False),
                     spec(dil, 2, True)]
        out_specs += [pl.BlockSpec((2, BLOCK * dil, 128), functools.partial(lambda d, t: (0, t // d, 0), dil))] * 2
        operands += [views[g]] * 5
    res = pl.pallas_call(
        body,
        name="dil_fwd",
        grid=(steps,),
        in_specs=in_specs,
        out_specs=out_specs,
        out_shape=[jax.ShapeDtypeStruct((2, s, 128), F32)] * (2 * n_groups),
        compiler_params=_cparams(("arbitrary",)),
    )(*operands)
    return [(res[2 * g], res[2 * g + 1]) for g in range(n_groups)]


def _dil_bwd_all(qkv_groups, dos, lses, cterms):
    s = qkv_groups[0].shape[0]
    steps = s // BLOCK
    n_groups = len(DIL_GROUPS)
    dils = [d for _, d in DIL_GROUPS]
    views = [qkv_groups[g].reshape(s // dils[g], dils[g] * 3 * DIL_W) for g in range(n_groups)]
    slopes = _alibi_slopes()
    n_in = 12

    def block_of(dil, shift):
        nb = steps // dil

        def index(t):
            n = t // dil
            if shift < 0:
                return jnp.maximum(n - 1, 0)
            return n if shift == 0 else jnp.minimum(n + 1, nb - 1)
        return index

    def col(dil, which, shift):
        index = block_of(dil, shift)
        return pl.BlockSpec((BLOCK, DIL_W), lambda t: (index(t), (t % dil) * 3 + which))

    def own(dil, shift):
        index = block_of(dil, shift)
        return pl.BlockSpec((2, BLOCK * dil, 128), lambda t: (0, index(t), 0))

    def head_col(v, hm):
        return jnp.max(jnp.where(hm, v, NEG_INF), axis=1, keepdims=True)

    def body(*refs):
        t = pl.program_id(0)
        masks = [_head_lane_mask(h, BLOCK) for h in range(DIL_HEADS)]
        work = []
        for g, dil in enumerate(dils):
            (q_ref, qn_ref, kc_ref, kp_ref, vc_ref, vp_ref, do_ref, don_ref, lse_ref, lsen_ref, c_ref,
             cn_ref) = refs[n_in * g:n_in * (g + 1)]
            mine = pl.ds(t % dil, BLOCK, stride=dil) if dil > 1 else slice(None)
            q, qn = q_ref[...], qn_ref[...]
            kc, vc = kc_ref[...], vc_ref[...]
            k2 = jnp.concatenate([kp_ref[...], kc], axis=0)
            v2 = jnp.concatenate([vp_ref[...], vc], axis=0)
            dov, donv = _load_halves(do_ref, mine), _load_halves(don_ref, mine)
            side = (_load_halves(lse_ref, mine), _load_halves(lsen_ref, mine),
                    _load_halves(c_ref, mine), _load_halves(cn_ref, mine))
            heads = range(DIL_HEADS)
            qhs = [_only_head(q, h) for h in heads]
            qnhs = [_only_head(qn, h) for h in heads]
            dohs = [_only_head(dov, h).astype(BF16) for h in heads]
            donhs = [_only_head(donv, h).astype(BF16) for h in heads]
            prods = ([_dot_nt(qhs[h], k2[:, _pair_lanes(h)]) for h in heads],
                     [_dot_nt(dohs[h], v2[:, _pair_lanes(h)]) for h in heads],
                     [_dot_nt(qnhs[h], kc[:, _pair_lanes(h)]) for h in heads],
                     [_dot_nt(donhs[h], vc[:, _pair_lanes(h)]) for h in heads])
            work.append((k2, qhs, qnhs, dohs, donhs, side, prods))
        grads = []
        for g, dil in enumerate(dils):
            n, nb = t // dil, steps // dil
            valid, dist = _band_terms(dil, n > 0)
            valid_n = _band_terms(dil, True)[0][:, :BLOCK] & (n < nb - 1)
            dist_n = dist[:, :BLOCK]
            (lsev, lsenv, cv, cnv), (logit, dp, logit_n, dp_n) = work[g][5], work[g][6]
            p16, dlog, pn16, dlog_n = [], [], [], []
            for h in range(DIL_HEADS):
                hm, slope = masks[h], float(slopes[g * DIL_HEADS + h])
                p = jnp.where(valid, jnp.exp(logit[h] * 0.125 - slope * dist - head_col(lsev, hm)), 0.0)
                dlog.append((p * (dp[h] + head_col(cv, hm)) * 0.125).astype(BF16))
                p16.append(p.astype(BF16))
                pn = jnp.where(valid_n, jnp.exp(logit_n[h] * 0.125 - slope * dist_n - head_col(lsenv, hm)), 0.0)
                dlog_n.append((pn * (dp_n[h] + head_col(cnv, hm)) * 0.125).astype(BF16))
                pn16.append(pn.astype(BF16))
            grads.append((p16, dlog, pn16, dlog_n))
        for g in range(n_groups):
            k2, qhs, qnhs, dohs, donhs = work[g][:5]
            p16, dlog, pn16, dlog_n = grads[g]
            dq_acc = jnp.zeros((BLOCK, DIL_W), F32)
            dk_pairs = [jnp.zeros((BLOCK, 128), F32)] * 2
            dv_pairs = [jnp.zeros((BLOCK, 128), F32)] * 2
            for h in range(DIL_HEADS):
                dq_acc = jnp.where(masks[h], _dot_nn(dlog[h], k2), dq_acc)
                dk_pairs[h // 2] = dk_pairs[h // 2] + _dot_tn(dlog[h][:, BLOCK:], qhs[h]) + _dot_tn(dlog_n[h], qnhs[h])
                dv_pairs[h // 2] = dv_pairs[h // 2] + _dot_tn(p16[h][:, BLOCK:], dohs[h]) + _dot_tn(pn16[h], donhs[h])
            dq_ref, dk_ref, dv_ref = refs[n_in * n_groups + 3 * g:n_in * n_groups + 3 * g + 3]
            dq_ref[...] = dq_acc.astype(BF16)
            dk_ref[...] = jnp.concatenate(dk_pairs, axis=1).astype(BF16)
            dv_ref[...] = jnp.concatenate(dv_pairs, axis=1).astype(BF16)

    in_specs, out_specs, out_shape, operands = [], [], [], []
    for g, dil in enumerate(dils):
        in_specs += [col(dil, 0, 0), col(dil, 0, 1), col(dil, 1, 0), col(dil, 1, -1), col(dil, 2, 0), col(dil, 2, -1),
                     own(dil, 0), own(dil, 1), own(dil, 0), own(dil, 1), own(dil, 0), own(dil, 1)]
        operands += [views[g]] * 6 + [dos[g], dos[g], lses[g], lses[g], cterms[g], cterms[g]]
        out_specs += [pl.BlockSpec((BLOCK, DIL_W), functools.partial(lambda d, t: (t // d, t % d), dil))] * 3
        out_shape += [jax.ShapeDtypeStruct((s // dil, dil * DIL_W), BF16)] * 3
    outs = pl.pallas_call(
        body,
        name="dil_bwd",
        grid=(steps,),
        in_specs=in_specs,
        out_specs=out_specs,
        out_shape=out_shape,
        compiler_params=_cparams(("parallel",)),
    )(*operands)
    return [tuple(v.reshape(s, DIL_W) for v in outs[3 * g:3 * g + 3]) for g in range(n_groups)]


SB_PAIRS = SB_HEADS // 2
SB_COL0 = 0
LOG2E = 1.4426950408889634


SB_EXP_CLAMP = 64.0


def _sb_softplus2(zs):
    t = 1.0 + jnp.exp2(jnp.minimum(zs, SB_EXP_CLAMP))
    return jnp.maximum(jnp.log(t) * LOG2E, zs)


def _sb_consts(nkb):
    row = lax.broadcasted_iota(jnp.int32, (SB_BQ, SB_BK), 0)
    colk = lax.broadcasted_iota(jnp.int32, (SB_BQ, SB_BK), 1)
    rr = lax.broadcasted_iota(jnp.int32, (SB_BK, SB_BK), 0)
    cc = lax.broadcasted_iota(jnp.int32, (SB_BK, SB_BK), 1)
    lane = lax.broadcasted_iota(jnp.int32, (SB_BQ, 128), 1)
    assert 2 * nkb <= 128
    return colk < row, rr, cc, lane < HEAD_DIM, lane


def _split_heads(t):
    first = lax.broadcasted_iota(jnp.int32, t.shape, 1) < HEAD_DIM
    zero = jnp.zeros_like(t)
    return jnp.where(first, t, zero), jnp.where(first, zero, t)


def _sb_fwd(qkv, shard_pack):
    s = qkv.shape[0]
    nq, nkb = s // SB_BQ, s // SB_BK
    zscale = LOG2E / math.sqrt(HEAD_DIM)
    r_pack, w_pack = shard_pack.shape

    def body(q_ref, k_ref, v_ref, pack_ref, o_ref, a_row, others_ref, zs_scr, a_scr, acc_scr, cl_scr,
             send_sems, recv_sems):
        i = pl.program_id(1)
        pair = pl.program_id(0)
        gather = (pack_ref, others_ref, send_sems, recv_sems)

        @pl.when((pair == 0) & (i == 0))
        def _():
            _gather_start(*gather)

        @pl.when((pair == 1) & (i == 0))
        def _():
            _gather_pass_on(*gather)

        @pl.when((pair == SB_PAIRS - 1) & (i == nq - 1))
        def _():
            _gather_finish(*gather)

        causal, rr, cc, _, _ = _sb_consts(nkb)
        later = (rr > cc).astype(BF16)
        qh = _split_heads(q_ref[...])

        def rows(j):
            return pl.ds(pl.multiple_of(j * SB_BK, SB_BK), SB_BK)

        def scores_to(slot, j):
            kb = k_ref[rows(j), :]
            for hh in range(2):
                zs_scr[slot, hh] = _dot_nt(qh[hh], kb) * zscale

        def weights(slot, j, masked):
            xs, sums, sufs = [], [], []
            for hh in range(2):
                zs = zs_scr[slot, hh]
                sp = _sb_softplus2(zs)
                if masked:
                    sp = jnp.where(causal, sp, 0.0)
                xs.append(zs - sp)
                sums.append(jnp.sum(sp, axis=1, keepdims=True))
                sufs.append(_dot_f32_by_01(sp, later, 2))
            for hh in range(2):
                cl = cl_scr[hh]
                a = jnp.exp2(xs[hh] - (sufs[hh] + jnp.concatenate([cl, cl], axis=1)))
                if masked:
                    a = jnp.where(causal, a, 0.0)
                a16 = a.astype(BF16)
                a_scr[slot, :, hh * SB_BK:(hh + 1) * SB_BK] = a16
                a_row[0, 0, j, :, hh * SB_BK:(hh + 1) * SB_BK] = a16
                cl_scr[hh] = cl + sums[hh]

        def add_av(slot, j):
            v0, v1 = _split_heads(v_ref[rows(j), :])
            acc_scr[...] += _dot_nn(a_scr[slot], jnp.concatenate([v0, v1], axis=0))

        acc_scr[...] = jnp.zeros_like(acc_scr)
        cl_scr[...] = jnp.zeros_like(cl_scr)
        scores_to(0, i)
        scores_to(1, jnp.maximum(i - 1, 0))
        weights(0, i, True)

        def step(j, prev, cur):
            scores_to(prev, jnp.maximum(j - 1, 0))
            add_av(prev, j + 1)
            weights(cur, j, False)

        def two_steps(u, _):
            j = i - 1 - 2 * u
            step(j, 0, 1)
            step(j - 1, 1, 0)
            return 0

        lax.fori_loop(0, i // 2, two_steps, 0)

        @pl.when(i % 2 == 1)
        def _():
            step(0, 0, 1)
            add_av(1, 0)

        @pl.when(i % 2 == 0)
        def _():
            add_av(0, 0)

        o_ref[...] = acc_scr[...]

    def full(which):
        return pl.BlockSpec((s, 128), lambda p, i: (0, SB_COL0 + 4 * which + p))

    return pl.pallas_call(
        body,
        name="sb_fwd",
        grid=(SB_PAIRS, nq),
        in_specs=[pl.BlockSpec((SB_BQ, 128), lambda p, i: (i, SB_COL0 + p)), full(1), full(2), ANY],
        out_specs=[pl.BlockSpec((SB_BQ, 128), lambda p, i: (i, p)),
                   pl.BlockSpec((1, 1, nkb, SB_BQ, 2 * SB_BK), lambda p, i: (p, i, 0, 0, 0)), ANY],
        out_shape=[jax.ShapeDtypeStruct((s, SB_W), F32),
                   jax.ShapeDtypeStruct((SB_PAIRS, nq, nkb, SB_BQ, 2 * SB_BK), BF16),
                   jax.ShapeDtypeStruct((N_CHIPS, 2, r_pack // 2, w_pack), shard_pack.dtype)],
        scratch_shapes=[pltpu.VMEM((2, 2, SB_BQ, SB_BK), F32), pltpu.VMEM((2, SB_BQ, 2 * SB_BK), BF16),
                        pltpu.VMEM((SB_BQ, 128), F32), pltpu.VMEM((2, SB_BQ, 128), F32),
                        pltpu.SemaphoreType.DMA((6,)), pltpu.SemaphoreType.DMA((6,))],
        compiler_params=_cparams(("arbitrary", "arbitrary")),
    )(qkv, qkv, qkv, shard_pack.reshape(2, r_pack // 2, w_pack))


def _sb_bwd(qkv, do, a_hbm, chip_sums):
    s = qkv.shape[0]
    nq, nkb = s // SB_BQ, s // SB_BK
    scale = 1.0 / math.sqrt(HEAD_DIM)
    zscale = LOG2E * scale

    def body(q_ref, k_ref, v_ref, do_ref, a_row, sums_ref, dq_ref, dk_ref, dv_ref, got_ref,
             zs_scr, da_scr, dz_scr, a_scr, cg_scr, send_sems, recv_sems):
        i = pl.program_id(1)
        pair = pl.program_id(0)
        first_step = (pair == 0) & (i == 0)
        last_step = (pair == SB_PAIRS - 1) & (i == nq - 1)

        @pl.when(first_step)
        def _():
            _exchange_start(sums_ref, got_ref, send_sems, recv_sems)

        @pl.when(i == 0)
        def _():
            dk_ref[...] = jnp.zeros_like(dk_ref)
            dv_ref[...] = jnp.zeros_like(dv_ref)

        causal, rr, cc, first, _ = _sb_consts(nkb)
        earlier = (rr < cc).astype(BF16)
        q2 = q_ref[...]
        qh = _split_heads(q2)
        do2 = do_ref[...].astype(BF16)
        doh = _split_heads(do2)

        def rows(j):
            return pl.ds(pl.multiple_of(j * SB_BK, SB_BK), SB_BK)

        def products_to(slot, j):
            kb, vb = k_ref[rows(j), :], v_ref[rows(j), :]
            for hh in range(2):
                zs_scr[slot, hh] = _dot_nt(qh[hh], kb) * (-zscale)
                da_scr[slot, hh] = _dot_nt(doh[hh], vb)

        head0_rows = lax.broadcasted_iota(jnp.int32, (128, SB_BK), 0) < HEAD_DIM

        def by_head(t):
            return jnp.where(head0_rows, t[:, :SB_BK], t[:, SB_BK:])

        def apply(slot, j):
            k0, k1 = _split_heads(k_ref[rows(j), :])
            dq_ref[...] += _dot_nn(dz_scr[slot], jnp.concatenate([k0, k1], axis=0)) * scale
            dk_ref[0, j] += by_head(_dot_tn(q2, dz_scr[slot])) * scale
            dv_ref[0, j] += by_head(_dot_tn(do2, a_scr[slot]))

        def grads(slot, j, masked):
            gs, gpres = [], []
            for hh in range(2):
                a16 = a_row[0, 0, j, :, hh * SB_BK:(hh + 1) * SB_BK]
                a_scr[slot, :, hh * SB_BK:(hh + 1) * SB_BK] = a16
                g = a16.astype(F32) * da_scr[slot, hh]
                gs.append(g)
                gpres.append(_dot_f32_by_01(g, earlier, 1))
            sigs = []
            for hh in range(2):
                e = jnp.exp2(jnp.minimum(zs_scr[slot, hh], SB_EXP_CLAMP))
                sigs.append(pl.reciprocal(1.0 + e, approx=True))
            for hh in range(2):
                cg = cg_scr[hh]
                dz = gs[hh] - (gs[hh] + (gpres[hh] + jnp.concatenate([cg, cg], axis=1))) * sigs[hh]
                if masked:
                    dz = jnp.where(causal, dz, 0.0)
                dz_scr[slot, :, hh * SB_BK:(hh + 1) * SB_BK] = dz.astype(BF16)
                cg_scr[hh] = cg + jnp.sum(gs[hh], axis=1, keepdims=True)

        dq_ref[...] = jnp.zeros_like(dq_ref)
        cg_scr[...] = jnp.zeros_like(cg_scr)
        dz_scr[1] = jnp.zeros((SB_BQ, 2 * SB_BK), BF16)
        a_scr[1] = jnp.zeros((SB_BQ, 2 * SB_BK), BF16)
        products_to(0, 0)

        def step(j, cur, nxt):
            products_to(nxt, j + 1)
            apply(nxt, jnp.maximum(j - 1, 0))
            grads(cur, j, False)

        def two_steps(u, _):
            step(2 * u, 0, 1)
            step(2 * u + 1, 1, 0)
            return 0

        lax.fori_loop(0, i // 2, two_steps, 0)

        def last(cur, nxt):
            apply(nxt, jnp.maximum(i - 1, 0))
            grads(cur, i, True)
            apply(cur, i)

        @pl.when(i % 2 == 1)
        def _():
            step(i - 1, 0, 1)
            last(1, 0)

        @pl.when(i % 2 == 0)
        def _():
            last(0, 1)

        @pl.when(last_step)
        def _():
            _exchange_wait(sums_ref, got_ref, send_sems, recv_sems)

    def full(which):
        return pl.BlockSpec((s, 128), lambda p, i: (0, SB_COL0 + 4 * which + p))

    qblk = pl.BlockSpec((SB_BQ, 128), lambda p, i: (i, p))
    acc = pl.BlockSpec((1, nkb, 128, SB_BK), lambda p, i: (p, 0, 0, 0))
    acc_shape = jax.ShapeDtypeStruct((SB_PAIRS, nkb, 128, SB_BK), F32)
    dq, dk_t, dv_t, got = pl.pallas_call(
        body,
        name="sb_bwd",
        grid=(SB_PAIRS, nq),
        in_specs=[pl.BlockSpec((SB_BQ, 128), lambda p, i: (i, SB_COL0 + p)), full(1), full(2), qblk,
                  pl.BlockSpec((1, 1, nkb, SB_BQ, 2 * SB_BK), lambda p, i: (p, i, 0, 0, 0)), ANY],
        out_specs=[qblk, acc, acc, ANY],
        out_shape=[jax.ShapeDtypeStruct((s, SB_W), F32), acc_shape, acc_shape,
                   jax.ShapeDtypeStruct(chip_sums.shape, chip_sums.dtype)],
        scratch_shapes=[pltpu.VMEM((2, 2, SB_BQ, SB_BK), F32), pltpu.VMEM((2, 2, SB_BQ, SB_BK), F32),
                        pltpu.VMEM((2, SB_BQ, 2 * SB_BK), BF16), pltpu.VMEM((2, SB_BQ, 2 * SB_BK), BF16),
                        pltpu.VMEM((2, SB_BQ, 128), F32),
                        pltpu.SemaphoreType.DMA((3,)), pltpu.SemaphoreType.DMA((3,))],
        compiler_params=_cparams(("arbitrary", "arbitrary")),
    )(qkv, qkv, qkv, do, a_hbm, chip_sums)

    def untranspose(t):
        return jnp.transpose(t, (1, 3, 0, 2)).reshape(s, SB_W)

    return dq, untranspose(dk_t), untranspose(dv_t), got


MERGE_TILE = 256


def _group_mix(lses):
    mx = jnp.maximum(jnp.maximum(lses[0], lses[1]), lses[2])
    es = [jnp.exp(t - mx) for t in lses]
    den = es[0] + es[1] + es[2]
    return [e / den for e in es]


def _merge_fwd(o_groups, lse_groups, o_sb, gl, b_gate, w_up_dil, w_up_sb):
    s = gl.shape[0]
    t = MERGE_TILE

    def body(o0, o1, o2, l0, l1, l2, ob_ref, gl_ref, bg_ref, wd_ref, ws_ref, merged_ref, oa_ref):
        rows = slice(None)
        w = _group_mix([_load_halves(l, rows) for l in (l0, l1, l2)])
        og = [_load_halves(o, rows) for o in (o0, o1, o2)]
        oa = (w[0] * og[0] + w[1] * og[1] + w[2] * og[2]).astype(BF16)
        ua = _dot_nn(oa, wd_ref[...])
        ub = _dot_nn(ob_ref[...].astype(BF16), ws_ref[...])
        gate = jax.nn.sigmoid(gl_ref[...] + bg_ref[...])
        merged_ref[...] = (gate[:, :D_MODEL] * ua + gate[:, D_MODEL:] * ub).astype(BF16)
        oa_ref[...] = oa

    dil = pl.BlockSpec((t, DIL_W), lambda i: (i, 0))
    halves = pl.BlockSpec((2, t, 128), lambda i: (0, i, 0))
    const = lambda shape: pl.BlockSpec(shape, lambda i: (0, 0))
    return pl.pallas_call(
        body,
        name="merge_fwd",
        grid=(s // t,),
        in_specs=[halves] * 6 + [pl.BlockSpec((t, SB_W), lambda i: (i, 0)), pl.BlockSpec((t, GATE_W), lambda i: (i, 0)),
                                 const((1, GATE_W)), const((DIL_W, D_MODEL)), const((SB_W, D_MODEL))],
        out_specs=[pl.BlockSpec((t, D_MODEL), lambda i: (i, 0)), dil],
        out_shape=[jax.ShapeDtypeStruct((s, D_MODEL), BF16), jax.ShapeDtypeStruct((s, DIL_W), BF16)],
        compiler_params=_cparams(("parallel",)),
    )(*o_groups, *lse_groups, o_sb, gl, b_gate, w_up_dil, w_up_sb)


def _merge_bwd(dmerged, o_groups, lse_groups, o_sb, gl, b_gate, w_up_dil, w_up_sb, swap):
    s = gl.shape[0]
    t = MERGE_TILE
    n_chunks, r_swap, w_swap = swap.shape
    swap = swap.reshape(n_chunks, 2, r_swap // 2, w_swap)

    def body(dm_ref, o0, o1, o2, l0, l1, l2, ob_ref, gl_ref, bg_ref, wd_ref, ws_ref, swap_ref,
             dua_ref, dub_ref, dgl_ref, dbg_ref, dosb_ref, d0, d1, d2, c0, c1, c2, got_ref, send_sem, recv_sem):
        i = pl.program_id(0)

        @pl.when(i == 0)
        def _():
            _swap_copy(swap_ref, got_ref, send_sem, recv_sem).start()

        @pl.when(i == pl.num_programs(0) - 1)
        def _():
            _swap_copy(swap_ref, got_ref, send_sem, recv_sem).wait()

        rows = slice(None)
        og = [_load_halves(o, rows) for o in (o0, o1, o2)]
        w = _group_mix([_load_halves(l, rows) for l in (l0, l1, l2)])
        oa = (w[0] * og[0] + w[1] * og[1] + w[2] * og[2]).astype(BF16)
        ua = _dot_nn(oa, wd_ref[...])
        ub = _dot_nn(ob_ref[...].astype(BF16), ws_ref[...])
        gate = jax.nn.sigmoid(gl_ref[...] + bg_ref[...])
        ga, gb = gate[:, :D_MODEL], gate[:, D_MODEL:]
        dm = dm_ref[...]
        dua = (dm * ga).astype(BF16)
        dub = (dm * gb).astype(BF16)
        dua_ref[...] = dua
        dub_ref[...] = dub
        dgl_a = dm * ua * ga * (1.0 - ga)
        dgl_b = dm * ub * gb * (1.0 - gb)
        dgl_ref[:, :D_MODEL] = dgl_a.astype(BF16)
        dgl_ref[:, D_MODEL:] = dgl_b.astype(BF16)
        part = jnp.concatenate([jnp.sum(dgl_a.reshape(t // 8, 8, D_MODEL), axis=0),
                                jnp.sum(dgl_b.reshape(t // 8, 8, D_MODEL), axis=0)], axis=1)

        @pl.when(i == 0)
        def _():
            dbg_ref[...] = part

        @pl.when(i > 0)
        def _():
            dbg_ref[...] += part

        dosb_ref[...] = _dot_nt(dub, ws_ref[...])
        doa = _dot_nt(dua, wd_ref[...])
        rr = lax.broadcasted_iota(jnp.int32, (DIL_W, DIL_W), 0) // HEAD_DIM
        cc = lax.broadcasted_iota(jnp.int32, (DIL_W, DIL_W), 1) // HEAD_DIM
        same_head = (rr == cc).astype(BF16)
        dw = [_dot_f32_by_01(doa * og[g], same_head, 2) for g in range(3)]
        mean_dw = w[0] * dw[0] + w[1] * dw[1] + w[2] * dw[2]
        for g, (d_ref, c_ref) in enumerate(((d0, c0), (d1, c1), (d2, c2))):
            _store_halves(d_ref, rows, w[g] * doa)
            _store_halves(c_ref, rows, -w[g] * mean_dw)

    dil = pl.BlockSpec((2, t, 128), lambda i: (0, i, 0))
    wide = pl.BlockSpec((t, D_MODEL), lambda i: (i, 0))
    gate2 = pl.BlockSpec((t, GATE_W), lambda i: (i, 0))
    sbw = pl.BlockSpec((t, SB_W), lambda i: (i, 0))
    const = lambda shape: pl.BlockSpec(shape, lambda i: (0, 0))
    return pl.pallas_call(
        body,
        name="merge_bwd",
        grid=(s // t,),
        in_specs=[wide] + [dil] * 6 + [sbw, gate2, const((1, GATE_W)), const((DIL_W, D_MODEL)), const((SB_W, D_MODEL)),
                                       ANY],
        out_specs=[wide, wide, gate2, const((8, GATE_W)), sbw] + [dil] * 6 + [ANY],
        out_shape=[jax.ShapeDtypeStruct((s, D_MODEL), BF16), jax.ShapeDtypeStruct((s, D_MODEL), BF16),
                   jax.ShapeDtypeStruct((s, GATE_W), BF16), jax.ShapeDtypeStruct((8, GATE_W), F32),
                   jax.ShapeDtypeStruct((s, SB_W), F32)] + [jax.ShapeDtypeStruct((2, s, 128), F32)] * 6
        + [jax.ShapeDtypeStruct((n_chunks, r_swap // 2, w_swap), swap.dtype)],
        scratch_shapes=[pltpu.SemaphoreType.DMA, pltpu.SemaphoreType.DMA],
        compiler_params=_cparams(("arbitrary",)),
    )(dmerged, *o_groups, *lse_groups, o_sb, gl, b_gate, w_up_dil, w_up_sb, swap)


ANY = pl.BlockSpec(memory_space=pl.ANY)


def _place():
    x, y, c = lax.axis_index("x"), lax.axis_index("y"), lax.axis_index("c")
    other_chips = [(1 - x, y), (x, 1 - y), (1 - x, 1 - y)]
    return x, y, c, other_chips


def _gather_copies(p_ref, out_ref, send_sems, recv_sems):
    x, y, c, chips = _place()
    me, sibling = 2 * x + y, (x, y, 1 - c)
    idx = [2 * chip[0] + chip[1] for chip in chips]

    def copy(k, chip_idx, core, to, src=None):
        return pltpu.make_async_remote_copy(
            src_ref=out_ref.at[chip_idx, core] if src is None else src, dst_ref=out_ref.at[chip_idx, core],
            send_sem=send_sems.at[k], recv_sem=recv_sems.at[k], device_id=to, device_id_type=MESH)

    first = lambda j: copy(j, me, c, (*chips[j], c), src=p_ref.at[c])
    landed = lambda j: copy(j, idx[j], c, (x, y, c))
    passed = lambda j: copy(3 + j, idx[j], c, sibling)
    handed = lambda j: copy(3 + j, idx[j], 1 - c, (x, y, c))
    return first, landed, passed, handed


def _gather_start(*refs):
    first = _gather_copies(*refs)[0]
    for j in range(3):
        first(j).start()


def _gather_pass_on(*refs):
    _, landed, passed, _ = _gather_copies(*refs)
    for j in range(3):
        landed(j).wait_recv()
        passed(j).start()


def _gather_finish(*refs):
    first, _, passed, handed = _gather_copies(*refs)
    for j in range(3):
        handed(j).wait_recv()
    for j in range(3):
        first(j).wait_send()
        passed(j).wait_send()


def _fill_own_slot(others, pack):
    n, _, rh, wd = others.shape
    me = 2 * lax.axis_index("x") + lax.axis_index("y")
    mine = lax.broadcasted_iota(jnp.int32, (n, 1, 1, 1), 0) == me
    return jnp.where(mine, pack.reshape(1, 2, rh, wd), others).reshape(n, 2 * rh, wd)


def _swap_copy(g_ref, out_ref, send_sem, recv_sem):
    x, y, c, _ = _place()
    return pltpu.make_async_remote_copy(
        src_ref=g_ref.at[:, 1 - c], dst_ref=out_ref,
        send_sem=send_sem, recv_sem=recv_sem, device_id=(x, y, 1 - c), device_id_type=MESH)


def _swap_halves(g):
    n, r, wd = g.shape
    rh = r // 2
    g = g.reshape(n, 2, rh, wd)

    def body(g_ref, out_ref, send_sem, recv_sem):
        cp = _swap_copy(g_ref, out_ref, send_sem, recv_sem)
        cp.start()
        cp.wait()

    return pl.pallas_call(
        body,
        name="grad_swap_halves",
        in_specs=[ANY],
        out_specs=ANY,
        out_shape=jax.ShapeDtypeStruct((n, rh, wd), g.dtype),
        scratch_shapes=[pltpu.SemaphoreType.DMA, pltpu.SemaphoreType.DMA],
    )(g)


def _add_halves(g, got, core):
    n, r, wd = g.shape
    rh = r // 2
    t = rh // 4
    nt = rh // t

    def body(c_ref, a_ref, b_ref, o_ref):
        o_ref[...] = (a_ref[0] + b_ref[...]).astype(BF16)

    grid_spec = pltpu.PrefetchScalarGridSpec(
        num_scalar_prefetch=1,
        grid=(n, nt),
        in_specs=[pl.BlockSpec((1, 1, t, wd), lambda s, i, c: (s, c[0], i, 0)),
                  pl.BlockSpec((1, t, wd), lambda s, i, c: (s, i, 0))],
        out_specs=pl.BlockSpec((1, t, wd), lambda s, i, c: (s, i, 0)),
    )
    return pl.pallas_call(
        body,
        name="grad_add_halves",
        grid_spec=grid_spec,
        out_shape=jax.ShapeDtypeStruct((n, rh, wd), BF16),
        compiler_params=_cparams(("parallel", "parallel")),
    )(core, g.reshape(n, 2, rh, wd), got)


def _exchange_copies(h_ref, out_ref, send_sems, recv_sems):
    x, y, c, chips = _place()
    me = 2 * x + y

    def copy(j, slot):
        them = 2 * chips[j][0] + chips[j][1]
        return pltpu.make_async_remote_copy(
            src_ref=h_ref.at[them], dst_ref=out_ref.at[me if slot == "mine" else them],
            send_sem=send_sems.at[j], recv_sem=recv_sems.at[j], device_id=(*chips[j], c), device_id_type=MESH)

    return (lambda j: copy(j, "mine")), (lambda j: copy(j, "theirs"))


def _exchange_start(h_ref, out_ref, send_sems, recv_sems):
    send = _exchange_copies(h_ref, out_ref, send_sems, recv_sems)[0]
    for j in range(3):
        send(j).start()


def _exchange_wait(h_ref, out_ref, send_sems, recv_sems):
    send, arrival = _exchange_copies(h_ref, out_ref, send_sems, recv_sems)
    for j in range(3):
        arrival(j).wait_recv()
    for j in range(3):
        send(j).wait_send()


def _sum_chips(b, h, chip):
    n, rh, wd = b.shape
    t = rh // 4

    def body(chip_ref, b_ref, own_ref, o_ref):
        own = own_ref[0]
        s0, s1, s2, s3 = (jnp.where(chip_ref[0] == k, own, b_ref[k]).astype(F32) for k in range(n))
        o_ref[...] = ((s0 + s1) + s2) + s3

    grid_spec = pltpu.PrefetchScalarGridSpec(
        num_scalar_prefetch=1,
        grid=(rh // t,),
        in_specs=[pl.BlockSpec((n, t, wd), lambda i, chip: (0, i, 0)),
                  pl.BlockSpec((1, t, wd), lambda i, chip: (chip[0], i, 0))],
        out_specs=pl.BlockSpec((t, wd), lambda i, chip: (i, 0)),
    )
    return pl.pallas_call(
        body,
        name="grad_sum_chips",
        grid_spec=grid_spec,
        out_shape=jax.ShapeDtypeStruct((rh, wd), F32),
        compiler_params=_cparams(("parallel",)),
    )(chip, b, h)


def _join_halves(tc):
    rh, wd = tc.shape

    def body(t_ref, out_ref, send_sem, recv_sem):
        x, y, c, _ = _place()
        cp = pltpu.make_async_remote_copy(
            src_ref=t_ref, dst_ref=out_ref.at[c],
            send_sem=send_sem, recv_sem=recv_sem, device_id=(x, y, 1 - c), device_id_type=MESH)
        cp.start()
        cp.wait()

    halves = pl.pallas_call(
        body,
        name="grad_join_halves",
        in_specs=[ANY],
        out_specs=ANY,
        out_shape=jax.ShapeDtypeStruct((2, rh, wd), tc.dtype),
        scratch_shapes=[pltpu.SemaphoreType.DMA, pltpu.SemaphoreType.DMA],
    )(tc)
    return lax.dynamic_update_slice(halves, tc[None], (lax.axis_index("c"), 0, 0)).reshape(2 * rh, wd)


def _all_reduce_small(pack):
    rows, lanes = pack.shape

    def body(p_ref, out_ref, buf, send_sems, recv_sems):
        x, y, c, _ = _place()
        me = 4 * x + 2 * y + c
        buf[me] = p_ref[...]
        sends = []
        for k in range(1, N_DEV):
            peer = (x ^ (k >> 2), y ^ ((k >> 1) & 1), c ^ (k & 1))
            sends.append(pltpu.make_async_remote_copy(
                src_ref=p_ref, dst_ref=buf.at[me], send_sem=send_sems.at[k - 1], recv_sem=recv_sems.at[k - 1],
                device_id=peer, device_id_type=MESH))
        for cp in sends:
            cp.start()
        for k in range(1, N_DEV):
            pltpu.make_async_remote_copy(
                src_ref=p_ref, dst_ref=buf.at[me ^ k], send_sem=send_sems.at[k - 1], recv_sem=recv_sems.at[k - 1],
                device_id=(x, y, c), device_id_type=MESH).wait_recv()
        for cp in sends:
            cp.wait_send()
        total = buf[0]
        for d in range(1, N_DEV):
            total = total + buf[d]
        out_ref[...] = total

    vm = pl.BlockSpec(memory_space=pltpu.VMEM)
    return pl.pallas_call(
        body,
        name="all_reduce_small",
        in_specs=[vm],
        out_specs=vm,
        out_shape=jax.ShapeDtypeStruct((rows, lanes), F32),
        scratch_shapes=[pltpu.VMEM((N_DEV, rows, lanes), F32), pltpu.SemaphoreType.DMA((N_DEV - 1,)),
                        pltpu.SemaphoreType.DMA((N_DEV - 1,))],
    )(pack)


def _adamw(g, w, m, v, name):
    rows, cols = g.shape
    t = rows
    for cand in (256, 128, 64, 32, 16, 8):
        if rows % cand == 0:
            t = cand
            break

    def body(g_ref, w_ref, m_ref, v_ref, d_ref, nm_ref, nv_ref):
        gv = g_ref[...]
        mv = ADAM_B1 * m_ref[...] + (1.0 - ADAM_B1) * gv
        vv = ADAM_B2 * v_ref[...] + (1.0 - ADAM_B2) * (gv * gv)
        m_hat = mv / (1.0 - ADAM_B1 ** ADAM_STEP)
        v_hat = vv / (1.0 - ADAM_B2 ** ADAM_STEP)
        d_ref[...] = -ADAM_LR * (m_hat / (jnp.sqrt(v_hat) + ADAM_EPS) + ADAM_WD * w_ref[...])
        nm_ref[...] = mv
        nv_ref[...] = vv

    blk = pl.BlockSpec((t, cols), lambda i: (i, 0))
    return pl.pallas_call(
        body,
        name=name,
        grid=(rows // t,),
        in_specs=[blk] * 4,
        out_specs=[blk] * 3,
        out_shape=[jax.ShapeDtypeStruct((rows, cols), F32)] * 3,
        compiler_params=_cparams(("parallel",)),
    )(g, w, m, v)


PACK_W = 1024
BIG = (("w_in", (D_MODEL, IN_COLS), 1), ("w_up_dil", (DIL_W, D_MODEL), 1), ("w_up_sb", (SB_W, D_MODEL), 1),
       ("w_out", (D_MODEL, D_MODEL), 0), ("w_mlp_in", (D_MODEL, D_FF), 1), ("w_mlp_out", (D_FF, D_MODEL), 0))


def _shard_shape(shape, axis):
    return tuple(d // N_CHIPS if a == axis else d for a, d in enumerate(shape))


MIXER_GROUP, MLP_GROUP = BIG[:4], BIG[4:]
LATE_WEIGHTS = BIG[1:]


def _pack_rows(group=BIG):
    rows, at = {}, 0
    for name, shape, axis in group:
        n = math.prod(_shard_shape(shape, axis)) // PACK_W
        rows[name] = (at, n)
        at += n
    return rows, at


def _pack_shards(shards, group):
    return jnp.concatenate([shards[name].reshape(-1, PACK_W) for name, _, _ in group], axis=0)


def _unpack_full(gathered, group):
    rows, _ = _pack_rows(group)
    full = {}
    for name, shape, axis in group:
        at, n = rows[name]
        parts = gathered[:, at:at + n, :].reshape((N_CHIPS,) + _shard_shape(shape, axis))
        if axis == 0:
            full[name] = parts.reshape(shape)
        else:
            full[name] = jnp.transpose(parts, (1, 0, 2)).reshape(shape)
    return full


def _pack_full_grads(grads, group):
    chunks = []
    for name, shape, axis in group:
        g = grads[name]
        if axis == 0:
            parts = g.reshape((N_CHIPS, shape[0] // N_CHIPS, shape[1]))
        else:
            parts = jnp.transpose(g.reshape((shape[0], N_CHIPS, shape[1] // N_CHIPS)), (1, 0, 2))
        chunks.append(parts.reshape(N_CHIPS, -1, PACK_W))
    return jnp.concatenate(chunks, axis=1)


def _unpack_shard(packed, group):
    rows, _ = _pack_rows(group)
    return {name: packed[rows[name][0]:rows[name][0] + rows[name][1]].reshape(_shard_shape(shape, axis))
            for name, shape, axis in group}


def _local_step(x, target, early_shards, late_shards, norm_mix_g, b_gate, norm_mlp_g, norm_final_g, core):
    h, h_t, early = _rms_fwd_and_gather(x, norm_mix_g, early_shards)
    w = {"w_in": jnp.transpose(early, (1, 0, 2)).reshape(D_MODEL, IN_COLS)}
    w_in = w["w_in"]
    sb0 = 9 * DIL_W
    w_sb, w_gate = w_in[:, sb0:QKV_W], w_in[:, QKV_W:]
    w_dil = [jnp.concatenate([w_in[:, (3 * i + g) * DIL_W:(3 * i + g + 1) * DIL_W] for i in range(3)], axis=1)
             for g in range(3)]

    qkv_dil = [_matmul(h, w_dil[g], mode="nn", out_dtypes=(BF16,), name=f"proj_dil_g{g}", tn=768)[0] for g in range(3)]
    (qkv_sb,) = _matmul(h, w_sb, mode="nn", out_dtypes=(BF16,), name="proj_sb", tn=768)
    (gl,) = _matmul(h, w_gate, mode="nn", out_dtypes=(F32,), name="proj_gate")
    dil = _dil_fwd_all(qkv_dil)
    o_groups, lse_groups = [d[0] for d in dil], [d[1] for d in dil]
    o_sb, a_sb, late_others = _sb_fwd(qkv_sb, late_shards)
    w = {**w, **_unpack_full(_fill_own_slot(late_others, late_shards), LATE_WEIGHTS)}
    merged, o_a = _merge_fwd(o_groups, lse_groups, o_sb, gl, b_gate, w["w_up_dil"], w["w_up_sb"])
    def residual_and_norm(acc, res, g):
        x1 = res + acc
        return x1, _rms_rows(x1)[0] * g

    x1, h2 = _matmul(merged, w["w_out"], mode="nn", out_dtypes=(F32, BF16), name="out_proj", tm=ROW_TILE,
                     extras=(x, norm_mlp_g), epilogue=residual_and_norm)
    u, act = _matmul(h2, w["w_mlp_in"], mode="nn", out_dtypes=(BF16, BF16), name="mlp_in",
                     epilogue=lambda acc: (acc, jnp.square(jnp.maximum(acc, 0.0))))

    def residual_and_loss(acc, res, tgt, g):
        xh, r = _rms_rows(res + acc)
        err = xh * g - tgt
        dy = err * (1.0 / D_MODEL)
        dxh = dy * g
        dx2 = r * (dxh - xh * jnp.mean(dxh * xh, axis=-1, keepdims=True))
        return dx2, _rows_sum8(dy * xh), (0.5 / D_MODEL) * _rows_sum8(err * err)

    dx2, dg_final, loss_part = _matmul(
        act, w["w_mlp_out"], mode="nn", out_dtypes=(F32, ("part", F32), ("part", F32)), name="mlp_out", tm=ROW_TILE,
        tk=2048, extras=(x1, target, norm_final_g.reshape(1, D_MODEL)), epilogue=residual_and_loss)

    (du,) = _matmul(dx2, w["w_mlp_out"], mode="nt", out_dtypes=(BF16,), name="mlp_out_dx",
                    extras=(u,), epilogue=lambda acc, uu: (acc * (2.0 * jnp.maximum(uu.astype(F32), 0.0)),))
    pack_shape = (N_CHIPS, 2 * D_MODEL, D_MODEL)
    (half_pack,) = _matmul(act, dx2, mode="tn", out_dtypes=(F32,), name="mlp_out_dw",
                           into=(pack_shape, lambda i, j: (i, 1, 0), None))
    (mlp_pack,) = _matmul(h2, du, mode="tn", out_dtypes=(F32,), name="mlp_in_dw",
                          into=(pack_shape, lambda i, j: (j, 0, 0), half_pack))

    def norm_bwd(acc, xx, dres, g):
        dx, dg = _rms_bwd_rows(acc, xx, g)
        return dres + dx, dg

    dx1, dg_mlp = _matmul(du, w["w_mlp_in"], mode="nt", out_dtypes=(F32, ("part", F32)), name="mlp_in_dx",
                          tm=ROW_TILE, tk=2048, extras=(x1, dx2, norm_mlp_g), epilogue=norm_bwd)

    (dmerged,) = _matmul(dx1, w["w_out"], mode="nt", out_dtypes=(F32,), name="out_proj_dx")
    (g_out,) = _matmul(merged, dx1, mode="tn", out_dtypes=(F32,), name="out_proj_dw")
    mb = _merge_bwd(dmerged, o_groups, lse_groups, o_sb, gl, b_gate, w["w_up_dil"], w["w_up_sb"], mlp_pack)
    dua, dub, dgl, dbg, do_sb = mb[:5]
    do_groups, c_groups = mb[5:8], mb[8:11]
    mlp_sums = _add_halves(mlp_pack, mb[11], core)
    (g_up_dil,) = _matmul(o_a, dua, mode="tn", out_dtypes=(F32,), name="up_dil_dw")
    (g_up_sb,) = _matmul(o_sb, dub, mode="tn", out_dtypes=(F32,), name="up_sb_dw")
    dq_sb, dk_sb, dv_sb, mlp_got = _sb_bwd(qkv_sb, do_sb, a_sb, mlp_sums)
    dil_b = _dil_bwd_all(qkv_dil, do_groups, lse_groups, c_groups)
    dproj = jnp.concatenate(
        [dil_b[g][i].astype(BF16) for i in range(3) for g in range(3)]
        + [t.astype(BF16) for t in (dq_sb, dk_sb, dv_sb)] + [dgl], axis=1)
    (g_in,) = _matmul(h_t, dproj, mode="nn", out_dtypes=(F32,), name="proj_dw", tm=512, tn=IN_COLS // 2)
    mixer_pack = _pack_full_grads({"w_in": g_in, "w_up_dil": g_up_dil, "w_up_sb": g_up_sb, "w_out": g_out}, MIXER_GROUP)
    mixer_sums = _add_halves(mixer_pack, _swap_halves(mixer_pack), core)
    dh, mixer_got = _matmul(dproj, w["w_in"], mode="nt", out_dtypes=(F32,), name="proj_dx", tk=IN_COLS // 2,
                            exchange=mixer_sums)
    grad_x, dg_mix = _rms_bwd_residual(dh, x, norm_mix_g, dx1)

    small = (dg_mix, dbg, dg_mlp, dg_final, loss_part)
    return grad_x, (mixer_got, mixer_sums), (mlp_got, mlp_sums), small


def kernel(x, norm_mix_g, w_in, b_gate, w_up_dil, w_up_sb, w_out, norm_mlp_g, w_mlp_in, w_mlp_out, norm_final_g, loss_target, m_norm_mix_g, m_w_in, m_b_gate, m_w_up_dil, m_w_up_sb, m_w_out, m_norm_mlp_g, m_w_mlp_in, m_w_mlp_out, m_norm_final_g, v_norm_mix_g, v_w_in, v_b_gate, v_w_up_dil, v_w_up_sb, v_w_out, v_norm_mlp_g, v_w_mlp_in, v_w_mlp_out, v_norm_final_g):
    shards = {"w_in": w_in[0], "w_up_dil": w_up_dil[0], "w_up_sb": w_up_sb[0], "w_out": w_out[0],
              "w_mlp_in": w_mlp_in[0], "w_mlp_out": w_mlp_out[0]}
    moments_m = {"w_in": m_w_in[0], "w_up_dil": m_w_up_dil[0], "w_up_sb": m_w_up_sb[0], "w_out": m_w_out[0],
                 "w_mlp_in": m_w_mlp_in[0], "w_mlp_out": m_w_mlp_out[0]}
    moments_v = {"w_in": v_w_in[0], "w_up_dil": v_w_up_dil[0], "w_up_sb": v_w_up_sb[0], "w_out": v_w_out[0],
                 "w_mlp_in": v_w_mlp_in[0], "w_mlp_out": v_w_mlp_out[0]}

    shards16 = {n: s.astype(BF16) for n, s in shards.items()}
    early_shards = shards16["w_in"]
    late_shards = _pack_shards(shards16, LATE_WEIGHTS)

    core = lax.axis_index("c").astype(jnp.int32).reshape(1)
    chip = (2 * lax.axis_index("x") + lax.axis_index("y")).astype(jnp.int32).reshape(1)
    grad_x, (mixer_got, mixer_sums), (mlp_got, mlp_sums), small = _local_step(
        x[0], loss_target[0], early_shards, late_shards, norm_mix_g, b_gate, norm_mlp_g, norm_final_g, core)

    reduced = _join_halves(_sum_chips(mixer_got, mixer_sums, chip))
    reduced_mlp = _join_halves(_sum_chips(mlp_got, mlp_sums, chip))
    g_shard = {**_unpack_shard(reduced, MIXER_GROUP), **_unpack_shard(reduced_mlp, MLP_GROUP)}

    dg_mix, dbg, dg_mlp, dg_final, loss_part = small
    loss_row = jnp.sum(loss_part, axis=0, keepdims=True)
    small_pack = jnp.concatenate(
        [jnp.sum(dg_mix, axis=0, keepdims=True), jnp.sum(dbg, axis=0, keepdims=True),
         jnp.sum(dg_mlp, axis=0, keepdims=True), jnp.sum(dg_final, axis=0, keepdims=True), loss_row], axis=1)
    n_small = small_pack.shape[1]
    small_sum = _all_reduce_small(small_pack.reshape(n_small // 128, 128)).reshape(1, n_small)
    g_norm_mix = small_sum[:, :D_MODEL]
    g_b_gate = small_sum[:, D_MODEL:3 * D_MODEL]
    g_norm_mlp = small_sum[:, 3 * D_MODEL:4 * D_MODEL]
    g_norm_final = small_sum[:, 4 * D_MODEL:5 * D_MODEL]
    loss = jnp.sum(small_sum[:, 5 * D_MODEL:])

    names = ["norm_mix_g", "w_in", "b_gate", "w_up_dil", "w_up_sb", "w_out", "norm_mlp_g", "w_mlp_in", "w_mlp_out",
             "norm_final_g"]
    grads = dict(g_shard)
    grads.update(norm_mix_g=g_norm_mix, b_gate=g_b_gate, norm_mlp_g=g_norm_mlp, norm_final_g=g_norm_final)
    weights = dict(shards)
    weights.update(norm_mix_g=norm_mix_g, b_gate=b_gate, norm_mlp_g=norm_mlp_g, norm_final_g=norm_final_g.reshape(1, D_MODEL))
    ms = dict(moments_m)
    ms.update(norm_mix_g=m_norm_mix_g, b_gate=m_b_gate, norm_mlp_g=m_norm_mlp_g, norm_final_g=m_norm_final_g.reshape(1, D_MODEL))
    vs = dict(moments_v)
    vs.update(norm_mix_g=v_norm_mix_g, b_gate=v_b_gate, norm_mlp_g=v_norm_mlp_g, norm_final_g=v_norm_final_g.reshape(1, D_MODEL))

    out_shapes = {"norm_mix_g": norm_mix_g.shape, "w_in": w_in.shape, "b_gate": b_gate.shape, "w_up_dil": w_up_dil.shape,
                  "w_up_sb": w_up_sb.shape, "w_out": w_out.shape, "norm_mlp_g": norm_mlp_g.shape,
                  "w_mlp_in": w_mlp_in.shape, "w_mlp_out": w_mlp_out.shape, "norm_final_g": norm_final_g.shape}
    g_out, d_out, m_out, v_out = [], [], [], []
    for n in names:
        d, nm, nv = _adamw(grads[n], weights[n], ms[n], vs[n], "adamw_" + n)
        shape = out_shapes[n]
        g_out.append(grads[n].reshape(shape))
        d_out.append(d.reshape(shape))
        m_out.append(nm.reshape(shape))
        v_out.append(nv.reshape(shape))
    return (loss, grad_x.reshape(x.shape), *g_out, *d_out, *m_out, *v_out)
```

```python
import functools
import math

import jax
import jax.numpy as jnp
import numpy as np
from jax import lax
from jax.experimental import pallas as pl
from jax.experimental.pallas import tpu as pltpu

F32 = jnp.float32
BF16 = jnp.bfloat16
MESH = pl.DeviceIdType.MESH

D_MODEL = 1024
HEAD_DIM = 64
DIL_GROUPS = ((128, 1), (512, 4), (2048, 16))
DIL_HEADS = 4
DIL_W = 256
N_DIL_HEADS = 12
SB_HEADS = 8
SB_W = SB_HEADS * HEAD_DIM
QKV_W = 3 * 3 * DIL_W + 3 * SB_W
GATE_W = 2 * D_MODEL
IN_COLS = QKV_W + GATE_W
D_FF = 4 * D_MODEL
BLOCK = 128
RMS_EPS = 1e-6
NEG_INF = -1e30
N_CHIPS = 4
N_DEV = 8

ADAM_LR = 0.001
ADAM_B1 = 0.9
ADAM_B2 = 0.999
ADAM_EPS = 1e-08
ADAM_WD = 0.01
ADAM_STEP = 10

VMEM_LIMIT = 56 * 1024 * 1024

SB_BQ = 256
SB_BK = 256


def _cparams(sem=None):
    if sem is None:
        return pltpu.CompilerParams(vmem_limit_bytes=VMEM_LIMIT)
    return pltpu.CompilerParams(dimension_semantics=sem, vmem_limit_bytes=VMEM_LIMIT)


def _dot(a, b, dims):
    return lax.dot_general(a, b, (dims, ((), ())), preferred_element_type=F32)


def _dot_nn(a, b):
    return _dot(a, b, ((1,), (0,)))


def _dot_nt(a, b):
    return _dot(a, b, ((1,), (1,)))


def _dot_tn(a, b):
    return _dot(a, b, ((0,), (0,)))


def _dot_f32_by_01(x, m01, pieces=3):
    hi = x.astype(BF16)
    if pieces == 1:
        return _dot_nn(hi, m01)
    r1 = x - hi.astype(F32)
    mid = r1.astype(BF16)
    if pieces == 2:
        return _dot_nn(hi, m01) + _dot_nn(mid, m01)
    lo = (r1 - mid.astype(F32)).astype(BF16)
    return _dot_nn(hi, m01) + _dot_nn(mid, m01) + _dot_nn(lo, m01)


def _matmul(a, b, *, mode, out_dtypes, name, tm=1024, tn=1024, tk=1024, extras=(), epilogue=None, exchange=None,
            into=None):
    if mode == "nn":
        (m, k), (k2, n) = a.shape, b.shape
    elif mode == "nt":
        (m, k), (n, k2) = a.shape, b.shape
    else:
        (k, m), (k2, n) = a.shape, b.shape
    assert k == k2, (a.shape, b.shape, mode)
    tm, tn, tk = min(tm, m), min(tn, n), min(tk, k)
    assert m % tm == 0 and n % tn == 0 and k % tk == 0, (m, n, k, tm, tn, tk)
    nk = k // tk
    n_out = len(out_dtypes)
    n_ex = len(extras)

    if mode == "nn":
        a_spec = pl.BlockSpec((tm, tk), lambda i, j, kk: (i, kk))
        b_spec = pl.BlockSpec((tk, tn), lambda i, j, kk: (kk, j))
        dot = _dot_nn
    elif mode == "nt":
        a_spec = pl.BlockSpec((tm, tk), lambda i, j, kk: (i, kk))
        b_spec = pl.BlockSpec((tn, tk), lambda i, j, kk: (j, kk))
        dot = _dot_nt
    else:
        a_spec = pl.BlockSpec((tk, tm), lambda i, j, kk: (kk, i))
        b_spec = pl.BlockSpec((tk, tn), lambda i, j, kk: (kk, j))
        dot = _dot_tn
    mn_spec = pl.BlockSpec((tm, tn), lambda i, j, kk: (i, j))
    row_spec = pl.BlockSpec((1, tn), lambda i, j, kk: (0, j))
    part_spec = pl.BlockSpec((8, tn), lambda i, j, kk: (i, j))
    ex_specs = [row_spec if e.shape[0] == 1 else mn_spec for e in extras]
    is_part = [isinstance(dt, tuple) for dt in out_dtypes]
    out_dts = [dt[1] if p else dt for dt, p in zip(out_dtypes, is_part)]
    out_specs = [part_spec if p else mn_spec for p in is_part]
    out_shapes = [jax.ShapeDtypeStruct((8 * (m // tm), n) if p else (m, n), dt) for dt, p in zip(out_dts, is_part)]

    n_side = 0 if exchange is None else 1
    grid = (m // tm, n // tn, nk)
    prior = []
    if into is not None:
        assert n_out == 1 and not extras and exchange is None
        into_shape, into_map, into_prior = into
        out_specs = [pl.BlockSpec((1, tm, tn), lambda i, j, kk: into_map(i, j))]
        out_shapes = [jax.ShapeDtypeStruct(into_shape, out_dts[0])]
        prior = [] if into_prior is None else [into_prior]

    def body(*refs):
        a_ref, b_ref = refs[0], refs[1]
        ex_refs = refs[2:2 + n_ex]
        n_in = 2 + n_ex + n_side + len(prior)
        out_refs = refs[n_in:n_in + n_out]
        scratch = refs[n_in + n_out + n_side:]
        acc_ref = scratch[0] if nk > 1 else None
        if exchange is not None:
            side = (refs[2 + n_ex], refs[2 + n_ex + n_side + n_out]) + tuple(scratch[-2:])
            step = (pl.program_id(0) * grid[1] + pl.program_id(1)) * grid[2] + pl.program_id(2)

            @pl.when(step == 0)
            def _():
                _exchange_start(*side)

            @pl.when(step == grid[0] * grid[1] * grid[2] - 1)
            def _():
                _exchange_wait(*side)

        part = dot(a_ref[...].astype(BF16), b_ref[...].astype(BF16))

        def finish(acc):
            if epilogue is None:
                outs = (acc,)
            else:
                outs = epilogue(acc, *[r[...] for r in ex_refs])
            for o_ref, o in zip(out_refs, outs):
                if into is None:
                    o_ref[...] = o.astype(o_ref.dtype)
                else:
                    o_ref[0] = o.astype(o_ref.dtype)

        if nk == 1:
            finish(part)
        else:
            kk = pl.program_id(2)

            @pl.when(kk == 0)
            def _():
                acc_ref[...] = part

            @pl.when(kk > 0)
            def _():
                acc_ref[...] += part

            @pl.when(kk == nk - 1)
            def _():
                finish(acc_ref[...])

    side_in = [] if exchange is None else [exchange]
    outs = pl.pallas_call(
        body,
        name=name,
        grid=grid,
        in_specs=[a_spec, b_spec] + ex_specs + [ANY] * (n_side + len(prior)),
        out_specs=out_specs + [ANY] * n_side,
        out_shape=out_shapes + [jax.ShapeDtypeStruct(e.shape, e.dtype) for e in side_in],
        scratch_shapes=([pltpu.VMEM((tm, tn), F32)] if nk > 1 else [])
        + [pltpu.SemaphoreType.DMA((3,)), pltpu.SemaphoreType.DMA((3,))] * n_side,
        input_output_aliases={2: 0} if prior else {},
        compiler_params=_cparams(("arbitrary",) * 3 if n_side else ("parallel", "parallel", "arbitrary")),
    )(a, b, *extras, *side_in, *prior)
    return outs


ROW_TILE = 512


def _rows_sum8(t):
    rows, d = t.shape
    return jnp.sum(t.reshape(rows // 8, 8, d), axis=0)


def _rms_rows(x):
    r = lax.rsqrt(jnp.mean(x * x, axis=-1, keepdims=True) + RMS_EPS)
    return x * r, r


def _rms_bwd_rows(dh, x, g):
    xh, r = _rms_rows(x)
    dxh = dh * g
    return r * (dxh - xh * jnp.mean(dxh * xh, axis=-1, keepdims=True)), _rows_sum8(dh * xh)


def _rms_bwd_residual(dh, x, g, dres):
    s, d = x.shape

    def body(dh_ref, x_ref, g_ref, dres_ref, dx_ref, dg_ref):
        dx, dg = _rms_bwd_rows(dh_ref[...], x_ref[...], g_ref[...])
        dx_ref[...] = dres_ref[...] + dx
        dg_ref[...] = dg

    row = pl.BlockSpec((ROW_TILE, d), lambda i: (i, 0))
    return pl.pallas_call(
        body,
        name="norm_mix_bwd",
        grid=(s // ROW_TILE,),
        in_specs=[row, row, pl.BlockSpec((1, d), lambda i: (0, 0)), row],
        out_specs=[row, pl.BlockSpec((8, d), lambda i: (i, 0))],
        out_shape=[jax.ShapeDtypeStruct((s, d), F32), jax.ShapeDtypeStruct((8 * (s // ROW_TILE), d), F32)],
        compiler_params=_cparams(("parallel",)),
    )(dh, x, g, dres)


def _rms_fwd_and_gather(x, g, shard_pack):
    s, d = x.shape
    r_pack, w_pack = shard_pack.shape
    steps = s // ROW_TILE

    def body(x_ref, g_ref, pack_ref, h_ref, ht_ref, others_ref, send_sems, recv_sems):
        i = pl.program_id(0)
        gather = (pack_ref, others_ref, send_sems, recv_sems)

        @pl.when(i == 0)
        def _():
            _gather_start(*gather)

        hv = _rms_rows(x_ref[...])[0] * g_ref[...]
        h_ref[...] = hv.astype(BF16)
        ht_ref[...] = hv.T.astype(BF16)

        @pl.when(i == steps - 1)
        def _():
            _gather_pass_on(*gather)
            _gather_finish(*gather)

    h, ht, others = pl.pallas_call(
        body,
        name="norm_mix",
        grid=(steps,),
        in_specs=[pl.BlockSpec((ROW_TILE, d), lambda i: (i, 0)), pl.BlockSpec((1, d), lambda i: (0, 0)), ANY],
        out_specs=[pl.BlockSpec((ROW_TILE, d), lambda i: (i, 0)), pl.BlockSpec((d, ROW_TILE), lambda i: (0, i)), ANY],
        out_shape=[jax.ShapeDtypeStruct((s, d), BF16), jax.ShapeDtypeStruct((d, s), BF16),
                   jax.ShapeDtypeStruct((N_CHIPS, 2, r_pack // 2, w_pack), shard_pack.dtype)],
        scratch_shapes=[pltpu.SemaphoreType.DMA((6,)), pltpu.SemaphoreType.DMA((6,))],
        compiler_params=_cparams(("arbitrary",)),
    )(x, g, shard_pack.reshape(2, r_pack // 2, w_pack))
    return h, ht, _fill_own_slot(others, shard_pack)


def _alibi_slopes():
    return np.exp2(np.float32(-8.0) * np.arange(1, N_DIL_HEADS + 1, dtype=np.float32) / np.float32(N_DIL_HEADS))


def _head_lane_mask(h, rows):
    lane = lax.broadcasted_iota(jnp.int32, (rows, DIL_W), 1)
    return (lane >= h * HEAD_DIM) & (lane < (h + 1) * HEAD_DIM)


def _pair_lanes(h):
    return slice((h // 2) * 128, (h // 2 + 1) * 128)


def _only_head(t, h):
    part = t[:, _pair_lanes(h)]
    lane = lax.broadcasted_iota(jnp.int32, part.shape, 1)
    keep = (lane < HEAD_DIM) if h % 2 == 0 else (lane >= HEAD_DIM)
    return jnp.where(keep, part, jnp.zeros_like(part))


def _band_terms(dil, has_prev):
    qi = lax.broadcasted_iota(jnp.int32, (BLOCK, 2 * BLOCK), 0)
    kj = lax.broadcasted_iota(jnp.int32, (BLOCK, 2 * BLOCK), 1)
    steps = qi + BLOCK - kj
    valid = (steps >= 0) & (steps <= BLOCK) & ((kj >= BLOCK) | has_prev)
    return valid, steps.astype(F32) * float(dil)


def _load_halves(ref, rows):
    return jnp.concatenate([ref[0, rows, :], ref[1, rows, :]], axis=1)


def _store_halves(ref, rows, value):
    ref[0, rows, :] = value[:, :128]
    ref[1, rows, :] = value[:, 128:]


def _dil_fwd_all(qkv_groups):
    s = qkv_groups[0].shape[0]
    steps = s // BLOCK
    n_groups = len(DIL_GROUPS)
    dils = [d for _, d in DIL_GROUPS]
    views = [qkv_groups[g].reshape(s // dils[g], dils[g] * 3 * DIL_W) for g in range(n_groups)]
    slopes = _alibi_slopes()

    def spec(dil, which, prev):
        def index(t):
            n = t // dil
            return (jnp.maximum(n - 1, 0) if prev else n, (t % dil) * 3 + which)
        return pl.BlockSpec((BLOCK, DIL_W), index)

    def body(*refs):
        t = pl.program_id(0)
        ins, outs = refs[:5 * n_groups], refs[5 * n_groups:]
        masks = [_head_lane_mask(h, BLOCK) for h in range(DIL_HEADS)]
        work = []
        for g in range(n_groups):
            q_ref, kc_ref, kp_ref, vc_ref, vp_ref = ins[5 * g:5 * g + 5]
            q = q_ref[...]
            k2 = jnp.concatenate([kp_ref[...], kc_ref[...]], axis=0)
            v2 = jnp.concatenate([vp_ref[...], vc_ref[...]], axis=0)
            logits = [_dot_nt(_only_head(q, h), k2[:, _pair_lanes(h)]) for h in range(DIL_HEADS)]
            work.append((v2, logits))
        probs = []
        for g in range(n_groups):
            valid, dist = _band_terms(dils[g], t // dils[g] > 0)
            ps, lses = [], []
            for h in range(DIL_HEADS):
                slope = float(slopes[g * DIL_HEADS + h])
                lg = jnp.where(valid, work[g][1][h] * 0.125 - slope * dist, NEG_INF)
                mx = jnp.max(lg, axis=1, keepdims=True)
                lse = mx + jnp.log(jnp.sum(jnp.exp(lg - mx), axis=1, keepdims=True))
                ps.append(jnp.exp(lg - lse).astype(BF16))
                lses.append(lse)
            probs.append((ps, lses))
        for g in range(n_groups):
            dil = dils[g]
            mine = pl.ds(t % dil, BLOCK, stride=dil) if dil > 1 else slice(None)
            o_acc = jnp.zeros((BLOCK, DIL_W), F32)
            lse_acc = jnp.zeros((BLOCK, DIL_W), F32)
            for h in range(DIL_HEADS):
                o_acc = jnp.where(masks[h], _dot_nn(probs[g][0][h], work[g][0]), o_acc)
                lse_acc = jnp.where(masks[h], probs[g][1][h], lse_acc)
            _store_halves(outs[2 * g], mine, o_acc)
            _store_halves(outs[2 * g + 1], mine, lse_acc)

    in_specs, out_specs, operands = [], [], []
    for g, dil in enumerate(dils):
        in_specs += [spec(dil, 0, False), spec(dil, 1, False), spec(dil, 1, True), spec(dil, 2, False),
                     spec(dil, 2, True)]
        out_specs += [pl.BlockSpec((2, BLOCK * dil, 128), functools.partial(lambda d, t: (0, t // d, 0), dil))] * 2
        operands += [views[g]] * 5
    res = pl.pallas_call(
        body,
        name="dil_fwd",
        grid=(steps,),
        in_specs=in_specs,
        out_specs=out_specs,
        out_shape=[jax.ShapeDtypeStruct((2, s, 128), F32)] * (2 * n_groups),
        compiler_params=_cparams(("arbitrary",)),
    )(*operands)
    return [(res[2 * g], res[2 * g + 1]) for g in range(n_groups)]


def _dil_bwd_all(qkv_groups, dos, lses, cterms):
    s = qkv_groups[0].shape[0]
    steps = s // BLOCK
    n_groups = len(DIL_GROUPS)
    dils = [d for _, d in DIL_GROUPS]
    views = [qkv_groups[g].reshape(s // dils[g], dils[g] * 3 * DIL_W) for g in range(n_groups)]
    slopes = _alibi_slopes()
    n_in = 12

    def block_of(dil, shift):
        nb = steps // dil

        def index(t):
            n = t // dil
            if shift < 0:
                return jnp.maximum(n - 1, 0)
            return n if shift == 0 else jnp.minimum(n + 1, nb - 1)
        return index

    def col(dil, which, shift):
        index = block_of(dil, shift)
        return pl.BlockSpec((BLOCK, DIL_W), lambda t: (index(t), (t % dil) * 3 + which))

    def own(dil, shift):
        index = block_of(dil, shift)
        return pl.BlockSpec((2, BLOCK * dil, 128), lambda t: (0, index(t), 0))

    def head_col(v, hm):
        return jnp.max(jnp.where(hm, v, NEG_INF), axis=1, keepdims=True)

    def body(*refs):
        t = pl.program_id(0)
        masks = [_head_lane_mask(h, BLOCK) for h in range(DIL_HEADS)]
        work = []
        for g, dil in enumerate(dils):
            (q_ref, qn_ref, kc_ref, kp_ref, vc_ref, vp_ref, do_ref, don_ref, lse_ref, lsen_ref, c_ref,
             cn_ref) = refs[n_in * g:n_in * (g + 1)]
            mine = pl.ds(t % dil, BLOCK, stride=dil) if dil > 1 else slice(None)
            q, qn = q_ref[...], qn_ref[...]
            kc, vc = kc_ref[...], vc_ref[...]
            k2 = jnp.concatenate([kp_ref[...], kc], axis=0)
            v2 = jnp.concatenate([vp_ref[...], vc], axis=0)
            dov, donv = _load_halves(do_ref, mine), _load_halves(don_ref, mine)
            side = (_load_halves(lse_ref, mine), _load_halves(lsen_ref, mine),
                    _load_halves(c_ref, mine), _load_halves(cn_ref, mine))
            heads = range(DIL_HEADS)
            qhs = [_only_head(q, h) for h in heads]
            qnhs = [_only_head(qn, h) for h in heads]
            dohs = [_only_head(dov, h).astype(BF16) for h in heads]
            donhs = [_only_head(donv, h).astype(BF16) for h in heads]
            prods = ([_dot_nt(qhs[h], k2[:, _pair_lanes(h)]) for h in heads],
                     [_dot_nt(dohs[h], v2[:, _pair_lanes(h)]) for h in heads],
                     [_dot_nt(qnhs[h], kc[:, _pair_lanes(h)]) for h in heads],
                     [_dot_nt(donhs[h], vc[:, _pair_lanes(h)]) for h in heads])
            work.append((k2, qhs, qnhs, dohs, donhs, side, prods))
        grads = []
        for g, dil in enumerate(dils):
            n, nb = t // dil, steps // dil
            valid, dist = _band_terms(dil, n > 0)
            valid_n = _band_terms(dil, True)[0][:, :BLOCK] & (n < nb - 1)
            dist_n = dist[:, :BLOCK]
            (lsev, lsenv, cv, cnv), (logit, dp, logit_n, dp_n) = work[g][5], work[g][6]
            p16, dlog, pn16, dlog_n = [], [], [], []
            for h in range(DIL_HEADS):
                hm, slope = masks[h], float(slopes[g * DIL_HEADS + h])
                p = jnp.where(valid, jnp.exp(logit[h] * 0.125 - slope * dist - head_col(lsev, hm)), 0.0)
                dlog.append((p * (dp[h] + head_col(cv, hm)) * 0.125).astype(BF16))
                p16.append(p.astype(BF16))
                pn = jnp.where(valid_n, jnp.exp(logit_n[h] * 0.125 - slope * dist_n - head_col(lsenv, hm)), 0.0)
                dlog_n.append((pn * (dp_n[h] + head_col(cnv, hm)) * 0.125).astype(BF16))
                pn16.append(pn.astype(BF16))
            grads.append((p16, dlog, pn16, dlog_n))
        for g in range(n_groups):
            k2, qhs, qnhs, dohs, donhs = work[g][:5]
            p16, dlog, pn16, dlog_n = grads[g]
            dq_acc = jnp.zeros((BLOCK, DIL_W), F32)
            dk_pairs = [jnp.zeros((BLOCK, 128), F32)] * 2
            dv_pairs = [jnp.zeros((BLOCK, 128), F32)] * 2
            for h in range(DIL_HEADS):
                dq_acc = jnp.where(masks[h], _dot_nn(dlog[h], k2), dq_acc)
                dk_pairs[h // 2] = dk_pairs[h // 2] + _dot_tn(dlog[h][:, BLOCK:], qhs[h]) + _dot_tn(dlog_n[h], qnhs[h])
                dv_pairs[h // 2] = dv_pairs[h // 2] + _dot_tn(p16[h][:, BLOCK:], dohs[h]) + _dot_tn(pn16[h], donhs[h])
            dq_ref, dk_ref, dv_ref = refs[n_in * n_groups + 3 * g:n_in * n_groups + 3 * g + 3]
            dq_ref[...] = dq_acc.astype(BF16)
            dk_ref[...] = jnp.concatenate(dk_pairs, axis=1).astype(BF16)
            dv_ref[...] = jnp.concatenate(dv_pairs, axis=1).astype(BF16)

    in_specs, out_specs, out_shape, operands = [], [], [], []
    for g, dil in enumerate(dils):
        in_specs += [col(dil, 0, 0), col(dil, 0, 1), col(dil, 1, 0), col(dil, 1, -1), col(dil, 2, 0), col(dil, 2, -1),
                     own(dil, 0), own(dil, 1), own(dil, 0), own(dil, 1), own(dil, 0), own(dil, 1)]
        operands += [views[g]] * 6 + [dos[g], dos[g], lses[g], lses[g], cterms[g], cterms[g]]
        out_specs += [pl.BlockSpec((BLOCK, DIL_W), functools.partial(lambda d, t: (t // d, t % d), dil))] * 3
        out_shape += [jax.ShapeDtypeStruct((s // dil, dil * DIL_W), BF16)] * 3
    outs = pl.pallas_call(
        body,
        name="dil_bwd",
        grid=(steps,),
        in_specs=in_specs,
        out_specs=out_specs,
        out_shape=out_shape,
        compiler_params=_cparams(("parallel",)),
    )(*operands)
    return [tuple(v.reshape(s, DIL_W) for v in outs[3 * g:3 * g + 3]) for g in range(n_groups)]


SB_PAIRS = SB_HEADS // 2
SB_COL0 = 0
LOG2E = 1.4426950408889634


SB_EXP_CLAMP = 64.0


def _sb_softplus2(zs):
    t = 1.0 + jnp.exp2(jnp.minimum(zs, SB_EXP_CLAMP))
    return jnp.maximum(jnp.log(t) * LOG2E, zs)


def _sb_consts(nkb):
    row = lax.broadcasted_iota(jnp.int32, (SB_BQ, SB_BK), 0)
    colk = lax.broadcasted_iota(jnp.int32, (SB_BQ, SB_BK), 1)
    rr = lax.broadcasted_iota(jnp.int32, (SB_BK, SB_BK), 0)
    cc = lax.broadcasted_iota(jnp.int32, (SB_BK, SB_BK), 1)
    lane = lax.broadcasted_iota(jnp.int32, (SB_BQ, 128), 1)
    assert 2 * nkb <= 128
    return colk < row, rr, cc, lane < HEAD_DIM, lane


def _split_heads(t):
    first = lax.broadcasted_iota(jnp.int32, t.shape, 1) < HEAD_DIM
    zero = jnp.zeros_like(t)
    return jnp.where(first, t, zero), jnp.where(first, zero, t)


def _sb_fwd(qkv, shard_pack):
    s = qkv.shape[0]
    nq, nkb = s // SB_BQ, s // SB_BK
    zscale = LOG2E / math.sqrt(HEAD_DIM)
    r_pack, w_pack = shard_pack.shape

    def body(q_ref, k_ref, v_ref, pack_ref, o_ref, a_row, others_ref, zs_scr, a_scr, acc_scr, cl_scr,
             send_sems, recv_sems):
        i = pl.program_id(1)
        pair = pl.program_id(0)
        gather = (pack_ref, others_ref, send_sems, recv_sems)

        @pl.when((pair == 0) & (i == 0))
        def _():
            _gather_start(*gather)

        @pl.when((pair == 1) & (i == 0))
        def _():
            _gather_pass_on(*gather)

        @pl.when((pair == SB_PAIRS - 1) & (i == nq - 1))
        def _():
            _gather_finish(*gather)

        causal, rr, cc, _, _ = _sb_consts(nkb)
        later = (rr > cc).astype(BF16)
        qh = _split_heads(q_ref[...])

        def rows(j):
            return pl.ds(pl.multiple_of(j * SB_BK, SB_BK), SB_BK)

        def scores_to(slot, j):
            kb = k_ref[rows(j), :]
            for hh in range(2):
                zs_scr[slot, hh] = _dot_nt(qh[hh], kb) * zscale

        def weights(slot, j, masked):
            xs, sums, sufs = [], [], []
            for hh in range(2):
                zs = zs_scr[slot, hh]
                sp = _sb_softplus2(zs)
                if masked:
                    sp = jnp.where(causal, sp, 0.0)
                xs.append(zs - sp)
                sums.append(jnp.sum(sp, axis=1, keepdims=True))
                sufs.append(_dot_f32_by_01(sp, later, 2))
            for hh in range(2):
                cl = cl_scr[hh]
                a = jnp.exp2(xs[hh] - (sufs[hh] + jnp.concatenate([cl, cl], axis=1)))
                if masked:
                    a = jnp.where(causal, a, 0.0)
                a16 = a.astype(BF16)
                a_scr[slot, :, hh * SB_BK:(hh + 1) * SB_BK] = a16
                a_row[0, 0, j, :, hh * SB_BK:(hh + 1) * SB_BK] = a16
                cl_scr[hh] = cl + sums[hh]

        def add_av(slot, j):
            v0, v1 = _split_heads(v_ref[rows(j), :])
            acc_scr[...] += _dot_nn(a_scr[slot], jnp.concatenate([v0, v1], axis=0))

        acc_scr[...] = jnp.zeros_like(acc_scr)
        cl_scr[...] = jnp.zeros_like(cl_scr)
        scores_to(0, i)
        scores_to(1, jnp.maximum(i - 1, 0))
        weights(0, i, True)

        def step(j, prev, cur):
            scores_to(prev, jnp.maximum(j - 1, 0))
            add_av(prev, j + 1)
            weights(cur, j, False)

        def two_steps(u, _):
            j = i - 1 - 2 * u
            step(j, 0, 1)
            step(j - 1, 1, 0)
            return 0

        lax.fori_loop(0, i // 2, two_steps, 0)

        @pl.when(i % 2 == 1)
        def _():
            step(0, 0, 1)
            add_av(1, 0)

        @pl.when(i % 2 == 0)
        def _():
            add_av(0, 0)

        o_ref[...] = acc_scr[...]

    def full(which):
        return pl.BlockSpec((s, 128), lambda p, i: (0, SB_COL0 + 4 * which + p))

    return pl.pallas_call(
        body,
        name="sb_fwd",
        grid=(SB_PAIRS, nq),
        in_specs=[pl.BlockSpec((SB_BQ, 128), lambda p, i: (i, SB_COL0 + p)), full(1), full(2), ANY],
        out_specs=[pl.BlockSpec((SB_BQ, 128), lambda p, i: (i, p)),
                   pl.BlockSpec((1, 1, nkb, SB_BQ, 2 * SB_BK), lambda p, i: (p, i, 0, 0, 0)), ANY],
        out_shape=[jax.ShapeDtypeStruct((s, SB_W), F32),
                   jax.ShapeDtypeStruct((SB_PAIRS, nq, nkb, SB_BQ, 2 * SB_BK), BF16),
                   jax.ShapeDtypeStruct((N_CHIPS, 2, r_pack // 2, w_pack), shard_pack.dtype)],
        scratch_shapes=[pltpu.VMEM((2, 2, SB_BQ, SB_BK), F32), pltpu.VMEM((2, SB_BQ, 2 * SB_BK), BF16),
                        pltpu.VMEM((SB_BQ, 128), F32), pltpu.VMEM((2, SB_BQ, 128), F32),
                        pltpu.SemaphoreType.DMA((6,)), pltpu.SemaphoreType.DMA((6,))],
        compiler_params=_cparams(("arbitrary", "arbitrary")),
    )(qkv, qkv, qkv, shard_pack.reshape(2, r_pack // 2, w_pack))


def _sb_bwd(qkv, do, a_hbm, chip_sums):
    s = qkv.shape[0]
    nq, nkb = s // SB_BQ, s // SB_BK
    scale = 1.0 / math.sqrt(HEAD_DIM)
    zscale = LOG2E * scale

    def body(q_ref, k_ref, v_ref, do_ref, a_row, sums_ref, dq_ref, dk_ref, dv_ref, got_ref,
             zs_scr, da_scr, dz_scr, a_scr, cg_scr, send_sems, recv_sems):
        i = pl.program_id(1)
        pair = pl.program_id(0)
        first_step = (pair == 0) & (i == 0)
        last_step = (pair == SB_PAIRS - 1) & (i == nq - 1)

        @pl.when(first_step)
        def _():
            _exchange_start(sums_ref, got_ref, send_sems, recv_sems)

        @pl.when(i == 0)
        def _():
            dk_ref[...] = jnp.zeros_like(dk_ref)
            dv_ref[...] = jnp.zeros_like(dv_ref)

        causal, rr, cc, first, _ = _sb_consts(nkb)
        earlier = (rr < cc).astype(BF16)
        q2 = q_ref[...]
        qh = _split_heads(q2)
        do2 = do_ref[...].astype(BF16)
        doh = _split_heads(do2)

        def rows(j):
            return pl.ds(pl.multiple_of(j * SB_BK, SB_BK), SB_BK)

        def products_to(slot, j):
            kb, vb = k_ref[rows(j), :], v_ref[rows(j), :]
            for hh in range(2):
                zs_scr[slot, hh] = _dot_nt(qh[hh], kb) * (-zscale)
                da_scr[slot, hh] = _dot_nt(doh[hh], vb)

        head0_rows = lax.broadcasted_iota(jnp.int32, (128, SB_BK), 0) < HEAD_DIM

        def by_head(t):
            return jnp.where(head0_rows, t[:, :SB_BK], t[:, SB_BK:])

        def apply(slot, j):
            k0, k1 = _split_heads(k_ref[rows(j), :])
            dq_ref[...] += _dot_nn(dz_scr[slot], jnp.concatenate([k0, k1], axis=0)) * scale
            dk_ref[0, j] += by_head(_dot_tn(q2, dz_scr[slot])) * scale
            dv_ref[0, j] += by_head(_dot_tn(do2, a_scr[slot]))

        def grads(slot, j, masked):
            gs, gpres = [], []
            for hh in range(2):
                a16 = a_row[0, 0, j, :, hh * SB_BK:(hh + 1) * SB_BK]
                a_scr[slot, :, hh * SB_BK:(hh + 1) * SB_BK] = a16
                g = a16.astype(F32) * da_scr[slot, hh]
                gs.append(g)
                gpres.append(_dot_f32_by_01(g, earlier, 1))
            sigs = []
            for hh in range(2):
                e = jnp.exp2(jnp.minimum(zs_scr[slot, hh], SB_EXP_CLAMP))
                sigs.append(pl.reciprocal(1.0 + e, approx=True))
            for hh in range(2):
                cg = cg_scr[hh]
                dz = gs[hh] - (gs[hh] + (gpres[hh] + jnp.concatenate([cg, cg], axis=1))) * sigs[hh]
                if masked:
                    dz = jnp.where(causal, dz, 0.0)
                dz_scr[slot, :, hh * SB_BK:(hh + 1) * SB_BK] = dz.astype(BF16)
                cg_scr[hh] = cg + jnp.sum(gs[hh], axis=1, keepdims=True)

        dq_ref[...] = jnp.zeros_like(dq_ref)
        cg_scr[...] = jnp.zeros_like(cg_scr)
        dz_scr[1] = jnp.zeros((SB_BQ, 2 * SB_BK), BF16)
        a_scr[1] = jnp.zeros((SB_BQ, 2 * SB_BK), BF16)
        products_to(0, 0)

        def step(j, cur, nxt):
            products_to(nxt, j + 1)
            apply(nxt, jnp.maximum(j - 1, 0))
            grads(cur, j, False)

        def two_steps(u, _):
            step(2 * u, 0, 1)
            step(2 * u + 1, 1, 0)
            return 0

        lax.fori_loop(0, i // 2, two_steps, 0)

        def last(cur, nxt):
            apply(nxt, jnp.maximum(i - 1, 0))
            grads(cur, i, True)
            apply(cur, i)

        @pl.when(i % 2 == 1)
        def _():
            step(i - 1, 0, 1)
            last(1, 0)

        @pl.when(i % 2 == 0)
        def _():
            last(0, 1)

        @pl.when(last_step)
        def _():
            _exchange_wait(sums_ref, got_ref, send_sems, recv_sems)

    def full(which):
        return pl.BlockSpec((s, 128), lambda p, i: (0, SB_COL0 + 4 * which + p))

    qblk = pl.BlockSpec((SB_BQ, 128), lambda p, i: (i, p))
    acc = pl.BlockSpec((1, nkb, 128, SB_BK), lambda p, i: (p, 0, 0, 0))
    acc_shape = jax.ShapeDtypeStruct((SB_PAIRS, nkb, 128, SB_BK), F32)
    dq, dk_t, dv_t, got = pl.pallas_call(
        body,
        name="sb_bwd",
        grid=(SB_PAIRS, nq),
        in_specs=[pl.BlockSpec((SB_BQ, 128), lambda p, i: (i, SB_COL0 + p)), full(1), full(2), qblk,
                  pl.BlockSpec((1, 1, nkb, SB_BQ, 2 * SB_BK), lambda p, i: (p, i, 0, 0, 0)), ANY],
        out_specs=[qblk, acc, acc, ANY],
        out_shape=[jax.ShapeDtypeStruct((s, SB_W), F32), acc_shape, acc_shape,
                   jax.ShapeDtypeStruct(chip_sums.shape, chip_sums.dtype)],
        scratch_shapes=[pltpu.VMEM((2, 2, SB_BQ, SB_BK), F32), pltpu.VMEM((2, 2, SB_BQ, SB_BK), F32),
                        pltpu.VMEM((2, SB_BQ, 2 * SB_BK), BF16), pltpu.VMEM((2, SB_BQ, 2 * SB_BK), BF16),
                        pltpu.VMEM((2, SB_BQ, 128), F32),
                        pltpu.SemaphoreType.DMA((3,)), pltpu.SemaphoreType.DMA((3,))],
        compiler_params=_cparams(("arbitrary", "arbitrary")),
    )(qkv, qkv, qkv, do, a_hbm, chip_sums)

    def untranspose(t):
        return jnp.transpose(t, (1, 3, 0, 2)).reshape(s, SB_W)

    return dq, untranspose(dk_t), untranspose(dv_t), got


MERGE_TILE = 512


def _group_mix(lses):
    mx = jnp.maximum(jnp.maximum(lses[0], lses[1]), lses[2])
    es = [jnp.exp(t - mx) for t in lses]
    den = es[0] + es[1] + es[2]
    return [e / den for e in es]


def _merge_fwd(o_groups, lse_groups, o_sb, gl, b_gate, w_up_dil, w_up_sb):
    s = gl.shape[0]
    t = MERGE_TILE

    def body(o0, o1, o2, l0, l1, l2, ob_ref, gl_ref, bg_ref, wd_ref, ws_ref, merged_ref, oa_ref):
        rows = slice(None)
        w = _group_mix([_load_halves(l, rows) for l in (l0, l1, l2)])
        og = [_load_halves(o, rows) for o in (o0, o1, o2)]
        oa = (w[0] * og[0] + w[1] * og[1] + w[2] * og[2]).astype(BF16)
        ua = _dot_nn(oa, wd_ref[...])
        ub = _dot_nn(ob_ref[...].astype(BF16), ws_ref[...])
        gate = jax.nn.sigmoid(gl_ref[...] + bg_ref[...])
        merged_ref[...] = (gate[:, :D_MODEL] * ua + gate[:, D_MODEL:] * ub).astype(BF16)
        oa_ref[...] = oa

    dil = pl.BlockSpec((t, DIL_W), lambda i: (i, 0))
    halves = pl.BlockSpec((2, t, 128), lambda i: (0, i, 0))
    const = lambda shape: pl.BlockSpec(shape, lambda i: (0, 0))
    return pl.pallas_call(
        body,
        name="merge_fwd",
        grid=(s // t,),
        in_specs=[halves] * 6 + [pl.BlockSpec((t, SB_W), lambda i: (i, 0)), pl.BlockSpec((t, GATE_W), lambda i: (i, 0)),
                                 const((1, GATE_W)), const((DIL_W, D_MODEL)), const((SB_W, D_MODEL))],
        out_specs=[pl.BlockSpec((t, D_MODEL), lambda i: (i, 0)), dil],
        out_shape=[jax.ShapeDtypeStruct((s, D_MODEL), BF16), jax.ShapeDtypeStruct((s, DIL_W), BF16)],
        compiler_params=_cparams(("parallel",)),
    )(*o_groups, *lse_groups, o_sb, gl, b_gate, w_up_dil, w_up_sb)


def _merge_bwd(dmerged, o_groups, lse_groups, o_sb, gl, b_gate, w_up_dil, w_up_sb, swap):
    s = gl.shape[0]
    t = MERGE_TILE
    n_chunks, r_swap, w_swap = swap.shape
    swap = swap.reshape(n_chunks, 2, r_swap // 2, w_swap)

    def body(dm_ref, o0, o1, o2, l0, l1, l2, ob_ref, gl_ref, bg_ref, wd_ref, ws_ref, swap_ref,
             dua_ref, dub_ref, dgl_ref, dbg_ref, dosb_ref, d0, d1, d2, c0, c1, c2, got_ref, send_sem, recv_sem):
        i = pl.program_id(0)

        @pl.when(i == 0)
        def _():
            _swap_copy(swap_ref, got_ref, send_sem, recv_sem).start()

        @pl.when(i == pl.num_programs(0) - 1)
        def _():
            _swap_copy(swap_ref, got_ref, send_sem, recv_sem).wait()

        rows = slice(None)
        og = [_load_halves(o, rows) for o in (o0, o1, o2)]
        w = _group_mix([_load_halves(l, rows) for l in (l0, l1, l2)])
        oa = (w[0] * og[0] + w[1] * og[1] + w[2] * og[2]).astype(BF16)
        ua = _dot_nn(oa, wd_ref[...])
        ub = _dot_nn(ob_ref[...].astype(BF16), ws_ref[...])
        gate = jax.nn.sigmoid(gl_ref[...] + bg_ref[...])
        ga, gb = gate[:, :D_MODEL], gate[:, D_MODEL:]
        dm = dm_ref[...]
        dua = (dm * ga).astype(BF16)
        dub = (dm * gb).astype(BF16)
        dua_ref[...] = dua
        dub_ref[...] = dub
        dgl_a = dm * ua * ga * (1.0 - ga)
        dgl_b = dm * ub * gb * (1.0 - gb)
        dgl_ref[:, :D_MODEL] = dgl_a.astype(BF16)
        dgl_ref[:, D_MODEL:] = dgl_b.astype(BF16)
        part = jnp.concatenate([jnp.sum(dgl_a.reshape(t // 8, 8, D_MODEL), axis=0),
                                jnp.sum(dgl_b.reshape(t // 8, 8, D_MODEL), axis=0)], axis=1)

        @pl.when(i == 0)
        def _():
            dbg_ref[...] = part

        @pl.when(i > 0)
        def _():
            dbg_ref[...] += part

        dosb_ref[...] = _dot_nt(dub, ws_ref[...])
        doa = _dot_nt(dua, wd_ref[...])
        rr = lax.broadcasted_iota(jnp.int32, (DIL_W, DIL_W), 0) // HEAD_DIM
        cc = lax.broadcasted_iota(jnp.int32, (DIL_W, DIL_W), 1) // HEAD_DIM
        same_head = (rr == cc).astype(BF16)
        dw = [_dot_f32_by_01(doa * og[g], same_head, 2) for g in range(3)]
        mean_dw = w[0] * dw[0] + w[1] * dw[1] + w[2] * dw[2]
        for g, (d_ref, c_ref) in enumerate(((d0, c0), (d1, c1), (d2, c2))):
            _store_halves(d_ref, rows, w[g] * doa)
            _store_halves(c_ref, rows, -w[g] * mean_dw)

    dil = pl.BlockSpec((2, t, 128), lambda i: (0, i, 0))
    wide = pl.BlockSpec((t, D_MODEL), lambda i: (i, 0))
    gate2 = pl.BlockSpec((t, GATE_W), lambda i: (i, 0))
    sbw = pl.BlockSpec((t, SB_W), lambda i: (i, 0))
    const = lambda shape: pl.BlockSpec(shape, lambda i: (0, 0))
    return pl.pallas_call(
        body,
        name="merge_bwd",
        grid=(s // t,),
        in_specs=[wide] + [dil] * 6 + [sbw, gate2, const((1, GATE_W)), const((DIL_W, D_MODEL)), const((SB_W, D_MODEL)),
                                       ANY],
        out_specs=[wide, wide, gate2, const((8, GATE_W)), sbw] + [dil] * 6 + [ANY],
        out_shape=[jax.ShapeDtypeStruct((s, D_MODEL), BF16), jax.ShapeDtypeStruct((s, D_MODEL), BF16),
                   jax.ShapeDtypeStruct((s, GATE_W), BF16), jax.ShapeDtypeStruct((8, GATE_W), F32),
                   jax.ShapeDtypeStruct((s, SB_W), F32)] + [jax.ShapeDtypeStruct((2, s, 128), F32)] * 6
        + [jax.ShapeDtypeStruct((n_chunks, r_swap // 2, w_swap), swap.dtype)],
        scratch_shapes=[pltpu.SemaphoreType.DMA, pltpu.SemaphoreType.DMA],
        compiler_params=_cparams(("arbitrary",)),
    )(dmerged, *o_groups, *lse_groups, o_sb, gl, b_gate, w_up_dil, w_up_sb, swap)


ANY = pl.BlockSpec(memory_space=pl.ANY)


def _place():
    x, y, c = lax.axis_index("x"), lax.axis_index("y"), lax.axis_index("c")
    other_chips = [(1 - x, y), (x, 1 - y), (1 - x, 1 - y)]
    return x, y, c, other_chips


def _gather_copies(p_ref, out_ref, send_sems, recv_sems):
    x, y, c, chips = _place()
    me, sibling = 2 * x + y, (x, y, 1 - c)
    idx = [2 * chip[0] + chip[1] for chip in chips]

    def copy(k, chip_idx, core, to, src=None):
        return pltpu.make_async_remote_copy(
            src_ref=out_ref.at[chip_idx, core] if src is None else src, dst_ref=out_ref.at[chip_idx, core],
            send_sem=send_sems.at[k], recv_sem=recv_sems.at[k], device_id=to, device_id_type=MESH)

    first = lambda j: copy(j, me, c, (*chips[j], c), src=p_ref.at[c])
    landed = lambda j: copy(j, idx[j], c, (x, y, c))
    passed = lambda j: copy(3 + j, idx[j], c, sibling)
    handed = lambda j: copy(3 + j, idx[j], 1 - c, (x, y, c))
    return first, landed, passed, handed


def _gather_start(*refs):
    first = _gather_copies(*refs)[0]
    for j in range(3):
        first(j).start()


def _gather_pass_on(*refs):
    _, landed, passed, _ = _gather_copies(*refs)
    for j in range(3):
        landed(j).wait_recv()
        passed(j).start()


def _gather_finish(*refs):
    first, _, passed, handed = _gather_copies(*refs)
    for j in range(3):
        handed(j).wait_recv()
    for j in range(3):
        first(j).wait_send()
        passed(j).wait_send()


def _fill_own_slot(others, pack):
    n, _, rh, wd = others.shape
    me = 2 * lax.axis_index("x") + lax.axis_index("y")
    mine = lax.broadcasted_iota(jnp.int32, (n, 1, 1, 1), 0) == me
    return jnp.where(mine, pack.reshape(1, 2, rh, wd), others).reshape(n, 2 * rh, wd)


def _swap_copy(g_ref, out_ref, send_sem, recv_sem):
    x, y, c, _ = _place()
    return pltpu.make_async_remote_copy(
        src_ref=g_ref.at[:, 1 - c], dst_ref=out_ref,
        send_sem=send_sem, recv_sem=recv_sem, device_id=(x, y, 1 - c), device_id_type=MESH)


def _swap_halves(g):
    n, r, wd = g.shape
    rh = r // 2
    g = g.reshape(n, 2, rh, wd)

    def body(g_ref, out_ref, send_sem, recv_sem):
        cp = _swap_copy(g_ref, out_ref, send_sem, recv_sem)
        cp.start()
        cp.wait()

    return pl.pallas_call(
        body,
        name="grad_swap_halves",
        in_specs=[ANY],
        out_specs=ANY,
        out_shape=jax.ShapeDtypeStruct((n, rh, wd), g.dtype),
        scratch_shapes=[pltpu.SemaphoreType.DMA, pltpu.SemaphoreType.DMA],
    )(g)


def _add_halves(g, got, core):
    n, r, wd = g.shape
    rh = r // 2
    t = rh // 4
    nt = rh // t

    def body(c_ref, a_ref, b_ref, o_ref):
        o_ref[...] = (a_ref[0] + b_ref[...]).astype(BF16)

    grid_spec = pltpu.PrefetchScalarGridSpec(
        num_scalar_prefetch=1,
        grid=(n, nt),
        in_specs=[pl.BlockSpec((1, 1, t, wd), lambda s, i, c: (s, c[0], i, 0)),
                  pl.BlockSpec((1, t, wd), lambda s, i, c: (s, i, 0))],
        out_specs=pl.BlockSpec((1, t, wd), lambda s, i, c: (s, i, 0)),
    )
    return pl.pallas_call(
        body,
        name="grad_add_halves",
        grid_spec=grid_spec,
        out_shape=jax.ShapeDtypeStruct((n, rh, wd), BF16),
        compiler_params=_cparams(("parallel", "parallel")),
    )(core, g.reshape(n, 2, rh, wd), got)


def _exchange_copies(h_ref, out_ref, send_sems, recv_sems):
    x, y, c, chips = _place()
    me = 2 * x + y

    def copy(j, slot):
        them = 2 * chips[j][0] + chips[j][1]
        return pltpu.make_async_remote_copy(
            src_ref=h_ref.at[them], dst_ref=out_ref.at[me if slot == "mine" else them],
            send_sem=send_sems.at[j], recv_sem=recv_sems.at[j], device_id=(*chips[j], c), device_id_type=MESH)

    return (lambda j: copy(j, "mine")), (lambda j: copy(j, "theirs"))


def _exchange_start(h_ref, out_ref, send_sems, recv_sems):
    send = _exchange_copies(h_ref, out_ref, send_sems, recv_sems)[0]
    for j in range(3):
        send(j).start()


def _exchange_wait(h_ref, out_ref, send_sems, recv_sems):
    send, arrival = _exchange_copies(h_ref, out_ref, send_sems, recv_sems)
    for j in range(3):
        arrival(j).wait_recv()
    for j in range(3):
        send(j).wait_send()


def _sum_chips(b, h, chip):
    n, rh, wd = b.shape
    t = rh // 4

    def body(chip_ref, b_ref, own_ref, o_ref):
        own = own_ref[0]
        s0, s1, s2, s3 = (jnp.where(chip_ref[0] == k, own, b_ref[k]).astype(F32) for k in range(n))
        o_ref[...] = ((s0 + s1) + s2) + s3

    grid_spec = pltpu.PrefetchScalarGridSpec(
        num_scalar_prefetch=1,
        grid=(rh // t,),
        in_specs=[pl.BlockSpec((n, t, wd), lambda i, chip: (0, i, 0)),
                  pl.BlockSpec((1, t, wd), lambda i, chip: (chip[0], i, 0))],
        out_specs=pl.BlockSpec((t, wd), lambda i, chip: (i, 0)),
    )
    return pl.pallas_call(
        body,
        name="grad_sum_chips",
        grid_spec=grid_spec,
        out_shape=jax.ShapeDtypeStruct((rh, wd), F32),
        compiler_params=_cparams(("parallel",)),
    )(chip, b, h)


def _join_halves(tc):
    rh, wd = tc.shape

    def body(t_ref, out_ref, send_sem, recv_sem):
        x, y, c, _ = _place()
        cp = pltpu.make_async_remote_copy(
            src_ref=t_ref, dst_ref=out_ref.at[c],
            send_sem=send_sem, recv_sem=recv_sem, device_id=(x, y, 1 - c), device_id_type=MESH)
        cp.start()
        cp.wait()

    halves = pl.pallas_call(
        body,
        name="grad_join_halves",
        in_specs=[ANY],
        out_specs=ANY,
        out_shape=jax.ShapeDtypeStruct((2, rh, wd), tc.dtype),
        scratch_shapes=[pltpu.SemaphoreType.DMA, pltpu.SemaphoreType.DMA],
    )(tc)
    return lax.dynamic_update_slice(halves, tc[None], (lax.axis_index("c"), 0, 0)).reshape(2 * rh, wd)


def _all_reduce_small(pack):
    rows, lanes = pack.shape

    def body(p_ref, out_ref, buf, send_sems, recv_sems):
        x, y, c, _ = _place()
        me = 4 * x + 2 * y + c
        buf[me] = p_ref[...]
        sends = []
        for k in range(1, N_DEV):
            peer = (x ^ (k >> 2), y ^ ((k >> 1) & 1), c ^ (k & 1))
            sends.append(pltpu.make_async_remote_copy(
                src_ref=p_ref, dst_ref=buf.at[me], send_sem=send_sems.at[k - 1], recv_sem=recv_sems.at[k - 1],
                device_id=peer, device_id_type=MESH))
        for cp in sends:
            cp.start()
        for k in range(1, N_DEV):
            pltpu.make_async_remote_copy(
                src_ref=p_ref, dst_ref=buf.at[me ^ k], send_sem=send_sems.at[k - 1], recv_sem=recv_sems.at[k - 1],
                device_id=(x, y, c), device_id_type=MESH).wait_recv()
        for cp in sends:
            cp.wait_send()
        total = buf[0]
        for d in range(1, N_DEV):
            total = total + buf[d]
        out_ref[...] = total

    vm = pl.BlockSpec(memory_space=pltpu.VMEM)
    return pl.pallas_call(
        body,
        name="all_reduce_small",
        in_specs=[vm],
        out_specs=vm,
        out_shape=jax.ShapeDtypeStruct((rows, lanes), F32),
        scratch_shapes=[pltpu.VMEM((N_DEV, rows, lanes), F32), pltpu.SemaphoreType.DMA((N_DEV - 1,)),
                        pltpu.SemaphoreType.DMA((N_DEV - 1,))],
    )(pack)


def _adamw(g, w, m, v, name):
    rows, cols = g.shape
    t = rows
    for cand in (256, 128, 64, 32, 16, 8):
        if rows % cand == 0:
            t = cand
            break

    def body(g_ref, w_ref, m_ref, v_ref, d_ref, nm_ref, nv_ref):
        gv = g_ref[...]
        mv = ADAM_B1 * m_ref[...] + (1.0 - ADAM_B1) * gv
        vv = ADAM_B2 * v_ref[...] + (1.0 - ADAM_B2) * (gv * gv)
        m_hat = mv / (1.0 - ADAM_B1 ** ADAM_STEP)
        v_hat = vv / (1.0 - ADAM_B2 ** ADAM_STEP)
        d_ref[...] = -ADAM_LR * (m_hat / (jnp.sqrt(v_hat) + ADAM_EPS) + ADAM_WD * w_ref[...])
        nm_ref[...] = mv
        nv_ref[...] = vv

    blk = pl.BlockSpec((t, cols), lambda i: (i, 0))
    return pl.pallas_call(
        body,
        name=name,
        grid=(rows // t,),
        in_specs=[blk] * 4,
        out_specs=[blk] * 3,
        out_shape=[jax.ShapeDtypeStruct((rows, cols), F32)] * 3,
        compiler_params=_cparams(("parallel",)),
    )(g, w, m, v)


PACK_W = 1024
BIG = (("w_in", (D_MODEL, IN_COLS), 1), ("w_up_dil", (DIL_W, D_MODEL), 1), ("w_up_sb", (SB_W, D_MODEL), 1),
       ("w_out", (D_MODEL, D_MODEL), 0), ("w_mlp_in", (D_MODEL, D_FF), 1), ("w_mlp_out", (D_FF, D_MODEL), 0))


def _shard_shape(shape, axis):
    return tuple(d // N_CHIPS if a == axis else d for a, d in enumerate(shape))


MIXER_GROUP, MLP_GROUP = BIG[:4], BIG[4:]
LATE_WEIGHTS = BIG[1:]


def _pack_rows(group=BIG):
    rows, at = {}, 0
    for name, shape, axis in group:
        n = math.prod(_shard_shape(shape, axis)) // PACK_W
        rows[name] = (at, n)
        at += n
    return rows, at


def _pack_shards(shards, group):
    return jnp.concatenate([shards[name].reshape(-1, PACK_W) for name, _, _ in group], axis=0)


def _unpack_full(gathered, group):
    rows, _ = _pack_rows(group)
    full = {}
    for name, shape, axis in group:
        at, n = rows[name]
        parts = gathered[:, at:at + n, :].reshape((N_CHIPS,) + _shard_shape(shape, axis))
        if axis == 0:
            full[name] = parts.reshape(shape)
        else:
            full[name] = jnp.transpose(parts, (1, 0, 2)).reshape(shape)
    return full


def _pack_full_grads(grads, group):
    chunks = []
    for name, shape, axis in group:
        g = grads[name]
        if axis == 0:
            parts = g.reshape((N_CHIPS, shape[0] // N_CHIPS, shape[1]))
        else:
            parts = jnp.transpose(g.reshape((shape[0], N_CHIPS, shape[1] // N_CHIPS)), (1, 0, 2))
        chunks.append(parts.reshape(N_CHIPS, -1, PACK_W))
    return jnp.concatenate(chunks, axis=1)


def _unpack_shard(packed, group):
    rows, _ = _pack_rows(group)
    return {name: packed[rows[name][0]:rows[name][0] + rows[name][1]].reshape(_shard_shape(shape, axis))
            for name, shape, axis in group}


def _local_step(x, target, early_shards, late_shards, norm_mix_g, b_gate, norm_mlp_g, norm_final_g, core):
    h, h_t, early = _rms_fwd_and_gather(x, norm_mix_g, early_shards)
    w = {"w_in": jnp.transpose(early, (1, 0, 2)).reshape(D_MODEL, IN_COLS)}
    w_in = w["w_in"]
    sb0 = 9 * DIL_W
    w_sb, w_gate = w_in[:, sb0:QKV_W], w_in[:, QKV_W:]
    w_dil = [jnp.concatenate([w_in[:, (3 * i + g) * DIL_W:(3 * i + g + 1) * DIL_W] for i in range(3)], axis=1)
             for g in range(3)]

    qkv_dil = [_matmul(h, w_dil[g], mode="nn", out_dtypes=(BF16,), name=f"proj_dil_g{g}", tn=768)[0] for g in range(3)]
    (qkv_sb,) = _matmul(h, w_sb, mode="nn", out_dtypes=(BF16,), name="proj_sb", tn=768)
    (gl,) = _matmul(h, w_gate, mode="nn", out_dtypes=(F32,), name="proj_gate")
    dil = _dil_fwd_all(qkv_dil)
    o_groups, lse_groups = [d[0] for d in dil], [d[1] for d in dil]
    o_sb, a_sb, late_others = _sb_fwd(qkv_sb, late_shards)
    w = {**w, **_unpack_full(_fill_own_slot(late_others, late_shards), LATE_WEIGHTS)}
    merged, o_a = _merge_fwd(o_groups, lse_groups, o_sb, gl, b_gate, w["w_up_dil"], w["w_up_sb"])
    def residual_and_norm(acc, res, g):
        x1 = res + acc
        return x1, _rms_rows(x1)[0] * g

    x1, h2 = _matmul(merged, w["w_out"], mode="nn", out_dtypes=(F32, BF16), name="out_proj", tm=ROW_TILE,
                     extras=(x, norm_mlp_g), epilogue=residual_and_norm)
    u, act = _matmul(h2, w["w_mlp_in"], mode="nn", out_dtypes=(BF16, BF16), name="mlp_in",
                     epilogue=lambda acc: (acc, jnp.square(jnp.maximum(acc, 0.0))))

    def residual_and_loss(acc, res, tgt, g):
        xh, r = _rms_rows(res + acc)
        err = xh * g - tgt
        dy = err * (1.0 / D_MODEL)
        dxh = dy * g
        dx2 = r * (dxh - xh * jnp.mean(dxh * xh, axis=-1, keepdims=True))
        return dx2, _rows_sum8(dy * xh), (0.5 / D_MODEL) * _rows_sum8(err * err)

    dx2, dg_final, loss_part = _matmul(
        act, w["w_mlp_out"], mode="nn", out_dtypes=(F32, ("part", F32), ("part", F32)), name="mlp_out", tm=ROW_TILE,
        tk=2048, extras=(x1, target, norm_final_g.reshape(1, D_MODEL)), epilogue=residual_and_loss)

    (du,) = _matmul(dx2, w["w_mlp_out"], mode="nt", out_dtypes=(BF16,), name="mlp_out_dx",
                    extras=(u,), epilogue=lambda acc, uu: (acc * (2.0 * jnp.maximum(uu.astype(F32), 0.0)),))
    pack_shape = (N_CHIPS, 2 * D_MODEL, D_MODEL)
    (half_pack,) = _matmul(act, dx2, mode="tn", out_dtypes=(F32,), name="mlp_out_dw",
                           into=(pack_shape, lambda i, j: (i, 1, 0), None))
    (mlp_pack,) = _matmul(h2, du, mode="tn", out_dtypes=(F32,), name="mlp_in_dw",
                          into=(pack_shape, lambda i, j: (j, 0, 0), half_pack))

    def norm_bwd(acc, xx, dres, g):
        dx, dg = _rms_bwd_rows(acc, xx, g)
        return dres + dx, dg

    dx1, dg_mlp = _matmul(du, w["w_mlp_in"], mode="nt", out_dtypes=(F32, ("part", F32)), name="mlp_in_dx",
                          tm=ROW_TILE, tk=2048, extras=(x1, dx2, norm_mlp_g), epilogue=norm_bwd)

    (dmerged,) = _matmul(dx1, w["w_out"], mode="nt", out_dtypes=(F32,), name="out_proj_dx")
    (g_out,) = _matmul(merged, dx1, mode="tn", out_dtypes=(F32,), name="out_proj_dw")
    mb = _merge_bwd(dmerged, o_groups, lse_groups, o_sb, gl, b_gate, w["w_up_dil"], w["w_up_sb"], mlp_pack)
    dua, dub, dgl, dbg, do_sb = mb[:5]
    do_groups, c_groups = mb[5:8], mb[8:11]
    mlp_sums = _add_halves(mlp_pack, mb[11], core)
    (g_up_dil,) = _matmul(o_a, dua, mode="tn", out_dtypes=(F32,), name="up_dil_dw")
    (g_up_sb,) = _matmul(o_sb, dub, mode="tn", out_dtypes=(F32,), name="up_sb_dw")
    dq_sb, dk_sb, dv_sb, mlp_got = _sb_bwd(qkv_sb, do_sb, a_sb, mlp_sums)
    dil_b = _dil_bwd_all(qkv_dil, do_groups, lse_groups, c_groups)
    dproj = jnp.concatenate(
        [dil_b[g][i].astype(BF16) for i in range(3) for g in range(3)]
        + [t.astype(BF16) for t in (dq_sb, dk_sb, dv_sb)] + [dgl], axis=1)
    (g_in,) = _matmul(h_t, dproj, mode="nn", out_dtypes=(F32,), name="proj_dw", tm=512, tn=IN_COLS // 2)
    mixer_pack = _pack_full_grads({"w_in": g_in, "w_up_dil": g_up_dil, "w_up_sb": g_up_sb, "w_out": g_out}, MIXER_GROUP)
    mixer_sums = _add_halves(mixer_pack, _swap_halves(mixer_pack), core)
    dh, mixer_got = _matmul(dproj, w["w_in"], mode="nt", out_dtypes=(F32,), name="proj_dx", tk=IN_COLS // 2,
                            exchange=mixer_sums)
    grad_x, dg_mix = _rms_bwd_residual(dh, x, norm_mix_g, dx1)

    small = (dg_mix, dbg, dg_mlp, dg_final, loss_part)
    return grad_x, (mixer_got, mixer_sums), (mlp_got, mlp_sums), small


def kernel(x, norm_mix_g, w_in, b_gate, w_up_dil, w_up_sb, w_out, norm_mlp_g, w_mlp_in, w_mlp_out, norm_final_g, loss_target, m_norm_mix_g, m_w_in, m_b_gate, m_w_up_dil, m_w_up_sb, m_w_out, m_norm_mlp_g, m_w_mlp_in, m_w_mlp_out, m_norm_final_g, v_norm_mix_g, v_w_in, v_b_gate, v_w_up_dil, v_w_up_sb, v_w_out, v_norm_mlp_g, v_w_mlp_in, v_w_mlp_out, v_norm_final_g):
    shards = {"w_in": w_in[0], "w_up_dil": w_up_dil[0], "w_up_sb": w_up_sb[0], "w_out": w_out[0],
              "w_mlp_in": w_mlp_in[0], "w_mlp_out": w_mlp_out[0]}
    moments_m = {"w_in": m_w_in[0], "w_up_dil": m_w_up_dil[0], "w_up_sb": m_w_up_sb[0], "w_out": m_w_out[0],
                 "w_mlp_in": m_w_mlp_in[0], "w_mlp_out": m_w_mlp_out[0]}
    moments_v = {"w_in": v_w_in[0], "w_up_dil": v_w_up_dil[0], "w_up_sb": v_w_up_sb[0], "w_out": v_w_out[0],
                 "w_mlp_in": v_w_mlp_in[0], "w_mlp_out": v_w_mlp_out[0]}

    shards16 = {n: s.astype(BF16) for n, s in shards.items()}
    early_shards = shards16["w_in"]
    late_shards = _pack_shards(shards16, LATE_WEIGHTS)

    core = lax.axis_index("c").astype(jnp.int32).reshape(1)
    chip = (2 * lax.axis_index("x") + lax.axis_index("y")).astype(jnp.int32).reshape(1)
    grad_x, (mixer_got, mixer_sums), (mlp_got, mlp_sums), small = _local_step(
        x[0], loss_target[0], early_shards, late_shards, norm_mix_g, b_gate, norm_mlp_g, norm_final_g, core)

    reduced = _join_halves(_sum_chips(mixer_got, mixer_sums, chip))
    reduced_mlp = _join_halves(_sum_chips(mlp_got, mlp_sums, chip))
    g_shard = {**_unpack_shard(reduced, MIXER_GROUP), **_unpack_shard(reduced_mlp, MLP_GROUP)}

    dg_mix, dbg, dg_mlp, dg_final, loss_part = small
    loss_row = jnp.sum(loss_part, axis=0, keepdims=True)
    small_pack = jnp.concatenate(
        [jnp.sum(dg_mix, axis=0, keepdims=True), jnp.sum(dbg, axis=0, keepdims=True),
         jnp.sum(dg_mlp, axis=0, keepdims=True), jnp.sum(dg_final, axis=0, keepdims=True), loss_row], axis=1)
    n_small = small_pack.shape[1]
    small_sum = _all_reduce_small(small_pack.reshape(n_small // 128, 128)).reshape(1, n_small)
    g_norm_mix = small_sum[:, :D_MODEL]
    g_b_gate = small_sum[:, D_MODEL:3 * D_MODEL]
    g_norm_mlp = small_sum[:, 3 * D_MODEL:4 * D_MODEL]
    g_norm_final = small_sum[:, 4 * D_MODEL:5 * D_MODEL]
    loss = jnp.sum(small_sum[:, 5 * D_MODEL:])

    names = ["norm_mix_g", "w_in", "b_gate", "w_up_dil", "w_up_sb", "w_out", "norm_mlp_g", "w_mlp_in", "w_mlp_out",
             "norm_final_g"]
    grads = dict(g_shard)
    grads.update(norm_mix_g=g_norm_mix, b_gate=g_b_gate, norm_mlp_g=g_norm_mlp, norm_final_g=g_norm_final)
    weights = dict(shards)
    weights.update(norm_mix_g=norm_mix_g, b_gate=b_gate, norm_mlp_g=norm_mlp_g, norm_final_g=norm_final_g.reshape(1, D_MODEL))
    ms = dict(moments_m)
    ms.update(norm_mix_g=m_norm_mix_g, b_gate=m_b_gate, norm_mlp_g=m_norm_mlp_g, norm_final_g=m_norm_final_g.reshape(1, D_MODEL))
    vs = dict(moments_v)
    vs.update(norm_mix_g=v_norm_mix_g, b_gate=v_b_gate, norm_mlp_g=v_norm_mlp_g, norm_final_g=v_norm_final_g.reshape(1, D_MODEL))

    out_shapes = {"norm_mix_g": norm_mix_g.shape, "w_in": w_in.shape, "b_gate": b_gate.shape, "w_up_dil": w_up_dil.shape,
                  "w_up_sb": w_up_sb.shape, "w_out": w_out.shape, "norm_mlp_g": norm_mlp_g.shape,
                  "w_mlp_in": w_mlp_in.shape, "w_mlp_out": w_mlp_out.shape, "norm_final_g": norm_final_g.shape}
    g_out, d_out, m_out, v_out = [], [], [], []
    for n in names:
        d, nm, nv = _adamw(grads[n], weights[n], ms[n], vs[n], "adamw_" + n)
        shape = out_shapes[n]
        g_out.append(grads[n].reshape(shape))
        d_out.append(d.reshape(shape))
        m_out.append(nm.reshape(shape))
        v_out.append(nv.reshape(shape))
    return (loss, grad_x.reshape(x.shape), *g_out, *d_out, *m_out, *v_out)
```
